```python
import jax, jax.numpy as jnp
from jax import lax
import numpy as np

D_MODEL = 1024
BATCH = 8
SEQ = 8192
DEPTH = 1

D_MIX = D_MODEL
D_A = D_MIX // 2
D_B = D_MIX - D_A
HEAD_DIM = 64
N_GROUPS_A = D_A // HEAD_DIM
N_GROUPS_B = D_B // HEAD_DIM
K_SHORT = 3
K_CONFORMER = 31
K_FFN = 3
D_FF = 2816
D_IN = 3 * D_A + 2 * D_B
RMS_EPS = 1e-6
LN_EPS = 1e-5

kernel_name = "hybrid_shortconv_conformer_convffn_block"


def rmsnorm(x, g):
    xf = x.astype(jnp.float32)
    y = xf * lax.rsqrt(jnp.mean(xf * xf, axis=-1, keepdims=True) + RMS_EPS)
    return (y * g.astype(jnp.float32)).astype(x.dtype)


def layernorm(x, g, b):
    xf = x.astype(jnp.float32)
    mu = jnp.mean(xf, axis=-1, keepdims=True)
    var = jnp.mean(jnp.square(xf - mu), axis=-1, keepdims=True)
    y = (xf - mu) * lax.rsqrt(var + LN_EPS)
    return (y * g.astype(jnp.float32) + b.astype(jnp.float32)).astype(x.dtype)


def dwconv(x, w):
    k, c = w.shape
    rhs = w[:, None, :].astype(x.dtype)
    return lax.conv_general_dilated(
        x, rhs, window_strides=(1,), padding=[(k // 2, k // 2)],
        dimension_numbers=("NWC", "WIO", "NWC"), feature_group_count=c)


def _fwd_setup_inputs(seed: int = 0) -> dict:
    key = jax.random.key(seed)
    ks = jax.random.split(key, 16)
    f32 = jnp.float32

    def nrm(k, shape, scale):
        return jax.random.normal(k, shape, f32) * scale

    return {
        "x": jax.random.normal(ks[0], (BATCH, SEQ, D_MODEL), f32),
        "norm_mix_g": 1.0 + nrm(ks[1], (DEPTH, D_MODEL), 0.05),
        "w_in": nrm(ks[2], (DEPTH, D_MODEL, D_IN), D_MODEL ** -0.5),
        "conv_a_w": nrm(ks[3], (DEPTH, K_SHORT, D_A), K_SHORT ** -0.5),
        "conv_b_w": nrm(ks[4], (DEPTH, K_CONFORMER, D_B), K_CONFORMER ** -0.5),
        "conv_b_b": nrm(ks[5], (DEPTH, D_B), 0.02),
        "ln_b_g": 1.0 + nrm(ks[6], (DEPTH, D_B), 0.05),
        "ln_b_b": nrm(ks[7], (DEPTH, D_B), 0.02),
        "w_out": nrm(ks[8], (DEPTH, D_MIX, D_MODEL), D_MIX ** -0.5),
        "norm_ffn_g": 1.0 + nrm(ks[9], (DEPTH, D_MODEL), 0.05),
        "w_gate": nrm(ks[10], (DEPTH, D_MODEL, D_FF), D_MODEL ** -0.5),
        "w_up": nrm(ks[11], (DEPTH, D_MODEL, D_FF), D_MODEL ** -0.5),
        "conv_ffn_w": nrm(ks[12], (DEPTH, K_FFN, D_FF), K_FFN ** -0.5),
        "w_down": nrm(ks[13], (DEPTH, D_FF, D_MODEL), D_FF ** -0.5),
        "norm_final_g": 1.0 + nrm(ks[14], (D_MODEL,), 0.05),
    }


def _fwd_reference(x, norm_mix_g, w_in, conv_a_w, conv_b_w, conv_b_b, ln_b_g, ln_b_b,
              w_out, norm_ffn_g, w_gate, w_up, conv_ffn_w, w_down, norm_final_g):
    for l in range(DEPTH):
        h = rmsnorm(x, norm_mix_g[l])
        z = jnp.einsum("bsd,de->bse", h, w_in[l])
        a_h, a_bg, a_cg, b_val, b_gate = jnp.split(
            z, [D_A, 2 * D_A, 3 * D_A, 3 * D_A + D_B], axis=-1)
        y_a = a_bg * dwconv(a_cg * a_h, conv_a_w[l])
        u = b_val * jax.nn.sigmoid(b_gate)
        u = dwconv(u, conv_b_w[l]) + conv_b_b[l].astype(u.dtype)
        y_b = jax.nn.silu(layernorm(u, ln_b_g[l], ln_b_b[l]))
        y = jnp.concatenate([y_a, y_b], axis=-1)
        x = x + jnp.einsum("bse,ed->bsd", y, w_out[l])

        h = rmsnorm(x, norm_ffn_g[l])
        g = dwconv(jnp.einsum("bsd,df->bsf", h, w_gate[l]), conv_ffn_w[l])
        v = jnp.einsum("bsd,df->bsf", h, w_up[l])
        x = x + jnp.einsum("bsf,fd->bsd", jax.nn.silu(g) * v, w_down[l])

    return rmsnorm(x, norm_final_g)


import jax as _jax
import jax.numpy as _jnp

TWIN_FORMAT = 'train_step'
FWD_PARAMS = ['x', 'norm_mix_g', 'w_in', 'conv_a_w', 'conv_b_w', 'conv_b_b', 'ln_b_g', 'ln_b_b', 'w_out', 'norm_ffn_g', 'w_gate', 'w_up', 'conv_ffn_w', 'w_down', 'norm_final_g']
TWIN_WEIGHTS = ['norm_mix_g', 'w_in', 'conv_a_w', 'conv_b_w', 'conv_b_b', 'ln_b_g', 'ln_b_b', 'w_out', 'norm_ffn_g', 'w_gate', 'w_up', 'conv_ffn_w', 'w_down', 'norm_final_g']
TWIN_DIFF_INPUT = 'x'
TWIN_INPUTS = ['x', 'norm_mix_g', 'w_in', 'conv_a_w', 'conv_b_w', 'conv_b_b', 'ln_b_g', 'ln_b_b', 'w_out', 'norm_ffn_g', 'w_gate', 'w_up', 'conv_ffn_w', 'w_down', 'norm_final_g', 'loss_target', 'm_norm_mix_g', 'm_w_in', 'm_conv_a_w', 'm_conv_b_w', 'm_conv_b_b', 'm_ln_b_g', 'm_ln_b_b', 'm_w_out', 'm_norm_ffn_g', 'm_w_gate', 'm_w_up', 'm_conv_ffn_w', 'm_w_down', 'm_norm_final_g', 'v_norm_mix_g', 'v_w_in', 'v_conv_a_w', 'v_conv_b_w', 'v_conv_b_b', 'v_ln_b_g', 'v_ln_b_b', 'v_w_out', 'v_norm_ffn_g', 'v_w_gate', 'v_w_up', 'v_conv_ffn_w', 'v_w_down', 'v_norm_final_g']
TWIN_OUTPUTS = ['loss', 'grad_x', 'grad_norm_mix_g', 'grad_w_in', 'grad_conv_a_w', 'grad_conv_b_w', 'grad_conv_b_b', 'grad_ln_b_g', 'grad_ln_b_b', 'grad_w_out', 'grad_norm_ffn_g', 'grad_w_gate', 'grad_w_up', 'grad_conv_ffn_w', 'grad_w_down', 'grad_norm_final_g', 'delta_norm_mix_g', 'delta_w_in', 'delta_conv_a_w', 'delta_conv_b_w', 'delta_conv_b_b', 'delta_ln_b_g', 'delta_ln_b_b', 'delta_w_out', 'delta_norm_ffn_g', 'delta_w_gate', 'delta_w_up', 'delta_conv_ffn_w', 'delta_w_down', 'delta_norm_final_g', 'new_m_norm_mix_g', 'new_m_w_in', 'new_m_conv_a_w', 'new_m_conv_b_w', 'new_m_conv_b_b', 'new_m_ln_b_g', 'new_m_ln_b_b', 'new_m_w_out', 'new_m_norm_ffn_g', 'new_m_w_gate', 'new_m_w_up', 'new_m_conv_ffn_w', 'new_m_w_down', 'new_m_norm_final_g', 'new_v_norm_mix_g', 'new_v_w_in', 'new_v_conv_a_w', 'new_v_conv_b_w', 'new_v_conv_b_b', 'new_v_ln_b_g', 'new_v_ln_b_b', 'new_v_w_out', 'new_v_norm_ffn_g', 'new_v_w_gate', 'new_v_w_up', 'new_v_conv_ffn_w', 'new_v_w_down', 'new_v_norm_final_g']
TWIN_LEAF_KINDS = {'loss': 'loss', 'grad_x': 'grad_x', 'grad_norm_mix_g': 'grad_w', 'grad_w_in': 'grad_w', 'grad_conv_a_w': 'grad_w', 'grad_conv_b_w': 'grad_w', 'grad_conv_b_b': 'grad_w', 'grad_ln_b_g': 'grad_w', 'grad_ln_b_b': 'grad_w', 'grad_w_out': 'grad_w', 'grad_norm_ffn_g': 'grad_w', 'grad_w_gate': 'grad_w', 'grad_w_up': 'grad_w', 'grad_conv_ffn_w': 'grad_w', 'grad_w_down': 'grad_w', 'grad_norm_final_g': 'grad_w', 'delta_norm_mix_g': 'delta_w', 'delta_w_in': 'delta_w', 'delta_conv_a_w': 'delta_w', 'delta_conv_b_w': 'delta_w', 'delta_conv_b_b': 'delta_w', 'delta_ln_b_g': 'delta_w', 'delta_ln_b_b': 'delta_w', 'delta_w_out': 'delta_w', 'delta_norm_ffn_g': 'delta_w', 'delta_w_gate': 'delta_w', 'delta_w_up': 'delta_w', 'delta_conv_ffn_w': 'delta_w', 'delta_w_down': 'delta_w', 'delta_norm_final_g': 'delta_w', 'new_m_norm_mix_g': 'new_m', 'new_m_w_in': 'new_m', 'new_m_conv_a_w': 'new_m', 'new_m_conv_b_w': 'new_m', 'new_m_conv_b_b': 'new_m', 'new_m_ln_b_g': 'new_m', 'new_m_ln_b_b': 'new_m', 'new_m_w_out': 'new_m', 'new_m_norm_ffn_g': 'new_m', 'new_m_w_gate': 'new_m', 'new_m_w_up': 'new_m', 'new_m_conv_ffn_w': 'new_m', 'new_m_w_down': 'new_m', 'new_m_norm_final_g': 'new_m', 'new_v_norm_mix_g': 'new_v', 'new_v_w_in': 'new_v', 'new_v_conv_a_w': 'new_v', 'new_v_conv_b_w': 'new_v', 'new_v_conv_b_b': 'new_v', 'new_v_ln_b_g': 'new_v', 'new_v_ln_b_b': 'new_v', 'new_v_w_out': 'new_v', 'new_v_norm_ffn_g': 'new_v', 'new_v_w_gate': 'new_v', 'new_v_w_up': 'new_v', 'new_v_conv_ffn_w': 'new_v', 'new_v_w_down': 'new_v', 'new_v_norm_final_g': 'new_v'}


def _forward(args):
    return _fwd_reference(*[args[k] for k in FWD_PARAMS])


def _output_shape():
    out = _jax.eval_shape(lambda: _forward(_fwd_setup_inputs(0)))
    return out.shape, out.dtype

N_MICROBATCH = 1
ADAM_LR = 0.001
ADAM_B1 = 0.9
ADAM_B2 = 0.999
ADAM_EPS = 1e-08
ADAM_WD = 0.01
ADAM_STEP = 10
PER_EXAMPLE_BATCH_AXIS = {'x': 0, 'loss_target': 0}
SHARED_INPUTS = []
_WEIGHT_DTYPES = {'norm_mix_g': _jnp.float32, 'w_in': _jnp.float32, 'conv_a_w': _jnp.float32, 'conv_b_w': _jnp.float32, 'conv_b_b': _jnp.float32, 'ln_b_g': _jnp.float32, 'ln_b_b': _jnp.float32, 'w_out': _jnp.float32, 'norm_ffn_g': _jnp.float32, 'w_gate': _jnp.float32, 'w_up': _jnp.float32, 'conv_ffn_w': _jnp.float32, 'w_down': _jnp.float32, 'norm_final_g': _jnp.float32}
MOMENT_SCALE = {'norm_mix_g': 2.736012e-01, 'w_in': 1.735331e-01, 'conv_a_w': 2.253341e-01, 'conv_b_w': 1.343632e-01, 'conv_b_b': 2.609047e-01, 'ln_b_g': 1.656715e-01, 'ln_b_b': 1.423123e-01, 'w_out': 1.772858e-01, 'norm_ffn_g': 1.648945e-01, 'w_gate': 6.773166e-02, 'w_up': 6.605067e-02, 'conv_ffn_w': 6.823115e-02, 'w_down': 1.100510e-01, 'norm_final_g': 6.405329e+01}


def _to_microbatches(a, axis):
    t = _jnp.moveaxis(a, axis, 0)
    t = t.reshape((N_MICROBATCH, t.shape[0] // N_MICROBATCH) + t.shape[1:])
    return _jnp.moveaxis(t, 1, axis + 1)


def setup_inputs(seed: int = 0) -> dict:
    inp = _fwd_setup_inputs(seed)
    key = _jax.random.fold_in(_jax.random.key(seed), 7919)
    shape, _ = _output_shape()
    out = dict(inp)
    out["loss_target"] = _jax.random.normal(_jax.random.fold_in(key, 0), shape, _jnp.float32)
    for i, name in enumerate(TWIN_WEIGHTS):
        w = inp[name].astype(_jnp.float32)
        if MOMENT_SCALE is None:
            s = _jnp.sqrt(_jnp.mean(_jnp.square(w)) + 1e-30)
        else:
            s = MOMENT_SCALE[name]
        km, kv = _jax.random.split(_jax.random.fold_in(key, i + 1))
        out[name] = w
        out["m_" + name] = s * _jax.random.normal(km, w.shape, _jnp.float32)
        out["v_" + name] = (s * s) * _jax.random.uniform(kv, w.shape, _jnp.float32, 0.5, 1.5)
    if N_MICROBATCH > 1:
        for name, axis in PER_EXAMPLE_BATCH_AXIS.items():
            out[name] = _to_microbatches(out[name], axis)
    return {'x': out['x'], 'norm_mix_g': out['norm_mix_g'], 'w_in': out['w_in'], 'conv_a_w': out['conv_a_w'], 'conv_b_w': out['conv_b_w'], 'conv_b_b': out['conv_b_b'], 'ln_b_g': out['ln_b_g'], 'ln_b_b': out['ln_b_b'], 'w_out': out['w_out'], 'norm_ffn_g': out['norm_ffn_g'], 'w_gate': out['w_gate'], 'w_up': out['w_up'], 'conv_ffn_w': out['conv_ffn_w'], 'w_down': out['w_down'], 'norm_final_g': out['norm_final_g'], 'loss_target': out['loss_target'], 'm_norm_mix_g': out['m_norm_mix_g'], 'm_w_in': out['m_w_in'], 'm_conv_a_w': out['m_conv_a_w'], 'm_conv_b_w': out['m_conv_b_w'], 'm_conv_b_b': out['m_conv_b_b'], 'm_ln_b_g': out['m_ln_b_g'], 'm_ln_b_b': out['m_ln_b_b'], 'm_w_out': out['m_w_out'], 'm_norm_ffn_g': out['m_norm_ffn_g'], 'm_w_gate': out['m_w_gate'], 'm_w_up': out['m_w_up'], 'm_conv_ffn_w': out['m_conv_ffn_w'], 'm_w_down': out['m_w_down'], 'm_norm_final_g': out['m_norm_final_g'], 'v_norm_mix_g': out['v_norm_mix_g'], 'v_w_in': out['v_w_in'], 'v_conv_a_w': out['v_conv_a_w'], 'v_conv_b_w': out['v_conv_b_w'], 'v_conv_b_b': out['v_conv_b_b'], 'v_ln_b_g': out['v_ln_b_g'], 'v_ln_b_b': out['v_ln_b_b'], 'v_w_out': out['v_w_out'], 'v_norm_ffn_g': out['v_norm_ffn_g'], 'v_w_gate': out['v_w_gate'], 'v_w_up': out['v_w_up'], 'v_conv_ffn_w': out['v_conv_ffn_w'], 'v_w_down': out['v_w_down'], 'v_norm_final_g': out['v_norm_final_g']}


def _loss(weights, diff, rest, loss_target):
    with _jax.named_scope("forward"):
        args = {**rest, TWIN_DIFF_INPUT: diff, **{k: w.astype(_WEIGHT_DTYPES[k]) for k, w in weights.items()}}
        y = _forward(args)
    with _jax.named_scope("loss_head"):
        err = _jnp.square(y.astype(_jnp.float32) - loss_target)
        return 0.5 * _jnp.sum(_jnp.mean(err, axis=-1)) if err.ndim else 0.5 * err


def _adamw(w, g, m, v):
    m = ADAM_B1 * m + (1.0 - ADAM_B1) * g
    v = ADAM_B2 * v + (1.0 - ADAM_B2) * _jnp.square(g)
    m_hat = m / (1.0 - ADAM_B1 ** ADAM_STEP)
    v_hat = v / (1.0 - ADAM_B2 ** ADAM_STEP)
    delta = -ADAM_LR * (m_hat / (_jnp.sqrt(v_hat) + ADAM_EPS) + ADAM_WD * w)
    return delta, m, v


def reference(x, norm_mix_g, w_in, conv_a_w, conv_b_w, conv_b_b, ln_b_g, ln_b_b, w_out, norm_ffn_g, w_gate, w_up, conv_ffn_w, w_down, norm_final_g, loss_target, m_norm_mix_g, m_w_in, m_conv_a_w, m_conv_b_w, m_conv_b_b, m_ln_b_g, m_ln_b_b, m_w_out, m_norm_ffn_g, m_w_gate, m_w_up, m_conv_ffn_w, m_w_down, m_norm_final_g, v_norm_mix_g, v_w_in, v_conv_a_w, v_conv_b_w, v_conv_b_b, v_ln_b_g, v_ln_b_b, v_w_out, v_norm_ffn_g, v_w_gate, v_w_up, v_conv_ffn_w, v_w_down, v_norm_final_g):
    given = dict(x=x, norm_mix_g=norm_mix_g, w_in=w_in, conv_a_w=conv_a_w, conv_b_w=conv_b_w, conv_b_b=conv_b_b, ln_b_g=ln_b_g, ln_b_b=ln_b_b, w_out=w_out, norm_ffn_g=norm_ffn_g, w_gate=w_gate, w_up=w_up, conv_ffn_w=conv_ffn_w, w_down=w_down, norm_final_g=norm_final_g, loss_target=loss_target, m_norm_mix_g=m_norm_mix_g, m_w_in=m_w_in, m_conv_a_w=m_conv_a_w, m_conv_b_w=m_conv_b_w, m_conv_b_b=m_conv_b_b, m_ln_b_g=m_ln_b_g, m_ln_b_b=m_ln_b_b, m_w_out=m_w_out, m_norm_ffn_g=m_norm_ffn_g, m_w_gate=m_w_gate, m_w_up=m_w_up, m_conv_ffn_w=m_conv_ffn_w, m_w_down=m_w_down, m_norm_final_g=m_norm_final_g, v_norm_mix_g=v_norm_mix_g, v_w_in=v_w_in, v_conv_a_w=v_conv_a_w, v_conv_b_w=v_conv_b_w, v_conv_b_b=v_conv_b_b, v_ln_b_g=v_ln_b_g, v_ln_b_b=v_ln_b_b, v_w_out=v_w_out, v_norm_ffn_g=v_norm_ffn_g, v_w_gate=v_w_gate, v_w_up=v_w_up, v_conv_ffn_w=v_conv_ffn_w, v_w_down=v_w_down, v_norm_final_g=v_norm_final_g)
    weights = {n: given[n] for n in TWIN_WEIGHTS}
    shared = {n: given[n] for n in SHARED_INPUTS}
    per_example = {n: given[n] for n in ['x']}
    grad_fn = _jax.value_and_grad(_loss, argnums=(0, 1))

    def one_microbatch(ex, loss_target):
        ex = dict(ex)
        diff = ex.pop(TWIN_DIFF_INPUT)
        return grad_fn(weights, diff, {**shared, **ex}, loss_target)

    if N_MICROBATCH == 1:
        loss, (grad_w, grad_x) = one_microbatch(per_example, given["loss_target"])
    else:
        def body(carry, xs):
            loss_sum, grad_sum = carry
            l_k, (gw_k, gx_k) = one_microbatch(xs[0], xs[1])
            with _jax.named_scope("update"):
                return (loss_sum + l_k, _jax.tree.map(_jnp.add, grad_sum, gw_k)), gx_k

        init = (_jnp.zeros((), _jnp.float32), _jax.tree.map(_jnp.zeros_like, weights))
        (loss, grad_w), grad_x = _jax.lax.scan(body, init, (per_example, given["loss_target"]))
    with _jax.named_scope("update"):
        delta_w, new_m, new_v = {}, {}, {}
        for n in TWIN_WEIGHTS:
            delta_w[n], new_m[n], new_v[n] = _adamw(weights[n], grad_w[n], given["m_" + n], given["v_" + n])
    return (loss, grad_x, *[grad_w[n] for n in TWIN_WEIGHTS], *[delta_w[n] for n in TWIN_WEIGHTS],
            *[new_m[n] for n in TWIN_WEIGHTS], *[new_v[n] for n in TWIN_WEIGHTS])
```

```python
import functools

import jax
import jax.numpy as jnp
from jax import lax
from jax.experimental import pallas as pl
from jax.experimental.pallas import tpu as pltpu

F32 = jnp.float32
BF16 = jnp.bfloat16

D_MODEL = 1024
D_A = 512
D_B = 512
D_IN = 3 * D_A + 2 * D_B
D_FF = 2816
K_A = 3
K_B = 31
K_F = 3
RMS_EPS = 1e-6
LN_EPS = 1e-5

ADAM_LR = 0.001
ADAM_B1 = 0.9
ADAM_B2 = 0.999
ADAM_EPS = 1e-08
ADAM_WD = 0.01
ADAM_STEP = 10

N_DEV = 8
N_CHIP = 4
LANES = 128
SUBLANES = 8
HALO = 16
ROW_CHUNK = 64
SEQ_TILE = 512
VMEM_LIMIT = 56 * 1024 * 1024

MESH = pl.DeviceIdType.MESH

W_OUT_ROWS = D_MODEL // N_DEV
W_DOWN_ROWS = D_FF // N_DEV
W_IN_COLS = D_IN // N_DEV
W_FF_COLS = D_FF // N_DEV
PACK_ROWS = D_MODEL + W_OUT_ROWS + W_DOWN_ROWS
PACK_COLS = W_IN_COLS + 2 * W_FF_COLS
CONV_A_COLS = D_A // N_DEV
SMALL_W_ROWS = 40
SMALL_W_COLS = 384
SMALL_G_ROWS = 64
SMALL_G_COLS = 512
FF_PAD = 3072


def _rows(ts, c):
    return pl.BlockSpec((ts, c), lambda i: (i, 0))


def _const(shape):
    return pl.BlockSpec(shape, lambda i: (0,) * len(shape), pipeline_mode=pl.Buffered(1))


def _acc_out(shape):
    return pl.BlockSpec(shape, lambda i: (0,) * len(shape))


def _prev(ts, c):
    return pl.BlockSpec((HALO, c), lambda i: (jnp.maximum(i * (ts // HALO) - 1, 0), 0))


def _next(ts, c, s):
    last = s // HALO - 1
    return pl.BlockSpec((HALO, c), lambda i: (jnp.minimum((i + 1) * (ts // HALO), last), 0))


def _params():
    return pltpu.CompilerParams(dimension_semantics=("arbitrary",), vmem_limit_bytes=VMEM_LIMIT)


def _sds(shape, dtype):
    return jax.ShapeDtypeStruct(shape, dtype)


def _sigmoid(v):
    return 1.0 / (1.0 + jnp.exp(-v))


def _conv_block(ext_ref, w_ref, r0, rc, l0, k_taps, transposed):
    acc = None
    for k in range(k_taps):
        d = (k_taps // 2 - k) if transposed else (k - k_taps // 2)
        term = ext_ref[l0 // LANES, pl.ds(r0 + HALO + d, rc), :] * w_ref[k:k + 1, l0:l0 + LANES]
        acc = term if acc is None else acc + term
    return acc


def _conv_wgrad_block(acc_ref, dout, ext_ref, r0, rc, l0, k_taps):
    for k in range(k_taps):
        prod = dout * ext_ref[l0 // LANES, pl.ds(r0 + HALO + k - k_taps // 2, rc), :]
        part = prod.reshape(rc // SUBLANES, SUBLANES, LANES).sum(axis=0)
        acc_ref[k, :, l0:l0 + LANES] = acc_ref[k, :, l0:l0 + LANES] + part


def _reduce_acc(out_ref, acc_ref, k_taps):
    for k in range(k_taps):
        out_ref[k:k + 1, :] = jnp.sum(acc_ref[k], axis=0, keepdims=True)


def _fold8(v):
    rc, c = v.shape
    return v.reshape(rc // SUBLANES, SUBLANES, c).sum(axis=0)


def _ext_scratch(ts, c):
    return pltpu.VMEM((c // LANES, ts + 2 * HALO, LANES), F32)


def _put_rows(ext_ref, r0, rc, val):
    for q in range(val.shape[1] // LANES):
        ext_ref[q, pl.ds(r0 + HALO, rc), :] = val[:, q * LANES:(q + 1) * LANES]


def _fill_halo(ext_ref, vals_prev, vals_next, ts, first, last):
    for q in range(vals_prev.shape[1] // LANES):
        cols = slice(q * LANES, (q + 1) * LANES)
        ext_ref[q, 0:HALO, :] = jnp.where(first, 0.0, vals_prev[:, cols])
        ext_ref[q, HALO + ts:HALO + ts + HALO, :] = jnp.where(last, 0.0, vals_next[:, cols])


def _rms_bwd_rows(dh, xf, g):
    r = lax.rsqrt(jnp.mean(xf * xf, axis=-1, keepdims=True) + RMS_EPS)
    xhat = xf * r
    dxh = dh * g
    dx = r * (dxh - xhat * jnp.mean(dxh * xhat, axis=-1, keepdims=True))
    return dx, dh * xhat


def _fwd_in(x, g1, w_in, ts):
    s = x.shape[0]

    def body(x_ref, g_ref, w_ref, z_ref, h_ref):
        xf = x_ref[...]
        r = lax.rsqrt(jnp.mean(xf * xf, axis=-1, keepdims=True) + RMS_EPS)
        h = (xf * r * g_ref[...]).astype(BF16)
        h_ref[...] = h
        for n0 in range(0, D_IN, 512):
            z_ref[:, n0:n0 + 512] = jnp.dot(h, w_ref[:, n0:n0 + 512], preferred_element_type=F32).astype(BF16)

    return pl.pallas_call(
        body, name="fwd_in", grid=(s // ts,),
        in_specs=[_rows(ts, D_MODEL), _const((1, D_MODEL)), _const((D_MODEL, D_IN))],
        out_specs=[_rows(ts, D_IN), _rows(ts, D_MODEL)],
        out_shape=[_sds((s, D_IN), BF16), _sds((s, D_MODEL), BF16)],
        compiler_params=_params(),
    )(x, g1, w_in)


def _p_u0(z_ref, rows):
    a_h = z_ref[rows, 0:D_A].astype(F32)
    a_c = z_ref[rows, 2 * D_A:3 * D_A].astype(F32)
    b_v = z_ref[rows, 3 * D_A:3 * D_A + D_B].astype(F32)
    b_g = z_ref[rows, 3 * D_A + D_B:D_IN].astype(F32)
    return a_c * a_h, b_v * _sigmoid(b_g)


def _layernorm_rows(u_blocks):
    tot = None
    for ub in u_blocks:
        sm = jnp.sum(ub, axis=-1, keepdims=True)
        tot = sm if tot is None else tot + sm
    mu = tot * (1.0 / D_B)
    var = None
    for ub in u_blocks:
        sq = jnp.sum((ub - mu) * (ub - mu), axis=-1, keepdims=True)
        var = sq if var is None else var + sq
    rstd = lax.rsqrt(var * (1.0 / D_B) + LN_EPS)
    return mu, rstd


def _fwd_mix(z, x, wa, wb, bb, lg, lb, w_out, ts):
    s = x.shape[0]
    nt = s // ts
    rc = min(ROW_CHUNK, ts)

    def body(z_ref, zp_ref, zn_ref, x_ref, wa_ref, wb_ref, bb_ref, lg_ref, lb_ref, wo_ref,
             x2_ref, y_ref, u_ref, pe_ref, ue_ref):
        i = pl.program_id(0)
        pp, up = _p_u0(zp_ref, slice(None))
        pn, un = _p_u0(zn_ref, slice(None))
        _fill_halo(pe_ref, pp, pn, ts, i == 0, i == nt - 1)
        _fill_halo(ue_ref, up, un, ts, i == 0, i == nt - 1)

        def fill(j, carry):
            r0 = pl.multiple_of(j * rc, rc)
            p, u0 = _p_u0(z_ref, pl.ds(r0, rc))
            _put_rows(pe_ref, r0, rc, p)
            _put_rows(ue_ref, r0, rc, u0)
            return carry

        lax.fori_loop(0, ts // rc, fill, 0)

        def main(j, carry):
            r0 = pl.multiple_of(j * rc, rc)
            rows = pl.ds(r0, rc)
            for l0 in range(0, D_A, LANES):
                ca = _conv_block(pe_ref, wa_ref, r0, rc, l0, K_A, False)
                a_b = z_ref[rows, D_A + l0:D_A + l0 + LANES].astype(F32)
                y_ref[rows, l0:l0 + LANES] = (a_b * ca).astype(BF16)
            ubs = []
            for l0 in range(0, D_B, LANES):
                ub = _conv_block(ue_ref, wb_ref, r0, rc, l0, K_B, False) + bb_ref[:, l0:l0 + LANES]
                u_ref[rows, l0:l0 + LANES] = ub
                ubs.append(ub)
            mu, rstd = _layernorm_rows(ubs)
            for q, l0 in enumerate(range(0, D_B, LANES)):
                t = (ubs[q] - mu) * rstd * lg_ref[:, l0:l0 + LANES] + lb_ref[:, l0:l0 + LANES]
                y_ref[rows, D_A + l0:D_A + l0 + LANES] = (t * _sigmoid(t)).astype(BF16)
            return carry

        lax.fori_loop(0, ts // rc, main, 0)
        for n0 in range(0, D_MODEL, 512):
            x2_ref[:, n0:n0 + 512] = x_ref[:, n0:n0 + 512] + jnp.dot(
                y_ref[...], wo_ref[:, n0:n0 + 512], preferred_element_type=F32)

    return pl.pallas_call(
        body, name="fwd_mix", grid=(nt,),
        in_specs=[_rows(ts, D_IN), _prev(ts, D_IN), _next(ts, D_IN, s), _rows(ts, D_MODEL),
                  _const((K_A, D_A)), _const((K_B, D_B)), _const((1, D_B)), _const((1, D_B)), _const((1, D_B)),
                  _const((D_MODEL, D_MODEL))],
        out_specs=[_rows(ts, D_MODEL), _rows(ts, D_MODEL), _rows(ts, D_B)],
        out_shape=[_sds((s, D_MODEL), F32), _sds((s, D_MODEL), BF16), _sds((s, D_B), F32)],
        scratch_shapes=[_ext_scratch(ts, D_A), _ext_scratch(ts, D_B)],
        compiler_params=_params(),
    )(z, z, z, x, wa, wb, bb, lg, lb, w_out)


def _fwd_ffn_in(x2, g2, w_gate, w_up, ts):
    s = x2.shape[0]
    half = D_FF // 2

    def body(x_ref, g_ref, wg_ref, wu_ref, g0_ref, v_ref, h_ref):
        xf = x_ref[...]
        r = lax.rsqrt(jnp.mean(xf * xf, axis=-1, keepdims=True) + RMS_EPS)
        h = (xf * r * g_ref[...]).astype(BF16)
        h_ref[...] = h
        for n0 in range(0, D_FF, half):
            g0_ref[:, n0:n0 + half] = jnp.dot(h, wg_ref[:, n0:n0 + half], preferred_element_type=F32).astype(BF16)
            v_ref[:, n0:n0 + half] = jnp.dot(h, wu_ref[:, n0:n0 + half], preferred_element_type=F32).astype(BF16)

    return pl.pallas_call(
        body, name="fwd_ffn_in", grid=(s // ts,),
        in_specs=[_rows(ts, D_MODEL), _const((1, D_MODEL)), _const((D_MODEL, D_FF)), _const((D_MODEL, D_FF))],
        out_specs=[_rows(ts, D_FF), _rows(ts, D_FF), _rows(ts, D_MODEL)],
        out_shape=[_sds((s, D_FF), BF16), _sds((s, D_FF), BF16), _sds((s, D_MODEL), BF16)],
        compiler_params=_params(),
    )(x2, g2, w_gate, w_up)


def _fwd_ffn_out(g0, v, x2, wf, w_down, g3, target, ts):
    s = x2.shape[0]
    nt = s // ts
    rc = min(ROW_CHUNK, ts)

    def body(g0_ref, gp_ref, gn_ref, v_ref, x2_ref, wf_ref, wd_ref, g3_ref, t_ref,
             a_ref, dx3_ref, dx3b_ref, loss_ref, dg3_ref, ge_ref):
        i = pl.program_id(0)
        _fill_halo(ge_ref, gp_ref[...].astype(F32), gn_ref[...].astype(F32), ts, i == 0, i == nt - 1)

        def fill(j, carry):
            r0 = pl.multiple_of(j * rc, rc)
            _put_rows(ge_ref, r0, rc, g0_ref[pl.ds(r0, rc), :].astype(F32))
            return carry

        lax.fori_loop(0, ts // rc, fill, 0)

        def act(j, carry):
            r0 = pl.multiple_of(j * rc, rc)
            rows = pl.ds(r0, rc)
            for l0 in range(0, D_FF, LANES):
                g = _conv_block(ge_ref, wf_ref, r0, rc, l0, K_F, False)
                vv = v_ref[rows, l0:l0 + LANES].astype(F32)
                a_ref[rows, l0:l0 + LANES] = (g * _sigmoid(g) * vv).astype(BF16)
            return carry

        lax.fori_loop(0, ts // rc, act, 0)
        for n0 in range(0, D_MODEL, 512):
            dx3_ref[:, n0:n0 + 512] = x2_ref[:, n0:n0 + 512] + jnp.dot(
                a_ref[...], wd_ref[:, n0:n0 + 512], preferred_element_type=F32)

        @pl.when(i == 0)
        def _():
            loss_ref[...] = jnp.zeros_like(loss_ref)
            dg3_ref[...] = jnp.zeros_like(dg3_ref)

        def tail(j, carry):
            lsum, dgsum = carry
            r0 = pl.multiple_of(j * rc, rc)
            rows = pl.ds(r0, rc)
            x3 = dx3_ref[rows, :]
            r = lax.rsqrt(jnp.mean(x3 * x3, axis=-1, keepdims=True) + RMS_EPS)
            xhat = x3 * r
            diff = xhat * g3_ref[...] - t_ref[rows, :]
            dout = diff * (1.0 / D_MODEL)
            dxh = dout * g3_ref[...]
            dx3 = r * (dxh - xhat * jnp.mean(dxh * xhat, axis=-1, keepdims=True))
            dx3_ref[rows, :] = dx3
            dx3b_ref[rows, :] = dx3.astype(BF16)
            lsum = lsum + _fold8(diff * diff)
            dgsum = dgsum + _fold8(dout * xhat)
            return lsum, dgsum

        zero = jnp.zeros((SUBLANES, D_MODEL), F32)
        lsum, dgsum = lax.fori_loop(0, ts // rc, tail, (zero, zero))
        loss_ref[...] = loss_ref[...] + (0.5 / D_MODEL) * jnp.sum(lsum, keepdims=True)
        dg3_ref[...] = dg3_ref[...] + jnp.sum(dgsum, axis=0, keepdims=True)

    return pl.pallas_call(
        body, name="fwd_ffn_out", grid=(nt,),
        in_specs=[_rows(ts, D_FF), _prev(ts, D_FF), _next(ts, D_FF, s), _rows(ts, D_FF), _rows(ts, D_MODEL),
                  _const((K_F, D_FF)), _const((D_FF, D_MODEL)), _const((1, D_MODEL)), _rows(ts, D_MODEL)],
        out_specs=[_rows(ts, D_FF), _rows(ts, D_MODEL), _rows(ts, D_MODEL), _acc_out((1, 1)), _acc_out((1, D_MODEL))],
        out_shape=[_sds((s, D_FF), BF16), _sds((s, D_MODEL), F32), _sds((s, D_MODEL), BF16),
                   _sds((1, 1), F32), _sds((1, D_MODEL), F32)],
        scratch_shapes=[_ext_scratch(ts, D_FF)],
        compiler_params=_params(),
    )(g0, g0, g0, v, x2, wf, w_down, g3, target)


_NT = (((1,), (1,)), ((), ()))
_TN = (((0,), (0,)), ((), ()))


def _bwd_ffn_a(dx3b, g0, v, w_down, wf, ts):
    s = dx3b.shape[0]
    nt = s // ts
    rc = min(ROW_CHUNK, ts)
    half = D_FF // 2

    def body(dx_ref, g0_ref, gp_ref, gn_ref, v_ref, wd_ref, wf_ref, dg_ref, dv_ref, dwf_ref, ge_ref, da_ref, acc_ref):
        i = pl.program_id(0)

        @pl.when(i == 0)
        def _():
            acc_ref[...] = jnp.zeros_like(acc_ref)

        _fill_halo(ge_ref, gp_ref[...].astype(F32), gn_ref[...].astype(F32), ts, i == 0, i == nt - 1)

        def fill(j, carry):
            r0 = pl.multiple_of(j * rc, rc)
            _put_rows(ge_ref, r0, rc, g0_ref[pl.ds(r0, rc), :].astype(F32))
            return carry

        lax.fori_loop(0, ts // rc, fill, 0)
        for n0 in range(0, D_FF, half):
            da_ref[:, n0:n0 + half] = lax.dot_general(dx_ref[...], wd_ref[n0:n0 + half, :], _NT,
                                                      preferred_element_type=F32)

        def main(j, carry):
            r0 = pl.multiple_of(j * rc, rc)
            rows = pl.ds(r0, rc)
            for l0 in range(0, D_FF, LANES):
                g = _conv_block(ge_ref, wf_ref, r0, rc, l0, K_F, False)
                sg = _sigmoid(g)
                da = da_ref[rows, l0:l0 + LANES]
                vv = v_ref[rows, l0:l0 + LANES].astype(F32)
                dv_ref[rows, l0:l0 + LANES] = (da * (g * sg)).astype(BF16)
                dgg = da * vv * (sg * (1.0 + g * (1.0 - sg)))
                dg_ref[rows, l0:l0 + LANES] = dgg.astype(BF16)
                _conv_wgrad_block(acc_ref, dgg, ge_ref, r0, rc, l0, K_F)
            return carry

        lax.fori_loop(0, ts // rc, main, 0)

        @pl.when(i == nt - 1)
        def _():
            _reduce_acc(dwf_ref, acc_ref, K_F)

    return pl.pallas_call(
        body, name="bwd_ffn_a", grid=(nt,),
        in_specs=[_rows(ts, D_MODEL), _rows(ts, D_FF), _prev(ts, D_FF), _next(ts, D_FF, s), _rows(ts, D_FF),
                  _const((D_FF, D_MODEL)), _const((K_F, D_FF))],
        out_specs=[_rows(ts, D_FF), _rows(ts, D_FF), _acc_out((K_F, D_FF))],
        out_shape=[_sds((s, D_FF), BF16), _sds((s, D_FF), BF16), _sds((K_F, D_FF), F32)],
        scratch_shapes=[_ext_scratch(ts, D_FF), pltpu.VMEM((ts, D_FF), F32),
                        pltpu.VMEM((K_F, SUBLANES, D_FF), F32)],
        compiler_params=_params(),
    )(dx3b, g0, g0, g0, v, w_down, wf)


def _bwd_ffn_b(dg, dv, wf, w_gate, w_up, x2, g2, dx3, ts):
    s = x2.shape[0]
    nt = s // ts
    rc = min(ROW_CHUNK, ts)

    def body(dg_ref, dgp_ref, dgn_ref, dv_ref, wf_ref, wg_ref, wu_ref, x2_ref, g2_ref, dx3_ref,
             dg0_ref, dx2_ref, dx2b_ref, dgn2_ref, dge_ref):
        i = pl.program_id(0)

        @pl.when(i == 0)
        def _():
            dgn2_ref[...] = jnp.zeros_like(dgn2_ref)

        _fill_halo(dge_ref, dgp_ref[...].astype(F32), dgn_ref[...].astype(F32), ts, i == 0, i == nt - 1)

        def fill(j, carry):
            r0 = pl.multiple_of(j * rc, rc)
            _put_rows(dge_ref, r0, rc, dg_ref[pl.ds(r0, rc), :].astype(F32))
            return carry

        lax.fori_loop(0, ts // rc, fill, 0)

        def main(j, carry):
            r0 = pl.multiple_of(j * rc, rc)
            for l0 in range(0, D_FF, LANES):
                dg0_ref[pl.ds(r0, rc), l0:l0 + LANES] = _conv_block(dge_ref, wf_ref, r0, rc, l0, K_F, True).astype(BF16)
            return carry

        lax.fori_loop(0, ts // rc, main, 0)
        for n0 in range(0, D_MODEL, 512):
            dx2_ref[:, n0:n0 + 512] = (
                lax.dot_general(dg0_ref[...], wg_ref[n0:n0 + 512, :], _NT, preferred_element_type=F32)
                + lax.dot_general(dv_ref[...], wu_ref[n0:n0 + 512, :], _NT, preferred_element_type=F32))

        def tail(j, dgsum):
            r0 = pl.multiple_of(j * rc, rc)
            rows = pl.ds(r0, rc)
            dx, dgrow = _rms_bwd_rows(dx2_ref[rows, :], x2_ref[rows, :], g2_ref[...])
            dx2 = dx3_ref[rows, :] + dx
            dx2_ref[rows, :] = dx2
            dx2b_ref[rows, :] = dx2.astype(BF16)
            return dgsum + _fold8(dgrow)

        dgsum = lax.fori_loop(0, ts // rc, tail, jnp.zeros((SUBLANES, D_MODEL), F32))
        dgn2_ref[...] = dgn2_ref[...] + jnp.sum(dgsum, axis=0, keepdims=True)

    return pl.pallas_call(
        body, name="bwd_ffn_b", grid=(nt,),
        in_specs=[_rows(ts, D_FF), _prev(ts, D_FF), _next(ts, D_FF, s), _rows(ts, D_FF), _const((K_F, D_FF)),
                  _const((D_MODEL, D_FF)), _const((D_MODEL, D_FF)), _rows(ts, D_MODEL), _const((1, D_MODEL)),
                  _rows(ts, D_MODEL)],
        out_specs=[_rows(ts, D_FF), _rows(ts, D_MODEL), _rows(ts, D_MODEL), _acc_out((1, D_MODEL))],
        out_shape=[_sds((s, D_FF), BF16), _sds((s, D_MODEL), F32), _sds((s, D_MODEL), BF16), _sds((1, D_MODEL), F32)],
        scratch_shapes=[_ext_scratch(ts, D_FF)],
        compiler_params=_params(),
    )(dg, dg, dg, dv, wf, w_gate, w_up, x2, g2, dx3)


def _bwd_mix_a(dx2b, w_out, z, u, wa, lg, lb, ts):
    s = dx2b.shape[0]
    nt = s // ts
    rc = min(ROW_CHUNK, ts)

    def body(dx_ref, wo_ref, z_ref, zp_ref, zn_ref, u_ref, wa_ref, lg_ref, lb_ref,
             dca_ref, du_ref, dab_ref, dwa_ref, dlg_ref, dlb_ref, dbb_ref, pe_ref, dy_ref, acc_ref, sacc_ref):
        i = pl.program_id(0)

        @pl.when(i == 0)
        def _():
            acc_ref[...] = jnp.zeros_like(acc_ref)
            sacc_ref[...] = jnp.zeros_like(sacc_ref)

        pp, _ = _p_u0(zp_ref, slice(None))
        pn, _ = _p_u0(zn_ref, slice(None))
        _fill_halo(pe_ref, pp, pn, ts, i == 0, i == nt - 1)

        def fill(j, carry):
            r0 = pl.multiple_of(j * rc, rc)
            rows = pl.ds(r0, rc)
            _put_rows(pe_ref, r0, rc, z_ref[rows, 2 * D_A:3 * D_A].astype(F32) * z_ref[rows, 0:D_A].astype(F32))
            return carry

        lax.fori_loop(0, ts // rc, fill, 0)
        for n0 in range(0, D_MODEL, 512):
            dy_ref[:, n0:n0 + 512] = lax.dot_general(dx_ref[...], wo_ref[n0:n0 + 512, :], _NT,
                                                     preferred_element_type=F32)

        def main(j, carry):
            r0 = pl.multiple_of(j * rc, rc)
            rows = pl.ds(r0, rc)
            for l0 in range(0, D_A, LANES):
                ca = _conv_block(pe_ref, wa_ref, r0, rc, l0, K_A, False)
                a_b = z_ref[rows, D_A + l0:D_A + l0 + LANES].astype(F32)
                dya = dy_ref[rows, l0:l0 + LANES]
                dab_ref[rows, l0:l0 + LANES] = (dya * ca).astype(BF16)
                dca = dya * a_b
                dca_ref[rows, l0:l0 + LANES] = dca
                _conv_wgrad_block(acc_ref, dca, pe_ref, r0, rc, l0, K_A)
            ubs = [u_ref[rows, l0:l0 + LANES] for l0 in range(0, D_B, LANES)]
            mu, rstd = _layernorm_rows(ubs)
            ns, dns = [], []
            m1 = None
            m2 = None
            for q, l0 in enumerate(range(0, D_B, LANES)):
                n = (ubs[q] - mu) * rstd
                lgq = lg_ref[:, l0:l0 + LANES]
                t = n * lgq + lb_ref[:, l0:l0 + LANES]
                sg = _sigmoid(t)
                dt = dy_ref[rows, D_A + l0:D_A + l0 + LANES] * (sg * (1.0 + t * (1.0 - sg)))
                dn = dt * lgq
                ns.append(n)
                dns.append(dn)
                s1 = jnp.sum(dn, axis=-1, keepdims=True)
                s2 = jnp.sum(dn * n, axis=-1, keepdims=True)
                m1 = s1 if m1 is None else m1 + s1
                m2 = s2 if m2 is None else m2 + s2
                sacc_ref[0, :, l0:l0 + LANES] = sacc_ref[0, :, l0:l0 + LANES] + _fold8(dt * n)
                sacc_ref[1, :, l0:l0 + LANES] = sacc_ref[1, :, l0:l0 + LANES] + _fold8(dt)
            m1 = m1 * (1.0 / D_B)
            m2 = m2 * (1.0 / D_B)
            for q, l0 in enumerate(range(0, D_B, LANES)):
                du = rstd * (dns[q] - m1 - ns[q] * m2)
                du_ref[rows, l0:l0 + LANES] = du
                sacc_ref[2, :, l0:l0 + LANES] = sacc_ref[2, :, l0:l0 + LANES] + _fold8(du)
            return carry

        lax.fori_loop(0, ts // rc, main, 0)

        @pl.when(i == nt - 1)
        def _():
            _reduce_acc(dwa_ref, acc_ref, K_A)
            dlg_ref[...] = jnp.sum(sacc_ref[0], axis=0, keepdims=True)
            dlb_ref[...] = jnp.sum(sacc_ref[1], axis=0, keepdims=True)
            dbb_ref[...] = jnp.sum(sacc_ref[2], axis=0, keepdims=True)

    return pl.pallas_call(
        body, name="bwd_mix_a", grid=(nt,),
        in_specs=[_rows(ts, D_MODEL), _const((D_MODEL, D_MODEL)), _rows(ts, D_IN), _prev(ts, D_IN), _next(ts, D_IN, s),
                  _rows(ts, D_B), _const((K_A, D_A)), _const((1, D_B)), _const((1, D_B))],
        out_specs=[_rows(ts, D_A), _rows(ts, D_B), _rows(ts, D_A), _acc_out((K_A, D_A)),
                   _acc_out((1, D_B)), _acc_out((1, D_B)), _acc_out((1, D_B))],
        out_shape=[_sds((s, D_A), F32), _sds((s, D_B), F32), _sds((s, D_A), BF16), _sds((K_A, D_A), F32),
                   _sds((1, D_B), F32), _sds((1, D_B), F32), _sds((1, D_B), F32)],
        scratch_shapes=[_ext_scratch(ts, D_A), pltpu.VMEM((ts, D_MODEL), F32),
                        pltpu.VMEM((K_A, SUBLANES, D_A), F32), pltpu.VMEM((3, SUBLANES, D_B), F32)],
        compiler_params=_params(),
    )(dx2b, w_out, z, z, z, u, wa, lg, lb)


def _bwd_mix_b(dca, du, z, dab, wa, wb, w_in, x, g1, dx2, ts):
    s = x.shape[0]
    nt = s // ts
    rc = min(ROW_CHUNK, ts)

    def body(dca_ref, dcap_ref, dcan_ref, du_ref, dup_ref, dun_ref, z_ref, zp_ref, zn_ref, dab_ref,
             wa_ref, wb_ref, wi_ref, x_ref, g1_ref, dx2_ref,
             dz_ref, dx_ref, dg1_ref, dwb_ref, dcae_ref, due_ref, ue_ref, acc_ref):
        i = pl.program_id(0)

        @pl.when(i == 0)
        def _():
            acc_ref[...] = jnp.zeros_like(acc_ref)
            dg1_ref[...] = jnp.zeros_like(dg1_ref)

        first = i == 0
        last = i == nt - 1
        _fill_halo(dcae_ref, dcap_ref[...], dcan_ref[...], ts, first, last)
        _fill_halo(due_ref, dup_ref[...], dun_ref[...], ts, first, last)
        _, up = _p_u0(zp_ref, slice(None))
        _, un = _p_u0(zn_ref, slice(None))
        _fill_halo(ue_ref, up, un, ts, first, last)

        def fill(j, carry):
            r0 = pl.multiple_of(j * rc, rc)
            rows = pl.ds(r0, rc)
            _put_rows(dcae_ref, r0, rc, dca_ref[rows, :])
            _put_rows(due_ref, r0, rc, du_ref[rows, :])
            b_v = z_ref[rows, 3 * D_A:3 * D_A + D_B].astype(F32)
            b_g = z_ref[rows, 3 * D_A + D_B:D_IN].astype(F32)
            _put_rows(ue_ref, r0, rc, b_v * _sigmoid(b_g))
            return carry

        lax.fori_loop(0, ts // rc, fill, 0)

        def main(j, carry):
            r0 = pl.multiple_of(j * rc, rc)
            rows = pl.ds(r0, rc)
            for l0 in range(0, D_A, LANES):
                dp = _conv_block(dcae_ref, wa_ref, r0, rc, l0, K_A, True)
                a_h = z_ref[rows, l0:l0 + LANES].astype(F32)
                a_c = z_ref[rows, 2 * D_A + l0:2 * D_A + l0 + LANES].astype(F32)
                dz_ref[rows, l0:l0 + LANES] = (dp * a_c).astype(BF16)
                dz_ref[rows, D_A + l0:D_A + l0 + LANES] = dab_ref[rows, l0:l0 + LANES]
                dz_ref[rows, 2 * D_A + l0:2 * D_A + l0 + LANES] = (dp * a_h).astype(BF16)
            for l0 in range(0, D_B, LANES):
                du0 = _conv_block(due_ref, wb_ref, r0, rc, l0, K_B, True)
                b_v = z_ref[rows, 3 * D_A + l0:3 * D_A + l0 + LANES].astype(F32)
                b_g = z_ref[rows, 3 * D_A + D_B + l0:3 * D_A + D_B + l0 + LANES].astype(F32)
                sg = _sigmoid(b_g)
                dz_ref[rows, 3 * D_A + l0:3 * D_A + l0 + LANES] = (du0 * sg).astype(BF16)
                dz_ref[rows, 3 * D_A + D_B + l0:3 * D_A + D_B + l0 + LANES] = (du0 * b_v * (sg * (1.0 - sg))).astype(BF16)
                _conv_wgrad_block(acc_ref, du_ref[rows, l0:l0 + LANES], ue_ref, r0, rc, l0, K_B)
            return carry

        lax.fori_loop(0, ts // rc, main, 0)
        for n0 in range(0, D_MODEL, 512):
            dx_ref[:, n0:n0 + 512] = lax.dot_general(dz_ref[...], wi_ref[n0:n0 + 512, :], _NT,
                                                     preferred_element_type=F32)

        def tail(j, dgsum):
            r0 = pl.multiple_of(j * rc, rc)
            rows = pl.ds(r0, rc)
            dx, dgrow = _rms_bwd_rows(dx_ref[rows, :], x_ref[rows, :], g1_ref[...])
            dx_ref[rows, :] = dx2_ref[rows, :] + dx
            return dgsum + _fold8(dgrow)

        dgsum = lax.fori_loop(0, ts // rc, tail, jnp.zeros((SUBLANES, D_MODEL), F32))
        dg1_ref[...] = dg1_ref[...] + jnp.sum(dgsum, axis=0, keepdims=True)

        @pl.when(i == nt - 1)
        def _():
            _reduce_acc(dwb_ref, acc_ref, K_B)

    return pl.pallas_call(
        body, name="bwd_mix_b", grid=(nt,),
        in_specs=[_rows(ts, D_A), _prev(ts, D_A), _next(ts, D_A, s), _rows(ts, D_B), _prev(ts, D_B), _next(ts, D_B, s),
                  _rows(ts, D_IN), _prev(ts, D_IN), _next(ts, D_IN, s), _rows(ts, D_A),
                  _const((K_A, D_A)), _const((K_B, D_B)), _const((D_MODEL, D_IN)), _rows(ts, D_MODEL),
                  _const((1, D_MODEL)), _rows(ts, D_MODEL)],
        out_specs=[_rows(ts, D_IN), _rows(ts, D_MODEL), _acc_out((1, D_MODEL)), _acc_out((K_B, D_B))],
        out_shape=[_sds((s, D_IN), BF16), _sds((s, D_MODEL), F32), _sds((1, D_MODEL), F32), _sds((K_B, D_B), F32)],
        scratch_shapes=[_ext_scratch(ts, D_A), _ext_scratch(ts, D_B), _ext_scratch(ts, D_B),
                        pltpu.VMEM((K_B, SUBLANES, D_B), F32)],
        compiler_params=_params(),
    )(dca, dca, dca, du, du, du, z, z, z, dab, wa, wb, w_in, x, g1, dx2)


def _matmul_tn(a, b, name):
    s, m = a.shape
    n = b.shape[1]
    tk = min(1024, s)
    nk = s // tk
    tm = 256

    def body(a_ref, b_ref, o_ref, acc_ref):
        k = pl.program_id(0)

        @pl.when(k == 0)
        def _():
            acc_ref[...] = jnp.zeros_like(acc_ref)

        for m0 in range(0, m, tm):
            acc_ref[m0:m0 + tm, :] = acc_ref[m0:m0 + tm, :] + lax.dot_general(
                a_ref[:, m0:m0 + tm], b_ref[...], _TN, preferred_element_type=F32)

        @pl.when(k == nk - 1)
        def _():
            o_ref[...] = acc_ref[...].astype(BF16)

    return pl.pallas_call(
        body, name=name, grid=(nk,),
        in_specs=[_rows(tk, m), _rows(tk, n)],
        out_specs=_acc_out((m, n)),
        out_shape=_sds((m, n), BF16),
        scratch_shapes=[pltpu.VMEM((m, n), F32)],
        compiler_params=_params(),
    )(a, b)


CHIP_RELS = ((1, 0, 0), (0, 1, 0), (1, 1, 0))
CORE_RELS = ((0, 0, 1),)
ALL_RELS = ((0, 0, 1), (0, 1, 0), (0, 1, 1), (1, 0, 0), (1, 0, 1), (1, 1, 0), (1, 1, 1))


def _chip_slot(dev):
    return 2 * dev[0] + dev[1]


def _core_slot(dev):
    return dev[2]


def _dev_slot(dev):
    return 4 * dev[0] + 2 * dev[1] + dev[2]


def _exchange(payloads, rels, slot_of, n_slots, gather, name):
    n_p = len(payloads)
    n_r = len(rels)

    def body(*refs):
        srcs = refs[:n_p]
        dsts = refs[n_p:2 * n_p]
        send_sems, recv_sems, local_sems = refs[2 * n_p:]
        me = (lax.axis_index("x"), lax.axis_index("y"), lax.axis_index("c"))
        copies = []
        if gather:
            for p in range(n_p):
                cp = pltpu.make_async_copy(srcs[p], dsts[p].at[slot_of(me)], local_sems.at[p])
                cp.start()
                copies.append(cp)
        for k, rel in enumerate(rels):
            peer = tuple((1 - me[a]) if rel[a] else me[a] for a in range(3))
            for p in range(n_p):
                if gather:
                    src, dst = srcs[p], dsts[p].at[slot_of(me)]
                else:
                    src, dst = srcs[p].at[slot_of(peer)], dsts[p].at[k]
                cp = pltpu.make_async_remote_copy(
                    src_ref=src, dst_ref=dst, send_sem=send_sems.at[p * n_r + k], recv_sem=recv_sems.at[p * n_r + k],
                    device_id=peer, device_id_type=MESH)
                cp.start()
                copies.append(cp)
        for cp in copies:
            cp.wait()

    if gather:
        out_shape = [_sds((n_slots,) + p.shape, p.dtype) for p in payloads]
    else:
        out_shape = [_sds((n_r,) + p.shape[1:], p.dtype) for p in payloads]
    any_spec = pl.BlockSpec(memory_space=pl.ANY)
    outs = pl.pallas_call(
        body, name=name,
        in_specs=[any_spec] * n_p, out_specs=[any_spec] * n_p, out_shape=out_shape,
        scratch_shapes=[pltpu.SemaphoreType.DMA((n_p * n_r,)), pltpu.SemaphoreType.DMA((n_p * n_r,)),
                        pltpu.SemaphoreType.DMA((n_p,))],
    )(*payloads)
    return list(outs)


ADD_ROWS = PACK_ROWS // 2


def _pair_sum(gp, r1, my_core):
    def body(c_ref, a_ref, b_ref, o_ref):
        del c_ref
        o_ref[...] = (a_ref[...].astype(F32) + b_ref[...].astype(F32)).astype(BF16)

    blk = (None, None, ADD_ROWS, PACK_COLS)
    return pl.pallas_call(
        body, name="rs_pair_sum",
        grid_spec=pltpu.PrefetchScalarGridSpec(
            num_scalar_prefetch=1, grid=(N_CHIP, PACK_ROWS // ADD_ROWS),
            in_specs=[pl.BlockSpec(blk, lambda j, r, c: (c[0], j, r, 0)),
                      pl.BlockSpec(blk, lambda j, r, c: (0, j, r, 0))],
            out_specs=pl.BlockSpec((None, ADD_ROWS, PACK_COLS), lambda j, r, c: (j, r, 0))),
        out_shape=_sds((N_CHIP, PACK_ROWS, PACK_COLS), BF16),
        compiler_params=pltpu.CompilerParams(dimension_semantics=("arbitrary", "arbitrary")),
    )(my_core, gp, r1)


def _chip_sum(p, r2, my_chip):
    def body(c_ref, a_ref, b_ref, o_ref):
        del c_ref
        acc = a_ref[...].astype(F32)
        for k in range(len(CHIP_RELS)):
            acc = acc + b_ref[k].astype(F32)
        o_ref[...] = acc

    return pl.pallas_call(
        body, name="rs_chip_sum",
        grid_spec=pltpu.PrefetchScalarGridSpec(
            num_scalar_prefetch=1, grid=(PACK_ROWS // ADD_ROWS,),
            in_specs=[pl.BlockSpec((None, ADD_ROWS, PACK_COLS), lambda r, c: (c[0], r, 0)),
                      pl.BlockSpec((len(CHIP_RELS), ADD_ROWS, PACK_COLS), lambda r, c: (0, r, 0))],
            out_specs=pl.BlockSpec((ADD_ROWS, PACK_COLS), lambda r, c: (r, 0))),
        out_shape=_sds((PACK_ROWS, PACK_COLS), F32),
        compiler_params=pltpu.CompilerParams(dimension_semantics=("arbitrary",)),
    )(my_chip, p, r2)


def _sum_devices(parts):
    def body(p_ref, o_ref):
        acc = p_ref[0]
        for j in range(1, N_DEV):
            acc = acc + p_ref[j]
        o_ref[...] = acc

    return pl.pallas_call(body, name="small_grad_sum", out_shape=_sds(parts.shape[1:], F32))(parts)


def _adamw(w, g, m, v, name):
    def body(w_ref, g_ref, m_ref, v_ref, d_ref, mo_ref, vo_ref):
        gg = g_ref[...]
        mn = ADAM_B1 * m_ref[...] + (1.0 - ADAM_B1) * gg
        vn = ADAM_B2 * v_ref[...] + (1.0 - ADAM_B2) * (gg * gg)
        m_hat = mn / (1.0 - ADAM_B1 ** ADAM_STEP)
        v_hat = vn / (1.0 - ADAM_B2 ** ADAM_STEP)
        d_ref[...] = -ADAM_LR * (m_hat / (jnp.sqrt(v_hat) + ADAM_EPS) + ADAM_WD * w_ref[...])
        mo_ref[...] = mn
        vo_ref[...] = vn

    sd = _sds(w.shape, F32)
    return pl.pallas_call(body, name=name, out_shape=[sd, sd, sd],
                          compiler_params=pltpu.CompilerParams(vmem_limit_bytes=VMEM_LIMIT))(w, g, m, v)


def _local_step(x, target, g1, w_in, wa, wb, bb, lg, lb, w_out, g2, w_gate, w_up, wf, w_down, g3, ts):
    z, h1 = _fwd_in(x, g1, w_in, ts)
    x2, y, u = _fwd_mix(z, x, wa, wb, bb, lg, lb, w_out, ts)
    g0, v, h2 = _fwd_ffn_in(x2, g2, w_gate, w_up, ts)
    a, dx3, dx3b, loss, dg3 = _fwd_ffn_out(g0, v, x2, wf, w_down, g3, target, ts)
    dgc, dv, dwf = _bwd_ffn_a(dx3b, g0, v, w_down, wf, ts)
    dg0, dx2, dx2b, dg2 = _bwd_ffn_b(dgc, dv, wf, w_gate, w_up, x2, g2, dx3, ts)
    dca, du, dab, dwa, dlg, dlb, dbb = _bwd_mix_a(dx2b, w_out, z, u, wa, lg, lb, ts)
    dz, dx, dg1, dwb = _bwd_mix_b(dca, du, z, dab, wa, wb, w_in, x, g1, dx2, ts)
    dw_down = _matmul_tn(a, dx3b, "wgrad_down")
    dw_gate = _matmul_tn(h2, dg0, "wgrad_gate")
    dw_up = _matmul_tn(h2, dv, "wgrad_up")
    dw_out = _matmul_tn(y, dx2b, "wgrad_out")
    dw_in = _matmul_tn(h1, dz, "wgrad_in")
    big = dict(w_in=dw_in, w_out=dw_out, w_gate=dw_gate, w_up=dw_up, w_down=dw_down)
    small = dict(norm_mix_g=dg1, conv_a_w=dwa, conv_b_w=dwb, conv_b_b=dbb, ln_b_g=dlg, ln_b_b=dlb,
                 norm_ffn_g=dg2, conv_ffn_w=dwf, norm_final_g=dg3)
    return loss, dx, big, small


def _pack_weight_shards(w_in_s, w_gate_s, w_up_s, w_out_s, w_down_s):
    top = jnp.concatenate([w_in_s, w_gate_s, w_up_s], axis=1)
    return jnp.concatenate([top, w_out_s, w_down_s], axis=0).astype(BF16)


def _unpack_weights(full):
    def cols(lo, hi):
        blk = full[:, :, 0:D_MODEL, lo:hi]
        return jnp.transpose(blk, (2, 1, 0, 3)).reshape(D_MODEL, N_DEV * (hi - lo))

    def rows(lo, hi):
        blk = full[:, :, lo:hi, :]
        return jnp.transpose(blk, (1, 0, 2, 3)).reshape(N_DEV * (hi - lo), PACK_COLS)

    w_in = cols(0, W_IN_COLS)
    w_gate = cols(W_IN_COLS, W_IN_COLS + W_FF_COLS)
    w_up = cols(W_IN_COLS + W_FF_COLS, PACK_COLS)
    w_out = rows(D_MODEL, D_MODEL + W_OUT_ROWS)
    w_down = rows(D_MODEL + W_OUT_ROWS, PACK_ROWS)
    return w_in, w_gate, w_up, w_out, w_down


def _pack_big_grads(big):
    def cols(g, w):
        return jnp.transpose(g.reshape(D_MODEL, N_CHIP, 2, w), (2, 1, 0, 3))

    def rows(g, r):
        return jnp.transpose(g.reshape(N_CHIP, 2, r, PACK_COLS), (1, 0, 2, 3))

    top = jnp.concatenate([cols(big["w_in"], W_IN_COLS), cols(big["w_gate"], W_FF_COLS), cols(big["w_up"], W_FF_COLS)],
                          axis=3)
    return jnp.concatenate([top, rows(big["w_out"], W_OUT_ROWS), rows(big["w_down"], W_DOWN_ROWS)], axis=2)


def _pack_small_weights(conv_a_s, conv_b_s, conv_ffn_s):
    buf = jnp.zeros((SMALL_W_ROWS, SMALL_W_COLS), F32)
    buf = buf.at[0:K_A, 0:CONV_A_COLS].set(conv_a_s)
    buf = buf.at[K_A:K_A + K_B, 0:CONV_A_COLS].set(conv_b_s)
    return buf.at[K_A + K_B:K_A + K_B + K_F, 0:W_FF_COLS].set(conv_ffn_s)


def _unpack_small_weights(full):
    def take(r0, k, w):
        blk = full[:, :, r0:r0 + k, 0:w]
        return jnp.transpose(blk, (2, 1, 0, 3)).reshape(k, N_DEV * w)

    return take(0, K_A, CONV_A_COLS), take(K_A, K_B, CONV_A_COLS), take(K_A + K_B, K_F, W_FF_COLS)


_SMALL_LAYOUT = (
    ("norm_mix_g", 2), ("norm_ffn_g", 2), ("norm_final_g", 2), ("conv_b_b", 1), ("ln_b_g", 1), ("ln_b_b", 1),
    ("conv_a_w", K_A), ("conv_b_w", K_B), ("conv_ffn_w", K_F * (FF_PAD // SMALL_G_COLS)), ("loss", 1))


def _pack_small_grads(small, loss):
    parts = []
    for name, n_rows in _SMALL_LAYOUT:
        if name == "loss":
            parts.append(jnp.broadcast_to(loss.reshape(1, 1), (1, SMALL_G_COLS)))
        elif name == "conv_ffn_w":
            parts.append(jnp.pad(small[name], ((0, 0), (0, FF_PAD - D_FF))).reshape(n_rows, SMALL_G_COLS))
        else:
            parts.append(small[name].reshape(n_rows, SMALL_G_COLS))
    used = sum(n for _, n in _SMALL_LAYOUT)
    parts.append(jnp.zeros((SMALL_G_ROWS - used, SMALL_G_COLS), F32))
    return jnp.concatenate(parts, axis=0)


def _unpack_small_grads(tot):
    out = {}
    r = 0
    for name, n_rows in _SMALL_LAYOUT:
        blk = tot[r:r + n_rows]
        r += n_rows
        if name == "loss":
            out[name] = blk[0, 0]
        elif name == "conv_ffn_w":
            out[name] = blk.reshape(K_F, FF_PAD)[:, 0:D_FF]
        elif name in ("conv_a_w", "conv_b_w"):
            out[name] = blk
        else:
            out[name] = blk.reshape(1, n_rows * SMALL_G_COLS)
    return out


def kernel(x, norm_mix_g, w_in, conv_a_w, conv_b_w, conv_b_b, ln_b_g, ln_b_b, w_out, norm_ffn_g, w_gate, w_up, conv_ffn_w, w_down, norm_final_g, loss_target, m_norm_mix_g, m_w_in, m_conv_a_w, m_conv_b_w, m_conv_b_b, m_ln_b_g, m_ln_b_b, m_w_out, m_norm_ffn_g, m_w_gate, m_w_up, m_conv_ffn_w, m_w_down, m_norm_final_g, v_norm_mix_g, v_w_in, v_conv_a_w, v_conv_b_w, v_conv_b_b, v_ln_b_g, v_ln_b_b, v_w_out, v_norm_ffn_g, v_w_gate, v_w_up, v_conv_ffn_w, v_w_down, v_norm_final_g):
    ix, iy, ic = lax.axis_index("x"), lax.axis_index("y"), lax.axis_index("c")
    my_chip = (2 * ix + iy).astype(jnp.int32)
    my_core = ic.astype(jnp.int32)
    my_dev = 2 * my_chip + my_core

    weights = dict(norm_mix_g=norm_mix_g, w_in=w_in, conv_a_w=conv_a_w, conv_b_w=conv_b_w, conv_b_b=conv_b_b,
                   ln_b_g=ln_b_g, ln_b_b=ln_b_b, w_out=w_out, norm_ffn_g=norm_ffn_g, w_gate=w_gate, w_up=w_up,
                   conv_ffn_w=conv_ffn_w, w_down=w_down, norm_final_g=norm_final_g)
    m_in = dict(norm_mix_g=m_norm_mix_g, w_in=m_w_in, conv_a_w=m_conv_a_w, conv_b_w=m_conv_b_w, conv_b_b=m_conv_b_b,
                ln_b_g=m_ln_b_g, ln_b_b=m_ln_b_b, w_out=m_w_out, norm_ffn_g=m_norm_ffn_g, w_gate=m_w_gate,
                w_up=m_w_up, conv_ffn_w=m_conv_ffn_w, w_down=m_w_down, norm_final_g=m_norm_final_g)
    v_in = dict(norm_mix_g=v_norm_mix_g, w_in=v_w_in, conv_a_w=v_conv_a_w, conv_b_w=v_conv_b_w, conv_b_b=v_conv_b_b,
                ln_b_g=v_ln_b_g, ln_b_b=v_ln_b_b, w_out=v_w_out, norm_ffn_g=v_norm_ffn_g, w_gate=v_w_gate,
                w_up=v_w_up, conv_ffn_w=v_conv_ffn_w, w_down=v_w_down, norm_final_g=v_norm_final_g)
    order = list(weights)

    wpack = _pack_weight_shards(w_in[0], w_gate[0], w_up[0], w_out[0], w_down[0])
    spack = _pack_small_weights(conv_a_w[0], conv_b_w[0], conv_ffn_w[0])
    wchips, schips = _exchange([wpack, spack], CHIP_RELS, _chip_slot, N_CHIP, True, "ag_chips")
    wfull, sfull = _exchange([wchips, schips], CORE_RELS, _core_slot, 2, True, "ag_cores")
    w_in_f, w_gate_f, w_up_f, w_out_f, w_down_f = _unpack_weights(wfull)
    wa_f, wb_f, wf_f = _unpack_small_weights(sfull)

    loss, dx, big, small = _local_step(
        x[0], loss_target[0], norm_mix_g, w_in_f, wa_f, wb_f, conv_b_b, ln_b_g, ln_b_b, w_out_f, norm_ffn_g,
        w_gate_f, w_up_f, wf_f, w_down_f, norm_final_g.reshape(1, D_MODEL), SEQ_TILE)

    gp = _pack_big_grads(big)
    (r1,) = _exchange([gp], CORE_RELS, _core_slot, 2, False, "rs_cores")
    psum = _pair_sum(gp, r1, my_core.reshape(1))
    (r2,) = _exchange([psum], CHIP_RELS, _chip_slot, N_CHIP, False, "rs_chips")
    gsum = _chip_sum(psum, r2, my_chip.reshape(1))
    (sparts,) = _exchange([_pack_small_grads(small, loss)], ALL_RELS, _dev_slot, N_DEV, True, "ag_small_grads")
    stot = _unpack_small_grads(_sum_devices(sparts))

    grads = dict(
        norm_mix_g=stot["norm_mix_g"],
        w_in=gsum[0:D_MODEL, 0:W_IN_COLS],
        conv_a_w=lax.dynamic_slice(stot["conv_a_w"], (0, my_dev * CONV_A_COLS), (K_A, CONV_A_COLS)),
        conv_b_w=lax.dynamic_slice(stot["conv_b_w"], (0, my_dev * CONV_A_COLS), (K_B, CONV_A_COLS)),
        conv_b_b=stot["conv_b_b"], ln_b_g=stot["ln_b_g"], ln_b_b=stot["ln_b_b"],
        w_out=gsum[D_MODEL:D_MODEL + W_OUT_ROWS, :],
        norm_ffn_g=stot["norm_ffn_g"],
        w_gate=gsum[0:D_MODEL, W_IN_COLS:W_IN_COLS + W_FF_COLS],
        w_up=gsum[0:D_MODEL, W_IN_COLS + W_FF_COLS:PACK_COLS],
        conv_ffn_w=lax.dynamic_slice(stot["conv_ffn_w"], (0, my_dev * W_FF_COLS), (K_F, W_FF_COLS)),
        w_down=gsum[D_MODEL + W_OUT_ROWS:PACK_ROWS, :],
        norm_final_g=stot["norm_final_g"],
    )

    g_out, d_out, m_out, v_out = [], [], [], []
    for name in order:
        w = weights[name]
        shape2 = (1, w.shape[0]) if w.ndim == 1 else w.shape[-2:]
        g2d = grads[name].reshape(shape2)
        d, mn, vn = _adamw(w.reshape(shape2), g2d, m_in[name].reshape(shape2), v_in[name].reshape(shape2),
                           "adamw_" + name)
        g_out.append(g2d.reshape(w.shape))
        d_out.append(d.reshape(w.shape))
        m_out.append(mn.reshape(w.shape))
        v_out.append(vn.reshape(w.shape))

    return (stot["loss"], dx[None], *g_out, *d_out, *m_out, *v_out)
```

```python
import jax
import jax.numpy as jnp
from jax import lax
from jax.experimental import pallas as pl
from jax.experimental.pallas import tpu as pltpu

F32 = jnp.float32
BF16 = jnp.bfloat16

D_MODEL = 1024
D_A = 512
D_B = 512
D_IN = 3 * D_A + 2 * D_B
D_FF = 2816
K_A = 3
K_B = 31
K_F = 3
RMS_EPS = 1e-6
LN_EPS = 1e-5

ADAM_LR = 0.001
ADAM_B1 = 0.9
ADAM_B2 = 0.999
ADAM_EPS = 1e-08
ADAM_WD = 0.01
ADAM_STEP = 10

N_DEV = 8
N_CHIP = 4
LANES = 128
SUBLANES = 8
HALO = 16
ROW_CHUNK = 64
SEQ_TILE = 512
VMEM_LIMIT = 56 * 1024 * 1024

MESH = pl.DeviceIdType.MESH

W_FF_COLS = D_FF // N_DEV
CONV_A_COLS = D_A // N_DEV
SMALL_W_ROWS = 40
SMALL_W_COLS = 384
SMALL_G_COLS = 512
FF_PAD = 3072


def _rows(ts, c):
    return pl.BlockSpec((ts, c), lambda i: (i, 0))


def _const(shape):
    return pl.BlockSpec(shape, lambda i: (0,) * len(shape), pipeline_mode=pl.Buffered(1))


def _acc_out(shape):
    return pl.BlockSpec(shape, lambda i: (0,) * len(shape))


def _prev(ts, c):
    return pl.BlockSpec((HALO, c), lambda i: (jnp.maximum(i * (ts // HALO) - 1, 0), 0))


def _next(ts, c, s):
    last = s // HALO - 1
    return pl.BlockSpec((HALO, c), lambda i: (jnp.minimum((i + 1) * (ts // HALO), last), 0))


def _params():
    return pltpu.CompilerParams(dimension_semantics=("arbitrary",), vmem_limit_bytes=VMEM_LIMIT)


def _sds(shape, dtype):
    return jax.ShapeDtypeStruct(shape, dtype)


def _sigmoid(v):
    return 1.0 / (1.0 + jnp.exp(-v))


def _conv_block(ext_ref, w_ref, r0, rc, l0, k_taps, transposed):
    acc = None
    for k in range(k_taps):
        d = (k_taps // 2 - k) if transposed else (k - k_taps // 2)
        term = ext_ref[l0 // LANES, pl.ds(r0 + HALO + d, rc), :] * w_ref[k:k + 1, l0:l0 + LANES]
        acc = term if acc is None else acc + term
    return acc


def _conv_wgrad_block(acc_ref, dout, ext_ref, r0, rc, l0, k_taps):
    for k in range(k_taps):
        prod = dout * ext_ref[l0 // LANES, pl.ds(r0 + HALO + k - k_taps // 2, rc), :]
        part = prod.reshape(rc // SUBLANES, SUBLANES, LANES).sum(axis=0)
        acc_ref[k, :, l0:l0 + LANES] = acc_ref[k, :, l0:l0 + LANES] + part


def _reduce_acc(out_ref, acc_ref, k_taps):
    for k in range(k_taps):
        out_ref[k:k + 1, :] = jnp.sum(acc_ref[k], axis=0, keepdims=True)


def _fold8(v):
    rc, c = v.shape
    return v.reshape(rc // SUBLANES, SUBLANES, c).sum(axis=0)


def _ext_scratch(ts, c):
    return pltpu.VMEM((c // LANES, ts + 2 * HALO, LANES), F32)


def _put_rows(ext_ref, r0, rc, val):
    for q in range(val.shape[1] // LANES):
        ext_ref[q, pl.ds(r0 + HALO, rc), :] = val[:, q * LANES:(q + 1) * LANES]


def _fill_halo(ext_ref, vals_prev, vals_next, ts, first, last):
    for q in range(vals_prev.shape[1] // LANES):
        cols = slice(q * LANES, (q + 1) * LANES)
        ext_ref[q, 0:HALO, :] = jnp.where(first, 0.0, vals_prev[:, cols])
        ext_ref[q, HALO + ts:HALO + ts + HALO, :] = jnp.where(last, 0.0, vals_next[:, cols])


def _rms_bwd_rows(dh, xf, g):
    r = lax.rsqrt(jnp.mean(xf * xf, axis=-1, keepdims=True) + RMS_EPS)
    xhat = xf * r
    dxh = dh * g
    dx = r * (dxh - xhat * jnp.mean(dxh * xhat, axis=-1, keepdims=True))
    return dx, dh * xhat


_NT = (((1,), (1,)), ((), ()))
_TN = (((0,), (0,)), ((), ()))


def _fwd_in(x, g1, w_in_t, ts):
    s = x.shape[0]

    def body(x_ref, g_ref, w_ref, z_ref, h_ref):
        xf = x_ref[...]
        r = lax.rsqrt(jnp.mean(xf * xf, axis=-1, keepdims=True) + RMS_EPS)
        h = (xf * r * g_ref[...]).astype(BF16)
        h_ref[...] = h
        for n0 in range(0, D_IN, 512):
            z_ref[:, n0:n0 + 512] = lax.dot_general(h, w_ref[n0:n0 + 512, :], _NT,
                                                    preferred_element_type=F32).astype(BF16)

    return pl.pallas_call(
        body, name="fwd_in", grid=(s // ts,),
        in_specs=[_rows(ts, D_MODEL), _const((1, D_MODEL)), _const((D_IN, D_MODEL))],
        out_specs=[_rows(ts, D_IN), _rows(ts, D_MODEL)],
        out_shape=[_sds((s, D_IN), BF16), _sds((s, D_MODEL), BF16)],
        compiler_params=_params(),
    )(x, g1, w_in_t)


def _p_u0(z_ref, rows):
    a_h = z_ref[rows, 0:D_A].astype(F32)
    a_c = z_ref[rows, 2 * D_A:3 * D_A].astype(F32)
    b_v = z_ref[rows, 3 * D_A:3 * D_A + D_B].astype(F32)
    b_g = z_ref[rows, 3 * D_A + D_B:D_IN].astype(F32)
    return a_c * a_h, b_v * _sigmoid(b_g)


def _layernorm_rows(u_blocks):
    tot = None
    for ub in u_blocks:
        sm = jnp.sum(ub, axis=-1, keepdims=True)
        tot = sm if tot is None else tot + sm
    mu = tot * (1.0 / D_B)
    var = None
    for ub in u_blocks:
        sq = jnp.sum((ub - mu) * (ub - mu), axis=-1, keepdims=True)
        var = sq if var is None else var + sq
    rstd = lax.rsqrt(var * (1.0 / D_B) + LN_EPS)
    return mu, rstd


def _fwd_mix(z, x, wa, wb, bb, lg, lb, w_out, ts):
    s = x.shape[0]
    nt = s // ts
    rc = min(ROW_CHUNK, ts)

    def body(z_ref, zp_ref, zn_ref, x_ref, wa_ref, wb_ref, bb_ref, lg_ref, lb_ref, wo_ref,
             x2_ref, y_ref, u_ref, pe_ref, ue_ref):
        i = pl.program_id(0)
        pp, up = _p_u0(zp_ref, slice(None))
        pn, un = _p_u0(zn_ref, slice(None))
        _fill_halo(pe_ref, pp, pn, ts, i == 0, i == nt - 1)
        _fill_halo(ue_ref, up, un, ts, i == 0, i == nt - 1)

        def fill(j, carry):
            r0 = pl.multiple_of(j * rc, rc)
            p, u0 = _p_u0(z_ref, pl.ds(r0, rc))
            _put_rows(pe_ref, r0, rc, p)
            _put_rows(ue_ref, r0, rc, u0)
            return carry

        lax.fori_loop(0, ts // rc, fill, 0)

        def main(j, carry):
            r0 = pl.multiple_of(j * rc, rc)
            rows = pl.ds(r0, rc)
            for l0 in range(0, D_A, LANES):
                ca = _conv_block(pe_ref, wa_ref, r0, rc, l0, K_A, False)
                a_b = z_ref[rows, D_A + l0:D_A + l0 + LANES].astype(F32)
                y_ref[rows, l0:l0 + LANES] = (a_b * ca).astype(BF16)
            ubs = []
            for l0 in range(0, D_B, LANES):
                ub = _conv_block(ue_ref, wb_ref, r0, rc, l0, K_B, False) + bb_ref[:, l0:l0 + LANES]
                u_ref[rows, l0:l0 + LANES] = ub
                ubs.append(ub)
            mu, rstd = _layernorm_rows(ubs)
            for q, l0 in enumerate(range(0, D_B, LANES)):
                t = (ubs[q] - mu) * rstd * lg_ref[:, l0:l0 + LANES] + lb_ref[:, l0:l0 + LANES]
                y_ref[rows, D_A + l0:D_A + l0 + LANES] = (t * _sigmoid(t)).astype(BF16)
            return carry

        lax.fori_loop(0, ts // rc, main, 0)
        for n0 in range(0, D_MODEL, 512):
            x2_ref[:, n0:n0 + 512] = x_ref[:, n0:n0 + 512] + jnp.dot(
                y_ref[...], wo_ref[:, n0:n0 + 512], preferred_element_type=F32)

    return pl.pallas_call(
        body, name="fwd_mix", grid=(nt,),
        in_specs=[_rows(ts, D_IN), _prev(ts, D_IN), _next(ts, D_IN, s), _rows(ts, D_MODEL),
                  _const((K_A, D_A)), _const((K_B, D_B)), _const((1, D_B)), _const((1, D_B)), _const((1, D_B)),
                  _const((D_MODEL, D_MODEL))],
        out_specs=[_rows(ts, D_MODEL), _rows(ts, D_MODEL), _rows(ts, D_B)],
        out_shape=[_sds((s, D_MODEL), F32), _sds((s, D_MODEL), BF16), _sds((s, D_B), F32)],
        scratch_shapes=[_ext_scratch(ts, D_A), _ext_scratch(ts, D_B)],
        compiler_params=_params(),
    )(z, z, z, x, wa, wb, bb, lg, lb, w_out)


def _fwd_ffn_in(x2, g2, w_gate_t, w_up_t, ts):
    s = x2.shape[0]
    half = D_FF // 2

    def body(x_ref, g_ref, wg_ref, wu_ref, g0_ref, v_ref, h_ref):
        xf = x_ref[...]
        r = lax.rsqrt(jnp.mean(xf * xf, axis=-1, keepdims=True) + RMS_EPS)
        h = (xf * r * g_ref[...]).astype(BF16)
        h_ref[...] = h
        for n0 in range(0, D_FF, half):
            g0_ref[:, n0:n0 + half] = lax.dot_general(h, wg_ref[n0:n0 + half, :], _NT,
                                                      preferred_element_type=F32).astype(BF16)
            v_ref[:, n0:n0 + half] = lax.dot_general(h, wu_ref[n0:n0 + half, :], _NT,
                                                     preferred_element_type=F32).astype(BF16)

    return pl.pallas_call(
        body, name="fwd_ffn_in", grid=(s // ts,),
        in_specs=[_rows(ts, D_MODEL), _const((1, D_MODEL)), _const((D_FF, D_MODEL)), _const((D_FF, D_MODEL))],
        out_specs=[_rows(ts, D_FF), _rows(ts, D_FF), _rows(ts, D_MODEL)],
        out_shape=[_sds((s, D_FF), BF16), _sds((s, D_FF), BF16), _sds((s, D_MODEL), BF16)],
        compiler_params=_params(),
    )(x2, g2, w_gate_t, w_up_t)


def _fwd_ffn_out(g0, v, x2, wf, w_down, g3, target, ts):
    s = x2.shape[0]
    nt = s // ts
    rc = min(ROW_CHUNK, ts)

    def body(g0_ref, gp_ref, gn_ref, v_ref, x2_ref, wf_ref, wd_ref, g3_ref, t_ref,
             a_ref, dx3_ref, dx3b_ref, loss_ref, dg3_ref, ge_ref):
        i = pl.program_id(0)
        _fill_halo(ge_ref, gp_ref[...].astype(F32), gn_ref[...].astype(F32), ts, i == 0, i == nt - 1)

        def fill(j, carry):
            r0 = pl.multiple_of(j * rc, rc)
            _put_rows(ge_ref, r0, rc, g0_ref[pl.ds(r0, rc), :].astype(F32))
            return carry

        lax.fori_loop(0, ts // rc, fill, 0)

        def act(j, carry):
            r0 = pl.multiple_of(j * rc, rc)
            rows = pl.ds(r0, rc)
            for l0 in range(0, D_FF, LANES):
                g = _conv_block(ge_ref, wf_ref, r0, rc, l0, K_F, False)
                vv = v_ref[rows, l0:l0 + LANES].astype(F32)
                a_ref[rows, l0:l0 + LANES] = (g * _sigmoid(g) * vv).astype(BF16)
            return carry

        lax.fori_loop(0, ts // rc, act, 0)
        for n0 in range(0, D_MODEL, 512):
            dx3_ref[:, n0:n0 + 512] = x2_ref[:, n0:n0 + 512] + jnp.dot(
                a_ref[...], wd_ref[:, n0:n0 + 512], preferred_element_type=F32)

        @pl.when(i == 0)
        def _():
            loss_ref[...] = jnp.zeros_like(loss_ref)
            dg3_ref[...] = jnp.zeros_like(dg3_ref)

        def tail(j, carry):
            lsum, dgsum = carry
            r0 = pl.multiple_of(j * rc, rc)
            rows = pl.ds(r0, rc)
            x3 = dx3_ref[rows, :]
            r = lax.rsqrt(jnp.mean(x3 * x3, axis=-1, keepdims=True) + RMS_EPS)
            xhat = x3 * r
            diff = xhat * g3_ref[...] - t_ref[rows, :]
            dout = diff * (1.0 / D_MODEL)
            dxh = dout * g3_ref[...]
            dx3 = r * (dxh - xhat * jnp.mean(dxh * xhat, axis=-1, keepdims=True))
            dx3_ref[rows, :] = dx3
            dx3b_ref[rows, :] = dx3.astype(BF16)
            lsum = lsum + _fold8(diff * diff)
            dgsum = dgsum + _fold8(dout * xhat)
            return lsum, dgsum

        zero = jnp.zeros((SUBLANES, D_MODEL), F32)
        lsum, dgsum = lax.fori_loop(0, ts // rc, tail, (zero, zero))
        loss_ref[...] = loss_ref[...] + (0.5 / D_MODEL) * jnp.sum(lsum, keepdims=True)
        dg3_ref[...] = dg3_ref[...] + jnp.sum(dgsum, axis=0, keepdims=True)

    return pl.pallas_call(
        body, name="fwd_ffn_out", grid=(nt,),
        in_specs=[_rows(ts, D_FF), _prev(ts, D_FF), _next(ts, D_FF, s), _rows(ts, D_FF), _rows(ts, D_MODEL),
                  _const((K_F, D_FF)), _const((D_FF, D_MODEL)), _const((1, D_MODEL)), _rows(ts, D_MODEL)],
        out_specs=[_rows(ts, D_FF), _rows(ts, D_MODEL), _rows(ts, D_MODEL), _acc_out((1, 1)), _acc_out((1, D_MODEL))],
        out_shape=[_sds((s, D_FF), BF16), _sds((s, D_MODEL), F32), _sds((s, D_MODEL), BF16),
                   _sds((1, 1), F32), _sds((1, D_MODEL), F32)],
        scratch_shapes=[_ext_scratch(ts, D_FF)],
        compiler_params=_params(),
    )(g0, g0, g0, v, x2, wf, w_down, g3, target)


def _bwd_ffn_a(dx3b, g0, v, w_down, wf, ts):
    s = dx3b.shape[0]
    nt = s // ts
    rc = min(ROW_CHUNK, ts)
    half = D_FF // 2

    def body(dx_ref, g0_ref, gp_ref, gn_ref, v_ref, wd_ref, wf_ref, dg_ref, dv_ref, dwf_ref, ge_ref, da_ref, acc_ref):
        i = pl.program_id(0)

        @pl.when(i == 0)
        def _():
            acc_ref[...] = jnp.zeros_like(acc_ref)

        _fill_halo(ge_ref, gp_ref[...].astype(F32), gn_ref[...].astype(F32), ts, i == 0, i == nt - 1)

        def fill(j, carry):
            r0 = pl.multiple_of(j * rc, rc)
            _put_rows(ge_ref, r0, rc, g0_ref[pl.ds(r0, rc), :].astype(F32))
            return carry

        lax.fori_loop(0, ts // rc, fill, 0)
        for n0 in range(0, D_FF, half):
            da_ref[:, n0:n0 + half] = lax.dot_general(dx_ref[...], wd_ref[n0:n0 + half, :], _NT,
                                                      preferred_element_type=F32)

        def main(j, carry):
            r0 = pl.multiple_of(j * rc, rc)
            rows = pl.ds(r0, rc)
            for l0 in range(0, D_FF, LANES):
                g = _conv_block(ge_ref, wf_ref, r0, rc, l0, K_F, False)
                sg = _sigmoid(g)
                da = da_ref[rows, l0:l0 + LANES]
                vv = v_ref[rows, l0:l0 + LANES].astype(F32)
                dv_ref[rows, l0:l0 + LANES] = (da * (g * sg)).astype(BF16)
                dgg = da * vv * (sg * (1.0 + g * (1.0 - sg)))
                dg_ref[rows, l0:l0 + LANES] = dgg.astype(BF16)
                _conv_wgrad_block(acc_ref, dgg, ge_ref, r0, rc, l0, K_F)
            return carry

        lax.fori_loop(0, ts // rc, main, 0)

        @pl.when(i == nt - 1)
        def _():
            _reduce_acc(dwf_ref, acc_ref, K_F)

    return pl.pallas_call(
        body, name="bwd_ffn_a", grid=(nt,),
        in_specs=[_rows(ts, D_MODEL), _rows(ts, D_FF), _prev(ts, D_FF), _next(ts, D_FF, s), _rows(ts, D_FF),
                  _const((D_FF, D_MODEL)), _const((K_F, D_FF))],
        out_specs=[_rows(ts, D_FF), _rows(ts, D_FF), _acc_out((K_F, D_FF))],
        out_shape=[_sds((s, D_FF), BF16), _sds((s, D_FF), BF16), _sds((K_F, D_FF), F32)],
        scratch_shapes=[_ext_scratch(ts, D_FF), pltpu.VMEM((ts, D_FF), F32),
                        pltpu.VMEM((K_F, SUBLANES, D_FF), F32)],
        compiler_params=_params(),
    )(dx3b, g0, g0, g0, v, w_down, wf)


def _bwd_ffn_b(dg, dv, wf, w_gate, w_up, x2, g2, dx3, ts):
    s = x2.shape[0]
    nt = s // ts
    rc = min(ROW_CHUNK, ts)

    def body(dg_ref, dgp_ref, dgn_ref, dv_ref, wf_ref, wg_ref, wu_ref, x2_ref, g2_ref, dx3_ref,
             dg0_ref, dx2_ref, dx2b_ref, dgn2_ref, dge_ref):
        i = pl.program_id(0)

        @pl.when(i == 0)
        def _():
            dgn2_ref[...] = jnp.zeros_like(dgn2_ref)

        _fill_halo(dge_ref, dgp_ref[...].astype(F32), dgn_ref[...].astype(F32), ts, i == 0, i == nt - 1)

        def fill(j, carry):
            r0 = pl.multiple_of(j * rc, rc)
            _put_rows(dge_ref, r0, rc, dg_ref[pl.ds(r0, rc), :].astype(F32))
            return carry

        lax.fori_loop(0, ts // rc, fill, 0)

        def main(j, carry):
            r0 = pl.multiple_of(j * rc, rc)
            for l0 in range(0, D_FF, LANES):
                dg0_ref[pl.ds(r0, rc), l0:l0 + LANES] = _conv_block(dge_ref, wf_ref, r0, rc, l0, K_F, True).astype(BF16)
            return carry

        lax.fori_loop(0, ts // rc, main, 0)
        for n0 in range(0, D_MODEL, 512):
            dx2_ref[:, n0:n0 + 512] = (
                jnp.dot(dg0_ref[...], wg_ref[:, n0:n0 + 512], preferred_element_type=F32)
                + jnp.dot(dv_ref[...], wu_ref[:, n0:n0 + 512], preferred_element_type=F32))

        def tail(j, dgsum):
            r0 = pl.multiple_of(j * rc, rc)
            rows = pl.ds(r0, rc)
            dx, dgrow = _rms_bwd_rows(dx2_ref[rows, :], x2_ref[rows, :], g2_ref[...])
            dx2 = dx3_ref[rows, :] + dx
            dx2_ref[rows, :] = dx2
            dx2b_ref[rows, :] = dx2.astype(BF16)
            return dgsum + _fold8(dgrow)

        dgsum = lax.fori_loop(0, ts // rc, tail, jnp.zeros((SUBLANES, D_MODEL), F32))
        dgn2_ref[...] = dgn2_ref[...] + jnp.sum(dgsum, axis=0, keepdims=True)

    return pl.pallas_call(
        body, name="bwd_ffn_b", grid=(nt,),
        in_specs=[_rows(ts, D_FF), _prev(ts, D_FF), _next(ts, D_FF, s), _rows(ts, D_FF), _const((K_F, D_FF)),
                  _const((D_FF, D_MODEL)), _const((D_FF, D_MODEL)), _rows(ts, D_MODEL), _const((1, D_MODEL)),
                  _rows(ts, D_MODEL)],
        out_specs=[_rows(ts, D_FF), _rows(ts, D_MODEL), _rows(ts, D_MODEL), _acc_out((1, D_MODEL))],
        out_shape=[_sds((s, D_FF), BF16), _sds((s, D_MODEL), F32), _sds((s, D_MODEL), BF16), _sds((1, D_MODEL), F32)],
        scratch_shapes=[_ext_scratch(ts, D_FF)],
        compiler_params=_params(),
    )(dg, dg, dg, dv, wf, w_gate, w_up, x2, g2, dx3)


def _bwd_mix_a(dx2b, w_out, z, u, wa, lg, lb, ts):
    s = dx2b.shape[0]
    nt = s // ts
    rc = min(ROW_CHUNK, ts)

    def body(dx_ref, wo_ref, z_ref, zp_ref, zn_ref, u_ref, wa_ref, lg_ref, lb_ref,
             dca_ref, du_ref, dab_ref, dwa_ref, dlg_ref, dlb_ref, dbb_ref, pe_ref, dy_ref, acc_ref, sacc_ref):
        i = pl.program_id(0)

        @pl.when(i == 0)
        def _():
            acc_ref[...] = jnp.zeros_like(acc_ref)
            sacc_ref[...] = jnp.zeros_like(sacc_ref)

        pp, _ = _p_u0(zp_ref, slice(None))
        pn, _ = _p_u0(zn_ref, slice(None))
        _fill_halo(pe_ref, pp, pn, ts, i == 0, i == nt - 1)

        def fill(j, carry):
            r0 = pl.multiple_of(j * rc, rc)
            rows = pl.ds(r0, rc)
            _put_rows(pe_ref, r0, rc, z_ref[rows, 2 * D_A:3 * D_A].astype(F32) * z_ref[rows, 0:D_A].astype(F32))
            return carry

        lax.fori_loop(0, ts // rc, fill, 0)
        for n0 in range(0, D_MODEL, 512):
            dy_ref[:, n0:n0 + 512] = lax.dot_general(dx_ref[...], wo_ref[n0:n0 + 512, :], _NT,
                                                     preferred_element_type=F32)

        def main(j, carry):
            r0 = pl.multiple_of(j * rc, rc)
            rows = pl.ds(r0, rc)
            for l0 in range(0, D_A, LANES):
                ca = _conv_block(pe_ref, wa_ref, r0, rc, l0, K_A, False)
                a_b = z_ref[rows, D_A + l0:D_A + l0 + LANES].astype(F32)
                dya = dy_ref[rows, l0:l0 + LANES]
                dab_ref[rows, l0:l0 + LANES] = (dya * ca).astype(BF16)
                dca = dya * a_b
                dca_ref[rows, l0:l0 + LANES] = dca
                _conv_wgrad_block(acc_ref, dca, pe_ref, r0, rc, l0, K_A)
            ubs = [u_ref[rows, l0:l0 + LANES] for l0 in range(0, D_B, LANES)]
            mu, rstd = _layernorm_rows(ubs)
            ns, dns = [], []
            m1 = None
            m2 = None
            for q, l0 in enumerate(range(0, D_B, LANES)):
                n = (ubs[q] - mu) * rstd
                lgq = lg_ref[:, l0:l0 + LANES]
                t = n * lgq + lb_ref[:, l0:l0 + LANES]
                sg = _sigmoid(t)
                dt = dy_ref[rows, D_A + l0:D_A + l0 + LANES] * (sg * (1.0 + t * (1.0 - sg)))
                dn = dt * lgq
                ns.append(n)
                dns.append(dn)
                s1 = jnp.sum(dn, axis=-1, keepdims=True)
                s2 = jnp.sum(dn * n, axis=-1, keepdims=True)
                m1 = s1 if m1 is None else m1 + s1
                m2 = s2 if m2 is None else m2 + s2
                sacc_ref[0, :, l0:l0 + LANES] = sacc_ref[0, :, l0:l0 + LANES] + _fold8(dt * n)
                sacc_ref[1, :, l0:l0 + LANES] = sacc_ref[1, :, l0:l0 + LANES] + _fold8(dt)
            m1 = m1 * (1.0 / D_B)
            m2 = m2 * (1.0 / D_B)
            for q, l0 in enumerate(range(0, D_B, LANES)):
                du = rstd * (dns[q] - m1 - ns[q] * m2)
                du_ref[rows, l0:l0 + LANES] = du
                sacc_ref[2, :, l0:l0 + LANES] = sacc_ref[2, :, l0:l0 + LANES] + _fold8(du)
            return carry

        lax.fori_loop(0, ts // rc, main, 0)

        @pl.when(i == nt - 1)
        def _():
            _reduce_acc(dwa_ref, acc_ref, K_A)
            dlg_ref[...] = jnp.sum(sacc_ref[0], axis=0, keepdims=True)
            dlb_ref[...] = jnp.sum(sacc_ref[1], axis=0, keepdims=True)
            dbb_ref[...] = jnp.sum(sacc_ref[2], axis=0, keepdims=True)

    return pl.pallas_call(
        body, name="bwd_mix_a", grid=(nt,),
        in_specs=[_rows(ts, D_MODEL), _const((D_MODEL, D_MODEL)), _rows(ts, D_IN), _prev(ts, D_IN), _next(ts, D_IN, s),
                  _rows(ts, D_B), _const((K_A, D_A)), _const((1, D_B)), _const((1, D_B))],
        out_specs=[_rows(ts, D_A), _rows(ts, D_B), _rows(ts, D_A), _acc_out((K_A, D_A)),
                   _acc_out((1, D_B)), _acc_out((1, D_B)), _acc_out((1, D_B))],
        out_shape=[_sds((s, D_A), F32), _sds((s, D_B), F32), _sds((s, D_A), BF16), _sds((K_A, D_A), F32),
                   _sds((1, D_B), F32), _sds((1, D_B), F32), _sds((1, D_B), F32)],
        scratch_shapes=[_ext_scratch(ts, D_A), pltpu.VMEM((ts, D_MODEL), F32),
                        pltpu.VMEM((K_A, SUBLANES, D_A), F32), pltpu.VMEM((3, SUBLANES, D_B), F32)],
        compiler_params=_params(),
    )(dx2b, w_out, z, z, z, u, wa, lg, lb)


def _bwd_mix_b(dca, du, z, dab, wa, wb, w_in, x, g1, dx2, ts):
    s = x.shape[0]
    nt = s // ts
    rc = min(ROW_CHUNK, ts)

    def body(dca_ref, dcap_ref, dcan_ref, du_ref, dup_ref, dun_ref, z_ref, zp_ref, zn_ref, dab_ref,
             wa_ref, wb_ref, wi_ref, x_ref, g1_ref, dx2_ref,
             dz_ref, dx_ref, dg1_ref, dwb_ref, dcae_ref, due_ref, ue_ref, acc_ref):
        i = pl.program_id(0)

        @pl.when(i == 0)
        def _():
            acc_ref[...] = jnp.zeros_like(acc_ref)
            dg1_ref[...] = jnp.zeros_like(dg1_ref)

        first = i == 0
        last = i == nt - 1
        _fill_halo(dcae_ref, dcap_ref[...], dcan_ref[...], ts, first, last)
        _fill_halo(due_ref, dup_ref[...], dun_ref[...], ts, first, last)
        _, up = _p_u0(zp_ref, slice(None))
        _, un = _p_u0(zn_ref, slice(None))
        _fill_halo(ue_ref, up, un, ts, first, last)

        def fill(j, carry):
            r0 = pl.multiple_of(j * rc, rc)
            rows = pl.ds(r0, rc)
            _put_rows(dcae_ref, r0, rc, dca_ref[rows, :])
            _put_rows(due_ref, r0, rc, du_ref[rows, :])
            b_v = z_ref[rows, 3 * D_A:3 * D_A + D_B].astype(F32)
            b_g = z_ref[rows, 3 * D_A + D_B:D_IN].astype(F32)
            _put_rows(ue_ref, r0, rc, b_v * _sigmoid(b_g))
            return carry

        lax.fori_loop(0, ts // rc, fill, 0)

        def main(j, carry):
            r0 = pl.multiple_of(j * rc, rc)
            rows = pl.ds(r0, rc)
            for l0 in range(0, D_A, LANES):
                dp = _conv_block(dcae_ref, wa_ref, r0, rc, l0, K_A, True)
                a_h = z_ref[rows, l0:l0 + LANES].astype(F32)
                a_c = z_ref[rows, 2 * D_A + l0:2 * D_A + l0 + LANES].astype(F32)
                dz_ref[rows, l0:l0 + LANES] = (dp * a_c).astype(BF16)
                dz_ref[rows, D_A + l0:D_A + l0 + LANES] = dab_ref[rows, l0:l0 + LANES]
                dz_ref[rows, 2 * D_A + l0:2 * D_A + l0 + LANES] = (dp * a_h).astype(BF16)
            for l0 in range(0, D_B, LANES):
                du0 = _conv_block(due_ref, wb_ref, r0, rc, l0, K_B, True)
                b_v = z_ref[rows, 3 * D_A + l0:3 * D_A + l0 + LANES].astype(F32)
                b_g = z_ref[rows, 3 * D_A + D_B + l0:3 * D_A + D_B + l0 + LANES].astype(F32)
                sg = _sigmoid(b_g)
                dz_ref[rows, 3 * D_A + l0:3 * D_A + l0 + LANES] = (du0 * sg).astype(BF16)
                dz_ref[rows, 3 * D_A + D_B + l0:3 * D_A + D_B + l0 + LANES] = (du0 * b_v * (sg * (1.0 - sg))).astype(BF16)
                _conv_wgrad_block(acc_ref, du_ref[rows, l0:l0 + LANES], ue_ref, r0, rc, l0, K_B)
            return carry

        lax.fori_loop(0, ts // rc, main, 0)
        for n0 in range(0, D_MODEL, 512):
            dx_ref[:, n0:n0 + 512] = jnp.dot(dz_ref[...], wi_ref[:, n0:n0 + 512], preferred_element_type=F32)

        def tail(j, dgsum):
            r0 = pl.multiple_of(j * rc, rc)
            rows = pl.ds(r0, rc)
            dx, dgrow = _rms_bwd_rows(dx_ref[rows, :], x_ref[rows, :], g1_ref[...])
            dx_ref[rows, :] = dx2_ref[rows, :] + dx
            return dgsum + _fold8(dgrow)

        dgsum = lax.fori_loop(0, ts // rc, tail, jnp.zeros((SUBLANES, D_MODEL), F32))
        dg1_ref[...] = dg1_ref[...] + jnp.sum(dgsum, axis=0, keepdims=True)

        @pl.when(i == nt - 1)
        def _():
            _reduce_acc(dwb_ref, acc_ref, K_B)

    return pl.pallas_call(
        body, name="bwd_mix_b", grid=(nt,),
        in_specs=[_rows(ts, D_A), _prev(ts, D_A), _next(ts, D_A, s), _rows(ts, D_B), _prev(ts, D_B), _next(ts, D_B, s),
                  _rows(ts, D_IN), _prev(ts, D_IN), _next(ts, D_IN, s), _rows(ts, D_A),
                  _const((K_A, D_A)), _const((K_B, D_B)), _const((D_IN, D_MODEL)), _rows(ts, D_MODEL),
                  _const((1, D_MODEL)), _rows(ts, D_MODEL)],
        out_specs=[_rows(ts, D_IN), _rows(ts, D_MODEL), _acc_out((1, D_MODEL)), _acc_out((K_B, D_B))],
        out_shape=[_sds((s, D_IN), BF16), _sds((s, D_MODEL), F32), _sds((1, D_MODEL), F32), _sds((K_B, D_B), F32)],
        scratch_shapes=[_ext_scratch(ts, D_A), _ext_scratch(ts, D_B), _ext_scratch(ts, D_B),
                        pltpu.VMEM((K_B, SUBLANES, D_B), F32)],
        compiler_params=_params(),
    )(dca, dca, dca, du, du, du, z, z, z, dab, wa, wb, w_in, x, g1, dx2)


def _matmul_tn(a, b, name):
    s, m = a.shape
    n = b.shape[1]
    tk = min(1024, s)
    nk = s // tk
    tm = 256

    def body(a_ref, b_ref, o_ref, acc_ref):
        k = pl.program_id(0)

        @pl.when(k == 0)
        def _():
            acc_ref[...] = jnp.zeros_like(acc_ref)

        for m0 in range(0, m, tm):
            acc_ref[m0:m0 + tm, :] = acc_ref[m0:m0 + tm, :] + lax.dot_general(
                a_ref[:, m0:m0 + tm], b_ref[...], _TN, preferred_element_type=F32)

        @pl.when(k == nk - 1)
        def _():
            o_ref[...] = acc_ref[...].astype(BF16)

    return pl.pallas_call(
        body, name=name, grid=(nk,),
        in_specs=[_rows(tk, m), _rows(tk, n)],
        out_specs=_acc_out((m, n)),
        out_shape=_sds((m, n), BF16),
        scratch_shapes=[pltpu.VMEM((m, n), F32)],
        compiler_params=_params(),
    )(a, b)


CHIP_RELS = ((1, 0, 0), (0, 1, 0), (1, 1, 0))
CORE_RELS = ((0, 0, 1),)
ALL_RELS = ((0, 0, 1), (0, 1, 0), (0, 1, 1), (1, 0, 0), (1, 0, 1), (1, 1, 0), (1, 1, 1))


def _chip_slot(dev):
    return 2 * dev[0] + dev[1]


def _dev_slot(dev):
    return 4 * dev[0] + 2 * dev[1] + dev[2]


def _me():
    return (lax.axis_index("x"), lax.axis_index("y"), lax.axis_index("c"))


def _peer(me, rel):
    return tuple((1 - me[a]) if rel[a] else me[a] for a in range(3))


_ANY = pl.BlockSpec(memory_space=pl.ANY)


def _all_gather(payloads):
    n_p = len(payloads)
    n_k = 1 + 2 * len(CHIP_RELS)

    def body(*refs):
        srcs = refs[:n_p]
        dsts = refs[n_p:2 * n_p]
        send_sems, recv_sems, local_sems = refs[2 * n_p:]
        me = _me()
        sibling = _peer(me, CORE_RELS[0])

        def copy(p, k, block_dev, to, from_src):
            blk = dsts[p].at[_dev_slot(block_dev)]
            return pltpu.make_async_remote_copy(
                src_ref=srcs[p] if from_src else blk, dst_ref=blk,
                send_sem=send_sems.at[n_k * p + k], recv_sem=recv_sems.at[n_k * p + k],
                device_id=to, device_id_type=MESH)

        started = []
        for p in range(n_p):
            own = pltpu.make_async_copy(srcs[p], dsts[p].at[_dev_slot(me)], local_sems.at[p])
            own.start()
            started.append(own)
        for j, rel in enumerate(CHIP_RELS):
            for p in range(n_p):
                cp = copy(p, 1 + j, me, _peer(me, rel), True)
                cp.start()
                started.append(cp)
        for p in range(n_p):
            cp = copy(p, 0, me, sibling, True)
            cp.start()
            started.append(cp)
        for j, rel in enumerate(CHIP_RELS):
            other = _peer(me, rel)
            for p in range(n_p):
                copy(p, 1 + j, other, me, False).wait_recv()
                fwd = copy(p, 4 + j, other, sibling, False)
                fwd.start()
                started.append(fwd)
        for p in range(n_p):
            copy(p, 0, sibling, me, False).wait_recv()
        for j, rel in enumerate(CHIP_RELS):
            other_sib = _peer(sibling, rel)
            for p in range(n_p):
                copy(p, 4 + j, other_sib, me, False).wait_recv()
        for cp in started[:n_p]:
            cp.wait()
        for cp in started[n_p:]:
            cp.wait_send()

    outs = pl.pallas_call(
        body, name="ag_weights",
        in_specs=[_ANY] * n_p, out_specs=[_ANY] * n_p,
        out_shape=[_sds((N_DEV,) + p.shape, p.dtype) for p in payloads],
        scratch_shapes=[pltpu.SemaphoreType.DMA((n_p * n_k,)), pltpu.SemaphoreType.DMA((n_p * n_k,)),
                        pltpu.SemaphoreType.DMA((n_p,))],
    )(*payloads)
    return list(outs)


def _gather_direct(payload, name):
    n_r = len(ALL_RELS)

    def body(src, dst, send_sems, recv_sems, local_sem):
        me = _me()
        mine = dst.at[_dev_slot(me)]
        own = pltpu.make_async_copy(src, mine, local_sem)
        own.start()
        copies = []
        for k, rel in enumerate(ALL_RELS):
            cp = pltpu.make_async_remote_copy(src_ref=src, dst_ref=mine, send_sem=send_sems.at[k],
                                              recv_sem=recv_sems.at[k], device_id=_peer(me, rel), device_id_type=MESH)
            cp.start()
            copies.append(cp)
        own.wait()
        for cp in copies:
            cp.wait()

    return pl.pallas_call(
        body, name=name, in_specs=[_ANY], out_specs=_ANY, out_shape=_sds((N_DEV,) + payload.shape, payload.dtype),
        scratch_shapes=[pltpu.SemaphoreType.DMA((n_r,)), pltpu.SemaphoreType.DMA((n_r,)), pltpu.SemaphoreType.DMA(())],
    )(payload)


def _scatter_exchange(payloads, rels, src_view, view_shapes, name):
    n_p = len(payloads)
    n_r = len(rels)

    def body(*refs):
        srcs = refs[:n_p]
        dsts = refs[n_p:2 * n_p]
        send_sems, recv_sems = refs[2 * n_p:]
        me = _me()
        copies = []
        for k, rel in enumerate(rels):
            peer = _peer(me, rel)
            for p in range(n_p):
                cp = pltpu.make_async_remote_copy(
                    src_ref=src_view(srcs[p], peer), dst_ref=dsts[p].at[k],
                    send_sem=send_sems.at[p * n_r + k], recv_sem=recv_sems.at[p * n_r + k],
                    device_id=peer, device_id_type=MESH)
                cp.start()
                copies.append(cp)
        for cp in copies:
            cp.wait()

    outs = pl.pallas_call(
        body, name=name,
        in_specs=[_ANY] * n_p, out_specs=[_ANY] * n_p,
        out_shape=[_sds((n_r,) + vs, p.dtype) for vs, p in zip(view_shapes, payloads)],
        scratch_shapes=[pltpu.SemaphoreType.DMA((n_p * n_r,)), pltpu.SemaphoreType.DMA((n_p * n_r,))],
    )(*payloads)
    return list(outs)


def _pair_sum(grads, recvd, my_core):
    n_p = len(grads)

    def body(c_ref, *refs):
        del c_ref
        for p in range(n_p):
            refs[2 * n_p + p][...] = (refs[p][...].astype(F32) + refs[n_p + p][...].astype(F32)).astype(BF16)

    def blk(g):
        return (None, None) + g.shape[2:]

    return pl.pallas_call(
        body, name="rs_pair_sum",
        grid_spec=pltpu.PrefetchScalarGridSpec(
            num_scalar_prefetch=1, grid=(N_CHIP,),
            in_specs=[pl.BlockSpec(blk(g), lambda j, c: (j, c[0], 0, 0)) for g in grads]
            + [pl.BlockSpec(blk(g), lambda j, c: (0, j, 0, 0)) for g in grads],
            out_specs=[pl.BlockSpec((None,) + g.shape[2:], lambda j, c: (j, 0, 0)) for g in grads]),
        out_shape=[_sds((N_CHIP,) + g.shape[2:], BF16) for g in grads],
        compiler_params=pltpu.CompilerParams(dimension_semantics=("arbitrary",), vmem_limit_bytes=VMEM_LIMIT),
    )(my_core, *grads, *recvd)


def _chip_sum(psums, recvd, my_chip):
    n_p = len(psums)

    def body(c_ref, *refs):
        del c_ref
        for p in range(n_p):
            acc = refs[p][...].astype(F32)
            for k in range(len(CHIP_RELS)):
                acc = acc + refs[n_p + p][k].astype(F32)
            refs[2 * n_p + p][...] = acc

    return pl.pallas_call(
        body, name="rs_chip_sum",
        grid_spec=pltpu.PrefetchScalarGridSpec(
            num_scalar_prefetch=1, grid=(1,),
            in_specs=[pl.BlockSpec((None,) + g.shape[1:], lambda i, c: (c[0], 0, 0)) for g in psums]
            + [pl.BlockSpec(r.shape, lambda i, c: (0, 0, 0)) for r in recvd],
            out_specs=[pl.BlockSpec(g.shape[1:], lambda i, c: (0, 0)) for g in psums]),
        out_shape=[_sds(g.shape[1:], F32) for g in psums],
        compiler_params=pltpu.CompilerParams(dimension_semantics=("arbitrary",), vmem_limit_bytes=VMEM_LIMIT),
    )(my_chip, *psums, *recvd)


def _sum_devices(parts):
    def body(p_ref, o_ref):
        acc = p_ref[0]
        for j in range(1, N_DEV):
            acc = acc + p_ref[j]
        o_ref[...] = acc

    return pl.pallas_call(body, name="small_grad_sum", out_shape=_sds(parts.shape[1:], F32))(parts)


def _adamw(w, g, m, v, name):
    def body(w_ref, g_ref, m_ref, v_ref, d_ref, mo_ref, vo_ref):
        gg = g_ref[...]
        mn = ADAM_B1 * m_ref[...] + (1.0 - ADAM_B1) * gg
        vn = ADAM_B2 * v_ref[...] + (1.0 - ADAM_B2) * (gg * gg)
        m_hat = mn / (1.0 - ADAM_B1 ** ADAM_STEP)
        v_hat = vn / (1.0 - ADAM_B2 ** ADAM_STEP)
        d_ref[...] = -ADAM_LR * (m_hat / (jnp.sqrt(v_hat) + ADAM_EPS) + ADAM_WD * w_ref[...])
        mo_ref[...] = mn
        vo_ref[...] = vn

    sd = _sds(w.shape, F32)
    return pl.pallas_call(body, name=name, out_shape=[sd, sd, sd],
                          compiler_params=pltpu.CompilerParams(vmem_limit_bytes=VMEM_LIMIT))(w, g, m, v)


def _local_step(x, target, g1, w_in_t, wa, wb, bb, lg, lb, w_out, g2, w_gate_t, w_up_t, wf, w_down, g3, ts):
    z, h1 = _fwd_in(x, g1, w_in_t, ts)
    x2, y, u = _fwd_mix(z, x, wa, wb, bb, lg, lb, w_out, ts)
    g0, v, h2 = _fwd_ffn_in(x2, g2, w_gate_t, w_up_t, ts)
    a, dx3, dx3b, loss, dg3 = _fwd_ffn_out(g0, v, x2, wf, w_down, g3, target, ts)
    dgc, dv, dwf = _bwd_ffn_a(dx3b, g0, v, w_down, wf, ts)
    dg0, dx2, dx2b, dg2 = _bwd_ffn_b(dgc, dv, wf, w_gate_t, w_up_t, x2, g2, dx3, ts)
    dca, du, dab, dwa, dlg, dlb, dbb = _bwd_mix_a(dx2b, w_out, z, u, wa, lg, lb, ts)
    dz, dx, dg1, dwb = _bwd_mix_b(dca, du, z, dab, wa, wb, w_in_t, x, g1, dx2, ts)
    dw_down = _matmul_tn(a, dx3b, "wgrad_down")
    dw_gate_t = _matmul_tn(dg0, h2, "wgrad_gate")
    dw_up_t = _matmul_tn(dv, h2, "wgrad_up")
    dw_out = _matmul_tn(y, dx2b, "wgrad_out")
    dw_in_t = _matmul_tn(dz, h1, "wgrad_in")
    big = dict(w_in=dw_in_t, w_out=dw_out, w_gate=dw_gate_t, w_up=dw_up_t, w_down=dw_down)
    small = dict(norm_mix_g=dg1, conv_a_w=dwa, conv_b_w=dwb, conv_b_b=dbb, ln_b_g=dlg, ln_b_b=dlb,
                 norm_ffn_g=dg2, conv_ffn_w=dwf, norm_final_g=dg3)
    return loss, dx, big, small


def _pack_small_weights(conv_a_s, conv_b_s, conv_ffn_s):
    buf = jnp.zeros((SMALL_W_ROWS, SMALL_W_COLS), F32)
    buf = buf.at[0:K_A, 0:CONV_A_COLS].set(conv_a_s)
    buf = buf.at[K_A:K_A + K_B, 0:CONV_A_COLS].set(conv_b_s)
    return buf.at[K_A + K_B:K_A + K_B + K_F, 0:W_FF_COLS].set(conv_ffn_s)


def _unpack_small_weights(full):
    def take(r0, k, w):
        return jnp.transpose(full[:, r0:r0 + k, 0:w], (1, 0, 2)).reshape(k, N_DEV * w)

    return take(0, K_A, CONV_A_COLS), take(K_A, K_B, CONV_A_COLS), take(K_A + K_B, K_F, W_FF_COLS)


def _up8(n):
    return -(-n // SUBLANES) * SUBLANES


_SMALL_LAYOUT = (
    ("norm_mix_g", 2), ("norm_ffn_g", 2), ("norm_final_g", 2), ("conv_b_b", 1), ("ln_b_g", 1), ("ln_b_b", 1),
    ("conv_a_w", K_A), ("conv_b_w", K_B), ("conv_ffn_w", K_F * (FF_PAD // SMALL_G_COLS)), ("loss", 1))
SMALL_G_ROWS = sum(_up8(n) for _, n in _SMALL_LAYOUT)


def _pack_small_grads(small, loss):
    parts = []
    for name, n_rows in _SMALL_LAYOUT:
        if name == "loss":
            part = jnp.broadcast_to(loss.reshape(1, 1), (1, SMALL_G_COLS))
        elif name == "conv_ffn_w":
            part = jnp.pad(small[name], ((0, 0), (0, FF_PAD - D_FF))).reshape(n_rows, SMALL_G_COLS)
        else:
            part = small[name].reshape(n_rows, SMALL_G_COLS)
        parts.append(jnp.pad(part, ((0, _up8(n_rows) - n_rows), (0, 0))))
    return jnp.concatenate(parts, axis=0)


def _unpack_small_grads(tot):
    out = {}
    r = 0
    for name, n_rows in _SMALL_LAYOUT:
        blk = tot[r:r + n_rows]
        r += _up8(n_rows)
        if name == "loss":
            out[name] = blk[0, 0]
        elif name == "conv_ffn_w":
            out[name] = blk.reshape(K_F, FF_PAD)[:, 0:D_FF]
        elif name in ("conv_a_w", "conv_b_w"):
            out[name] = blk
        else:
            out[name] = blk.reshape(1, n_rows * SMALL_G_COLS)
    return out


def kernel(x, norm_mix_g, w_in, conv_a_w, conv_b_w, conv_b_b, ln_b_g, ln_b_b, w_out, norm_ffn_g, w_gate, w_up, conv_ffn_w, w_down, norm_final_g, loss_target, m_norm_mix_g, m_w_in, m_conv_a_w, m_conv_b_w, m_conv_b_b, m_ln_b_g, m_ln_b_b, m_w_out, m_norm_ffn_g, m_w_gate, m_w_up, m_conv_ffn_w, m_w_down, m_norm_final_g, v_norm_mix_g, v_w_in, v_conv_a_w, v_conv_b_w, v_conv_b_b, v_ln_b_g, v_ln_b_b, v_w_out, v_norm_ffn_g, v_w_gate, v_w_up, v_conv_ffn_w, v_w_down, v_norm_final_g):
    ix, iy, ic = lax.axis_index("x"), lax.axis_index("y"), lax.axis_index("c")
    my_chip = (2 * ix + iy).astype(jnp.int32)
    my_core = ic.astype(jnp.int32)
    my_dev = 2 * my_chip + my_core

    weights = dict(norm_mix_g=norm_mix_g, w_in=w_in, conv_a_w=conv_a_w, conv_b_w=conv_b_w, conv_b_b=conv_b_b,
                   ln_b_g=ln_b_g, ln_b_b=ln_b_b, w_out=w_out, norm_ffn_g=norm_ffn_g, w_gate=w_gate, w_up=w_up,
                   conv_ffn_w=conv_ffn_w, w_down=w_down, norm_final_g=norm_final_g)
    m_in = dict(norm_mix_g=m_norm_mix_g, w_in=m_w_in, conv_a_w=m_conv_a_w, conv_b_w=m_conv_b_w, conv_b_b=m_conv_b_b,
                ln_b_g=m_ln_b_g, ln_b_b=m_ln_b_b, w_out=m_w_out, norm_ffn_g=m_norm_ffn_g, w_gate=m_w_gate,
                w_up=m_w_up, conv_ffn_w=m_conv_ffn_w, w_down=m_w_down, norm_final_g=m_norm_final_g)
    v_in = dict(norm_mix_g=v_norm_mix_g, w_in=v_w_in, conv_a_w=v_conv_a_w, conv_b_w=v_conv_b_w, conv_b_b=v_conv_b_b,
                ln_b_g=v_ln_b_g, ln_b_b=v_ln_b_b, w_out=v_w_out, norm_ffn_g=v_norm_ffn_g, w_gate=v_w_gate,
                w_up=v_w_up, conv_ffn_w=v_conv_ffn_w, w_down=v_w_down, norm_final_g=v_norm_final_g)
    order = list(weights)
    big_names = ("w_in", "w_gate", "w_up", "w_out", "w_down")
    transposed = ("w_in", "w_gate", "w_up")

    def shard2d(name, a):
        if name in transposed:
            return jnp.swapaxes(a[0], 0, 1)
        return a.reshape(1, a.shape[0]) if a.ndim == 1 else a.reshape(a.shape[-2:])

    def unshard2d(name, a2, like):
        if name in transposed:
            return jnp.swapaxes(a2, 0, 1)[None]
        return a2.reshape(like.shape)

    payloads = [shard2d(n, weights[n]).astype(BF16) for n in big_names]
    payloads.append(_pack_small_weights(conv_a_w[0], conv_b_w[0], conv_ffn_w[0]))
    gathered = _all_gather(payloads)
    w_in_t, w_gate_t, w_up_t, w_out_f, w_down_f = [g.reshape(N_DEV * g.shape[1], D_MODEL) for g in gathered[:5]]
    wa_f, wb_f, wf_f = _unpack_small_weights(gathered[5])

    loss, dx, big, small = _local_step(
        x[0], loss_target[0], norm_mix_g, w_in_t, wa_f, wb_f, conv_b_b, ln_b_g, ln_b_b, w_out_f, norm_ffn_g,
        w_gate_t, w_up_t, wf_f, w_down_f, norm_final_g.reshape(1, D_MODEL), SEQ_TILE)

    parts = [big[n].reshape(N_CHIP, 2, big[n].shape[0] // N_DEV, D_MODEL) for n in big_names]
    shard_shapes = [p.shape[2:] for p in parts]
    from_sibling = _scatter_exchange(parts, CORE_RELS, lambda ref, peer: ref.at[:, peer[2]],
                                     [(N_CHIP,) + s for s in shard_shapes], "rs_cores")
    pair = _pair_sum(parts, from_sibling, my_core.reshape(1))
    from_chips = _scatter_exchange(pair, CHIP_RELS, lambda ref, peer: ref.at[_chip_slot(peer)], shard_shapes, "rs_chips")
    gsum = dict(zip(big_names, _chip_sum(pair, from_chips, my_chip.reshape(1))))
    sparts = _gather_direct(_pack_small_grads(small, loss), "ag_small_grads")
    stot = _unpack_small_grads(_sum_devices(sparts))

    grads2d = dict(
        norm_mix_g=stot["norm_mix_g"],
        conv_a_w=lax.dynamic_slice(stot["conv_a_w"], (0, my_dev * CONV_A_COLS), (K_A, CONV_A_COLS)),
        conv_b_w=lax.dynamic_slice(stot["conv_b_w"], (0, my_dev * CONV_A_COLS), (K_B, CONV_A_COLS)),
        conv_b_b=stot["conv_b_b"], ln_b_g=stot["ln_b_g"], ln_b_b=stot["ln_b_b"],
        norm_ffn_g=stot["norm_ffn_g"],
        conv_ffn_w=lax.dynamic_slice(stot["conv_ffn_w"], (0, my_dev * W_FF_COLS), (K_F, W_FF_COLS)),
        norm_final_g=stot["norm_final_g"],
        **gsum,
    )

    g_out, d_out, m_out, v_out = [], [], [], []
    for name in order:
        w = weights[name]
        g2d = grads2d[name]
        d, mn, vn = _adamw(shard2d(name, w), g2d, shard2d(name, m_in[name]), shard2d(name, v_in[name]), "adamw_" + name)
        g_out.append(unshard2d(name, g2d, w))
        d_out.append(unshard2d(name, d, w))
        m_out.append(unshard2d(name, mn, w))
        v_out.append(unshard2d(name, vn, w))

    return (stot["loss"], dx[None], *g_out, *d_out, *m_out, *v_out)
```

```python
import jax
import jax.numpy as jnp
from jax import lax
from jax.experimental import pallas as pl
from jax.experimental.pallas import tpu as pltpu

F32 = jnp.float32
BF16 = jnp.bfloat16

D_MODEL = 1024
D_A = 512
D_B = 512
D_IN = 3 * D_A + 2 * D_B
D_FF = 2816
K_A = 3
K_B = 31
K_F = 3
RMS_EPS = 1e-6
LN_EPS = 1e-5

ADAM_LR = 0.001
ADAM_B1 = 0.9
ADAM_B2 = 0.999
ADAM_EPS = 1e-08
ADAM_WD = 0.01
ADAM_STEP = 10

N_DEV = 8
N_CHIP = 4
LANES = 128
SUBLANES = 8
HALO = 16
ROW_CHUNK = 64
SEQ_TILE = 512
VMEM_LIMIT = 56 * 1024 * 1024

MESH = pl.DeviceIdType.MESH

W_FF_COLS = D_FF // N_DEV
CONV_A_COLS = D_A // N_DEV
SMALL_W_ROWS = 40
SMALL_W_COLS = 384
SMALL_G_COLS = 512
FF_PAD = 3072


def _rows(ts, c):
    return pl.BlockSpec((ts, c), lambda i: (i, 0))


def _const(shape):
    return pl.BlockSpec(shape, lambda i: (0,) * len(shape), pipeline_mode=pl.Buffered(1))


def _acc_out(shape):
    return pl.BlockSpec(shape, lambda i: (0,) * len(shape))


def _prev(ts, c):
    return pl.BlockSpec((HALO, c), lambda i: (jnp.maximum(i * (ts // HALO) - 1, 0), 0))


def _next(ts, c, s):
    last = s // HALO - 1
    return pl.BlockSpec((HALO, c), lambda i: (jnp.minimum((i + 1) * (ts // HALO), last), 0))


def _params():
    return pltpu.CompilerParams(dimension_semantics=("arbitrary",), vmem_limit_bytes=VMEM_LIMIT)


def _sds(shape, dtype):
    return jax.ShapeDtypeStruct(shape, dtype)


def _sigmoid(v):
    return 1.0 / (1.0 + jnp.exp(-v))


def _conv_block(ext_ref, w_ref, r0, rc, l0, k_taps, transposed):
    acc = None
    for k in range(k_taps):
        d = (k_taps // 2 - k) if transposed else (k - k_taps // 2)
        term = ext_ref[l0 // LANES, pl.ds(r0 + HALO + d, rc), :] * w_ref[k:k + 1, l0:l0 + LANES]
        acc = term if acc is None else acc + term
    return acc


def _conv_wgrad_block(acc_ref, dout, ext_ref, r0, rc, l0, k_taps):
    for k in range(k_taps):
        prod = dout * ext_ref[l0 // LANES, pl.ds(r0 + HALO + k - k_taps // 2, rc), :]
        part = prod.reshape(rc // SUBLANES, SUBLANES, LANES).sum(axis=0)
        acc_ref[k, :, l0:l0 + LANES] = acc_ref[k, :, l0:l0 + LANES] + part


def _reduce_acc(out_ref, acc_ref, k_taps):
    for k in range(k_taps):
        out_ref[k:k + 1, :] = jnp.sum(acc_ref[k], axis=0, keepdims=True)


def _fold8(v):
    rc, c = v.shape
    return v.reshape(rc // SUBLANES, SUBLANES, c).sum(axis=0)


def _ext_scratch(ts, c):
    return pltpu.VMEM((c // LANES, ts + 2 * HALO, LANES), F32)


def _put_rows(ext_ref, r0, rc, val):
    for q in range(val.shape[1] // LANES):
        ext_ref[q, pl.ds(r0 + HALO, rc), :] = val[:, q * LANES:(q + 1) * LANES]


def _fill_halo(ext_ref, vals_prev, vals_next, ts, first, last):
    for q in range(vals_prev.shape[1] // LANES):
        cols = slice(q * LANES, (q + 1) * LANES)
        ext_ref[q, 0:HALO, :] = jnp.where(first, 0.0, vals_prev[:, cols])
        ext_ref[q, HALO + ts:HALO + ts + HALO, :] = jnp.where(last, 0.0, vals_next[:, cols])


def _rms_bwd_rows(dh, xf, g):
    r = lax.rsqrt(jnp.mean(xf * xf, axis=-1, keepdims=True) + RMS_EPS)
    xhat = xf * r
    dxh = dh * g
    dx = r * (dxh - xhat * jnp.mean(dxh * xhat, axis=-1, keepdims=True))
    return dx, dh * xhat


_NT = (((1,), (1,)), ((), ()))
_TN = (((0,), (0,)), ((), ()))


def _fwd_in(x, g1, w_in_t, ts, exchange=None):
    s = x.shape[0]

    def body(x_ref, g_ref, w_ref, z_ref, h_ref):
        xf = x_ref[...]
        r = lax.rsqrt(jnp.mean(xf * xf, axis=-1, keepdims=True) + RMS_EPS)
        h = (xf * r * g_ref[...]).astype(BF16)
        h_ref[...] = h
        for n0 in range(0, D_IN, 512):
            z_ref[:, n0:n0 + 512] = lax.dot_general(h, w_ref[n0:n0 + 512, :], _NT,
                                                    preferred_element_type=F32).astype(BF16)

    return _call(
        body, name="fwd_in", grid=(s // ts,),
        in_specs=[_rows(ts, D_MODEL), _const((1, D_MODEL)), _const((D_IN, D_MODEL))],
        out_specs=[_rows(ts, D_IN), _rows(ts, D_MODEL)],
        out_shape=[_sds((s, D_IN), BF16), _sds((s, D_MODEL), BF16)],
        scratch_shapes=[], args=(x, g1, w_in_t), exchange=exchange, forward_step=(s // ts) // 2)


def _p_u0(z_ref, rows):
    a_h = z_ref[rows, 0:D_A].astype(F32)
    a_c = z_ref[rows, 2 * D_A:3 * D_A].astype(F32)
    b_v = z_ref[rows, 3 * D_A:3 * D_A + D_B].astype(F32)
    b_g = z_ref[rows, 3 * D_A + D_B:D_IN].astype(F32)
    return a_c * a_h, b_v * _sigmoid(b_g)


def _layernorm_rows(u_blocks):
    tot = None
    for ub in u_blocks:
        sm = jnp.sum(ub, axis=-1, keepdims=True)
        tot = sm if tot is None else tot + sm
    mu = tot * (1.0 / D_B)
    var = None
    for ub in u_blocks:
        sq = jnp.sum((ub - mu) * (ub - mu), axis=-1, keepdims=True)
        var = sq if var is None else var + sq
    rstd = lax.rsqrt(var * (1.0 / D_B) + LN_EPS)
    return mu, rstd


def _fwd_mix(z, x, wa, wb, bb, lg, lb, w_out, ts, exchange=None):
    s = x.shape[0]
    nt = s // ts
    rc = min(ROW_CHUNK, ts)

    def body(z_ref, zp_ref, zn_ref, x_ref, wa_ref, wb_ref, bb_ref, lg_ref, lb_ref, wo_ref,
             x2_ref, y_ref, u_ref, pe_ref, ue_ref):
        i = pl.program_id(0)
        pp, up = _p_u0(zp_ref, slice(None))
        pn, un = _p_u0(zn_ref, slice(None))
        _fill_halo(pe_ref, pp, pn, ts, i == 0, i == nt - 1)
        _fill_halo(ue_ref, up, un, ts, i == 0, i == nt - 1)

        def fill(j, carry):
            r0 = pl.multiple_of(j * rc, rc)
            p, u0 = _p_u0(z_ref, pl.ds(r0, rc))
            _put_rows(pe_ref, r0, rc, p)
            _put_rows(ue_ref, r0, rc, u0)
            return carry

        lax.fori_loop(0, ts // rc, fill, 0)

        def main(j, carry):
            r0 = pl.multiple_of(j * rc, rc)
            rows = pl.ds(r0, rc)
            for l0 in range(0, D_A, LANES):
                ca = _conv_block(pe_ref, wa_ref, r0, rc, l0, K_A, False)
                a_b = z_ref[rows, D_A + l0:D_A + l0 + LANES].astype(F32)
                y_ref[rows, l0:l0 + LANES] = (a_b * ca).astype(BF16)
            ubs = []
            for l0 in range(0, D_B, LANES):
                ub = _conv_block(ue_ref, wb_ref, r0, rc, l0, K_B, False) + bb_ref[:, l0:l0 + LANES]
                u_ref[rows, l0:l0 + LANES] = ub
                ubs.append(ub)
            mu, rstd = _layernorm_rows(ubs)
            for q, l0 in enumerate(range(0, D_B, LANES)):
                t = (ubs[q] - mu) * rstd * lg_ref[:, l0:l0 + LANES] + lb_ref[:, l0:l0 + LANES]
                y_ref[rows, D_A + l0:D_A + l0 + LANES] = (t * _sigmoid(t)).astype(BF16)
            return carry

        lax.fori_loop(0, ts // rc, main, 0)
        for n0 in range(0, D_MODEL, 512):
            x2_ref[:, n0:n0 + 512] = x_ref[:, n0:n0 + 512] + jnp.dot(
                y_ref[...], wo_ref[:, n0:n0 + 512], preferred_element_type=F32)

    return _call(
        body, name="fwd_mix", grid=(nt,),
        in_specs=[_rows(ts, D_IN), _prev(ts, D_IN), _next(ts, D_IN, s), _rows(ts, D_MODEL),
                  _const((K_A, D_A)), _const((K_B, D_B)), _const((1, D_B)), _const((1, D_B)), _const((1, D_B)),
                  _const((D_MODEL, D_MODEL))],
        out_specs=[_rows(ts, D_MODEL), _rows(ts, D_MODEL), _rows(ts, D_B)],
        out_shape=[_sds((s, D_MODEL), F32), _sds((s, D_MODEL), BF16), _sds((s, D_B), F32)],
        scratch_shapes=[_ext_scratch(ts, D_A), _ext_scratch(ts, D_B)],
        args=(z, z, z, x, wa, wb, bb, lg, lb, w_out), exchange=exchange, forward_step=nt - 1 - nt // 8)


def _fwd_ffn_in(x2, g2, w_gate_t, w_up_t, ts):
    s = x2.shape[0]
    half = D_FF // 2

    def body(x_ref, g_ref, wg_ref, wu_ref, g0_ref, v_ref, h_ref):
        xf = x_ref[...]
        r = lax.rsqrt(jnp.mean(xf * xf, axis=-1, keepdims=True) + RMS_EPS)
        h = (xf * r * g_ref[...]).astype(BF16)
        h_ref[...] = h
        for n0 in range(0, D_FF, half):
            g0_ref[:, n0:n0 + half] = lax.dot_general(h, wg_ref[n0:n0 + half, :], _NT,
                                                      preferred_element_type=F32).astype(BF16)
            v_ref[:, n0:n0 + half] = lax.dot_general(h, wu_ref[n0:n0 + half, :], _NT,
                                                     preferred_element_type=F32).astype(BF16)

    return pl.pallas_call(
        body, name="fwd_ffn_in", grid=(s // ts,),
        in_specs=[_rows(ts, D_MODEL), _const((1, D_MODEL)), _const((D_FF, D_MODEL)), _const((D_FF, D_MODEL))],
        out_specs=[_rows(ts, D_FF), _rows(ts, D_FF), _rows(ts, D_MODEL)],
        out_shape=[_sds((s, D_FF), BF16), _sds((s, D_FF), BF16), _sds((s, D_MODEL), BF16)],
        compiler_params=_params(),
    )(x2, g2, w_gate_t, w_up_t)


def _fwd_ffn_out(g0, v, x2, wf, w_down, g3, target, ts):
    s = x2.shape[0]
    nt = s // ts
    rc = min(ROW_CHUNK, ts)

    def body(g0_ref, gp_ref, gn_ref, v_ref, x2_ref, wf_ref, wd_ref, g3_ref, t_ref,
             a_ref, dx3_ref, dx3b_ref, loss_ref, dg3_ref, ge_ref):
        i = pl.program_id(0)
        _fill_halo(ge_ref, gp_ref[...].astype(F32), gn_ref[...].astype(F32), ts, i == 0, i == nt - 1)

        def fill(j, carry):
            r0 = pl.multiple_of(j * rc, rc)
            _put_rows(ge_ref, r0, rc, g0_ref[pl.ds(r0, rc), :].astype(F32))
            return carry

        lax.fori_loop(0, ts // rc, fill, 0)

        def act(j, carry):
            r0 = pl.multiple_of(j * rc, rc)
            rows = pl.ds(r0, rc)
            for l0 in range(0, D_FF, LANES):
                g = _conv_block(ge_ref, wf_ref, r0, rc, l0, K_F, False)
                vv = v_ref[rows, l0:l0 + LANES].astype(F32)
                a_ref[rows, l0:l0 + LANES] = (g * _sigmoid(g) * vv).astype(BF16)
            return carry

        lax.fori_loop(0, ts // rc, act, 0)
        for n0 in range(0, D_MODEL, 512):
            dx3_ref[:, n0:n0 + 512] = x2_ref[:, n0:n0 + 512] + jnp.dot(
                a_ref[...], wd_ref[:, n0:n0 + 512], preferred_element_type=F32)

        @pl.when(i == 0)
        def _():
            loss_ref[...] = jnp.zeros_like(loss_ref)
            dg3_ref[...] = jnp.zeros_like(dg3_ref)

        def tail(j, carry):
            lsum, dgsum = carry
            r0 = pl.multiple_of(j * rc, rc)
            rows = pl.ds(r0, rc)
            x3 = dx3_ref[rows, :]
            r = lax.rsqrt(jnp.mean(x3 * x3, axis=-1, keepdims=True) + RMS_EPS)
            xhat = x3 * r
            diff = xhat * g3_ref[...] - t_ref[rows, :]
            dout = diff * (1.0 / D_MODEL)
            dxh = dout * g3_ref[...]
            dx3 = r * (dxh - xhat * jnp.mean(dxh * xhat, axis=-1, keepdims=True))
            dx3_ref[rows, :] = dx3
            dx3b_ref[rows, :] = dx3.astype(BF16)
            lsum = lsum + _fold8(diff * diff)
            dgsum = dgsum + _fold8(dout * xhat)
            return lsum, dgsum

        zero = jnp.zeros((SUBLANES, D_MODEL), F32)
        lsum, dgsum = lax.fori_loop(0, ts // rc, tail, (zero, zero))
        loss_ref[...] = loss_ref[...] + (0.5 / D_MODEL) * jnp.sum(lsum, keepdims=True)
        dg3_ref[...] = dg3_ref[...] + jnp.sum(dgsum, axis=0, keepdims=True)

    return pl.pallas_call(
        body, name="fwd_ffn_out", grid=(nt,),
        in_specs=[_rows(ts, D_FF), _prev(ts, D_FF), _next(ts, D_FF, s), _rows(ts, D_FF), _rows(ts, D_MODEL),
                  _const((K_F, D_FF)), _const((D_FF, D_MODEL)), _const((1, D_MODEL)), _rows(ts, D_MODEL)],
        out_specs=[_rows(ts, D_FF), _rows(ts, D_MODEL), _rows(ts, D_MODEL), _acc_out((1, 1)), _acc_out((1, D_MODEL))],
        out_shape=[_sds((s, D_FF), BF16), _sds((s, D_MODEL), F32), _sds((s, D_MODEL), BF16),
                   _sds((1, 1), F32), _sds((1, D_MODEL), F32)],
        scratch_shapes=[_ext_scratch(ts, D_FF)],
        compiler_params=_params(),
    )(g0, g0, g0, v, x2, wf, w_down, g3, target)


def _bwd_ffn_a(dx3b, g0, v, w_down, wf, ts):
    s = dx3b.shape[0]
    nt = s // ts
    rc = min(ROW_CHUNK, ts)
    half = D_FF // 2

    def body(dx_ref, g0_ref, gp_ref, gn_ref, v_ref, wd_ref, wf_ref, dg_ref, dv_ref, dwf_ref, ge_ref, da_ref, acc_ref):
        i = pl.program_id(0)

        @pl.when(i == 0)
        def _():
            acc_ref[...] = jnp.zeros_like(acc_ref)

        _fill_halo(ge_ref, gp_ref[...].astype(F32), gn_ref[...].astype(F32), ts, i == 0, i == nt - 1)

        def fill(j, carry):
            r0 = pl.multiple_of(j * rc, rc)
            _put_rows(ge_ref, r0, rc, g0_ref[pl.ds(r0, rc), :].astype(F32))
            return carry

        lax.fori_loop(0, ts // rc, fill, 0)
        for n0 in range(0, D_FF, half):
            da_ref[:, n0:n0 + half] = lax.dot_general(dx_ref[...], wd_ref[n0:n0 + half, :], _NT,
                                                      preferred_element_type=F32)

        def main(j, carry):
            r0 = pl.multiple_of(j * rc, rc)
            rows = pl.ds(r0, rc)
            for l0 in range(0, D_FF, LANES):
                g = _conv_block(ge_ref, wf_ref, r0, rc, l0, K_F, False)
                sg = _sigmoid(g)
                da = da_ref[rows, l0:l0 + LANES]
                vv = v_ref[rows, l0:l0 + LANES].astype(F32)
                dv_ref[rows, l0:l0 + LANES] = (da * (g * sg)).astype(BF16)
                dgg = da * vv * (sg * (1.0 + g * (1.0 - sg)))
                dg_ref[rows, l0:l0 + LANES] = dgg.astype(BF16)
                _conv_wgrad_block(acc_ref, dgg, ge_ref, r0, rc, l0, K_F)
            return carry

        lax.fori_loop(0, ts // rc, main, 0)

        @pl.when(i == nt - 1)
        def _():
            _reduce_acc(dwf_ref, acc_ref, K_F)

    return pl.pallas_call(
        body, name="bwd_ffn_a", grid=(nt,),
        in_specs=[_rows(ts, D_MODEL), _rows(ts, D_FF), _prev(ts, D_FF), _next(ts, D_FF, s), _rows(ts, D_FF),
                  _const((D_FF, D_MODEL)), _const((K_F, D_FF))],
        out_specs=[_rows(ts, D_FF), _rows(ts, D_FF), _acc_out((K_F, D_FF))],
        out_shape=[_sds((s, D_FF), BF16), _sds((s, D_FF), BF16), _sds((K_F, D_FF), F32)],
        scratch_shapes=[_ext_scratch(ts, D_FF), pltpu.VMEM((ts, D_FF), F32),
                        pltpu.VMEM((K_F, SUBLANES, D_FF), F32)],
        compiler_params=_params(),
    )(dx3b, g0, g0, g0, v, w_down, wf)


def _bwd_ffn_b(dg, dv, wf, w_gate, w_up, x2, g2, dx3, ts):
    s = x2.shape[0]
    nt = s // ts
    rc = min(ROW_CHUNK, ts)

    def body(dg_ref, dgp_ref, dgn_ref, dv_ref, wf_ref, wg_ref, wu_ref, x2_ref, g2_ref, dx3_ref,
             dg0_ref, dx2_ref, dx2b_ref, dgn2_ref, dge_ref):
        i = pl.program_id(0)

        @pl.when(i == 0)
        def _():
            dgn2_ref[...] = jnp.zeros_like(dgn2_ref)

        _fill_halo(dge_ref, dgp_ref[...].astype(F32), dgn_ref[...].astype(F32), ts, i == 0, i == nt - 1)

        def fill(j, carry):
            r0 = pl.multiple_of(j * rc, rc)
            _put_rows(dge_ref, r0, rc, dg_ref[pl.ds(r0, rc), :].astype(F32))
            return carry

        lax.fori_loop(0, ts // rc, fill, 0)

        def main(j, carry):
            r0 = pl.multiple_of(j * rc, rc)
            for l0 in range(0, D_FF, LANES):
                dg0_ref[pl.ds(r0, rc), l0:l0 + LANES] = _conv_block(dge_ref, wf_ref, r0, rc, l0, K_F, True).astype(BF16)
            return carry

        lax.fori_loop(0, ts // rc, main, 0)
        for n0 in range(0, D_MODEL, 512):
            dx2_ref[:, n0:n0 + 512] = (
                jnp.dot(dg0_ref[...], wg_ref[:, n0:n0 + 512], preferred_element_type=F32)
                + jnp.dot(dv_ref[...], wu_ref[:, n0:n0 + 512], preferred_element_type=F32))

        def tail(j, dgsum):
            r0 = pl.multiple_of(j * rc, rc)
            rows = pl.ds(r0, rc)
            dx, dgrow = _rms_bwd_rows(dx2_ref[rows, :], x2_ref[rows, :], g2_ref[...])
            dx2 = dx3_ref[rows, :] + dx
            dx2_ref[rows, :] = dx2
            dx2b_ref[rows, :] = dx2.astype(BF16)
            return dgsum + _fold8(dgrow)

        dgsum = lax.fori_loop(0, ts // rc, tail, jnp.zeros((SUBLANES, D_MODEL), F32))
        dgn2_ref[...] = dgn2_ref[...] + jnp.sum(dgsum, axis=0, keepdims=True)

    return pl.pallas_call(
        body, name="bwd_ffn_b", grid=(nt,),
        in_specs=[_rows(ts, D_FF), _prev(ts, D_FF), _next(ts, D_FF, s), _rows(ts, D_FF), _const((K_F, D_FF)),
                  _const((D_FF, D_MODEL)), _const((D_FF, D_MODEL)), _rows(ts, D_MODEL), _const((1, D_MODEL)),
                  _rows(ts, D_MODEL)],
        out_specs=[_rows(ts, D_FF), _rows(ts, D_MODEL), _rows(ts, D_MODEL), _acc_out((1, D_MODEL))],
        out_shape=[_sds((s, D_FF), BF16), _sds((s, D_MODEL), F32), _sds((s, D_MODEL), BF16), _sds((1, D_MODEL), F32)],
        scratch_shapes=[_ext_scratch(ts, D_FF)],
        compiler_params=_params(),
    )(dg, dg, dg, dv, wf, w_gate, w_up, x2, g2, dx3)


def _bwd_mix_a(dx2b, w_out, z, u, wa, lg, lb, ts, exchange=None):
    s = dx2b.shape[0]
    nt = s // ts
    rc = min(ROW_CHUNK, ts)

    def body(dx_ref, wo_ref, z_ref, zp_ref, zn_ref, u_ref, wa_ref, lg_ref, lb_ref,
             dca_ref, du_ref, dab_ref, dwa_ref, dlg_ref, dlb_ref, dbb_ref, pe_ref, dy_ref, acc_ref, sacc_ref):
        i = pl.program_id(0)

        @pl.when(i == 0)
        def _():
            acc_ref[...] = jnp.zeros_like(acc_ref)
            sacc_ref[...] = jnp.zeros_like(sacc_ref)

        pp, _ = _p_u0(zp_ref, slice(None))
        pn, _ = _p_u0(zn_ref, slice(None))
        _fill_halo(pe_ref, pp, pn, ts, i == 0, i == nt - 1)

        def fill(j, carry):
            r0 = pl.multiple_of(j * rc, rc)
            rows = pl.ds(r0, rc)
            _put_rows(pe_ref, r0, rc, z_ref[rows, 2 * D_A:3 * D_A].astype(F32) * z_ref[rows, 0:D_A].astype(F32))
            return carry

        lax.fori_loop(0, ts // rc, fill, 0)
        for n0 in range(0, D_MODEL, 512):
            dy_ref[:, n0:n0 + 512] = lax.dot_general(dx_ref[...], wo_ref[n0:n0 + 512, :], _NT,
                                                     preferred_element_type=F32)

        def main(j, carry):
            r0 = pl.multiple_of(j * rc, rc)
            rows = pl.ds(r0, rc)
            for l0 in range(0, D_A, LANES):
                ca = _conv_block(pe_ref, wa_ref, r0, rc, l0, K_A, False)
                a_b = z_ref[rows, D_A + l0:D_A + l0 + LANES].astype(F32)
                dya = dy_ref[rows, l0:l0 + LANES]
                dab_ref[rows, l0:l0 + LANES] = (dya * ca).astype(BF16)
                dca = dya * a_b
                dca_ref[rows, l0:l0 + LANES] = dca
                _conv_wgrad_block(acc_ref, dca, pe_ref, r0, rc, l0, K_A)
            ubs = [u_ref[rows, l0:l0 + LANES] for l0 in range(0, D_B, LANES)]
            mu, rstd = _layernorm_rows(ubs)
            ns, dns = [], []
            m1 = None
            m2 = None
            for q, l0 in enumerate(range(0, D_B, LANES)):
                n = (ubs[q] - mu) * rstd
                lgq = lg_ref[:, l0:l0 + LANES]
                t = n * lgq + lb_ref[:, l0:l0 + LANES]
                sg = _sigmoid(t)
                dt = dy_ref[rows, D_A + l0:D_A + l0 + LANES] * (sg * (1.0 + t * (1.0 - sg)))
                dn = dt * lgq
                ns.append(n)
                dns.append(dn)
                s1 = jnp.sum(dn, axis=-1, keepdims=True)
                s2 = jnp.sum(dn * n, axis=-1, keepdims=True)
                m1 = s1 if m1 is None else m1 + s1
                m2 = s2 if m2 is None else m2 + s2
                sacc_ref[0, :, l0:l0 + LANES] = sacc_ref[0, :, l0:l0 + LANES] + _fold8(dt * n)
                sacc_ref[1, :, l0:l0 + LANES] = sacc_ref[1, :, l0:l0 + LANES] + _fold8(dt)
            m1 = m1 * (1.0 / D_B)
            m2 = m2 * (1.0 / D_B)
            for q, l0 in enumerate(range(0, D_B, LANES)):
                du = rstd * (dns[q] - m1 - ns[q] * m2)
                du_ref[rows, l0:l0 + LANES] = du
                sacc_ref[2, :, l0:l0 + LANES] = sacc_ref[2, :, l0:l0 + LANES] + _fold8(du)
            return carry

        lax.fori_loop(0, ts // rc, main, 0)

        @pl.when(i == nt - 1)
        def _():
            _reduce_acc(dwa_ref, acc_ref, K_A)
            dlg_ref[...] = jnp.sum(sacc_ref[0], axis=0, keepdims=True)
            dlb_ref[...] = jnp.sum(sacc_ref[1], axis=0, keepdims=True)
            dbb_ref[...] = jnp.sum(sacc_ref[2], axis=0, keepdims=True)

    return _call(
        body, name="bwd_mix_a", grid=(nt,),
        in_specs=[_rows(ts, D_MODEL), _const((D_MODEL, D_MODEL)), _rows(ts, D_IN), _prev(ts, D_IN), _next(ts, D_IN, s),
                  _rows(ts, D_B), _const((K_A, D_A)), _const((1, D_B)), _const((1, D_B))],
        out_specs=[_rows(ts, D_A), _rows(ts, D_B), _rows(ts, D_A), _acc_out((K_A, D_A)),
                   _acc_out((1, D_B)), _acc_out((1, D_B)), _acc_out((1, D_B))],
        out_shape=[_sds((s, D_A), F32), _sds((s, D_B), F32), _sds((s, D_A), BF16), _sds((K_A, D_A), F32),
                   _sds((1, D_B), F32), _sds((1, D_B), F32), _sds((1, D_B), F32)],
        scratch_shapes=[_ext_scratch(ts, D_A), pltpu.VMEM((ts, D_MODEL), F32),
                        pltpu.VMEM((K_A, SUBLANES, D_A), F32), pltpu.VMEM((3, SUBLANES, D_B), F32)],
        args=(dx2b, w_out, z, z, z, u, wa, lg, lb), exchange=exchange)


def _bwd_mix_b(dca, du, z, dab, wa, wb, w_in, x, g1, dx2, ts, exchange=None):
    s = x.shape[0]
    nt = s // ts
    rc = min(ROW_CHUNK, ts)

    def body(dca_ref, dcap_ref, dcan_ref, du_ref, dup_ref, dun_ref, z_ref, zp_ref, zn_ref, dab_ref,
             wa_ref, wb_ref, wi_ref, x_ref, g1_ref, dx2_ref,
             dz_ref, dx_ref, dg1_ref, dwb_ref, dcae_ref, due_ref, ue_ref, acc_ref):
        i = pl.program_id(0)

        @pl.when(i == 0)
        def _():
            acc_ref[...] = jnp.zeros_like(acc_ref)
            dg1_ref[...] = jnp.zeros_like(dg1_ref)

        first = i == 0
        last = i == nt - 1
        _fill_halo(dcae_ref, dcap_ref[...], dcan_ref[...], ts, first, last)
        _fill_halo(due_ref, dup_ref[...], dun_ref[...], ts, first, last)
        _, up = _p_u0(zp_ref, slice(None))
        _, un = _p_u0(zn_ref, slice(None))
        _fill_halo(ue_ref, up, un, ts, first, last)

        def fill(j, carry):
            r0 = pl.multiple_of(j * rc, rc)
            rows = pl.ds(r0, rc)
            _put_rows(dcae_ref, r0, rc, dca_ref[rows, :])
            _put_rows(due_ref, r0, rc, du_ref[rows, :])
            b_v = z_ref[rows, 3 * D_A:3 * D_A + D_B].astype(F32)
            b_g = z_ref[rows, 3 * D_A + D_B:D_IN].astype(F32)
            _put_rows(ue_ref, r0, rc, b_v * _sigmoid(b_g))
            return carry

        lax.fori_loop(0, ts // rc, fill, 0)

        def main(j, carry):
            r0 = pl.multiple_of(j * rc, rc)
            rows = pl.ds(r0, rc)
            for l0 in range(0, D_A, LANES):
                dp = _conv_block(dcae_ref, wa_ref, r0, rc, l0, K_A, True)
                a_h = z_ref[rows, l0:l0 + LANES].astype(F32)
                a_c = z_ref[rows, 2 * D_A + l0:2 * D_A + l0 + LANES].astype(F32)
                dz_ref[rows, l0:l0 + LANES] = (dp * a_c).astype(BF16)
                dz_ref[rows, D_A + l0:D_A + l0 + LANES] = dab_ref[rows, l0:l0 + LANES]
                dz_ref[rows, 2 * D_A + l0:2 * D_A + l0 + LANES] = (dp * a_h).astype(BF16)
            for l0 in range(0, D_B, LANES):
                du0 = _conv_block(due_ref, wb_ref, r0, rc, l0, K_B, True)
                b_v = z_ref[rows, 3 * D_A + l0:3 * D_A + l0 + LANES].astype(F32)
                b_g = z_ref[rows, 3 * D_A + D_B + l0:3 * D_A + D_B + l0 + LANES].astype(F32)
                sg = _sigmoid(b_g)
                dz_ref[rows, 3 * D_A + l0:3 * D_A + l0 + LANES] = (du0 * sg).astype(BF16)
                dz_ref[rows, 3 * D_A + D_B + l0:3 * D_A + D_B + l0 + LANES] = (du0 * b_v * (sg * (1.0 - sg))).astype(BF16)
                _conv_wgrad_block(acc_ref, du_ref[rows, l0:l0 + LANES], ue_ref, r0, rc, l0, K_B)
            return carry

        lax.fori_loop(0, ts // rc, main, 0)
        for n0 in range(0, D_MODEL, 512):
            dx_ref[:, n0:n0 + 512] = jnp.dot(dz_ref[...], wi_ref[:, n0:n0 + 512], preferred_element_type=F32)

        def tail(j, dgsum):
            r0 = pl.multiple_of(j * rc, rc)
            rows = pl.ds(r0, rc)
            dx, dgrow = _rms_bwd_rows(dx_ref[rows, :], x_ref[rows, :], g1_ref[...])
            dx_ref[rows, :] = dx2_ref[rows, :] + dx
            return dgsum + _fold8(dgrow)

        dgsum = lax.fori_loop(0, ts // rc, tail, jnp.zeros((SUBLANES, D_MODEL), F32))
        dg1_ref[...] = dg1_ref[...] + jnp.sum(dgsum, axis=0, keepdims=True)

        @pl.when(i == nt - 1)
        def _():
            _reduce_acc(dwb_ref, acc_ref, K_B)

    return _call(
        body, name="bwd_mix_b", grid=(nt,),
        in_specs=[_rows(ts, D_A), _prev(ts, D_A), _next(ts, D_A, s), _rows(ts, D_B), _prev(ts, D_B), _next(ts, D_B, s),
                  _rows(ts, D_IN), _prev(ts, D_IN), _next(ts, D_IN, s), _rows(ts, D_A),
                  _const((K_A, D_A)), _const((K_B, D_B)), _const((D_IN, D_MODEL)), _rows(ts, D_MODEL),
                  _const((1, D_MODEL)), _rows(ts, D_MODEL)],
        out_specs=[_rows(ts, D_IN), _rows(ts, D_MODEL), _acc_out((1, D_MODEL)), _acc_out((K_B, D_B))],
        out_shape=[_sds((s, D_IN), BF16), _sds((s, D_MODEL), F32), _sds((1, D_MODEL), F32), _sds((K_B, D_B), F32)],
        scratch_shapes=[_ext_scratch(ts, D_A), _ext_scratch(ts, D_B), _ext_scratch(ts, D_B),
                        pltpu.VMEM((K_B, SUBLANES, D_B), F32)],
        args=(dca, dca, dca, du, du, du, z, z, z, dab, wa, wb, w_in, x, g1, dx2), exchange=exchange)


def _matmul_tn(a, b, name):
    s, m = a.shape
    n = b.shape[1]
    tk = min(1024, s)
    nk = s // tk
    tm = 256

    def body(a_ref, b_ref, o_ref, acc_ref):
        k = pl.program_id(0)

        @pl.when(k == 0)
        def _():
            acc_ref[...] = jnp.zeros_like(acc_ref)

        for m0 in range(0, m, tm):
            acc_ref[m0:m0 + tm, :] = acc_ref[m0:m0 + tm, :] + lax.dot_general(
                a_ref[:, m0:m0 + tm], b_ref[...], _TN, preferred_element_type=F32)

        @pl.when(k == nk - 1)
        def _():
            o_ref[...] = acc_ref[...].astype(BF16)

    return pl.pallas_call(
        body, name=name, grid=(nk,),
        in_specs=[_rows(tk, m), _rows(tk, n)],
        out_specs=_acc_out((m, n)),
        out_shape=_sds((m, n), BF16),
        scratch_shapes=[pltpu.VMEM((m, n), F32)],
        compiler_params=_params(),
    )(a, b)


CHIP_RELS = ((1, 0, 0), (0, 1, 0), (1, 1, 0))
CORE_RELS = ((0, 0, 1),)
ALL_RELS = ((0, 0, 1), (0, 1, 0), (0, 1, 1), (1, 0, 0), (1, 0, 1), (1, 1, 0), (1, 1, 1))


def _chip_slot(dev):
    return 2 * dev[0] + dev[1]


def _dev_slot(dev):
    return 4 * dev[0] + 2 * dev[1] + dev[2]


def _me():
    return (lax.axis_index("x"), lax.axis_index("y"), lax.axis_index("c"))


def _peer(me, rel):
    return tuple((1 - me[a]) if rel[a] else me[a] for a in range(3))


_ANY = pl.BlockSpec(memory_space=pl.ANY)


class _Exchange:
    def __init__(self, inputs, out_shape, scratch, start, finish, forward=None):
        self.inputs, self.out_shape, self.scratch = list(inputs), list(out_shape), list(scratch)
        self.start, self.finish, self.forward = start, finish, forward


def _all_gather(payloads):
    n_p = len(payloads)
    n_k = 1 + 2 * len(CHIP_RELS)

    def copy(srcs, dsts, sems, p, k, block_dev, to, from_src):
        blk = dsts[p].at[_dev_slot(block_dev)]
        return pltpu.make_async_remote_copy(
            src_ref=srcs[p] if from_src else blk, dst_ref=blk,
            send_sem=sems[0].at[n_k * p + k], recv_sem=sems[1].at[n_k * p + k], device_id=to, device_id_type=MESH)

    def own_copy(srcs, dsts, sems, p):
        return pltpu.make_async_copy(srcs[p], dsts[p].at[_dev_slot(_me())], sems[2].at[p])

    def start(srcs, dsts, sems):
        me = _me()
        for p in range(n_p):
            own_copy(srcs, dsts, sems, p).start()
        for j, rel in enumerate(CHIP_RELS):
            for p in range(n_p):
                copy(srcs, dsts, sems, p, 1 + j, me, _peer(me, rel), True).start()
        for p in range(n_p):
            copy(srcs, dsts, sems, p, 0, me, _peer(me, CORE_RELS[0]), True).start()

    def forward(srcs, dsts, sems):
        me = _me()
        sibling = _peer(me, CORE_RELS[0])
        for j, rel in enumerate(CHIP_RELS):
            other = _peer(me, rel)
            for p in range(n_p):
                copy(srcs, dsts, sems, p, 1 + j, other, me, False).wait_recv()
                copy(srcs, dsts, sems, p, 4 + j, other, sibling, False).start()

    def finish(srcs, dsts, sems):
        me = _me()
        sibling = _peer(me, CORE_RELS[0])
        for p in range(n_p):
            copy(srcs, dsts, sems, p, 0, sibling, me, False).wait_recv()
        for j, rel in enumerate(CHIP_RELS):
            for p in range(n_p):
                copy(srcs, dsts, sems, p, 4 + j, _peer(sibling, rel), me, False).wait_recv()
        for p in range(n_p):
            own_copy(srcs, dsts, sems, p).wait()
            copy(srcs, dsts, sems, p, 0, me, sibling, True).wait_send()
            for j, rel in enumerate(CHIP_RELS):
                copy(srcs, dsts, sems, p, 1 + j, me, _peer(me, rel), True).wait_send()
                copy(srcs, dsts, sems, p, 4 + j, _peer(me, rel), sibling, False).wait_send()

    return _Exchange(
        payloads, [_sds((N_DEV,) + p.shape, p.dtype) for p in payloads],
        [pltpu.SemaphoreType.DMA((n_p * n_k,)), pltpu.SemaphoreType.DMA((n_p * n_k,)), pltpu.SemaphoreType.DMA((n_p,))],
        start, finish, forward)


def _gather_direct(payload):
    n_r = len(ALL_RELS)

    def copies(srcs, dsts, sems):
        me = _me()
        mine = dsts[0].at[_dev_slot(me)]
        own = pltpu.make_async_copy(srcs[0], mine, sems[2].at[0])
        remote = [pltpu.make_async_remote_copy(src_ref=srcs[0], dst_ref=mine, send_sem=sems[0].at[k], recv_sem=sems[1].at[k],
                                               device_id=_peer(me, rel), device_id_type=MESH)
                  for k, rel in enumerate(ALL_RELS)]
        return [own] + remote

    def start(srcs, dsts, sems):
        for cp in copies(srcs, dsts, sems):
            cp.start()

    def finish(srcs, dsts, sems):
        for cp in copies(srcs, dsts, sems):
            cp.wait()

    return _Exchange([payload], [_sds((N_DEV,) + payload.shape, payload.dtype)],
                     [pltpu.SemaphoreType.DMA((n_r,)), pltpu.SemaphoreType.DMA((n_r,)), pltpu.SemaphoreType.DMA((1,))],
                     start, finish)


def _scatter_exchange(payloads, rels, src_view, view_shapes):
    n_p = len(payloads)
    n_r = len(rels)

    def copies(srcs, dsts, sems):
        me = _me()
        out = []
        for k, rel in enumerate(rels):
            peer = _peer(me, rel)
            for p in range(n_p):
                out.append(pltpu.make_async_remote_copy(
                    src_ref=src_view(srcs[p], peer), dst_ref=dsts[p].at[k],
                    send_sem=sems[0].at[p * n_r + k], recv_sem=sems[1].at[p * n_r + k],
                    device_id=peer, device_id_type=MESH))
        return out

    def start(srcs, dsts, sems):
        for cp in copies(srcs, dsts, sems):
            cp.start()

    def finish(srcs, dsts, sems):
        for cp in copies(srcs, dsts, sems):
            cp.wait()

    return _Exchange(payloads, [_sds((n_r,) + vs, p.dtype) for vs, p in zip(view_shapes, payloads)],
                     [pltpu.SemaphoreType.DMA((n_p * n_r,)), pltpu.SemaphoreType.DMA((n_p * n_r,))], start, finish)


def _split_refs(refs, sizes):
    out, at = [], 0
    for n in sizes:
        out.append(refs[at:at + n])
        at += n
    return out


def _run_exchanges(name, exchanges):
    n_in = [len(e.inputs) for e in exchanges]
    n_out = [len(e.out_shape) for e in exchanges]
    n_sc = [len(e.scratch) for e in exchanges]

    def body(*refs):
        ins, outs, scs = _split_refs(refs, [sum(n_in), sum(n_out), sum(n_sc)])
        parts = list(zip(exchanges, _split_refs(ins, n_in), _split_refs(outs, n_out), _split_refs(scs, n_sc)))
        for e, i, o, s in parts:
            e.start(i, o, s)
        for e, i, o, s in parts:
            if e.forward is not None:
                e.forward(i, o, s)
        for e, i, o, s in parts:
            e.finish(i, o, s)

    outs = pl.pallas_call(
        body, name=name, in_specs=[_ANY] * sum(n_in), out_specs=[_ANY] * sum(n_out),
        out_shape=[sd for e in exchanges for sd in e.out_shape],
        scratch_shapes=[sc for e in exchanges for sc in e.scratch],
    )(*[a for e in exchanges for a in e.inputs])
    return _split_refs(list(outs), n_out)


def _call(body, *, name, grid, in_specs, out_specs, out_shape, scratch_shapes, args, exchange=None, forward_step=None):
    n_in, n_out, n_sc = len(in_specs), len(out_specs), len(scratch_shapes)
    if exchange is None:
        outs = pl.pallas_call(body, name=name, grid=grid, in_specs=in_specs, out_specs=out_specs, out_shape=out_shape,
                              scratch_shapes=scratch_shapes, compiler_params=_params())(*args)
        return list(outs), []
    e = exchange
    sizes = [n_in, len(e.inputs), n_out, len(e.out_shape), n_sc, len(e.scratch)]
    last = grid[0] - 1

    def wrapped(*refs):
        a, ei, o, eo, sc, es = _split_refs(refs, sizes)
        i = pl.program_id(0)

        @pl.when(i == 0)
        def _():
            e.start(ei, eo, es)

        if e.forward is not None:
            @pl.when(i == forward_step)
            def _():
                e.forward(ei, eo, es)

        body(*a, *o, *sc)

        @pl.when(i == last)
        def _():
            e.finish(ei, eo, es)

    outs = pl.pallas_call(
        wrapped, name=name, grid=grid,
        in_specs=list(in_specs) + [_ANY] * len(e.inputs), out_specs=list(out_specs) + [_ANY] * len(e.out_shape),
        out_shape=list(out_shape) + e.out_shape, scratch_shapes=list(scratch_shapes) + e.scratch,
        compiler_params=_params(),
    )(*args, *e.inputs)
    outs = list(outs)
    return outs[:n_out], outs[n_out:]


def _pair_sum(grads, recvd, my_core, name):
    n_p = len(grads)

    def body(c_ref, *refs):
        del c_ref
        for p in range(n_p):
            refs[2 * n_p + p][...] = (refs[p][...].astype(F32) + refs[n_p + p][...].astype(F32)).astype(BF16)

    def blk(g):
        return (None, None) + g.shape[2:]

    return pl.pallas_call(
        body, name=name,
        grid_spec=pltpu.PrefetchScalarGridSpec(
            num_scalar_prefetch=1, grid=(N_CHIP,),
            in_specs=[pl.BlockSpec(blk(g), lambda j, c: (j, c[0], 0, 0)) for g in grads]
            + [pl.BlockSpec(blk(g), lambda j, c: (0, j, 0, 0)) for g in grads],
            out_specs=[pl.BlockSpec((None,) + g.shape[2:], lambda j, c: (j, 0, 0)) for g in grads]),
        out_shape=[_sds((N_CHIP,) + g.shape[2:], BF16) for g in grads],
        compiler_params=pltpu.CompilerParams(dimension_semantics=("arbitrary",), vmem_limit_bytes=VMEM_LIMIT),
    )(my_core, *grads, *recvd)


def _chip_sum(psums, recvd, my_chip, name):
    n_p = len(psums)

    def body(c_ref, *refs):
        del c_ref
        for p in range(n_p):
            acc = refs[p][...].astype(F32)
            for k in range(len(CHIP_RELS)):
                acc = acc + refs[n_p + p][k].astype(F32)
            refs[2 * n_p + p][...] = acc

    return pl.pallas_call(
        body, name=name,
        grid_spec=pltpu.PrefetchScalarGridSpec(
            num_scalar_prefetch=1, grid=(1,),
            in_specs=[pl.BlockSpec((None,) + g.shape[1:], lambda i, c: (c[0], 0, 0)) for g in psums]
            + [pl.BlockSpec(r.shape, lambda i, c: (0, 0, 0)) for r in recvd],
            out_specs=[pl.BlockSpec(g.shape[1:], lambda i, c: (0, 0)) for g in psums]),
        out_shape=[_sds(g.shape[1:], F32) for g in psums],
        compiler_params=pltpu.CompilerParams(dimension_semantics=("arbitrary",), vmem_limit_bytes=VMEM_LIMIT),
    )(my_chip, *psums, *recvd)


def _sum_devices(parts):
    def body(p_ref, o_ref):
        acc = p_ref[0]
        for j in range(1, N_DEV):
            acc = acc + p_ref[j]
        o_ref[...] = acc

    return pl.pallas_call(body, name="small_grad_sum", out_shape=_sds(parts.shape[1:], F32))(parts)


def _adamw(w, g, m, v, name):
    def body(w_ref, g_ref, m_ref, v_ref, d_ref, mo_ref, vo_ref):
        gg = g_ref[...]
        mn = ADAM_B1 * m_ref[...] + (1.0 - ADAM_B1) * gg
        vn = ADAM_B2 * v_ref[...] + (1.0 - ADAM_B2) * (gg * gg)
        m_hat = mn / (1.0 - ADAM_B1 ** ADAM_STEP)
        v_hat = vn / (1.0 - ADAM_B2 ** ADAM_STEP)
        d_ref[...] = -ADAM_LR * (m_hat / (jnp.sqrt(v_hat) + ADAM_EPS) + ADAM_WD * w_ref[...])
        mo_ref[...] = mn
        vo_ref[...] = vn

    sd = _sds(w.shape, F32)
    return pl.pallas_call(body, name=name, out_shape=[sd, sd, sd],
                          compiler_params=pltpu.CompilerParams(vmem_limit_bytes=VMEM_LIMIT))(w, g, m, v)


class _Mesh:
    def __init__(self, shards, my_chip, my_core):
        self.shards, self.my_chip, self.my_core = shards, my_chip.reshape(1), my_core.reshape(1)

    def gather(self, names):
        return _all_gather([self.shards[n] for n in names])

    @staticmethod
    def whole(gathered):
        return gathered.reshape(N_DEV * gathered.shape[1], gathered.shape[2])

    @staticmethod
    def by_device(grads):
        return [g.reshape(N_CHIP, 2, g.shape[0] // N_DEV, g.shape[1]) for g in grads]

    @staticmethod
    def to_sibling(parts):
        return _scatter_exchange(parts, CORE_RELS, lambda ref, peer: ref.at[:, peer[2]],
                                 [(N_CHIP,) + p.shape[2:] for p in parts])

    @staticmethod
    def to_chips(pair):
        return _scatter_exchange(pair, CHIP_RELS, lambda ref, peer: ref.at[_chip_slot(peer)], [p.shape[1:] for p in pair])


def _step(x, target, g1, w_in_t, wa, wb, bb, lg, lb, w_out, g2, w_gate_t, w_up_t, wf, w_down, g3, ts, mesh=None):
    (z, h1), got = _fwd_in(x, g1, w_in_t, ts, exchange=mesh and mesh.gather(["w_out"]))
    if mesh:
        w_out = mesh.whole(got[0])
    (x2, y, u), got = _fwd_mix(z, x, wa, wb, bb, lg, lb, w_out, ts,
                               exchange=mesh and mesh.gather(["w_gate", "w_up", "w_down"]))
    if mesh:
        w_gate_t, w_up_t, w_down = [mesh.whole(g) for g in got]
    g0, v, h2 = _fwd_ffn_in(x2, g2, w_gate_t, w_up_t, ts)
    a, dx3, dx3b, loss, dg3 = _fwd_ffn_out(g0, v, x2, wf, w_down, g3, target, ts)
    dgc, dv, dwf = _bwd_ffn_a(dx3b, g0, v, w_down, wf, ts)
    dg0, dx2, dx2b, dg2 = _bwd_ffn_b(dgc, dv, wf, w_gate_t, w_up_t, x2, g2, dx3, ts)
    first = dict(w_down=_matmul_tn(a, dx3b, "wgrad_down"), w_gate=_matmul_tn(dg0, h2, "wgrad_gate"),
                 w_up=_matmul_tn(dv, h2, "wgrad_up"), w_out=_matmul_tn(y, dx2b, "wgrad_out"))
    if mesh:
        parts = mesh.by_device(list(first.values()))
    (dca, du, dab, dwa, dlg, dlb, dbb), got = _bwd_mix_a(dx2b, w_out, z, u, wa, lg, lb, ts,
                                                         exchange=mesh and mesh.to_sibling(parts))
    if mesh:
        pair = _pair_sum(parts, got, mesh.my_core, "rs_pair_sum_first")
    (dz, dx, dg1, dwb), got = _bwd_mix_b(dca, du, z, dab, wa, wb, w_in_t, x, g1, dx2, ts,
                                         exchange=mesh and mesh.to_chips(pair))
    dw_in_t = _matmul_tn(dz, h1, "wgrad_in")
    small = dict(norm_mix_g=dg1, conv_a_w=dwa, conv_b_w=dwb, conv_b_b=dbb, ln_b_g=dlg, ln_b_b=dlb,
                 norm_ffn_g=dg2, conv_ffn_w=dwf, norm_final_g=dg3)
    if not mesh:
        return loss, dx, dict(w_in=dw_in_t, **first), small
    big = dict(zip(first, _chip_sum(pair, got, mesh.my_chip, "rs_chip_sum_first")))
    parts = mesh.by_device([dw_in_t])
    (got,) = _run_exchanges("rs_cores_last", [mesh.to_sibling(parts)])
    pair = _pair_sum(parts, got, mesh.my_core, "rs_pair_sum_last")
    got, (every,) = _run_exchanges("rs_chips_last", [mesh.to_chips(pair), _gather_direct(_pack_small_grads(small, loss))])
    (big["w_in"],) = _chip_sum(pair, got, mesh.my_chip, "rs_chip_sum_last")
    return None, dx, big, _unpack_small_grads(_sum_devices(every))


def _pack_small_weights(conv_a_s, conv_b_s, conv_ffn_s):
    buf = jnp.zeros((SMALL_W_ROWS, SMALL_W_COLS), F32)
    buf = buf.at[0:K_A, 0:CONV_A_COLS].set(conv_a_s)
    buf = buf.at[K_A:K_A + K_B, 0:CONV_A_COLS].set(conv_b_s)
    return buf.at[K_A + K_B:K_A + K_B + K_F, 0:W_FF_COLS].set(conv_ffn_s)


def _unpack_small_weights(full):
    def take(r0, k, w):
        return jnp.transpose(full[:, r0:r0 + k, 0:w], (1, 0, 2)).reshape(k, N_DEV * w)

    return take(0, K_A, CONV_A_COLS), take(K_A, K_B, CONV_A_COLS), take(K_A + K_B, K_F, W_FF_COLS)


def _up8(n):
    return -(-n // SUBLANES) * SUBLANES


_SMALL_LAYOUT = (
    ("norm_mix_g", 2), ("norm_ffn_g", 2), ("norm_final_g", 2), ("conv_b_b", 1), ("ln_b_g", 1), ("ln_b_b", 1),
    ("conv_a_w", K_A), ("conv_b_w", K_B), ("conv_ffn_w", K_F * (FF_PAD // SMALL_G_COLS)), ("loss", 1))
SMALL_G_ROWS = sum(_up8(n) for _, n in _SMALL_LAYOUT)


def _pack_small_grads(small, loss):
    parts = []
    for name, n_rows in _SMALL_LAYOUT:
        if name == "loss":
            part = jnp.broadcast_to(loss.reshape(1, 1), (1, SMALL_G_COLS))
        elif name == "conv_ffn_w":
            part = jnp.pad(small[name], ((0, 0), (0, FF_PAD - D_FF))).reshape(n_rows, SMALL_G_COLS)
        else:
            part = small[name].reshape(n_rows, SMALL_G_COLS)
        parts.append(jnp.pad(part, ((0, _up8(n_rows) - n_rows), (0, 0))))
    return jnp.concatenate(parts, axis=0)


def _unpack_small_grads(tot):
    out = {}
    r = 0
    for name, n_rows in _SMALL_LAYOUT:
        blk = tot[r:r + n_rows]
        r += _up8(n_rows)
        if name == "loss":
            out[name] = blk[0, 0]
        elif name == "conv_ffn_w":
            out[name] = blk.reshape(K_F, FF_PAD)[:, 0:D_FF]
        elif name in ("conv_a_w", "conv_b_w"):
            out[name] = blk
        else:
            out[name] = blk.reshape(1, n_rows * SMALL_G_COLS)
    return out


def kernel(x, norm_mix_g, w_in, conv_a_w, conv_b_w, conv_b_b, ln_b_g, ln_b_b, w_out, norm_ffn_g, w_gate, w_up, conv_ffn_w, w_down, norm_final_g, loss_target, m_norm_mix_g, m_w_in, m_conv_a_w, m_conv_b_w, m_conv_b_b, m_ln_b_g, m_ln_b_b, m_w_out, m_norm_ffn_g, m_w_gate, m_w_up, m_conv_ffn_w, m_w_down, m_norm_final_g, v_norm_mix_g, v_w_in, v_conv_a_w, v_conv_b_w, v_conv_b_b, v_ln_b_g, v_ln_b_b, v_w_out, v_norm_ffn_g, v_w_gate, v_w_up, v_conv_ffn_w, v_w_down, v_norm_final_g):
    ix, iy, ic = lax.axis_index("x"), lax.axis_index("y"), lax.axis_index("c")
    my_chip = (2 * ix + iy).astype(jnp.int32)
    my_core = ic.astype(jnp.int32)
    my_dev = 2 * my_chip + my_core

    weights = dict(norm_mix_g=norm_mix_g, w_in=w_in, conv_a_w=conv_a_w, conv_b_w=conv_b_w, conv_b_b=conv_b_b,
                   ln_b_g=ln_b_g, ln_b_b=ln_b_b, w_out=w_out, norm_ffn_g=norm_ffn_g, w_gate=w_gate, w_up=w_up,
                   conv_ffn_w=conv_ffn_w, w_down=w_down, norm_final_g=norm_final_g)
    m_in = dict(norm_mix_g=m_norm_mix_g, w_in=m_w_in, conv_a_w=m_conv_a_w, conv_b_w=m_conv_b_w, conv_b_b=m_conv_b_b,
                ln_b_g=m_ln_b_g, ln_b_b=m_ln_b_b, w_out=m_w_out, norm_ffn_g=m_norm_ffn_g, w_gate=m_w_gate,
                w_up=m_w_up, conv_ffn_w=m_conv_ffn_w, w_down=m_w_down, norm_final_g=m_norm_final_g)
    v_in = dict(norm_mix_g=v_norm_mix_g, w_in=v_w_in, conv_a_w=v_conv_a_w, conv_b_w=v_conv_b_w, conv_b_b=v_conv_b_b,
                ln_b_g=v_ln_b_g, ln_b_b=v_ln_b_b, w_out=v_w_out, norm_ffn_g=v_norm_ffn_g, w_gate=v_w_gate,
                w_up=v_w_up, conv_ffn_w=v_conv_ffn_w, w_down=v_w_down, norm_final_g=v_norm_final_g)
    order = list(weights)
    big_names = ("w_in", "w_gate", "w_up", "w_out", "w_down")
    transposed = ("w_in", "w_gate", "w_up")

    def shard2d(name, a):
        if name in transposed:
            return jnp.swapaxes(a[0], 0, 1)
        return a.reshape(1, a.shape[0]) if a.ndim == 1 else a.reshape(a.shape[-2:])

    def unshard2d(name, a2, like):
        if name in transposed:
            return jnp.swapaxes(a2, 0, 1)[None]
        return a2.reshape(like.shape)

    mesh = _Mesh({n: shard2d(n, weights[n]).astype(BF16) for n in big_names}, my_chip, my_core)
    gathered, = _run_exchanges("ag_first", [_all_gather(
        [mesh.shards["w_in"], _pack_small_weights(conv_a_w[0], conv_b_w[0], conv_ffn_w[0])])])
    w_in_t = mesh.whole(gathered[0])
    wa_f, wb_f, wf_f = _unpack_small_weights(gathered[1])

    _, dx, gsum, stot = _step(
        x[0], loss_target[0], norm_mix_g, w_in_t, wa_f, wb_f, conv_b_b, ln_b_g, ln_b_b, None, norm_ffn_g,
        None, None, wf_f, None, norm_final_g.reshape(1, D_MODEL), SEQ_TILE, mesh)

    grads2d = dict(
        norm_mix_g=stot["norm_mix_g"],
        conv_a_w=lax.dynamic_slice(stot["conv_a_w"], (0, my_dev * CONV_A_COLS), (K_A, CONV_A_COLS)),
        conv_b_w=lax.dynamic_slice(stot["conv_b_w"], (0, my_dev * CONV_A_COLS), (K_B, CONV_A_COLS)),
        conv_b_b=stot["conv_b_b"], ln_b_g=stot["ln_b_g"], ln_b_b=stot["ln_b_b"],
        norm_ffn_g=stot["norm_ffn_g"],
        conv_ffn_w=lax.dynamic_slice(stot["conv_ffn_w"], (0, my_dev * W_FF_COLS), (K_F, W_FF_COLS)),
        norm_final_g=stot["norm_final_g"],
        **gsum,
    )

    g_out, d_out, m_out, v_out = [], [], [], []
    for name in order:
        w = weights[name]
        g2d = grads2d[name]
        d, mn, vn = _adamw(shard2d(name, w), g2d, shard2d(name, m_in[name]), shard2d(name, v_in[name]), "adamw_" + name)
        g_out.append(unshard2d(name, g2d, w))
        d_out.append(unshard2d(name, d, w))
        m_out.append(unshard2d(name, mn, w))
        v_out.append(unshard2d(name, vn, w))

    return (stot["loss"], dx[None], *g_out, *d_out, *m_out, *v_out)
```

```python
import jax
import jax.numpy as jnp
from jax import lax
from jax.experimental import pallas as pl
from jax.experimental.pallas import tpu as pltpu

F32 = jnp.float32
BF16 = jnp.bfloat16

D_MODEL = 1024
D_A = 512
D_B = 512
D_IN = 3 * D_A + 2 * D_B
D_FF = 2816
K_A = 3
K_B = 31
K_F = 3
RMS_EPS = 1e-6
LN_EPS = 1e-5

ADAM_LR = 0.001
ADAM_B1 = 0.9
ADAM_B2 = 0.999
ADAM_EPS = 1e-08
ADAM_WD = 0.01
ADAM_STEP = 10

N_DEV = 8
N_CHIP = 4
LANES = 128
SUBLANES = 8
HALO = 16
ROW_CHUNK = 64
SEQ_TILE = 512
SKEW_TILE = 256
VMEM_LIMIT = 56 * 1024 * 1024

MESH = pl.DeviceIdType.MESH

W_FF_COLS = D_FF // N_DEV
CONV_A_COLS = D_A // N_DEV
SMALL_W_ROWS = 40
SMALL_W_COLS = 384
SMALL_G_COLS = 512
FF_PAD = 3072


def _rows(ts, c):
    return pl.BlockSpec((ts, c), lambda i: (i, 0))


def _const(shape):
    return pl.BlockSpec(shape, lambda i: (0,) * len(shape), pipeline_mode=pl.Buffered(1))


def _acc_out(shape):
    return pl.BlockSpec(shape, lambda i: (0,) * len(shape))


def _rows_at(ts, c, tile):
    return pl.BlockSpec((ts, c), lambda i: (tile(i), 0))


def _prev_at(ts, c, tile):
    return pl.BlockSpec((HALO, c), lambda i: (jnp.maximum(tile(i) * (ts // HALO) - 1, 0), 0))


def _next_at(ts, c, s, tile):
    last = s // HALO - 1
    return pl.BlockSpec((HALO, c), lambda i: (jnp.minimum((tile(i) + 1) * (ts // HALO), last), 0))


def _prev(ts, c):
    return _prev_at(ts, c, lambda i: i)


def _next(ts, c, s):
    return _next_at(ts, c, s, lambda i: i)


MXU_COLS = 256
MXU_ROWS = 128


def _col_pieces(n):
    return [(c0, min(MXU_COLS, n - c0)) for c0 in range(0, n, MXU_COLS)]


def _matmul_pieces(terms, out_ref, k_parts):
    m, n = out_ref.shape
    steps = []
    for lhs_ref, w_ref in terms:
        tiles = lhs_ref.shape[1] // MXU_COLS
        cuts = [MXU_COLS * (tiles * j // k_parts) for j in range(k_parts)] + [lhs_ref.shape[1]]
        steps += [(lhs_ref, w_ref, cuts[j], cuts[j + 1]) for j in range(k_parts)]

    def piece(m0, n0, width, step):
        lhs_ref, w_ref, k0, k1 = steps[step]
        part = jnp.dot(lhs_ref[m0:m0 + MXU_ROWS, k0:k1], w_ref[k0:k1, n0:n0 + width], preferred_element_type=F32)
        if step:
            part = part + out_ref[m0:m0 + MXU_ROWS, n0:n0 + width]
        out_ref[m0:m0 + MXU_ROWS, n0:n0 + width] = part

    return [(piece, (m0, n0, w, j)) for j in range(len(steps)) for n0, w in _col_pieces(n) for m0 in range(0, m, MXU_ROWS)]


def _interleaved(vector_units, matmul_pieces):
    n_u, n_p = len(vector_units), len(matmul_pieces)
    done = 0
    for k, (unit, args) in enumerate(vector_units):
        while done < n_p and done * n_u <= k * n_p:
            matmul_pieces[done][0](*matmul_pieces[done][1])
            done += 1
        unit(*args)
    for fn, args in matmul_pieces[done:]:
        fn(*args)


def _params():
    return pltpu.CompilerParams(dimension_semantics=("arbitrary",), vmem_limit_bytes=VMEM_LIMIT)


def _sds(shape, dtype):
    return jax.ShapeDtypeStruct(shape, dtype)


def _sigmoid(v):
    return 0.5 * jnp.tanh(0.5 * v) + 0.5


def _conv_block(ext_ref, w_ref, r0, rc, l0, k_taps, transposed):
    acc = None
    for k in range(k_taps):
        d = (k_taps // 2 - k) if transposed else (k - k_taps // 2)
        term = ext_ref[l0 // LANES, pl.ds(r0 + HALO + d, rc), :] * w_ref[k:k + 1, l0:l0 + LANES]
        acc = term if acc is None else acc + term
    return acc


def _conv_wgrad_block(acc_ref, dout, ext_ref, r0, rc, l0, k_taps, scale=None):
    for k in range(k_taps):
        prod = dout * ext_ref[l0 // LANES, pl.ds(r0 + HALO + k - k_taps // 2, rc), :]
        part = prod.reshape(rc // SUBLANES, SUBLANES, LANES).sum(axis=0)
        if scale is not None:
            part = part * scale
        acc_ref[k, :, l0:l0 + LANES] = acc_ref[k, :, l0:l0 + LANES] + part


def _reduce_acc(out_ref, acc_ref, k_taps):
    for k in range(k_taps):
        out_ref[k:k + 1, :] = jnp.sum(acc_ref[k], axis=0, keepdims=True)


def _fold8(v):
    rc, c = v.shape
    return v.reshape(rc // SUBLANES, SUBLANES, c).sum(axis=0)


def _ext_scratch(ts, c):
    return pltpu.VMEM((c // LANES, ts + 2 * HALO, LANES), F32)


def _put_rows(ext_ref, r0, rc, val):
    for q in range(val.shape[1] // LANES):
        ext_ref[q, pl.ds(r0 + HALO, rc), :] = val[:, q * LANES:(q + 1) * LANES]


def _fill_halo(ext_ref, vals_prev, vals_next, ts, first, last):
    for q in range(vals_prev.shape[1] // LANES):
        cols = slice(q * LANES, (q + 1) * LANES)
        ext_ref[q, 0:HALO, :] = jnp.where(first, 0.0, vals_prev[:, cols])
        ext_ref[q, HALO + ts:HALO + ts + HALO, :] = jnp.where(last, 0.0, vals_next[:, cols])


def _rms_bwd_rows(dh, xf, g):
    r = lax.rsqrt(jnp.mean(xf * xf, axis=-1, keepdims=True) + RMS_EPS)
    xhat = xf * r
    dxh = dh * g
    dx = r * (dxh - xhat * jnp.mean(dxh * xhat, axis=-1, keepdims=True))
    return dx, dh * xhat


_NT = (((1,), (1,)), ((), ()))
_TN = (((0,), (0,)), ((), ()))


def _fwd_in(x, g1, w_in_t, ts, exchange=None):
    s = x.shape[0]

    def body(x_ref, g_ref, w_ref, z_ref, h_ref):
        xf = x_ref[...]
        r = lax.rsqrt(jnp.mean(xf * xf, axis=-1, keepdims=True) + RMS_EPS)
        h = (xf * r * g_ref[...]).astype(BF16)
        h_ref[...] = h
        for n0 in range(0, D_IN, 512):
            z_ref[:, n0:n0 + 512] = lax.dot_general(h, w_ref[n0:n0 + 512, :], _NT,
                                                    preferred_element_type=F32).astype(BF16)

    return _call(
        body, name="fwd_in", grid=(s // ts,),
        in_specs=[_rows(ts, D_MODEL), _const((1, D_MODEL)), _const((D_IN, D_MODEL))],
        out_specs=[_rows(ts, D_IN), _rows(ts, D_MODEL)],
        out_shape=[_sds((s, D_IN), BF16), _sds((s, D_MODEL), BF16)],
        scratch_shapes=[], args=(x, g1, w_in_t), exchange=exchange, forward_step=(s // ts) * 3 // 4)


def _p_u0(z_ref, rows):
    a_h = z_ref[rows, 0:D_A].astype(F32)
    a_c = z_ref[rows, 2 * D_A:3 * D_A].astype(F32)
    b_v = z_ref[rows, 3 * D_A:3 * D_A + D_B].astype(F32)
    b_g = z_ref[rows, 3 * D_A + D_B:D_IN].astype(F32)
    return a_c * a_h, b_v * _sigmoid(b_g)


def _layernorm_rows(u_blocks):
    tot = None
    for ub in u_blocks:
        sm = jnp.sum(ub, axis=-1, keepdims=True)
        tot = sm if tot is None else tot + sm
    mu = tot * (1.0 / D_B)
    var = None
    for ub in u_blocks:
        sq = jnp.sum((ub - mu) * (ub - mu), axis=-1, keepdims=True)
        var = sq if var is None else var + sq
    rstd = lax.rsqrt(var * (1.0 / D_B) + LN_EPS)
    return mu, rstd


def _fwd_mix(z, x, wa, wb, bb, lg, lb, w_out, ts, exchange=None):
    s = x.shape[0]
    nt = s // ts
    rc = min(ROW_CHUNK, ts)

    def body(z_ref, zp_ref, zn_ref, x_ref, wa_ref, wb_ref, bb_ref, lg_ref, lb_ref, wo_ref,
             x2_ref, y_ref, u_ref, pe_ref, ue_ref):
        i = pl.program_id(0)
        pp, up = _p_u0(zp_ref, slice(None))
        pn, un = _p_u0(zn_ref, slice(None))
        _fill_halo(pe_ref, pp, pn, ts, i == 0, i == nt - 1)
        _fill_halo(ue_ref, up, un, ts, i == 0, i == nt - 1)

        def fill(j, carry):
            r0 = pl.multiple_of(j * rc, rc)
            p, u0 = _p_u0(z_ref, pl.ds(r0, rc))
            _put_rows(pe_ref, r0, rc, p)
            _put_rows(ue_ref, r0, rc, u0)
            return carry

        lax.fori_loop(0, ts // rc, fill, 0)

        def main(j, carry):
            r0 = pl.multiple_of(j * rc, rc)
            rows = pl.ds(r0, rc)
            for l0 in range(0, D_A, LANES):
                ca = _conv_block(pe_ref, wa_ref, r0, rc, l0, K_A, False)
                a_b = z_ref[rows, D_A + l0:D_A + l0 + LANES].astype(F32)
                y_ref[rows, l0:l0 + LANES] = (a_b * ca).astype(BF16)
            ubs = []
            for l0 in range(0, D_B, LANES):
                ub = _conv_block(ue_ref, wb_ref, r0, rc, l0, K_B, False) + bb_ref[:, l0:l0 + LANES]
                u_ref[rows, l0:l0 + LANES] = ub
                ubs.append(ub)
            mu, rstd = _layernorm_rows(ubs)
            for q, l0 in enumerate(range(0, D_B, LANES)):
                t = (ubs[q] - mu) * rstd * lg_ref[:, l0:l0 + LANES] + lb_ref[:, l0:l0 + LANES]
                y_ref[rows, D_A + l0:D_A + l0 + LANES] = (t * _sigmoid(t)).astype(BF16)
            return carry

        lax.fori_loop(0, ts // rc, main, 0)
        for n0 in range(0, D_MODEL, 512):
            x2_ref[:, n0:n0 + 512] = x_ref[:, n0:n0 + 512] + jnp.dot(
                y_ref[...], wo_ref[:, n0:n0 + 512], preferred_element_type=F32)

    return _call(
        body, name="fwd_mix", grid=(nt,),
        in_specs=[_rows(ts, D_IN), _prev(ts, D_IN), _next(ts, D_IN, s), _rows(ts, D_MODEL),
                  _const((K_A, D_A)), _const((K_B, D_B)), _const((1, D_B)), _const((1, D_B)), _const((1, D_B)),
                  _const((D_MODEL, D_MODEL))],
        out_specs=[_rows(ts, D_MODEL), _rows(ts, D_MODEL), _rows(ts, D_B)],
        out_shape=[_sds((s, D_MODEL), F32), _sds((s, D_MODEL), BF16), _sds((s, D_B), F32)],
        scratch_shapes=[_ext_scratch(ts, D_A), _ext_scratch(ts, D_B)],
        args=(z, z, z, x, wa, wb, bb, lg, lb, w_out), exchange=exchange, forward_step=nt * 5 // 8)


def _fwd_ffn_in(x2, g2, w_gate_t, w_up_t, ts):
    s = x2.shape[0]
    half = D_FF // 2

    def body(x_ref, g_ref, wg_ref, wu_ref, g0_ref, v_ref, h_ref):
        xf = x_ref[...]
        r = lax.rsqrt(jnp.mean(xf * xf, axis=-1, keepdims=True) + RMS_EPS)
        h = (xf * r * g_ref[...]).astype(BF16)
        h_ref[...] = h
        for n0 in range(0, D_FF, half):
            g0_ref[:, n0:n0 + half] = lax.dot_general(h, wg_ref[n0:n0 + half, :], _NT,
                                                      preferred_element_type=F32).astype(BF16)
            v_ref[:, n0:n0 + half] = lax.dot_general(h, wu_ref[n0:n0 + half, :], _NT,
                                                     preferred_element_type=F32).astype(BF16)

    return pl.pallas_call(
        body, name="fwd_ffn_in", grid=(s // ts,),
        in_specs=[_rows(ts, D_MODEL), _const((1, D_MODEL)), _const((D_FF, D_MODEL)), _const((D_FF, D_MODEL))],
        out_specs=[_rows(ts, D_FF), _rows(ts, D_FF), _rows(ts, D_MODEL)],
        out_shape=[_sds((s, D_FF), BF16), _sds((s, D_FF), BF16), _sds((s, D_MODEL), BF16)],
        compiler_params=_params(),
    )(x2, g2, w_gate_t, w_up_t)


def _fwd_ffn_out(g0, v, x2, wf, w_down, g3, target, ts):
    s = x2.shape[0]
    nt = s // ts
    rc = min(ROW_CHUNK, ts)
    n_sub = ts // rc

    def body(g0_ref, gp_ref, gn_ref, v_ref, x2_ref, wf_ref, wd_ref, g3_ref, t_ref,
             a_ref, dx3_ref, dx3b_ref, loss_ref, dg3_ref, ge_ref, sums_ref, a0_ref, a1_ref, p0_ref, p1_ref):
        i = pl.program_id(0)
        vt = jnp.minimum(i, nt - 1)
        live = (i >= 2).astype(F32)

        @pl.when(i == 0)
        def _():
            sums_ref[...] = jnp.zeros_like(sums_ref)
            a1_ref[...] = jnp.zeros_like(a1_ref)
            p1_ref[...] = jnp.zeros_like(p1_ref)

        _fill_halo(ge_ref, gp_ref[...].astype(F32), gn_ref[...].astype(F32), ts, vt == 0, vt == nt - 1)

        def fill(j, carry):
            r0 = pl.multiple_of(j * rc, rc)
            _put_rows(ge_ref, r0, rc, g0_ref[pl.ds(r0, rc), :].astype(F32))
            return carry

        lax.fori_loop(0, n_sub, fill, 0)

        def stage(a_new, a_old, p_new, p_old):
            def act(r0, l0):
                rows = slice(r0, r0 + rc)
                g = _conv_block(ge_ref, wf_ref, r0, rc, l0, K_F, False)
                vv = v_ref[rows, l0:l0 + LANES].astype(F32)
                a = (g * _sigmoid(g) * vv).astype(BF16)
                a_ref[rows, l0:l0 + LANES] = a
                a_new[rows, l0:l0 + LANES] = a

            def tail(r0):
                rows = slice(r0, r0 + rc)
                x3 = x2_ref[rows, :] + p_old[rows, :]
                r = lax.rsqrt(jnp.mean(x3 * x3, axis=-1, keepdims=True) + RMS_EPS)
                xhat = x3 * r
                diff = xhat * g3_ref[...] - t_ref[rows, :]
                dout = diff * (1.0 / D_MODEL)
                dxh = dout * g3_ref[...]
                dx3 = r * (dxh - xhat * jnp.mean(dxh * xhat, axis=-1, keepdims=True))
                dx3_ref[rows, :] = dx3
                dx3b_ref[rows, :] = dx3.astype(BF16)
                sums_ref[0] = sums_ref[0] + _fold8(diff * diff) * live
                sums_ref[1] = sums_ref[1] + _fold8(dout * xhat) * live

            units = []
            for q in range(n_sub):
                units += [(act, (q * rc, l0)) for l0 in range(0, D_FF, LANES)]
                units.append((tail, (q * rc,)))
            _interleaved(units, _matmul_pieces([(a_old, wd_ref)], p_new, 2))

        @pl.when(i % 2 == 0)
        def _():
            stage(a0_ref, a1_ref, p0_ref, p1_ref)

        @pl.when(i % 2 == 1)
        def _():
            stage(a1_ref, a0_ref, p1_ref, p0_ref)

        @pl.when(i == nt + 1)
        def _():
            loss_ref[...] = (0.5 / D_MODEL) * jnp.sum(sums_ref[0], keepdims=True)
            dg3_ref[...] = jnp.sum(sums_ref[1], axis=0, keepdims=True)

    vtile = lambda i: jnp.minimum(i, nt - 1)
    ttile = lambda i: jnp.clip(i - 2, 0, nt - 1)
    return pl.pallas_call(
        body, name="fwd_ffn_out", grid=(nt + 2,),
        in_specs=[_rows_at(ts, D_FF, vtile), _prev_at(ts, D_FF, vtile), _next_at(ts, D_FF, s, vtile),
                  _rows_at(ts, D_FF, vtile), _rows_at(ts, D_MODEL, ttile),
                  _const((K_F, D_FF)), _const((D_FF, D_MODEL)), _const((1, D_MODEL)), _rows_at(ts, D_MODEL, ttile)],
        out_specs=[_rows_at(ts, D_FF, vtile), _rows_at(ts, D_MODEL, ttile), _rows_at(ts, D_MODEL, ttile),
                   _acc_out((1, 1)), _acc_out((1, D_MODEL))],
        out_shape=[_sds((s, D_FF), BF16), _sds((s, D_MODEL), F32), _sds((s, D_MODEL), BF16),
                   _sds((1, 1), F32), _sds((1, D_MODEL), F32)],
        scratch_shapes=[_ext_scratch(ts, D_FF), pltpu.VMEM((2, SUBLANES, D_MODEL), F32),
                        pltpu.VMEM((ts, D_FF), BF16), pltpu.VMEM((ts, D_FF), BF16),
                        pltpu.VMEM((ts, D_MODEL), F32), pltpu.VMEM((ts, D_MODEL), F32)],
        compiler_params=_params(),
    )(g0, g0, g0, v, x2, wf, w_down, g3, target)


def _bwd_ffn_a(dx3b, g0, v, w_down_t, wf, ts):
    s = dx3b.shape[0]
    nt = s // ts
    rc = min(ROW_CHUNK, ts)
    n_sub = ts // rc

    def body(dx_ref, g0_ref, gp_ref, gn_ref, v_ref, wd_ref, wf_ref, dg_ref, dv_ref, dwf_ref, ge_ref, da0_ref, da1_ref,
             acc_ref):
        i = pl.program_id(0)
        t = jnp.maximum(i - 1, 0)

        @pl.when(i == 0)
        def _():
            acc_ref[...] = jnp.zeros_like(acc_ref)
            da1_ref[...] = jnp.zeros_like(da1_ref)

        _fill_halo(ge_ref, gp_ref[...].astype(F32), gn_ref[...].astype(F32), ts, t == 0, t == nt - 1)

        def fill(j, carry):
            r0 = pl.multiple_of(j * rc, rc)
            _put_rows(ge_ref, r0, rc, g0_ref[pl.ds(r0, rc), :].astype(F32))
            return carry

        lax.fori_loop(0, n_sub, fill, 0)

        def stage(da_new, da_old):
            def piece(m0, n0, width):
                da_new[m0:m0 + MXU_ROWS, n0:n0 + width] = jnp.dot(
                    dx_ref[m0:m0 + MXU_ROWS, :], wd_ref[:, n0:n0 + width], preferred_element_type=F32).astype(BF16)

            def unit(r0, l0):
                rows = slice(r0, r0 + rc)
                g = _conv_block(ge_ref, wf_ref, r0, rc, l0, K_F, False)
                sg = _sigmoid(g)
                da = da_old[rows, l0:l0 + LANES].astype(F32)
                vv = v_ref[rows, l0:l0 + LANES].astype(F32)
                dv_ref[rows, l0:l0 + LANES] = (da * (g * sg)).astype(BF16)
                dgg = da * vv * (sg * (1.0 + g * (1.0 - sg)))
                dg_ref[rows, l0:l0 + LANES] = dgg.astype(BF16)
                _conv_wgrad_block(acc_ref, dgg, ge_ref, r0, rc, l0, K_F)

            _interleaved([(unit, (q * rc, l0)) for q in range(n_sub) for l0 in range(0, D_FF, LANES)],
                         [(piece, (m0, n0, w)) for n0, w in _col_pieces(D_FF) for m0 in range(0, ts, MXU_ROWS)])

        @pl.when(i % 2 == 0)
        def _():
            stage(da0_ref, da1_ref)

        @pl.when(i % 2 == 1)
        def _():
            stage(da1_ref, da0_ref)

        @pl.when(i == nt)
        def _():
            _reduce_acc(dwf_ref, acc_ref, K_F)

    cur = lambda i: jnp.minimum(i, nt - 1)
    old = lambda i: jnp.maximum(i - 1, 0)
    return pl.pallas_call(
        body, name="bwd_ffn_a", grid=(nt + 1,),
        in_specs=[_rows_at(ts, D_MODEL, cur), _rows_at(ts, D_FF, old), _prev_at(ts, D_FF, old), _next_at(ts, D_FF, s, old),
                  _rows_at(ts, D_FF, old), _const((D_MODEL, D_FF)), _const((K_F, D_FF))],
        out_specs=[_rows_at(ts, D_FF, old), _rows_at(ts, D_FF, old), _acc_out((K_F, D_FF))],
        out_shape=[_sds((s, D_FF), BF16), _sds((s, D_FF), BF16), _sds((K_F, D_FF), F32)],
        scratch_shapes=[_ext_scratch(ts, D_FF), pltpu.VMEM((ts, D_FF), BF16), pltpu.VMEM((ts, D_FF), BF16),
                        pltpu.VMEM((K_F, SUBLANES, D_FF), F32)],
        compiler_params=_params(),
    )(dx3b, g0, g0, g0, v, w_down_t, wf)


def _bwd_ffn_b(dg, dv, wf, w_gate, w_up, x2, g2, dx3, ts):
    s = x2.shape[0]
    nt = s // ts
    rc = min(ROW_CHUNK, ts)
    n_sub = ts // rc

    def body(dg_ref, dgp_ref, dgn_ref, dv_ref, wf_ref, wg_ref, wu_ref, x2_ref, g2_ref, dx3_ref,
             dg0_ref, dx2_ref, dx2b_ref, dgn2_ref, dge_ref, dg8_ref, a0_ref, a1_ref, p0_ref, p1_ref):
        i = pl.program_id(0)
        vt = jnp.minimum(i, nt - 1)
        live = (i >= 2).astype(F32)

        @pl.when(i == 0)
        def _():
            dg8_ref[...] = jnp.zeros_like(dg8_ref)
            a1_ref[...] = jnp.zeros_like(a1_ref)
            p1_ref[...] = jnp.zeros_like(p1_ref)

        _fill_halo(dge_ref, dgp_ref[...].astype(F32), dgn_ref[...].astype(F32), ts, vt == 0, vt == nt - 1)

        def fill(j, carry):
            r0 = pl.multiple_of(j * rc, rc)
            _put_rows(dge_ref, r0, rc, dg_ref[pl.ds(r0, rc), :].astype(F32))
            return carry

        lax.fori_loop(0, n_sub, fill, 0)

        def stage(a_new, a_old, p_new, p_old):
            def conv_t(r0, l0):
                rows = slice(r0, r0 + rc)
                dg0 = _conv_block(dge_ref, wf_ref, r0, rc, l0, K_F, True).astype(BF16)
                dg0_ref[rows, l0:l0 + LANES] = dg0
                a_new[rows, l0:l0 + LANES] = dg0

            def tail(r0):
                rows = slice(r0, r0 + rc)
                dx, dgrow = _rms_bwd_rows(p_old[rows, :], x2_ref[rows, :], g2_ref[...])
                dx2 = dx3_ref[rows, :] + dx
                dx2_ref[rows, :] = dx2
                dx2b_ref[rows, :] = dx2.astype(BF16)
                dg8_ref[...] = dg8_ref[...] + _fold8(dgrow) * live

            units = []
            for q in range(n_sub):
                units += [(conv_t, (q * rc, l0)) for l0 in range(0, D_FF, LANES)]
                units.append((tail, (q * rc,)))
            _interleaved(units, _matmul_pieces([(a_old, wg_ref), (dv_ref, wu_ref)], p_new, 2))

        @pl.when(i % 2 == 0)
        def _():
            stage(a0_ref, a1_ref, p0_ref, p1_ref)

        @pl.when(i % 2 == 1)
        def _():
            stage(a1_ref, a0_ref, p1_ref, p0_ref)

        @pl.when(i == nt + 1)
        def _():
            dgn2_ref[...] = jnp.sum(dg8_ref[...], axis=0, keepdims=True)

    vtile = lambda i: jnp.minimum(i, nt - 1)
    mtile = lambda i: jnp.clip(i - 1, 0, nt - 1)
    ttile = lambda i: jnp.clip(i - 2, 0, nt - 1)
    return pl.pallas_call(
        body, name="bwd_ffn_b", grid=(nt + 2,),
        in_specs=[_rows_at(ts, D_FF, vtile), _prev_at(ts, D_FF, vtile), _next_at(ts, D_FF, s, vtile),
                  _rows_at(ts, D_FF, mtile), _const((K_F, D_FF)),
                  _const((D_FF, D_MODEL)), _const((D_FF, D_MODEL)), _rows_at(ts, D_MODEL, ttile), _const((1, D_MODEL)),
                  _rows_at(ts, D_MODEL, ttile)],
        out_specs=[_rows_at(ts, D_FF, vtile), _rows_at(ts, D_MODEL, ttile), _rows_at(ts, D_MODEL, ttile),
                   _acc_out((1, D_MODEL))],
        out_shape=[_sds((s, D_FF), BF16), _sds((s, D_MODEL), F32), _sds((s, D_MODEL), BF16), _sds((1, D_MODEL), F32)],
        scratch_shapes=[_ext_scratch(ts, D_FF), pltpu.VMEM((SUBLANES, D_MODEL), F32),
                        pltpu.VMEM((ts, D_FF), BF16), pltpu.VMEM((ts, D_FF), BF16),
                        pltpu.VMEM((ts, D_MODEL), F32), pltpu.VMEM((ts, D_MODEL), F32)],
        compiler_params=_params(),
    )(dg, dg, dg, dv, wf, w_gate, w_up, x2, g2, dx3)


def _bwd_mix_a(dx2b, w_out_t, z, u, wa, lg, lb, ts, exchange=None):
    s = dx2b.shape[0]
    nt = s // ts
    rc = min(ROW_CHUNK, ts)
    n_sub = ts // rc

    def body(dx_ref, wo_ref, z_ref, zp_ref, zn_ref, u_ref, wa_ref, lg_ref, lb_ref,
             dca_ref, du_ref, dab_ref, dwa_ref, dlg_ref, dlb_ref, dbb_ref, pe_ref, dy0_ref, dy1_ref, acc_ref, sacc_ref):
        i = pl.program_id(0)
        t = jnp.maximum(i - 1, 0)

        @pl.when(i == 0)
        def _():
            acc_ref[...] = jnp.zeros_like(acc_ref)
            sacc_ref[...] = jnp.zeros_like(sacc_ref)
            dy1_ref[...] = jnp.zeros_like(dy1_ref)

        pp, _ = _p_u0(zp_ref, slice(None))
        pn, _ = _p_u0(zn_ref, slice(None))
        _fill_halo(pe_ref, pp, pn, ts, t == 0, t == nt - 1)

        def fill(j, carry):
            r0 = pl.multiple_of(j * rc, rc)
            rows = pl.ds(r0, rc)
            _put_rows(pe_ref, r0, rc, z_ref[rows, 2 * D_A:3 * D_A].astype(F32) * z_ref[rows, 0:D_A].astype(F32))
            return carry

        lax.fori_loop(0, n_sub, fill, 0)

        def stage(dy_new, dy_old):
            def piece(m0, n0, width):
                dy_new[m0:m0 + MXU_ROWS, n0:n0 + width] = jnp.dot(
                    dx_ref[m0:m0 + MXU_ROWS, :], wo_ref[:, n0:n0 + width], preferred_element_type=F32).astype(BF16)

            units = []
            for q in range(n_sub):
                units += [(mixer_a, (dy_old, q * rc, l0)) for l0 in range(0, D_A, LANES)]
                units.append((mixer_b, (dy_old, q * rc)))
            _interleaved(units, [(piece, (m0, n0, w)) for n0, w in _col_pieces(D_MODEL) for m0 in range(0, ts, MXU_ROWS)])

        def mixer_a(dy_ref, r0, l0):
            rows = slice(r0, r0 + rc)
            ca = _conv_block(pe_ref, wa_ref, r0, rc, l0, K_A, False)
            a_b = z_ref[rows, D_A + l0:D_A + l0 + LANES].astype(F32)
            dya = dy_ref[rows, l0:l0 + LANES].astype(F32)
            dab_ref[rows, l0:l0 + LANES] = (dya * ca).astype(BF16)
            dca = dya * a_b
            dca_ref[rows, l0:l0 + LANES] = dca
            _conv_wgrad_block(acc_ref, dca, pe_ref, r0, rc, l0, K_A)

        def mixer_b(dy_ref, r0):
            rows = slice(r0, r0 + rc)
            ubs = [u_ref[rows, l0:l0 + LANES] for l0 in range(0, D_B, LANES)]
            mu, rstd = _layernorm_rows(ubs)
            ns, dns = [], []
            m1 = None
            m2 = None
            for q, l0 in enumerate(range(0, D_B, LANES)):
                n = (ubs[q] - mu) * rstd
                lgq = lg_ref[:, l0:l0 + LANES]
                t = n * lgq + lb_ref[:, l0:l0 + LANES]
                sg = _sigmoid(t)
                dt = dy_ref[rows, D_A + l0:D_A + l0 + LANES].astype(F32) * (sg * (1.0 + t * (1.0 - sg)))
                dn = dt * lgq
                ns.append(n)
                dns.append(dn)
                s1 = jnp.sum(dn, axis=-1, keepdims=True)
                s2 = jnp.sum(dn * n, axis=-1, keepdims=True)
                m1 = s1 if m1 is None else m1 + s1
                m2 = s2 if m2 is None else m2 + s2
                sacc_ref[0, :, l0:l0 + LANES] = sacc_ref[0, :, l0:l0 + LANES] + _fold8(dt * n)
                sacc_ref[1, :, l0:l0 + LANES] = sacc_ref[1, :, l0:l0 + LANES] + _fold8(dt)
            m1 = m1 * (1.0 / D_B)
            m2 = m2 * (1.0 / D_B)
            for q, l0 in enumerate(range(0, D_B, LANES)):
                du = rstd * (dns[q] - m1 - ns[q] * m2)
                du_ref[rows, l0:l0 + LANES] = du
                sacc_ref[2, :, l0:l0 + LANES] = sacc_ref[2, :, l0:l0 + LANES] + _fold8(du)

        @pl.when(i % 2 == 0)
        def _():
            stage(dy0_ref, dy1_ref)

        @pl.when(i % 2 == 1)
        def _():
            stage(dy1_ref, dy0_ref)

        @pl.when(i == nt)
        def _():
            _reduce_acc(dwa_ref, acc_ref, K_A)
            dlg_ref[...] = jnp.sum(sacc_ref[0], axis=0, keepdims=True)
            dlb_ref[...] = jnp.sum(sacc_ref[1], axis=0, keepdims=True)
            dbb_ref[...] = jnp.sum(sacc_ref[2], axis=0, keepdims=True)

    cur = lambda i: jnp.minimum(i, nt - 1)
    old = lambda i: jnp.maximum(i - 1, 0)
    return _call(
        body, name="bwd_mix_a", grid=(nt + 1,),
        in_specs=[_rows_at(ts, D_MODEL, cur), _const((D_MODEL, D_MODEL)), _rows_at(ts, D_IN, old), _prev_at(ts, D_IN, old),
                  _next_at(ts, D_IN, s, old), _rows_at(ts, D_B, old), _const((K_A, D_A)), _const((1, D_B)), _const((1, D_B))],
        out_specs=[_rows_at(ts, D_A, old), _rows_at(ts, D_B, old), _rows_at(ts, D_A, old), _acc_out((K_A, D_A)),
                   _acc_out((1, D_B)), _acc_out((1, D_B)), _acc_out((1, D_B))],
        out_shape=[_sds((s, D_A), F32), _sds((s, D_B), F32), _sds((s, D_A), BF16), _sds((K_A, D_A), F32),
                   _sds((1, D_B), F32), _sds((1, D_B), F32), _sds((1, D_B), F32)],
        scratch_shapes=[_ext_scratch(ts, D_A), pltpu.VMEM((ts, D_MODEL), BF16), pltpu.VMEM((ts, D_MODEL), BF16),
                        pltpu.VMEM((K_A, SUBLANES, D_A), F32), pltpu.VMEM((3, SUBLANES, D_B), F32)],
        args=(dx2b, w_out_t, z, z, z, u, wa, lg, lb), exchange=exchange)


def _bwd_mix_b(dca, du, z, dab, wa, wb, w_in, x, g1, dx2, ts, exchange=None):
    s = x.shape[0]
    nt = s // ts
    rc = min(ROW_CHUNK, ts)
    n_sub = ts // rc

    def body(dca_ref, dcap_ref, dcan_ref, du_ref, dup_ref, dun_ref, z_ref, zp_ref, zn_ref, dab_ref,
             wa_ref, wb_ref, wi_ref, x_ref, g1_ref, dx2_ref,
             dz_ref, dx_ref, dg1_ref, dwb_ref, dcae_ref, due_ref, ue_ref, acc_ref, dg8_ref,
             dz0_ref, dz1_ref, dh0_ref, dh1_ref):
        i = pl.program_id(0)

        @pl.when(i == 0)
        def _():
            acc_ref[...] = jnp.zeros_like(acc_ref)
            dg8_ref[...] = jnp.zeros_like(dg8_ref)
            dz1_ref[...] = jnp.zeros_like(dz1_ref)
            dh1_ref[...] = jnp.zeros_like(dh1_ref)

        vt = jnp.minimum(i, nt - 1)
        first = vt == 0
        last = vt == nt - 1
        live = (i < nt).astype(F32)
        _fill_halo(dcae_ref, dcap_ref[...], dcan_ref[...], ts, first, last)
        _fill_halo(due_ref, dup_ref[...], dun_ref[...], ts, first, last)
        _, up = _p_u0(zp_ref, slice(None))
        _, un = _p_u0(zn_ref, slice(None))
        _fill_halo(ue_ref, up, un, ts, first, last)

        def fill(j, carry):
            r0 = pl.multiple_of(j * rc, rc)
            rows = pl.ds(r0, rc)
            _put_rows(dcae_ref, r0, rc, dca_ref[rows, :])
            _put_rows(due_ref, r0, rc, du_ref[rows, :])
            b_v = z_ref[rows, 3 * D_A:3 * D_A + D_B].astype(F32)
            b_g = z_ref[rows, 3 * D_A + D_B:D_IN].astype(F32)
            _put_rows(ue_ref, r0, rc, b_v * _sigmoid(b_g))
            return carry

        lax.fori_loop(0, n_sub, fill, 0)

        def stage(dz_new, dz_old, dh_new, dh_old):
            def put(rows, c0, val):
                dz_ref[rows, c0:c0 + LANES] = val
                dz_new[rows, c0:c0 + LANES] = val

            def mixer_a(r0, l0):
                rows = slice(r0, r0 + rc)
                dp = _conv_block(dcae_ref, wa_ref, r0, rc, l0, K_A, True)
                a_h = z_ref[rows, l0:l0 + LANES].astype(F32)
                a_c = z_ref[rows, 2 * D_A + l0:2 * D_A + l0 + LANES].astype(F32)
                put(rows, l0, (dp * a_c).astype(BF16))
                put(rows, D_A + l0, dab_ref[rows, l0:l0 + LANES])
                put(rows, 2 * D_A + l0, (dp * a_h).astype(BF16))

            def mixer_b(r0, l0):
                rows = slice(r0, r0 + rc)
                du0 = _conv_block(due_ref, wb_ref, r0, rc, l0, K_B, True)
                b_v = z_ref[rows, 3 * D_A + l0:3 * D_A + l0 + LANES].astype(F32)
                b_g = z_ref[rows, 3 * D_A + D_B + l0:3 * D_A + D_B + l0 + LANES].astype(F32)
                sg = _sigmoid(b_g)
                put(rows, 3 * D_A + l0, (du0 * sg).astype(BF16))
                put(rows, 3 * D_A + D_B + l0, (du0 * b_v * (sg * (1.0 - sg))).astype(BF16))
                _conv_wgrad_block(acc_ref, du_ref[rows, l0:l0 + LANES], ue_ref, r0, rc, l0, K_B, live)

            def tail(r0):
                rows = slice(r0, r0 + rc)
                dx, dgrow = _rms_bwd_rows(dh_old[rows, :], x_ref[rows, :], g1_ref[...])
                dx_ref[rows, :] = dx2_ref[rows, :] + dx
                dg8_ref[...] = dg8_ref[...] + _fold8(dgrow)

            units = []
            for q in range(n_sub):
                units += [(mixer_a, (q * rc, l0)) for l0 in range(0, D_A, LANES)]
                units += [(mixer_b, (q * rc, l0)) for l0 in range(0, D_B, LANES)]
                units.append((tail, (q * rc,)))
            _interleaved(units, _matmul_pieces([(dz_old, wi_ref)], dh_new, 2))

        @pl.when(i % 2 == 0)
        def _():
            stage(dz0_ref, dz1_ref, dh0_ref, dh1_ref)

        @pl.when(i % 2 == 1)
        def _():
            stage(dz1_ref, dz0_ref, dh1_ref, dh0_ref)

        @pl.when(i == nt + 1)
        def _():
            _reduce_acc(dwb_ref, acc_ref, K_B)
            dg1_ref[...] = jnp.sum(dg8_ref[...], axis=0, keepdims=True)

    vtile = lambda i: jnp.minimum(i, nt - 1)
    ttile = lambda i: jnp.clip(i - 2, 0, nt - 1)
    return _call(
        body, name="bwd_mix_b", grid=(nt + 2,),
        in_specs=[_rows_at(ts, D_A, vtile), _prev_at(ts, D_A, vtile), _next_at(ts, D_A, s, vtile),
                  _rows_at(ts, D_B, vtile), _prev_at(ts, D_B, vtile), _next_at(ts, D_B, s, vtile),
                  _rows_at(ts, D_IN, vtile), _prev_at(ts, D_IN, vtile), _next_at(ts, D_IN, s, vtile), _rows_at(ts, D_A, vtile),
                  _const((K_A, D_A)), _const((K_B, D_B)), _const((D_IN, D_MODEL)), _rows_at(ts, D_MODEL, ttile),
                  _const((1, D_MODEL)), _rows_at(ts, D_MODEL, ttile)],
        out_specs=[_rows_at(ts, D_IN, vtile), _rows_at(ts, D_MODEL, ttile), _acc_out((1, D_MODEL)), _acc_out((K_B, D_B))],
        out_shape=[_sds((s, D_IN), BF16), _sds((s, D_MODEL), F32), _sds((1, D_MODEL), F32), _sds((K_B, D_B), F32)],
        scratch_shapes=[_ext_scratch(ts, D_A), _ext_scratch(ts, D_B), _ext_scratch(ts, D_B),
                        pltpu.VMEM((K_B, SUBLANES, D_B), F32), pltpu.VMEM((SUBLANES, D_MODEL), F32),
                        pltpu.VMEM((ts, D_IN), BF16), pltpu.VMEM((ts, D_IN), BF16),
                        pltpu.VMEM((ts, D_MODEL), F32), pltpu.VMEM((ts, D_MODEL), F32)],
        args=(dca, dca, dca, du, du, du, z, z, z, dab, wa, wb, w_in, x, g1, dx2), exchange=exchange)


def _matmul_tn(a, b, name):
    s, m = a.shape
    n = b.shape[1]
    tk = min(1024, s)
    nk = s // tk
    tm = 256

    def body(a_ref, b_ref, o_ref, acc_ref):
        k = pl.program_id(0)

        @pl.when(k == 0)
        def _():
            acc_ref[...] = jnp.zeros_like(acc_ref)

        for m0 in range(0, m, tm):
            acc_ref[m0:m0 + tm, :] = acc_ref[m0:m0 + tm, :] + lax.dot_general(
                a_ref[:, m0:m0 + tm], b_ref[...], _TN, preferred_element_type=F32)

        @pl.when(k == nk - 1)
        def _():
            o_ref[...] = acc_ref[...].astype(BF16)

    return pl.pallas_call(
        body, name=name, grid=(nk,),
        in_specs=[_rows(tk, m), _rows(tk, n)],
        out_specs=_acc_out((m, n)),
        out_shape=_sds((m, n), BF16),
        scratch_shapes=[pltpu.VMEM((m, n), F32)],
        compiler_params=_params(),
    )(a, b)


CHIP_RELS = ((1, 0, 0), (0, 1, 0), (1, 1, 0))
CORE_RELS = ((0, 0, 1),)
ALL_RELS = ((0, 0, 1), (0, 1, 0), (0, 1, 1), (1, 0, 0), (1, 0, 1), (1, 1, 0), (1, 1, 1))


def _chip_slot(dev):
    return 2 * dev[0] + dev[1]


def _dev_slot(dev):
    return 4 * dev[0] + 2 * dev[1] + dev[2]


def _me():
    return (lax.axis_index("x"), lax.axis_index("y"), lax.axis_index("c"))


def _peer(me, rel):
    return tuple((1 - me[a]) if rel[a] else me[a] for a in range(3))


_ANY = pl.BlockSpec(memory_space=pl.ANY)


class _Exchange:
    def __init__(self, inputs, out_shape, scratch, start, finish, forward=None):
        self.inputs, self.out_shape, self.scratch = list(inputs), list(out_shape), list(scratch)
        self.start, self.finish, self.forward = start, finish, forward


def _all_gather(payloads):
    n_p = len(payloads)
    n_k = 1 + 2 * len(CHIP_RELS)

    def copy(srcs, dsts, sems, p, k, block_dev, to, from_src):
        blk = dsts[p].at[_dev_slot(block_dev)]
        return pltpu.make_async_remote_copy(
            src_ref=srcs[p] if from_src else blk, dst_ref=blk,
            send_sem=sems[0].at[n_k * p + k], recv_sem=sems[1].at[n_k * p + k], device_id=to, device_id_type=MESH)

    def own_copy(srcs, dsts, sems, p):
        return pltpu.make_async_copy(srcs[p], dsts[p].at[_dev_slot(_me())], sems[2].at[p])

    def start(srcs, dsts, sems):
        me = _me()
        for p in range(n_p):
            own_copy(srcs, dsts, sems, p).start()
        for j, rel in enumerate(CHIP_RELS):
            for p in range(n_p):
                copy(srcs, dsts, sems, p, 1 + j, me, _peer(me, rel), True).start()
        for p in range(n_p):
            copy(srcs, dsts, sems, p, 0, me, _peer(me, CORE_RELS[0]), True).start()

    def forward(srcs, dsts, sems):
        me = _me()
        sibling = _peer(me, CORE_RELS[0])
        for j, rel in enumerate(CHIP_RELS):
            other = _peer(me, rel)
            for p in range(n_p):
                copy(srcs, dsts, sems, p, 1 + j, other, me, False).wait_recv()
                copy(srcs, dsts, sems, p, 4 + j, other, sibling, False).start()

    def finish(srcs, dsts, sems):
        me = _me()
        sibling = _peer(me, CORE_RELS[0])
        for p in range(n_p):
            copy(srcs, dsts, sems, p, 0, sibling, me, False).wait_recv()
        for j, rel in enumerate(CHIP_RELS):
            for p in range(n_p):
                copy(srcs, dsts, sems, p, 4 + j, _peer(sibling, rel), me, False).wait_recv()
        for p in range(n_p):
            own_copy(srcs, dsts, sems, p).wait()
            copy(srcs, dsts, sems, p, 0, me, sibling, True).wait_send()
            for j, rel in enumerate(CHIP_RELS):
                copy(srcs, dsts, sems, p, 1 + j, me, _peer(me, rel), True).wait_send()
                copy(srcs, dsts, sems, p, 4 + j, _peer(me, rel), sibling, False).wait_send()

    return _Exchange(
        payloads, [_sds((N_DEV,) + p.shape, p.dtype) for p in payloads],
        [pltpu.SemaphoreType.DMA((n_p * n_k,)), pltpu.SemaphoreType.DMA((n_p * n_k,)), pltpu.SemaphoreType.DMA((n_p,))],
        start, finish, forward)


def _gather_direct(payload):
    n_r = len(ALL_RELS)

    def copies(srcs, dsts, sems):
        me = _me()
        mine = dsts[0].at[_dev_slot(me)]
        own = pltpu.make_async_copy(srcs[0], mine, sems[2].at[0])
        remote = [pltpu.make_async_remote_copy(src_ref=srcs[0], dst_ref=mine, send_sem=sems[0].at[k], recv_sem=sems[1].at[k],
                                               device_id=_peer(me, rel), device_id_type=MESH)
                  for k, rel in enumerate(ALL_RELS)]
        return [own] + remote

    def start(srcs, dsts, sems):
        for cp in copies(srcs, dsts, sems):
            cp.start()

    def finish(srcs, dsts, sems):
        for cp in copies(srcs, dsts, sems):
            cp.wait()

    return _Exchange([payload], [_sds((N_DEV,) + payload.shape, payload.dtype)],
                     [pltpu.SemaphoreType.DMA((n_r,)), pltpu.SemaphoreType.DMA((n_r,)), pltpu.SemaphoreType.DMA((1,))],
                     start, finish)


def _scatter_exchange(payloads, rels, src_view, view_shapes):
    n_p = len(payloads)
    n_r = len(rels)

    def copies(srcs, dsts, sems):
        me = _me()
        out = []
        for k, rel in enumerate(rels):
            peer = _peer(me, rel)
            for p in range(n_p):
                out.append(pltpu.make_async_remote_copy(
                    src_ref=src_view(srcs[p], peer), dst_ref=dsts[p].at[k],
                    send_sem=sems[0].at[p * n_r + k], recv_sem=sems[1].at[p * n_r + k],
                    device_id=peer, device_id_type=MESH))
        return out

    def start(srcs, dsts, sems):
        for cp in copies(srcs, dsts, sems):
            cp.start()

    def finish(srcs, dsts, sems):
        for cp in copies(srcs, dsts, sems):
            cp.wait()

    return _Exchange(payloads, [_sds((n_r,) + vs, p.dtype) for vs, p in zip(view_shapes, payloads)],
                     [pltpu.SemaphoreType.DMA((n_p * n_r,)), pltpu.SemaphoreType.DMA((n_p * n_r,))], start, finish)


def _split_refs(refs, sizes):
    out, at = [], 0
    for n in sizes:
        out.append(refs[at:at + n])
        at += n
    return out


def _run_exchanges(name, exchanges):
    n_in = [len(e.inputs) for e in exchanges]
    n_out = [len(e.out_shape) for e in exchanges]
    n_sc = [len(e.scratch) for e in exchanges]

    def body(*refs):
        ins, outs, scs = _split_refs(refs, [sum(n_in), sum(n_out), sum(n_sc)])
        parts = list(zip(exchanges, _split_refs(ins, n_in), _split_refs(outs, n_out), _split_refs(scs, n_sc)))
        for e, i, o, s in parts:
            e.start(i, o, s)
        for e, i, o, s in parts:
            if e.forward is not None:
                e.forward(i, o, s)
        for e, i, o, s in parts:
            e.finish(i, o, s)

    outs = pl.pallas_call(
        body, name=name, in_specs=[_ANY] * sum(n_in), out_specs=[_ANY] * sum(n_out),
        out_shape=[sd for e in exchanges for sd in e.out_shape],
        scratch_shapes=[sc for e in exchanges for sc in e.scratch],
    )(*[a for e in exchanges for a in e.inputs])
    return _split_refs(list(outs), n_out)


def _call(body, *, name, grid, in_specs, out_specs, out_shape, scratch_shapes, args, exchange=None, forward_step=None):
    n_in, n_out, n_sc = len(in_specs), len(out_specs), len(scratch_shapes)
    if exchange is None:
        outs = pl.pallas_call(body, name=name, grid=grid, in_specs=in_specs, out_specs=out_specs, out_shape=out_shape,
                              scratch_shapes=scratch_shapes, compiler_params=_params())(*args)
        return list(outs), []
    e = exchange
    sizes = [n_in, len(e.inputs), n_out, len(e.out_shape), n_sc, len(e.scratch)]
    last = grid[0] - 1

    def wrapped(*refs):
        a, ei, o, eo, sc, es = _split_refs(refs, sizes)
        i = pl.program_id(0)

        @pl.when(i == 0)
        def _():
            e.start(ei, eo, es)

        if e.forward is not None:
            @pl.when(i == forward_step)
            def _():
                e.forward(ei, eo, es)

        body(*a, *o, *sc)

        @pl.when(i == last)
        def _():
            e.finish(ei, eo, es)

    outs = pl.pallas_call(
        wrapped, name=name, grid=grid,
        in_specs=list(in_specs) + [_ANY] * len(e.inputs), out_specs=list(out_specs) + [_ANY] * len(e.out_shape),
        out_shape=list(out_shape) + e.out_shape, scratch_shapes=list(scratch_shapes) + e.scratch,
        compiler_params=_params(),
    )(*args, *e.inputs)
    outs = list(outs)
    return outs[:n_out], outs[n_out:]


def _pair_sum(grads, recvd, my_core, name):
    n_p = len(grads)

    def body(c_ref, *refs):
        del c_ref
        for p in range(n_p):
            refs[2 * n_p + p][...] = (refs[p][...].astype(F32) + refs[n_p + p][...].astype(F32)).astype(BF16)

    def blk(g):
        return (None, None) + g.shape[2:]

    return pl.pallas_call(
        body, name=name,
        grid_spec=pltpu.PrefetchScalarGridSpec(
            num_scalar_prefetch=1, grid=(N_CHIP,),
            in_specs=[pl.BlockSpec(blk(g), lambda j, c: (j, c[0], 0, 0)) for g in grads]
            + [pl.BlockSpec(blk(g), lambda j, c: (0, j, 0, 0)) for g in grads],
            out_specs=[pl.BlockSpec((None,) + g.shape[2:], lambda j, c: (j, 0, 0)) for g in grads]),
        out_shape=[_sds((N_CHIP,) + g.shape[2:], BF16) for g in grads],
        compiler_params=pltpu.CompilerParams(dimension_semantics=("arbitrary",), vmem_limit_bytes=VMEM_LIMIT),
    )(my_core, *grads, *recvd)


def _chip_sum(psums, recvd, my_chip, name):
    n_p = len(psums)

    def body(c_ref, *refs):
        del c_ref
        for p in range(n_p):
            acc = refs[p][...].astype(F32)
            for k in range(len(CHIP_RELS)):
                acc = acc + refs[n_p + p][k].astype(F32)
            refs[2 * n_p + p][...] = acc

    return pl.pallas_call(
        body, name=name,
        grid_spec=pltpu.PrefetchScalarGridSpec(
            num_scalar_prefetch=1, grid=(1,),
            in_specs=[pl.BlockSpec((None,) + g.shape[1:], lambda i, c: (c[0], 0, 0)) for g in psums]
            + [pl.BlockSpec(r.shape, lambda i, c: (0, 0, 0)) for r in recvd],
            out_specs=[pl.BlockSpec(g.shape[1:], lambda i, c: (0, 0)) for g in psums]),
        out_shape=[_sds(g.shape[1:], F32) for g in psums],
        compiler_params=pltpu.CompilerParams(dimension_semantics=("arbitrary",), vmem_limit_bytes=VMEM_LIMIT),
    )(my_chip, *psums, *recvd)


def _sum_devices(parts):
    def body(p_ref, o_ref):
        acc = p_ref[0]
        for j in range(1, N_DEV):
            acc = acc + p_ref[j]
        o_ref[...] = acc

    return pl.pallas_call(body, name="small_grad_sum", out_shape=_sds(parts.shape[1:], F32))(parts)


def _adamw(w, g, m, v, name):
    def body(w_ref, g_ref, m_ref, v_ref, d_ref, mo_ref, vo_ref):
        gg = g_ref[...]
        mn = ADAM_B1 * m_ref[...] + (1.0 - ADAM_B1) * gg
        vn = ADAM_B2 * v_ref[...] + (1.0 - ADAM_B2) * (gg * gg)
        m_hat = mn / (1.0 - ADAM_B1 ** ADAM_STEP)
        v_hat = vn / (1.0 - ADAM_B2 ** ADAM_STEP)
        d_ref[...] = -ADAM_LR * (m_hat / (jnp.sqrt(v_hat) + ADAM_EPS) + ADAM_WD * w_ref[...])
        mo_ref[...] = mn
        vo_ref[...] = vn

    sd = _sds(w.shape, F32)
    return pl.pallas_call(body, name=name, out_shape=[sd, sd, sd],
                          compiler_params=pltpu.CompilerParams(vmem_limit_bytes=VMEM_LIMIT))(w, g, m, v)


class _Mesh:
    def __init__(self, shards, my_chip, my_core):
        self.shards, self.my_chip, self.my_core = shards, my_chip.reshape(1), my_core.reshape(1)

    def gather(self, names):
        return _all_gather([self.shards[n] for n in names])

    @staticmethod
    def whole(gathered):
        return gathered.reshape(N_DEV * gathered.shape[1], gathered.shape[2])

    @staticmethod
    def by_device(grads):
        return [g.reshape(N_CHIP, 2, g.shape[0] // N_DEV, g.shape[1]) for g in grads]

    @staticmethod
    def to_sibling(parts):
        return _scatter_exchange(parts, CORE_RELS, lambda ref, peer: ref.at[:, peer[2]],
                                 [(N_CHIP,) + p.shape[2:] for p in parts])

    @staticmethod
    def to_chips(pair):
        return _scatter_exchange(pair, CHIP_RELS, lambda ref, peer: ref.at[_chip_slot(peer)], [p.shape[1:] for p in pair])


def _step(x, target, g1, w_in_t, wa, wb, bb, lg, lb, w_out, g2, w_gate_t, w_up_t, wf, w_down, g3, ts, mesh=None):
    (z, h1), got = _fwd_in(x, g1, w_in_t, ts, exchange=mesh and mesh.gather(["w_out", "w_gate"]))
    if mesh:
        w_out, w_gate_t = [mesh.whole(g) for g in got]
    (x2, y, u), got = _fwd_mix(z, x, wa, wb, bb, lg, lb, w_out, ts, exchange=mesh and mesh.gather(["w_up", "w_down"]))
    if mesh:
        w_up_t, w_down = [mesh.whole(g) for g in got]
    g0, v, h2 = _fwd_ffn_in(x2, g2, w_gate_t, w_up_t, ts)
    tk = min(ts, SKEW_TILE)
    a, dx3, dx3b, loss, dg3 = _fwd_ffn_out(g0, v, x2, wf, w_down, g3, target, tk)
    dgc, dv, dwf = _bwd_ffn_a(dx3b, g0, v, jnp.swapaxes(w_down, 0, 1), wf, tk)
    dg0, dx2, dx2b, dg2 = _bwd_ffn_b(dgc, dv, wf, w_gate_t, w_up_t, x2, g2, dx3, tk)
    first = dict(w_down=_matmul_tn(a, dx3b, "wgrad_down"), w_gate=_matmul_tn(dg0, h2, "wgrad_gate"),
                 w_up=_matmul_tn(dv, h2, "wgrad_up"), w_out=_matmul_tn(y, dx2b, "wgrad_out"))
    if mesh:
        parts = mesh.by_device(list(first.values()))
    (dca, du, dab, dwa, dlg, dlb, dbb), got = _bwd_mix_a(dx2b, jnp.swapaxes(w_out, 0, 1), z, u, wa, lg, lb, tk,
                                                         exchange=mesh and mesh.to_sibling(parts))
    if mesh:
        pair = _pair_sum(parts, got, mesh.my_core, "rs_pair_sum_first")
    (dz, dx, dg1, dwb), got = _bwd_mix_b(dca, du, z, dab, wa, wb, w_in_t, x, g1, dx2, tk,
                                         exchange=mesh and mesh.to_chips(pair))
    dw_in_t = _matmul_tn(dz, h1, "wgrad_in")
    small = dict(norm_mix_g=dg1, conv_a_w=dwa, conv_b_w=dwb, conv_b_b=dbb, ln_b_g=dlg, ln_b_b=dlb,
                 norm_ffn_g=dg2, conv_ffn_w=dwf, norm_final_g=dg3)
    if not mesh:
        return loss, dx, dict(w_in=dw_in_t, **first), small
    big = dict(zip(first, _chip_sum(pair, got, mesh.my_chip, "rs_chip_sum_first")))
    parts = mesh.by_device([dw_in_t])
    (got,) = _run_exchanges("rs_cores_last", [mesh.to_sibling(parts)])
    pair = _pair_sum(parts, got, mesh.my_core, "rs_pair_sum_last")
    got, (every,) = _run_exchanges("rs_chips_last", [mesh.to_chips(pair), _gather_direct(_pack_small_grads(small, loss))])
    (big["w_in"],) = _chip_sum(pair, got, mesh.my_chip, "rs_chip_sum_last")
    return None, dx, big, _unpack_small_grads(_sum_devices(every))


def _pack_small_weights(conv_a_s, conv_b_s, conv_ffn_s):
    buf = jnp.zeros((SMALL_W_ROWS, SMALL_W_COLS), F32)
    buf = buf.at[0:K_A, 0:CONV_A_COLS].set(conv_a_s)
    buf = buf.at[K_A:K_A + K_B, 0:CONV_A_COLS].set(conv_b_s)
    return buf.at[K_A + K_B:K_A + K_B + K_F, 0:W_FF_COLS].set(conv_ffn_s)


def _unpack_small_weights(full):
    def take(r0, k, w):
        return jnp.transpose(full[:, r0:r0 + k, 0:w], (1, 0, 2)).reshape(k, N_DEV * w)

    return take(0, K_A, CONV_A_COLS), take(K_A, K_B, CONV_A_COLS), take(K_A + K_B, K_F, W_FF_COLS)


def _up8(n):
    return -(-n // SUBLANES) * SUBLANES


_SMALL_LAYOUT = (
    ("norm_mix_g", 2), ("norm_ffn_g", 2), ("norm_final_g", 2), ("conv_b_b", 1), ("ln_b_g", 1), ("ln_b_b", 1),
    ("conv_a_w", K_A), ("conv_b_w", K_B), ("conv_ffn_w", K_F * (FF_PAD // SMALL_G_COLS)), ("loss", 1))
SMALL_G_ROWS = sum(_up8(n) for _, n in _SMALL_LAYOUT)


def _pack_small_grads(small, loss):
    parts = []
    for name, n_rows in _SMALL_LAYOUT:
        if name == "loss":
            part = jnp.broadcast_to(loss.reshape(1, 1), (1, SMALL_G_COLS))
        elif name == "conv_ffn_w":
            part = jnp.pad(small[name], ((0, 0), (0, FF_PAD - D_FF))).reshape(n_rows, SMALL_G_COLS)
        else:
            part = small[name].reshape(n_rows, SMALL_G_COLS)
        parts.append(jnp.pad(part, ((0, _up8(n_rows) - n_rows), (0, 0))))
    return jnp.concatenate(parts, axis=0)


def _unpack_small_grads(tot):
    out = {}
    r = 0
    for name, n_rows in _SMALL_LAYOUT:
        blk = tot[r:r + n_rows]
        r += _up8(n_rows)
        if name == "loss":
            out[name] = blk[0, 0]
        elif name == "conv_ffn_w":
            out[name] = blk.reshape(K_F, FF_PAD)[:, 0:D_FF]
        elif name in ("conv_a_w", "conv_b_w"):
            out[name] = blk
        else:
            out[name] = blk.reshape(1, n_rows * SMALL_G_COLS)
    return out


def kernel(x, norm_mix_g, w_in, conv_a_w, conv_b_w, conv_b_b, ln_b_g, ln_b_b, w_out, norm_ffn_g, w_gate, w_up, conv_ffn_w, w_down, norm_final_g, loss_target, m_norm_mix_g, m_w_in, m_conv_a_w, m_conv_b_w, m_conv_b_b, m_ln_b_g, m_ln_b_b, m_w_out, m_norm_ffn_g, m_w_gate, m_w_up, m_conv_ffn_w, m_w_down, m_norm_final_g, v_norm_mix_g, v_w_in, v_conv_a_w, v_conv_b_w, v_conv_b_b, v_ln_b_g, v_ln_b_b, v_w_out, v_norm_ffn_g, v_w_gate, v_w_up, v_conv_ffn_w, v_w_down, v_norm_final_g):
    ix, iy, ic = lax.axis_index("x"), lax.axis_index("y"), lax.axis_index("c")
    my_chip = (2 * ix + iy).astype(jnp.int32)
    my_core = ic.astype(jnp.int32)
    my_dev = 2 * my_chip + my_core

    weights = dict(norm_mix_g=norm_mix_g, w_in=w_in, conv_a_w=conv_a_w, conv_b_w=conv_b_w, conv_b_b=conv_b_b,
                   ln_b_g=ln_b_g, ln_b_b=ln_b_b, w_out=w_out, norm_ffn_g=norm_ffn_g, w_gate=w_gate, w_up=w_up,
                   conv_ffn_w=conv_ffn_w, w_down=w_down, norm_final_g=norm_final_g)
    m_in = dict(norm_mix_g=m_norm_mix_g, w_in=m_w_in, conv_a_w=m_conv_a_w, conv_b_w=m_conv_b_w, conv_b_b=m_conv_b_b,
                ln_b_g=m_ln_b_g, ln_b_b=m_ln_b_b, w_out=m_w_out, norm_ffn_g=m_norm_ffn_g, w_gate=m_w_gate,
                w_up=m_w_up, conv_ffn_w=m_conv_ffn_w, w_down=m_w_down, norm_final_g=m_norm_final_g)
    v_in = dict(norm_mix_g=v_norm_mix_g, w_in=v_w_in, conv_a_w=v_conv_a_w, conv_b_w=v_conv_b_w, conv_b_b=v_conv_b_b,
                ln_b_g=v_ln_b_g, ln_b_b=v_ln_b_b, w_out=v_w_out, norm_ffn_g=v_norm_ffn_g, w_gate=v_w_gate,
                w_up=v_w_up, conv_ffn_w=v_conv_ffn_w, w_down=v_w_down, norm_final_g=v_norm_final_g)
    order = list(weights)
    big_names = ("w_in", "w_gate", "w_up", "w_out", "w_down")
    transposed = ("w_in", "w_gate", "w_up")

    def shard2d(name, a):
        if name in transposed:
            return jnp.swapaxes(a[0], 0, 1)
        return a.reshape(1, a.shape[0]) if a.ndim == 1 else a.reshape(a.shape[-2:])

    def unshard2d(name, a2, like):
        if name in transposed:
            return jnp.swapaxes(a2, 0, 1)[None]
        return a2.reshape(like.shape)

    mesh = _Mesh({n: shard2d(n, weights[n]).astype(BF16) for n in big_names}, my_chip, my_core)
    gathered, = _run_exchanges("ag_first", [_all_gather(
        [mesh.shards["w_in"], _pack_small_weights(conv_a_w[0], conv_b_w[0], conv_ffn_w[0])])])
    w_in_t = mesh.whole(gathered[0])
    wa_f, wb_f, wf_f = _unpack_small_weights(gathered[1])

    _, dx, gsum, stot = _step(
        x[0], loss_target[0], norm_mix_g, w_in_t, wa_f, wb_f, conv_b_b, ln_b_g, ln_b_b, None, norm_ffn_g,
        None, None, wf_f, None, norm_final_g.reshape(1, D_MODEL), SEQ_TILE, mesh)

    grads2d = dict(
        norm_mix_g=stot["norm_mix_g"],
        conv_a_w=lax.dynamic_slice(stot["conv_a_w"], (0, my_dev * CONV_A_COLS), (K_A, CONV_A_COLS)),
        conv_b_w=lax.dynamic_slice(stot["conv_b_w"], (0, my_dev * CONV_A_COLS), (K_B, CONV_A_COLS)),
        conv_b_b=stot["conv_b_b"], ln_b_g=stot["ln_b_g"], ln_b_b=stot["ln_b_b"],
        norm_ffn_g=stot["norm_ffn_g"],
        conv_ffn_w=lax.dynamic_slice(stot["conv_ffn_w"], (0, my_dev * W_FF_COLS), (K_F, W_FF_COLS)),
        norm_final_g=stot["norm_final_g"],
        **gsum,
    )

    g_out, d_out, m_out, v_out = [], [], [], []
    for name in order:
        w = weights[name]
        g2d = grads2d[name]
        d, mn, vn = _adamw(shard2d(name, w), g2d, shard2d(name, m_in[name]), shard2d(name, v_in[name]), "adamw_" + name)
        g_out.append(unshard2d(name, g2d, w))
        d_out.append(unshard2d(name, d, w))
        m_out.append(unshard2d(name, mn, w))
        v_out.append(unshard2d(name, vn, w))

    return (stot["loss"], dx[None], *g_out, *d_out, *m_out, *v_out)
```

```python
import jax
import jax.numpy as jnp
from jax import lax
from jax.experimental import pallas as pl
from jax.experimental.pallas import tpu as pltpu

F32 = jnp.float32
BF16 = jnp.bfloat16

D_MODEL = 1024
D_A = 512
D_B = 512
D_IN = 3 * D_A + 2 * D_B
D_FF = 2816
K_A = 3
K_B = 31
K_F = 3
RMS_EPS = 1e-6
LN_EPS = 1e-5

ADAM_LR = 0.001
ADAM_B1 = 0.9
ADAM_B2 = 0.999
ADAM_EPS = 1e-08
ADAM_WD = 0.01
ADAM_STEP = 10

N_DEV = 8
N_CHIP = 4
LANES = 128
SUBLANES = 8
HALO = 16
ROW_CHUNK = 64
SEQ_TILE = 512
SKEW_TILE = 256
VMEM_LIMIT = 56 * 1024 * 1024

MESH = pl.DeviceIdType.MESH

W_FF_COLS = D_FF // N_DEV
CONV_A_COLS = D_A // N_DEV
SMALL_W_ROWS = 40
SMALL_W_COLS = 384
SMALL_G_COLS = 512
FF_PAD = 3072


def _rows(ts, c):
    return pl.BlockSpec((ts, c), lambda i: (i, 0))


def _const(shape):
    return pl.BlockSpec(shape, lambda i: (0,) * len(shape), pipeline_mode=pl.Buffered(1))


def _acc_out(shape):
    return pl.BlockSpec(shape, lambda i: (0,) * len(shape))


def _rows_at(ts, c, tile):
    return pl.BlockSpec((ts, c), lambda i: (tile(i), 0))


def _prev_at(ts, c, tile):
    return pl.BlockSpec((HALO, c), lambda i: (jnp.maximum(tile(i) * (ts // HALO) - 1, 0), 0))


def _next_at(ts, c, s, tile):
    last = s // HALO - 1
    return pl.BlockSpec((HALO, c), lambda i: (jnp.minimum((tile(i) + 1) * (ts // HALO), last), 0))


def _prev(ts, c):
    return _prev_at(ts, c, lambda i: i)


def _next(ts, c, s):
    return _next_at(ts, c, s, lambda i: i)


MXU_COLS = 256
MXU_ROWS = 256


def _col_pieces(n):
    return [(c0, min(MXU_COLS, n - c0)) for c0 in range(0, n, MXU_COLS)]


def _matmul_pieces(terms, out_ref, k_parts):
    m, n = out_ref.shape
    steps = []
    for lhs_ref, w_ref in terms:
        tiles = lhs_ref.shape[1] // MXU_COLS
        cuts = [MXU_COLS * (tiles * j // k_parts) for j in range(k_parts)] + [lhs_ref.shape[1]]
        steps += [(lhs_ref, w_ref, cuts[j], cuts[j + 1]) for j in range(k_parts)]

    def piece(m0, n0, width, step):
        lhs_ref, w_ref, k0, k1 = steps[step]
        part = jnp.dot(lhs_ref[m0:m0 + MXU_ROWS, k0:k1], w_ref[k0:k1, n0:n0 + width], preferred_element_type=F32)
        if step:
            part = part + out_ref[m0:m0 + MXU_ROWS, n0:n0 + width]
        out_ref[m0:m0 + MXU_ROWS, n0:n0 + width] = part

    return [(piece, (m0, n0, w, j)) for j in range(len(steps)) for n0, w in _col_pieces(n) for m0 in range(0, m, MXU_ROWS)]


def _interleaved(vector_units, matmul_pieces):
    n_u, n_p = len(vector_units), len(matmul_pieces)
    done = 0
    for k, (unit, args) in enumerate(vector_units):
        while done < n_p and done * n_u <= k * n_p:
            matmul_pieces[done][0](*matmul_pieces[done][1])
            done += 1
        unit(*args)
    for fn, args in matmul_pieces[done:]:
        fn(*args)


def _params():
    return pltpu.CompilerParams(dimension_semantics=("arbitrary",), vmem_limit_bytes=VMEM_LIMIT)


def _sds(shape, dtype):
    return jax.ShapeDtypeStruct(shape, dtype)


def _sigmoid(v):
    return 0.5 * jnp.tanh(0.5 * v) + 0.5


def _conv_block(ext_ref, w_ref, r0, rc, l0, k_taps, transposed):
    acc = None
    for k in range(k_taps):
        d = (k_taps // 2 - k) if transposed else (k - k_taps // 2)
        term = ext_ref[l0 // LANES, pl.ds(r0 + HALO + d, rc), :] * w_ref[k:k + 1, l0:l0 + LANES]
        acc = term if acc is None else acc + term
    return acc


def _conv_wgrad_block(acc_ref, dout, ext_ref, r0, rc, l0, k_taps, scale=None):
    for k in range(k_taps):
        prod = dout * ext_ref[l0 // LANES, pl.ds(r0 + HALO + k - k_taps // 2, rc), :]
        part = prod.reshape(rc // SUBLANES, SUBLANES, LANES).sum(axis=0)
        if scale is not None:
            part = part * scale
        acc_ref[k, :, l0:l0 + LANES] = acc_ref[k, :, l0:l0 + LANES] + part


def _reduce_acc(out_ref, acc_ref, k_taps):
    for k in range(k_taps):
        out_ref[k:k + 1, :] = jnp.sum(acc_ref[k], axis=0, keepdims=True)


def _fold8(v):
    rc, c = v.shape
    return v.reshape(rc // SUBLANES, SUBLANES, c).sum(axis=0)


def _ext_scratch(ts, c):
    return pltpu.VMEM((c // LANES, ts + 2 * HALO, LANES), F32)


def _put_rows(ext_ref, r0, rc, val):
    for q in range(val.shape[1] // LANES):
        ext_ref[q, pl.ds(r0 + HALO, rc), :] = val[:, q * LANES:(q + 1) * LANES]


def _fill_halo(ext_ref, vals_prev, vals_next, ts, first, last):
    for q in range(vals_prev.shape[1] // LANES):
        cols = slice(q * LANES, (q + 1) * LANES)
        ext_ref[q, 0:HALO, :] = jnp.where(first, 0.0, vals_prev[:, cols])
        ext_ref[q, HALO + ts:HALO + ts + HALO, :] = jnp.where(last, 0.0, vals_next[:, cols])


def _rms_bwd_rows(dh, xf, g):
    r = lax.rsqrt(jnp.mean(xf * xf, axis=-1, keepdims=True) + RMS_EPS)
    xhat = xf * r
    dxh = dh * g
    dx = r * (dxh - xhat * jnp.mean(dxh * xhat, axis=-1, keepdims=True))
    return dx, dh * xhat


_NT = (((1,), (1,)), ((), ()))
_TN = (((0,), (0,)), ((), ()))


def _fwd_in(x, g1, w_in_t, ts, exchange=None):
    s = x.shape[0]

    def body(x_ref, g_ref, w_ref, z_ref, h_ref):
        xf = x_ref[...]
        r = lax.rsqrt(jnp.mean(xf * xf, axis=-1, keepdims=True) + RMS_EPS)
        h = (xf * r * g_ref[...]).astype(BF16)
        h_ref[...] = h
        for n0 in range(0, D_IN, 512):
            z_ref[:, n0:n0 + 512] = lax.dot_general(h, w_ref[n0:n0 + 512, :], _NT,
                                                    preferred_element_type=F32).astype(BF16)

    return _call(
        body, name="fwd_in", grid=(s // ts,),
        in_specs=[_rows(ts, D_MODEL), _const((1, D_MODEL)), _const((D_IN, D_MODEL))],
        out_specs=[_rows(ts, D_IN), _rows(ts, D_MODEL)],
        out_shape=[_sds((s, D_IN), BF16), _sds((s, D_MODEL), BF16)],
        scratch_shapes=[], args=(x, g1, w_in_t), exchange=exchange, forward_step=(s // ts) * 3 // 4)


def _p_u0(z_ref, rows):
    a_h = z_ref[rows, 0:D_A].astype(F32)
    a_c = z_ref[rows, 2 * D_A:3 * D_A].astype(F32)
    b_v = z_ref[rows, 3 * D_A:3 * D_A + D_B].astype(F32)
    b_g = z_ref[rows, 3 * D_A + D_B:D_IN].astype(F32)
    return a_c * a_h, b_v * _sigmoid(b_g)


def _layernorm_rows(u_blocks):
    tot = None
    for ub in u_blocks:
        sm = jnp.sum(ub, axis=-1, keepdims=True)
        tot = sm if tot is None else tot + sm
    mu = tot * (1.0 / D_B)
    var = None
    for ub in u_blocks:
        sq = jnp.sum((ub - mu) * (ub - mu), axis=-1, keepdims=True)
        var = sq if var is None else var + sq
    rstd = lax.rsqrt(var * (1.0 / D_B) + LN_EPS)
    return mu, rstd


def _fwd_mix(z, x, wa, wb, bb, lg, lb, w_out, ts, exchange=None):
    s = x.shape[0]
    nt = s // ts
    rc = min(ROW_CHUNK, ts)

    def body(z_ref, zp_ref, zn_ref, x_ref, wa_ref, wb_ref, bb_ref, lg_ref, lb_ref, wo_ref,
             x2_ref, y_ref, u_ref, pe_ref, ue_ref):
        i = pl.program_id(0)
        pp, up = _p_u0(zp_ref, slice(None))
        pn, un = _p_u0(zn_ref, slice(None))
        _fill_halo(pe_ref, pp, pn, ts, i == 0, i == nt - 1)
        _fill_halo(ue_ref, up, un, ts, i == 0, i == nt - 1)

        def fill(j, carry):
            r0 = pl.multiple_of(j * rc, rc)
            p, u0 = _p_u0(z_ref, pl.ds(r0, rc))
            _put_rows(pe_ref, r0, rc, p)
            _put_rows(ue_ref, r0, rc, u0)
            return carry

        lax.fori_loop(0, ts // rc, fill, 0)

        def main(j, carry):
            r0 = pl.multiple_of(j * rc, rc)
            rows = pl.ds(r0, rc)
            for l0 in range(0, D_A, LANES):
                ca = _conv_block(pe_ref, wa_ref, r0, rc, l0, K_A, False)
                a_b = z_ref[rows, D_A + l0:D_A + l0 + LANES].astype(F32)
                y_ref[rows, l0:l0 + LANES] = (a_b * ca).astype(BF16)
            ubs = []
            for l0 in range(0, D_B, LANES):
                ub = _conv_block(ue_ref, wb_ref, r0, rc, l0, K_B, False) + bb_ref[:, l0:l0 + LANES]
                u_ref[rows, l0:l0 + LANES] = ub
                ubs.append(ub)
            mu, rstd = _layernorm_rows(ubs)
            for q, l0 in enumerate(range(0, D_B, LANES)):
                t = (ubs[q] - mu) * rstd * lg_ref[:, l0:l0 + LANES] + lb_ref[:, l0:l0 + LANES]
                y_ref[rows, D_A + l0:D_A + l0 + LANES] = (t * _sigmoid(t)).astype(BF16)
            return carry

        lax.fori_loop(0, ts // rc, main, 0)
        for n0 in range(0, D_MODEL, 512):
            x2_ref[:, n0:n0 + 512] = x_ref[:, n0:n0 + 512] + jnp.dot(
                y_ref[...], wo_ref[:, n0:n0 + 512], preferred_element_type=F32)

    return _call(
        body, name="fwd_mix", grid=(nt,),
        in_specs=[_rows(ts, D_IN), _prev(ts, D_IN), _next(ts, D_IN, s), _rows(ts, D_MODEL),
                  _const((K_A, D_A)), _const((K_B, D_B)), _const((1, D_B)), _const((1, D_B)), _const((1, D_B)),
                  _const((D_MODEL, D_MODEL))],
        out_specs=[_rows(ts, D_MODEL), _rows(ts, D_MODEL), _rows(ts, D_B)],
        out_shape=[_sds((s, D_MODEL), F32), _sds((s, D_MODEL), BF16), _sds((s, D_B), F32)],
        scratch_shapes=[_ext_scratch(ts, D_A), _ext_scratch(ts, D_B)],
        args=(z, z, z, x, wa, wb, bb, lg, lb, w_out), exchange=exchange, forward_step=nt * 5 // 8)


def _fwd_ffn_in(x2, g2, w_gate_t, w_up_t, ts):
    s = x2.shape[0]
    half = D_FF // 2

    def body(x_ref, g_ref, wg_ref, wu_ref, g0_ref, v_ref, h_ref):
        xf = x_ref[...]
        r = lax.rsqrt(jnp.mean(xf * xf, axis=-1, keepdims=True) + RMS_EPS)
        h = (xf * r * g_ref[...]).astype(BF16)
        h_ref[...] = h
        for n0 in range(0, D_FF, half):
            g0_ref[:, n0:n0 + half] = lax.dot_general(h, wg_ref[n0:n0 + half, :], _NT,
                                                      preferred_element_type=F32).astype(BF16)
            v_ref[:, n0:n0 + half] = lax.dot_general(h, wu_ref[n0:n0 + half, :], _NT,
                                                     preferred_element_type=F32).astype(BF16)

    return pl.pallas_call(
        body, name="fwd_ffn_in", grid=(s // ts,),
        in_specs=[_rows(ts, D_MODEL), _const((1, D_MODEL)), _const((D_FF, D_MODEL)), _const((D_FF, D_MODEL))],
        out_specs=[_rows(ts, D_FF), _rows(ts, D_FF), _rows(ts, D_MODEL)],
        out_shape=[_sds((s, D_FF), BF16), _sds((s, D_FF), BF16), _sds((s, D_MODEL), BF16)],
        compiler_params=_params(),
    )(x2, g2, w_gate_t, w_up_t)


def _fwd_ffn_out(g0, v, x2, wf, w_down, g3, target, ts):
    s = x2.shape[0]
    nt = s // ts
    rc = min(ROW_CHUNK, ts)
    n_sub = ts // rc

    def body(g0_ref, gp_ref, gn_ref, v_ref, x2_ref, wf_ref, wd_ref, g3_ref, t_ref,
             a_ref, dx3_ref, dx3b_ref, loss_ref, dg3_ref, ge_ref, sums_ref, a0_ref, a1_ref, p0_ref, p1_ref):
        i = pl.program_id(0)
        vt = jnp.minimum(i, nt - 1)
        live = (i >= 2).astype(F32)

        @pl.when(i == 0)
        def _():
            sums_ref[...] = jnp.zeros_like(sums_ref)
            a1_ref[...] = jnp.zeros_like(a1_ref)
            p1_ref[...] = jnp.zeros_like(p1_ref)

        _fill_halo(ge_ref, gp_ref[...].astype(F32), gn_ref[...].astype(F32), ts, vt == 0, vt == nt - 1)

        def fill(j, carry):
            r0 = pl.multiple_of(j * rc, rc)
            _put_rows(ge_ref, r0, rc, g0_ref[pl.ds(r0, rc), :].astype(F32))
            return carry

        lax.fori_loop(0, n_sub, fill, 0)

        def stage(a_new, a_old, p_new, p_old):
            def act(r0, l0):
                rows = slice(r0, r0 + rc)
                g = _conv_block(ge_ref, wf_ref, r0, rc, l0, K_F, False)
                vv = v_ref[rows, l0:l0 + LANES].astype(F32)
                a = (g * _sigmoid(g) * vv).astype(BF16)
                a_ref[rows, l0:l0 + LANES] = a
                a_new[rows, l0:l0 + LANES] = a

            def tail(r0):
                rows = slice(r0, r0 + rc)
                x3 = x2_ref[rows, :] + p_old[rows, :]
                r = lax.rsqrt(jnp.mean(x3 * x3, axis=-1, keepdims=True) + RMS_EPS)
                xhat = x3 * r
                diff = xhat * g3_ref[...] - t_ref[rows, :]
                dout = diff * (1.0 / D_MODEL)
                dxh = dout * g3_ref[...]
                dx3 = r * (dxh - xhat * jnp.mean(dxh * xhat, axis=-1, keepdims=True))
                dx3_ref[rows, :] = dx3
                dx3b_ref[rows, :] = dx3.astype(BF16)
                sums_ref[0] = sums_ref[0] + _fold8(diff * diff) * live
                sums_ref[1] = sums_ref[1] + _fold8(dout * xhat) * live

            units = []
            for q in range(n_sub):
                units += [(act, (q * rc, l0)) for l0 in range(0, D_FF, LANES)]
                units.append((tail, (q * rc,)))
            _interleaved(units, _matmul_pieces([(a_old, wd_ref)], p_new, 2))

        @pl.when(i % 2 == 0)
        def _():
            stage(a0_ref, a1_ref, p0_ref, p1_ref)

        @pl.when(i % 2 == 1)
        def _():
            stage(a1_ref, a0_ref, p1_ref, p0_ref)

        @pl.when(i == nt + 1)
        def _():
            loss_ref[...] = (0.5 / D_MODEL) * jnp.sum(sums_ref[0], keepdims=True)
            dg3_ref[...] = jnp.sum(sums_ref[1], axis=0, keepdims=True)

    vtile = lambda i: jnp.minimum(i, nt - 1)
    ttile = lambda i: jnp.clip(i - 2, 0, nt - 1)
    return pl.pallas_call(
        body, name="fwd_ffn_out", grid=(nt + 2,),
        in_specs=[_rows_at(ts, D_FF, vtile), _prev_at(ts, D_FF, vtile), _next_at(ts, D_FF, s, vtile),
                  _rows_at(ts, D_FF, vtile), _rows_at(ts, D_MODEL, ttile),
                  _const((K_F, D_FF)), _const((D_FF, D_MODEL)), _const((1, D_MODEL)), _rows_at(ts, D_MODEL, ttile)],
        out_specs=[_rows_at(ts, D_FF, vtile), _rows_at(ts, D_MODEL, ttile), _rows_at(ts, D_MODEL, ttile),
                   _acc_out((1, 1)), _acc_out((1, D_MODEL))],
        out_shape=[_sds((s, D_FF), BF16), _sds((s, D_MODEL), F32), _sds((s, D_MODEL), BF16),
                   _sds((1, 1), F32), _sds((1, D_MODEL), F32)],
        scratch_shapes=[_ext_scratch(ts, D_FF), pltpu.VMEM((2, SUBLANES, D_MODEL), F32),
                        pltpu.VMEM((ts, D_FF), BF16), pltpu.VMEM((ts, D_FF), BF16),
                        pltpu.VMEM((ts, D_MODEL), F32), pltpu.VMEM((ts, D_MODEL), F32)],
        compiler_params=_params(),
    )(g0, g0, g0, v, x2, wf, w_down, g3, target)


def _bwd_ffn_a(dx3b, g0, v, a, w_down_t, wf, ts):
    s = dx3b.shape[0]
    nt = s // ts
    rc = min(ROW_CHUNK, ts)
    n_sub = ts // rc

    def body(dx_ref, g0_ref, gp_ref, gn_ref, v_ref, a_ref, wd_ref, wf_ref, dg_ref, dv_ref, dwf_ref, dwd_ref,
             ge_ref, da0_ref, da1_ref, acc_ref, dxs_ref, wacc_ref):
        i = pl.program_id(0)
        t = jnp.maximum(i - 1, 0)

        @pl.when(i == 0)
        def _():
            acc_ref[...] = jnp.zeros_like(acc_ref)
            da1_ref[...] = jnp.zeros_like(da1_ref)
            wacc_ref[...] = jnp.zeros_like(wacc_ref)

        dxs_ref[...] = dx_ref[...] * (i < nt).astype(BF16)
        _fill_halo(ge_ref, gp_ref[...].astype(F32), gn_ref[...].astype(F32), ts, t == 0, t == nt - 1)

        def fill(j, carry):
            r0 = pl.multiple_of(j * rc, rc)
            _put_rows(ge_ref, r0, rc, g0_ref[pl.ds(r0, rc), :].astype(F32))
            return carry

        lax.fori_loop(0, n_sub, fill, 0)

        def wgrad_piece(c0, width):
            wacc_ref[c0:c0 + width, :] = wacc_ref[c0:c0 + width, :] + lax.dot_general(
                a_ref[:, c0:c0 + width], dxs_ref[...], _TN, preferred_element_type=F32)

        def stage(da_new, da_old):
            def piece(m0, n0, width):
                da_new[m0:m0 + MXU_ROWS, n0:n0 + width] = jnp.dot(
                    dx_ref[m0:m0 + MXU_ROWS, :], wd_ref[:, n0:n0 + width], preferred_element_type=F32).astype(BF16)
                if m0 + MXU_ROWS >= ts:
                    wgrad_piece(n0, width)

            def unit(r0, l0):
                rows = slice(r0, r0 + rc)
                g = _conv_block(ge_ref, wf_ref, r0, rc, l0, K_F, False)
                sg = _sigmoid(g)
                da = da_old[rows, l0:l0 + LANES].astype(F32)
                vv = v_ref[rows, l0:l0 + LANES].astype(F32)
                dv_ref[rows, l0:l0 + LANES] = (da * (g * sg)).astype(BF16)
                dgg = da * vv * (sg * (1.0 + g * (1.0 - sg)))
                dg_ref[rows, l0:l0 + LANES] = dgg.astype(BF16)
                _conv_wgrad_block(acc_ref, dgg, ge_ref, r0, rc, l0, K_F)

            _interleaved([(unit, (q * rc, l0)) for q in range(n_sub) for l0 in range(0, D_FF, LANES)],
                         [(piece, (m0, n0, w)) for n0, w in _col_pieces(D_FF) for m0 in range(0, ts, MXU_ROWS)])

        @pl.when(i % 2 == 0)
        def _():
            stage(da0_ref, da1_ref)

        @pl.when(i % 2 == 1)
        def _():
            stage(da1_ref, da0_ref)

        @pl.when(i == nt)
        def _():
            _reduce_acc(dwf_ref, acc_ref, K_F)
            dwd_ref[...] = wacc_ref[...].astype(BF16)

    cur = lambda i: jnp.minimum(i, nt - 1)
    old = lambda i: jnp.maximum(i - 1, 0)
    return pl.pallas_call(
        body, name="bwd_ffn_a", grid=(nt + 1,),
        in_specs=[_rows_at(ts, D_MODEL, cur), _rows_at(ts, D_FF, old), _prev_at(ts, D_FF, old), _next_at(ts, D_FF, s, old),
                  _rows_at(ts, D_FF, old), _rows_at(ts, D_FF, cur), _const((D_MODEL, D_FF)), _const((K_F, D_FF))],
        out_specs=[_rows_at(ts, D_FF, old), _rows_at(ts, D_FF, old), _acc_out((K_F, D_FF)),
                   _const((D_FF, D_MODEL))],
        out_shape=[_sds((s, D_FF), BF16), _sds((s, D_FF), BF16), _sds((K_F, D_FF), F32), _sds((D_FF, D_MODEL), BF16)],
        scratch_shapes=[_ext_scratch(ts, D_FF), pltpu.VMEM((ts, D_FF), BF16), pltpu.VMEM((ts, D_FF), BF16),
                        pltpu.VMEM((K_F, SUBLANES, D_FF), F32), pltpu.VMEM((ts, D_MODEL), BF16),
                        pltpu.VMEM((D_FF, D_MODEL), F32)],
        compiler_params=_params(),
    )(dx3b, g0, g0, g0, v, a, w_down_t, wf)


def _bwd_ffn_b(dg, dv, wf, w_gate, w_up, x2, g2, dx3, ts):
    s = x2.shape[0]
    nt = s // ts
    rc = min(ROW_CHUNK, ts)
    n_sub = ts // rc

    def body(dg_ref, dgp_ref, dgn_ref, dv_ref, wf_ref, wg_ref, wu_ref, x2_ref, g2_ref, dx3_ref,
             dg0_ref, dx2_ref, dx2b_ref, dgn2_ref, dge_ref, dg8_ref, a0_ref, a1_ref, p0_ref, p1_ref):
        i = pl.program_id(0)
        vt = jnp.minimum(i, nt - 1)
        live = (i >= 2).astype(F32)

        @pl.when(i == 0)
        def _():
            dg8_ref[...] = jnp.zeros_like(dg8_ref)
            a1_ref[...] = jnp.zeros_like(a1_ref)
            p1_ref[...] = jnp.zeros_like(p1_ref)

        _fill_halo(dge_ref, dgp_ref[...].astype(F32), dgn_ref[...].astype(F32), ts, vt == 0, vt == nt - 1)

        def fill(j, carry):
            r0 = pl.multiple_of(j * rc, rc)
            _put_rows(dge_ref, r0, rc, dg_ref[pl.ds(r0, rc), :].astype(F32))
            return carry

        lax.fori_loop(0, n_sub, fill, 0)

        def stage(a_new, a_old, p_new, p_old):
            def conv_t(r0, l0):
                rows = slice(r0, r0 + rc)
                dg0 = _conv_block(dge_ref, wf_ref, r0, rc, l0, K_F, True).astype(BF16)
                dg0_ref[rows, l0:l0 + LANES] = dg0
                a_new[rows, l0:l0 + LANES] = dg0

            def tail(r0):
                rows = slice(r0, r0 + rc)
                dx, dgrow = _rms_bwd_rows(p_old[rows, :], x2_ref[rows, :], g2_ref[...])
                dx2 = dx3_ref[rows, :] + dx
                dx2_ref[rows, :] = dx2
                dx2b_ref[rows, :] = dx2.astype(BF16)
                dg8_ref[...] = dg8_ref[...] + _fold8(dgrow) * live

            units = []
            for q in range(n_sub):
                units += [(conv_t, (q * rc, l0)) for l0 in range(0, D_FF, LANES)]
                units.append((tail, (q * rc,)))
            _interleaved(units, _matmul_pieces([(a_old, wg_ref), (dv_ref, wu_ref)], p_new, 2))

        @pl.when(i % 2 == 0)
        def _():
            stage(a0_ref, a1_ref, p0_ref, p1_ref)

        @pl.when(i % 2 == 1)
        def _():
            stage(a1_ref, a0_ref, p1_ref, p0_ref)

        @pl.when(i == nt + 1)
        def _():
            dgn2_ref[...] = jnp.sum(dg8_ref[...], axis=0, keepdims=True)

    vtile = lambda i: jnp.minimum(i, nt - 1)
    mtile = lambda i: jnp.clip(i - 1, 0, nt - 1)
    ttile = lambda i: jnp.clip(i - 2, 0, nt - 1)
    return pl.pallas_call(
        body, name="bwd_ffn_b", grid=(nt + 2,),
        in_specs=[_rows_at(ts, D_FF, vtile), _prev_at(ts, D_FF, vtile), _next_at(ts, D_FF, s, vtile),
                  _rows_at(ts, D_FF, mtile), _const((K_F, D_FF)),
                  _const((D_FF, D_MODEL)), _const((D_FF, D_MODEL)), _rows_at(ts, D_MODEL, ttile), _const((1, D_MODEL)),
                  _rows_at(ts, D_MODEL, ttile)],
        out_specs=[_rows_at(ts, D_FF, vtile), _rows_at(ts, D_MODEL, ttile), _rows_at(ts, D_MODEL, ttile),
                   _acc_out((1, D_MODEL))],
        out_shape=[_sds((s, D_FF), BF16), _sds((s, D_MODEL), F32), _sds((s, D_MODEL), BF16), _sds((1, D_MODEL), F32)],
        scratch_shapes=[_ext_scratch(ts, D_FF), pltpu.VMEM((SUBLANES, D_MODEL), F32),
                        pltpu.VMEM((ts, D_FF), BF16), pltpu.VMEM((ts, D_FF), BF16),
                        pltpu.VMEM((ts, D_MODEL), F32), pltpu.VMEM((ts, D_MODEL), F32)],
        compiler_params=_params(),
    )(dg, dg, dg, dv, wf, w_gate, w_up, x2, g2, dx3)


def _bwd_mix_a(dx2b, w_out_t, z, u, wa, lg, lb, ts, exchange=None):
    s = dx2b.shape[0]
    nt = s // ts
    rc = min(ROW_CHUNK, ts)
    n_sub = ts // rc

    def body(dx_ref, wo_ref, z_ref, zp_ref, zn_ref, u_ref, wa_ref, lg_ref, lb_ref,
             dca_ref, du_ref, dab_ref, dwa_ref, dlg_ref, dlb_ref, dbb_ref, pe_ref, dy0_ref, dy1_ref, acc_ref, sacc_ref):
        i = pl.program_id(0)
        t = jnp.maximum(i - 1, 0)

        @pl.when(i == 0)
        def _():
            acc_ref[...] = jnp.zeros_like(acc_ref)
            sacc_ref[...] = jnp.zeros_like(sacc_ref)
            dy1_ref[...] = jnp.zeros_like(dy1_ref)

        pp, _ = _p_u0(zp_ref, slice(None))
        pn, _ = _p_u0(zn_ref, slice(None))
        _fill_halo(pe_ref, pp, pn, ts, t == 0, t == nt - 1)

        def fill(j, carry):
            r0 = pl.multiple_of(j * rc, rc)
            rows = pl.ds(r0, rc)
            _put_rows(pe_ref, r0, rc, z_ref[rows, 2 * D_A:3 * D_A].astype(F32) * z_ref[rows, 0:D_A].astype(F32))
            return carry

        lax.fori_loop(0, n_sub, fill, 0)

        def stage(dy_new, dy_old):
            def piece(m0, n0, width):
                dy_new[m0:m0 + MXU_ROWS, n0:n0 + width] = jnp.dot(
                    dx_ref[m0:m0 + MXU_ROWS, :], wo_ref[:, n0:n0 + width], preferred_element_type=F32).astype(BF16)

            units = []
            for q in range(n_sub):
                units += [(mixer_a, (dy_old, q * rc, l0)) for l0 in range(0, D_A, LANES)]
                units.append((mixer_b, (dy_old, q * rc)))
            _interleaved(units, [(piece, (m0, n0, w)) for n0, w in _col_pieces(D_MODEL) for m0 in range(0, ts, MXU_ROWS)])

        def mixer_a(dy_ref, r0, l0):
            rows = slice(r0, r0 + rc)
            ca = _conv_block(pe_ref, wa_ref, r0, rc, l0, K_A, False)
            a_b = z_ref[rows, D_A + l0:D_A + l0 + LANES].astype(F32)
            dya = dy_ref[rows, l0:l0 + LANES].astype(F32)
            dab_ref[rows, l0:l0 + LANES] = (dya * ca).astype(BF16)
            dca = dya * a_b
            dca_ref[rows, l0:l0 + LANES] = dca
            _conv_wgrad_block(acc_ref, dca, pe_ref, r0, rc, l0, K_A)

        def mixer_b(dy_ref, r0):
            rows = slice(r0, r0 + rc)
            ubs = [u_ref[rows, l0:l0 + LANES] for l0 in range(0, D_B, LANES)]
            mu, rstd = _layernorm_rows(ubs)
            ns, dns = [], []
            m1 = None
            m2 = None
            for q, l0 in enumerate(range(0, D_B, LANES)):
                n = (ubs[q] - mu) * rstd
                lgq = lg_ref[:, l0:l0 + LANES]
                t = n * lgq + lb_ref[:, l0:l0 + LANES]
                sg = _sigmoid(t)
                dt = dy_ref[rows, D_A + l0:D_A + l0 + LANES].astype(F32) * (sg * (1.0 + t * (1.0 - sg)))
                dn = dt * lgq
                ns.append(n)
                dns.append(dn)
                s1 = jnp.sum(dn, axis=-1, keepdims=True)
                s2 = jnp.sum(dn * n, axis=-1, keepdims=True)
                m1 = s1 if m1 is None else m1 + s1
                m2 = s2 if m2 is None else m2 + s2
                sacc_ref[0, :, l0:l0 + LANES] = sacc_ref[0, :, l0:l0 + LANES] + _fold8(dt * n)
                sacc_ref[1, :, l0:l0 + LANES] = sacc_ref[1, :, l0:l0 + LANES] + _fold8(dt)
            m1 = m1 * (1.0 / D_B)
            m2 = m2 * (1.0 / D_B)
            for q, l0 in enumerate(range(0, D_B, LANES)):
                du = rstd * (dns[q] - m1 - ns[q] * m2)
                du_ref[rows, l0:l0 + LANES] = du
                sacc_ref[2, :, l0:l0 + LANES] = sacc_ref[2, :, l0:l0 + LANES] + _fold8(du)

        @pl.when(i % 2 == 0)
        def _():
            stage(dy0_ref, dy1_ref)

        @pl.when(i % 2 == 1)
        def _():
            stage(dy1_ref, dy0_ref)

        @pl.when(i == nt)
        def _():
            _reduce_acc(dwa_ref, acc_ref, K_A)
            dlg_ref[...] = jnp.sum(sacc_ref[0], axis=0, keepdims=True)
            dlb_ref[...] = jnp.sum(sacc_ref[1], axis=0, keepdims=True)
            dbb_ref[...] = jnp.sum(sacc_ref[2], axis=0, keepdims=True)

    cur = lambda i: jnp.minimum(i, nt - 1)
    old = lambda i: jnp.maximum(i - 1, 0)
    return _call(
        body, name="bwd_mix_a", grid=(nt + 1,),
        in_specs=[_rows_at(ts, D_MODEL, cur), _const((D_MODEL, D_MODEL)), _rows_at(ts, D_IN, old), _prev_at(ts, D_IN, old),
                  _next_at(ts, D_IN, s, old), _rows_at(ts, D_B, old), _const((K_A, D_A)), _const((1, D_B)), _const((1, D_B))],
        out_specs=[_rows_at(ts, D_A, old), _rows_at(ts, D_B, old), _rows_at(ts, D_A, old), _acc_out((K_A, D_A)),
                   _acc_out((1, D_B)), _acc_out((1, D_B)), _acc_out((1, D_B))],
        out_shape=[_sds((s, D_A), F32), _sds((s, D_B), F32), _sds((s, D_A), BF16), _sds((K_A, D_A), F32),
                   _sds((1, D_B), F32), _sds((1, D_B), F32), _sds((1, D_B), F32)],
        scratch_shapes=[_ext_scratch(ts, D_A), pltpu.VMEM((ts, D_MODEL), BF16), pltpu.VMEM((ts, D_MODEL), BF16),
                        pltpu.VMEM((K_A, SUBLANES, D_A), F32), pltpu.VMEM((3, SUBLANES, D_B), F32)],
        args=(dx2b, w_out_t, z, z, z, u, wa, lg, lb), exchange=exchange)


def _bwd_mix_b(dca, du, z, dab, wa, wb, w_in, x, g1, dx2, ts, exchange=None):
    s = x.shape[0]
    nt = s // ts
    rc = min(ROW_CHUNK, ts)
    n_sub = ts // rc

    def body(dca_ref, dcap_ref, dcan_ref, du_ref, dup_ref, dun_ref, z_ref, zp_ref, zn_ref, dab_ref,
             wa_ref, wb_ref, wi_ref, x_ref, g1_ref, dx2_ref,
             dz_ref, dx_ref, dg1_ref, dwb_ref, dcae_ref, due_ref, ue_ref, acc_ref, dg8_ref,
             dz0_ref, dz1_ref, dh0_ref, dh1_ref):
        i = pl.program_id(0)

        @pl.when(i == 0)
        def _():
            acc_ref[...] = jnp.zeros_like(acc_ref)
            dg8_ref[...] = jnp.zeros_like(dg8_ref)
            dz1_ref[...] = jnp.zeros_like(dz1_ref)
            dh1_ref[...] = jnp.zeros_like(dh1_ref)

        vt = jnp.minimum(i, nt - 1)
        first = vt == 0
        last = vt == nt - 1
        live = (i < nt).astype(F32)
        _fill_halo(dcae_ref, dcap_ref[...], dcan_ref[...], ts, first, last)
        _fill_halo(due_ref, dup_ref[...], dun_ref[...], ts, first, last)
        _, up = _p_u0(zp_ref, slice(None))
        _, un = _p_u0(zn_ref, slice(None))
        _fill_halo(ue_ref, up, un, ts, first, last)

        def fill(j, carry):
            r0 = pl.multiple_of(j * rc, rc)
            rows = pl.ds(r0, rc)
            _put_rows(dcae_ref, r0, rc, dca_ref[rows, :])
            _put_rows(due_ref, r0, rc, du_ref[rows, :])
            b_v = z_ref[rows, 3 * D_A:3 * D_A + D_B].astype(F32)
            b_g = z_ref[rows, 3 * D_A + D_B:D_IN].astype(F32)
            _put_rows(ue_ref, r0, rc, b_v * _sigmoid(b_g))
            return carry

        lax.fori_loop(0, n_sub, fill, 0)

        def stage(dz_new, dz_old, dh_new, dh_old):
            def put(rows, c0, val):
                dz_ref[rows, c0:c0 + LANES] = val
                dz_new[rows, c0:c0 + LANES] = val

            def mixer_a(r0, l0):
                rows = slice(r0, r0 + rc)
                dp = _conv_block(dcae_ref, wa_ref, r0, rc, l0, K_A, True)
                a_h = z_ref[rows, l0:l0 + LANES].astype(F32)
                a_c = z_ref[rows, 2 * D_A + l0:2 * D_A + l0 + LANES].astype(F32)
                put(rows, l0, (dp * a_c).astype(BF16))
                put(rows, D_A + l0, dab_ref[rows, l0:l0 + LANES])
                put(rows, 2 * D_A + l0, (dp * a_h).astype(BF16))

            def mixer_b(r0, l0):
                rows = slice(r0, r0 + rc)
                du0 = _conv_block(due_ref, wb_ref, r0, rc, l0, K_B, True)
                b_v = z_ref[rows, 3 * D_A + l0:3 * D_A + l0 + LANES].astype(F32)
                b_g = z_ref[rows, 3 * D_A + D_B + l0:3 * D_A + D_B + l0 + LANES].astype(F32)
                sg = _sigmoid(b_g)
                put(rows, 3 * D_A + l0, (du0 * sg).astype(BF16))
                put(rows, 3 * D_A + D_B + l0, (du0 * b_v * (sg * (1.0 - sg))).astype(BF16))
                _conv_wgrad_block(acc_ref, du_ref[rows, l0:l0 + LANES], ue_ref, r0, rc, l0, K_B, live)

            def tail(r0):
                rows = slice(r0, r0 + rc)
                dx, dgrow = _rms_bwd_rows(dh_old[rows, :], x_ref[rows, :], g1_ref[...])
                dx_ref[rows, :] = dx2_ref[rows, :] + dx
                dg8_ref[...] = dg8_ref[...] + _fold8(dgrow)

            units = []
            for q in range(n_sub):
                units += [(mixer_a, (q * rc, l0)) for l0 in range(0, D_A, LANES)]
                units += [(mixer_b, (q * rc, l0)) for l0 in range(0, D_B, LANES)]
                units.append((tail, (q * rc,)))
            _interleaved(units, _matmul_pieces([(dz_old, wi_ref)], dh_new, 2))

        @pl.when(i % 2 == 0)
        def _():
            stage(dz0_ref, dz1_ref, dh0_ref, dh1_ref)

        @pl.when(i % 2 == 1)
        def _():
            stage(dz1_ref, dz0_ref, dh1_ref, dh0_ref)

        @pl.when(i == nt + 1)
        def _():
            _reduce_acc(dwb_ref, acc_ref, K_B)
            dg1_ref[...] = jnp.sum(dg8_ref[...], axis=0, keepdims=True)

    vtile = lambda i: jnp.minimum(i, nt - 1)
    ttile = lambda i: jnp.clip(i - 2, 0, nt - 1)
    return _call(
        body, name="bwd_mix_b", grid=(nt + 2,),
        in_specs=[_rows_at(ts, D_A, vtile), _prev_at(ts, D_A, vtile), _next_at(ts, D_A, s, vtile),
                  _rows_at(ts, D_B, vtile), _prev_at(ts, D_B, vtile), _next_at(ts, D_B, s, vtile),
                  _rows_at(ts, D_IN, vtile), _prev_at(ts, D_IN, vtile), _next_at(ts, D_IN, s, vtile), _rows_at(ts, D_A, vtile),
                  _const((K_A, D_A)), _const((K_B, D_B)), _const((D_IN, D_MODEL)), _rows_at(ts, D_MODEL, ttile),
                  _const((1, D_MODEL)), _rows_at(ts, D_MODEL, ttile)],
        out_specs=[_rows_at(ts, D_IN, vtile), _rows_at(ts, D_MODEL, ttile), _acc_out((1, D_MODEL)), _acc_out((K_B, D_B))],
        out_shape=[_sds((s, D_IN), BF16), _sds((s, D_MODEL), F32), _sds((1, D_MODEL), F32), _sds((K_B, D_B), F32)],
        scratch_shapes=[_ext_scratch(ts, D_A), _ext_scratch(ts, D_B), _ext_scratch(ts, D_B),
                        pltpu.VMEM((K_B, SUBLANES, D_B), F32), pltpu.VMEM((SUBLANES, D_MODEL), F32),
                        pltpu.VMEM((ts, D_IN), BF16), pltpu.VMEM((ts, D_IN), BF16),
                        pltpu.VMEM((ts, D_MODEL), F32), pltpu.VMEM((ts, D_MODEL), F32)],
        args=(dca, dca, dca, du, du, du, z, z, z, dab, wa, wb, w_in, x, g1, dx2), exchange=exchange)


def _matmul_tn(a, b, name):
    s, m = a.shape
    n = b.shape[1]
    tk = min(1024, s)
    nk = s // tk
    tm = 256

    def body(a_ref, b_ref, o_ref, acc_ref):
        k = pl.program_id(0)

        @pl.when(k == 0)
        def _():
            acc_ref[...] = jnp.zeros_like(acc_ref)

        for m0 in range(0, m, tm):
            acc_ref[m0:m0 + tm, :] = acc_ref[m0:m0 + tm, :] + lax.dot_general(
                a_ref[:, m0:m0 + tm], b_ref[...], _TN, preferred_element_type=F32)

        @pl.when(k == nk - 1)
        def _():
            o_ref[...] = acc_ref[...].astype(BF16)

    return pl.pallas_call(
        body, name=name, grid=(nk,),
        in_specs=[_rows(tk, m), _rows(tk, n)],
        out_specs=_acc_out((m, n)),
        out_shape=_sds((m, n), BF16),
        scratch_shapes=[pltpu.VMEM((m, n), F32)],
        compiler_params=_params(),
    )(a, b)


CHIP_RELS = ((1, 0, 0), (0, 1, 0), (1, 1, 0))
CORE_RELS = ((0, 0, 1),)
ALL_RELS = ((0, 0, 1), (0, 1, 0), (0, 1, 1), (1, 0, 0), (1, 0, 1), (1, 1, 0), (1, 1, 1))


def _chip_slot(dev):
    return 2 * dev[0] + dev[1]


def _dev_slot(dev):
    return 4 * dev[0] + 2 * dev[1] + dev[2]


def _me():
    return (lax.axis_index("x"), lax.axis_index("y"), lax.axis_index("c"))


def _peer(me, rel):
    return tuple((1 - me[a]) if rel[a] else me[a] for a in range(3))


_ANY = pl.BlockSpec(memory_space=pl.ANY)


class _Exchange:
    def __init__(self, inputs, out_shape, scratch, start, finish, forward=None):
        self.inputs, self.out_shape, self.scratch = list(inputs), list(out_shape), list(scratch)
        self.start, self.finish, self.forward = start, finish, forward


def _all_gather(payloads):
    n_p = len(payloads)
    n_k = 1 + 2 * len(CHIP_RELS)

    def copy(srcs, dsts, sems, p, k, block_dev, to, from_src):
        blk = dsts[p].at[_dev_slot(block_dev)]
        return pltpu.make_async_remote_copy(
            src_ref=srcs[p] if from_src else blk, dst_ref=blk,
            send_sem=sems[0].at[n_k * p + k], recv_sem=sems[1].at[n_k * p + k], device_id=to, device_id_type=MESH)

    def own_copy(srcs, dsts, sems, p):
        return pltpu.make_async_copy(srcs[p], dsts[p].at[_dev_slot(_me())], sems[2].at[p])

    def start(srcs, dsts, sems):
        me = _me()
        for p in range(n_p):
            own_copy(srcs, dsts, sems, p).start()
        for j, rel in enumerate(CHIP_RELS):
            for p in range(n_p):
                copy(srcs, dsts, sems, p, 1 + j, me, _peer(me, rel), True).start()
        for p in range(n_p):
            copy(srcs, dsts, sems, p, 0, me, _peer(me, CORE_RELS[0]), True).start()

    def forward(srcs, dsts, sems):
        me = _me()
        sibling = _peer(me, CORE_RELS[0])
        for j, rel in enumerate(CHIP_RELS):
            other = _peer(me, rel)
            for p in range(n_p):
                copy(srcs, dsts, sems, p, 1 + j, other, me, False).wait_recv()
                copy(srcs, dsts, sems, p, 4 + j, other, sibling, False).start()

    def finish(srcs, dsts, sems):
        me = _me()
        sibling = _peer(me, CORE_RELS[0])
        for p in range(n_p):
            copy(srcs, dsts, sems, p, 0, sibling, me, False).wait_recv()
        for j, rel in enumerate(CHIP_RELS):
            for p in range(n_p):
                copy(srcs, dsts, sems, p, 4 + j, _peer(sibling, rel), me, False).wait_recv()
        for p in range(n_p):
            own_copy(srcs, dsts, sems, p).wait()
            copy(srcs, dsts, sems, p, 0, me, sibling, True).wait_send()
            for j, rel in enumerate(CHIP_RELS):
                copy(srcs, dsts, sems, p, 1 + j, me, _peer(me, rel), True).wait_send()
                copy(srcs, dsts, sems, p, 4 + j, _peer(me, rel), sibling, False).wait_send()

    return _Exchange(
        payloads, [_sds((N_DEV,) + p.shape, p.dtype) for p in payloads],
        [pltpu.SemaphoreType.DMA((n_p * n_k,)), pltpu.SemaphoreType.DMA((n_p * n_k,)), pltpu.SemaphoreType.DMA((n_p,))],
        start, finish, forward)


def _gather_direct(payload):
    n_r = len(ALL_RELS)

    def copies(srcs, dsts, sems):
        me = _me()
        mine = dsts[0].at[_dev_slot(me)]
        own = pltpu.make_async_copy(srcs[0], mine, sems[2].at[0])
        remote = [pltpu.make_async_remote_copy(src_ref=srcs[0], dst_ref=mine, send_sem=sems[0].at[k], recv_sem=sems[1].at[k],
                                               device_id=_peer(me, rel), device_id_type=MESH)
                  for k, rel in enumerate(ALL_RELS)]
        return [own] + remote

    def start(srcs, dsts, sems):
        for cp in copies(srcs, dsts, sems):
            cp.start()

    def finish(srcs, dsts, sems):
        for cp in copies(srcs, dsts, sems):
            cp.wait()

    return _Exchange([payload], [_sds((N_DEV,) + payload.shape, payload.dtype)],
                     [pltpu.SemaphoreType.DMA((n_r,)), pltpu.SemaphoreType.DMA((n_r,)), pltpu.SemaphoreType.DMA((1,))],
                     start, finish)


def _scatter_exchange(payloads, rels, src_view, view_shapes):
    n_p = len(payloads)
    n_r = len(rels)

    def copies(srcs, dsts, sems):
        me = _me()
        out = []
        for k, rel in enumerate(rels):
            peer = _peer(me, rel)
            for p in range(n_p):
                out.append(pltpu.make_async_remote_copy(
                    src_ref=src_view(srcs[p], peer), dst_ref=dsts[p].at[k],
                    send_sem=sems[0].at[p * n_r + k], recv_sem=sems[1].at[p * n_r + k],
                    device_id=peer, device_id_type=MESH))
        return out

    def start(srcs, dsts, sems):
        for cp in copies(srcs, dsts, sems):
            cp.start()

    def finish(srcs, dsts, sems):
        for cp in copies(srcs, dsts, sems):
            cp.wait()

    return _Exchange(payloads, [_sds((n_r,) + vs, p.dtype) for vs, p in zip(view_shapes, payloads)],
                     [pltpu.SemaphoreType.DMA((n_p * n_r,)), pltpu.SemaphoreType.DMA((n_p * n_r,))], start, finish)


def _split_refs(refs, sizes):
    out, at = [], 0
    for n in sizes:
        out.append(refs[at:at + n])
        at += n
    return out


def _run_exchanges(name, exchanges):
    n_in = [len(e.inputs) for e in exchanges]
    n_out = [len(e.out_shape) for e in exchanges]
    n_sc = [len(e.scratch) for e in exchanges]

    def body(*refs):
        ins, outs, scs = _split_refs(refs, [sum(n_in), sum(n_out), sum(n_sc)])
        parts = list(zip(exchanges, _split_refs(ins, n_in), _split_refs(outs, n_out), _split_refs(scs, n_sc)))
        for e, i, o, s in parts:
            e.start(i, o, s)
        for e, i, o, s in parts:
            if e.forward is not None:
                e.forward(i, o, s)
        for e, i, o, s in parts:
            e.finish(i, o, s)

    outs = pl.pallas_call(
        body, name=name, in_specs=[_ANY] * sum(n_in), out_specs=[_ANY] * sum(n_out),
        out_shape=[sd for e in exchanges for sd in e.out_shape],
        scratch_shapes=[sc for e in exchanges for sc in e.scratch],
    )(*[a for e in exchanges for a in e.inputs])
    return _split_refs(list(outs), n_out)


def _call(body, *, name, grid, in_specs, out_specs, out_shape, scratch_shapes, args, exchange=None, forward_step=None):
    n_in, n_out, n_sc = len(in_specs), len(out_specs), len(scratch_shapes)
    if exchange is None:
        outs = pl.pallas_call(body, name=name, grid=grid, in_specs=in_specs, out_specs=out_specs, out_shape=out_shape,
                              scratch_shapes=scratch_shapes, compiler_params=_params())(*args)
        return list(outs), []
    e = exchange
    sizes = [n_in, len(e.inputs), n_out, len(e.out_shape), n_sc, len(e.scratch)]
    last = grid[0] - 1

    def wrapped(*refs):
        a, ei, o, eo, sc, es = _split_refs(refs, sizes)
        i = pl.program_id(0)

        @pl.when(i == 0)
        def _():
            e.start(ei, eo, es)

        if e.forward is not None:
            @pl.when(i == forward_step)
            def _():
                e.forward(ei, eo, es)

        body(*a, *o, *sc)

        @pl.when(i == last)
        def _():
            e.finish(ei, eo, es)

    outs = pl.pallas_call(
        wrapped, name=name, grid=grid,
        in_specs=list(in_specs) + [_ANY] * len(e.inputs), out_specs=list(out_specs) + [_ANY] * len(e.out_shape),
        out_shape=list(out_shape) + e.out_shape, scratch_shapes=list(scratch_shapes) + e.scratch,
        compiler_params=_params(),
    )(*args, *e.inputs)
    outs = list(outs)
    return outs[:n_out], outs[n_out:]


def _pair_sum(grads, recvd, my_core, name):
    n_p = len(grads)

    def body(c_ref, *refs):
        del c_ref
        for p in range(n_p):
            refs[2 * n_p + p][...] = (refs[p][...].astype(F32) + refs[n_p + p][...].astype(F32)).astype(BF16)

    def blk(g):
        return (None, None) + g.shape[2:]

    return pl.pallas_call(
        body, name=name,
        grid_spec=pltpu.PrefetchScalarGridSpec(
            num_scalar_prefetch=1, grid=(N_CHIP,),
            in_specs=[pl.BlockSpec(blk(g), lambda j, c: (j, c[0], 0, 0)) for g in grads]
            + [pl.BlockSpec(blk(g), lambda j, c: (0, j, 0, 0)) for g in grads],
            out_specs=[pl.BlockSpec((None,) + g.shape[2:], lambda j, c: (j, 0, 0)) for g in grads]),
        out_shape=[_sds((N_CHIP,) + g.shape[2:], BF16) for g in grads],
        compiler_params=pltpu.CompilerParams(dimension_semantics=("arbitrary",), vmem_limit_bytes=VMEM_LIMIT),
    )(my_core, *grads, *recvd)


def _chip_sum(psums, recvd, my_chip, name):
    n_p = len(psums)

    def body(c_ref, *refs):
        del c_ref
        for p in range(n_p):
            acc = refs[p][...].astype(F32)
            for k in range(len(CHIP_RELS)):
                acc = acc + refs[n_p + p][k].astype(F32)
            refs[2 * n_p + p][...] = acc

    return pl.pallas_call(
        body, name=name,
        grid_spec=pltpu.PrefetchScalarGridSpec(
            num_scalar_prefetch=1, grid=(1,),
            in_specs=[pl.BlockSpec((None,) + g.shape[1:], lambda i, c: (c[0], 0, 0)) for g in psums]
            + [pl.BlockSpec(r.shape, lambda i, c: (0, 0, 0)) for r in recvd],
            out_specs=[pl.BlockSpec(g.shape[1:], lambda i, c: (0, 0)) for g in psums]),
        out_shape=[_sds(g.shape[1:], F32) for g in psums],
        compiler_params=pltpu.CompilerParams(dimension_semantics=("arbitrary",), vmem_limit_bytes=VMEM_LIMIT),
    )(my_chip, *psums, *recvd)


def _sum_devices(parts):
    def body(p_ref, o_ref):
        acc = p_ref[0]
        for j in range(1, N_DEV):
            acc = acc + p_ref[j]
        o_ref[...] = acc

    return pl.pallas_call(body, name="small_grad_sum", out_shape=_sds(parts.shape[1:], F32))(parts)


def _adamw(w, g, m, v, name):
    def body(w_ref, g_ref, m_ref, v_ref, d_ref, mo_ref, vo_ref):
        gg = g_ref[...]
        mn = ADAM_B1 * m_ref[...] + (1.0 - ADAM_B1) * gg
        vn = ADAM_B2 * v_ref[...] + (1.0 - ADAM_B2) * (gg * gg)
        m_hat = mn / (1.0 - ADAM_B1 ** ADAM_STEP)
        v_hat = vn / (1.0 - ADAM_B2 ** ADAM_STEP)
        d_ref[...] = -ADAM_LR * (m_hat / (jnp.sqrt(v_hat) + ADAM_EPS) + ADAM_WD * w_ref[...])
        mo_ref[...] = mn
        vo_ref[...] = vn

    sd = _sds(w.shape, F32)
    return pl.pallas_call(body, name=name, out_shape=[sd, sd, sd],
                          compiler_params=pltpu.CompilerParams(vmem_limit_bytes=VMEM_LIMIT))(w, g, m, v)


class _Mesh:
    def __init__(self, shards, my_chip, my_core):
        self.shards, self.my_chip, self.my_core = shards, my_chip.reshape(1), my_core.reshape(1)

    def gather(self, names):
        return _all_gather([self.shards[n] for n in names])

    @staticmethod
    def whole(gathered):
        return gathered.reshape(N_DEV * gathered.shape[1], gathered.shape[2])

    @staticmethod
    def by_device(grads):
        return [g.reshape(N_CHIP, 2, g.shape[0] // N_DEV, g.shape[1]) for g in grads]

    @staticmethod
    def to_sibling(parts):
        return _scatter_exchange(parts, CORE_RELS, lambda ref, peer: ref.at[:, peer[2]],
                                 [(N_CHIP,) + p.shape[2:] for p in parts])

    @staticmethod
    def to_chips(pair):
        return _scatter_exchange(pair, CHIP_RELS, lambda ref, peer: ref.at[_chip_slot(peer)], [p.shape[1:] for p in pair])


def _step(x, target, g1, w_in_t, wa, wb, bb, lg, lb, w_out, g2, w_gate_t, w_up_t, wf, w_down, g3, ts, mesh=None):
    (z, h1), got = _fwd_in(x, g1, w_in_t, ts, exchange=mesh and mesh.gather(["w_out", "w_gate"]))
    if mesh:
        w_out, w_gate_t = [mesh.whole(g) for g in got]
    (x2, y, u), got = _fwd_mix(z, x, wa, wb, bb, lg, lb, w_out, ts, exchange=mesh and mesh.gather(["w_up", "w_down"]))
    if mesh:
        w_up_t, w_down = [mesh.whole(g) for g in got]
    g0, v, h2 = _fwd_ffn_in(x2, g2, w_gate_t, w_up_t, ts)
    tk = min(ts, SKEW_TILE)
    a, dx3, dx3b, loss, dg3 = _fwd_ffn_out(g0, v, x2, wf, w_down, g3, target, tk)
    dgc, dv, dwf, dw_down = _bwd_ffn_a(dx3b, g0, v, a, jnp.swapaxes(w_down, 0, 1), wf, tk)
    dg0, dx2, dx2b, dg2 = _bwd_ffn_b(dgc, dv, wf, w_gate_t, w_up_t, x2, g2, dx3, tk)
    first = dict(w_down=dw_down, w_gate=_matmul_tn(dg0, h2, "wgrad_gate"),
                 w_up=_matmul_tn(dv, h2, "wgrad_up"), w_out=_matmul_tn(y, dx2b, "wgrad_out"))
    if mesh:
        parts = mesh.by_device(list(first.values()))
    (dca, du, dab, dwa, dlg, dlb, dbb), got = _bwd_mix_a(dx2b, jnp.swapaxes(w_out, 0, 1), z, u, wa, lg, lb, tk,
                                                         exchange=mesh and mesh.to_sibling(parts))
    if mesh:
        pair = _pair_sum(parts, got, mesh.my_core, "rs_pair_sum_first")
    (dz, dx, dg1, dwb), got = _bwd_mix_b(dca, du, z, dab, wa, wb, w_in_t, x, g1, dx2, tk,
                                         exchange=mesh and mesh.to_chips(pair))
    dw_in_t = _matmul_tn(dz, h1, "wgrad_in")
    small = dict(norm_mix_g=dg1, conv_a_w=dwa, conv_b_w=dwb, conv_b_b=dbb, ln_b_g=dlg, ln_b_b=dlb,
                 norm_ffn_g=dg2, conv_ffn_w=dwf, norm_final_g=dg3)
    if not mesh:
        return loss, dx, dict(w_in=dw_in_t, **first), small
    big = dict(zip(first, _chip_sum(pair, got, mesh.my_chip, "rs_chip_sum_first")))
    parts = mesh.by_device([dw_in_t])
    (got,) = _run_exchanges("rs_cores_last", [mesh.to_sibling(parts)])
    pair = _pair_sum(parts, got, mesh.my_core, "rs_pair_sum_last")
    got, (every,) = _run_exchanges("rs_chips_last", [mesh.to_chips(pair), _gather_direct(_pack_small_grads(small, loss))])
    (big["w_in"],) = _chip_sum(pair, got, mesh.my_chip, "rs_chip_sum_last")
    return None, dx, big, _unpack_small_grads(_sum_devices(every))


def _pack_small_weights(conv_a_s, conv_b_s, conv_ffn_s):
    buf = jnp.zeros((SMALL_W_ROWS, SMALL_W_COLS), F32)
    buf = buf.at[0:K_A, 0:CONV_A_COLS].set(conv_a_s)
    buf = buf.at[K_A:K_A + K_B, 0:CONV_A_COLS].set(conv_b_s)
    return buf.at[K_A + K_B:K_A + K_B + K_F, 0:W_FF_COLS].set(conv_ffn_s)


def _unpack_small_weights(full):
    def take(r0, k, w):
        return jnp.transpose(full[:, r0:r0 + k, 0:w], (1, 0, 2)).reshape(k, N_DEV * w)

    return take(0, K_A, CONV_A_COLS), take(K_A, K_B, CONV_A_COLS), take(K_A + K_B, K_F, W_FF_COLS)


def _up8(n):
    return -(-n // SUBLANES) * SUBLANES


_SMALL_LAYOUT = (
    ("norm_mix_g", 2), ("norm_ffn_g", 2), ("norm_final_g", 2), ("conv_b_b", 1), ("ln_b_g", 1), ("ln_b_b", 1),
    ("conv_a_w", K_A), ("conv_b_w", K_B), ("conv_ffn_w", K_F * (FF_PAD // SMALL_G_COLS)), ("loss", 1))
SMALL_G_ROWS = sum(_up8(n) for _, n in _SMALL_LAYOUT)


def _pack_small_grads(small, loss):
    parts = []
    for name, n_rows in _SMALL_LAYOUT:
        if name == "loss":
            part = jnp.broadcast_to(loss.reshape(1, 1), (1, SMALL_G_COLS))
        elif name == "conv_ffn_w":
            part = jnp.pad(small[name], ((0, 0), (0, FF_PAD - D_FF))).reshape(n_rows, SMALL_G_COLS)
        else:
            part = small[name].reshape(n_rows, SMALL_G_COLS)
        parts.append(jnp.pad(part, ((0, _up8(n_rows) - n_rows), (0, 0))))
    return jnp.concatenate(parts, axis=0)


def _unpack_small_grads(tot):
    out = {}
    r = 0
    for name, n_rows in _SMALL_LAYOUT:
        blk = tot[r:r + n_rows]
        r += _up8(n_rows)
        if name == "loss":
            out[name] = blk[0, 0]
        elif name == "conv_ffn_w":
            out[name] = blk.reshape(K_F, FF_PAD)[:, 0:D_FF]
        elif name in ("conv_a_w", "conv_b_w"):
            out[name] = blk
        else:
            out[name] = blk.reshape(1, n_rows * SMALL_G_COLS)
    return out


def kernel(x, norm_mix_g, w_in, conv_a_w, conv_b_w, conv_b_b, ln_b_g, ln_b_b, w_out, norm_ffn_g, w_gate, w_up, conv_ffn_w, w_down, norm_final_g, loss_target, m_norm_mix_g, m_w_in, m_conv_a_w, m_conv_b_w, m_conv_b_b, m_ln_b_g, m_ln_b_b, m_w_out, m_norm_ffn_g, m_w_gate, m_w_up, m_conv_ffn_w, m_w_down, m_norm_final_g, v_norm_mix_g, v_w_in, v_conv_a_w, v_conv_b_w, v_conv_b_b, v_ln_b_g, v_ln_b_b, v_w_out, v_norm_ffn_g, v_w_gate, v_w_up, v_conv_ffn_w, v_w_down, v_norm_final_g):
    ix, iy, ic = lax.axis_index("x"), lax.axis_index("y"), lax.axis_index("c")
    my_chip = (2 * ix + iy).astype(jnp.int32)
    my_core = ic.astype(jnp.int32)
    my_dev = 2 * my_chip + my_core

    weights = dict(norm_mix_g=norm_mix_g, w_in=w_in, conv_a_w=conv_a_w, conv_b_w=conv_b_w, conv_b_b=conv_b_b,
                   ln_b_g=ln_b_g, ln_b_b=ln_b_b, w_out=w_out, norm_ffn_g=norm_ffn_g, w_gate=w_gate, w_up=w_up,
                   conv_ffn_w=conv_ffn_w, w_down=w_down, norm_final_g=norm_final_g)
    m_in = dict(norm_mix_g=m_norm_mix_g, w_in=m_w_in, conv_a_w=m_conv_a_w, conv_b_w=m_conv_b_w, conv_b_b=m_conv_b_b,
                ln_b_g=m_ln_b_g, ln_b_b=m_ln_b_b, w_out=m_w_out, norm_ffn_g=m_norm_ffn_g, w_gate=m_w_gate,
                w_up=m_w_up, conv_ffn_w=m_conv_ffn_w, w_down=m_w_down, norm_final_g=m_norm_final_g)
    v_in = dict(norm_mix_g=v_norm_mix_g, w_in=v_w_in, conv_a_w=v_conv_a_w, conv_b_w=v_conv_b_w, conv_b_b=v_conv_b_b,
                ln_b_g=v_ln_b_g, ln_b_b=v_ln_b_b, w_out=v_w_out, norm_ffn_g=v_norm_ffn_g, w_gate=v_w_gate,
                w_up=v_w_up, conv_ffn_w=v_conv_ffn_w, w_down=v_w_down, norm_final_g=v_norm_final_g)
    order = list(weights)
    big_names = ("w_in", "w_gate", "w_up", "w_out", "w_down")
    transposed = ("w_in", "w_gate", "w_up")

    def shard2d(name, a):
        if name in transposed:
            return jnp.swapaxes(a[0], 0, 1)
        return a.reshape(1, a.shape[0]) if a.ndim == 1 else a.reshape(a.shape[-2:])

    def unshard2d(name, a2, like):
        if name in transposed:
            return jnp.swapaxes(a2, 0, 1)[None]
        return a2.reshape(like.shape)

    mesh = _Mesh({n: shard2d(n, weights[n]).astype(BF16) for n in big_names}, my_chip, my_core)
    gathered, = _run_exchanges("ag_first", [_all_gather(
        [mesh.shards["w_in"], _pack_small_weights(conv_a_w[0], conv_b_w[0], conv_ffn_w[0])])])
    w_in_t = mesh.whole(gathered[0])
    wa_f, wb_f, wf_f = _unpack_small_weights(gathered[1])

    _, dx, gsum, stot = _step(
        x[0], loss_target[0], norm_mix_g, w_in_t, wa_f, wb_f, conv_b_b, ln_b_g, ln_b_b, None, norm_ffn_g,
        None, None, wf_f, None, norm_final_g.reshape(1, D_MODEL), SEQ_TILE, mesh)

    grads2d = dict(
        norm_mix_g=stot["norm_mix_g"],
        conv_a_w=lax.dynamic_slice(stot["conv_a_w"], (0, my_dev * CONV_A_COLS), (K_A, CONV_A_COLS)),
        conv_b_w=lax.dynamic_slice(stot["conv_b_w"], (0, my_dev * CONV_A_COLS), (K_B, CONV_A_COLS)),
        conv_b_b=stot["conv_b_b"], ln_b_g=stot["ln_b_g"], ln_b_b=stot["ln_b_b"],
        norm_ffn_g=stot["norm_ffn_g"],
        conv_ffn_w=lax.dynamic_slice(stot["conv_ffn_w"], (0, my_dev * W_FF_COLS), (K_F, W_FF_COLS)),
        norm_final_g=stot["norm_final_g"],
        **gsum,
    )

    g_out, d_out, m_out, v_out = [], [], [], []
    for name in order:
        w = weights[name]
        g2d = grads2d[name]
        d, mn, vn = _adamw(shard2d(name, w), g2d, shard2d(name, m_in[name]), shard2d(name, v_in[name]), "adamw_" + name)
        g_out.append(unshard2d(name, g2d, w))
        d_out.append(unshard2d(name, d, w))
        m_out.append(unshard2d(name, mn, w))
        v_out.append(unshard2d(name, vn, w))

    return (stot["loss"], dx[None], *g_out, *d_out, *m_out, *v_out)
```

```python
import jax
import jax.numpy as jnp
from jax import lax
from jax.experimental import pallas as pl
from jax.experimental.pallas import tpu as pltpu

F32 = jnp.float32
BF16 = jnp.bfloat16

D_MODEL = 1024
D_A = 512
D_B = 512
D_IN = 3 * D_A + 2 * D_B
D_FF = 2816
K_A = 3
K_B = 31
K_F = 3
RMS_EPS = 1e-6
LN_EPS = 1e-5

ADAM_LR = 0.001
ADAM_B1 = 0.9
ADAM_B2 = 0.999
ADAM_EPS = 1e-08
ADAM_WD = 0.01
ADAM_STEP = 10

N_DEV = 8
N_CHIP = 4
LANES = 128
SUBLANES = 8
HALO = 16
ROW_CHUNK = 64
SEQ_TILE = 512
SKEW_TILE = 256
VMEM_LIMIT = 56 * 1024 * 1024

MESH = pl.DeviceIdType.MESH

W_FF_COLS = D_FF // N_DEV
CONV_A_COLS = D_A // N_DEV
SMALL_W_ROWS = 40
SMALL_W_COLS = 384
SMALL_G_COLS = 512
FF_PAD = 3072


def _rows(ts, c):
    return pl.BlockSpec((ts, c), lambda i: (i, 0))


def _const(shape):
    return pl.BlockSpec(shape, lambda i: (0,) * len(shape), pipeline_mode=pl.Buffered(1))


def _acc_out(shape):
    return pl.BlockSpec(shape, lambda i: (0,) * len(shape))


def _rows_at(ts, c, tile):
    return pl.BlockSpec((ts, c), lambda i: (tile(i), 0))


def _prev_at(ts, c, tile):
    return pl.BlockSpec((HALO, c), lambda i: (jnp.maximum(tile(i) * (ts // HALO) - 1, 0), 0))


def _next_at(ts, c, s, tile):
    last = s // HALO - 1
    return pl.BlockSpec((HALO, c), lambda i: (jnp.minimum((tile(i) + 1) * (ts // HALO), last), 0))


def _prev(ts, c):
    return _prev_at(ts, c, lambda i: i)


def _next(ts, c, s):
    return _next_at(ts, c, s, lambda i: i)


MXU_COLS = 256
MXU_ROWS = 256


def _col_pieces(n):
    return [(c0, min(MXU_COLS, n - c0)) for c0 in range(0, n, MXU_COLS)]


def _matmul_pieces(terms, out_ref, k_parts):
    m, n = out_ref.shape
    steps = []
    for lhs_ref, w_ref in terms:
        tiles = lhs_ref.shape[1] // MXU_COLS
        cuts = [MXU_COLS * (tiles * j // k_parts) for j in range(k_parts)] + [lhs_ref.shape[1]]
        steps += [(lhs_ref, w_ref, cuts[j], cuts[j + 1]) for j in range(k_parts)]

    def piece(m0, n0, width, step):
        lhs_ref, w_ref, k0, k1 = steps[step]
        part = jnp.dot(lhs_ref[m0:m0 + MXU_ROWS, k0:k1], w_ref[k0:k1, n0:n0 + width], preferred_element_type=F32)
        if step:
            part = part + out_ref[m0:m0 + MXU_ROWS, n0:n0 + width]
        out_ref[m0:m0 + MXU_ROWS, n0:n0 + width] = part

    return [(piece, (m0, n0, w, j)) for j in range(len(steps)) for n0, w in _col_pieces(n) for m0 in range(0, m, MXU_ROWS)]


def _interleaved(vector_units, matmul_pieces):
    n_u, n_p = len(vector_units), len(matmul_pieces)
    done = 0
    for k, (unit, args) in enumerate(vector_units):
        while done < n_p and done * n_u <= k * n_p:
            matmul_pieces[done][0](*matmul_pieces[done][1])
            done += 1
        unit(*args)
    for fn, args in matmul_pieces[done:]:
        fn(*args)


def _params():
    return pltpu.CompilerParams(dimension_semantics=("arbitrary",), vmem_limit_bytes=VMEM_LIMIT)


def _sds(shape, dtype):
    return jax.ShapeDtypeStruct(shape, dtype)


def _sigmoid(v):
    return 0.5 * jnp.tanh(0.5 * v) + 0.5


def _conv_block(ext_ref, w_ref, r0, rc, l0, k_taps, transposed):
    acc = None
    for k in range(k_taps):
        d = (k_taps // 2 - k) if transposed else (k - k_taps // 2)
        term = ext_ref[l0 // LANES, pl.ds(r0 + HALO + d, rc), :] * w_ref[k:k + 1, l0:l0 + LANES]
        acc = term if acc is None else acc + term
    return acc


def _conv_wgrad_block(acc_ref, dout, ext_ref, r0, rc, l0, k_taps, scale=None):
    for k in range(k_taps):
        prod = dout * ext_ref[l0 // LANES, pl.ds(r0 + HALO + k - k_taps // 2, rc), :]
        part = prod.reshape(rc // SUBLANES, SUBLANES, LANES).sum(axis=0)
        if scale is not None:
            part = part * scale
        acc_ref[k, :, l0:l0 + LANES] = acc_ref[k, :, l0:l0 + LANES] + part


def _reduce_acc(out_ref, acc_ref, k_taps):
    for k in range(k_taps):
        out_ref[k:k + 1, :] = jnp.sum(acc_ref[k], axis=0, keepdims=True)


def _fold8(v):
    rc, c = v.shape
    return v.reshape(rc // SUBLANES, SUBLANES, c).sum(axis=0)


def _ext_scratch(ts, c):
    return pltpu.VMEM((c // LANES, ts + 2 * HALO, LANES), F32)


def _put_rows(ext_ref, r0, rc, val):
    for q in range(val.shape[1] // LANES):
        ext_ref[q, pl.ds(r0 + HALO, rc), :] = val[:, q * LANES:(q + 1) * LANES]


def _fill_halo(ext_ref, vals_prev, vals_next, ts, first, last):
    for q in range(vals_prev.shape[1] // LANES):
        cols = slice(q * LANES, (q + 1) * LANES)
        ext_ref[q, 0:HALO, :] = jnp.where(first, 0.0, vals_prev[:, cols])
        ext_ref[q, HALO + ts:HALO + ts + HALO, :] = jnp.where(last, 0.0, vals_next[:, cols])


def _rms_bwd_rows(dh, xf, g):
    r = lax.rsqrt(jnp.mean(xf * xf, axis=-1, keepdims=True) + RMS_EPS)
    xhat = xf * r
    dxh = dh * g
    dx = r * (dxh - xhat * jnp.mean(dxh * xhat, axis=-1, keepdims=True))
    return dx, dh * xhat


_NT = (((1,), (1,)), ((), ()))
_TN = (((0,), (0,)), ((), ()))


def _fwd_in(x, g1, w_in_t, ts, exchange=None):
    s = x.shape[0]

    def body(x_ref, g_ref, w_ref, z_ref, h_ref):
        xf = x_ref[...]
        r = lax.rsqrt(jnp.mean(xf * xf, axis=-1, keepdims=True) + RMS_EPS)
        h = (xf * r * g_ref[...]).astype(BF16)
        h_ref[...] = h
        for n0 in range(0, D_IN, 512):
            z_ref[:, n0:n0 + 512] = lax.dot_general(h, w_ref[n0:n0 + 512, :], _NT,
                                                    preferred_element_type=F32).astype(BF16)

    return _call(
        body, name="fwd_in", grid=(s // ts,),
        in_specs=[_rows(ts, D_MODEL), _const((1, D_MODEL)), _const((D_IN, D_MODEL))],
        out_specs=[_rows(ts, D_IN), _rows(ts, D_MODEL)],
        out_shape=[_sds((s, D_IN), BF16), _sds((s, D_MODEL), BF16)],
        scratch_shapes=[], args=(x, g1, w_in_t), exchange=exchange, forward_step=(s // ts) * 3 // 4)


def _p_u0(z_ref, rows):
    a_h = z_ref[rows, 0:D_A].astype(F32)
    a_c = z_ref[rows, 2 * D_A:3 * D_A].astype(F32)
    b_v = z_ref[rows, 3 * D_A:3 * D_A + D_B].astype(F32)
    b_g = z_ref[rows, 3 * D_A + D_B:D_IN].astype(F32)
    return a_c * a_h, b_v * _sigmoid(b_g)


def _layernorm_rows(u_blocks):
    tot = None
    for ub in u_blocks:
        sm = jnp.sum(ub, axis=-1, keepdims=True)
        tot = sm if tot is None else tot + sm
    mu = tot * (1.0 / D_B)
    var = None
    for ub in u_blocks:
        sq = jnp.sum((ub - mu) * (ub - mu), axis=-1, keepdims=True)
        var = sq if var is None else var + sq
    rstd = lax.rsqrt(var * (1.0 / D_B) + LN_EPS)
    return mu, rstd


def _fwd_mix(z, x, wa, wb, bb, lg, lb, w_out, ts, exchange=None):
    s = x.shape[0]
    nt = s // ts
    rc = min(ROW_CHUNK, ts)

    def body(z_ref, zp_ref, zn_ref, x_ref, wa_ref, wb_ref, bb_ref, lg_ref, lb_ref, wo_ref,
             x2_ref, y_ref, u_ref, pe_ref, ue_ref):
        i = pl.program_id(0)
        pp, up = _p_u0(zp_ref, slice(None))
        pn, un = _p_u0(zn_ref, slice(None))
        _fill_halo(pe_ref, pp, pn, ts, i == 0, i == nt - 1)
        _fill_halo(ue_ref, up, un, ts, i == 0, i == nt - 1)

        def fill(j, carry):
            r0 = pl.multiple_of(j * rc, rc)
            p, u0 = _p_u0(z_ref, pl.ds(r0, rc))
            _put_rows(pe_ref, r0, rc, p)
            _put_rows(ue_ref, r0, rc, u0)
            return carry

        lax.fori_loop(0, ts // rc, fill, 0)

        def main(j, carry):
            r0 = pl.multiple_of(j * rc, rc)
            rows = pl.ds(r0, rc)
            for l0 in range(0, D_A, LANES):
                ca = _conv_block(pe_ref, wa_ref, r0, rc, l0, K_A, False)
                a_b = z_ref[rows, D_A + l0:D_A + l0 + LANES].astype(F32)
                y_ref[rows, l0:l0 + LANES] = (a_b * ca).astype(BF16)
            ubs = []
            for l0 in range(0, D_B, LANES):
                ub = _conv_block(ue_ref, wb_ref, r0, rc, l0, K_B, False) + bb_ref[:, l0:l0 + LANES]
                u_ref[rows, l0:l0 + LANES] = ub
                ubs.append(ub)
            mu, rstd = _layernorm_rows(ubs)
            for q, l0 in enumerate(range(0, D_B, LANES)):
                t = (ubs[q] - mu) * rstd * lg_ref[:, l0:l0 + LANES] + lb_ref[:, l0:l0 + LANES]
                y_ref[rows, D_A + l0:D_A + l0 + LANES] = (t * _sigmoid(t)).astype(BF16)
            return carry

        lax.fori_loop(0, ts // rc, main, 0)
        for n0 in range(0, D_MODEL, 512):
            x2_ref[:, n0:n0 + 512] = x_ref[:, n0:n0 + 512] + jnp.dot(
                y_ref[...], wo_ref[:, n0:n0 + 512], preferred_element_type=F32)

    return _call(
        body, name="fwd_mix", grid=(nt,),
        in_specs=[_rows(ts, D_IN), _prev(ts, D_IN), _next(ts, D_IN, s), _rows(ts, D_MODEL),
                  _const((K_A, D_A)), _const((K_B, D_B)), _const((1, D_B)), _const((1, D_B)), _const((1, D_B)),
                  _const((D_MODEL, D_MODEL))],
        out_specs=[_rows(ts, D_MODEL), _rows(ts, D_MODEL), _rows(ts, D_B)],
        out_shape=[_sds((s, D_MODEL), F32), _sds((s, D_MODEL), BF16), _sds((s, D_B), F32)],
        scratch_shapes=[_ext_scratch(ts, D_A), _ext_scratch(ts, D_B)],
        args=(z, z, z, x, wa, wb, bb, lg, lb, w_out), exchange=exchange, forward_step=nt * 5 // 8)


def _fwd_ffn_in(x2, g2, w_gate_t, w_up_t, ts):
    s = x2.shape[0]
    half = D_FF // 2

    def body(x_ref, g_ref, wg_ref, wu_ref, g0_ref, v_ref, h_ref):
        xf = x_ref[...]
        r = lax.rsqrt(jnp.mean(xf * xf, axis=-1, keepdims=True) + RMS_EPS)
        h = (xf * r * g_ref[...]).astype(BF16)
        h_ref[...] = h
        for n0 in range(0, D_FF, half):
            g0_ref[:, n0:n0 + half] = lax.dot_general(h, wg_ref[n0:n0 + half, :], _NT,
                                                      preferred_element_type=F32).astype(BF16)
            v_ref[:, n0:n0 + half] = lax.dot_general(h, wu_ref[n0:n0 + half, :], _NT,
                                                     preferred_element_type=F32).astype(BF16)

    return pl.pallas_call(
        body, name="fwd_ffn_in", grid=(s // ts,),
        in_specs=[_rows(ts, D_MODEL), _const((1, D_MODEL)), _const((D_FF, D_MODEL)), _const((D_FF, D_MODEL))],
        out_specs=[_rows(ts, D_FF), _rows(ts, D_FF), _rows(ts, D_MODEL)],
        out_shape=[_sds((s, D_FF), BF16), _sds((s, D_FF), BF16), _sds((s, D_MODEL), BF16)],
        compiler_params=_params(),
    )(x2, g2, w_gate_t, w_up_t)


def _fwd_ffn_out(g0, v, x2, wf, w_down, g3, target, ts):
    s = x2.shape[0]
    nt = s // ts
    rc = min(ROW_CHUNK, ts)

    def body(g0_ref, gp_ref, gn_ref, v_ref, x2_ref, wf_ref, wd_ref, g3_ref, t_ref,
             a_ref, dx3_ref, dx3b_ref, loss_ref, dg3_ref, ge_ref):
        i = pl.program_id(0)
        _fill_halo(ge_ref, gp_ref[...].astype(F32), gn_ref[...].astype(F32), ts, i == 0, i == nt - 1)

        def fill(j, carry):
            r0 = pl.multiple_of(j * rc, rc)
            _put_rows(ge_ref, r0, rc, g0_ref[pl.ds(r0, rc), :].astype(F32))
            return carry

        lax.fori_loop(0, ts // rc, fill, 0)

        def act(j, carry):
            r0 = pl.multiple_of(j * rc, rc)
            rows = pl.ds(r0, rc)
            for l0 in range(0, D_FF, LANES):
                g = _conv_block(ge_ref, wf_ref, r0, rc, l0, K_F, False)
                vv = v_ref[rows, l0:l0 + LANES].astype(F32)
                a_ref[rows, l0:l0 + LANES] = (g * _sigmoid(g) * vv).astype(BF16)
            return carry

        lax.fori_loop(0, ts // rc, act, 0)
        for n0 in range(0, D_MODEL, 512):
            dx3_ref[:, n0:n0 + 512] = x2_ref[:, n0:n0 + 512] + jnp.dot(
                a_ref[...], wd_ref[:, n0:n0 + 512], preferred_element_type=F32)

        @pl.when(i == 0)
        def _():
            loss_ref[...] = jnp.zeros_like(loss_ref)
            dg3_ref[...] = jnp.zeros_like(dg3_ref)

        def tail(j, carry):
            lsum, dgsum = carry
            r0 = pl.multiple_of(j * rc, rc)
            rows = pl.ds(r0, rc)
            x3 = dx3_ref[rows, :]
            r = lax.rsqrt(jnp.mean(x3 * x3, axis=-1, keepdims=True) + RMS_EPS)
            xhat = x3 * r
            diff = xhat * g3_ref[...] - t_ref[rows, :]
            dout = diff * (1.0 / D_MODEL)
            dxh = dout * g3_ref[...]
            dx3 = r * (dxh - xhat * jnp.mean(dxh * xhat, axis=-1, keepdims=True))
            dx3_ref[rows, :] = dx3
            dx3b_ref[rows, :] = dx3.astype(BF16)
            lsum = lsum + _fold8(diff * diff)
            dgsum = dgsum + _fold8(dout * xhat)
            return lsum, dgsum

        zero = jnp.zeros((SUBLANES, D_MODEL), F32)
        lsum, dgsum = lax.fori_loop(0, ts // rc, tail, (zero, zero), unroll=2)
        loss_ref[...] = loss_ref[...] + (0.5 / D_MODEL) * jnp.sum(lsum, keepdims=True)
        dg3_ref[...] = dg3_ref[...] + jnp.sum(dgsum, axis=0, keepdims=True)

    return pl.pallas_call(
        body, name="fwd_ffn_out", grid=(nt,),
        in_specs=[_rows(ts, D_FF), _prev(ts, D_FF), _next(ts, D_FF, s), _rows(ts, D_FF), _rows(ts, D_MODEL),
                  _const((K_F, D_FF)), _const((D_FF, D_MODEL)), _const((1, D_MODEL)), _rows(ts, D_MODEL)],
        out_specs=[_rows(ts, D_FF), _rows(ts, D_MODEL), _rows(ts, D_MODEL), _acc_out((1, 1)), _acc_out((1, D_MODEL))],
        out_shape=[_sds((s, D_FF), BF16), _sds((s, D_MODEL), F32), _sds((s, D_MODEL), BF16),
                   _sds((1, 1), F32), _sds((1, D_MODEL), F32)],
        scratch_shapes=[_ext_scratch(ts, D_FF)],
        compiler_params=_params(),
    )(g0, g0, g0, v, x2, wf, w_down, g3, target)


def _bwd_ffn_a(dx3b, g0, v, w_down, wf, ts):
    s = dx3b.shape[0]
    nt = s // ts
    rc = min(ROW_CHUNK, ts)
    half = D_FF // 2

    def body(dx_ref, g0_ref, gp_ref, gn_ref, v_ref, wd_ref, wf_ref, dg_ref, dv_ref, dwf_ref, ge_ref, da_ref, acc_ref):
        i = pl.program_id(0)

        @pl.when(i == 0)
        def _():
            acc_ref[...] = jnp.zeros_like(acc_ref)

        _fill_halo(ge_ref, gp_ref[...].astype(F32), gn_ref[...].astype(F32), ts, i == 0, i == nt - 1)

        def fill(j, carry):
            r0 = pl.multiple_of(j * rc, rc)
            _put_rows(ge_ref, r0, rc, g0_ref[pl.ds(r0, rc), :].astype(F32))
            return carry

        lax.fori_loop(0, ts // rc, fill, 0)
        for n0 in range(0, D_FF, half):
            da_ref[:, n0:n0 + half] = lax.dot_general(dx_ref[...], wd_ref[n0:n0 + half, :], _NT,
                                                      preferred_element_type=F32)

        def main(j, carry):
            r0 = pl.multiple_of(j * rc, rc)
            rows = pl.ds(r0, rc)
            for l0 in range(0, D_FF, LANES):
                g = _conv_block(ge_ref, wf_ref, r0, rc, l0, K_F, False)
                sg = _sigmoid(g)
                silu = g * sg
                da = da_ref[rows, l0:l0 + LANES]
                vv = v_ref[rows, l0:l0 + LANES].astype(F32)
                dv_ref[rows, l0:l0 + LANES] = (da * silu).astype(BF16)
                dgg = (da * vv) * (sg + silu * (1.0 - sg))
                dg_ref[rows, l0:l0 + LANES] = dgg.astype(BF16)
                _conv_wgrad_block(acc_ref, dgg, ge_ref, r0, rc, l0, K_F)
            return carry

        lax.fori_loop(0, ts // rc, main, 0)

        @pl.when(i == nt - 1)
        def _():
            _reduce_acc(dwf_ref, acc_ref, K_F)

    return pl.pallas_call(
        body, name="bwd_ffn_a", grid=(nt,),
        in_specs=[_rows(ts, D_MODEL), _rows(ts, D_FF), _prev(ts, D_FF), _next(ts, D_FF, s), _rows(ts, D_FF),
                  _const((D_FF, D_MODEL)), _const((K_F, D_FF))],
        out_specs=[_rows(ts, D_FF), _rows(ts, D_FF), _acc_out((K_F, D_FF))],
        out_shape=[_sds((s, D_FF), BF16), _sds((s, D_FF), BF16), _sds((K_F, D_FF), F32)],
        scratch_shapes=[_ext_scratch(ts, D_FF), pltpu.VMEM((ts, D_FF), F32),
                        pltpu.VMEM((K_F, SUBLANES, D_FF), F32)],
        compiler_params=_params(),
    )(dx3b, g0, g0, g0, v, w_down, wf)


def _bwd_ffn_b(dg, dv, wf, w_gate, w_up, x2, g2, dx3, ts):
    s = x2.shape[0]
    nt = s // ts
    rc = min(ROW_CHUNK, ts)
    n_sub = ts // rc

    def body(dg_ref, dgp_ref, dgn_ref, dv_ref, wf_ref, wg_ref, wu_ref, x2_ref, g2_ref, dx3_ref,
             dg0_ref, dx2_ref, dx2b_ref, dgn2_ref, dge_ref, dg8_ref, a0_ref, a1_ref, p0_ref, p1_ref):
        i = pl.program_id(0)
        vt = jnp.minimum(i, nt - 1)
        live = (i >= 2).astype(F32)

        @pl.when(i == 0)
        def _():
            dg8_ref[...] = jnp.zeros_like(dg8_ref)
            a1_ref[...] = jnp.zeros_like(a1_ref)
            p1_ref[...] = jnp.zeros_like(p1_ref)

        _fill_halo(dge_ref, dgp_ref[...].astype(F32), dgn_ref[...].astype(F32), ts, vt == 0, vt == nt - 1)

        def fill(j, carry):
            r0 = pl.multiple_of(j * rc, rc)
            _put_rows(dge_ref, r0, rc, dg_ref[pl.ds(r0, rc), :].astype(F32))
            return carry

        lax.fori_loop(0, n_sub, fill, 0)

        def stage(a_new, a_old, p_new, p_old):
            def conv_t(r0, l0):
                rows = slice(r0, r0 + rc)
                dg0 = _conv_block(dge_ref, wf_ref, r0, rc, l0, K_F, True).astype(BF16)
                dg0_ref[rows, l0:l0 + LANES] = dg0
                a_new[rows, l0:l0 + LANES] = dg0

            def tail(r0):
                rows = slice(r0, r0 + rc)
                dx, dgrow = _rms_bwd_rows(p_old[rows, :], x2_ref[rows, :], g2_ref[...])
                dx2 = dx3_ref[rows, :] + dx
                dx2_ref[rows, :] = dx2
                dx2b_ref[rows, :] = dx2.astype(BF16)
                dg8_ref[...] = dg8_ref[...] + _fold8(dgrow) * live

            units = []
            for q in range(n_sub):
                units += [(conv_t, (q * rc, l0)) for l0 in range(0, D_FF, LANES)]
                units.append((tail, (q * rc,)))
            _interleaved(units, _matmul_pieces([(a_old, wg_ref), (dv_ref, wu_ref)], p_new, 2))

        @pl.when(i % 2 == 0)
        def _():
            stage(a0_ref, a1_ref, p0_ref, p1_ref)

        @pl.when(i % 2 == 1)
        def _():
            stage(a1_ref, a0_ref, p1_ref, p0_ref)

        @pl.when(i == nt + 1)
        def _():
            dgn2_ref[...] = jnp.sum(dg8_ref[...], axis=0, keepdims=True)

    vtile = lambda i: jnp.minimum(i, nt - 1)
    mtile = lambda i: jnp.clip(i - 1, 0, nt - 1)
    ttile = lambda i: jnp.clip(i - 2, 0, nt - 1)
    return pl.pallas_call(
        body, name="bwd_ffn_b", grid=(nt + 2,),
        in_specs=[_rows_at(ts, D_FF, vtile), _prev_at(ts, D_FF, vtile), _next_at(ts, D_FF, s, vtile),
                  _rows_at(ts, D_FF, mtile), _const((K_F, D_FF)),
                  _const((D_FF, D_MODEL)), _const((D_FF, D_MODEL)), _rows_at(ts, D_MODEL, ttile), _const((1, D_MODEL)),
                  _rows_at(ts, D_MODEL, ttile)],
        out_specs=[_rows_at(ts, D_FF, vtile), _rows_at(ts, D_MODEL, ttile), _rows_at(ts, D_MODEL, ttile),
                   _acc_out((1, D_MODEL))],
        out_shape=[_sds((s, D_FF), BF16), _sds((s, D_MODEL), F32), _sds((s, D_MODEL), BF16), _sds((1, D_MODEL), F32)],
        scratch_shapes=[_ext_scratch(ts, D_FF), pltpu.VMEM((SUBLANES, D_MODEL), F32),
                        pltpu.VMEM((ts, D_FF), BF16), pltpu.VMEM((ts, D_FF), BF16),
                        pltpu.VMEM((ts, D_MODEL), F32), pltpu.VMEM((ts, D_MODEL), F32)],
        compiler_params=_params(),
    )(dg, dg, dg, dv, wf, w_gate, w_up, x2, g2, dx3)


def _bwd_mix_a(dx2b, w_out_t, z, u, wa, lg, lb, ts, exchange=None):
    s = dx2b.shape[0]
    nt = s // ts
    rc = min(ROW_CHUNK, ts)
    n_sub = ts // rc

    def body(dx_ref, wo_ref, z_ref, zp_ref, zn_ref, u_ref, wa_ref, lg_ref, lb_ref,
             dca_ref, du_ref, dab_ref, dwa_ref, dlg_ref, dlb_ref, dbb_ref, pe_ref, dy0_ref, dy1_ref, acc_ref, sacc_ref):
        i = pl.program_id(0)
        t = jnp.maximum(i - 1, 0)

        @pl.when(i == 0)
        def _():
            acc_ref[...] = jnp.zeros_like(acc_ref)
            sacc_ref[...] = jnp.zeros_like(sacc_ref)
            dy1_ref[...] = jnp.zeros_like(dy1_ref)

        pp, _ = _p_u0(zp_ref, slice(None))
        pn, _ = _p_u0(zn_ref, slice(None))
        _fill_halo(pe_ref, pp, pn, ts, t == 0, t == nt - 1)

        def fill(j, carry):
            r0 = pl.multiple_of(j * rc, rc)
            rows = pl.ds(r0, rc)
            _put_rows(pe_ref, r0, rc, z_ref[rows, 2 * D_A:3 * D_A].astype(F32) * z_ref[rows, 0:D_A].astype(F32))
            return carry

        lax.fori_loop(0, n_sub, fill, 0)

        def stage(dy_new, dy_old):
            def piece(m0, n0, width):
                dy_new[m0:m0 + MXU_ROWS, n0:n0 + width] = jnp.dot(
                    dx_ref[m0:m0 + MXU_ROWS, :], wo_ref[:, n0:n0 + width], preferred_element_type=F32).astype(BF16)

            units = []
            for q in range(n_sub):
                units += [(mixer_a, (dy_old, q * rc, l0)) for l0 in range(0, D_A, LANES)]
                units.append((mixer_b, (dy_old, q * rc)))
            _interleaved(units, [(piece, (m0, n0, w)) for n0, w in _col_pieces(D_MODEL) for m0 in range(0, ts, MXU_ROWS)])

        def mixer_a(dy_ref, r0, l0):
            rows = slice(r0, r0 + rc)
            ca = _conv_block(pe_ref, wa_ref, r0, rc, l0, K_A, False)
            a_b = z_ref[rows, D_A + l0:D_A + l0 + LANES].astype(F32)
            dya = dy_ref[rows, l0:l0 + LANES].astype(F32)
            dab_ref[rows, l0:l0 + LANES] = (dya * ca).astype(BF16)
            dca = dya * a_b
            dca_ref[rows, l0:l0 + LANES] = dca
            _conv_wgrad_block(acc_ref, dca, pe_ref, r0, rc, l0, K_A)

        def mixer_b(dy_ref, r0):
            rows = slice(r0, r0 + rc)
            ubs = [u_ref[rows, l0:l0 + LANES] for l0 in range(0, D_B, LANES)]
            mu, rstd = _layernorm_rows(ubs)
            ns, dns = [], []
            m1 = None
            m2 = None
            for q, l0 in enumerate(range(0, D_B, LANES)):
                n = (ubs[q] - mu) * rstd
                lgq = lg_ref[:, l0:l0 + LANES]
                t = n * lgq + lb_ref[:, l0:l0 + LANES]
                sg = _sigmoid(t)
                dt = dy_ref[rows, D_A + l0:D_A + l0 + LANES].astype(F32) * (sg * (1.0 + t * (1.0 - sg)))
                dn = dt * lgq
                ns.append(n)
                dns.append(dn)
                s1 = jnp.sum(dn, axis=-1, keepdims=True)
                s2 = jnp.sum(dn * n, axis=-1, keepdims=True)
                m1 = s1 if m1 is None else m1 + s1
                m2 = s2 if m2 is None else m2 + s2
                sacc_ref[0, :, l0:l0 + LANES] = sacc_ref[0, :, l0:l0 + LANES] + _fold8(dt * n)
                sacc_ref[1, :, l0:l0 + LANES] = sacc_ref[1, :, l0:l0 + LANES] + _fold8(dt)
            m1 = m1 * (1.0 / D_B)
            m2 = m2 * (1.0 / D_B)
            for q, l0 in enumerate(range(0, D_B, LANES)):
                du = rstd * (dns[q] - m1 - ns[q] * m2)
                du_ref[rows, l0:l0 + LANES] = du
                sacc_ref[2, :, l0:l0 + LANES] = sacc_ref[2, :, l0:l0 + LANES] + _fold8(du)

        @pl.when(i % 2 == 0)
        def _():
            stage(dy0_ref, dy1_ref)

        @pl.when(i % 2 == 1)
        def _():
            stage(dy1_ref, dy0_ref)

        @pl.when(i == nt)
        def _():
            _reduce_acc(dwa_ref, acc_ref, K_A)
            dlg_ref[...] = jnp.sum(sacc_ref[0], axis=0, keepdims=True)
            dlb_ref[...] = jnp.sum(sacc_ref[1], axis=0, keepdims=True)
            dbb_ref[...] = jnp.sum(sacc_ref[2], axis=0, keepdims=True)

    cur = lambda i: jnp.minimum(i, nt - 1)
    old = lambda i: jnp.maximum(i - 1, 0)
    return _call(
        body, name="bwd_mix_a", grid=(nt + 1,),
        in_specs=[_rows_at(ts, D_MODEL, cur), _const((D_MODEL, D_MODEL)), _rows_at(ts, D_IN, old), _prev_at(ts, D_IN, old),
                  _next_at(ts, D_IN, s, old), _rows_at(ts, D_B, old), _const((K_A, D_A)), _const((1, D_B)), _const((1, D_B))],
        out_specs=[_rows_at(ts, D_A, old), _rows_at(ts, D_B, old), _rows_at(ts, D_A, old), _acc_out((K_A, D_A)),
                   _acc_out((1, D_B)), _acc_out((1, D_B)), _acc_out((1, D_B))],
        out_shape=[_sds((s, D_A), F32), _sds((s, D_B), F32), _sds((s, D_A), BF16), _sds((K_A, D_A), F32),
                   _sds((1, D_B), F32), _sds((1, D_B), F32), _sds((1, D_B), F32)],
        scratch_shapes=[_ext_scratch(ts, D_A), pltpu.VMEM((ts, D_MODEL), BF16), pltpu.VMEM((ts, D_MODEL), BF16),
                        pltpu.VMEM((K_A, SUBLANES, D_A), F32), pltpu.VMEM((3, SUBLANES, D_B), F32)],
        args=(dx2b, w_out_t, z, z, z, u, wa, lg, lb), exchange=exchange)


def _bwd_mix_b(dca, du, z, dab, wa, wb, w_in, x, g1, dx2, ts, exchange=None):
    s = x.shape[0]
    nt = s // ts
    rc = min(ROW_CHUNK, ts)
    n_sub = ts // rc

    def body(dca_ref, dcap_ref, dcan_ref, du_ref, dup_ref, dun_ref, z_ref, zp_ref, zn_ref, dab_ref,
             wa_ref, wb_ref, wi_ref, x_ref, g1_ref, dx2_ref,
             dz_ref, dx_ref, dg1_ref, dwb_ref, dcae_ref, due_ref, ue_ref, acc_ref, dg8_ref,
             dz0_ref, dz1_ref, dh0_ref, dh1_ref):
        i = pl.program_id(0)

        @pl.when(i == 0)
        def _():
            acc_ref[...] = jnp.zeros_like(acc_ref)
            dg8_ref[...] = jnp.zeros_like(dg8_ref)
            dz1_ref[...] = jnp.zeros_like(dz1_ref)
            dh1_ref[...] = jnp.zeros_like(dh1_ref)

        vt = jnp.minimum(i, nt - 1)
        first = vt == 0
        last = vt == nt - 1
        live = (i < nt).astype(F32)
        _fill_halo(dcae_ref, dcap_ref[...], dcan_ref[...], ts, first, last)
        _fill_halo(due_ref, dup_ref[...], dun_ref[...], ts, first, last)
        _, up = _p_u0(zp_ref, slice(None))
        _, un = _p_u0(zn_ref, slice(None))
        _fill_halo(ue_ref, up, un, ts, first, last)

        def fill(j, carry):
            r0 = pl.multiple_of(j * rc, rc)
            rows = pl.ds(r0, rc)
            _put_rows(dcae_ref, r0, rc, dca_ref[rows, :])
            _put_rows(due_ref, r0, rc, du_ref[rows, :])
            b_v = z_ref[rows, 3 * D_A:3 * D_A + D_B].astype(F32)
            b_g = z_ref[rows, 3 * D_A + D_B:D_IN].astype(F32)
            _put_rows(ue_ref, r0, rc, b_v * _sigmoid(b_g))
            return carry

        lax.fori_loop(0, n_sub, fill, 0)

        def stage(dz_new, dz_old, dh_new, dh_old):
            def put(rows, c0, val):
                dz_ref[rows, c0:c0 + LANES] = val
                dz_new[rows, c0:c0 + LANES] = val

            def mixer_a(r0, l0):
                rows = slice(r0, r0 + rc)
                dp = _conv_block(dcae_ref, wa_ref, r0, rc, l0, K_A, True)
                a_h = z_ref[rows, l0:l0 + LANES].astype(F32)
                a_c = z_ref[rows, 2 * D_A + l0:2 * D_A + l0 + LANES].astype(F32)
                put(rows, l0, (dp * a_c).astype(BF16))
                put(rows, D_A + l0, dab_ref[rows, l0:l0 + LANES])
                put(rows, 2 * D_A + l0, (dp * a_h).astype(BF16))

            def mixer_b(r0, l0):
                rows = slice(r0, r0 + rc)
                du0 = _conv_block(due_ref, wb_ref, r0, rc, l0, K_B, True)
                b_v = z_ref[rows, 3 * D_A + l0:3 * D_A + l0 + LANES].astype(F32)
                b_g = z_ref[rows, 3 * D_A + D_B + l0:3 * D_A + D_B + l0 + LANES].astype(F32)
                sg = _sigmoid(b_g)
                put(rows, 3 * D_A + l0, (du0 * sg).astype(BF16))
                put(rows, 3 * D_A + D_B + l0, (du0 * b_v * (sg * (1.0 - sg))).astype(BF16))
                _conv_wgrad_block(acc_ref, du_ref[rows, l0:l0 + LANES], ue_ref, r0, rc, l0, K_B, live)

            def tail(r0):
                rows = slice(r0, r0 + rc)
                dx, dgrow = _rms_bwd_rows(dh_old[rows, :], x_ref[rows, :], g1_ref[...])
                dx_ref[rows, :] = dx2_ref[rows, :] + dx
                dg8_ref[...] = dg8_ref[...] + _fold8(dgrow)

            units = []
            for q in range(n_sub):
                units += [(mixer_a, (q * rc, l0)) for l0 in range(0, D_A, LANES)]
                units += [(mixer_b, (q * rc, l0)) for l0 in range(0, D_B, LANES)]
                units.append((tail, (q * rc,)))
            _interleaved(units, _matmul_pieces([(dz_old, wi_ref)], dh_new, 2))

        @pl.when(i % 2 == 0)
        def _():
            stage(dz0_ref, dz1_ref, dh0_ref, dh1_ref)

        @pl.when(i % 2 == 1)
        def _():
            stage(dz1_ref, dz0_ref, dh1_ref, dh0_ref)

        @pl.when(i == nt + 1)
        def _():
            _reduce_acc(dwb_ref, acc_ref, K_B)
            dg1_ref[...] = jnp.sum(dg8_ref[...], axis=0, keepdims=True)

    vtile = lambda i: jnp.minimum(i, nt - 1)
    ttile = lambda i: jnp.clip(i - 2, 0, nt - 1)
    return _call(
        body, name="bwd_mix_b", grid=(nt + 2,),
        in_specs=[_rows_at(ts, D_A, vtile), _prev_at(ts, D_A, vtile), _next_at(ts, D_A, s, vtile),
                  _rows_at(ts, D_B, vtile), _prev_at(ts, D_B, vtile), _next_at(ts, D_B, s, vtile),
                  _rows_at(ts, D_IN, vtile), _prev_at(ts, D_IN, vtile), _next_at(ts, D_IN, s, vtile), _rows_at(ts, D_A, vtile),
                  _const((K_A, D_A)), _const((K_B, D_B)), _const((D_IN, D_MODEL)), _rows_at(ts, D_MODEL, ttile),
                  _const((1, D_MODEL)), _rows_at(ts, D_MODEL, ttile)],
        out_specs=[_rows_at(ts, D_IN, vtile), _rows_at(ts, D_MODEL, ttile), _acc_out((1, D_MODEL)), _acc_out((K_B, D_B))],
        out_shape=[_sds((s, D_IN), BF16), _sds((s, D_MODEL), F32), _sds((1, D_MODEL), F32), _sds((K_B, D_B), F32)],
        scratch_shapes=[_ext_scratch(ts, D_A), _ext_scratch(ts, D_B), _ext_scratch(ts, D_B),
                        pltpu.VMEM((K_B, SUBLANES, D_B), F32), pltpu.VMEM((SUBLANES, D_MODEL), F32),
                        pltpu.VMEM((ts, D_IN), BF16), pltpu.VMEM((ts, D_IN), BF16),
                        pltpu.VMEM((ts, D_MODEL), F32), pltpu.VMEM((ts, D_MODEL), F32)],
        args=(dca, dca, dca, du, du, du, z, z, z, dab, wa, wb, w_in, x, g1, dx2), exchange=exchange)


def _matmul_tn(a, b, name):
    s, m = a.shape
    n = b.shape[1]
    tk = min(1024, s)
    nk = s // tk
    tm = 256

    def body(a_ref, b_ref, o_ref, acc_ref):
        k = pl.program_id(0)

        @pl.when(k == 0)
        def _():
            acc_ref[...] = jnp.zeros_like(acc_ref)

        for m0 in range(0, m, tm):
            acc_ref[m0:m0 + tm, :] = acc_ref[m0:m0 + tm, :] + lax.dot_general(
                a_ref[:, m0:m0 + tm], b_ref[...], _TN, preferred_element_type=F32)

        @pl.when(k == nk - 1)
        def _():
            o_ref[...] = acc_ref[...].astype(BF16)

    return pl.pallas_call(
        body, name=name, grid=(nk,),
        in_specs=[_rows(tk, m), _rows(tk, n)],
        out_specs=_acc_out((m, n)),
        out_shape=_sds((m, n), BF16),
        scratch_shapes=[pltpu.VMEM((m, n), F32)],
        compiler_params=_params(),
    )(a, b)


CHIP_RELS = ((1, 0, 0), (0, 1, 0), (1, 1, 0))
CORE_RELS = ((0, 0, 1),)
ALL_RELS = ((0, 0, 1), (0, 1, 0), (0, 1, 1), (1, 0, 0), (1, 0, 1), (1, 1, 0), (1, 1, 1))


def _chip_slot(dev):
    return 2 * dev[0] + dev[1]


def _dev_slot(dev):
    return 4 * dev[0] + 2 * dev[1] + dev[2]


def _me():
    return (lax.axis_index("x"), lax.axis_index("y"), lax.axis_index("c"))


def _peer(me, rel):
    return tuple((1 - me[a]) if rel[a] else me[a] for a in range(3))


_ANY = pl.BlockSpec(memory_space=pl.ANY)


class _Exchange:
    def __init__(self, inputs, out_shape, scratch, start, finish, forward=None):
        self.inputs, self.out_shape, self.scratch = list(inputs), list(out_shape), list(scratch)
        self.start, self.finish, self.forward = start, finish, forward


def _all_gather(payloads):
    n_p = len(payloads)
    n_k = 1 + 2 * len(CHIP_RELS)

    def copy(srcs, dsts, sems, p, k, block_dev, to, from_src):
        blk = dsts[p].at[_dev_slot(block_dev)]
        return pltpu.make_async_remote_copy(
            src_ref=srcs[p] if from_src else blk, dst_ref=blk,
            send_sem=sems[0].at[n_k * p + k], recv_sem=sems[1].at[n_k * p + k], device_id=to, device_id_type=MESH)

    def own_copy(srcs, dsts, sems, p):
        return pltpu.make_async_copy(srcs[p], dsts[p].at[_dev_slot(_me())], sems[2].at[p])

    def start(srcs, dsts, sems):
        me = _me()
        for p in range(n_p):
            own_copy(srcs, dsts, sems, p).start()
        for j, rel in enumerate(CHIP_RELS):
            for p in range(n_p):
                copy(srcs, dsts, sems, p, 1 + j, me, _peer(me, rel), True).start()
        for p in range(n_p):
            copy(srcs, dsts, sems, p, 0, me, _peer(me, CORE_RELS[0]), True).start()

    def forward(srcs, dsts, sems):
        me = _me()
        sibling = _peer(me, CORE_RELS[0])
        for j, rel in enumerate(CHIP_RELS):
            other = _peer(me, rel)
            for p in range(n_p):
                copy(srcs, dsts, sems, p, 1 + j, other, me, False).wait_recv()
                copy(srcs, dsts, sems, p, 4 + j, other, sibling, False).start()

    def finish(srcs, dsts, sems):
        me = _me()
        sibling = _peer(me, CORE_RELS[0])
        for p in range(n_p):
            copy(srcs, dsts, sems, p, 0, sibling, me, False).wait_recv()
        for j, rel in enumerate(CHIP_RELS):
            for p in range(n_p):
                copy(srcs, dsts, sems, p, 4 + j, _peer(sibling, rel), me, False).wait_recv()
        for p in range(n_p):
            own_copy(srcs, dsts, sems, p).wait()
            copy(srcs, dsts, sems, p, 0, me, sibling, True).wait_send()
            for j, rel in enumerate(CHIP_RELS):
                copy(srcs, dsts, sems, p, 1 + j, me, _peer(me, rel), True).wait_send()
                copy(srcs, dsts, sems, p, 4 + j, _peer(me, rel), sibling, False).wait_send()

    return _Exchange(
        payloads, [_sds((N_DEV,) + p.shape, p.dtype) for p in payloads],
        [pltpu.SemaphoreType.DMA((n_p * n_k,)), pltpu.SemaphoreType.DMA((n_p * n_k,)), pltpu.SemaphoreType.DMA((n_p,))],
        start, finish, forward)


def _gather_direct(payload):
    n_r = len(ALL_RELS)

    def copies(srcs, dsts, sems):
        me = _me()
        mine = dsts[0].at[_dev_slot(me)]
        own = pltpu.make_async_copy(srcs[0], mine, sems[2].at[0])
        remote = [pltpu.make_async_remote_copy(src_ref=srcs[0], dst_ref=mine, send_sem=sems[0].at[k], recv_sem=sems[1].at[k],
                                               device_id=_peer(me, rel), device_id_type=MESH)
                  for k, rel in enumerate(ALL_RELS)]
        return [own] + remote

    def start(srcs, dsts, sems):
        for cp in copies(srcs, dsts, sems):
            cp.start()

    def finish(srcs, dsts, sems):
        for cp in copies(srcs, dsts, sems):
            cp.wait()

    return _Exchange([payload], [_sds((N_DEV,) + payload.shape, payload.dtype)],
                     [pltpu.SemaphoreType.DMA((n_r,)), pltpu.SemaphoreType.DMA((n_r,)), pltpu.SemaphoreType.DMA((1,))],
                     start, finish)


def _scatter_exchange(payloads, rels, src_view, view_shapes):
    n_p = len(payloads)
    n_r = len(rels)

    def copies(srcs, dsts, sems):
        me = _me()
        out = []
        for k, rel in enumerate(rels):
            peer = _peer(me, rel)
            for p in range(n_p):
                out.append(pltpu.make_async_remote_copy(
                    src_ref=src_view(srcs[p], peer), dst_ref=dsts[p].at[k],
                    send_sem=sems[0].at[p * n_r + k], recv_sem=sems[1].at[p * n_r + k],
                    device_id=peer, device_id_type=MESH))
        return out

    def start(srcs, dsts, sems):
        for cp in copies(srcs, dsts, sems):
            cp.start()

    def finish(srcs, dsts, sems):
        for cp in copies(srcs, dsts, sems):
            cp.wait()

    return _Exchange(payloads, [_sds((n_r,) + vs, p.dtype) for vs, p in zip(view_shapes, payloads)],
                     [pltpu.SemaphoreType.DMA((n_p * n_r,)), pltpu.SemaphoreType.DMA((n_p * n_r,))], start, finish)


def _split_refs(refs, sizes):
    out, at = [], 0
    for n in sizes:
        out.append(refs[at:at + n])
        at += n
    return out


def _run_exchanges(name, exchanges):
    n_in = [len(e.inputs) for e in exchanges]
    n_out = [len(e.out_shape) for e in exchanges]
    n_sc = [len(e.scratch) for e in exchanges]

    def body(*refs):
        ins, outs, scs = _split_refs(refs, [sum(n_in), sum(n_out), sum(n_sc)])
        parts = list(zip(exchanges, _split_refs(ins, n_in), _split_refs(outs, n_out), _split_refs(scs, n_sc)))
        for e, i, o, s in parts:
            e.start(i, o, s)
        for e, i, o, s in parts:
            if e.forward is not None:
                e.forward(i, o, s)
        for e, i, o, s in parts:
            e.finish(i, o, s)

    outs = pl.pallas_call(
        body, name=name, in_specs=[_ANY] * sum(n_in), out_specs=[_ANY] * sum(n_out),
        out_shape=[sd for e in exchanges for sd in e.out_shape],
        scratch_shapes=[sc for e in exchanges for sc in e.scratch],
    )(*[a for e in exchanges for a in e.inputs])
    return _split_refs(list(outs), n_out)


def _call(body, *, name, grid, in_specs, out_specs, out_shape, scratch_shapes, args, exchange=None, forward_step=None):
    n_in, n_out, n_sc = len(in_specs), len(out_specs), len(scratch_shapes)
    if exchange is None:
        outs = pl.pallas_call(body, name=name, grid=grid, in_specs=in_specs, out_specs=out_specs, out_shape=out_shape,
                              scratch_shapes=scratch_shapes, compiler_params=_params())(*args)
        return list(outs), []
    e = exchange
    sizes = [n_in, len(e.inputs), n_out, len(e.out_shape), n_sc, len(e.scratch)]
    last = grid[0] - 1

    def wrapped(*refs):
        a, ei, o, eo, sc, es = _split_refs(refs, sizes)
        i = pl.program_id(0)

        @pl.when(i == 0)
        def _():
            e.start(ei, eo, es)

        if e.forward is not None:
            @pl.when(i == forward_step)
            def _():
                e.forward(ei, eo, es)

        body(*a, *o, *sc)

        @pl.when(i == last)
        def _():
            e.finish(ei, eo, es)

    outs = pl.pallas_call(
        wrapped, name=name, grid=grid,
        in_specs=list(in_specs) + [_ANY] * len(e.inputs), out_specs=list(out_specs) + [_ANY] * len(e.out_shape),
        out_shape=list(out_shape) + e.out_shape, scratch_shapes=list(scratch_shapes) + e.scratch,
        compiler_params=_params(),
    )(*args, *e.inputs)
    outs = list(outs)
    return outs[:n_out], outs[n_out:]


def _pair_sum(grads, recvd, my_core, name):
    n_p = len(grads)

    def body(c_ref, *refs):
        del c_ref
        for p in range(n_p):
            refs[2 * n_p + p][...] = (refs[p][...].astype(F32) + refs[n_p + p][...].astype(F32)).astype(BF16)

    def blk(g):
        return (None, None) + g.shape[2:]

    return pl.pallas_call(
        body, name=name,
        grid_spec=pltpu.PrefetchScalarGridSpec(
            num_scalar_prefetch=1, grid=(N_CHIP,),
            in_specs=[pl.BlockSpec(blk(g), lambda j, c: (j, c[0], 0, 0)) for g in grads]
            + [pl.BlockSpec(blk(g), lambda j, c: (0, j, 0, 0)) for g in grads],
            out_specs=[pl.BlockSpec((None,) + g.shape[2:], lambda j, c: (j, 0, 0)) for g in grads]),
        out_shape=[_sds((N_CHIP,) + g.shape[2:], BF16) for g in grads],
        compiler_params=pltpu.CompilerParams(dimension_semantics=("arbitrary",), vmem_limit_bytes=VMEM_LIMIT),
    )(my_core, *grads, *recvd)


def _chip_sum(psums, recvd, my_chip, name):
    n_p = len(psums)

    def body(c_ref, *refs):
        del c_ref
        for p in range(n_p):
            acc = refs[p][...].astype(F32)
            for k in range(len(CHIP_RELS)):
                acc = acc + refs[n_p + p][k].astype(F32)
            refs[2 * n_p + p][...] = acc

    return pl.pallas_call(
        body, name=name,
        grid_spec=pltpu.PrefetchScalarGridSpec(
            num_scalar_prefetch=1, grid=(1,),
            in_specs=[pl.BlockSpec((None,) + g.shape[1:], lambda i, c: (c[0], 0, 0)) for g in psums]
            + [pl.BlockSpec(r.shape, lambda i, c: (0, 0, 0)) for r in recvd],
            out_specs=[pl.BlockSpec(g.shape[1:], lambda i, c: (0, 0)) for g in psums]),
        out_shape=[_sds(g.shape[1:], F32) for g in psums],
        compiler_params=pltpu.CompilerParams(dimension_semantics=("arbitrary",), vmem_limit_bytes=VMEM_LIMIT),
    )(my_chip, *psums, *recvd)


def _sum_devices(parts):
    def body(p_ref, o_ref):
        acc = p_ref[0]
        for j in range(1, N_DEV):
            acc = acc + p_ref[j]
        o_ref[...] = acc

    return pl.pallas_call(body, name="small_grad_sum", out_shape=_sds(parts.shape[1:], F32))(parts)


def _adamw(ws, gs, ms, vs, name):
    n_t = len(ws)

    def body(*refs):
        w_refs, g_refs, m_refs, v_refs = (refs[j * n_t:(j + 1) * n_t] for j in range(4))
        outs = refs[4 * n_t:]
        for k in range(n_t):
            gg = g_refs[k][...]
            mn = ADAM_B1 * m_refs[k][...] + (1.0 - ADAM_B1) * gg
            vn = ADAM_B2 * v_refs[k][...] + (1.0 - ADAM_B2) * (gg * gg)
            m_hat = mn / (1.0 - ADAM_B1 ** ADAM_STEP)
            v_hat = vn / (1.0 - ADAM_B2 ** ADAM_STEP)
            outs[3 * k][...] = -ADAM_LR * (m_hat / (jnp.sqrt(v_hat) + ADAM_EPS) + ADAM_WD * w_refs[k][...])
            outs[3 * k + 1][...] = mn
            outs[3 * k + 2][...] = vn

    out_shape = [_sds(w.shape, F32) for w in ws for _ in range(3)]
    return pl.pallas_call(body, name=name, out_shape=out_shape,
                          compiler_params=pltpu.CompilerParams(vmem_limit_bytes=VMEM_LIMIT))(*ws, *gs, *ms, *vs)


class _Mesh:
    def __init__(self, shards, my_chip, my_core):
        self.shards, self.my_chip, self.my_core = shards, my_chip.reshape(1), my_core.reshape(1)

    def gather(self, names):
        return _all_gather([self.shards[n] for n in names])

    @staticmethod
    def whole(gathered):
        return gathered.reshape(N_DEV * gathered.shape[1], gathered.shape[2])

    @staticmethod
    def by_device(grads):
        return [g.reshape(N_CHIP, 2, g.shape[0] // N_DEV, g.shape[1]) for g in grads]

    @staticmethod
    def to_sibling(parts):
        return _scatter_exchange(parts, CORE_RELS, lambda ref, peer: ref.at[:, peer[2]],
                                 [(N_CHIP,) + p.shape[2:] for p in parts])

    @staticmethod
    def to_chips(pair):
        return _scatter_exchange(pair, CHIP_RELS, lambda ref, peer: ref.at[_chip_slot(peer)], [p.shape[1:] for p in pair])


def _step(x, target, g1, w_in_t, wa, wb, bb, lg, lb, w_out, g2, w_gate_t, w_up_t, wf, w_down, g3, ts, mesh=None):
    (z, h1), got = _fwd_in(x, g1, w_in_t, ts, exchange=mesh and mesh.gather(["w_out", "w_gate"]))
    if mesh:
        w_out, w_gate_t = [mesh.whole(g) for g in got]
    (x2, y, u), got = _fwd_mix(z, x, wa, wb, bb, lg, lb, w_out, ts, exchange=mesh and mesh.gather(["w_up", "w_down"]))
    if mesh:
        w_up_t, w_down = [mesh.whole(g) for g in got]
    g0, v, h2 = _fwd_ffn_in(x2, g2, w_gate_t, w_up_t, ts)
    tk = min(ts, SKEW_TILE)
    a, dx3, dx3b, loss, dg3 = _fwd_ffn_out(g0, v, x2, wf, w_down, g3, target, ts)
    dgc, dv, dwf = _bwd_ffn_a(dx3b, g0, v, w_down, wf, ts)
    dg0, dx2, dx2b, dg2 = _bwd_ffn_b(dgc, dv, wf, w_gate_t, w_up_t, x2, g2, dx3, tk)
    first = dict(w_down=_matmul_tn(a, dx3b, "wgrad_down"), w_gate=_matmul_tn(dg0, h2, "wgrad_gate"),
                 w_up=_matmul_tn(dv, h2, "wgrad_up"), w_out=_matmul_tn(y, dx2b, "wgrad_out"))
    if mesh:
        parts = mesh.by_device(list(first.values()))
    (dca, du, dab, dwa, dlg, dlb, dbb), got = _bwd_mix_a(dx2b, jnp.swapaxes(w_out, 0, 1), z, u, wa, lg, lb, tk,
                                                         exchange=mesh and mesh.to_sibling(parts))
    if mesh:
        pair = _pair_sum(parts, got, mesh.my_core, "rs_pair_sum_first")
    (dz, dx, dg1, dwb), got = _bwd_mix_b(dca, du, z, dab, wa, wb, w_in_t, x, g1, dx2, tk,
                                         exchange=mesh and mesh.to_chips(pair))
    dw_in_t = _matmul_tn(dz, h1, "wgrad_in")
    small = dict(norm_mix_g=dg1, conv_a_w=dwa, conv_b_w=dwb, conv_b_b=dbb, ln_b_g=dlg, ln_b_b=dlb,
                 norm_ffn_g=dg2, conv_ffn_w=dwf, norm_final_g=dg3)
    if not mesh:
        return loss, dx, dict(w_in=dw_in_t, **first), small
    big = dict(zip(first, _chip_sum(pair, got, mesh.my_chip, "rs_chip_sum_first")))
    parts = mesh.by_device([dw_in_t])
    (got,) = _run_exchanges("rs_cores_last", [mesh.to_sibling(parts)])
    pair = _pair_sum(parts, got, mesh.my_core, "rs_pair_sum_last")
    got, (every,) = _run_exchanges("rs_chips_last", [mesh.to_chips(pair), _gather_direct(_pack_small_grads(small, loss))])
    (big["w_in"],) = _chip_sum(pair, got, mesh.my_chip, "rs_chip_sum_last")
    shapes = [loss.shape if n == "loss" else small[n].shape for n in _SMALL_NAMES]
    return None, dx, big, _unpack_small_grads(_sum_devices(every), shapes)


def _pack_small_weights(conv_a_s, conv_b_s, conv_ffn_s):
    buf = jnp.zeros((SMALL_W_ROWS, SMALL_W_COLS), F32)
    buf = buf.at[0:K_A, 0:CONV_A_COLS].set(conv_a_s)
    buf = buf.at[K_A:K_A + K_B, 0:CONV_A_COLS].set(conv_b_s)
    return buf.at[K_A + K_B:K_A + K_B + K_F, 0:W_FF_COLS].set(conv_ffn_s)


def _unpack_small_weights(full):
    def take(r0, k, w):
        return jnp.transpose(full[:, r0:r0 + k, 0:w], (1, 0, 2)).reshape(k, N_DEV * w)

    return take(0, K_A, CONV_A_COLS), take(K_A, K_B, CONV_A_COLS), take(K_A + K_B, K_F, W_FF_COLS)


_SMALL_NAMES = ("conv_b_w", "conv_a_w", "conv_ffn_w", "norm_mix_g", "norm_ffn_g", "norm_final_g",
                "conv_b_b", "ln_b_g", "ln_b_b", "loss")
SMALL_G_ROWS = 64


def _small_rows(shapes):
    plan, at = [], 0
    for idx, (rows, cols) in enumerate(shapes):
        for r in range(rows):
            for c0 in range(0, cols, SMALL_G_COLS):
                plan.append((idx, r, c0, min(SMALL_G_COLS, cols - c0), at))
                at += 1
    assert at <= SMALL_G_ROWS
    return plan


def _pack_small_grads(small, loss):
    srcs = [loss if n == "loss" else small[n] for n in _SMALL_NAMES]
    plan = _small_rows([a.shape for a in srcs])

    def body(*refs):
        out = refs[-1]
        out[...] = jnp.zeros_like(out)
        for idx, r, c0, width, at in plan:
            out[at:at + 1, 0:width] = refs[idx][r:r + 1, c0:c0 + width]

    return pl.pallas_call(body, name="small_grad_pack", out_shape=_sds((SMALL_G_ROWS, SMALL_G_COLS), F32))(*srcs)


def _unpack_small_grads(tot, shapes):
    out, at = {}, 0
    for name, (rows, cols) in zip(_SMALL_NAMES, shapes):
        per_row = -(-cols // SMALL_G_COLS)
        blk = tot[at:at + rows * per_row]
        at += rows * per_row
        out[name] = blk.reshape(rows, per_row * SMALL_G_COLS)[:, 0:cols]
    out["loss"] = out["loss"][0, 0]
    return out


def kernel(x, norm_mix_g, w_in, conv_a_w, conv_b_w, conv_b_b, ln_b_g, ln_b_b, w_out, norm_ffn_g, w_gate, w_up, conv_ffn_w, w_down, norm_final_g, loss_target, m_norm_mix_g, m_w_in, m_conv_a_w, m_conv_b_w, m_conv_b_b, m_ln_b_g, m_ln_b_b, m_w_out, m_norm_ffn_g, m_w_gate, m_w_up, m_conv_ffn_w, m_w_down, m_norm_final_g, v_norm_mix_g, v_w_in, v_conv_a_w, v_conv_b_w, v_conv_b_b, v_ln_b_g, v_ln_b_b, v_w_out, v_norm_ffn_g, v_w_gate, v_w_up, v_conv_ffn_w, v_w_down, v_norm_final_g):
    ix, iy, ic = lax.axis_index("x"), lax.axis_index("y"), lax.axis_index("c")
    my_chip = (2 * ix + iy).astype(jnp.int32)
    my_core = ic.astype(jnp.int32)
    my_dev = 2 * my_chip + my_core

    weights = dict(norm_mix_g=norm_mix_g, w_in=w_in, conv_a_w=conv_a_w, conv_b_w=conv_b_w, conv_b_b=conv_b_b,
                   ln_b_g=ln_b_g, ln_b_b=ln_b_b, w_out=w_out, norm_ffn_g=norm_ffn_g, w_gate=w_gate, w_up=w_up,
                   conv_ffn_w=conv_ffn_w, w_down=w_down, norm_final_g=norm_final_g)
    m_in = dict(norm_mix_g=m_norm_mix_g, w_in=m_w_in, conv_a_w=m_conv_a_w, conv_b_w=m_conv_b_w, conv_b_b=m_conv_b_b,
                ln_b_g=m_ln_b_g, ln_b_b=m_ln_b_b, w_out=m_w_out, norm_ffn_g=m_norm_ffn_g, w_gate=m_w_gate,
                w_up=m_w_up, conv_ffn_w=m_conv_ffn_w, w_down=m_w_down, norm_final_g=m_norm_final_g)
    v_in = dict(norm_mix_g=v_norm_mix_g, w_in=v_w_in, conv_a_w=v_conv_a_w, conv_b_w=v_conv_b_w, conv_b_b=v_conv_b_b,
                ln_b_g=v_ln_b_g, ln_b_b=v_ln_b_b, w_out=v_w_out, norm_ffn_g=v_norm_ffn_g, w_gate=v_w_gate,
                w_up=v_w_up, conv_ffn_w=v_conv_ffn_w, w_down=v_w_down, norm_final_g=v_norm_final_g)
    order = list(weights)
    big_names = ("w_in", "w_gate", "w_up", "w_out", "w_down")
    transposed = ("w_in", "w_gate", "w_up")

    def shard2d(name, a):
        if name in transposed:
            return jnp.swapaxes(a[0], 0, 1)
        return a.reshape(1, a.shape[0]) if a.ndim == 1 else a.reshape(a.shape[-2:])

    def unshard2d(name, a2, like):
        if name in transposed:
            return jnp.swapaxes(a2, 0, 1)[None]
        return a2.reshape(like.shape)

    mesh = _Mesh({n: shard2d(n, weights[n]).astype(BF16) for n in big_names}, my_chip, my_core)
    gathered, = _run_exchanges("ag_first", [_all_gather(
        [mesh.shards["w_in"], _pack_small_weights(conv_a_w[0], conv_b_w[0], conv_ffn_w[0])])])
    w_in_t = mesh.whole(gathered[0])
    wa_f, wb_f, wf_f = _unpack_small_weights(gathered[1])

    _, dx, gsum, stot = _step(
        x[0], loss_target[0], norm_mix_g, w_in_t, wa_f, wb_f, conv_b_b, ln_b_g, ln_b_b, None, norm_ffn_g,
        None, None, wf_f, None, norm_final_g.reshape(1, D_MODEL), SEQ_TILE, mesh)

    grads2d = dict(
        norm_mix_g=stot["norm_mix_g"],
        conv_a_w=lax.dynamic_slice(stot["conv_a_w"], (0, my_dev * CONV_A_COLS), (K_A, CONV_A_COLS)),
        conv_b_w=lax.dynamic_slice(stot["conv_b_w"], (0, my_dev * CONV_A_COLS), (K_B, CONV_A_COLS)),
        conv_b_b=stot["conv_b_b"], ln_b_g=stot["ln_b_g"], ln_b_b=stot["ln_b_b"],
        norm_ffn_g=stot["norm_ffn_g"],
        conv_ffn_w=lax.dynamic_slice(stot["conv_ffn_w"], (0, my_dev * W_FF_COLS), (K_F, W_FF_COLS)),
        norm_final_g=stot["norm_final_g"],
        **gsum,
    )

    updates = {}
    small_names = [n for n in order if n not in big_names]
    for group, label in [([n], "adamw_" + n) for n in big_names] + [(small_names, "adamw_small")]:
        outs = _adamw([shard2d(n, weights[n]) for n in group], [grads2d[n] for n in group],
                      [shard2d(n, m_in[n]) for n in group], [shard2d(n, v_in[n]) for n in group], label)
        for k, n in enumerate(group):
            updates[n] = outs[3 * k:3 * k + 3]
    g_out = [unshard2d(n, grads2d[n], weights[n]) for n in order]
    d_out, m_out, v_out = [[unshard2d(n, updates[n][j], weights[n]) for n in order] for j in range(3)]

    return (stot["loss"], dx[None], *g_out, *d_out, *m_out, *v_out)
```

```python
import jax
import jax.numpy as jnp
from jax import lax
from jax.experimental import pallas as pl
from jax.experimental.pallas import tpu as pltpu

F32 = jnp.float32
BF16 = jnp.bfloat16

D_MODEL = 1024
D_A = 512
D_B = 512
D_IN = 3 * D_A + 2 * D_B
D_FF = 2816
K_A = 3
K_B = 31
K_F = 3
RMS_EPS = 1e-6
LN_EPS = 1e-5

ADAM_LR = 0.001
ADAM_B1 = 0.9
ADAM_B2 = 0.999
ADAM_EPS = 1e-08
ADAM_WD = 0.01
ADAM_STEP = 10

N_DEV = 8
N_CHIP = 4
LANES = 128
SUBLANES = 8
HALO = 16
ROW_CHUNK = 64
SEQ_TILE = 512
SKEW_TILE = 256
VMEM_LIMIT = 56 * 1024 * 1024

MESH = pl.DeviceIdType.MESH

W_FF_COLS = D_FF // N_DEV
CONV_A_COLS = D_A // N_DEV
SMALL_W_ROWS = 40
SMALL_W_COLS = 384
SMALL_G_COLS = 512
FF_PAD = 3072


def _rows(ts, c):
    return pl.BlockSpec((ts, c), lambda i: (i, 0))


def _const(shape):
    return pl.BlockSpec(shape, lambda i: (0,) * len(shape), pipeline_mode=pl.Buffered(1))


def _acc_out(shape):
    return pl.BlockSpec(shape, lambda i: (0,) * len(shape))


def _rows_at(ts, c, tile):
    return pl.BlockSpec((ts, c), lambda i: (tile(i), 0))


def _prev_at(ts, c, tile):
    return pl.BlockSpec((HALO, c), lambda i: (jnp.maximum(tile(i) * (ts // HALO) - 1, 0), 0))


def _next_at(ts, c, s, tile):
    last = s // HALO - 1
    return pl.BlockSpec((HALO, c), lambda i: (jnp.minimum((tile(i) + 1) * (ts // HALO), last), 0))


def _prev(ts, c):
    return _prev_at(ts, c, lambda i: i)


def _next(ts, c, s):
    return _next_at(ts, c, s, lambda i: i)


MXU_COLS = 256
MXU_ROWS = 256


def _col_pieces(n):
    return [(c0, min(MXU_COLS, n - c0)) for c0 in range(0, n, MXU_COLS)]


def _matmul_pieces(terms, out_ref, k_parts):
    m, n = out_ref.shape
    steps = []
    for lhs_ref, w_ref in terms:
        tiles = lhs_ref.shape[1] // MXU_COLS
        cuts = [MXU_COLS * (tiles * j // k_parts) for j in range(k_parts)] + [lhs_ref.shape[1]]
        steps += [(lhs_ref, w_ref, cuts[j], cuts[j + 1]) for j in range(k_parts)]

    def piece(m0, n0, width, step):
        lhs_ref, w_ref, k0, k1 = steps[step]
        part = jnp.dot(lhs_ref[m0:m0 + MXU_ROWS, k0:k1], w_ref[k0:k1, n0:n0 + width], preferred_element_type=F32)
        if step:
            part = part + out_ref[m0:m0 + MXU_ROWS, n0:n0 + width]
        out_ref[m0:m0 + MXU_ROWS, n0:n0 + width] = part

    return [(piece, (m0, n0, w, j)) for j in range(len(steps)) for n0, w in _col_pieces(n) for m0 in range(0, m, MXU_ROWS)]


def _interleaved(vector_units, matmul_pieces):
    n_u, n_p = len(vector_units), len(matmul_pieces)
    done = 0
    for k, (unit, args) in enumerate(vector_units):
        while done < n_p and done * n_u <= k * n_p:
            matmul_pieces[done][0](*matmul_pieces[done][1])
            done += 1
        unit(*args)
    for fn, args in matmul_pieces[done:]:
        fn(*args)


def _params():
    return pltpu.CompilerParams(dimension_semantics=("arbitrary",), vmem_limit_bytes=VMEM_LIMIT)


def _sds(shape, dtype):
    return jax.ShapeDtypeStruct(shape, dtype)


def _sigmoid(v):
    return 0.5 * jnp.tanh(0.5 * v) + 0.5


def _conv_block(ext_ref, w_ref, r0, rc, l0, k_taps, transposed):
    acc = None
    for k in range(k_taps):
        d = (k_taps // 2 - k) if transposed else (k - k_taps // 2)
        term = ext_ref[l0 // LANES, pl.ds(r0 + HALO + d, rc), :] * w_ref[k:k + 1, l0:l0 + LANES]
        acc = term if acc is None else acc + term
    return acc


def _conv_wgrad_block(acc_ref, dout, ext_ref, r0, rc, l0, k_taps, scale=None):
    for k in range(k_taps):
        prod = dout * ext_ref[l0 // LANES, pl.ds(r0 + HALO + k - k_taps // 2, rc), :]
        part = prod.reshape(rc // SUBLANES, SUBLANES, LANES).sum(axis=0)
        if scale is not None:
            part = part * scale
        acc_ref[k, :, l0:l0 + LANES] = acc_ref[k, :, l0:l0 + LANES] + part


def _reduce_acc(out_ref, acc_ref, k_taps):
    for k in range(k_taps):
        out_ref[k:k + 1, :] = jnp.sum(acc_ref[k], axis=0, keepdims=True)


def _fold8(v):
    rc, c = v.shape
    return v.reshape(rc // SUBLANES, SUBLANES, c).sum(axis=0)


def _ext_scratch(ts, c):
    return pltpu.VMEM((c // LANES, ts + 2 * HALO, LANES), F32)


def _put_rows(ext_ref, r0, rc, val):
    for q in range(val.shape[1] // LANES):
        ext_ref[q, pl.ds(r0 + HALO, rc), :] = val[:, q * LANES:(q + 1) * LANES]


def _fill_halo(ext_ref, vals_prev, vals_next, ts, first, last):
    for q in range(vals_prev.shape[1] // LANES):
        cols = slice(q * LANES, (q + 1) * LANES)
        ext_ref[q, 0:HALO, :] = jnp.where(first, 0.0, vals_prev[:, cols])
        ext_ref[q, HALO + ts:HALO + ts + HALO, :] = jnp.where(last, 0.0, vals_next[:, cols])


def _rms_bwd_rows(dh, xf, g):
    r = lax.rsqrt(jnp.mean(xf * xf, axis=-1, keepdims=True) + RMS_EPS)
    xhat = xf * r
    dxh = dh * g
    dx = r * (dxh - xhat * jnp.mean(dxh * xhat, axis=-1, keepdims=True))
    return dx, dh * xhat


_NT = (((1,), (1,)), ((), ()))
_TN = (((0,), (0,)), ((), ()))


def _fwd_in(x, g1, w_in_t, ts, exchange=None):
    s = x.shape[0]

    def body(x_ref, g_ref, w_ref, z_ref, h_ref):
        xf = x_ref[...]
        r = lax.rsqrt(jnp.mean(xf * xf, axis=-1, keepdims=True) + RMS_EPS)
        h = (xf * r * g_ref[...]).astype(BF16)
        h_ref[...] = h
        for n0 in range(0, D_IN, 512):
            z_ref[:, n0:n0 + 512] = lax.dot_general(h, w_ref[n0:n0 + 512, :], _NT,
                                                    preferred_element_type=F32).astype(BF16)

    return _call(
        body, name="fwd_in", grid=(s // ts,),
        in_specs=[_rows(ts, D_MODEL), _const((1, D_MODEL)), _const((D_IN, D_MODEL))],
        out_specs=[_rows(ts, D_IN), _rows(ts, D_MODEL)],
        out_shape=[_sds((s, D_IN), BF16), _sds((s, D_MODEL), BF16)],
        scratch_shapes=[], args=(x, g1, w_in_t), exchange=exchange, forward_step=(s // ts) * 3 // 4)


def _p_u0(z_ref, rows):
    a_h = z_ref[rows, 0:D_A].astype(F32)
    a_c = z_ref[rows, 2 * D_A:3 * D_A].astype(F32)
    b_v = z_ref[rows, 3 * D_A:3 * D_A + D_B].astype(F32)
    b_g = z_ref[rows, 3 * D_A + D_B:D_IN].astype(F32)
    return a_c * a_h, b_v * _sigmoid(b_g)


def _layernorm_rows(u_blocks):
    tot = None
    for ub in u_blocks:
        sm = jnp.sum(ub, axis=-1, keepdims=True)
        tot = sm if tot is None else tot + sm
    mu = tot * (1.0 / D_B)
    var = None
    for ub in u_blocks:
        sq = jnp.sum((ub - mu) * (ub - mu), axis=-1, keepdims=True)
        var = sq if var is None else var + sq
    rstd = lax.rsqrt(var * (1.0 / D_B) + LN_EPS)
    return mu, rstd


def _fwd_mix(z, x, wa, wb, bb, lg, lb, w_out, ts, exchange=None):
    s = x.shape[0]
    nt = s // ts
    rc = min(ROW_CHUNK, ts)

    def body(z_ref, zp_ref, zn_ref, x_ref, wa_ref, wb_ref, bb_ref, lg_ref, lb_ref, wo_ref,
             x2_ref, y_ref, u_ref, pe_ref, ue_ref):
        i = pl.program_id(0)
        pp, up = _p_u0(zp_ref, slice(None))
        pn, un = _p_u0(zn_ref, slice(None))
        _fill_halo(pe_ref, pp, pn, ts, i == 0, i == nt - 1)
        _fill_halo(ue_ref, up, un, ts, i == 0, i == nt - 1)

        def fill(j, carry):
            r0 = pl.multiple_of(j * rc, rc)
            p, u0 = _p_u0(z_ref, pl.ds(r0, rc))
            _put_rows(pe_ref, r0, rc, p)
            _put_rows(ue_ref, r0, rc, u0)
            return carry

        lax.fori_loop(0, ts // rc, fill, 0)

        def main(j, carry):
            r0 = pl.multiple_of(j * rc, rc)
            rows = pl.ds(r0, rc)
            for l0 in range(0, D_A, LANES):
                ca = _conv_block(pe_ref, wa_ref, r0, rc, l0, K_A, False)
                a_b = z_ref[rows, D_A + l0:D_A + l0 + LANES].astype(F32)
                y_ref[rows, l0:l0 + LANES] = (a_b * ca).astype(BF16)
            ubs = []
            for l0 in range(0, D_B, LANES):
                ub = _conv_block(ue_ref, wb_ref, r0, rc, l0, K_B, False) + bb_ref[:, l0:l0 + LANES]
                u_ref[rows, l0:l0 + LANES] = ub
                ubs.append(ub)
            mu, rstd = _layernorm_rows(ubs)
            for q, l0 in enumerate(range(0, D_B, LANES)):
                t = (ubs[q] - mu) * rstd * lg_ref[:, l0:l0 + LANES] + lb_ref[:, l0:l0 + LANES]
                y_ref[rows, D_A + l0:D_A + l0 + LANES] = (t * _sigmoid(t)).astype(BF16)
            return carry

        lax.fori_loop(0, ts // rc, main, 0)
        for n0 in range(0, D_MODEL, 512):
            x2_ref[:, n0:n0 + 512] = x_ref[:, n0:n0 + 512] + jnp.dot(
                y_ref[...], wo_ref[:, n0:n0 + 512], preferred_element_type=F32)

    return _call(
        body, name="fwd_mix", grid=(nt,),
        in_specs=[_rows(ts, D_IN), _prev(ts, D_IN), _next(ts, D_IN, s), _rows(ts, D_MODEL),
                  _const((K_A, D_A)), _const((K_B, D_B)), _const((1, D_B)), _const((1, D_B)), _const((1, D_B)),
                  _const((D_MODEL, D_MODEL))],
        out_specs=[_rows(ts, D_MODEL), _rows(ts, D_MODEL), _rows(ts, D_B)],
        out_shape=[_sds((s, D_MODEL), F32), _sds((s, D_MODEL), BF16), _sds((s, D_B), F32)],
        scratch_shapes=[_ext_scratch(ts, D_A), _ext_scratch(ts, D_B)],
        args=(z, z, z, x, wa, wb, bb, lg, lb, w_out), exchange=exchange, forward_step=nt * 5 // 8)


def _fwd_ffn_in(x2, g2, w_gate_t, w_up_t, ts):
    s = x2.shape[0]
    half = D_FF // 2

    def body(x_ref, g_ref, wg_ref, wu_ref, g0_ref, v_ref, h_ref):
        xf = x_ref[...]
        r = lax.rsqrt(jnp.mean(xf * xf, axis=-1, keepdims=True) + RMS_EPS)
        h = (xf * r * g_ref[...]).astype(BF16)
        h_ref[...] = h
        for n0 in range(0, D_FF, half):
            g0_ref[:, n0:n0 + half] = lax.dot_general(h, wg_ref[n0:n0 + half, :], _NT,
                                                      preferred_element_type=F32).astype(BF16)
            v_ref[:, n0:n0 + half] = lax.dot_general(h, wu_ref[n0:n0 + half, :], _NT,
                                                     preferred_element_type=F32).astype(BF16)

    return pl.pallas_call(
        body, name="fwd_ffn_in", grid=(s // ts,),
        in_specs=[_rows(ts, D_MODEL), _const((1, D_MODEL)), _const((D_FF, D_MODEL)), _const((D_FF, D_MODEL))],
        out_specs=[_rows(ts, D_FF), _rows(ts, D_FF), _rows(ts, D_MODEL)],
        out_shape=[_sds((s, D_FF), BF16), _sds((s, D_FF), BF16), _sds((s, D_MODEL), BF16)],
        compiler_params=_params(),
    )(x2, g2, w_gate_t, w_up_t)


def _ffn_out_and_back(g0, v, x2, wf, w_down, g3, target, ts):
    s = x2.shape[0]
    nt = s // ts
    rc = min(ROW_CHUNK, ts)
    half = D_FF // 2

    def body(g0_ref, gp_ref, gn_ref, v_ref, x2_ref, wf_ref, wd_ref, g3_ref, t_ref,
             a_ref, dx3_ref, dx3b_ref, loss_ref, dg3_ref, dg_ref, dv_ref, dwf_ref,
             ge_ref, silu_ref, dsv_ref, da_ref, acc_ref):
        i = pl.program_id(0)

        @pl.when(i == 0)
        def _():
            acc_ref[...] = jnp.zeros_like(acc_ref)

        _fill_halo(ge_ref, gp_ref[...].astype(F32), gn_ref[...].astype(F32), ts, i == 0, i == nt - 1)

        def fill(j, carry):
            r0 = pl.multiple_of(j * rc, rc)
            _put_rows(ge_ref, r0, rc, g0_ref[pl.ds(r0, rc), :].astype(F32))
            return carry

        lax.fori_loop(0, ts // rc, fill, 0)

        def act(j, carry):
            r0 = pl.multiple_of(j * rc, rc)
            rows = pl.ds(r0, rc)
            for l0 in range(0, D_FF, LANES):
                g = _conv_block(ge_ref, wf_ref, r0, rc, l0, K_F, False)
                vv = v_ref[rows, l0:l0 + LANES].astype(F32)
                sg = _sigmoid(g)
                silu = g * sg
                a_ref[rows, l0:l0 + LANES] = (silu * vv).astype(BF16)
                silu_ref[rows, l0:l0 + LANES] = silu
                dsv_ref[rows, l0:l0 + LANES] = (sg + silu * (1.0 - sg)) * vv
            return carry

        lax.fori_loop(0, ts // rc, act, 0)
        for n0 in range(0, D_MODEL, 512):
            dx3_ref[:, n0:n0 + 512] = x2_ref[:, n0:n0 + 512] + jnp.dot(
                a_ref[...], wd_ref[:, n0:n0 + 512], preferred_element_type=F32)

        @pl.when(i == 0)
        def _():
            loss_ref[...] = jnp.zeros_like(loss_ref)
            dg3_ref[...] = jnp.zeros_like(dg3_ref)

        def tail(j, carry):
            lsum, dgsum = carry
            r0 = pl.multiple_of(j * rc, rc)
            rows = pl.ds(r0, rc)
            x3 = dx3_ref[rows, :]
            r = lax.rsqrt(jnp.mean(x3 * x3, axis=-1, keepdims=True) + RMS_EPS)
            xhat = x3 * r
            diff = xhat * g3_ref[...] - t_ref[rows, :]
            dout = diff * (1.0 / D_MODEL)
            dxh = dout * g3_ref[...]
            dx3 = r * (dxh - xhat * jnp.mean(dxh * xhat, axis=-1, keepdims=True))
            dx3_ref[rows, :] = dx3
            dx3b_ref[rows, :] = dx3.astype(BF16)
            lsum = lsum + _fold8(diff * diff)
            dgsum = dgsum + _fold8(dout * xhat)
            return lsum, dgsum

        zero = jnp.zeros((SUBLANES, D_MODEL), F32)
        lsum, dgsum = lax.fori_loop(0, ts // rc, tail, (zero, zero), unroll=2)
        loss_ref[...] = loss_ref[...] + (0.5 / D_MODEL) * jnp.sum(lsum, keepdims=True)
        dg3_ref[...] = dg3_ref[...] + jnp.sum(dgsum, axis=0, keepdims=True)

        for n0 in range(0, D_FF, half):
            da_ref[:, n0:n0 + half] = lax.dot_general(dx3b_ref[...], wd_ref[n0:n0 + half, :], _NT,
                                                      preferred_element_type=F32)

        def back(j, carry):
            r0 = pl.multiple_of(j * rc, rc)
            rows = pl.ds(r0, rc)
            for l0 in range(0, D_FF, LANES):
                da = da_ref[rows, l0:l0 + LANES]
                dv_ref[rows, l0:l0 + LANES] = (da * silu_ref[rows, l0:l0 + LANES]).astype(BF16)
                dgg = da * dsv_ref[rows, l0:l0 + LANES]
                dg_ref[rows, l0:l0 + LANES] = dgg.astype(BF16)
                _conv_wgrad_block(acc_ref, dgg, ge_ref, r0, rc, l0, K_F)
            return carry

        lax.fori_loop(0, ts // rc, back, 0)

        @pl.when(i == nt - 1)
        def _():
            _reduce_acc(dwf_ref, acc_ref, K_F)

    return pl.pallas_call(
        body, name="ffn_out_and_back", grid=(nt,),
        in_specs=[_rows(ts, D_FF), _prev(ts, D_FF), _next(ts, D_FF, s), _rows(ts, D_FF), _rows(ts, D_MODEL),
                  _const((K_F, D_FF)), _const((D_FF, D_MODEL)), _const((1, D_MODEL)), _rows(ts, D_MODEL)],
        out_specs=[_rows(ts, D_FF), _rows(ts, D_MODEL), _rows(ts, D_MODEL), _acc_out((1, 1)), _acc_out((1, D_MODEL)),
                   _rows(ts, D_FF), _rows(ts, D_FF), _acc_out((K_F, D_FF))],
        out_shape=[_sds((s, D_FF), BF16), _sds((s, D_MODEL), F32), _sds((s, D_MODEL), BF16),
                   _sds((1, 1), F32), _sds((1, D_MODEL), F32),
                   _sds((s, D_FF), BF16), _sds((s, D_FF), BF16), _sds((K_F, D_FF), F32)],
        scratch_shapes=[_ext_scratch(ts, D_FF), pltpu.VMEM((ts, D_FF), F32), pltpu.VMEM((ts, D_FF), F32),
                        pltpu.VMEM((ts, D_FF), F32), pltpu.VMEM((K_F, SUBLANES, D_FF), F32)],
        compiler_params=_params(),
    )(g0, g0, g0, v, x2, wf, w_down, g3, target)


def _bwd_ffn_b(dg, dv, wf, w_gate, w_up, x2, g2, dx3, ts):
    s = x2.shape[0]
    nt = s // ts
    rc = min(ROW_CHUNK, ts)
    n_sub = ts // rc

    def body(dg_ref, dgp_ref, dgn_ref, dv_ref, wf_ref, wg_ref, wu_ref, x2_ref, g2_ref, dx3_ref,
             dg0_ref, dx2_ref, dx2b_ref, dgn2_ref, dge_ref, dg8_ref, a0_ref, a1_ref, p0_ref, p1_ref):
        i = pl.program_id(0)
        vt = jnp.minimum(i, nt - 1)
        live = (i >= 2).astype(F32)

        @pl.when(i == 0)
        def _():
            dg8_ref[...] = jnp.zeros_like(dg8_ref)
            a1_ref[...] = jnp.zeros_like(a1_ref)
            p1_ref[...] = jnp.zeros_like(p1_ref)

        _fill_halo(dge_ref, dgp_ref[...].astype(F32), dgn_ref[...].astype(F32), ts, vt == 0, vt == nt - 1)

        def fill(j, carry):
            r0 = pl.multiple_of(j * rc, rc)
            _put_rows(dge_ref, r0, rc, dg_ref[pl.ds(r0, rc), :].astype(F32))
            return carry

        lax.fori_loop(0, n_sub, fill, 0)

        def stage(a_new, a_old, p_new, p_old):
            def conv_t(r0, l0):
                rows = slice(r0, r0 + rc)
                dg0 = _conv_block(dge_ref, wf_ref, r0, rc, l0, K_F, True).astype(BF16)
                dg0_ref[rows, l0:l0 + LANES] = dg0
                a_new[rows, l0:l0 + LANES] = dg0

            def tail(r0):
                rows = slice(r0, r0 + rc)
                dx, dgrow = _rms_bwd_rows(p_old[rows, :], x2_ref[rows, :], g2_ref[...])
                dx2 = dx3_ref[rows, :] + dx
                dx2_ref[rows, :] = dx2
                dx2b_ref[rows, :] = dx2.astype(BF16)
                dg8_ref[...] = dg8_ref[...] + _fold8(dgrow) * live

            units = []
            for q in range(n_sub):
                units += [(conv_t, (q * rc, l0)) for l0 in range(0, D_FF, LANES)]
                units.append((tail, (q * rc,)))
            _interleaved(units, _matmul_pieces([(a_old, wg_ref), (dv_ref, wu_ref)], p_new, 2))

        @pl.when(i % 2 == 0)
        def _():
            stage(a0_ref, a1_ref, p0_ref, p1_ref)

        @pl.when(i % 2 == 1)
        def _():
            stage(a1_ref, a0_ref, p1_ref, p0_ref)

        @pl.when(i == nt + 1)
        def _():
            dgn2_ref[...] = jnp.sum(dg8_ref[...], axis=0, keepdims=True)

    vtile = lambda i: jnp.minimum(i, nt - 1)
    mtile = lambda i: jnp.clip(i - 1, 0, nt - 1)
    ttile = lambda i: jnp.clip(i - 2, 0, nt - 1)
    return pl.pallas_call(
        body, name="bwd_ffn_b", grid=(nt + 2,),
        in_specs=[_rows_at(ts, D_FF, vtile), _prev_at(ts, D_FF, vtile), _next_at(ts, D_FF, s, vtile),
                  _rows_at(ts, D_FF, mtile), _const((K_F, D_FF)),
                  _const((D_FF, D_MODEL)), _const((D_FF, D_MODEL)), _rows_at(ts, D_MODEL, ttile), _const((1, D_MODEL)),
                  _rows_at(ts, D_MODEL, ttile)],
        out_specs=[_rows_at(ts, D_FF, vtile), _rows_at(ts, D_MODEL, ttile), _rows_at(ts, D_MODEL, ttile),
                   _acc_out((1, D_MODEL))],
        out_shape=[_sds((s, D_FF), BF16), _sds((s, D_MODEL), F32), _sds((s, D_MODEL), BF16), _sds((1, D_MODEL), F32)],
        scratch_shapes=[_ext_scratch(ts, D_FF), pltpu.VMEM((SUBLANES, D_MODEL), F32),
                        pltpu.VMEM((ts, D_FF), BF16), pltpu.VMEM((ts, D_FF), BF16),
                        pltpu.VMEM((ts, D_MODEL), F32), pltpu.VMEM((ts, D_MODEL), F32)],
        compiler_params=_params(),
    )(dg, dg, dg, dv, wf, w_gate, w_up, x2, g2, dx3)


def _bwd_mix_a(dx2b, w_out_t, z, u, wa, lg, lb, ts, exchange=None):
    s = dx2b.shape[0]
    nt = s // ts
    rc = min(ROW_CHUNK, ts)
    n_sub = ts // rc

    def body(dx_ref, wo_ref, z_ref, zp_ref, zn_ref, u_ref, wa_ref, lg_ref, lb_ref,
             dca_ref, du_ref, dab_ref, dwa_ref, dlg_ref, dlb_ref, dbb_ref, pe_ref, dy0_ref, dy1_ref, acc_ref, sacc_ref):
        i = pl.program_id(0)
        t = jnp.maximum(i - 1, 0)

        @pl.when(i == 0)
        def _():
            acc_ref[...] = jnp.zeros_like(acc_ref)
            sacc_ref[...] = jnp.zeros_like(sacc_ref)
            dy1_ref[...] = jnp.zeros_like(dy1_ref)

        pp, _ = _p_u0(zp_ref, slice(None))
        pn, _ = _p_u0(zn_ref, slice(None))
        _fill_halo(pe_ref, pp, pn, ts, t == 0, t == nt - 1)

        def fill(j, carry):
            r0 = pl.multiple_of(j * rc, rc)
            rows = pl.ds(r0, rc)
            _put_rows(pe_ref, r0, rc, z_ref[rows, 2 * D_A:3 * D_A].astype(F32) * z_ref[rows, 0:D_A].astype(F32))
            return carry

        lax.fori_loop(0, n_sub, fill, 0)

        def stage(dy_new, dy_old):
            def piece(m0, n0, width):
                dy_new[m0:m0 + MXU_ROWS, n0:n0 + width] = jnp.dot(
                    dx_ref[m0:m0 + MXU_ROWS, :], wo_ref[:, n0:n0 + width], preferred_element_type=F32).astype(BF16)

            units = []
            for q in range(n_sub):
                units += [(mixer_a, (dy_old, q * rc, l0)) for l0 in range(0, D_A, LANES)]
                units.append((mixer_b, (dy_old, q * rc)))
            _interleaved(units, [(piece, (m0, n0, w)) for n0, w in _col_pieces(D_MODEL) for m0 in range(0, ts, MXU_ROWS)])

        def mixer_a(dy_ref, r0, l0):
            rows = slice(r0, r0 + rc)
            ca = _conv_block(pe_ref, wa_ref, r0, rc, l0, K_A, False)
            a_b = z_ref[rows, D_A + l0:D_A + l0 + LANES].astype(F32)
            dya = dy_ref[rows, l0:l0 + LANES].astype(F32)
            dab_ref[rows, l0:l0 + LANES] = (dya * ca).astype(BF16)
            dca = dya * a_b
            dca_ref[rows, l0:l0 + LANES] = dca
            _conv_wgrad_block(acc_ref, dca, pe_ref, r0, rc, l0, K_A)

        def mixer_b(dy_ref, r0):
            rows = slice(r0, r0 + rc)
            ubs = [u_ref[rows, l0:l0 + LANES] for l0 in range(0, D_B, LANES)]
            mu, rstd = _layernorm_rows(ubs)
            ns, dns = [], []
            m1 = None
            m2 = None
            for q, l0 in enumerate(range(0, D_B, LANES)):
                n = (ubs[q] - mu) * rstd
                lgq = lg_ref[:, l0:l0 + LANES]
                t = n * lgq + lb_ref[:, l0:l0 + LANES]
                sg = _sigmoid(t)
                dt = dy_ref[rows, D_A + l0:D_A + l0 + LANES].astype(F32) * (sg * (1.0 + t * (1.0 - sg)))
                dn = dt * lgq
                ns.append(n)
                dns.append(dn)
                s1 = jnp.sum(dn, axis=-1, keepdims=True)
                s2 = jnp.sum(dn * n, axis=-1, keepdims=True)
                m1 = s1 if m1 is None else m1 + s1
                m2 = s2 if m2 is None else m2 + s2
                sacc_ref[0, :, l0:l0 + LANES] = sacc_ref[0, :, l0:l0 + LANES] + _fold8(dt * n)
                sacc_ref[1, :, l0:l0 + LANES] = sacc_ref[1, :, l0:l0 + LANES] + _fold8(dt)
            m1 = m1 * (1.0 / D_B)
            m2 = m2 * (1.0 / D_B)
            for q, l0 in enumerate(range(0, D_B, LANES)):
                du = rstd * (dns[q] - m1 - ns[q] * m2)
                du_ref[rows, l0:l0 + LANES] = du
                sacc_ref[2, :, l0:l0 + LANES] = sacc_ref[2, :, l0:l0 + LANES] + _fold8(du)

        @pl.when(i % 2 == 0)
        def _():
            stage(dy0_ref, dy1_ref)

        @pl.when(i % 2 == 1)
        def _():
            stage(dy1_ref, dy0_ref)

        @pl.when(i == nt)
        def _():
            _reduce_acc(dwa_ref, acc_ref, K_A)
            dlg_ref[...] = jnp.sum(sacc_ref[0], axis=0, keepdims=True)
            dlb_ref[...] = jnp.sum(sacc_ref[1], axis=0, keepdims=True)
            dbb_ref[...] = jnp.sum(sacc_ref[2], axis=0, keepdims=True)

    cur = lambda i: jnp.minimum(i, nt - 1)
    old = lambda i: jnp.maximum(i - 1, 0)
    return _call(
        body, name="bwd_mix_a", grid=(nt + 1,),
        in_specs=[_rows_at(ts, D_MODEL, cur), _const((D_MODEL, D_MODEL)), _rows_at(ts, D_IN, old), _prev_at(ts, D_IN, old),
                  _next_at(ts, D_IN, s, old), _rows_at(ts, D_B, old), _const((K_A, D_A)), _const((1, D_B)), _const((1, D_B))],
        out_specs=[_rows_at(ts, D_A, old), _rows_at(ts, D_B, old), _rows_at(ts, D_A, old), _acc_out((K_A, D_A)),
                   _acc_out((1, D_B)), _acc_out((1, D_B)), _acc_out((1, D_B))],
        out_shape=[_sds((s, D_A), F32), _sds((s, D_B), F32), _sds((s, D_A), BF16), _sds((K_A, D_A), F32),
                   _sds((1, D_B), F32), _sds((1, D_B), F32), _sds((1, D_B), F32)],
        scratch_shapes=[_ext_scratch(ts, D_A), pltpu.VMEM((ts, D_MODEL), BF16), pltpu.VMEM((ts, D_MODEL), BF16),
                        pltpu.VMEM((K_A, SUBLANES, D_A), F32), pltpu.VMEM((3, SUBLANES, D_B), F32)],
        args=(dx2b, w_out_t, z, z, z, u, wa, lg, lb), exchange=exchange)


def _bwd_mix_b(dca, du, z, dab, wa, wb, w_in, x, g1, dx2, ts, exchange=None):
    s = x.shape[0]
    nt = s // ts
    rc = min(ROW_CHUNK, ts)
    n_sub = ts // rc

    def body(dca_ref, dcap_ref, dcan_ref, du_ref, dup_ref, dun_ref, z_ref, zp_ref, zn_ref, dab_ref,
             wa_ref, wb_ref, wi_ref, x_ref, g1_ref, dx2_ref,
             dz_ref, dx_ref, dg1_ref, dwb_ref, dcae_ref, due_ref, ue_ref, acc_ref, dg8_ref,
             dz0_ref, dz1_ref, dh0_ref, dh1_ref):
        i = pl.program_id(0)

        @pl.when(i == 0)
        def _():
            acc_ref[...] = jnp.zeros_like(acc_ref)
            dg8_ref[...] = jnp.zeros_like(dg8_ref)
            dz1_ref[...] = jnp.zeros_like(dz1_ref)
            dh1_ref[...] = jnp.zeros_like(dh1_ref)

        vt = jnp.minimum(i, nt - 1)
        first = vt == 0
        last = vt == nt - 1
        live = (i < nt).astype(F32)
        _fill_halo(dcae_ref, dcap_ref[...], dcan_ref[...], ts, first, last)
        _fill_halo(due_ref, dup_ref[...], dun_ref[...], ts, first, last)
        _, up = _p_u0(zp_ref, slice(None))
        _, un = _p_u0(zn_ref, slice(None))
        _fill_halo(ue_ref, up, un, ts, first, last)

        def fill(j, carry):
            r0 = pl.multiple_of(j * rc, rc)
            rows = pl.ds(r0, rc)
            _put_rows(dcae_ref, r0, rc, dca_ref[rows, :])
            _put_rows(due_ref, r0, rc, du_ref[rows, :])
            b_v = z_ref[rows, 3 * D_A:3 * D_A + D_B].astype(F32)
            b_g = z_ref[rows, 3 * D_A + D_B:D_IN].astype(F32)
            _put_rows(ue_ref, r0, rc, b_v * _sigmoid(b_g))
            return carry

        lax.fori_loop(0, n_sub, fill, 0)

        def stage(dz_new, dz_old, dh_new, dh_old):
            def put(rows, c0, val):
                dz_ref[rows, c0:c0 + LANES] = val
                dz_new[rows, c0:c0 + LANES] = val

            def mixer_a(r0, l0):
                rows = slice(r0, r0 + rc)
                dp = _conv_block(dcae_ref, wa_ref, r0, rc, l0, K_A, True)
                a_h = z_ref[rows, l0:l0 + LANES].astype(F32)
                a_c = z_ref[rows, 2 * D_A + l0:2 * D_A + l0 + LANES].astype(F32)
                put(rows, l0, (dp * a_c).astype(BF16))
                put(rows, D_A + l0, dab_ref[rows, l0:l0 + LANES])
                put(rows, 2 * D_A + l0, (dp * a_h).astype(BF16))

            def mixer_b(r0, l0):
                rows = slice(r0, r0 + rc)
                du0 = _conv_block(due_ref, wb_ref, r0, rc, l0, K_B, True)
                b_v = z_ref[rows, 3 * D_A + l0:3 * D_A + l0 + LANES].astype(F32)
                b_g = z_ref[rows, 3 * D_A + D_B + l0:3 * D_A + D_B + l0 + LANES].astype(F32)
                sg = _sigmoid(b_g)
                put(rows, 3 * D_A + l0, (du0 * sg).astype(BF16))
                put(rows, 3 * D_A + D_B + l0, (du0 * b_v * (sg * (1.0 - sg))).astype(BF16))
                _conv_wgrad_block(acc_ref, du_ref[rows, l0:l0 + LANES], ue_ref, r0, rc, l0, K_B, live)

            def tail(r0):
                rows = slice(r0, r0 + rc)
                dx, dgrow = _rms_bwd_rows(dh_old[rows, :], x_ref[rows, :], g1_ref[...])
                dx_ref[rows, :] = dx2_ref[rows, :] + dx
                dg8_ref[...] = dg8_ref[...] + _fold8(dgrow)

            units = []
            for q in range(n_sub):
                units += [(mixer_a, (q * rc, l0)) for l0 in range(0, D_A, LANES)]
                units += [(mixer_b, (q * rc, l0)) for l0 in range(0, D_B, LANES)]
                units.append((tail, (q * rc,)))
            _interleaved(units, _matmul_pieces([(dz_old, wi_ref)], dh_new, 2))

        @pl.when(i % 2 == 0)
        def _():
            stage(dz0_ref, dz1_ref, dh0_ref, dh1_ref)

        @pl.when(i % 2 == 1)
        def _():
            stage(dz1_ref, dz0_ref, dh1_ref, dh0_ref)

        @pl.when(i == nt + 1)
        def _():
            _reduce_acc(dwb_ref, acc_ref, K_B)
            dg1_ref[...] = jnp.sum(dg8_ref[...], axis=0, keepdims=True)

    vtile = lambda i: jnp.minimum(i, nt - 1)
    ttile = lambda i: jnp.clip(i - 2, 0, nt - 1)
    return _call(
        body, name="bwd_mix_b", grid=(nt + 2,),
        in_specs=[_rows_at(ts, D_A, vtile), _prev_at(ts, D_A, vtile), _next_at(ts, D_A, s, vtile),
                  _rows_at(ts, D_B, vtile), _prev_at(ts, D_B, vtile), _next_at(ts, D_B, s, vtile),
                  _rows_at(ts, D_IN, vtile), _prev_at(ts, D_IN, vtile), _next_at(ts, D_IN, s, vtile), _rows_at(ts, D_A, vtile),
                  _const((K_A, D_A)), _const((K_B, D_B)), _const((D_IN, D_MODEL)), _rows_at(ts, D_MODEL, ttile),
                  _const((1, D_MODEL)), _rows_at(ts, D_MODEL, ttile)],
        out_specs=[_rows_at(ts, D_IN, vtile), _rows_at(ts, D_MODEL, ttile), _acc_out((1, D_MODEL)), _acc_out((K_B, D_B))],
        out_shape=[_sds((s, D_IN), BF16), _sds((s, D_MODEL), F32), _sds((1, D_MODEL), F32), _sds((K_B, D_B), F32)],
        scratch_shapes=[_ext_scratch(ts, D_A), _ext_scratch(ts, D_B), _ext_scratch(ts, D_B),
                        pltpu.VMEM((K_B, SUBLANES, D_B), F32), pltpu.VMEM((SUBLANES, D_MODEL), F32),
                        pltpu.VMEM((ts, D_IN), BF16), pltpu.VMEM((ts, D_IN), BF16),
                        pltpu.VMEM((ts, D_MODEL), F32), pltpu.VMEM((ts, D_MODEL), F32)],
        args=(dca, dca, dca, du, du, du, z, z, z, dab, wa, wb, w_in, x, g1, dx2), exchange=exchange)


def _matmul_tn(a, b, name):
    s, m = a.shape
    n = b.shape[1]
    tk = min(1024, s)
    nk = s // tk
    tm = 256

    def body(a_ref, b_ref, o_ref, acc_ref):
        k = pl.program_id(0)

        @pl.when(k == 0)
        def _():
            acc_ref[...] = jnp.zeros_like(acc_ref)

        for m0 in range(0, m, tm):
            acc_ref[m0:m0 + tm, :] = acc_ref[m0:m0 + tm, :] + lax.dot_general(
                a_ref[:, m0:m0 + tm], b_ref[...], _TN, preferred_element_type=F32)

        @pl.when(k == nk - 1)
        def _():
            o_ref[...] = acc_ref[...].astype(BF16)

    return pl.pallas_call(
        body, name=name, grid=(nk,),
        in_specs=[_rows(tk, m), _rows(tk, n)],
        out_specs=_acc_out((m, n)),
        out_shape=_sds((m, n), BF16),
        scratch_shapes=[pltpu.VMEM((m, n), F32)],
        compiler_params=_params(),
    )(a, b)


CHIP_RELS = ((1, 0, 0), (0, 1, 0), (1, 1, 0))
CORE_RELS = ((0, 0, 1),)
ALL_RELS = ((0, 0, 1), (0, 1, 0), (0, 1, 1), (1, 0, 0), (1, 0, 1), (1, 1, 0), (1, 1, 1))


def _chip_slot(dev):
    return 2 * dev[0] + dev[1]


def _dev_slot(dev):
    return 4 * dev[0] + 2 * dev[1] + dev[2]


def _me():
    return (lax.axis_index("x"), lax.axis_index("y"), lax.axis_index("c"))


def _peer(me, rel):
    return tuple((1 - me[a]) if rel[a] else me[a] for a in range(3))


_ANY = pl.BlockSpec(memory_space=pl.ANY)


class _Exchange:
    def __init__(self, inputs, out_shape, scratch, start, finish, forward=None):
        self.inputs, self.out_shape, self.scratch = list(inputs), list(out_shape), list(scratch)
        self.start, self.finish, self.forward = start, finish, forward


def _all_gather(payloads):
    n_p = len(payloads)
    n_k = 1 + 2 * len(CHIP_RELS)

    def copy(srcs, dsts, sems, p, k, block_dev, to, from_src):
        blk = dsts[p].at[_dev_slot(block_dev)]
        return pltpu.make_async_remote_copy(
            src_ref=srcs[p] if from_src else blk, dst_ref=blk,
            send_sem=sems[0].at[n_k * p + k], recv_sem=sems[1].at[n_k * p + k], device_id=to, device_id_type=MESH)

    def own_copy(srcs, dsts, sems, p):
        return pltpu.make_async_copy(srcs[p], dsts[p].at[_dev_slot(_me())], sems[2].at[p])

    def start(srcs, dsts, sems):
        me = _me()
        for p in range(n_p):
            own_copy(srcs, dsts, sems, p).start()
        for j, rel in enumerate(CHIP_RELS):
            for p in range(n_p):
                copy(srcs, dsts, sems, p, 1 + j, me, _peer(me, rel), True).start()
        for p in range(n_p):
            copy(srcs, dsts, sems, p, 0, me, _peer(me, CORE_RELS[0]), True).start()

    def forward(srcs, dsts, sems):
        me = _me()
        sibling = _peer(me, CORE_RELS[0])
        for j, rel in enumerate(CHIP_RELS):
            other = _peer(me, rel)
            for p in range(n_p):
                copy(srcs, dsts, sems, p, 1 + j, other, me, False).wait_recv()
                copy(srcs, dsts, sems, p, 4 + j, other, sibling, False).start()

    def finish(srcs, dsts, sems):
        me = _me()
        sibling = _peer(me, CORE_RELS[0])
        for p in range(n_p):
            copy(srcs, dsts, sems, p, 0, sibling, me, False).wait_recv()
        for j, rel in enumerate(CHIP_RELS):
            for p in range(n_p):
                copy(srcs, dsts, sems, p, 4 + j, _peer(sibling, rel), me, False).wait_recv()
        for p in range(n_p):
            own_copy(srcs, dsts, sems, p).wait()
            copy(srcs, dsts, sems, p, 0, me, sibling, True).wait_send()
            for j, rel in enumerate(CHIP_RELS):
                copy(srcs, dsts, sems, p, 1 + j, me, _peer(me, rel), True).wait_send()
                copy(srcs, dsts, sems, p, 4 + j, _peer(me, rel), sibling, False).wait_send()

    return _Exchange(
        payloads, [_sds((N_DEV,) + p.shape, p.dtype) for p in payloads],
        [pltpu.SemaphoreType.DMA((n_p * n_k,)), pltpu.SemaphoreType.DMA((n_p * n_k,)), pltpu.SemaphoreType.DMA((n_p,))],
        start, finish, forward)


def _gather_direct(payload):
    n_r = len(ALL_RELS)

    def copies(srcs, dsts, sems):
        me = _me()
        mine = dsts[0].at[_dev_slot(me)]
        own = pltpu.make_async_copy(srcs[0], mine, sems[2].at[0])
        remote = [pltpu.make_async_remote_copy(src_ref=srcs[0], dst_ref=mine, send_sem=sems[0].at[k], recv_sem=sems[1].at[k],
                                               device_id=_peer(me, rel), device_id_type=MESH)
                  for k, rel in enumerate(ALL_RELS)]
        return [own] + remote

    def start(srcs, dsts, sems):
        for cp in copies(srcs, dsts, sems):
            cp.start()

    def finish(srcs, dsts, sems):
        for cp in copies(srcs, dsts, sems):
            cp.wait()

    return _Exchange([payload], [_sds((N_DEV,) + payload.shape, payload.dtype)],
                     [pltpu.SemaphoreType.DMA((n_r,)), pltpu.SemaphoreType.DMA((n_r,)), pltpu.SemaphoreType.DMA((1,))],
                     start, finish)


def _scatter_exchange(payloads, rels, src_view, view_shapes):
    n_p = len(payloads)
    n_r = len(rels)

    def copies(srcs, dsts, sems):
        me = _me()
        out = []
        for k, rel in enumerate(rels):
            peer = _peer(me, rel)
            for p in range(n_p):
                out.append(pltpu.make_async_remote_copy(
                    src_ref=src_view(srcs[p], peer), dst_ref=dsts[p].at[k],
                    send_sem=sems[0].at[p * n_r + k], recv_sem=sems[1].at[p * n_r + k],
                    device_id=peer, device_id_type=MESH))
        return out

    def start(srcs, dsts, sems):
        for cp in copies(srcs, dsts, sems):
            cp.start()

    def finish(srcs, dsts, sems):
        for cp in copies(srcs, dsts, sems):
            cp.wait()

    return _Exchange(payloads, [_sds((n_r,) + vs, p.dtype) for vs, p in zip(view_shapes, payloads)],
                     [pltpu.SemaphoreType.DMA((n_p * n_r,)), pltpu.SemaphoreType.DMA((n_p * n_r,))], start, finish)


def _split_refs(refs, sizes):
    out, at = [], 0
    for n in sizes:
        out.append(refs[at:at + n])
        at += n
    return out


def _run_exchanges(name, exchanges):
    n_in = [len(e.inputs) for e in exchanges]
    n_out = [len(e.out_shape) for e in exchanges]
    n_sc = [len(e.scratch) for e in exchanges]

    def body(*refs):
        ins, outs, scs = _split_refs(refs, [sum(n_in), sum(n_out), sum(n_sc)])
        parts = list(zip(exchanges, _split_refs(ins, n_in), _split_refs(outs, n_out), _split_refs(scs, n_sc)))
        for e, i, o, s in parts:
            e.start(i, o, s)
        for e, i, o, s in parts:
            if e.forward is not None:
                e.forward(i, o, s)
        for e, i, o, s in parts:
            e.finish(i, o, s)

    outs = pl.pallas_call(
        body, name=name, in_specs=[_ANY] * sum(n_in), out_specs=[_ANY] * sum(n_out),
        out_shape=[sd for e in exchanges for sd in e.out_shape],
        scratch_shapes=[sc for e in exchanges for sc in e.scratch],
    )(*[a for e in exchanges for a in e.inputs])
    return _split_refs(list(outs), n_out)


def _call(body, *, name, grid, in_specs, out_specs, out_shape, scratch_shapes, args, exchange=None, forward_step=None):
    n_in, n_out, n_sc = len(in_specs), len(out_specs), len(scratch_shapes)
    if exchange is None:
        outs = pl.pallas_call(body, name=name, grid=grid, in_specs=in_specs, out_specs=out_specs, out_shape=out_shape,
                              scratch_shapes=scratch_shapes, compiler_params=_params())(*args)
        return list(outs), []
    e = exchange
    sizes = [n_in, len(e.inputs), n_out, len(e.out_shape), n_sc, len(e.scratch)]
    last = grid[0] - 1

    def wrapped(*refs):
        a, ei, o, eo, sc, es = _split_refs(refs, sizes)
        i = pl.program_id(0)

        @pl.when(i == 0)
        def _():
            e.start(ei, eo, es)

        if e.forward is not None:
            @pl.when(i == forward_step)
            def _():
                e.forward(ei, eo, es)

        body(*a, *o, *sc)

        @pl.when(i == last)
        def _():
            e.finish(ei, eo, es)

    outs = pl.pallas_call(
        wrapped, name=name, grid=grid,
        in_specs=list(in_specs) + [_ANY] * len(e.inputs), out_specs=list(out_specs) + [_ANY] * len(e.out_shape),
        out_shape=list(out_shape) + e.out_shape, scratch_shapes=list(scratch_shapes) + e.scratch,
        compiler_params=_params(),
    )(*args, *e.inputs)
    outs = list(outs)
    return outs[:n_out], outs[n_out:]


def _pair_sum(grads, recvd, my_core, name):
    n_p = len(grads)

    def body(c_ref, *refs):
        del c_ref
        for p in range(n_p):
            refs[2 * n_p + p][...] = (refs[p][...].astype(F32) + refs[n_p + p][...].astype(F32)).astype(BF16)

    def blk(g):
        return (None, None) + g.shape[2:]

    return pl.pallas_call(
        body, name=name,
        grid_spec=pltpu.PrefetchScalarGridSpec(
            num_scalar_prefetch=1, grid=(N_CHIP,),
            in_specs=[pl.BlockSpec(blk(g), lambda j, c: (j, c[0], 0, 0)) for g in grads]
            + [pl.BlockSpec(blk(g), lambda j, c: (0, j, 0, 0)) for g in grads],
            out_specs=[pl.BlockSpec((None,) + g.shape[2:], lambda j, c: (j, 0, 0)) for g in grads]),
        out_shape=[_sds((N_CHIP,) + g.shape[2:], BF16) for g in grads],
        compiler_params=pltpu.CompilerParams(dimension_semantics=("arbitrary",), vmem_limit_bytes=VMEM_LIMIT),
    )(my_core, *grads, *recvd)


def _chip_sum(psums, recvd, my_chip, name):
    n_p = len(psums)

    def body(c_ref, *refs):
        del c_ref
        for p in range(n_p):
            acc = refs[p][...].astype(F32)
            for k in range(len(CHIP_RELS)):
                acc = acc + refs[n_p + p][k].astype(F32)
            refs[2 * n_p + p][...] = acc

    return pl.pallas_call(
        body, name=name,
        grid_spec=pltpu.PrefetchScalarGridSpec(
            num_scalar_prefetch=1, grid=(1,),
            in_specs=[pl.BlockSpec((None,) + g.shape[1:], lambda i, c: (c[0], 0, 0)) for g in psums]
            + [pl.BlockSpec(r.shape, lambda i, c: (0, 0, 0)) for r in recvd],
            out_specs=[pl.BlockSpec(g.shape[1:], lambda i, c: (0, 0)) for g in psums]),
        out_shape=[_sds(g.shape[1:], F32) for g in psums],
        compiler_params=pltpu.CompilerParams(dimension_semantics=("arbitrary",), vmem_limit_bytes=VMEM_LIMIT),
    )(my_chip, *psums, *recvd)


def _sum_devices(parts):
    def body(p_ref, o_ref):
        acc = p_ref[0]
        for j in range(1, N_DEV):
            acc = acc + p_ref[j]
        o_ref[...] = acc

    return pl.pallas_call(body, name="small_grad_sum", out_shape=_sds(parts.shape[1:], F32))(parts)


def _adamw(ws, gs, ms, vs, name):
    n_t = len(ws)

    def body(*refs):
        w_refs, g_refs, m_refs, v_refs = (refs[j * n_t:(j + 1) * n_t] for j in range(4))
        outs = refs[4 * n_t:]
        for k in range(n_t):
            gg = g_refs[k][...]
            mn = ADAM_B1 * m_refs[k][...] + (1.0 - ADAM_B1) * gg
            vn = ADAM_B2 * v_refs[k][...] + (1.0 - ADAM_B2) * (gg * gg)
            m_hat = mn / (1.0 - ADAM_B1 ** ADAM_STEP)
            v_hat = vn / (1.0 - ADAM_B2 ** ADAM_STEP)
            outs[3 * k][...] = -ADAM_LR * (m_hat / (jnp.sqrt(v_hat) + ADAM_EPS) + ADAM_WD * w_refs[k][...])
            outs[3 * k + 1][...] = mn
            outs[3 * k + 2][...] = vn

    out_shape = [_sds(w.shape, F32) for w in ws for _ in range(3)]
    return pl.pallas_call(body, name=name, out_shape=out_shape,
                          compiler_params=pltpu.CompilerParams(vmem_limit_bytes=VMEM_LIMIT))(*ws, *gs, *ms, *vs)


class _Mesh:
    def __init__(self, shards, my_chip, my_core):
        self.shards, self.my_chip, self.my_core = shards, my_chip.reshape(1), my_core.reshape(1)

    def gather(self, names):
        return _all_gather([self.shards[n] for n in names])

    @staticmethod
    def whole(gathered):
        return gathered.reshape(N_DEV * gathered.shape[1], gathered.shape[2])

    @staticmethod
    def by_device(grads):
        return [g.reshape(N_CHIP, 2, g.shape[0] // N_DEV, g.shape[1]) for g in grads]

    @staticmethod
    def to_sibling(parts):
        return _scatter_exchange(parts, CORE_RELS, lambda ref, peer: ref.at[:, peer[2]],
                                 [(N_CHIP,) + p.shape[2:] for p in parts])

    @staticmethod
    def to_chips(pair):
        return _scatter_exchange(pair, CHIP_RELS, lambda ref, peer: ref.at[_chip_slot(peer)], [p.shape[1:] for p in pair])


def _step(x, target, g1, w_in_t, wa, wb, bb, lg, lb, w_out, g2, w_gate_t, w_up_t, wf, w_down, g3, ts, mesh=None):
    (z, h1), got = _fwd_in(x, g1, w_in_t, ts, exchange=mesh and mesh.gather(["w_out", "w_gate"]))
    if mesh:
        w_out, w_gate_t = [mesh.whole(g) for g in got]
    (x2, y, u), got = _fwd_mix(z, x, wa, wb, bb, lg, lb, w_out, ts, exchange=mesh and mesh.gather(["w_up", "w_down"]))
    if mesh:
        w_up_t, w_down = [mesh.whole(g) for g in got]
    g0, v, h2 = _fwd_ffn_in(x2, g2, w_gate_t, w_up_t, ts)
    tk = min(ts, SKEW_TILE)
    a, dx3, dx3b, loss, dg3, dgc, dv, dwf = _ffn_out_and_back(g0, v, x2, wf, w_down, g3, target, tk)
    dg0, dx2, dx2b, dg2 = _bwd_ffn_b(dgc, dv, wf, w_gate_t, w_up_t, x2, g2, dx3, tk)
    first = dict(w_down=_matmul_tn(a, dx3b, "wgrad_down"), w_gate=_matmul_tn(dg0, h2, "wgrad_gate"),
                 w_up=_matmul_tn(dv, h2, "wgrad_up"), w_out=_matmul_tn(y, dx2b, "wgrad_out"))
    if mesh:
        parts = mesh.by_device(list(first.values()))
    (dca, du, dab, dwa, dlg, dlb, dbb), got = _bwd_mix_a(dx2b, jnp.swapaxes(w_out, 0, 1), z, u, wa, lg, lb, tk,
                                                         exchange=mesh and mesh.to_sibling(parts))
    if mesh:
        pair = _pair_sum(parts, got, mesh.my_core, "rs_pair_sum_first")
    (dz, dx, dg1, dwb), got = _bwd_mix_b(dca, du, z, dab, wa, wb, w_in_t, x, g1, dx2, tk,
                                         exchange=mesh and mesh.to_chips(pair))
    dw_in_t = _matmul_tn(dz, h1, "wgrad_in")
    small = dict(norm_mix_g=dg1, conv_a_w=dwa, conv_b_w=dwb, conv_b_b=dbb, ln_b_g=dlg, ln_b_b=dlb,
                 norm_ffn_g=dg2, conv_ffn_w=dwf, norm_final_g=dg3)
    if not mesh:
        return loss, dx, dict(w_in=dw_in_t, **first), small
    big = dict(zip(first, _chip_sum(pair, got, mesh.my_chip, "rs_chip_sum_first")))
    parts = mesh.by_device([dw_in_t])
    (got,) = _run_exchanges("rs_cores_last", [mesh.to_sibling(parts)])
    pair = _pair_sum(parts, got, mesh.my_core, "rs_pair_sum_last")
    got, (every,) = _run_exchanges("rs_chips_last", [mesh.to_chips(pair), _gather_direct(_pack_small_grads(small, loss))])
    (big["w_in"],) = _chip_sum(pair, got, mesh.my_chip, "rs_chip_sum_last")
    shapes = [loss.shape if n == "loss" else small[n].shape for n in _SMALL_NAMES]
    return None, dx, big, _unpack_small_grads(_sum_devices(every), shapes)


def _pack_small_weights(conv_a_s, conv_b_s, conv_ffn_s):
    buf = jnp.zeros((SMALL_W_ROWS, SMALL_W_COLS), F32)
    buf = buf.at[0:K_A, 0:CONV_A_COLS].set(conv_a_s)
    buf = buf.at[K_A:K_A + K_B, 0:CONV_A_COLS].set(conv_b_s)
    return buf.at[K_A + K_B:K_A + K_B + K_F, 0:W_FF_COLS].set(conv_ffn_s)


def _unpack_small_weights(full):
    def take(r0, k, w):
        return jnp.transpose(full[:, r0:r0 + k, 0:w], (1, 0, 2)).reshape(k, N_DEV * w)

    return take(0, K_A, CONV_A_COLS), take(K_A, K_B, CONV_A_COLS), take(K_A + K_B, K_F, W_FF_COLS)


_SMALL_NAMES = ("conv_b_w", "conv_a_w", "conv_ffn_w", "norm_mix_g", "norm_ffn_g", "norm_final_g",
                "conv_b_b", "ln_b_g", "ln_b_b", "loss")
SMALL_G_ROWS = 64


def _small_rows(shapes):
    plan, at = [], 0
    for idx, (rows, cols) in enumerate(shapes):
        for r in range(rows):
            for c0 in range(0, cols, SMALL_G_COLS):
                plan.append((idx, r, c0, min(SMALL_G_COLS, cols - c0), at))
                at += 1
    assert at <= SMALL_G_ROWS
    return plan


def _pack_small_grads(small, loss):
    srcs = [loss if n == "loss" else small[n] for n in _SMALL_NAMES]
    plan = _small_rows([a.shape for a in srcs])

    def body(*refs):
        out = refs[-1]
        out[...] = jnp.zeros_like(out)
        for idx, r, c0, width, at in plan:
            out[at:at + 1, 0:width] = refs[idx][r:r + 1, c0:c0 + width]

    return pl.pallas_call(body, name="small_grad_pack", out_shape=_sds((SMALL_G_ROWS, SMALL_G_COLS), F32))(*srcs)


def _unpack_small_grads(tot, shapes):
    out, at = {}, 0
    for name, (rows, cols) in zip(_SMALL_NAMES, shapes):
        per_row = -(-cols // SMALL_G_COLS)
        blk = tot[at:at + rows * per_row]
        at += rows * per_row
        out[name] = blk.reshape(rows, per_row * SMALL_G_COLS)[:, 0:cols]
    out["loss"] = out["loss"][0, 0]
    return out


def kernel(x, norm_mix_g, w_in, conv_a_w, conv_b_w, conv_b_b, ln_b_g, ln_b_b, w_out, norm_ffn_g, w_gate, w_up, conv_ffn_w, w_down, norm_final_g, loss_target, m_norm_mix_g, m_w_in, m_conv_a_w, m_conv_b_w, m_conv_b_b, m_ln_b_g, m_ln_b_b, m_w_out, m_norm_ffn_g, m_w_gate, m_w_up, m_conv_ffn_w, m_w_down, m_norm_final_g, v_norm_mix_g, v_w_in, v_conv_a_w, v_conv_b_w, v_conv_b_b, v_ln_b_g, v_ln_b_b, v_w_out, v_norm_ffn_g, v_w_gate, v_w_up, v_conv_ffn_w, v_w_down, v_norm_final_g):
    ix, iy, ic = lax.axis_index("x"), lax.axis_index("y"), lax.axis_index("c")
    my_chip = (2 * ix + iy).astype(jnp.int32)
    my_core = ic.astype(jnp.int32)
    my_dev = 2 * my_chip + my_core

    weights = dict(norm_mix_g=norm_mix_g, w_in=w_in, conv_a_w=conv_a_w, conv_b_w=conv_b_w, conv_b_b=conv_b_b,
                   ln_b_g=ln_b_g, ln_b_b=ln_b_b, w_out=w_out, norm_ffn_g=norm_ffn_g, w_gate=w_gate, w_up=w_up,
                   conv_ffn_w=conv_ffn_w, w_down=w_down, norm_final_g=norm_final_g)
    m_in = dict(norm_mix_g=m_norm_mix_g, w_in=m_w_in, conv_a_w=m_conv_a_w, conv_b_w=m_conv_b_w, conv_b_b=m_conv_b_b,
                ln_b_g=m_ln_b_g, ln_b_b=m_ln_b_b, w_out=m_w_out, norm_ffn_g=m_norm_ffn_g, w_gate=m_w_gate,
                w_up=m_w_up, conv_ffn_w=m_conv_ffn_w, w_down=m_w_down, norm_final_g=m_norm_final_g)
    v_in = dict(norm_mix_g=v_norm_mix_g, w_in=v_w_in, conv_a_w=v_conv_a_w, conv_b_w=v_conv_b_w, conv_b_b=v_conv_b_b,
                ln_b_g=v_ln_b_g, ln_b_b=v_ln_b_b, w_out=v_w_out, norm_ffn_g=v_norm_ffn_g, w_gate=v_w_gate,
                w_up=v_w_up, conv_ffn_w=v_conv_ffn_w, w_down=v_w_down, norm_final_g=v_norm_final_g)
    order = list(weights)
    big_names = ("w_in", "w_gate", "w_up", "w_out", "w_down")
    transposed = ("w_in", "w_gate", "w_up")

    def shard2d(name, a):
        if name in transposed:
            return jnp.swapaxes(a[0], 0, 1)
        return a.reshape(1, a.shape[0]) if a.ndim == 1 else a.reshape(a.shape[-2:])

    def unshard2d(name, a2, like):
        if name in transposed:
            return jnp.swapaxes(a2, 0, 1)[None]
        return a2.reshape(like.shape)

    mesh = _Mesh({n: shard2d(n, weights[n]).astype(BF16) for n in big_names}, my_chip, my_core)
    gathered, = _run_exchanges("ag_first", [_all_gather(
        [mesh.shards["w_in"], _pack_small_weights(conv_a_w[0], conv_b_w[0], conv_ffn_w[0])])])
    w_in_t = mesh.whole(gathered[0])
    wa_f, wb_f, wf_f = _unpack_small_weights(gathered[1])

    _, dx, gsum, stot = _step(
        x[0], loss_target[0], norm_mix_g, w_in_t, wa_f, wb_f, conv_b_b, ln_b_g, ln_b_b, None, norm_ffn_g,
        None, None, wf_f, None, norm_final_g.reshape(1, D_MODEL), SEQ_TILE, mesh)

    grads2d = dict(
        norm_mix_g=stot["norm_mix_g"],
        conv_a_w=lax.dynamic_slice(stot["conv_a_w"], (0, my_dev * CONV_A_COLS), (K_A, CONV_A_COLS)),
        conv_b_w=lax.dynamic_slice(stot["conv_b_w"], (0, my_dev * CONV_A_COLS), (K_B, CONV_A_COLS)),
        conv_b_b=stot["conv_b_b"], ln_b_g=stot["ln_b_g"], ln_b_b=stot["ln_b_b"],
        norm_ffn_g=stot["norm_ffn_g"],
        conv_ffn_w=lax.dynamic_slice(stot["conv_ffn_w"], (0, my_dev * W_FF_COLS), (K_F, W_FF_COLS)),
        norm_final_g=stot["norm_final_g"],
        **gsum,
    )

    updates = {}
    small_names = [n for n in order if n not in big_names]
    for group, label in [([n], "adamw_" + n) for n in big_names] + [(small_names, "adamw_small")]:
        outs = _adamw([shard2d(n, weights[n]) for n in group], [grads2d[n] for n in group],
                      [shard2d(n, m_in[n]) for n in group], [shard2d(n, v_in[n]) for n in group], label)
        for k, n in enumerate(group):
            updates[n] = outs[3 * k:3 * k + 3]
    g_out = [unshard2d(n, grads2d[n], weights[n]) for n in order]
    d_out, m_out, v_out = [[unshard2d(n, updates[n][j], weights[n]) for n in order] for j in range(3)]

    return (stot["loss"], dx[None], *g_out, *d_out, *m_out, *v_out)
```

```python
import jax
import jax.numpy as jnp
from jax import lax
from jax.experimental import pallas as pl
from jax.experimental.pallas import tpu as pltpu

F32 = jnp.float32
BF16 = jnp.bfloat16

D_MODEL = 1024
D_A = 512
D_B = 512
D_IN = 3 * D_A + 2 * D_B
D_FF = 2816
K_A = 3
K_B = 31
K_F = 3
RMS_EPS = 1e-6
LN_EPS = 1e-5

ADAM_LR = 0.001
ADAM_B1 = 0.9
ADAM_B2 = 0.999
ADAM_EPS = 1e-08
ADAM_WD = 0.01
ADAM_STEP = 10

N_DEV = 8
N_CHIP = 4
LANES = 128
SUBLANES = 8
HALO = 16
ROW_CHUNK = 64
SEQ_TILE = 512
SKEW_TILE = 256
VMEM_LIMIT = 56 * 1024 * 1024

MESH = pl.DeviceIdType.MESH

W_FF_COLS = D_FF // N_DEV
CONV_A_COLS = D_A // N_DEV
SMALL_W_ROWS = 40
SMALL_W_COLS = 384
SMALL_G_COLS = 512
FF_PAD = 3072


def _rows(ts, c):
    return pl.BlockSpec((ts, c), lambda i: (i, 0))


def _const(shape):
    return pl.BlockSpec(shape, lambda i: (0,) * len(shape), pipeline_mode=pl.Buffered(1))


def _acc_out(shape):
    return pl.BlockSpec(shape, lambda i: (0,) * len(shape))


def _rows_at(ts, c, tile):
    return pl.BlockSpec((ts, c), lambda i: (tile(i), 0))


def _prev_at(ts, c, tile):
    return pl.BlockSpec((HALO, c), lambda i: (jnp.maximum(tile(i) * (ts // HALO) - 1, 0), 0))


def _next_at(ts, c, s, tile):
    last = s // HALO - 1
    return pl.BlockSpec((HALO, c), lambda i: (jnp.minimum((tile(i) + 1) * (ts // HALO), last), 0))


def _prev(ts, c):
    return _prev_at(ts, c, lambda i: i)


def _next(ts, c, s):
    return _next_at(ts, c, s, lambda i: i)


MXU_COLS = 256
MXU_ROWS = 256


def _col_pieces(n):
    return [(c0, min(MXU_COLS, n - c0)) for c0 in range(0, n, MXU_COLS)]


def _matmul_pieces(terms, out_ref, k_parts):
    m, n = out_ref.shape
    steps = []
    for lhs_ref, w_ref in terms:
        tiles = lhs_ref.shape[1] // MXU_COLS
        cuts = [MXU_COLS * (tiles * j // k_parts) for j in range(k_parts)] + [lhs_ref.shape[1]]
        steps += [(lhs_ref, w_ref, cuts[j], cuts[j + 1]) for j in range(k_parts)]

    def piece(m0, n0, width, step):
        lhs_ref, w_ref, k0, k1 = steps[step]
        part = jnp.dot(lhs_ref[m0:m0 + MXU_ROWS, k0:k1], w_ref[k0:k1, n0:n0 + width], preferred_element_type=F32)
        if step:
            part = part + out_ref[m0:m0 + MXU_ROWS, n0:n0 + width]
        out_ref[m0:m0 + MXU_ROWS, n0:n0 + width] = part

    return [(piece, (m0, n0, w, j)) for j in range(len(steps)) for n0, w in _col_pieces(n) for m0 in range(0, m, MXU_ROWS)]


def _interleaved(vector_units, matmul_pieces):
    n_u, n_p = len(vector_units), len(matmul_pieces)
    done = 0
    for k, (unit, args) in enumerate(vector_units):
        while done < n_p and done * n_u <= k * n_p:
            matmul_pieces[done][0](*matmul_pieces[done][1])
            done += 1
        unit(*args)
    for fn, args in matmul_pieces[done:]:
        fn(*args)


def _params():
    return pltpu.CompilerParams(dimension_semantics=("arbitrary",), vmem_limit_bytes=VMEM_LIMIT)


def _sds(shape, dtype):
    return jax.ShapeDtypeStruct(shape, dtype)


def _sigmoid(v):
    return 0.5 * jnp.tanh(0.5 * v) + 0.5


def _conv_block(ext_ref, w_ref, r0, rc, l0, k_taps, transposed):
    acc = None
    for k in range(k_taps):
        d = (k_taps // 2 - k) if transposed else (k - k_taps // 2)
        term = ext_ref[l0 // LANES, pl.ds(r0 + HALO + d, rc), :] * w_ref[k:k + 1, l0:l0 + LANES]
        acc = term if acc is None else acc + term
    return acc


def _conv_wgrad_block(acc_ref, dout, ext_ref, r0, rc, l0, k_taps, scale=None):
    for k in range(k_taps):
        prod = dout * ext_ref[l0 // LANES, pl.ds(r0 + HALO + k - k_taps // 2, rc), :]
        part = prod.reshape(rc // SUBLANES, SUBLANES, LANES).sum(axis=0)
        if scale is not None:
            part = part * scale
        acc_ref[k, :, l0:l0 + LANES] = acc_ref[k, :, l0:l0 + LANES] + part


def _reduce_acc(out_ref, acc_ref, k_taps):
    for k in range(k_taps):
        out_ref[k:k + 1, :] = jnp.sum(acc_ref[k], axis=0, keepdims=True)


def _fold8(v):
    rc, c = v.shape
    return v.reshape(rc // SUBLANES, SUBLANES, c).sum(axis=0)


def _ext_scratch(ts, c):
    return pltpu.VMEM((c // LANES, ts + 2 * HALO, LANES), F32)


def _put_rows(ext_ref, r0, rc, val):
    for q in range(val.shape[1] // LANES):
        ext_ref[q, pl.ds(r0 + HALO, rc), :] = val[:, q * LANES:(q + 1) * LANES]


def _fill_halo(ext_ref, vals_prev, vals_next, ts, first, last):
    for q in range(vals_prev.shape[1] // LANES):
        cols = slice(q * LANES, (q + 1) * LANES)
        ext_ref[q, 0:HALO, :] = jnp.where(first, 0.0, vals_prev[:, cols])
        ext_ref[q, HALO + ts:HALO + ts + HALO, :] = jnp.where(last, 0.0, vals_next[:, cols])


def _rms_bwd_rows(dh, xf, g):
    r = lax.rsqrt(jnp.mean(xf * xf, axis=-1, keepdims=True) + RMS_EPS)
    xhat = xf * r
    dxh = dh * g
    dx = r * (dxh - xhat * jnp.mean(dxh * xhat, axis=-1, keepdims=True))
    return dx, dh * xhat


_NT = (((1,), (1,)), ((), ()))
_TN = (((0,), (0,)), ((), ()))


def _fwd_in(x, g1, w_in_t, ts, exchange=None):
    s = x.shape[0]

    def body(x_ref, g_ref, w_ref, z_ref, h_ref):
        xf = x_ref[...]
        r = lax.rsqrt(jnp.mean(xf * xf, axis=-1, keepdims=True) + RMS_EPS)
        h = (xf * r * g_ref[...]).astype(BF16)
        h_ref[...] = h
        for n0 in range(0, D_IN, 512):
            z_ref[:, n0:n0 + 512] = lax.dot_general(h, w_ref[n0:n0 + 512, :], _NT,
                                                    preferred_element_type=F32).astype(BF16)

    return _call(
        body, name="fwd_in", grid=(s // ts,),
        in_specs=[_rows(ts, D_MODEL), _const((1, D_MODEL)), _const((D_IN, D_MODEL))],
        out_specs=[_rows(ts, D_IN), _rows(ts, D_MODEL)],
        out_shape=[_sds((s, D_IN), BF16), _sds((s, D_MODEL), BF16)],
        scratch_shapes=[], args=(x, g1, w_in_t), exchange=exchange, forward_step=(s // ts) * 3 // 4)


def _p_u0(z_ref, rows):
    a_h = z_ref[rows, 0:D_A].astype(F32)
    a_c = z_ref[rows, 2 * D_A:3 * D_A].astype(F32)
    b_v = z_ref[rows, 3 * D_A:3 * D_A + D_B].astype(F32)
    b_g = z_ref[rows, 3 * D_A + D_B:D_IN].astype(F32)
    return a_c * a_h, b_v * _sigmoid(b_g)


def _layernorm_rows(u_blocks):
    tot = None
    for ub in u_blocks:
        sm = jnp.sum(ub, axis=-1, keepdims=True)
        tot = sm if tot is None else tot + sm
    mu = tot * (1.0 / D_B)
    var = None
    for ub in u_blocks:
        sq = jnp.sum((ub - mu) * (ub - mu), axis=-1, keepdims=True)
        var = sq if var is None else var + sq
    rstd = lax.rsqrt(var * (1.0 / D_B) + LN_EPS)
    return mu, rstd


def _fwd_mix(z, x, wa, wb, bb, lg, lb, w_out, ts, exchange=None):
    s = x.shape[0]
    nt = s // ts
    rc = min(ROW_CHUNK, ts)

    def body(z_ref, zp_ref, zn_ref, x_ref, wa_ref, wb_ref, bb_ref, lg_ref, lb_ref, wo_ref,
             x2_ref, y_ref, u_ref, pe_ref, ue_ref):
        i = pl.program_id(0)
        pp, up = _p_u0(zp_ref, slice(None))
        pn, un = _p_u0(zn_ref, slice(None))
        _fill_halo(pe_ref, pp, pn, ts, i == 0, i == nt - 1)
        _fill_halo(ue_ref, up, un, ts, i == 0, i == nt - 1)

        def fill(j, carry):
            r0 = pl.multiple_of(j * rc, rc)
            p, u0 = _p_u0(z_ref, pl.ds(r0, rc))
            _put_rows(pe_ref, r0, rc, p)
            _put_rows(ue_ref, r0, rc, u0)
            return carry

        lax.fori_loop(0, ts // rc, fill, 0)

        def main(j, carry):
            r0 = pl.multiple_of(j * rc, rc)
            rows = pl.ds(r0, rc)
            for l0 in range(0, D_A, LANES):
                ca = _conv_block(pe_ref, wa_ref, r0, rc, l0, K_A, False)
                a_b = z_ref[rows, D_A + l0:D_A + l0 + LANES].astype(F32)
                y_ref[rows, l0:l0 + LANES] = (a_b * ca).astype(BF16)
            ubs = []
            for l0 in range(0, D_B, LANES):
                ub = _conv_block(ue_ref, wb_ref, r0, rc, l0, K_B, False) + bb_ref[:, l0:l0 + LANES]
                u_ref[rows, l0:l0 + LANES] = ub
                ubs.append(ub)
            mu, rstd = _layernorm_rows(ubs)
            for q, l0 in enumerate(range(0, D_B, LANES)):
                t = (ubs[q] - mu) * rstd * lg_ref[:, l0:l0 + LANES] + lb_ref[:, l0:l0 + LANES]
                y_ref[rows, D_A + l0:D_A + l0 + LANES] = (t * _sigmoid(t)).astype(BF16)
            return carry

        lax.fori_loop(0, ts // rc, main, 0)
        for n0 in range(0, D_MODEL, 512):
            x2_ref[:, n0:n0 + 512] = x_ref[:, n0:n0 + 512] + jnp.dot(
                y_ref[...], wo_ref[:, n0:n0 + 512], preferred_element_type=F32)

    return _call(
        body, name="fwd_mix", grid=(nt,),
        in_specs=[_rows(ts, D_IN), _prev(ts, D_IN), _next(ts, D_IN, s), _rows(ts, D_MODEL),
                  _const((K_A, D_A)), _const((K_B, D_B)), _const((1, D_B)), _const((1, D_B)), _const((1, D_B)),
                  _const((D_MODEL, D_MODEL))],
        out_specs=[_rows(ts, D_MODEL), _rows(ts, D_MODEL), _rows(ts, D_B)],
        out_shape=[_sds((s, D_MODEL), F32), _sds((s, D_MODEL), BF16), _sds((s, D_B), F32)],
        scratch_shapes=[_ext_scratch(ts, D_A), _ext_scratch(ts, D_B)],
        args=(z, z, z, x, wa, wb, bb, lg, lb, w_out), exchange=exchange, forward_step=nt * 5 // 8)


def _fwd_ffn_in(x2, g2, w_gate_t, w_up_t, ts, exchange=None):
    s = x2.shape[0]
    half = D_FF // 2

    def body(x_ref, g_ref, wg_ref, wu_ref, g0_ref, v_ref, h_ref):
        xf = x_ref[...]
        r = lax.rsqrt(jnp.mean(xf * xf, axis=-1, keepdims=True) + RMS_EPS)
        h = (xf * r * g_ref[...]).astype(BF16)
        h_ref[...] = h
        for n0 in range(0, D_FF, half):
            g0_ref[:, n0:n0 + half] = lax.dot_general(h, wg_ref[n0:n0 + half, :], _NT,
                                                      preferred_element_type=F32).astype(BF16)
            v_ref[:, n0:n0 + half] = lax.dot_general(h, wu_ref[n0:n0 + half, :], _NT,
                                                     preferred_element_type=F32).astype(BF16)

    return _call(
        body, name="fwd_ffn_in", grid=(s // ts,),
        in_specs=[_rows(ts, D_MODEL), _const((1, D_MODEL)), _const((D_FF, D_MODEL)), _const((D_FF, D_MODEL))],
        out_specs=[_rows(ts, D_FF), _rows(ts, D_FF), _rows(ts, D_MODEL)],
        out_shape=[_sds((s, D_FF), BF16), _sds((s, D_FF), BF16), _sds((s, D_MODEL), BF16)],
        scratch_shapes=[], args=(x2, g2, w_gate_t, w_up_t), exchange=exchange, forward_step=(s // ts) // 2)


def _ffn_out_and_back(g0, v, x2, wf, w_down, g3, target, ts):
    s = x2.shape[0]
    nt = s // ts
    rc = min(ROW_CHUNK, ts)
    half = D_FF // 2

    def body(g0_ref, gp_ref, gn_ref, v_ref, x2_ref, wf_ref, wd_ref, g3_ref, t_ref,
             a_ref, dx3_ref, dx3b_ref, loss_ref, dg3_ref, dg_ref, dv_ref, dwf_ref,
             ge_ref, silu_ref, dsv_ref, da_ref, acc_ref):
        i = pl.program_id(0)

        @pl.when(i == 0)
        def _():
            acc_ref[...] = jnp.zeros_like(acc_ref)

        _fill_halo(ge_ref, gp_ref[...].astype(F32), gn_ref[...].astype(F32), ts, i == 0, i == nt - 1)

        def fill(j, carry):
            r0 = pl.multiple_of(j * rc, rc)
            _put_rows(ge_ref, r0, rc, g0_ref[pl.ds(r0, rc), :].astype(F32))
            return carry

        lax.fori_loop(0, ts // rc, fill, 0)

        def act(j, carry):
            r0 = pl.multiple_of(j * rc, rc)
            rows = pl.ds(r0, rc)
            for l0 in range(0, D_FF, LANES):
                g = _conv_block(ge_ref, wf_ref, r0, rc, l0, K_F, False)
                vv = v_ref[rows, l0:l0 + LANES].astype(F32)
                sg = _sigmoid(g)
                silu = g * sg
                a_ref[rows, l0:l0 + LANES] = (silu * vv).astype(BF16)
                silu_ref[rows, l0:l0 + LANES] = silu
                dsv_ref[rows, l0:l0 + LANES] = (sg + silu * (1.0 - sg)) * vv
            return carry

        lax.fori_loop(0, ts // rc, act, 0)
        for n0 in range(0, D_MODEL, 512):
            dx3_ref[:, n0:n0 + 512] = x2_ref[:, n0:n0 + 512] + jnp.dot(
                a_ref[...], wd_ref[:, n0:n0 + 512], preferred_element_type=F32)

        @pl.when(i == 0)
        def _():
            loss_ref[...] = jnp.zeros_like(loss_ref)
            dg3_ref[...] = jnp.zeros_like(dg3_ref)

        def tail(j, carry):
            lsum, dgsum = carry
            r0 = j * rc
            rows = slice(r0, r0 + rc)
            x3 = dx3_ref[rows, :]
            r = lax.rsqrt(jnp.mean(x3 * x3, axis=-1, keepdims=True) + RMS_EPS)
            xhat = x3 * r
            diff = xhat * g3_ref[...] - t_ref[rows, :]
            dout = diff * (1.0 / D_MODEL)
            dxh = dout * g3_ref[...]
            dx3 = r * (dxh - xhat * jnp.mean(dxh * xhat, axis=-1, keepdims=True))
            dx3_ref[rows, :] = dx3
            dx3b_ref[rows, :] = dx3.astype(BF16)
            lsum = lsum + _fold8(diff * diff)
            dgsum = dgsum + _fold8(dout * xhat)
            return lsum, dgsum

        sums = (jnp.zeros((SUBLANES, D_MODEL), F32),) * 2
        for j in range(ts // rc):
            sums = tail(j, sums)
        lsum, dgsum = sums
        loss_ref[...] = loss_ref[...] + (0.5 / D_MODEL) * jnp.sum(lsum, keepdims=True)
        dg3_ref[...] = dg3_ref[...] + jnp.sum(dgsum, axis=0, keepdims=True)

        for n0 in range(0, D_FF, half):
            da_ref[:, n0:n0 + half] = lax.dot_general(dx3b_ref[...], wd_ref[n0:n0 + half, :], _NT,
                                                      preferred_element_type=F32)

        def back(j, carry):
            r0 = pl.multiple_of(j * rc, rc)
            rows = pl.ds(r0, rc)
            for l0 in range(0, D_FF, LANES):
                da = da_ref[rows, l0:l0 + LANES]
                dv_ref[rows, l0:l0 + LANES] = (da * silu_ref[rows, l0:l0 + LANES]).astype(BF16)
                dgg = da * dsv_ref[rows, l0:l0 + LANES]
                dg_ref[rows, l0:l0 + LANES] = dgg.astype(BF16)
                _conv_wgrad_block(acc_ref, dgg, ge_ref, r0, rc, l0, K_F)
            return carry

        lax.fori_loop(0, ts // rc, back, 0)

        @pl.when(i == nt - 1)
        def _():
            _reduce_acc(dwf_ref, acc_ref, K_F)

    return pl.pallas_call(
        body, name="ffn_out_and_back", grid=(nt,),
        in_specs=[_rows(ts, D_FF), _prev(ts, D_FF), _next(ts, D_FF, s), _rows(ts, D_FF), _rows(ts, D_MODEL),
                  _const((K_F, D_FF)), _const((D_FF, D_MODEL)), _const((1, D_MODEL)), _rows(ts, D_MODEL)],
        out_specs=[_rows(ts, D_FF), _rows(ts, D_MODEL), _rows(ts, D_MODEL), _acc_out((1, 1)), _acc_out((1, D_MODEL)),
                   _rows(ts, D_FF), _rows(ts, D_FF), _acc_out((K_F, D_FF))],
        out_shape=[_sds((s, D_FF), BF16), _sds((s, D_MODEL), F32), _sds((s, D_MODEL), BF16),
                   _sds((1, 1), F32), _sds((1, D_MODEL), F32),
                   _sds((s, D_FF), BF16), _sds((s, D_FF), BF16), _sds((K_F, D_FF), F32)],
        scratch_shapes=[_ext_scratch(ts, D_FF), pltpu.VMEM((ts, D_FF), F32), pltpu.VMEM((ts, D_FF), F32),
                        pltpu.VMEM((ts, D_FF), F32), pltpu.VMEM((K_F, SUBLANES, D_FF), F32)],
        compiler_params=_params(),
    )(g0, g0, g0, v, x2, wf, w_down, g3, target)


def _bwd_ffn_b(dg, dv, wf, w_gate, w_up, x2, g2, dx3, ts):
    s = x2.shape[0]
    nt = s // ts
    rc = min(ROW_CHUNK, ts)
    n_sub = ts // rc

    def body(dg_ref, dgp_ref, dgn_ref, dv_ref, wf_ref, wg_ref, wu_ref, x2_ref, g2_ref, dx3_ref,
             dg0_ref, dx2_ref, dx2b_ref, dgn2_ref, dge_ref, dg8_ref, a0_ref, a1_ref, p0_ref, p1_ref):
        i = pl.program_id(0)
        vt = jnp.minimum(i, nt - 1)
        live = (i >= 2).astype(F32)

        @pl.when(i == 0)
        def _():
            dg8_ref[...] = jnp.zeros_like(dg8_ref)
            a1_ref[...] = jnp.zeros_like(a1_ref)
            p1_ref[...] = jnp.zeros_like(p1_ref)

        _fill_halo(dge_ref, dgp_ref[...].astype(F32), dgn_ref[...].astype(F32), ts, vt == 0, vt == nt - 1)

        def fill(j, carry):
            r0 = pl.multiple_of(j * rc, rc)
            _put_rows(dge_ref, r0, rc, dg_ref[pl.ds(r0, rc), :].astype(F32))
            return carry

        lax.fori_loop(0, n_sub, fill, 0)

        def stage(a_new, a_old, p_new, p_old):
            def conv_t(r0, l0):
                rows = slice(r0, r0 + rc)
                dg0 = _conv_block(dge_ref, wf_ref, r0, rc, l0, K_F, True).astype(BF16)
                dg0_ref[rows, l0:l0 + LANES] = dg0
                a_new[rows, l0:l0 + LANES] = dg0

            def tail(r0):
                rows = slice(r0, r0 + rc)
                dx, dgrow = _rms_bwd_rows(p_old[rows, :], x2_ref[rows, :], g2_ref[...])
                dx2 = dx3_ref[rows, :] + dx
                dx2_ref[rows, :] = dx2
                dx2b_ref[rows, :] = dx2.astype(BF16)
                dg8_ref[...] = dg8_ref[...] + _fold8(dgrow) * live

            units = []
            for q in range(n_sub):
                units += [(conv_t, (q * rc, l0)) for l0 in range(0, D_FF, LANES)]
                units.append((tail, (q * rc,)))
            _interleaved(units, _matmul_pieces([(a_old, wg_ref), (dv_ref, wu_ref)], p_new, 2))

        @pl.when(i % 2 == 0)
        def _():
            stage(a0_ref, a1_ref, p0_ref, p1_ref)

        @pl.when(i % 2 == 1)
        def _():
            stage(a1_ref, a0_ref, p1_ref, p0_ref)

        @pl.when(i == nt + 1)
        def _():
            dgn2_ref[...] = jnp.sum(dg8_ref[...], axis=0, keepdims=True)

    vtile = lambda i: jnp.minimum(i, nt - 1)
    mtile = lambda i: jnp.clip(i - 1, 0, nt - 1)
    ttile = lambda i: jnp.clip(i - 2, 0, nt - 1)
    return pl.pallas_call(
        body, name="bwd_ffn_b", grid=(nt + 2,),
        in_specs=[_rows_at(ts, D_FF, vtile), _prev_at(ts, D_FF, vtile), _next_at(ts, D_FF, s, vtile),
                  _rows_at(ts, D_FF, mtile), _const((K_F, D_FF)),
                  _const((D_FF, D_MODEL)), _const((D_FF, D_MODEL)), _rows_at(ts, D_MODEL, ttile), _const((1, D_MODEL)),
                  _rows_at(ts, D_MODEL, ttile)],
        out_specs=[_rows_at(ts, D_FF, vtile), _rows_at(ts, D_MODEL, ttile), _rows_at(ts, D_MODEL, ttile),
                   _acc_out((1, D_MODEL))],
        out_shape=[_sds((s, D_FF), BF16), _sds((s, D_MODEL), F32), _sds((s, D_MODEL), BF16), _sds((1, D_MODEL), F32)],
        scratch_shapes=[_ext_scratch(ts, D_FF), pltpu.VMEM((SUBLANES, D_MODEL), F32),
                        pltpu.VMEM((ts, D_FF), BF16), pltpu.VMEM((ts, D_FF), BF16),
                        pltpu.VMEM((ts, D_MODEL), F32), pltpu.VMEM((ts, D_MODEL), F32)],
        compiler_params=_params(),
    )(dg, dg, dg, dv, wf, w_gate, w_up, x2, g2, dx3)


def _bwd_mix_a(dx2b, w_out_t, z, u, wa, lg, lb, ts, exchange=None):
    s = dx2b.shape[0]
    nt = s // ts
    rc = min(ROW_CHUNK, ts)
    n_sub = ts // rc

    def body(dx_ref, wo_ref, z_ref, zp_ref, zn_ref, u_ref, wa_ref, lg_ref, lb_ref,
             dca_ref, du_ref, dab_ref, dwa_ref, dlg_ref, dlb_ref, dbb_ref, pe_ref, dy0_ref, dy1_ref, acc_ref, sacc_ref):
        i = pl.program_id(0)
        t = jnp.maximum(i - 1, 0)

        @pl.when(i == 0)
        def _():
            acc_ref[...] = jnp.zeros_like(acc_ref)
            sacc_ref[...] = jnp.zeros_like(sacc_ref)
            dy1_ref[...] = jnp.zeros_like(dy1_ref)

        pp, _ = _p_u0(zp_ref, slice(None))
        pn, _ = _p_u0(zn_ref, slice(None))
        _fill_halo(pe_ref, pp, pn, ts, t == 0, t == nt - 1)

        def fill(j, carry):
            r0 = pl.multiple_of(j * rc, rc)
            rows = pl.ds(r0, rc)
            _put_rows(pe_ref, r0, rc, z_ref[rows, 2 * D_A:3 * D_A].astype(F32) * z_ref[rows, 0:D_A].astype(F32))
            return carry

        lax.fori_loop(0, n_sub, fill, 0)

        def stage(dy_new, dy_old):
            def piece(m0, n0, width):
                dy_new[m0:m0 + MXU_ROWS, n0:n0 + width] = jnp.dot(
                    dx_ref[m0:m0 + MXU_ROWS, :], wo_ref[:, n0:n0 + width], preferred_element_type=F32).astype(BF16)

            units = []
            for q in range(n_sub):
                units += [(mixer_a, (dy_old, q * rc, l0)) for l0 in range(0, D_A, LANES)]
                units.append((mixer_b, (dy_old, q * rc)))
            _interleaved(units, [(piece, (m0, n0, w)) for n0, w in _col_pieces(D_MODEL) for m0 in range(0, ts, MXU_ROWS)])

        def mixer_a(dy_ref, r0, l0):
            rows = slice(r0, r0 + rc)
            ca = _conv_block(pe_ref, wa_ref, r0, rc, l0, K_A, False)
            a_b = z_ref[rows, D_A + l0:D_A + l0 + LANES].astype(F32)
            dya = dy_ref[rows, l0:l0 + LANES].astype(F32)
            dab_ref[rows, l0:l0 + LANES] = (dya * ca).astype(BF16)
            dca = dya * a_b
            dca_ref[rows, l0:l0 + LANES] = dca
            _conv_wgrad_block(acc_ref, dca, pe_ref, r0, rc, l0, K_A)

        def mixer_b(dy_ref, r0):
            rows = slice(r0, r0 + rc)
            ubs = [u_ref[rows, l0:l0 + LANES] for l0 in range(0, D_B, LANES)]
            mu, rstd = _layernorm_rows(ubs)
            ns, dns = [], []
            m1 = None
            m2 = None
            for q, l0 in enumerate(range(0, D_B, LANES)):
                n = (ubs[q] - mu) * rstd
                lgq = lg_ref[:, l0:l0 + LANES]
                t = n * lgq + lb_ref[:, l0:l0 + LANES]
                sg = _sigmoid(t)
                dt = dy_ref[rows, D_A + l0:D_A + l0 + LANES].astype(F32) * (sg * (1.0 + t * (1.0 - sg)))
                dn = dt * lgq
                ns.append(n)
                dns.append(dn)
                s1 = jnp.sum(dn, axis=-1, keepdims=True)
                s2 = jnp.sum(dn * n, axis=-1, keepdims=True)
                m1 = s1 if m1 is None else m1 + s1
                m2 = s2 if m2 is None else m2 + s2
                sacc_ref[0, :, l0:l0 + LANES] = sacc_ref[0, :, l0:l0 + LANES] + _fold8(dt * n)
                sacc_ref[1, :, l0:l0 + LANES] = sacc_ref[1, :, l0:l0 + LANES] + _fold8(dt)
            m1 = m1 * (1.0 / D_B)
            m2 = m2 * (1.0 / D_B)
            for q, l0 in enumerate(range(0, D_B, LANES)):
                du = rstd * (dns[q] - m1 - ns[q] * m2)
                du_ref[rows, l0:l0 + LANES] = du
                sacc_ref[2, :, l0:l0 + LANES] = sacc_ref[2, :, l0:l0 + LANES] + _fold8(du)

        @pl.when(i % 2 == 0)
        def _():
            stage(dy0_ref, dy1_ref)

        @pl.when(i % 2 == 1)
        def _():
            stage(dy1_ref, dy0_ref)

        @pl.when(i == nt)
        def _():
            _reduce_acc(dwa_ref, acc_ref, K_A)
            dlg_ref[...] = jnp.sum(sacc_ref[0], axis=0, keepdims=True)
            dlb_ref[...] = jnp.sum(sacc_ref[1], axis=0, keepdims=True)
            dbb_ref[...] = jnp.sum(sacc_ref[2], axis=0, keepdims=True)

    cur = lambda i: jnp.minimum(i, nt - 1)
    old = lambda i: jnp.maximum(i - 1, 0)
    return _call(
        body, name="bwd_mix_a", grid=(nt + 1,),
        in_specs=[_rows_at(ts, D_MODEL, cur), _const((D_MODEL, D_MODEL)), _rows_at(ts, D_IN, old), _prev_at(ts, D_IN, old),
                  _next_at(ts, D_IN, s, old), _rows_at(ts, D_B, old), _const((K_A, D_A)), _const((1, D_B)), _const((1, D_B))],
        out_specs=[_rows_at(ts, D_A, old), _rows_at(ts, D_B, old), _rows_at(ts, D_A, old), _acc_out((K_A, D_A)),
                   _acc_out((1, D_B)), _acc_out((1, D_B)), _acc_out((1, D_B))],
        out_shape=[_sds((s, D_A), F32), _sds((s, D_B), F32), _sds((s, D_A), BF16), _sds((K_A, D_A), F32),
                   _sds((1, D_B), F32), _sds((1, D_B), F32), _sds((1, D_B), F32)],
        scratch_shapes=[_ext_scratch(ts, D_A), pltpu.VMEM((ts, D_MODEL), BF16), pltpu.VMEM((ts, D_MODEL), BF16),
                        pltpu.VMEM((K_A, SUBLANES, D_A), F32), pltpu.VMEM((3, SUBLANES, D_B), F32)],
        args=(dx2b, w_out_t, z, z, z, u, wa, lg, lb), exchange=exchange)


def _bwd_mix_b(dca, du, z, dab, wa, wb, w_in, x, g1, dx2, ts, exchange=None):
    s = x.shape[0]
    nt = s // ts
    rc = min(ROW_CHUNK, ts)
    n_sub = ts // rc

    def body(dca_ref, dcap_ref, dcan_ref, du_ref, dup_ref, dun_ref, z_ref, zp_ref, zn_ref, dab_ref,
             wa_ref, wb_ref, wi_ref, x_ref, g1_ref, dx2_ref,
             dz_ref, dx_ref, dg1_ref, dwb_ref, dcae_ref, due_ref, ue_ref, acc_ref, dg8_ref,
             dz0_ref, dz1_ref, dh0_ref, dh1_ref):
        i = pl.program_id(0)

        @pl.when(i == 0)
        def _():
            acc_ref[...] = jnp.zeros_like(acc_ref)
            dg8_ref[...] = jnp.zeros_like(dg8_ref)
            dz1_ref[...] = jnp.zeros_like(dz1_ref)
            dh1_ref[...] = jnp.zeros_like(dh1_ref)

        vt = jnp.minimum(i, nt - 1)
        first = vt == 0
        last = vt == nt - 1
        live = (i < nt).astype(F32)
        _fill_halo(dcae_ref, dcap_ref[...], dcan_ref[...], ts, first, last)
        _fill_halo(due_ref, dup_ref[...], dun_ref[...], ts, first, last)
        _, up = _p_u0(zp_ref, slice(None))
        _, un = _p_u0(zn_ref, slice(None))
        _fill_halo(ue_ref, up, un, ts, first, last)

        def fill(j, carry):
            r0 = pl.multiple_of(j * rc, rc)
            rows = pl.ds(r0, rc)
            _put_rows(dcae_ref, r0, rc, dca_ref[rows, :])
            _put_rows(due_ref, r0, rc, du_ref[rows, :])
            b_v = z_ref[rows, 3 * D_A:3 * D_A + D_B].astype(F32)
            b_g = z_ref[rows, 3 * D_A + D_B:D_IN].astype(F32)
            _put_rows(ue_ref, r0, rc, b_v * _sigmoid(b_g))
            return carry

        lax.fori_loop(0, n_sub, fill, 0)

        def stage(dz_new, dz_old, dh_new, dh_old):
            def put(rows, c0, val):
                dz_ref[rows, c0:c0 + LANES] = val
                dz_new[rows, c0:c0 + LANES] = val

            def mixer_a(r0, l0):
                rows = slice(r0, r0 + rc)
                dp = _conv_block(dcae_ref, wa_ref, r0, rc, l0, K_A, True)
                a_h = z_ref[rows, l0:l0 + LANES].astype(F32)
                a_c = z_ref[rows, 2 * D_A + l0:2 * D_A + l0 + LANES].astype(F32)
                put(rows, l0, (dp * a_c).astype(BF16))
                put(rows, D_A + l0, dab_ref[rows, l0:l0 + LANES])
                put(rows, 2 * D_A + l0, (dp * a_h).astype(BF16))

            def mixer_b(r0, l0):
                rows = slice(r0, r0 + rc)
                du0 = _conv_block(due_ref, wb_ref, r0, rc, l0, K_B, True)
                b_v = z_ref[rows, 3 * D_A + l0:3 * D_A + l0 + LANES].astype(F32)
                b_g = z_ref[rows, 3 * D_A + D_B + l0:3 * D_A + D_B + l0 + LANES].astype(F32)
                sg = _sigmoid(b_g)
                put(rows, 3 * D_A + l0, (du0 * sg).astype(BF16))
                put(rows, 3 * D_A + D_B + l0, (du0 * b_v * (sg * (1.0 - sg))).astype(BF16))
                _conv_wgrad_block(acc_ref, du_ref[rows, l0:l0 + LANES], ue_ref, r0, rc, l0, K_B, live)

            def tail(r0):
                rows = slice(r0, r0 + rc)
                dx, dgrow = _rms_bwd_rows(dh_old[rows, :], x_ref[rows, :], g1_ref[...])
                dx_ref[rows, :] = dx2_ref[rows, :] + dx
                dg8_ref[...] = dg8_ref[...] + _fold8(dgrow)

            units = []
            for q in range(n_sub):
                units += [(mixer_a, (q * rc, l0)) for l0 in range(0, D_A, LANES)]
                units += [(mixer_b, (q * rc, l0)) for l0 in range(0, D_B, LANES)]
                units.append((tail, (q * rc,)))
            _interleaved(units, _matmul_pieces([(dz_old, wi_ref)], dh_new, 2))

        @pl.when(i % 2 == 0)
        def _():
            stage(dz0_ref, dz1_ref, dh0_ref, dh1_ref)

        @pl.when(i % 2 == 1)
        def _():
            stage(dz1_ref, dz0_ref, dh1_ref, dh0_ref)

        @pl.when(i == nt + 1)
        def _():
            _reduce_acc(dwb_ref, acc_ref, K_B)
            dg1_ref[...] = jnp.sum(dg8_ref[...], axis=0, keepdims=True)

    vtile = lambda i: jnp.minimum(i, nt - 1)
    ttile = lambda i: jnp.clip(i - 2, 0, nt - 1)
    return _call(
        body, name="bwd_mix_b", grid=(nt + 2,),
        in_specs=[_rows_at(ts, D_A, vtile), _prev_at(ts, D_A, vtile), _next_at(ts, D_A, s, vtile),
                  _rows_at(ts, D_B, vtile), _prev_at(ts, D_B, vtile), _next_at(ts, D_B, s, vtile),
                  _rows_at(ts, D_IN, vtile), _prev_at(ts, D_IN, vtile), _next_at(ts, D_IN, s, vtile), _rows_at(ts, D_A, vtile),
                  _const((K_A, D_A)), _const((K_B, D_B)), _const((D_IN, D_MODEL)), _rows_at(ts, D_MODEL, ttile),
                  _const((1, D_MODEL)), _rows_at(ts, D_MODEL, ttile)],
        out_specs=[_rows_at(ts, D_IN, vtile), _rows_at(ts, D_MODEL, ttile), _acc_out((1, D_MODEL)), _acc_out((K_B, D_B))],
        out_shape=[_sds((s, D_IN), BF16), _sds((s, D_MODEL), F32), _sds((1, D_MODEL), F32), _sds((K_B, D_B), F32)],
        scratch_shapes=[_ext_scratch(ts, D_A), _ext_scratch(ts, D_B), _ext_scratch(ts, D_B),
                        pltpu.VMEM((K_B, SUBLANES, D_B), F32), pltpu.VMEM((SUBLANES, D_MODEL), F32),
                        pltpu.VMEM((ts, D_IN), BF16), pltpu.VMEM((ts, D_IN), BF16),
                        pltpu.VMEM((ts, D_MODEL), F32), pltpu.VMEM((ts, D_MODEL), F32)],
        args=(dca, dca, dca, du, du, du, z, z, z, dab, wa, wb, w_in, x, g1, dx2), exchange=exchange)


def _matmul_tn(a, b, name, exchange=None):
    s, m = a.shape
    n = b.shape[1]
    tk = min(1024, s)
    nk = s // tk
    tm = 256

    def body(a_ref, b_ref, o_ref, acc_ref):
        k = pl.program_id(0)

        @pl.when(k == 0)
        def _():
            acc_ref[...] = jnp.zeros_like(acc_ref)

        for m0 in range(0, m, tm):
            acc_ref[m0:m0 + tm, :] = acc_ref[m0:m0 + tm, :] + lax.dot_general(
                a_ref[:, m0:m0 + tm], b_ref[...], _TN, preferred_element_type=F32)

        @pl.when(k == nk - 1)
        def _():
            o_ref[...] = acc_ref[...].astype(BF16)

    (out,), got = _call(
        body, name=name, grid=(nk,),
        in_specs=[_rows(tk, m), _rows(tk, n)],
        out_specs=[_acc_out((m, n))],
        out_shape=[_sds((m, n), BF16)],
        scratch_shapes=[pltpu.VMEM((m, n), F32)], args=(a, b), exchange=exchange)
    return out if exchange is None else (out, got)


CHIP_RELS = ((1, 0, 0), (0, 1, 0), (1, 1, 0))
CORE_RELS = ((0, 0, 1),)
ALL_RELS = ((0, 0, 1), (0, 1, 0), (0, 1, 1), (1, 0, 0), (1, 0, 1), (1, 1, 0), (1, 1, 1))


def _chip_slot(dev):
    return 2 * dev[0] + dev[1]


def _dev_slot(dev):
    return 4 * dev[0] + 2 * dev[1] + dev[2]


def _me():
    return (lax.axis_index("x"), lax.axis_index("y"), lax.axis_index("c"))


def _peer(me, rel):
    return tuple((1 - me[a]) if rel[a] else me[a] for a in range(3))


_ANY = pl.BlockSpec(memory_space=pl.ANY)


class _Exchange:
    def __init__(self, inputs, out_shape, scratch, start, finish, forward=None):
        self.inputs, self.out_shape, self.scratch = list(inputs), list(out_shape), list(scratch)
        self.start, self.finish, self.forward = start, finish, forward


def _all_gather(payloads):
    n_p = len(payloads)
    n_k = 1 + 2 * len(CHIP_RELS)

    def copy(srcs, dsts, sems, p, k, block_dev, to, from_src):
        blk = dsts[p].at[_dev_slot(block_dev)]
        return pltpu.make_async_remote_copy(
            src_ref=srcs[p] if from_src else blk, dst_ref=blk,
            send_sem=sems[0].at[n_k * p + k], recv_sem=sems[1].at[n_k * p + k], device_id=to, device_id_type=MESH)

    def own_copy(srcs, dsts, sems, p):
        return pltpu.make_async_copy(srcs[p], dsts[p].at[_dev_slot(_me())], sems[2].at[p])

    def start(srcs, dsts, sems):
        me = _me()
        for p in range(n_p):
            own_copy(srcs, dsts, sems, p).start()
        for j, rel in enumerate(CHIP_RELS):
            for p in range(n_p):
                copy(srcs, dsts, sems, p, 1 + j, me, _peer(me, rel), True).start()
        for p in range(n_p):
            copy(srcs, dsts, sems, p, 0, me, _peer(me, CORE_RELS[0]), True).start()

    def forward(srcs, dsts, sems):
        me = _me()
        sibling = _peer(me, CORE_RELS[0])
        for j, rel in enumerate(CHIP_RELS):
            other = _peer(me, rel)
            for p in range(n_p):
                copy(srcs, dsts, sems, p, 1 + j, other, me, False).wait_recv()
                copy(srcs, dsts, sems, p, 4 + j, other, sibling, False).start()

    def finish(srcs, dsts, sems):
        me = _me()
        sibling = _peer(me, CORE_RELS[0])
        for p in range(n_p):
            copy(srcs, dsts, sems, p, 0, sibling, me, False).wait_recv()
        for j, rel in enumerate(CHIP_RELS):
            for p in range(n_p):
                copy(srcs, dsts, sems, p, 4 + j, _peer(sibling, rel), me, False).wait_recv()
        for p in range(n_p):
            own_copy(srcs, dsts, sems, p).wait()
            copy(srcs, dsts, sems, p, 0, me, sibling, True).wait_send()
            for j, rel in enumerate(CHIP_RELS):
                copy(srcs, dsts, sems, p, 1 + j, me, _peer(me, rel), True).wait_send()
                copy(srcs, dsts, sems, p, 4 + j, _peer(me, rel), sibling, False).wait_send()

    return _Exchange(
        payloads, [_sds((N_DEV,) + p.shape, p.dtype) for p in payloads],
        [pltpu.SemaphoreType.DMA((n_p * n_k,)), pltpu.SemaphoreType.DMA((n_p * n_k,)), pltpu.SemaphoreType.DMA((n_p,))],
        start, finish, forward)


def _gather_direct(payload):
    n_r = len(ALL_RELS)

    def copies(srcs, dsts, sems):
        me = _me()
        mine = dsts[0].at[_dev_slot(me)]
        own = pltpu.make_async_copy(srcs[0], mine, sems[2].at[0])
        remote = [pltpu.make_async_remote_copy(src_ref=srcs[0], dst_ref=mine, send_sem=sems[0].at[k], recv_sem=sems[1].at[k],
                                               device_id=_peer(me, rel), device_id_type=MESH)
                  for k, rel in enumerate(ALL_RELS)]
        return [own] + remote

    def start(srcs, dsts, sems):
        for cp in copies(srcs, dsts, sems):
            cp.start()

    def finish(srcs, dsts, sems):
        for cp in copies(srcs, dsts, sems):
            cp.wait()

    return _Exchange([payload], [_sds((N_DEV,) + payload.shape, payload.dtype)],
                     [pltpu.SemaphoreType.DMA((n_r,)), pltpu.SemaphoreType.DMA((n_r,)), pltpu.SemaphoreType.DMA((1,))],
                     start, finish)


def _scatter_exchange(payloads, rels, src_view, view_shapes):
    n_p = len(payloads)
    n_r = len(rels)

    def copies(srcs, dsts, sems):
        me = _me()
        out = []
        for k, rel in enumerate(rels):
            peer = _peer(me, rel)
            for p in range(n_p):
                out.append(pltpu.make_async_remote_copy(
                    src_ref=src_view(srcs[p], peer), dst_ref=dsts[p].at[k],
                    send_sem=sems[0].at[p * n_r + k], recv_sem=sems[1].at[p * n_r + k],
                    device_id=peer, device_id_type=MESH))
        return out

    def start(srcs, dsts, sems):
        for cp in copies(srcs, dsts, sems):
            cp.start()

    def finish(srcs, dsts, sems):
        for cp in copies(srcs, dsts, sems):
            cp.wait()

    return _Exchange(payloads, [_sds((n_r,) + vs, p.dtype) for vs, p in zip(view_shapes, payloads)],
                     [pltpu.SemaphoreType.DMA((n_p * n_r,)), pltpu.SemaphoreType.DMA((n_p * n_r,))], start, finish)


def _split_refs(refs, sizes):
    out, at = [], 0
    for n in sizes:
        out.append(refs[at:at + n])
        at += n
    return out


def _run_exchanges(name, exchanges):
    n_in = [len(e.inputs) for e in exchanges]
    n_out = [len(e.out_shape) for e in exchanges]
    n_sc = [len(e.scratch) for e in exchanges]

    def body(*refs):
        ins, outs, scs = _split_refs(refs, [sum(n_in), sum(n_out), sum(n_sc)])
        parts = list(zip(exchanges, _split_refs(ins, n_in), _split_refs(outs, n_out), _split_refs(scs, n_sc)))
        for e, i, o, s in parts:
            e.start(i, o, s)
        for e, i, o, s in parts:
            if e.forward is not None:
                e.forward(i, o, s)
        for e, i, o, s in parts:
            e.finish(i, o, s)

    outs = pl.pallas_call(
        body, name=name, in_specs=[_ANY] * sum(n_in), out_specs=[_ANY] * sum(n_out),
        out_shape=[sd for e in exchanges for sd in e.out_shape],
        scratch_shapes=[sc for e in exchanges for sc in e.scratch],
    )(*[a for e in exchanges for a in e.inputs])
    return _split_refs(list(outs), n_out)


def _call(body, *, name, grid, in_specs, out_specs, out_shape, scratch_shapes, args, exchange=None, forward_step=None):
    n_in, n_out, n_sc = len(in_specs), len(out_specs), len(scratch_shapes)
    if exchange is None:
        outs = pl.pallas_call(body, name=name, grid=grid, in_specs=in_specs, out_specs=out_specs, out_shape=out_shape,
                              scratch_shapes=scratch_shapes, compiler_params=_params())(*args)
        return list(outs), []
    e = exchange
    sizes = [n_in, len(e.inputs), n_out, len(e.out_shape), n_sc, len(e.scratch)]
    last = grid[0] - 1

    def wrapped(*refs):
        a, ei, o, eo, sc, es = _split_refs(refs, sizes)
        i = pl.program_id(0)

        @pl.when(i == 0)
        def _():
            e.start(ei, eo, es)

        if e.forward is not None:
            @pl.when(i == forward_step)
            def _():
                e.forward(ei, eo, es)

        body(*a, *o, *sc)

        @pl.when(i == last)
        def _():
            e.finish(ei, eo, es)

    outs = pl.pallas_call(
        wrapped, name=name, grid=grid,
        in_specs=list(in_specs) + [_ANY] * len(e.inputs), out_specs=list(out_specs) + [_ANY] * len(e.out_shape),
        out_shape=list(out_shape) + e.out_shape, scratch_shapes=list(scratch_shapes) + e.scratch,
        compiler_params=_params(),
    )(*args, *e.inputs)
    outs = list(outs)
    return outs[:n_out], outs[n_out:]


def _pair_sum(grads, recvd, my_core, name):
    n_p = len(grads)

    def body(c_ref, *refs):
        del c_ref
        for p in range(n_p):
            refs[2 * n_p + p][...] = (refs[p][...].astype(F32) + refs[n_p + p][...].astype(F32)).astype(BF16)

    def blk(g):
        return (None, None) + g.shape[2:]

    return pl.pallas_call(
        body, name=name,
        grid_spec=pltpu.PrefetchScalarGridSpec(
            num_scalar_prefetch=1, grid=(N_CHIP,),
            in_specs=[pl.BlockSpec(blk(g), lambda j, c: (j, c[0], 0, 0)) for g in grads]
            + [pl.BlockSpec(blk(g), lambda j, c: (0, j, 0, 0)) for g in grads],
            out_specs=[pl.BlockSpec((None,) + g.shape[2:], lambda j, c: (j, 0, 0)) for g in grads]),
        out_shape=[_sds((N_CHIP,) + g.shape[2:], BF16) for g in grads],
        compiler_params=pltpu.CompilerParams(dimension_semantics=("arbitrary",), vmem_limit_bytes=VMEM_LIMIT),
    )(my_core, *grads, *recvd)


def _chip_sum(psums, recvd, my_chip, name):
    n_p = len(psums)

    def body(c_ref, *refs):
        del c_ref
        for p in range(n_p):
            acc = refs[p][...].astype(F32)
            for k in range(len(CHIP_RELS)):
                acc = acc + refs[n_p + p][k].astype(F32)
            refs[2 * n_p + p][...] = acc

    return pl.pallas_call(
        body, name=name,
        grid_spec=pltpu.PrefetchScalarGridSpec(
            num_scalar_prefetch=1, grid=(1,),
            in_specs=[pl.BlockSpec((None,) + g.shape[1:], lambda i, c: (c[0], 0, 0)) for g in psums]
            + [pl.BlockSpec(r.shape, lambda i, c: (0, 0, 0)) for r in recvd],
            out_specs=[pl.BlockSpec(g.shape[1:], lambda i, c: (0, 0)) for g in psums]),
        out_shape=[_sds(g.shape[1:], F32) for g in psums],
        compiler_params=pltpu.CompilerParams(dimension_semantics=("arbitrary",), vmem_limit_bytes=VMEM_LIMIT),
    )(my_chip, *psums, *recvd)


def _sum_devices(parts):
    def body(p_ref, o_ref):
        acc = p_ref[0]
        for j in range(1, N_DEV):
            acc = acc + p_ref[j]
        o_ref[...] = acc

    return pl.pallas_call(body, name="small_grad_sum", out_shape=_sds(parts.shape[1:], F32))(parts)


def _adamw(ws, gs, ms, vs, name):
    n_t = len(ws)

    def body(*refs):
        w_refs, g_refs, m_refs, v_refs = (refs[j * n_t:(j + 1) * n_t] for j in range(4))
        outs = refs[4 * n_t:]
        for k in range(n_t):
            gg = g_refs[k][...]
            mn = ADAM_B1 * m_refs[k][...] + (1.0 - ADAM_B1) * gg
            vn = ADAM_B2 * v_refs[k][...] + (1.0 - ADAM_B2) * (gg * gg)
            m_hat = mn / (1.0 - ADAM_B1 ** ADAM_STEP)
            v_hat = vn / (1.0 - ADAM_B2 ** ADAM_STEP)
            outs[3 * k][...] = -ADAM_LR * (m_hat / (jnp.sqrt(v_hat) + ADAM_EPS) + ADAM_WD * w_refs[k][...])
            outs[3 * k + 1][...] = mn
            outs[3 * k + 2][...] = vn

    out_shape = [_sds(w.shape, F32) for w in ws for _ in range(3)]
    return pl.pallas_call(body, name=name, out_shape=out_shape,
                          compiler_params=pltpu.CompilerParams(vmem_limit_bytes=VMEM_LIMIT))(*ws, *gs, *ms, *vs)


class _Mesh:
    def __init__(self, shards, my_chip, my_core):
        self.shards, self.my_chip, self.my_core = shards, my_chip.reshape(1), my_core.reshape(1)

    def gather(self, names):
        return _all_gather([self.shards[n] for n in names])

    @staticmethod
    def whole(gathered):
        return gathered.reshape(N_DEV * gathered.shape[1], gathered.shape[2])

    @staticmethod
    def by_device(grads):
        return [g.reshape(N_CHIP, 2, g.shape[0] // N_DEV, g.shape[1]) for g in grads]

    @staticmethod
    def to_sibling(parts):
        return _scatter_exchange(parts, CORE_RELS, lambda ref, peer: ref.at[:, peer[2]],
                                 [(N_CHIP,) + p.shape[2:] for p in parts])

    @staticmethod
    def to_chips(pair):
        return _scatter_exchange(pair, CHIP_RELS, lambda ref, peer: ref.at[_chip_slot(peer)], [p.shape[1:] for p in pair])


def _step(x, target, g1, w_in_t, wa, wb, bb, lg, lb, w_out, g2, w_gate_t, w_up_t, wf, w_down, g3, ts, mesh=None):
    (z, h1), got = _fwd_in(x, g1, w_in_t, ts, exchange=mesh and mesh.gather(["w_out"]))
    if mesh:
        w_out = mesh.whole(got[0])
    (x2, y, u), got = _fwd_mix(z, x, wa, wb, bb, lg, lb, w_out, ts, exchange=mesh and mesh.gather(["w_gate", "w_up"]))
    if mesh:
        w_gate_t, w_up_t = [mesh.whole(g) for g in got]
    (g0, v, h2), got = _fwd_ffn_in(x2, g2, w_gate_t, w_up_t, ts, exchange=mesh and mesh.gather(["w_down"]))
    if mesh:
        w_down = mesh.whole(got[0])
    tk = min(ts, SKEW_TILE)
    a, dx3, dx3b, loss, dg3, dgc, dv, dwf = _ffn_out_and_back(g0, v, x2, wf, w_down, g3, target, tk)
    dg0, dx2, dx2b, dg2 = _bwd_ffn_b(dgc, dv, wf, w_gate_t, w_up_t, x2, g2, dx3, tk)
    first = dict(w_down=_matmul_tn(a, dx3b, "wgrad_down"), w_gate=_matmul_tn(dg0, h2, "wgrad_gate"),
                 w_up=_matmul_tn(dv, h2, "wgrad_up"), w_out=_matmul_tn(y, dx2b, "wgrad_out"))
    if mesh:
        parts = mesh.by_device(list(first.values()))
    (dca, du, dab, dwa, dlg, dlb, dbb), got = _bwd_mix_a(dx2b, jnp.swapaxes(w_out, 0, 1), z, u, wa, lg, lb, tk,
                                                         exchange=mesh and mesh.to_sibling(parts))
    if mesh:
        pair = _pair_sum(parts, got, mesh.my_core, "rs_pair_sum_first")
    (dz, dx, dg1, dwb), got = _bwd_mix_b(dca, du, z, dab, wa, wb, w_in_t, x, g1, dx2, tk,
                                         exchange=mesh and mesh.to_chips(pair))
    small = dict(norm_mix_g=dg1, conv_a_w=dwa, conv_b_w=dwb, conv_b_b=dbb, ln_b_g=dlg, ln_b_b=dlb,
                 norm_ffn_g=dg2, conv_ffn_w=dwf, norm_final_g=dg3)
    if not mesh:
        return loss, dx, dict(w_in=_matmul_tn(dz, h1, "wgrad_in"), **first), small
    big = dict(zip(first, _chip_sum(pair, got, mesh.my_chip, "rs_chip_sum_first")))
    dw_in_t, (every,) = _matmul_tn(dz, h1, "wgrad_in", exchange=_gather_direct(_pack_small_grads(small, loss)))
    parts = mesh.by_device([dw_in_t])
    (got,) = _run_exchanges("rs_cores_last", [mesh.to_sibling(parts)])
    pair = _pair_sum(parts, got, mesh.my_core, "rs_pair_sum_last")
    (got,) = _run_exchanges("rs_chips_last", [mesh.to_chips(pair)])
    (big["w_in"],) = _chip_sum(pair, got, mesh.my_chip, "rs_chip_sum_last")
    shapes = [loss.shape if n == "loss" else small[n].shape for n in _SMALL_NAMES]
    return None, dx, big, _unpack_small_grads(_sum_devices(every), shapes)


def _pack_small_weights(conv_a_s, conv_b_s, conv_ffn_s):
    buf = jnp.zeros((SMALL_W_ROWS, SMALL_W_COLS), F32)
    buf = buf.at[0:K_A, 0:CONV_A_COLS].set(conv_a_s)
    buf = buf.at[K_A:K_A + K_B, 0:CONV_A_COLS].set(conv_b_s)
    return buf.at[K_A + K_B:K_A + K_B + K_F, 0:W_FF_COLS].set(conv_ffn_s)


def _unpack_small_weights(full):
    def take(r0, k, w):
        return jnp.transpose(full[:, r0:r0 + k, 0:w], (1, 0, 2)).reshape(k, N_DEV * w)

    return take(0, K_A, CONV_A_COLS), take(K_A, K_B, CONV_A_COLS), take(K_A + K_B, K_F, W_FF_COLS)


_SMALL_NAMES = ("conv_b_w", "conv_a_w", "conv_ffn_w", "norm_mix_g", "norm_ffn_g", "norm_final_g",
                "conv_b_b", "ln_b_g", "ln_b_b", "loss")
SMALL_G_ROWS = 64


def _small_rows(shapes):
    plan, at = [], 0
    for idx, (rows, cols) in enumerate(shapes):
        for r in range(rows):
            for c0 in range(0, cols, SMALL_G_COLS):
                plan.append((idx, r, c0, min(SMALL_G_COLS, cols - c0), at))
                at += 1
    assert at <= SMALL_G_ROWS
    return plan


def _pack_small_grads(small, loss):
    srcs = [loss if n == "loss" else small[n] for n in _SMALL_NAMES]
    plan = _small_rows([a.shape for a in srcs])

    def body(*refs):
        out = refs[-1]
        out[...] = jnp.zeros_like(out)
        for idx, r, c0, width, at in plan:
            out[at:at + 1, 0:width] = refs[idx][r:r + 1, c0:c0 + width]

    return pl.pallas_call(body, name="small_grad_pack", out_shape=_sds((SMALL_G_ROWS, SMALL_G_COLS), F32))(*srcs)


def _unpack_small_grads(tot, shapes):
    out, at = {}, 0
    for name, (rows, cols) in zip(_SMALL_NAMES, shapes):
        per_row = -(-cols // SMALL_G_COLS)
        blk = tot[at:at + rows * per_row]
        at += rows * per_row
        out[name] = blk.reshape(rows, per_row * SMALL_G_COLS)[:, 0:cols]
    out["loss"] = out["loss"][0, 0]
    return out


def kernel(x, norm_mix_g, w_in, conv_a_w, conv_b_w, conv_b_b, ln_b_g, ln_b_b, w_out, norm_ffn_g, w_gate, w_up, conv_ffn_w, w_down, norm_final_g, loss_target, m_norm_mix_g, m_w_in, m_conv_a_w, m_conv_b_w, m_conv_b_b, m_ln_b_g, m_ln_b_b, m_w_out, m_norm_ffn_g, m_w_gate, m_w_up, m_conv_ffn_w, m_w_down, m_norm_final_g, v_norm_mix_g, v_w_in, v_conv_a_w, v_conv_b_w, v_conv_b_b, v_ln_b_g, v_ln_b_b, v_w_out, v_norm_ffn_g, v_w_gate, v_w_up, v_conv_ffn_w, v_w_down, v_norm_final_g):
    ix, iy, ic = lax.axis_index("x"), lax.axis_index("y"), lax.axis_index("c")
    my_chip = (2 * ix + iy).astype(jnp.int32)
    my_core = ic.astype(jnp.int32)
    my_dev = 2 * my_chip + my_core

    weights = dict(norm_mix_g=norm_mix_g, w_in=w_in, conv_a_w=conv_a_w, conv_b_w=conv_b_w, conv_b_b=conv_b_b,
                   ln_b_g=ln_b_g, ln_b_b=ln_b_b, w_out=w_out, norm_ffn_g=norm_ffn_g, w_gate=w_gate, w_up=w_up,
                   conv_ffn_w=conv_ffn_w, w_down=w_down, norm_final_g=norm_final_g)
    m_in = dict(norm_mix_g=m_norm_mix_g, w_in=m_w_in, conv_a_w=m_conv_a_w, conv_b_w=m_conv_b_w, conv_b_b=m_conv_b_b,
                ln_b_g=m_ln_b_g, ln_b_b=m_ln_b_b, w_out=m_w_out, norm_ffn_g=m_norm_ffn_g, w_gate=m_w_gate,
                w_up=m_w_up, conv_ffn_w=m_conv_ffn_w, w_down=m_w_down, norm_final_g=m_norm_final_g)
    v_in = dict(norm_mix_g=v_norm_mix_g, w_in=v_w_in, conv_a_w=v_conv_a_w, conv_b_w=v_conv_b_w, conv_b_b=v_conv_b_b,
                ln_b_g=v_ln_b_g, ln_b_b=v_ln_b_b, w_out=v_w_out, norm_ffn_g=v_norm_ffn_g, w_gate=v_w_gate,
                w_up=v_w_up, conv_ffn_w=v_conv_ffn_w, w_down=v_w_down, norm_final_g=v_norm_final_g)
    order = list(weights)
    big_names = ("w_in", "w_gate", "w_up", "w_out", "w_down")
    transposed = ("w_in", "w_gate", "w_up")

    def shard2d(name, a):
        if name in transposed:
            return jnp.swapaxes(a[0], 0, 1)
        return a.reshape(1, a.shape[0]) if a.ndim == 1 else a.reshape(a.shape[-2:])

    def unshard2d(name, a2, like):
        if name in transposed:
            return jnp.swapaxes(a2, 0, 1)[None]
        return a2.reshape(like.shape)

    mesh = _Mesh({n: shard2d(n, weights[n]).astype(BF16) for n in big_names}, my_chip, my_core)
    gathered, = _run_exchanges("ag_first", [_all_gather(
        [mesh.shards["w_in"], _pack_small_weights(conv_a_w[0], conv_b_w[0], conv_ffn_w[0])])])
    w_in_t = mesh.whole(gathered[0])
    wa_f, wb_f, wf_f = _unpack_small_weights(gathered[1])

    _, dx, gsum, stot = _step(
        x[0], loss_target[0], norm_mix_g, w_in_t, wa_f, wb_f, conv_b_b, ln_b_g, ln_b_b, None, norm_ffn_g,
        None, None, wf_f, None, norm_final_g.reshape(1, D_MODEL), SEQ_TILE, mesh)

    grads2d = dict(
        norm_mix_g=stot["norm_mix_g"],
        conv_a_w=lax.dynamic_slice(stot["conv_a_w"], (0, my_dev * CONV_A_COLS), (K_A, CONV_A_COLS)),
        conv_b_w=lax.dynamic_slice(stot["conv_b_w"], (0, my_dev * CONV_A_COLS), (K_B, CONV_A_COLS)),
        conv_b_b=stot["conv_b_b"], ln_b_g=stot["ln_b_g"], ln_b_b=stot["ln_b_b"],
        norm_ffn_g=stot["norm_ffn_g"],
        conv_ffn_w=lax.dynamic_slice(stot["conv_ffn_w"], (0, my_dev * W_FF_COLS), (K_F, W_FF_COLS)),
        norm_final_g=stot["norm_final_g"],
        **gsum,
    )

    updates = {}
    small_names = [n for n in order if n not in big_names]
    for group, label in [([n], "adamw_" + n) for n in big_names] + [(small_names, "adamw_small")]:
        outs = _adamw([shard2d(n, weights[n]) for n in group], [grads2d[n] for n in group],
                      [shard2d(n, m_in[n]) for n in group], [shard2d(n, v_in[n]) for n in group], label)
        for k, n in enumerate(group):
            updates[n] = outs[3 * k:3 * k + 3]
    g_out = [unshard2d(n, grads2d[n], weights[n]) for n in order]
    d_out, m_out, v_out = [[unshard2d(n, updates[n][j], weights[n]) for n in order] for j in range(3)]

    return (stot["loss"], dx[None], *g_out, *d_out, *m_out, *v_out)
```

```python
import jax
import jax.numpy as jnp
from jax import lax
from jax.experimental import pallas as pl
from jax.experimental.pallas import tpu as pltpu

F32 = jnp.float32
BF16 = jnp.bfloat16

D_MODEL = 1024
D_A = 512
D_B = 512
D_IN = 3 * D_A + 2 * D_B
D_FF = 2816
K_A = 3
K_B = 31
K_F = 3
RMS_EPS = 1e-6
LN_EPS = 1e-5

ADAM_LR = 0.001
ADAM_B1 = 0.9
ADAM_B2 = 0.999
ADAM_EPS = 1e-08
ADAM_WD = 0.01
ADAM_STEP = 10

N_DEV = 8
N_CHIP = 4
LANES = 128
SUBLANES = 8
HALO = 16
ROW_CHUNK = 64
SEQ_TILE = 512
SKEW_TILE = 256
VMEM_LIMIT = 56 * 1024 * 1024

MESH = pl.DeviceIdType.MESH

W_FF_COLS = D_FF // N_DEV
CONV_A_COLS = D_A // N_DEV
SMALL_W_ROWS = 40
SMALL_W_COLS = 384
SMALL_G_COLS = 512
FF_PAD = 3072


def _rows(ts, c):
    return pl.BlockSpec((ts, c), lambda i: (i, 0))


def _const(shape):
    return pl.BlockSpec(shape, lambda i: (0,) * len(shape), pipeline_mode=pl.Buffered(1))


def _acc_out(shape):
    return pl.BlockSpec(shape, lambda i: (0,) * len(shape))


def _rows_at(ts, c, tile):
    return pl.BlockSpec((ts, c), lambda i: (tile(i), 0))


def _prev_at(ts, c, tile):
    return pl.BlockSpec((HALO, c), lambda i: (jnp.maximum(tile(i) * (ts // HALO) - 1, 0), 0))


def _next_at(ts, c, s, tile):
    last = s // HALO - 1
    return pl.BlockSpec((HALO, c), lambda i: (jnp.minimum((tile(i) + 1) * (ts // HALO), last), 0))


def _prev(ts, c):
    return _prev_at(ts, c, lambda i: i)


def _next(ts, c, s):
    return _next_at(ts, c, s, lambda i: i)


MXU_COLS = 256
MXU_ROWS = 256


def _col_pieces(n):
    return [(c0, min(MXU_COLS, n - c0)) for c0 in range(0, n, MXU_COLS)]


def _matmul_pieces(terms, out_ref, k_parts):
    m, n = out_ref.shape
    steps = []
    for lhs_ref, w_ref in terms:
        tiles = lhs_ref.shape[1] // MXU_COLS
        cuts = [MXU_COLS * (tiles * j // k_parts) for j in range(k_parts)] + [lhs_ref.shape[1]]
        steps += [(lhs_ref, w_ref, cuts[j], cuts[j + 1]) for j in range(k_parts)]

    def piece(m0, n0, width, step):
        lhs_ref, w_ref, k0, k1 = steps[step]
        part = jnp.dot(lhs_ref[m0:m0 + MXU_ROWS, k0:k1], w_ref[k0:k1, n0:n0 + width], preferred_element_type=F32)
        if step:
            part = part + out_ref[m0:m0 + MXU_ROWS, n0:n0 + width]
        out_ref[m0:m0 + MXU_ROWS, n0:n0 + width] = part

    return [(piece, (m0, n0, w, j)) for j in range(len(steps)) for n0, w in _col_pieces(n) for m0 in range(0, m, MXU_ROWS)]


def _interleaved(vector_units, matmul_pieces):
    n_u, n_p = len(vector_units), len(matmul_pieces)
    done = 0
    for k, (unit, args) in enumerate(vector_units):
        while done < n_p and done * n_u <= k * n_p:
            matmul_pieces[done][0](*matmul_pieces[done][1])
            done += 1
        unit(*args)
    for fn, args in matmul_pieces[done:]:
        fn(*args)


def _params():
    return pltpu.CompilerParams(dimension_semantics=("arbitrary",), vmem_limit_bytes=VMEM_LIMIT)


def _sds(shape, dtype):
    return jax.ShapeDtypeStruct(shape, dtype)


def _sigmoid(v):
    return 0.5 * jnp.tanh(0.5 * v) + 0.5


def _conv_block(ext_ref, w_ref, r0, rc, l0, k_taps, transposed):
    acc = None
    for k in range(k_taps):
        d = (k_taps // 2 - k) if transposed else (k - k_taps // 2)
        term = ext_ref[l0 // LANES, pl.ds(r0 + HALO + d, rc), :] * w_ref[k:k + 1, l0:l0 + LANES]
        acc = term if acc is None else acc + term
    return acc


def _conv_wgrad_block(acc_ref, dout, ext_ref, r0, rc, l0, k_taps, scale=None):
    for k in range(k_taps):
        prod = dout * ext_ref[l0 // LANES, pl.ds(r0 + HALO + k - k_taps // 2, rc), :]
        part = prod.reshape(rc // SUBLANES, SUBLANES, LANES).sum(axis=0)
        if scale is not None:
            part = part * scale
        acc_ref[k, :, l0:l0 + LANES] = acc_ref[k, :, l0:l0 + LANES] + part


def _reduce_acc(out_ref, acc_ref, k_taps):
    for k in range(k_taps):
        out_ref[k:k + 1, :] = jnp.sum(acc_ref[k], axis=0, keepdims=True)


def _fold8(v):
    rc, c = v.shape
    return v.reshape(rc // SUBLANES, SUBLANES, c).sum(axis=0)


def _ext_scratch(ts, c):
    return pltpu.VMEM((c // LANES, ts + 2 * HALO, LANES), F32)


def _put_rows(ext_ref, r0, rc, val):
    for q in range(val.shape[1] // LANES):
        ext_ref[q, pl.ds(r0 + HALO, rc), :] = val[:, q * LANES:(q + 1) * LANES]


def _fill_halo(ext_ref, vals_prev, vals_next, ts, first, last):
    for q in range(vals_prev.shape[1] // LANES):
        cols = slice(q * LANES, (q + 1) * LANES)
        ext_ref[q, 0:HALO, :] = jnp.where(first, 0.0, vals_prev[:, cols])
        ext_ref[q, HALO + ts:HALO + ts + HALO, :] = jnp.where(last, 0.0, vals_next[:, cols])


def _rms_bwd_rows(dh, xf, g):
    r = lax.rsqrt(jnp.mean(xf * xf, axis=-1, keepdims=True) + RMS_EPS)
    xhat = xf * r
    dxh = dh * g
    dx = r * (dxh - xhat * jnp.mean(dxh * xhat, axis=-1, keepdims=True))
    return dx, dh * xhat


_NT = (((1,), (1,)), ((), ()))
_TN = (((0,), (0,)), ((), ()))


def _fwd_in(x, g1, w_in_t, ts, exchange=None):
    s = x.shape[0]

    def body(x_ref, g_ref, w_ref, z_ref, h_ref):
        xf = x_ref[...]
        r = lax.rsqrt(jnp.mean(xf * xf, axis=-1, keepdims=True) + RMS_EPS)
        h = (xf * r * g_ref[...]).astype(BF16)
        h_ref[...] = h
        for n0 in range(0, D_IN, 512):
            z_ref[:, n0:n0 + 512] = lax.dot_general(h, w_ref[n0:n0 + 512, :], _NT,
                                                    preferred_element_type=F32).astype(BF16)

    return _call(
        body, name="fwd_in", grid=(s // ts,),
        in_specs=[_rows(ts, D_MODEL), _const((1, D_MODEL)), _const((D_IN, D_MODEL))],
        out_specs=[_rows(ts, D_IN), _rows(ts, D_MODEL)],
        out_shape=[_sds((s, D_IN), BF16), _sds((s, D_MODEL), BF16)],
        scratch_shapes=[], args=(x, g1, w_in_t), exchange=exchange, forward_step=(s // ts) * 3 // 4)


def _p_u0(z_ref, rows):
    a_h = z_ref[rows, 0:D_A].astype(F32)
    a_c = z_ref[rows, 2 * D_A:3 * D_A].astype(F32)
    b_v = z_ref[rows, 3 * D_A:3 * D_A + D_B].astype(F32)
    b_g = z_ref[rows, 3 * D_A + D_B:D_IN].astype(F32)
    return a_c * a_h, b_v * _sigmoid(b_g)


def _layernorm_rows(u_blocks):
    tot = None
    for ub in u_blocks:
        sm = jnp.sum(ub, axis=-1, keepdims=True)
        tot = sm if tot is None else tot + sm
    mu = tot * (1.0 / D_B)
    var = None
    for ub in u_blocks:
        sq = jnp.sum((ub - mu) * (ub - mu), axis=-1, keepdims=True)
        var = sq if var is None else var + sq
    rstd = lax.rsqrt(var * (1.0 / D_B) + LN_EPS)
    return mu, rstd


def _fwd_mix(z, x, wa, wb, bb, lg, lb, w_out, ts, exchange=None):
    s = x.shape[0]
    nt = s // ts
    rc = min(ROW_CHUNK, ts)

    def body(z_ref, zp_ref, zn_ref, x_ref, wa_ref, wb_ref, bb_ref, lg_ref, lb_ref, wo_ref,
             x2_ref, y_ref, u_ref, pe_ref, ue_ref):
        i = pl.program_id(0)
        pp, up = _p_u0(zp_ref, slice(None))
        pn, un = _p_u0(zn_ref, slice(None))
        _fill_halo(pe_ref, pp, pn, ts, i == 0, i == nt - 1)
        _fill_halo(ue_ref, up, un, ts, i == 0, i == nt - 1)

        def fill(j, carry):
            r0 = pl.multiple_of(j * rc, rc)
            p, u0 = _p_u0(z_ref, pl.ds(r0, rc))
            _put_rows(pe_ref, r0, rc, p)
            _put_rows(ue_ref, r0, rc, u0)
            return carry

        lax.fori_loop(0, ts // rc, fill, 0)

        def main(j, carry):
            r0 = pl.multiple_of(j * rc, rc)
            rows = pl.ds(r0, rc)
            for l0 in range(0, D_A, LANES):
                ca = _conv_block(pe_ref, wa_ref, r0, rc, l0, K_A, False)
                a_b = z_ref[rows, D_A + l0:D_A + l0 + LANES].astype(F32)
                y_ref[rows, l0:l0 + LANES] = (a_b * ca).astype(BF16)
            ubs = []
            for l0 in range(0, D_B, LANES):
                ub = _conv_block(ue_ref, wb_ref, r0, rc, l0, K_B, False) + bb_ref[:, l0:l0 + LANES]
                u_ref[rows, l0:l0 + LANES] = ub
                ubs.append(ub)
            mu, rstd = _layernorm_rows(ubs)
            for q, l0 in enumerate(range(0, D_B, LANES)):
                t = (ubs[q] - mu) * rstd * lg_ref[:, l0:l0 + LANES] + lb_ref[:, l0:l0 + LANES]
                y_ref[rows, D_A + l0:D_A + l0 + LANES] = (t * _sigmoid(t)).astype(BF16)
            return carry

        lax.fori_loop(0, ts // rc, main, 0, unroll=4)
        for n0 in range(0, D_MODEL, 512):
            x2_ref[:, n0:n0 + 512] = x_ref[:, n0:n0 + 512] + jnp.dot(
                y_ref[...], wo_ref[:, n0:n0 + 512], preferred_element_type=F32)

    return _call(
        body, name="fwd_mix", grid=(nt,),
        in_specs=[_rows(ts, D_IN), _prev(ts, D_IN), _next(ts, D_IN, s), _rows(ts, D_MODEL),
                  _const((K_A, D_A)), _const((K_B, D_B)), _const((1, D_B)), _const((1, D_B)), _const((1, D_B)),
                  _const((D_MODEL, D_MODEL))],
        out_specs=[_rows(ts, D_MODEL), _rows(ts, D_MODEL), _rows(ts, D_B)],
        out_shape=[_sds((s, D_MODEL), F32), _sds((s, D_MODEL), BF16), _sds((s, D_B), F32)],
        scratch_shapes=[_ext_scratch(ts, D_A), _ext_scratch(ts, D_B)],
        args=(z, z, z, x, wa, wb, bb, lg, lb, w_out), exchange=exchange, forward_step=nt * 5 // 8)


def _fwd_ffn_in(x2, g2, w_gate_t, w_up_t, ts, exchange=None):
    s = x2.shape[0]
    half = D_FF // 2

    def body(x_ref, g_ref, wg_ref, wu_ref, g0_ref, v_ref, h_ref):
        xf = x_ref[...]
        r = lax.rsqrt(jnp.mean(xf * xf, axis=-1, keepdims=True) + RMS_EPS)
        h = (xf * r * g_ref[...]).astype(BF16)
        h_ref[...] = h
        for n0 in range(0, D_FF, half):
            g0_ref[:, n0:n0 + half] = lax.dot_general(h, wg_ref[n0:n0 + half, :], _NT,
                                                      preferred_element_type=F32).astype(BF16)
            v_ref[:, n0:n0 + half] = lax.dot_general(h, wu_ref[n0:n0 + half, :], _NT,
                                                     preferred_element_type=F32).astype(BF16)

    return _call(
        body, name="fwd_ffn_in", grid=(s // ts,),
        in_specs=[_rows(ts, D_MODEL), _const((1, D_MODEL)), _const((D_FF, D_MODEL)), _const((D_FF, D_MODEL))],
        out_specs=[_rows(ts, D_FF), _rows(ts, D_FF), _rows(ts, D_MODEL)],
        out_shape=[_sds((s, D_FF), BF16), _sds((s, D_FF), BF16), _sds((s, D_MODEL), BF16)],
        scratch_shapes=[], args=(x2, g2, w_gate_t, w_up_t), exchange=exchange, forward_step=(s // ts) // 2)


def _ffn_out_and_back(g0, v, x2, wf, w_down, g3, target, ts):
    s = x2.shape[0]
    nt = s // ts
    rc = min(ROW_CHUNK, ts)
    half = D_FF // 2

    def body(g0_ref, gp_ref, gn_ref, v_ref, x2_ref, wf_ref, wd_ref, g3_ref, t_ref,
             a_ref, dx3_ref, dx3b_ref, loss_ref, dg3_ref, dg_ref, dv_ref, dwf_ref,
             ge_ref, silu_ref, dsv_ref, da_ref, acc_ref):
        i = pl.program_id(0)

        @pl.when(i == 0)
        def _():
            acc_ref[...] = jnp.zeros_like(acc_ref)

        _fill_halo(ge_ref, gp_ref[...].astype(F32), gn_ref[...].astype(F32), ts, i == 0, i == nt - 1)

        def fill(j, carry):
            r0 = pl.multiple_of(j * rc, rc)
            _put_rows(ge_ref, r0, rc, g0_ref[pl.ds(r0, rc), :].astype(F32))
            return carry

        lax.fori_loop(0, ts // rc, fill, 0)

        def act(j, carry):
            r0 = pl.multiple_of(j * rc, rc)
            rows = pl.ds(r0, rc)
            for l0 in range(0, D_FF, LANES):
                g = _conv_block(ge_ref, wf_ref, r0, rc, l0, K_F, False)
                vv = v_ref[rows, l0:l0 + LANES].astype(F32)
                sg = _sigmoid(g)
                silu = g * sg
                a_ref[rows, l0:l0 + LANES] = (silu * vv).astype(BF16)
                silu_ref[rows, l0:l0 + LANES] = silu
                dsv_ref[rows, l0:l0 + LANES] = (sg + silu * (1.0 - sg)) * vv
            return carry

        lax.fori_loop(0, ts // rc, act, 0)
        for n0 in range(0, D_MODEL, 512):
            dx3_ref[:, n0:n0 + 512] = x2_ref[:, n0:n0 + 512] + jnp.dot(
                a_ref[...], wd_ref[:, n0:n0 + 512], preferred_element_type=F32)

        @pl.when(i == 0)
        def _():
            loss_ref[...] = jnp.zeros_like(loss_ref)
            dg3_ref[...] = jnp.zeros_like(dg3_ref)

        def tail(j, carry):
            lsum, dgsum = carry
            r0 = j * rc
            rows = slice(r0, r0 + rc)
            x3 = dx3_ref[rows, :]
            r = lax.rsqrt(jnp.mean(x3 * x3, axis=-1, keepdims=True) + RMS_EPS)
            xhat = x3 * r
            diff = xhat * g3_ref[...] - t_ref[rows, :]
            dout = diff * (1.0 / D_MODEL)
            dxh = dout * g3_ref[...]
            dx3 = r * (dxh - xhat * jnp.mean(dxh * xhat, axis=-1, keepdims=True))
            dx3_ref[rows, :] = dx3
            dx3b_ref[rows, :] = dx3.astype(BF16)
            lsum = lsum + _fold8(diff * diff)
            dgsum = dgsum + _fold8(dout * xhat)
            return lsum, dgsum

        sums = (jnp.zeros((SUBLANES, D_MODEL), F32),) * 2
        for j in range(ts // rc):
            sums = tail(j, sums)
        lsum, dgsum = sums
        loss_ref[...] = loss_ref[...] + (0.5 / D_MODEL) * jnp.sum(lsum, keepdims=True)
        dg3_ref[...] = dg3_ref[...] + jnp.sum(dgsum, axis=0, keepdims=True)

        for n0 in range(0, D_FF, half):
            da_ref[:, n0:n0 + half] = lax.dot_general(dx3b_ref[...], wd_ref[n0:n0 + half, :], _NT,
                                                      preferred_element_type=F32)

        def back(j, carry):
            r0 = pl.multiple_of(j * rc, rc)
            rows = pl.ds(r0, rc)
            for l0 in range(0, D_FF, LANES):
                da = da_ref[rows, l0:l0 + LANES]
                dv_ref[rows, l0:l0 + LANES] = (da * silu_ref[rows, l0:l0 + LANES]).astype(BF16)
                dgg = da * dsv_ref[rows, l0:l0 + LANES]
                dg_ref[rows, l0:l0 + LANES] = dgg.astype(BF16)
                _conv_wgrad_block(acc_ref, dgg, ge_ref, r0, rc, l0, K_F)
            return carry

        lax.fori_loop(0, ts // rc, back, 0)

        @pl.when(i == nt - 1)
        def _():
            _reduce_acc(dwf_ref, acc_ref, K_F)

    return pl.pallas_call(
        body, name="ffn_out_and_back", grid=(nt,),
        in_specs=[_rows(ts, D_FF), _prev(ts, D_FF), _next(ts, D_FF, s), _rows(ts, D_FF), _rows(ts, D_MODEL),
                  _const((K_F, D_FF)), _const((D_FF, D_MODEL)), _const((1, D_MODEL)), _rows(ts, D_MODEL)],
        out_specs=[_rows(ts, D_FF), _rows(ts, D_MODEL), _rows(ts, D_MODEL), _acc_out((1, 1)), _acc_out((1, D_MODEL)),
                   _rows(ts, D_FF), _rows(ts, D_FF), _acc_out((K_F, D_FF))],
        out_shape=[_sds((s, D_FF), BF16), _sds((s, D_MODEL), F32), _sds((s, D_MODEL), BF16),
                   _sds((1, 1), F32), _sds((1, D_MODEL), F32),
                   _sds((s, D_FF), BF16), _sds((s, D_FF), BF16), _sds((K_F, D_FF), F32)],
        scratch_shapes=[_ext_scratch(ts, D_FF), pltpu.VMEM((ts, D_FF), F32), pltpu.VMEM((ts, D_FF), F32),
                        pltpu.VMEM((ts, D_FF), F32), pltpu.VMEM((K_F, SUBLANES, D_FF), F32)],
        compiler_params=_params(),
    )(g0, g0, g0, v, x2, wf, w_down, g3, target)


def _bwd_ffn_b(dg, dv, wf, w_gate, w_up, x2, g2, dx3, ts):
    s = x2.shape[0]
    nt = s // ts
    rc = min(ROW_CHUNK, ts)
    n_sub = ts // rc

    def body(dg_ref, dgp_ref, dgn_ref, dv_ref, wf_ref, wg_ref, wu_ref, x2_ref, g2_ref, dx3_ref,
             dg0_ref, dx2_ref, dx2b_ref, dgn2_ref, dge_ref, dg8_ref, a0_ref, a1_ref, p0_ref, p1_ref):
        i = pl.program_id(0)
        vt = jnp.minimum(i, nt - 1)
        live = (i >= 2).astype(F32)

        @pl.when(i == 0)
        def _():
            dg8_ref[...] = jnp.zeros_like(dg8_ref)
            a1_ref[...] = jnp.zeros_like(a1_ref)
            p1_ref[...] = jnp.zeros_like(p1_ref)

        _fill_halo(dge_ref, dgp_ref[...].astype(F32), dgn_ref[...].astype(F32), ts, vt == 0, vt == nt - 1)

        def fill(j, carry):
            r0 = pl.multiple_of(j * rc, rc)
            _put_rows(dge_ref, r0, rc, dg_ref[pl.ds(r0, rc), :].astype(F32))
            return carry

        lax.fori_loop(0, n_sub, fill, 0)

        def stage(a_new, a_old, p_new, p_old):
            def conv_t(r0, l0):
                rows = slice(r0, r0 + rc)
                dg0 = _conv_block(dge_ref, wf_ref, r0, rc, l0, K_F, True).astype(BF16)
                dg0_ref[rows, l0:l0 + LANES] = dg0
                a_new[rows, l0:l0 + LANES] = dg0

            def tail(r0):
                rows = slice(r0, r0 + rc)
                dx, dgrow = _rms_bwd_rows(p_old[rows, :], x2_ref[rows, :], g2_ref[...])
                dx2 = dx3_ref[rows, :] + dx
                dx2_ref[rows, :] = dx2
                dx2b_ref[rows, :] = dx2.astype(BF16)
                dg8_ref[...] = dg8_ref[...] + _fold8(dgrow) * live

            units = []
            for q in range(n_sub):
                units += [(conv_t, (q * rc, l0)) for l0 in range(0, D_FF, LANES)]
                units.append((tail, (q * rc,)))
            _interleaved(units, _matmul_pieces([(a_old, wg_ref), (dv_ref, wu_ref)], p_new, 2))

        @pl.when(i % 2 == 0)
        def _():
            stage(a0_ref, a1_ref, p0_ref, p1_ref)

        @pl.when(i % 2 == 1)
        def _():
            stage(a1_ref, a0_ref, p1_ref, p0_ref)

        @pl.when(i == nt + 1)
        def _():
            dgn2_ref[...] = jnp.sum(dg8_ref[...], axis=0, keepdims=True)

    vtile = lambda i: jnp.minimum(i, nt - 1)
    mtile = lambda i: jnp.clip(i - 1, 0, nt - 1)
    ttile = lambda i: jnp.clip(i - 2, 0, nt - 1)
    return pl.pallas_call(
        body, name="bwd_ffn_b", grid=(nt + 2,),
        in_specs=[_rows_at(ts, D_FF, vtile), _prev_at(ts, D_FF, vtile), _next_at(ts, D_FF, s, vtile),
                  _rows_at(ts, D_FF, mtile), _const((K_F, D_FF)),
                  _const((D_FF, D_MODEL)), _const((D_FF, D_MODEL)), _rows_at(ts, D_MODEL, ttile), _const((1, D_MODEL)),
                  _rows_at(ts, D_MODEL, ttile)],
        out_specs=[_rows_at(ts, D_FF, vtile), _rows_at(ts, D_MODEL, ttile), _rows_at(ts, D_MODEL, ttile),
                   _acc_out((1, D_MODEL))],
        out_shape=[_sds((s, D_FF), BF16), _sds((s, D_MODEL), F32), _sds((s, D_MODEL), BF16), _sds((1, D_MODEL), F32)],
        scratch_shapes=[_ext_scratch(ts, D_FF), pltpu.VMEM((SUBLANES, D_MODEL), F32),
                        pltpu.VMEM((ts, D_FF), BF16), pltpu.VMEM((ts, D_FF), BF16),
                        pltpu.VMEM((ts, D_MODEL), F32), pltpu.VMEM((ts, D_MODEL), F32)],
        compiler_params=_params(),
    )(dg, dg, dg, dv, wf, w_gate, w_up, x2, g2, dx3)


def _bwd_mix_a(dx2b, w_out_t, z, u, wa, lg, lb, ts, exchange=None):
    s = dx2b.shape[0]
    nt = s // ts
    rc = min(ROW_CHUNK, ts)
    n_sub = ts // rc

    def body(dx_ref, wo_ref, z_ref, zp_ref, zn_ref, u_ref, wa_ref, lg_ref, lb_ref,
             dca_ref, du_ref, dab_ref, dwa_ref, dlg_ref, dlb_ref, dbb_ref, pe_ref, dy0_ref, dy1_ref, acc_ref, sacc_ref):
        i = pl.program_id(0)
        t = jnp.maximum(i - 1, 0)

        @pl.when(i == 0)
        def _():
            acc_ref[...] = jnp.zeros_like(acc_ref)
            sacc_ref[...] = jnp.zeros_like(sacc_ref)
            dy1_ref[...] = jnp.zeros_like(dy1_ref)

        pp, _ = _p_u0(zp_ref, slice(None))
        pn, _ = _p_u0(zn_ref, slice(None))
        _fill_halo(pe_ref, pp, pn, ts, t == 0, t == nt - 1)

        def fill(j, carry):
            r0 = pl.multiple_of(j * rc, rc)
            rows = pl.ds(r0, rc)
            _put_rows(pe_ref, r0, rc, z_ref[rows, 2 * D_A:3 * D_A].astype(F32) * z_ref[rows, 0:D_A].astype(F32))
            return carry

        lax.fori_loop(0, n_sub, fill, 0)

        def stage(dy_new, dy_old):
            def piece(m0, n0, width):
                dy_new[m0:m0 + MXU_ROWS, n0:n0 + width] = jnp.dot(
                    dx_ref[m0:m0 + MXU_ROWS, :], wo_ref[:, n0:n0 + width], preferred_element_type=F32).astype(BF16)

            units = []
            for q in range(n_sub):
                units += [(mixer_a, (dy_old, q * rc, l0)) for l0 in range(0, D_A, LANES)]
                units.append((mixer_b, (dy_old, q * rc)))
            _interleaved(units, [(piece, (m0, n0, w)) for n0, w in _col_pieces(D_MODEL) for m0 in range(0, ts, MXU_ROWS)])

        def mixer_a(dy_ref, r0, l0):
            rows = slice(r0, r0 + rc)
            ca = _conv_block(pe_ref, wa_ref, r0, rc, l0, K_A, False)
            a_b = z_ref[rows, D_A + l0:D_A + l0 + LANES].astype(F32)
            dya = dy_ref[rows, l0:l0 + LANES].astype(F32)
            dab_ref[rows, l0:l0 + LANES] = (dya * ca).astype(BF16)
            dca = dya * a_b
            dca_ref[rows, l0:l0 + LANES] = dca
            _conv_wgrad_block(acc_ref, dca, pe_ref, r0, rc, l0, K_A)

        def mixer_b(dy_ref, r0):
            rows = slice(r0, r0 + rc)
            ubs = [u_ref[rows, l0:l0 + LANES] for l0 in range(0, D_B, LANES)]
            mu, rstd = _layernorm_rows(ubs)
            ns, dns = [], []
            m1 = None
            m2 = None
            for q, l0 in enumerate(range(0, D_B, LANES)):
                n = (ubs[q] - mu) * rstd
                lgq = lg_ref[:, l0:l0 + LANES]
                t = n * lgq + lb_ref[:, l0:l0 + LANES]
                sg = _sigmoid(t)
                dt = dy_ref[rows, D_A + l0:D_A + l0 + LANES].astype(F32) * (sg * (1.0 + t * (1.0 - sg)))
                dn = dt * lgq
                ns.append(n)
                dns.append(dn)
                s1 = jnp.sum(dn, axis=-1, keepdims=True)
                s2 = jnp.sum(dn * n, axis=-1, keepdims=True)
                m1 = s1 if m1 is None else m1 + s1
                m2 = s2 if m2 is None else m2 + s2
                sacc_ref[0, :, l0:l0 + LANES] = sacc_ref[0, :, l0:l0 + LANES] + _fold8(dt * n)
                sacc_ref[1, :, l0:l0 + LANES] = sacc_ref[1, :, l0:l0 + LANES] + _fold8(dt)
            m1 = m1 * (1.0 / D_B)
            m2 = m2 * (1.0 / D_B)
            for q, l0 in enumerate(range(0, D_B, LANES)):
                du = rstd * (dns[q] - m1 - ns[q] * m2)
                du_ref[rows, l0:l0 + LANES] = du
                sacc_ref[2, :, l0:l0 + LANES] = sacc_ref[2, :, l0:l0 + LANES] + _fold8(du)

        @pl.when(i % 2 == 0)
        def _():
            stage(dy0_ref, dy1_ref)

        @pl.when(i % 2 == 1)
        def _():
            stage(dy1_ref, dy0_ref)

        @pl.when(i == nt)
        def _():
            _reduce_acc(dwa_ref, acc_ref, K_A)
            dlg_ref[...] = jnp.sum(sacc_ref[0], axis=0, keepdims=True)
            dlb_ref[...] = jnp.sum(sacc_ref[1], axis=0, keepdims=True)
            dbb_ref[...] = jnp.sum(sacc_ref[2], axis=0, keepdims=True)

    cur = lambda i: jnp.minimum(i, nt - 1)
    old = lambda i: jnp.maximum(i - 1, 0)
    return _call(
        body, name="bwd_mix_a", grid=(nt + 1,),
        in_specs=[_rows_at(ts, D_MODEL, cur), _const((D_MODEL, D_MODEL)), _rows_at(ts, D_IN, old), _prev_at(ts, D_IN, old),
                  _next_at(ts, D_IN, s, old), _rows_at(ts, D_B, old), _const((K_A, D_A)), _const((1, D_B)), _const((1, D_B))],
        out_specs=[_rows_at(ts, D_A, old), _rows_at(ts, D_B, old), _rows_at(ts, D_A, old), _acc_out((K_A, D_A)),
                   _acc_out((1, D_B)), _acc_out((1, D_B)), _acc_out((1, D_B))],
        out_shape=[_sds((s, D_A), F32), _sds((s, D_B), F32), _sds((s, D_A), BF16), _sds((K_A, D_A), F32),
                   _sds((1, D_B), F32), _sds((1, D_B), F32), _sds((1, D_B), F32)],
        scratch_shapes=[_ext_scratch(ts, D_A), pltpu.VMEM((ts, D_MODEL), BF16), pltpu.VMEM((ts, D_MODEL), BF16),
                        pltpu.VMEM((K_A, SUBLANES, D_A), F32), pltpu.VMEM((3, SUBLANES, D_B), F32)],
        args=(dx2b, w_out_t, z, z, z, u, wa, lg, lb), exchange=exchange)


def _bwd_mix_b(dca, du, z, dab, wa, wb, w_in, x, g1, dx2, ts, exchange=None):
    s = x.shape[0]
    nt = s // ts
    rc = min(ROW_CHUNK, ts)
    n_sub = ts // rc

    def body(dca_ref, dcap_ref, dcan_ref, du_ref, dup_ref, dun_ref, z_ref, zp_ref, zn_ref, dab_ref,
             wa_ref, wb_ref, wi_ref, x_ref, g1_ref, dx2_ref,
             dz_ref, dx_ref, dg1_ref, dwb_ref, dcae_ref, due_ref, ue_ref, acc_ref, dg8_ref,
             dz0_ref, dz1_ref, dh0_ref, dh1_ref):
        i = pl.program_id(0)

        @pl.when(i == 0)
        def _():
            acc_ref[...] = jnp.zeros_like(acc_ref)
            dg8_ref[...] = jnp.zeros_like(dg8_ref)
            dz1_ref[...] = jnp.zeros_like(dz1_ref)
            dh1_ref[...] = jnp.zeros_like(dh1_ref)

        vt = jnp.minimum(i, nt - 1)
        first = vt == 0
        last = vt == nt - 1
        live = (i < nt).astype(F32)
        _fill_halo(dcae_ref, dcap_ref[...], dcan_ref[...], ts, first, last)
        _fill_halo(due_ref, dup_ref[...], dun_ref[...], ts, first, last)
        _, up = _p_u0(zp_ref, slice(None))
        _, un = _p_u0(zn_ref, slice(None))
        _fill_halo(ue_ref, up, un, ts, first, last)

        def fill(j, carry):
            r0 = pl.multiple_of(j * rc, rc)
            rows = pl.ds(r0, rc)
            _put_rows(dcae_ref, r0, rc, dca_ref[rows, :])
            _put_rows(due_ref, r0, rc, du_ref[rows, :])
            b_v = z_ref[rows, 3 * D_A:3 * D_A + D_B].astype(F32)
            b_g = z_ref[rows, 3 * D_A + D_B:D_IN].astype(F32)
            _put_rows(ue_ref, r0, rc, b_v * _sigmoid(b_g))
            return carry

        lax.fori_loop(0, n_sub, fill, 0)

        def stage(dz_new, dz_old, dh_new, dh_old):
            def put(rows, c0, val):
                dz_ref[rows, c0:c0 + LANES] = val
                dz_new[rows, c0:c0 + LANES] = val

            def mixer_a(r0, l0):
                rows = slice(r0, r0 + rc)
                dp = _conv_block(dcae_ref, wa_ref, r0, rc, l0, K_A, True)
                a_h = z_ref[rows, l0:l0 + LANES].astype(F32)
                a_c = z_ref[rows, 2 * D_A + l0:2 * D_A + l0 + LANES].astype(F32)
                put(rows, l0, (dp * a_c).astype(BF16))
                put(rows, D_A + l0, dab_ref[rows, l0:l0 + LANES])
                put(rows, 2 * D_A + l0, (dp * a_h).astype(BF16))

            def mixer_b(r0, l0):
                rows = slice(r0, r0 + rc)
                du0 = _conv_block(due_ref, wb_ref, r0, rc, l0, K_B, True)
                b_v = z_ref[rows, 3 * D_A + l0:3 * D_A + l0 + LANES].astype(F32)
                b_g = z_ref[rows, 3 * D_A + D_B + l0:3 * D_A + D_B + l0 + LANES].astype(F32)
                sg = _sigmoid(b_g)
                put(rows, 3 * D_A + l0, (du0 * sg).astype(BF16))
                put(rows, 3 * D_A + D_B + l0, (du0 * b_v * (sg * (1.0 - sg))).astype(BF16))
                _conv_wgrad_block(acc_ref, du_ref[rows, l0:l0 + LANES], ue_ref, r0, rc, l0, K_B, live)

            def tail(r0):
                rows = slice(r0, r0 + rc)
                dx, dgrow = _rms_bwd_rows(dh_old[rows, :], x_ref[rows, :], g1_ref[...])
                dx_ref[rows, :] = dx2_ref[rows, :] + dx
                dg8_ref[...] = dg8_ref[...] + _fold8(dgrow)

            units = []
            for q in range(n_sub):
                units += [(mixer_a, (q * rc, l0)) for l0 in range(0, D_A, LANES)]
                units += [(mixer_b, (q * rc, l0)) for l0 in range(0, D_B, LANES)]
                units.append((tail, (q * rc,)))
            _interleaved(units, _matmul_pieces([(dz_old, wi_ref)], dh_new, 2))

        @pl.when(i % 2 == 0)
        def _():
            stage(dz0_ref, dz1_ref, dh0_ref, dh1_ref)

        @pl.when(i % 2 == 1)
        def _():
            stage(dz1_ref, dz0_ref, dh1_ref, dh0_ref)

        @pl.when(i == nt + 1)
        def _():
            _reduce_acc(dwb_ref, acc_ref, K_B)
            dg1_ref[...] = jnp.sum(dg8_ref[...], axis=0, keepdims=True)

    vtile = lambda i: jnp.minimum(i, nt - 1)
    ttile = lambda i: jnp.clip(i - 2, 0, nt - 1)
    return _call(
        body, name="bwd_mix_b", grid=(nt + 2,),
        in_specs=[_rows_at(ts, D_A, vtile), _prev_at(ts, D_A, vtile), _next_at(ts, D_A, s, vtile),
                  _rows_at(ts, D_B, vtile), _prev_at(ts, D_B, vtile), _next_at(ts, D_B, s, vtile),
                  _rows_at(ts, D_IN, vtile), _prev_at(ts, D_IN, vtile), _next_at(ts, D_IN, s, vtile), _rows_at(ts, D_A, vtile),
                  _const((K_A, D_A)), _const((K_B, D_B)), _const((D_IN, D_MODEL)), _rows_at(ts, D_MODEL, ttile),
                  _const((1, D_MODEL)), _rows_at(ts, D_MODEL, ttile)],
        out_specs=[_rows_at(ts, D_IN, vtile), _rows_at(ts, D_MODEL, ttile), _acc_out((1, D_MODEL)), _acc_out((K_B, D_B))],
        out_shape=[_sds((s, D_IN), BF16), _sds((s, D_MODEL), F32), _sds((1, D_MODEL), F32), _sds((K_B, D_B), F32)],
        scratch_shapes=[_ext_scratch(ts, D_A), _ext_scratch(ts, D_B), _ext_scratch(ts, D_B),
                        pltpu.VMEM((K_B, SUBLANES, D_B), F32), pltpu.VMEM((SUBLANES, D_MODEL), F32),
                        pltpu.VMEM((ts, D_IN), BF16), pltpu.VMEM((ts, D_IN), BF16),
                        pltpu.VMEM((ts, D_MODEL), F32), pltpu.VMEM((ts, D_MODEL), F32)],
        args=(dca, dca, dca, du, du, du, z, z, z, dab, wa, wb, w_in, x, g1, dx2), exchange=exchange)


def _matmul_tn(a, b, name, exchange=None):
    s, m = a.shape
    n = b.shape[1]
    tk = min(1024, s)
    nk = s // tk
    tm = 256

    def body(a_ref, b_ref, o_ref, acc_ref):
        k = pl.program_id(0)

        @pl.when(k == 0)
        def _():
            acc_ref[...] = jnp.zeros_like(acc_ref)

        for m0 in range(0, m, tm):
            acc_ref[m0:m0 + tm, :] = acc_ref[m0:m0 + tm, :] + lax.dot_general(
                a_ref[:, m0:m0 + tm], b_ref[...], _TN, preferred_element_type=F32)

        @pl.when(k == nk - 1)
        def _():
            o_ref[...] = acc_ref[...].astype(BF16)

    (out,), got = _call(
        body, name=name, grid=(nk,),
        in_specs=[_rows(tk, m), _rows(tk, n)],
        out_specs=[_acc_out((m, n))],
        out_shape=[_sds((m, n), BF16)],
        scratch_shapes=[pltpu.VMEM((m, n), F32)], args=(a, b), exchange=exchange)
    return out if exchange is None else (out, got)


CHIP_RELS = ((1, 0, 0), (0, 1, 0), (1, 1, 0))
CORE_RELS = ((0, 0, 1),)
ALL_RELS = ((0, 0, 1), (0, 1, 0), (0, 1, 1), (1, 0, 0), (1, 0, 1), (1, 1, 0), (1, 1, 1))


def _chip_slot(dev):
    return 2 * dev[0] + dev[1]


def _dev_slot(dev):
    return 4 * dev[0] + 2 * dev[1] + dev[2]


def _me():
    return (lax.axis_index("x"), lax.axis_index("y"), lax.axis_index("c"))


def _peer(me, rel):
    return tuple((1 - me[a]) if rel[a] else me[a] for a in range(3))


_ANY = pl.BlockSpec(memory_space=pl.ANY)


class _Exchange:
    def __init__(self, inputs, out_shape, scratch, start, finish, forward=None):
        self.inputs, self.out_shape, self.scratch = list(inputs), list(out_shape), list(scratch)
        self.start, self.finish, self.forward = start, finish, forward


def _all_gather(payloads):
    n_p = len(payloads)
    n_k = 1 + 2 * len(CHIP_RELS)

    def copy(srcs, dsts, sems, p, k, block_dev, to, from_src):
        blk = dsts[p].at[_dev_slot(block_dev)]
        return pltpu.make_async_remote_copy(
            src_ref=srcs[p] if from_src else blk, dst_ref=blk,
            send_sem=sems[0].at[n_k * p + k], recv_sem=sems[1].at[n_k * p + k], device_id=to, device_id_type=MESH)

    def own_copy(srcs, dsts, sems, p):
        return pltpu.make_async_copy(srcs[p], dsts[p].at[_dev_slot(_me())], sems[2].at[p])

    def start(srcs, dsts, sems):
        me = _me()
        for p in range(n_p):
            own_copy(srcs, dsts, sems, p).start()
        for j, rel in enumerate(CHIP_RELS):
            for p in range(n_p):
                copy(srcs, dsts, sems, p, 1 + j, me, _peer(me, rel), True).start()
        for p in range(n_p):
            copy(srcs, dsts, sems, p, 0, me, _peer(me, CORE_RELS[0]), True).start()

    def forward(srcs, dsts, sems):
        me = _me()
        sibling = _peer(me, CORE_RELS[0])
        for j, rel in enumerate(CHIP_RELS):
            other = _peer(me, rel)
            for p in range(n_p):
                copy(srcs, dsts, sems, p, 1 + j, other, me, False).wait_recv()
                copy(srcs, dsts, sems, p, 4 + j, other, sibling, False).start()

    def finish(srcs, dsts, sems):
        me = _me()
        sibling = _peer(me, CORE_RELS[0])
        for p in range(n_p):
            copy(srcs, dsts, sems, p, 0, sibling, me, False).wait_recv()
        for j, rel in enumerate(CHIP_RELS):
            for p in range(n_p):
                copy(srcs, dsts, sems, p, 4 + j, _peer(sibling, rel), me, False).wait_recv()
        for p in range(n_p):
            own_copy(srcs, dsts, sems, p).wait()
            copy(srcs, dsts, sems, p, 0, me, sibling, True).wait_send()
            for j, rel in enumerate(CHIP_RELS):
                copy(srcs, dsts, sems, p, 1 + j, me, _peer(me, rel), True).wait_send()
                copy(srcs, dsts, sems, p, 4 + j, _peer(me, rel), sibling, False).wait_send()

    return _Exchange(
        payloads, [_sds((N_DEV,) + p.shape, p.dtype) for p in payloads],
        [pltpu.SemaphoreType.DMA((n_p * n_k,)), pltpu.SemaphoreType.DMA((n_p * n_k,)), pltpu.SemaphoreType.DMA((n_p,))],
        start, finish, forward)


def _gather_direct(payload):
    n_r = len(ALL_RELS)

    def copies(srcs, dsts, sems):
        me = _me()
        mine = dsts[0].at[_dev_slot(me)]
        own = pltpu.make_async_copy(srcs[0], mine, sems[2].at[0])
        remote = [pltpu.make_async_remote_copy(src_ref=srcs[0], dst_ref=mine, send_sem=sems[0].at[k], recv_sem=sems[1].at[k],
                                               device_id=_peer(me, rel), device_id_type=MESH)
                  for k, rel in enumerate(ALL_RELS)]
        return [own] + remote

    def start(srcs, dsts, sems):
        for cp in copies(srcs, dsts, sems):
            cp.start()

    def finish(srcs, dsts, sems):
        for cp in copies(srcs, dsts, sems):
            cp.wait()

    return _Exchange([payload], [_sds((N_DEV,) + payload.shape, payload.dtype)],
                     [pltpu.SemaphoreType.DMA((n_r,)), pltpu.SemaphoreType.DMA((n_r,)), pltpu.SemaphoreType.DMA((1,))],
                     start, finish)


def _scatter_exchange(payloads, rels, src_view, view_shapes):
    n_p = len(payloads)
    n_r = len(rels)

    def copies(srcs, dsts, sems):
        me = _me()
        out = []
        for k, rel in enumerate(rels):
            peer = _peer(me, rel)
            for p in range(n_p):
                out.append(pltpu.make_async_remote_copy(
                    src_ref=src_view(srcs[p], peer), dst_ref=dsts[p].at[k],
                    send_sem=sems[0].at[p * n_r + k], recv_sem=sems[1].at[p * n_r + k],
                    device_id=peer, device_id_type=MESH))
        return out

    def start(srcs, dsts, sems):
        for cp in copies(srcs, dsts, sems):
            cp.start()

    def finish(srcs, dsts, sems):
        for cp in copies(srcs, dsts, sems):
            cp.wait()

    return _Exchange(payloads, [_sds((n_r,) + vs, p.dtype) for vs, p in zip(view_shapes, payloads)],
                     [pltpu.SemaphoreType.DMA((n_p * n_r,)), pltpu.SemaphoreType.DMA((n_p * n_r,))], start, finish)


def _split_refs(refs, sizes):
    out, at = [], 0
    for n in sizes:
        out.append(refs[at:at + n])
        at += n
    return out


def _run_exchanges(name, exchanges):
    n_in = [len(e.inputs) for e in exchanges]
    n_out = [len(e.out_shape) for e in exchanges]
    n_sc = [len(e.scratch) for e in exchanges]

    def body(*refs):
        ins, outs, scs = _split_refs(refs, [sum(n_in), sum(n_out), sum(n_sc)])
        parts = list(zip(exchanges, _split_refs(ins, n_in), _split_refs(outs, n_out), _split_refs(scs, n_sc)))
        for e, i, o, s in parts:
            e.start(i, o, s)
        for e, i, o, s in parts:
            if e.forward is not None:
                e.forward(i, o, s)
        for e, i, o, s in parts:
            e.finish(i, o, s)

    outs = pl.pallas_call(
        body, name=name, in_specs=[_ANY] * sum(n_in), out_specs=[_ANY] * sum(n_out),
        out_shape=[sd for e in exchanges for sd in e.out_shape],
        scratch_shapes=[sc for e in exchanges for sc in e.scratch],
    )(*[a for e in exchanges for a in e.inputs])
    return _split_refs(list(outs), n_out)


def _call(body, *, name, grid, in_specs, out_specs, out_shape, scratch_shapes, args, exchange=None, forward_step=None):
    n_in, n_out, n_sc = len(in_specs), len(out_specs), len(scratch_shapes)
    if exchange is None:
        outs = pl.pallas_call(body, name=name, grid=grid, in_specs=in_specs, out_specs=out_specs, out_shape=out_shape,
                              scratch_shapes=scratch_shapes, compiler_params=_params())(*args)
        return list(outs), []
    e = exchange
    sizes = [n_in, len(e.inputs), n_out, len(e.out_shape), n_sc, len(e.scratch)]
    last = grid[0] - 1

    def wrapped(*refs):
        a, ei, o, eo, sc, es = _split_refs(refs, sizes)
        i = pl.program_id(0)

        @pl.when(i == 0)
        def _():
            e.start(ei, eo, es)

        if e.forward is not None:
            @pl.when(i == forward_step)
            def _():
                e.forward(ei, eo, es)

        body(*a, *o, *sc)

        @pl.when(i == last)
        def _():
            e.finish(ei, eo, es)

    outs = pl.pallas_call(
        wrapped, name=name, grid=grid,
        in_specs=list(in_specs) + [_ANY] * len(e.inputs), out_specs=list(out_specs) + [_ANY] * len(e.out_shape),
        out_shape=list(out_shape) + e.out_shape, scratch_shapes=list(scratch_shapes) + e.scratch,
        compiler_params=_params(),
    )(*args, *e.inputs)
    outs = list(outs)
    return outs[:n_out], outs[n_out:]


def _pair_sum(grads, recvd, my_core, name):
    n_p = len(grads)

    def body(c_ref, *refs):
        del c_ref
        for p in range(n_p):
            refs[2 * n_p + p][...] = (refs[p][...].astype(F32) + refs[n_p + p][...].astype(F32)).astype(BF16)

    def blk(g):
        return (None, None) + g.shape[2:]

    return pl.pallas_call(
        body, name=name,
        grid_spec=pltpu.PrefetchScalarGridSpec(
            num_scalar_prefetch=1, grid=(N_CHIP,),
            in_specs=[pl.BlockSpec(blk(g), lambda j, c: (j, c[0], 0, 0)) for g in grads]
            + [pl.BlockSpec(blk(g), lambda j, c: (0, j, 0, 0)) for g in grads],
            out_specs=[pl.BlockSpec((None,) + g.shape[2:], lambda j, c: (j, 0, 0)) for g in grads]),
        out_shape=[_sds((N_CHIP,) + g.shape[2:], BF16) for g in grads],
        compiler_params=pltpu.CompilerParams(dimension_semantics=("arbitrary",), vmem_limit_bytes=VMEM_LIMIT),
    )(my_core, *grads, *recvd)


def _chip_sum(psums, recvd, my_chip, name):
    n_p = len(psums)

    def body(c_ref, *refs):
        del c_ref
        for p in range(n_p):
            acc = refs[p][...].astype(F32)
            for k in range(len(CHIP_RELS)):
                acc = acc + refs[n_p + p][k].astype(F32)
            refs[2 * n_p + p][...] = acc

    return pl.pallas_call(
        body, name=name,
        grid_spec=pltpu.PrefetchScalarGridSpec(
            num_scalar_prefetch=1, grid=(1,),
            in_specs=[pl.BlockSpec((None,) + g.shape[1:], lambda i, c: (c[0], 0, 0)) for g in psums]
            + [pl.BlockSpec(r.shape, lambda i, c: (0, 0, 0)) for r in recvd],
            out_specs=[pl.BlockSpec(g.shape[1:], lambda i, c: (0, 0)) for g in psums]),
        out_shape=[_sds(g.shape[1:], F32) for g in psums],
        compiler_params=pltpu.CompilerParams(dimension_semantics=("arbitrary",), vmem_limit_bytes=VMEM_LIMIT),
    )(my_chip, *psums, *recvd)


def _sum_devices(parts, shapes):
    plan = _small_rows(shapes)

    def body(p_ref, *refs):
        outs, tot_ref = refs[:-1], refs[-1]
        acc = p_ref[0]
        for j in range(1, N_DEV):
            acc = acc + p_ref[j]
        tot_ref[...] = acc
        for idx, r, c0, width, at in plan:
            outs[idx][r:r + 1, c0:c0 + width] = tot_ref[at:at + 1, 0:width]

    return pl.pallas_call(body, name="small_grad_sum", out_shape=[_sds(s, F32) for s in shapes],
                          scratch_shapes=[pltpu.VMEM(parts.shape[1:], F32)])(parts)


def _cast_shards(shards):
    def body(*refs):
        for src, dst in zip(refs[:len(shards)], refs[len(shards):]):
            dst[...] = src[...].astype(BF16)

    return pl.pallas_call(body, name="cast_shards", out_shape=[_sds(a.shape, BF16) for a in shards],
                          compiler_params=pltpu.CompilerParams(vmem_limit_bytes=VMEM_LIMIT))(*shards)


def _adamw(ws, gs, ms, vs, name):
    n_t = len(ws)

    def body(*refs):
        w_refs, g_refs, m_refs, v_refs = (refs[j * n_t:(j + 1) * n_t] for j in range(4))
        outs = refs[4 * n_t:]
        for k in range(n_t):
            gg = g_refs[k][...]
            mn = ADAM_B1 * m_refs[k][...] + (1.0 - ADAM_B1) * gg
            vn = ADAM_B2 * v_refs[k][...] + (1.0 - ADAM_B2) * (gg * gg)
            m_hat = mn / (1.0 - ADAM_B1 ** ADAM_STEP)
            v_hat = vn / (1.0 - ADAM_B2 ** ADAM_STEP)
            outs[3 * k][...] = -ADAM_LR * (m_hat / (jnp.sqrt(v_hat) + ADAM_EPS) + ADAM_WD * w_refs[k][...])
            outs[3 * k + 1][...] = mn
            outs[3 * k + 2][...] = vn

    out_shape = [_sds(w.shape, F32) for w in ws for _ in range(3)]
    return pl.pallas_call(body, name=name, out_shape=out_shape,
                          compiler_params=pltpu.CompilerParams(vmem_limit_bytes=VMEM_LIMIT))(*ws, *gs, *ms, *vs)


class _Mesh:
    def __init__(self, shards, my_chip, my_core):
        self.shards, self.my_chip, self.my_core = shards, my_chip.reshape(1), my_core.reshape(1)

    def gather(self, names):
        return _all_gather([self.shards[n] for n in names])

    @staticmethod
    def whole(gathered):
        return gathered.reshape(N_DEV * gathered.shape[1], gathered.shape[2])

    @staticmethod
    def by_device(grads):
        return [g.reshape(N_CHIP, 2, g.shape[0] // N_DEV, g.shape[1]) for g in grads]

    @staticmethod
    def to_sibling(parts):
        return _scatter_exchange(parts, CORE_RELS, lambda ref, peer: ref.at[:, peer[2]],
                                 [(N_CHIP,) + p.shape[2:] for p in parts])

    @staticmethod
    def to_chips(pair):
        return _scatter_exchange(pair, CHIP_RELS, lambda ref, peer: ref.at[_chip_slot(peer)], [p.shape[1:] for p in pair])


def _step(x, target, g1, w_in_t, wa, wb, bb, lg, lb, w_out, g2, w_gate_t, w_up_t, wf, w_down, g3, ts, mesh=None):
    (z, h1), got = _fwd_in(x, g1, w_in_t, ts, exchange=mesh and mesh.gather(["w_out"]))
    if mesh:
        w_out = mesh.whole(got[0])
    (x2, y, u), got = _fwd_mix(z, x, wa, wb, bb, lg, lb, w_out, ts, exchange=mesh and mesh.gather(["w_gate", "w_up"]))
    if mesh:
        w_gate_t, w_up_t = [mesh.whole(g) for g in got]
    (g0, v, h2), got = _fwd_ffn_in(x2, g2, w_gate_t, w_up_t, ts, exchange=mesh and mesh.gather(["w_down"]))
    if mesh:
        w_down = mesh.whole(got[0])
    tk = min(ts, SKEW_TILE)
    a, dx3, dx3b, loss, dg3, dgc, dv, dwf = _ffn_out_and_back(g0, v, x2, wf, w_down, g3, target, tk)
    dg0, dx2, dx2b, dg2 = _bwd_ffn_b(dgc, dv, wf, w_gate_t, w_up_t, x2, g2, dx3, tk)
    first = dict(w_down=_matmul_tn(a, dx3b, "wgrad_down"), w_gate=_matmul_tn(dg0, h2, "wgrad_gate"),
                 w_up=_matmul_tn(dv, h2, "wgrad_up"), w_out=_matmul_tn(y, dx2b, "wgrad_out"))
    if mesh:
        parts = mesh.by_device(list(first.values()))
    (dca, du, dab, dwa, dlg, dlb, dbb), got = _bwd_mix_a(dx2b, jnp.swapaxes(w_out, 0, 1), z, u, wa, lg, lb, tk,
                                                         exchange=mesh and mesh.to_sibling(parts))
    if mesh:
        pair = _pair_sum(parts, got, mesh.my_core, "rs_pair_sum_first")
    (dz, dx, dg1, dwb), got = _bwd_mix_b(dca, du, z, dab, wa, wb, w_in_t, x, g1, dx2, tk,
                                         exchange=mesh and mesh.to_chips(pair))
    small = dict(norm_mix_g=dg1, conv_a_w=dwa, conv_b_w=dwb, conv_b_b=dbb, ln_b_g=dlg, ln_b_b=dlb,
                 norm_ffn_g=dg2, conv_ffn_w=dwf, norm_final_g=dg3)
    if not mesh:
        return loss, dx, dict(w_in=_matmul_tn(dz, h1, "wgrad_in"), **first), small
    big = dict(zip(first, _chip_sum(pair, got, mesh.my_chip, "rs_chip_sum_first")))
    dw_in_t, (every,) = _matmul_tn(dz, h1, "wgrad_in", exchange=_gather_direct(_pack_small_grads(small, loss)))
    parts = mesh.by_device([dw_in_t])
    (got,) = _run_exchanges("rs_cores_last", [mesh.to_sibling(parts)])
    pair = _pair_sum(parts, got, mesh.my_core, "rs_pair_sum_last")
    (got,) = _run_exchanges("rs_chips_last", [mesh.to_chips(pair)])
    (big["w_in"],) = _chip_sum(pair, got, mesh.my_chip, "rs_chip_sum_last")
    shapes = [loss.shape if n == "loss" else small[n].shape for n in _SMALL_NAMES]
    return None, dx, big, dict(zip(_SMALL_NAMES, _sum_devices(every, shapes)))


def _pack_small_weights(conv_a_s, conv_b_s, conv_ffn_s):
    def body(a_ref, b_ref, f_ref, out):
        out[...] = jnp.zeros_like(out)
        at = 0
        for src in (a_ref, b_ref, f_ref):
            rows, cols = src.shape
            for r in range(rows):
                out[at:at + 1, 0:cols] = src[r:r + 1, :]
                at += 1

    return pl.pallas_call(body, name="small_weight_pack", out_shape=_sds((SMALL_W_ROWS, SMALL_W_COLS), F32))(
        conv_a_s, conv_b_s, conv_ffn_s)


def _unpack_small_weights(full):
    def take(r0, k, w):
        return jnp.transpose(full[:, r0:r0 + k, 0:w], (1, 0, 2)).reshape(k, N_DEV * w)

    return take(0, K_A, CONV_A_COLS), take(K_A, K_B, CONV_A_COLS), take(K_A + K_B, K_F, W_FF_COLS)


_SMALL_NAMES = ("conv_b_w", "conv_a_w", "conv_ffn_w", "norm_mix_g", "norm_ffn_g", "norm_final_g",
                "conv_b_b", "ln_b_g", "ln_b_b", "loss")
SMALL_G_ROWS = 64


def _small_rows(shapes):
    plan, at = [], 0
    for idx, (rows, cols) in enumerate(shapes):
        for r in range(rows):
            for c0 in range(0, cols, SMALL_G_COLS):
                plan.append((idx, r, c0, min(SMALL_G_COLS, cols - c0), at))
                at += 1
    assert at <= SMALL_G_ROWS
    return plan


def _pack_small_grads(small, loss):
    srcs = [loss if n == "loss" else small[n] for n in _SMALL_NAMES]
    plan = _small_rows([a.shape for a in srcs])

    def body(*refs):
        out = refs[-1]
        out[...] = jnp.zeros_like(out)
        for idx, r, c0, width, at in plan:
            out[at:at + 1, 0:width] = refs[idx][r:r + 1, c0:c0 + width]

    return pl.pallas_call(body, name="small_grad_pack", out_shape=_sds((SMALL_G_ROWS, SMALL_G_COLS), F32))(*srcs)


def kernel(x, norm_mix_g, w_in, conv_a_w, conv_b_w, conv_b_b, ln_b_g, ln_b_b, w_out, norm_ffn_g, w_gate, w_up, conv_ffn_w, w_down, norm_final_g, loss_target, m_norm_mix_g, m_w_in, m_conv_a_w, m_conv_b_w, m_conv_b_b, m_ln_b_g, m_ln_b_b, m_w_out, m_norm_ffn_g, m_w_gate, m_w_up, m_conv_ffn_w, m_w_down, m_norm_final_g, v_norm_mix_g, v_w_in, v_conv_a_w, v_conv_b_w, v_conv_b_b, v_ln_b_g, v_ln_b_b, v_w_out, v_norm_ffn_g, v_w_gate, v_w_up, v_conv_ffn_w, v_w_down, v_norm_final_g):
    ix, iy, ic = lax.axis_index("x"), lax.axis_index("y"), lax.axis_index("c")
    my_chip = (2 * ix + iy).astype(jnp.int32)
    my_core = ic.astype(jnp.int32)
    my_dev = 2 * my_chip + my_core

    weights = dict(norm_mix_g=norm_mix_g, w_in=w_in, conv_a_w=conv_a_w, conv_b_w=conv_b_w, conv_b_b=conv_b_b,
                   ln_b_g=ln_b_g, ln_b_b=ln_b_b, w_out=w_out, norm_ffn_g=norm_ffn_g, w_gate=w_gate, w_up=w_up,
                   conv_ffn_w=conv_ffn_w, w_down=w_down, norm_final_g=norm_final_g)
    m_in = dict(norm_mix_g=m_norm_mix_g, w_in=m_w_in, conv_a_w=m_conv_a_w, conv_b_w=m_conv_b_w, conv_b_b=m_conv_b_b,
                ln_b_g=m_ln_b_g, ln_b_b=m_ln_b_b, w_out=m_w_out, norm_ffn_g=m_norm_ffn_g, w_gate=m_w_gate,
                w_up=m_w_up, conv_ffn_w=m_conv_ffn_w, w_down=m_w_down, norm_final_g=m_norm_final_g)
    v_in = dict(norm_mix_g=v_norm_mix_g, w_in=v_w_in, conv_a_w=v_conv_a_w, conv_b_w=v_conv_b_w, conv_b_b=v_conv_b_b,
                ln_b_g=v_ln_b_g, ln_b_b=v_ln_b_b, w_out=v_w_out, norm_ffn_g=v_norm_ffn_g, w_gate=v_w_gate,
                w_up=v_w_up, conv_ffn_w=v_conv_ffn_w, w_down=v_w_down, norm_final_g=v_norm_final_g)
    order = list(weights)
    big_names = ("w_in", "w_gate", "w_up", "w_out", "w_down")
    transposed = ("w_in", "w_gate", "w_up")

    def shard2d(name, a):
        if name in transposed:
            return jnp.swapaxes(a[0], 0, 1)
        return a.reshape(1, a.shape[0]) if a.ndim == 1 else a.reshape(a.shape[-2:])

    def unshard2d(name, a2, like):
        if name in transposed:
            return jnp.swapaxes(a2, 0, 1)[None]
        return a2.reshape(like.shape)

    mesh = _Mesh(dict(zip(big_names, _cast_shards([shard2d(n, weights[n]) for n in big_names]))), my_chip, my_core)
    gathered, = _run_exchanges("ag_first", [_all_gather(
        [mesh.shards["w_in"], _pack_small_weights(conv_a_w[0], conv_b_w[0], conv_ffn_w[0])])])
    w_in_t = mesh.whole(gathered[0])
    wa_f, wb_f, wf_f = _unpack_small_weights(gathered[1])

    _, dx, gsum, stot = _step(
        x[0], loss_target[0], norm_mix_g, w_in_t, wa_f, wb_f, conv_b_b, ln_b_g, ln_b_b, None, norm_ffn_g,
        None, None, wf_f, None, norm_final_g.reshape(1, D_MODEL), SEQ_TILE, mesh)

    grads2d = dict(
        norm_mix_g=stot["norm_mix_g"],
        conv_a_w=lax.dynamic_slice(stot["conv_a_w"], (0, my_dev * CONV_A_COLS), (K_A, CONV_A_COLS)),
        conv_b_w=lax.dynamic_slice(stot["conv_b_w"], (0, my_dev * CONV_A_COLS), (K_B, CONV_A_COLS)),
        conv_b_b=stot["conv_b_b"], ln_b_g=stot["ln_b_g"], ln_b_b=stot["ln_b_b"],
        norm_ffn_g=stot["norm_ffn_g"],
        conv_ffn_w=lax.dynamic_slice(stot["conv_ffn_w"], (0, my_dev * W_FF_COLS), (K_F, W_FF_COLS)),
        norm_final_g=stot["norm_final_g"],
        **gsum,
    )

    updates = {}
    small_names = [n for n in order if n not in big_names]
    for group, label in [([n], "adamw_" + n) for n in big_names] + [(small_names, "adamw_small")]:
        outs = _adamw([shard2d(n, weights[n]) for n in group], [grads2d[n] for n in group],
                      [shard2d(n, m_in[n]) for n in group], [shard2d(n, v_in[n]) for n in group], label)
        for k, n in enumerate(group):
            updates[n] = outs[3 * k:3 * k + 3]
    g_out = [unshard2d(n, grads2d[n], weights[n]) for n in order]
    d_out, m_out, v_out = [[unshard2d(n, updates[n][j], weights[n]) for n in order] for j in range(3)]

    return (stot["loss"][0, 0], dx[None], *g_out, *d_out, *m_out, *v_out)
```

```python
import jax
import jax.numpy as jnp
from jax import lax
from jax.experimental import pallas as pl
from jax.experimental.pallas import tpu as pltpu

F32 = jnp.float32
BF16 = jnp.bfloat16

D_MODEL = 1024
D_A = 512
D_B = 512
D_IN = 3 * D_A + 2 * D_B
D_FF = 2816
K_A = 3
K_B = 31
K_F = 3
RMS_EPS = 1e-6
LN_EPS = 1e-5

ADAM_LR = 0.001
ADAM_B1 = 0.9
ADAM_B2 = 0.999
ADAM_EPS = 1e-08
ADAM_WD = 0.01
ADAM_STEP = 10

N_DEV = 8
N_CHIP = 4
LANES = 128
SUBLANES = 8
HALO = 16
ROW_CHUNK = 64
SEQ_TILE = 512
SKEW_TILE = 256
VMEM_LIMIT = 56 * 1024 * 1024

MESH = pl.DeviceIdType.MESH

W_FF_COLS = D_FF // N_DEV
CONV_A_COLS = D_A // N_DEV
SMALL_W_ROWS = 40
SMALL_W_COLS = 384
SMALL_G_COLS = 512
FF_PAD = 3072


def _rows(ts, c):
    return pl.BlockSpec((ts, c), lambda i: (i, 0))


def _const(shape):
    return pl.BlockSpec(shape, lambda i: (0,) * len(shape), pipeline_mode=pl.Buffered(1))


def _acc_out(shape):
    return pl.BlockSpec(shape, lambda i: (0,) * len(shape))


def _rows_at(ts, c, tile):
    return pl.BlockSpec((ts, c), lambda i: (tile(i), 0))


def _prev_at(ts, c, tile):
    return pl.BlockSpec((HALO, c), lambda i: (jnp.maximum(tile(i) * (ts // HALO) - 1, 0), 0))


def _next_at(ts, c, s, tile):
    last = s // HALO - 1
    return pl.BlockSpec((HALO, c), lambda i: (jnp.minimum((tile(i) + 1) * (ts // HALO), last), 0))


def _prev(ts, c):
    return _prev_at(ts, c, lambda i: i)


def _next(ts, c, s):
    return _next_at(ts, c, s, lambda i: i)


MXU_COLS = 256
MXU_ROWS = 256


def _col_pieces(n):
    return [(c0, min(MXU_COLS, n - c0)) for c0 in range(0, n, MXU_COLS)]


def _matmul_pieces(terms, out_ref, k_parts):
    m, n = out_ref.shape
    steps = []
    for lhs_ref, w_ref in terms:
        tiles = lhs_ref.shape[1] // MXU_COLS
        cuts = [MXU_COLS * (tiles * j // k_parts) for j in range(k_parts)] + [lhs_ref.shape[1]]
        steps += [(lhs_ref, w_ref, cuts[j], cuts[j + 1]) for j in range(k_parts)]

    def piece(m0, n0, width, step):
        lhs_ref, w_ref, k0, k1 = steps[step]
        part = jnp.dot(lhs_ref[m0:m0 + MXU_ROWS, k0:k1], w_ref[k0:k1, n0:n0 + width], preferred_element_type=F32)
        if step:
            part = part + out_ref[m0:m0 + MXU_ROWS, n0:n0 + width]
        out_ref[m0:m0 + MXU_ROWS, n0:n0 + width] = part

    return [(piece, (m0, n0, w, j)) for j in range(len(steps)) for n0, w in _col_pieces(n) for m0 in range(0, m, MXU_ROWS)]


def _interleaved(vector_units, matmul_pieces):
    n_u, n_p = len(vector_units), len(matmul_pieces)
    done = 0
    for k, (unit, args) in enumerate(vector_units):
        while done < n_p and done * n_u <= k * n_p:
            matmul_pieces[done][0](*matmul_pieces[done][1])
            done += 1
        unit(*args)
    for fn, args in matmul_pieces[done:]:
        fn(*args)


def _params():
    return pltpu.CompilerParams(dimension_semantics=("arbitrary",), vmem_limit_bytes=VMEM_LIMIT)


def _sds(shape, dtype):
    return jax.ShapeDtypeStruct(shape, dtype)


def _sigmoid(v):
    return 0.5 * jnp.tanh(0.5 * v) + 0.5


def _conv_block(ext_ref, w_ref, r0, rc, l0, k_taps, transposed):
    acc = None
    for k in range(k_taps):
        d = (k_taps // 2 - k) if transposed else (k - k_taps // 2)
        term = ext_ref[l0 // LANES, pl.ds(r0 + HALO + d, rc), :] * w_ref[k:k + 1, l0:l0 + LANES]
        acc = term if acc is None else acc + term
    return acc


def _conv_wgrad_block(acc_ref, dout, ext_ref, r0, rc, l0, k_taps, scale=None):
    for k in range(k_taps):
        prod = dout * ext_ref[l0 // LANES, pl.ds(r0 + HALO + k - k_taps // 2, rc), :]
        part = prod.reshape(rc // SUBLANES, SUBLANES, LANES).sum(axis=0)
        if scale is not None:
            part = part * scale
        acc_ref[k, :, l0:l0 + LANES] = acc_ref[k, :, l0:l0 + LANES] + part


def _reduce_acc(out_ref, acc_ref, k_taps):
    for k in range(k_taps):
        out_ref[k:k + 1, :] = jnp.sum(acc_ref[k], axis=0, keepdims=True)


def _fold8(v):
    rc, c = v.shape
    return v.reshape(rc // SUBLANES, SUBLANES, c).sum(axis=0)


def _ext_scratch(ts, c):
    return pltpu.VMEM((c // LANES, ts + 2 * HALO, LANES), F32)


def _put_rows(ext_ref, r0, rc, val):
    for q in range(val.shape[1] // LANES):
        ext_ref[q, pl.ds(r0 + HALO, rc), :] = val[:, q * LANES:(q + 1) * LANES]


def _fill_halo(ext_ref, vals_prev, vals_next, ts, first, last):
    for q in range(vals_prev.shape[1] // LANES):
        cols = slice(q * LANES, (q + 1) * LANES)
        ext_ref[q, 0:HALO, :] = jnp.where(first, 0.0, vals_prev[:, cols])
        ext_ref[q, HALO + ts:HALO + ts + HALO, :] = jnp.where(last, 0.0, vals_next[:, cols])


def _rms_bwd_rows(dh, xf, g):
    r = lax.rsqrt(jnp.mean(xf * xf, axis=-1, keepdims=True) + RMS_EPS)
    xhat = xf * r
    dxh = dh * g
    dx = r * (dxh - xhat * jnp.mean(dxh * xhat, axis=-1, keepdims=True))
    return dx, dh * xhat


_NT = (((1,), (1,)), ((), ()))
_TN = (((0,), (0,)), ((), ()))


def _fwd_in(x, g1, w_in_t, ts, exchange=None):
    s = x.shape[0]

    def body(x_ref, g_ref, w_ref, z_ref, h_ref):
        xf = x_ref[...]
        r = lax.rsqrt(jnp.mean(xf * xf, axis=-1, keepdims=True) + RMS_EPS)
        h = (xf * r * g_ref[...]).astype(BF16)
        h_ref[...] = h
        for n0 in range(0, D_IN, 512):
            z_ref[:, n0:n0 + 512] = lax.dot_general(h, w_ref[n0:n0 + 512, :], _NT,
                                                    preferred_element_type=F32).astype(BF16)

    return _call(
        body, name="fwd_in", grid=(s // ts,),
        in_specs=[_rows(ts, D_MODEL), _const((1, D_MODEL)), _const((D_IN, D_MODEL))],
        out_specs=[_rows(ts, D_IN), _rows(ts, D_MODEL)],
        out_shape=[_sds((s, D_IN), BF16), _sds((s, D_MODEL), BF16)],
        scratch_shapes=[], args=(x, g1, w_in_t), exchange=exchange, forward_step=(s // ts) * 3 // 4)


def _p_u0(z_ref, rows):
    a_h = z_ref[rows, 0:D_A].astype(F32)
    a_c = z_ref[rows, 2 * D_A:3 * D_A].astype(F32)
    b_v = z_ref[rows, 3 * D_A:3 * D_A + D_B].astype(F32)
    b_g = z_ref[rows, 3 * D_A + D_B:D_IN].astype(F32)
    return a_c * a_h, b_v * _sigmoid(b_g)


def _layernorm_rows(u_blocks):
    tot = None
    for ub in u_blocks:
        sm = jnp.sum(ub, axis=-1, keepdims=True)
        tot = sm if tot is None else tot + sm
    mu = tot * (1.0 / D_B)
    var = None
    for ub in u_blocks:
        sq = jnp.sum((ub - mu) * (ub - mu), axis=-1, keepdims=True)
        var = sq if var is None else var + sq
    rstd = lax.rsqrt(var * (1.0 / D_B) + LN_EPS)
    return mu, rstd


def _fwd_mix(z, x, wa, wb, bb, lg, lb, w_out, ts, exchange=None):
    s = x.shape[0]
    nt = s // ts
    rc = min(ROW_CHUNK, ts)

    def body(z_ref, zp_ref, zn_ref, x_ref, wa_ref, wb_ref, bb_ref, lg_ref, lb_ref, wo_ref,
             x2_ref, y_ref, u_ref, pe_ref, ue_ref):
        i = pl.program_id(0)
        pp, up = _p_u0(zp_ref, slice(None))
        pn, un = _p_u0(zn_ref, slice(None))
        _fill_halo(pe_ref, pp, pn, ts, i == 0, i == nt - 1)
        _fill_halo(ue_ref, up, un, ts, i == 0, i == nt - 1)

        def fill(j, carry):
            r0 = pl.multiple_of(j * rc, rc)
            p, u0 = _p_u0(z_ref, pl.ds(r0, rc))
            _put_rows(pe_ref, r0, rc, p)
            _put_rows(ue_ref, r0, rc, u0)
            return carry

        lax.fori_loop(0, ts // rc, fill, 0)

        def main(j, carry):
            r0 = pl.multiple_of(j * rc, rc)
            rows = pl.ds(r0, rc)
            for l0 in range(0, D_A, LANES):
                ca = _conv_block(pe_ref, wa_ref, r0, rc, l0, K_A, False)
                a_b = z_ref[rows, D_A + l0:D_A + l0 + LANES].astype(F32)
                y_ref[rows, l0:l0 + LANES] = (a_b * ca).astype(BF16)
            ubs = []
            for l0 in range(0, D_B, LANES):
                ub = _conv_block(ue_ref, wb_ref, r0, rc, l0, K_B, False) + bb_ref[:, l0:l0 + LANES]
                u_ref[rows, l0:l0 + LANES] = ub
                ubs.append(ub)
            mu, rstd = _layernorm_rows(ubs)
            for q, l0 in enumerate(range(0, D_B, LANES)):
                t = (ubs[q] - mu) * rstd * lg_ref[:, l0:l0 + LANES] + lb_ref[:, l0:l0 + LANES]
                y_ref[rows, D_A + l0:D_A + l0 + LANES] = (t * _sigmoid(t)).astype(BF16)
            return carry

        lax.fori_loop(0, ts // rc, main, 0, unroll=4)
        for n0 in range(0, D_MODEL, 512):
            x2_ref[:, n0:n0 + 512] = x_ref[:, n0:n0 + 512] + jnp.dot(
                y_ref[...], wo_ref[:, n0:n0 + 512], preferred_element_type=F32)

    return _call(
        body, name="fwd_mix", grid=(nt,),
        in_specs=[_rows(ts, D_IN), _prev(ts, D_IN), _next(ts, D_IN, s), _rows(ts, D_MODEL),
                  _const((K_A, D_A)), _const((K_B, D_B)), _const((1, D_B)), _const((1, D_B)), _const((1, D_B)),
                  _const((D_MODEL, D_MODEL))],
        out_specs=[_rows(ts, D_MODEL), _rows(ts, D_MODEL), _rows(ts, D_B)],
        out_shape=[_sds((s, D_MODEL), F32), _sds((s, D_MODEL), BF16), _sds((s, D_B), F32)],
        scratch_shapes=[_ext_scratch(ts, D_A), _ext_scratch(ts, D_B)],
        args=(z, z, z, x, wa, wb, bb, lg, lb, w_out), exchange=exchange, forward_step=nt * 5 // 8)


def _fwd_ffn_in(x2, g2, w_gate_t, w_up_t, ts, exchange=None):
    s = x2.shape[0]
    half = D_FF // 2

    def body(x_ref, g_ref, wg_ref, wu_ref, g0_ref, v_ref, h_ref):
        xf = x_ref[...]
        r = lax.rsqrt(jnp.mean(xf * xf, axis=-1, keepdims=True) + RMS_EPS)
        h = (xf * r * g_ref[...]).astype(BF16)
        h_ref[...] = h
        for n0 in range(0, D_FF, half):
            g0_ref[:, n0:n0 + half] = lax.dot_general(h, wg_ref[n0:n0 + half, :], _NT,
                                                      preferred_element_type=F32).astype(BF16)
            v_ref[:, n0:n0 + half] = lax.dot_general(h, wu_ref[n0:n0 + half, :], _NT,
                                                     preferred_element_type=F32).astype(BF16)

    return _call(
        body, name="fwd_ffn_in", grid=(s // ts,),
        in_specs=[_rows(ts, D_MODEL), _const((1, D_MODEL)), _const((D_FF, D_MODEL)), _const((D_FF, D_MODEL))],
        out_specs=[_rows(ts, D_FF), _rows(ts, D_FF), _rows(ts, D_MODEL)],
        out_shape=[_sds((s, D_FF), BF16), _sds((s, D_FF), BF16), _sds((s, D_MODEL), BF16)],
        scratch_shapes=[], args=(x2, g2, w_gate_t, w_up_t), exchange=exchange, forward_step=(s // ts) // 2)


def _ffn_out_and_back(g0, v, x2, wf, w_down, g3, target, ts):
    s = x2.shape[0]
    nt = s // ts
    rc = min(ROW_CHUNK, ts)
    half = D_FF // 2

    def body(g0_ref, gp_ref, gn_ref, v_ref, x2_ref, wf_ref, wd_ref, g3_ref, t_ref,
             a_ref, dx3_ref, dx3b_ref, loss_ref, dg3_ref, dg_ref, dv_ref, dwf_ref,
             ge_ref, silu_ref, dsv_ref, da_ref, acc_ref):
        i = pl.program_id(0)

        @pl.when(i == 0)
        def _():
            acc_ref[...] = jnp.zeros_like(acc_ref)

        _fill_halo(ge_ref, gp_ref[...].astype(F32), gn_ref[...].astype(F32), ts, i == 0, i == nt - 1)

        def fill(j, carry):
            r0 = pl.multiple_of(j * rc, rc)
            _put_rows(ge_ref, r0, rc, g0_ref[pl.ds(r0, rc), :].astype(F32))
            return carry

        lax.fori_loop(0, ts // rc, fill, 0)

        def act(j, carry):
            r0 = pl.multiple_of(j * rc, rc)
            rows = pl.ds(r0, rc)
            for l0 in range(0, D_FF, LANES):
                g = _conv_block(ge_ref, wf_ref, r0, rc, l0, K_F, False)
                vv = v_ref[rows, l0:l0 + LANES].astype(F32)
                sg = _sigmoid(g)
                silu = g * sg
                a_ref[rows, l0:l0 + LANES] = (silu * vv).astype(BF16)
                silu_ref[rows, l0:l0 + LANES] = silu
                dsv_ref[rows, l0:l0 + LANES] = (sg + silu * (1.0 - sg)) * vv
            return carry

        lax.fori_loop(0, ts // rc, act, 0)
        for n0 in range(0, D_MODEL, 512):
            dx3_ref[:, n0:n0 + 512] = x2_ref[:, n0:n0 + 512] + jnp.dot(
                a_ref[...], wd_ref[:, n0:n0 + 512], preferred_element_type=F32)

        @pl.when(i == 0)
        def _():
            loss_ref[...] = jnp.zeros_like(loss_ref)
            dg3_ref[...] = jnp.zeros_like(dg3_ref)

        def tail(j, carry):
            lsum, dgsum = carry
            r0 = j * rc
            rows = slice(r0, r0 + rc)
            x3 = dx3_ref[rows, :]
            r = lax.rsqrt(jnp.mean(x3 * x3, axis=-1, keepdims=True) + RMS_EPS)
            xhat = x3 * r
            diff = xhat * g3_ref[...] - t_ref[rows, :]
            dout = diff * (1.0 / D_MODEL)
            dxh = dout * g3_ref[...]
            dx3 = r * (dxh - xhat * jnp.mean(dxh * xhat, axis=-1, keepdims=True))
            dx3_ref[rows, :] = dx3
            dx3b_ref[rows, :] = dx3.astype(BF16)
            lsum = lsum + _fold8(diff * diff)
            dgsum = dgsum + _fold8(dout * xhat)
            return lsum, dgsum

        sums = (jnp.zeros((SUBLANES, D_MODEL), F32),) * 2
        for j in range(ts // rc):
            sums = tail(j, sums)
        lsum, dgsum = sums
        loss_ref[...] = loss_ref[...] + (0.5 / D_MODEL) * jnp.sum(lsum, keepdims=True)
        dg3_ref[...] = dg3_ref[...] + jnp.sum(dgsum, axis=0, keepdims=True)

        for n0 in range(0, D_FF, half):
            da_ref[:, n0:n0 + half] = lax.dot_general(dx3b_ref[...], wd_ref[n0:n0 + half, :], _NT,
                                                      preferred_element_type=F32)

        def back(j, carry):
            r0 = pl.multiple_of(j * rc, rc)
            rows = pl.ds(r0, rc)
            for l0 in range(0, D_FF, LANES):
                da = da_ref[rows, l0:l0 + LANES]
                dv_ref[rows, l0:l0 + LANES] = (da * silu_ref[rows, l0:l0 + LANES]).astype(BF16)
                dgg = da * dsv_ref[rows, l0:l0 + LANES]
                dg_ref[rows, l0:l0 + LANES] = dgg.astype(BF16)
                _conv_wgrad_block(acc_ref, dgg, ge_ref, r0, rc, l0, K_F)
            return carry

        lax.fori_loop(0, ts // rc, back, 0)

        @pl.when(i == nt - 1)
        def _():
            _reduce_acc(dwf_ref, acc_ref, K_F)

    return pl.pallas_call(
        body, name="ffn_out_and_back", grid=(nt,),
        in_specs=[_rows(ts, D_FF), _prev(ts, D_FF), _next(ts, D_FF, s), _rows(ts, D_FF), _rows(ts, D_MODEL),
                  _const((K_F, D_FF)), _const((D_FF, D_MODEL)), _const((1, D_MODEL)), _rows(ts, D_MODEL)],
        out_specs=[_rows(ts, D_FF), _rows(ts, D_MODEL), _rows(ts, D_MODEL), _acc_out((1, 1)), _acc_out((1, D_MODEL)),
                   _rows(ts, D_FF), _rows(ts, D_FF), _acc_out((K_F, D_FF))],
        out_shape=[_sds((s, D_FF), BF16), _sds((s, D_MODEL), F32), _sds((s, D_MODEL), BF16),
                   _sds((1, 1), F32), _sds((1, D_MODEL), F32),
                   _sds((s, D_FF), BF16), _sds((s, D_FF), BF16), _sds((K_F, D_FF), F32)],
        scratch_shapes=[_ext_scratch(ts, D_FF), pltpu.VMEM((ts, D_FF), F32), pltpu.VMEM((ts, D_FF), F32),
                        pltpu.VMEM((ts, D_FF), F32), pltpu.VMEM((K_F, SUBLANES, D_FF), F32)],
        compiler_params=_params(),
    )(g0, g0, g0, v, x2, wf, w_down, g3, target)


def _bwd_ffn_b(dg, dv, wf, w_gate, w_up, x2, g2, dx3, ts, exchange=None):
    s = x2.shape[0]
    nt = s // ts
    rc = min(ROW_CHUNK, ts)
    n_sub = ts // rc

    def body(dg_ref, dgp_ref, dgn_ref, dv_ref, wf_ref, wg_ref, wu_ref, x2_ref, g2_ref, dx3_ref,
             dg0_ref, dx2_ref, dx2b_ref, dgn2_ref, dge_ref, dg8_ref, a0_ref, a1_ref, p0_ref, p1_ref):
        i = pl.program_id(0)
        vt = jnp.minimum(i, nt - 1)
        live = (i >= 2).astype(F32)

        @pl.when(i == 0)
        def _():
            dg8_ref[...] = jnp.zeros_like(dg8_ref)
            a1_ref[...] = jnp.zeros_like(a1_ref)
            p1_ref[...] = jnp.zeros_like(p1_ref)

        _fill_halo(dge_ref, dgp_ref[...].astype(F32), dgn_ref[...].astype(F32), ts, vt == 0, vt == nt - 1)

        def fill(j, carry):
            r0 = pl.multiple_of(j * rc, rc)
            _put_rows(dge_ref, r0, rc, dg_ref[pl.ds(r0, rc), :].astype(F32))
            return carry

        lax.fori_loop(0, n_sub, fill, 0)

        def stage(a_new, a_old, p_new, p_old):
            def conv_t(r0, l0):
                rows = slice(r0, r0 + rc)
                dg0 = _conv_block(dge_ref, wf_ref, r0, rc, l0, K_F, True).astype(BF16)
                dg0_ref[rows, l0:l0 + LANES] = dg0
                a_new[rows, l0:l0 + LANES] = dg0

            def tail(r0):
                rows = slice(r0, r0 + rc)
                dx, dgrow = _rms_bwd_rows(p_old[rows, :], x2_ref[rows, :], g2_ref[...])
                dx2 = dx3_ref[rows, :] + dx
                dx2_ref[rows, :] = dx2
                dx2b_ref[rows, :] = dx2.astype(BF16)
                dg8_ref[...] = dg8_ref[...] + _fold8(dgrow) * live

            units = []
            for q in range(n_sub):
                units += [(conv_t, (q * rc, l0)) for l0 in range(0, D_FF, LANES)]
                units.append((tail, (q * rc,)))
            _interleaved(units, _matmul_pieces([(a_old, wg_ref), (dv_ref, wu_ref)], p_new, 2))

        @pl.when(i % 2 == 0)
        def _():
            stage(a0_ref, a1_ref, p0_ref, p1_ref)

        @pl.when(i % 2 == 1)
        def _():
            stage(a1_ref, a0_ref, p1_ref, p0_ref)

        @pl.when(i == nt + 1)
        def _():
            dgn2_ref[...] = jnp.sum(dg8_ref[...], axis=0, keepdims=True)

    vtile = lambda i: jnp.minimum(i, nt - 1)
    mtile = lambda i: jnp.clip(i - 1, 0, nt - 1)
    ttile = lambda i: jnp.clip(i - 2, 0, nt - 1)
    return _call(
        body, name="bwd_ffn_b", grid=(nt + 2,),
        in_specs=[_rows_at(ts, D_FF, vtile), _prev_at(ts, D_FF, vtile), _next_at(ts, D_FF, s, vtile),
                  _rows_at(ts, D_FF, mtile), _const((K_F, D_FF)),
                  _const((D_FF, D_MODEL)), _const((D_FF, D_MODEL)), _rows_at(ts, D_MODEL, ttile), _const((1, D_MODEL)),
                  _rows_at(ts, D_MODEL, ttile)],
        out_specs=[_rows_at(ts, D_FF, vtile), _rows_at(ts, D_MODEL, ttile), _rows_at(ts, D_MODEL, ttile),
                   _acc_out((1, D_MODEL))],
        out_shape=[_sds((s, D_FF), BF16), _sds((s, D_MODEL), F32), _sds((s, D_MODEL), BF16), _sds((1, D_MODEL), F32)],
        scratch_shapes=[_ext_scratch(ts, D_FF), pltpu.VMEM((SUBLANES, D_MODEL), F32),
                        pltpu.VMEM((ts, D_FF), BF16), pltpu.VMEM((ts, D_FF), BF16),
                        pltpu.VMEM((ts, D_MODEL), F32), pltpu.VMEM((ts, D_MODEL), F32)],
        args=(dg, dg, dg, dv, wf, w_gate, w_up, x2, g2, dx3), exchange=exchange)


def _bwd_mix_a(dx2b, w_out_t, z, u, wa, lg, lb, ts, exchange=None):
    s = dx2b.shape[0]
    nt = s // ts
    rc = min(ROW_CHUNK, ts)
    n_sub = ts // rc

    def body(dx_ref, wo_ref, z_ref, zp_ref, zn_ref, u_ref, wa_ref, lg_ref, lb_ref,
             dca_ref, du_ref, dab_ref, dwa_ref, dlg_ref, dlb_ref, dbb_ref, pe_ref, dy0_ref, dy1_ref, acc_ref, sacc_ref):
        i = pl.program_id(0)
        t = jnp.maximum(i - 1, 0)

        @pl.when(i == 0)
        def _():
            acc_ref[...] = jnp.zeros_like(acc_ref)
            sacc_ref[...] = jnp.zeros_like(sacc_ref)
            dy1_ref[...] = jnp.zeros_like(dy1_ref)

        pp, _ = _p_u0(zp_ref, slice(None))
        pn, _ = _p_u0(zn_ref, slice(None))
        _fill_halo(pe_ref, pp, pn, ts, t == 0, t == nt - 1)

        def fill(j, carry):
            r0 = pl.multiple_of(j * rc, rc)
            rows = pl.ds(r0, rc)
            _put_rows(pe_ref, r0, rc, z_ref[rows, 2 * D_A:3 * D_A].astype(F32) * z_ref[rows, 0:D_A].astype(F32))
            return carry

        lax.fori_loop(0, n_sub, fill, 0)

        def stage(dy_new, dy_old):
            def piece(m0, n0, width):
                dy_new[m0:m0 + MXU_ROWS, n0:n0 + width] = jnp.dot(
                    dx_ref[m0:m0 + MXU_ROWS, :], wo_ref[:, n0:n0 + width], preferred_element_type=F32).astype(BF16)

            units = []
            for q in range(n_sub):
                units += [(mixer_a, (dy_old, q * rc, l0)) for l0 in range(0, D_A, LANES)]
                units.append((mixer_b, (dy_old, q * rc)))
            _interleaved(units, [(piece, (m0, n0, w)) for n0, w in _col_pieces(D_MODEL) for m0 in range(0, ts, MXU_ROWS)])

        def mixer_a(dy_ref, r0, l0):
            rows = slice(r0, r0 + rc)
            ca = _conv_block(pe_ref, wa_ref, r0, rc, l0, K_A, False)
            a_b = z_ref[rows, D_A + l0:D_A + l0 + LANES].astype(F32)
            dya = dy_ref[rows, l0:l0 + LANES].astype(F32)
            dab_ref[rows, l0:l0 + LANES] = (dya * ca).astype(BF16)
            dca = dya * a_b
            dca_ref[rows, l0:l0 + LANES] = dca
            _conv_wgrad_block(acc_ref, dca, pe_ref, r0, rc, l0, K_A)

        def mixer_b(dy_ref, r0):
            rows = slice(r0, r0 + rc)
            ubs = [u_ref[rows, l0:l0 + LANES] for l0 in range(0, D_B, LANES)]
            mu, rstd = _layernorm_rows(ubs)
            ns, dns = [], []
            m1 = None
            m2 = None
            for q, l0 in enumerate(range(0, D_B, LANES)):
                n = (ubs[q] - mu) * rstd
                lgq = lg_ref[:, l0:l0 + LANES]
                t = n * lgq + lb_ref[:, l0:l0 + LANES]
                sg = _sigmoid(t)
                dt = dy_ref[rows, D_A + l0:D_A + l0 + LANES].astype(F32) * (sg * (1.0 + t * (1.0 - sg)))
                dn = dt * lgq
                ns.append(n)
                dns.append(dn)
                s1 = jnp.sum(dn, axis=-1, keepdims=True)
                s2 = jnp.sum(dn * n, axis=-1, keepdims=True)
                m1 = s1 if m1 is None else m1 + s1
                m2 = s2 if m2 is None else m2 + s2
                sacc_ref[0, :, l0:l0 + LANES] = sacc_ref[0, :, l0:l0 + LANES] + _fold8(dt * n)
                sacc_ref[1, :, l0:l0 + LANES] = sacc_ref[1, :, l0:l0 + LANES] + _fold8(dt)
            m1 = m1 * (1.0 / D_B)
            m2 = m2 * (1.0 / D_B)
            for q, l0 in enumerate(range(0, D_B, LANES)):
                du = rstd * (dns[q] - m1 - ns[q] * m2)
                du_ref[rows, l0:l0 + LANES] = du
                sacc_ref[2, :, l0:l0 + LANES] = sacc_ref[2, :, l0:l0 + LANES] + _fold8(du)

        @pl.when(i % 2 == 0)
        def _():
            stage(dy0_ref, dy1_ref)

        @pl.when(i % 2 == 1)
        def _():
            stage(dy1_ref, dy0_ref)

        @pl.when(i == nt)
        def _():
            _reduce_acc(dwa_ref, acc_ref, K_A)
            dlg_ref[...] = jnp.sum(sacc_ref[0], axis=0, keepdims=True)
            dlb_ref[...] = jnp.sum(sacc_ref[1], axis=0, keepdims=True)
            dbb_ref[...] = jnp.sum(sacc_ref[2], axis=0, keepdims=True)

    cur = lambda i: jnp.minimum(i, nt - 1)
    old = lambda i: jnp.maximum(i - 1, 0)
    return _call(
        body, name="bwd_mix_a", grid=(nt + 1,),
        in_specs=[_rows_at(ts, D_MODEL, cur), _const((D_MODEL, D_MODEL)), _rows_at(ts, D_IN, old), _prev_at(ts, D_IN, old),
                  _next_at(ts, D_IN, s, old), _rows_at(ts, D_B, old), _const((K_A, D_A)), _const((1, D_B)), _const((1, D_B))],
        out_specs=[_rows_at(ts, D_A, old), _rows_at(ts, D_B, old), _rows_at(ts, D_A, old), _acc_out((K_A, D_A)),
                   _acc_out((1, D_B)), _acc_out((1, D_B)), _acc_out((1, D_B))],
        out_shape=[_sds((s, D_A), F32), _sds((s, D_B), F32), _sds((s, D_A), BF16), _sds((K_A, D_A), F32),
                   _sds((1, D_B), F32), _sds((1, D_B), F32), _sds((1, D_B), F32)],
        scratch_shapes=[_ext_scratch(ts, D_A), pltpu.VMEM((ts, D_MODEL), BF16), pltpu.VMEM((ts, D_MODEL), BF16),
                        pltpu.VMEM((K_A, SUBLANES, D_A), F32), pltpu.VMEM((3, SUBLANES, D_B), F32)],
        args=(dx2b, w_out_t, z, z, z, u, wa, lg, lb), exchange=exchange)


def _bwd_mix_b(dca, du, z, dab, wa, wb, w_in, x, g1, dx2, ts, exchange=None):
    s = x.shape[0]
    nt = s // ts
    rc = min(ROW_CHUNK, ts)
    n_sub = ts // rc

    def body(dca_ref, dcap_ref, dcan_ref, du_ref, dup_ref, dun_ref, z_ref, zp_ref, zn_ref, dab_ref,
             wa_ref, wb_ref, wi_ref, x_ref, g1_ref, dx2_ref,
             dz_ref, dx_ref, dg1_ref, dwb_ref, dcae_ref, due_ref, ue_ref, acc_ref, dg8_ref,
             dz0_ref, dz1_ref, dh0_ref, dh1_ref):
        i = pl.program_id(0)

        @pl.when(i == 0)
        def _():
            acc_ref[...] = jnp.zeros_like(acc_ref)
            dg8_ref[...] = jnp.zeros_like(dg8_ref)
            dz1_ref[...] = jnp.zeros_like(dz1_ref)
            dh1_ref[...] = jnp.zeros_like(dh1_ref)

        vt = jnp.minimum(i, nt - 1)
        first = vt == 0
        last = vt == nt - 1
        live = (i < nt).astype(F32)
        _fill_halo(dcae_ref, dcap_ref[...], dcan_ref[...], ts, first, last)
        _fill_halo(due_ref, dup_ref[...], dun_ref[...], ts, first, last)
        _, up = _p_u0(zp_ref, slice(None))
        _, un = _p_u0(zn_ref, slice(None))
        _fill_halo(ue_ref, up, un, ts, first, last)

        def fill(j, carry):
            r0 = pl.multiple_of(j * rc, rc)
            rows = pl.ds(r0, rc)
            _put_rows(dcae_ref, r0, rc, dca_ref[rows, :])
            _put_rows(due_ref, r0, rc, du_ref[rows, :])
            b_v = z_ref[rows, 3 * D_A:3 * D_A + D_B].astype(F32)
            b_g = z_ref[rows, 3 * D_A + D_B:D_IN].astype(F32)
            _put_rows(ue_ref, r0, rc, b_v * _sigmoid(b_g))
            return carry

        lax.fori_loop(0, n_sub, fill, 0)

        def stage(dz_new, dz_old, dh_new, dh_old):
            def put(rows, c0, val):
                dz_ref[rows, c0:c0 + LANES] = val
                dz_new[rows, c0:c0 + LANES] = val

            def mixer_a(r0, l0):
                rows = slice(r0, r0 + rc)
                dp = _conv_block(dcae_ref, wa_ref, r0, rc, l0, K_A, True)
                a_h = z_ref[rows, l0:l0 + LANES].astype(F32)
                a_c = z_ref[rows, 2 * D_A + l0:2 * D_A + l0 + LANES].astype(F32)
                put(rows, l0, (dp * a_c).astype(BF16))
                put(rows, D_A + l0, dab_ref[rows, l0:l0 + LANES])
                put(rows, 2 * D_A + l0, (dp * a_h).astype(BF16))

            def mixer_b(r0, l0):
                rows = slice(r0, r0 + rc)
                du0 = _conv_block(due_ref, wb_ref, r0, rc, l0, K_B, True)
                b_v = z_ref[rows, 3 * D_A + l0:3 * D_A + l0 + LANES].astype(F32)
                b_g = z_ref[rows, 3 * D_A + D_B + l0:3 * D_A + D_B + l0 + LANES].astype(F32)
                sg = _sigmoid(b_g)
                put(rows, 3 * D_A + l0, (du0 * sg).astype(BF16))
                put(rows, 3 * D_A + D_B + l0, (du0 * b_v * (sg * (1.0 - sg))).astype(BF16))
                _conv_wgrad_block(acc_ref, du_ref[rows, l0:l0 + LANES], ue_ref, r0, rc, l0, K_B, live)

            def tail(r0):
                rows = slice(r0, r0 + rc)
                dx, dgrow = _rms_bwd_rows(dh_old[rows, :], x_ref[rows, :], g1_ref[...])
                dx_ref[rows, :] = dx2_ref[rows, :] + dx
                dg8_ref[...] = dg8_ref[...] + _fold8(dgrow)

            units = []
            for q in range(n_sub):
                units += [(mixer_a, (q * rc, l0)) for l0 in range(0, D_A, LANES)]
                units += [(mixer_b, (q * rc, l0)) for l0 in range(0, D_B, LANES)]
                units.append((tail, (q * rc,)))
            _interleaved(units, _matmul_pieces([(dz_old, wi_ref)], dh_new, 2))

        @pl.when(i % 2 == 0)
        def _():
            stage(dz0_ref, dz1_ref, dh0_ref, dh1_ref)

        @pl.when(i % 2 == 1)
        def _():
            stage(dz1_ref, dz0_ref, dh1_ref, dh0_ref)

        @pl.when(i == nt + 1)
        def _():
            _reduce_acc(dwb_ref, acc_ref, K_B)
            dg1_ref[...] = jnp.sum(dg8_ref[...], axis=0, keepdims=True)

    vtile = lambda i: jnp.minimum(i, nt - 1)
    ttile = lambda i: jnp.clip(i - 2, 0, nt - 1)
    return _call(
        body, name="bwd_mix_b", grid=(nt + 2,),
        in_specs=[_rows_at(ts, D_A, vtile), _prev_at(ts, D_A, vtile), _next_at(ts, D_A, s, vtile),
                  _rows_at(ts, D_B, vtile), _prev_at(ts, D_B, vtile), _next_at(ts, D_B, s, vtile),
                  _rows_at(ts, D_IN, vtile), _prev_at(ts, D_IN, vtile), _next_at(ts, D_IN, s, vtile), _rows_at(ts, D_A, vtile),
                  _const((K_A, D_A)), _const((K_B, D_B)), _const((D_IN, D_MODEL)), _rows_at(ts, D_MODEL, ttile),
                  _const((1, D_MODEL)), _rows_at(ts, D_MODEL, ttile)],
        out_specs=[_rows_at(ts, D_IN, vtile), _rows_at(ts, D_MODEL, ttile), _acc_out((1, D_MODEL)), _acc_out((K_B, D_B))],
        out_shape=[_sds((s, D_IN), BF16), _sds((s, D_MODEL), F32), _sds((1, D_MODEL), F32), _sds((K_B, D_B), F32)],
        scratch_shapes=[_ext_scratch(ts, D_A), _ext_scratch(ts, D_B), _ext_scratch(ts, D_B),
                        pltpu.VMEM((K_B, SUBLANES, D_B), F32), pltpu.VMEM((SUBLANES, D_MODEL), F32),
                        pltpu.VMEM((ts, D_IN), BF16), pltpu.VMEM((ts, D_IN), BF16),
                        pltpu.VMEM((ts, D_MODEL), F32), pltpu.VMEM((ts, D_MODEL), F32)],
        args=(dca, dca, dca, du, du, du, z, z, z, dab, wa, wb, w_in, x, g1, dx2), exchange=exchange)


def _matmul_tn(a, b, name, exchange=None):
    s, m = a.shape
    n = b.shape[1]
    tk = min(1024, s)
    nk = s // tk
    tm = 256

    def body(a_ref, b_ref, o_ref, acc_ref):
        k = pl.program_id(0)

        @pl.when(k == 0)
        def _():
            acc_ref[...] = jnp.zeros_like(acc_ref)

        for m0 in range(0, m, tm):
            acc_ref[m0:m0 + tm, :] = acc_ref[m0:m0 + tm, :] + lax.dot_general(
                a_ref[:, m0:m0 + tm], b_ref[...], _TN, preferred_element_type=F32)

        @pl.when(k == nk - 1)
        def _():
            o_ref[...] = acc_ref[...].astype(BF16)

    (out,), got = _call(
        body, name=name, grid=(nk,),
        in_specs=[_rows(tk, m), _rows(tk, n)],
        out_specs=[_acc_out((m, n))],
        out_shape=[_sds((m, n), BF16)],
        scratch_shapes=[pltpu.VMEM((m, n), F32)], args=(a, b), exchange=exchange)
    return out if exchange is None else (out, got)


CHIP_RELS = ((1, 0, 0), (0, 1, 0), (1, 1, 0))
CORE_RELS = ((0, 0, 1),)
ALL_RELS = ((0, 0, 1), (0, 1, 0), (0, 1, 1), (1, 0, 0), (1, 0, 1), (1, 1, 0), (1, 1, 1))


def _chip_slot(dev):
    return 2 * dev[0] + dev[1]


def _dev_slot(dev):
    return 4 * dev[0] + 2 * dev[1] + dev[2]


def _me():
    return (lax.axis_index("x"), lax.axis_index("y"), lax.axis_index("c"))


def _peer(me, rel):
    return tuple((1 - me[a]) if rel[a] else me[a] for a in range(3))


_ANY = pl.BlockSpec(memory_space=pl.ANY)


class _Exchange:
    def __init__(self, inputs, out_shape, scratch, start, finish, forward=None):
        self.inputs, self.out_shape, self.scratch = list(inputs), list(out_shape), list(scratch)
        self.start, self.finish, self.forward = start, finish, forward


def _all_gather(payloads):
    n_p = len(payloads)
    n_k = 1 + 2 * len(CHIP_RELS)

    def copy(srcs, dsts, sems, p, k, block_dev, to, from_src):
        blk = dsts[p].at[_dev_slot(block_dev)]
        return pltpu.make_async_remote_copy(
            src_ref=srcs[p] if from_src else blk, dst_ref=blk,
            send_sem=sems[0].at[n_k * p + k], recv_sem=sems[1].at[n_k * p + k], device_id=to, device_id_type=MESH)

    def own_copy(srcs, dsts, sems, p):
        return pltpu.make_async_copy(srcs[p], dsts[p].at[_dev_slot(_me())], sems[2].at[p])

    def start(srcs, dsts, sems):
        me = _me()
        for p in range(n_p):
            own_copy(srcs, dsts, sems, p).start()
        for j, rel in enumerate(CHIP_RELS):
            for p in range(n_p):
                copy(srcs, dsts, sems, p, 1 + j, me, _peer(me, rel), True).start()
        for p in range(n_p):
            copy(srcs, dsts, sems, p, 0, me, _peer(me, CORE_RELS[0]), True).start()

    def forward(srcs, dsts, sems):
        me = _me()
        sibling = _peer(me, CORE_RELS[0])
        for j, rel in enumerate(CHIP_RELS):
            other = _peer(me, rel)
            for p in range(n_p):
                copy(srcs, dsts, sems, p, 1 + j, other, me, False).wait_recv()
                copy(srcs, dsts, sems, p, 4 + j, other, sibling, False).start()

    def finish(srcs, dsts, sems):
        me = _me()
        sibling = _peer(me, CORE_RELS[0])
        for p in range(n_p):
            copy(srcs, dsts, sems, p, 0, sibling, me, False).wait_recv()
        for j, rel in enumerate(CHIP_RELS):
            for p in range(n_p):
                copy(srcs, dsts, sems, p, 4 + j, _peer(sibling, rel), me, False).wait_recv()
        for p in range(n_p):
            own_copy(srcs, dsts, sems, p).wait()
            copy(srcs, dsts, sems, p, 0, me, sibling, True).wait_send()
            for j, rel in enumerate(CHIP_RELS):
                copy(srcs, dsts, sems, p, 1 + j, me, _peer(me, rel), True).wait_send()
                copy(srcs, dsts, sems, p, 4 + j, _peer(me, rel), sibling, False).wait_send()

    return _Exchange(
        payloads, [_sds((N_DEV,) + p.shape, p.dtype) for p in payloads],
        [pltpu.SemaphoreType.DMA((n_p * n_k,)), pltpu.SemaphoreType.DMA((n_p * n_k,)), pltpu.SemaphoreType.DMA((n_p,))],
        start, finish, forward)


def _gather_direct(payload):
    n_r = len(ALL_RELS)

    def copies(srcs, dsts, sems):
        me = _me()
        mine = dsts[0].at[_dev_slot(me)]
        own = pltpu.make_async_copy(srcs[0], mine, sems[2].at[0])
        remote = [pltpu.make_async_remote_copy(src_ref=srcs[0], dst_ref=mine, send_sem=sems[0].at[k], recv_sem=sems[1].at[k],
                                               device_id=_peer(me, rel), device_id_type=MESH)
                  for k, rel in enumerate(ALL_RELS)]
        return [own] + remote

    def start(srcs, dsts, sems):
        for cp in copies(srcs, dsts, sems):
            cp.start()

    def finish(srcs, dsts, sems):
        for cp in copies(srcs, dsts, sems):
            cp.wait()

    return _Exchange([payload], [_sds((N_DEV,) + payload.shape, payload.dtype)],
                     [pltpu.SemaphoreType.DMA((n_r,)), pltpu.SemaphoreType.DMA((n_r,)), pltpu.SemaphoreType.DMA((1,))],
                     start, finish)


def _scatter_exchange(payloads, rels, src_view, view_shapes):
    n_p = len(payloads)
    n_r = len(rels)

    def copies(srcs, dsts, sems):
        me = _me()
        out = []
        for k, rel in enumerate(rels):
            peer = _peer(me, rel)
            for p in range(n_p):
                out.append(pltpu.make_async_remote_copy(
                    src_ref=src_view(srcs[p], peer), dst_ref=dsts[p].at[k],
                    send_sem=sems[0].at[p * n_r + k], recv_sem=sems[1].at[p * n_r + k],
                    device_id=peer, device_id_type=MESH))
        return out

    def start(srcs, dsts, sems):
        for cp in copies(srcs, dsts, sems):
            cp.start()

    def finish(srcs, dsts, sems):
        for cp in copies(srcs, dsts, sems):
            cp.wait()

    return _Exchange(payloads, [_sds((n_r,) + vs, p.dtype) for vs, p in zip(view_shapes, payloads)],
                     [pltpu.SemaphoreType.DMA((n_p * n_r,)), pltpu.SemaphoreType.DMA((n_p * n_r,))], start, finish)


def _split_refs(refs, sizes):
    out, at = [], 0
    for n in sizes:
        out.append(refs[at:at + n])
        at += n
    return out


def _join(exchanges):
    n_in = [len(e.inputs) for e in exchanges]
    n_out = [len(e.out_shape) for e in exchanges]
    n_sc = [len(e.scratch) for e in exchanges]

    def phase(name):
        def run(ins, outs, scs):
            for e, i, o, s in zip(exchanges, _split_refs(ins, n_in), _split_refs(outs, n_out), _split_refs(scs, n_sc)):
                if getattr(e, name) is not None:
                    getattr(e, name)(i, o, s)
        return run

    return _Exchange([a for e in exchanges for a in e.inputs], [s for e in exchanges for s in e.out_shape],
                     [s for e in exchanges for s in e.scratch], phase("start"), phase("finish"),
                     phase("forward") if any(e.forward is not None for e in exchanges) else None)


def _run_exchanges(name, exchanges):
    n_in = [len(e.inputs) for e in exchanges]
    n_out = [len(e.out_shape) for e in exchanges]
    n_sc = [len(e.scratch) for e in exchanges]

    def body(*refs):
        ins, outs, scs = _split_refs(refs, [sum(n_in), sum(n_out), sum(n_sc)])
        parts = list(zip(exchanges, _split_refs(ins, n_in), _split_refs(outs, n_out), _split_refs(scs, n_sc)))
        for e, i, o, s in parts:
            e.start(i, o, s)
        for e, i, o, s in parts:
            if e.forward is not None:
                e.forward(i, o, s)
        for e, i, o, s in parts:
            e.finish(i, o, s)

    outs = pl.pallas_call(
        body, name=name, in_specs=[_ANY] * sum(n_in), out_specs=[_ANY] * sum(n_out),
        out_shape=[sd for e in exchanges for sd in e.out_shape],
        scratch_shapes=[sc for e in exchanges for sc in e.scratch],
    )(*[a for e in exchanges for a in e.inputs])
    return _split_refs(list(outs), n_out)


def _call(body, *, name, grid, in_specs, out_specs, out_shape, scratch_shapes, args, exchange=None, forward_step=None):
    n_in, n_out, n_sc = len(in_specs), len(out_specs), len(scratch_shapes)
    if exchange is None:
        outs = pl.pallas_call(body, name=name, grid=grid, in_specs=in_specs, out_specs=out_specs, out_shape=out_shape,
                              scratch_shapes=scratch_shapes, compiler_params=_params())(*args)
        return list(outs), []
    e = exchange
    sizes = [n_in, len(e.inputs), n_out, len(e.out_shape), n_sc, len(e.scratch)]
    last = grid[0] - 1

    def wrapped(*refs):
        a, ei, o, eo, sc, es = _split_refs(refs, sizes)
        i = pl.program_id(0)

        @pl.when(i == 0)
        def _():
            e.start(ei, eo, es)

        if e.forward is not None:
            @pl.when(i == forward_step)
            def _():
                e.forward(ei, eo, es)

        body(*a, *o, *sc)

        @pl.when(i == last)
        def _():
            e.finish(ei, eo, es)

    outs = pl.pallas_call(
        wrapped, name=name, grid=grid,
        in_specs=list(in_specs) + [_ANY] * len(e.inputs), out_specs=list(out_specs) + [_ANY] * len(e.out_shape),
        out_shape=list(out_shape) + e.out_shape, scratch_shapes=list(scratch_shapes) + e.scratch,
        compiler_params=_params(),
    )(*args, *e.inputs)
    outs = list(outs)
    return outs[:n_out], outs[n_out:]


def _pair_sum(grads, recvd, my_core, name):
    n_p = len(grads)

    def body(c_ref, *refs):
        del c_ref
        for p in range(n_p):
            refs[2 * n_p + p][...] = (refs[p][...].astype(F32) + refs[n_p + p][...].astype(F32)).astype(BF16)

    def blk(g):
        return (None, None) + g.shape[2:]

    return pl.pallas_call(
        body, name=name,
        grid_spec=pltpu.PrefetchScalarGridSpec(
            num_scalar_prefetch=1, grid=(N_CHIP,),
            in_specs=[pl.BlockSpec(blk(g), lambda j, c: (j, c[0], 0, 0)) for g in grads]
            + [pl.BlockSpec(blk(g), lambda j, c: (0, j, 0, 0)) for g in grads],
            out_specs=[pl.BlockSpec((None,) + g.shape[2:], lambda j, c: (j, 0, 0)) for g in grads]),
        out_shape=[_sds((N_CHIP,) + g.shape[2:], BF16) for g in grads],
        compiler_params=pltpu.CompilerParams(dimension_semantics=("arbitrary",), vmem_limit_bytes=VMEM_LIMIT),
    )(my_core, *grads, *recvd)


def _chip_sum(psums, recvd, my_chip, name):
    n_p = len(psums)

    def body(c_ref, *refs):
        del c_ref
        for p in range(n_p):
            acc = refs[p][...].astype(F32)
            for k in range(len(CHIP_RELS)):
                acc = acc + refs[n_p + p][k].astype(F32)
            refs[2 * n_p + p][...] = acc

    return pl.pallas_call(
        body, name=name,
        grid_spec=pltpu.PrefetchScalarGridSpec(
            num_scalar_prefetch=1, grid=(1,),
            in_specs=[pl.BlockSpec((None,) + g.shape[1:], lambda i, c: (c[0], 0, 0)) for g in psums]
            + [pl.BlockSpec(r.shape, lambda i, c: (0, 0, 0)) for r in recvd],
            out_specs=[pl.BlockSpec(g.shape[1:], lambda i, c: (0, 0)) for g in psums]),
        out_shape=[_sds(g.shape[1:], F32) for g in psums],
        compiler_params=pltpu.CompilerParams(dimension_semantics=("arbitrary",), vmem_limit_bytes=VMEM_LIMIT),
    )(my_chip, *psums, *recvd)


def _sum_devices(parts, shapes):
    plan = _small_rows(shapes)

    def body(p_ref, *refs):
        outs, tot_ref = refs[:-1], refs[-1]
        acc = p_ref[0]
        for j in range(1, N_DEV):
            acc = acc + p_ref[j]
        tot_ref[...] = acc
        for idx, r, c0, width, at in plan:
            outs[idx][r:r + 1, c0:c0 + width] = tot_ref[at:at + 1, 0:width]

    return pl.pallas_call(body, name="small_grad_sum", out_shape=[_sds(s, F32) for s in shapes],
                          scratch_shapes=[pltpu.VMEM(parts.shape[1:], F32)])(parts)


def _cast_shards(shards):
    def body(*refs):
        for src, dst in zip(refs[:len(shards)], refs[len(shards):]):
            dst[...] = src[...].astype(BF16)

    return pl.pallas_call(body, name="cast_shards", out_shape=[_sds(a.shape, BF16) for a in shards],
                          compiler_params=pltpu.CompilerParams(vmem_limit_bytes=VMEM_LIMIT))(*shards)


def _adamw(ws, gs, ms, vs, name):
    n_t = len(ws)

    def body(*refs):
        w_refs, g_refs, m_refs, v_refs = (refs[j * n_t:(j + 1) * n_t] for j in range(4))
        outs = refs[4 * n_t:]
        for k in range(n_t):
            gg = g_refs[k][...]
            mn = ADAM_B1 * m_refs[k][...] + (1.0 - ADAM_B1) * gg
            vn = ADAM_B2 * v_refs[k][...] + (1.0 - ADAM_B2) * (gg * gg)
            m_hat = mn / (1.0 - ADAM_B1 ** ADAM_STEP)
            v_hat = vn / (1.0 - ADAM_B2 ** ADAM_STEP)
            outs[3 * k][...] = -ADAM_LR * (m_hat / (jnp.sqrt(v_hat) + ADAM_EPS) + ADAM_WD * w_refs[k][...])
            outs[3 * k + 1][...] = mn
            outs[3 * k + 2][...] = vn

    out_shape = [_sds(w.shape, F32) for w in ws for _ in range(3)]
    return pl.pallas_call(body, name=name, out_shape=out_shape,
                          compiler_params=pltpu.CompilerParams(vmem_limit_bytes=VMEM_LIMIT))(*ws, *gs, *ms, *vs)


class _Mesh:
    def __init__(self, shards, my_chip, my_core):
        self.shards, self.my_chip, self.my_core = shards, my_chip.reshape(1), my_core.reshape(1)

    def gather(self, names):
        return _all_gather([self.shards[n] for n in names])

    @staticmethod
    def whole(gathered):
        return gathered.reshape(N_DEV * gathered.shape[1], gathered.shape[2])

    @staticmethod
    def by_device(grads):
        return [g.reshape(N_CHIP, 2, g.shape[0] // N_DEV, g.shape[1]) for g in grads]

    @staticmethod
    def to_sibling(parts):
        return _scatter_exchange(parts, CORE_RELS, lambda ref, peer: ref.at[:, peer[2]],
                                 [(N_CHIP,) + p.shape[2:] for p in parts])

    @staticmethod
    def to_chips(pair):
        return _scatter_exchange(pair, CHIP_RELS, lambda ref, peer: ref.at[_chip_slot(peer)], [p.shape[1:] for p in pair])


def _step(x, target, g1, w_in_t, wa, wb, bb, lg, lb, w_out, g2, w_gate_t, w_up_t, wf, w_down, g3, ts, mesh=None):
    (z, h1), got = _fwd_in(x, g1, w_in_t, ts, exchange=mesh and mesh.gather(["w_out"]))
    if mesh:
        w_out = mesh.whole(got[0])
    (x2, y, u), got = _fwd_mix(z, x, wa, wb, bb, lg, lb, w_out, ts, exchange=mesh and mesh.gather(["w_gate", "w_up"]))
    if mesh:
        w_gate_t, w_up_t = [mesh.whole(g) for g in got]
    (g0, v, h2), got = _fwd_ffn_in(x2, g2, w_gate_t, w_up_t, ts, exchange=mesh and mesh.gather(["w_down"]))
    if mesh:
        w_down = mesh.whole(got[0])
    tk = min(ts, SKEW_TILE)
    a, dx3, dx3b, loss, dg3, dgc, dv, dwf = _ffn_out_and_back(g0, v, x2, wf, w_down, g3, target, tk)
    one = dict(w_down=_matmul_tn(a, dx3b, "wgrad_down"), w_up=_matmul_tn(dv, h2, "wgrad_up"))
    parts1 = mesh and mesh.by_device(list(one.values()))
    (dg0, dx2, dx2b, dg2), got = _bwd_ffn_b(dgc, dv, wf, w_gate_t, w_up_t, x2, g2, dx3, tk,
                                            exchange=mesh and mesh.to_sibling(parts1))
    pair1 = mesh and _pair_sum(parts1, got, mesh.my_core, "rs_pair_sum_1")
    two = dict(w_gate=_matmul_tn(dg0, h2, "wgrad_gate"), w_out=_matmul_tn(y, dx2b, "wgrad_out"))
    parts2 = mesh and mesh.by_device(list(two.values()))
    (dca, du, dab, dwa, dlg, dlb, dbb), got = _bwd_mix_a(
        dx2b, jnp.swapaxes(w_out, 0, 1), z, u, wa, lg, lb, tk,
        exchange=mesh and _join([mesh.to_chips(pair1), mesh.to_sibling(parts2)]))
    (dz, dx, dg1, dwb), _ = _bwd_mix_b(dca, du, z, dab, wa, wb, w_in_t, x, g1, dx2, tk)
    small = dict(norm_mix_g=dg1, conv_a_w=dwa, conv_b_w=dwb, conv_b_b=dbb, ln_b_g=dlg, ln_b_b=dlb,
                 norm_ffn_g=dg2, conv_ffn_w=dwf, norm_final_g=dg3)
    if not mesh:
        return loss, dx, dict(w_in=_matmul_tn(dz, h1, "wgrad_in"), **one, **two), small
    big = dict(zip(one, _chip_sum(pair1, got[:2], mesh.my_chip, "rs_chip_sum_1")))
    pair2 = _pair_sum(parts2, got[2:], mesh.my_core, "rs_pair_sum_2")
    dw_in_t, got = _matmul_tn(dz, h1, "wgrad_in",
                              exchange=_join([mesh.to_chips(pair2), _gather_direct(_pack_small_grads(small, loss))]))
    big.update(zip(two, _chip_sum(pair2, got[:2], mesh.my_chip, "rs_chip_sum_2")))
    every = got[2]
    parts = mesh.by_device([dw_in_t])
    (got,) = _run_exchanges("rs_cores_last", [mesh.to_sibling(parts)])
    pair = _pair_sum(parts, got, mesh.my_core, "rs_pair_sum_last")
    (got,) = _run_exchanges("rs_chips_last", [mesh.to_chips(pair)])
    (big["w_in"],) = _chip_sum(pair, got, mesh.my_chip, "rs_chip_sum_last")
    shapes = [loss.shape if n == "loss" else small[n].shape for n in _SMALL_NAMES]
    return None, dx, big, dict(zip(_SMALL_NAMES, _sum_devices(every, shapes)))


def _pack_small_weights(conv_a_s, conv_b_s, conv_ffn_s):
    def body(a_ref, b_ref, f_ref, out):
        out[...] = jnp.zeros_like(out)
        at = 0
        for src in (a_ref, b_ref, f_ref):
            rows, cols = src.shape
            for r in range(rows):
                out[at:at + 1, 0:cols] = src[r:r + 1, :]
                at += 1

    return pl.pallas_call(body, name="small_weight_pack", out_shape=_sds((SMALL_W_ROWS, SMALL_W_COLS), F32))(
        conv_a_s, conv_b_s, conv_ffn_s)


def _unpack_small_weights(full):
    def take(r0, k, w):
        return jnp.transpose(full[:, r0:r0 + k, 0:w], (1, 0, 2)).reshape(k, N_DEV * w)

    return take(0, K_A, CONV_A_COLS), take(K_A, K_B, CONV_A_COLS), take(K_A + K_B, K_F, W_FF_COLS)


_SMALL_NAMES = ("conv_b_w", "conv_a_w", "conv_ffn_w", "norm_mix_g", "norm_ffn_g", "norm_final_g",
                "conv_b_b", "ln_b_g", "ln_b_b", "loss")
SMALL_G_ROWS = 64


def _small_rows(shapes):
    plan, at = [], 0
    for idx, (rows, cols) in enumerate(shapes):
        for r in range(rows):
            for c0 in range(0, cols, SMALL_G_COLS):
                plan.append((idx, r, c0, min(SMALL_G_COLS, cols - c0), at))
                at += 1
    assert at <= SMALL_G_ROWS
    return plan


def _pack_small_grads(small, loss):
    srcs = [loss if n == "loss" else small[n] for n in _SMALL_NAMES]
    plan = _small_rows([a.shape for a in srcs])

    def body(*refs):
        out = refs[-1]
        out[...] = jnp.zeros_like(out)
        for idx, r, c0, width, at in plan:
            out[at:at + 1, 0:width] = refs[idx][r:r + 1, c0:c0 + width]

    return pl.pallas_call(body, name="small_grad_pack", out_shape=_sds((SMALL_G_ROWS, SMALL_G_COLS), F32))(*srcs)


def kernel(x, norm_mix_g, w_in, conv_a_w, conv_b_w, conv_b_b, ln_b_g, ln_b_b, w_out, norm_ffn_g, w_gate, w_up, conv_ffn_w, w_down, norm_final_g, loss_target, m_norm_mix_g, m_w_in, m_conv_a_w, m_conv_b_w, m_conv_b_b, m_ln_b_g, m_ln_b_b, m_w_out, m_norm_ffn_g, m_w_gate, m_w_up, m_conv_ffn_w, m_w_down, m_norm_final_g, v_norm_mix_g, v_w_in, v_conv_a_w, v_conv_b_w, v_conv_b_b, v_ln_b_g, v_ln_b_b, v_w_out, v_norm_ffn_g, v_w_gate, v_w_up, v_conv_ffn_w, v_w_down, v_norm_final_g):
    ix, iy, ic = lax.axis_index("x"), lax.axis_index("y"), lax.axis_index("c")
    my_chip = (2 * ix + iy).astype(jnp.int32)
    my_core = ic.astype(jnp.int32)
    my_dev = 2 * my_chip + my_core

    weights = dict(norm_mix_g=norm_mix_g, w_in=w_in, conv_a_w=conv_a_w, conv_b_w=conv_b_w, conv_b_b=conv_b_b,
                   ln_b_g=ln_b_g, ln_b_b=ln_b_b, w_out=w_out, norm_ffn_g=norm_ffn_g, w_gate=w_gate, w_up=w_up,
                   conv_ffn_w=conv_ffn_w, w_down=w_down, norm_final_g=norm_final_g)
    m_in = dict(norm_mix_g=m_norm_mix_g, w_in=m_w_in, conv_a_w=m_conv_a_w, conv_b_w=m_conv_b_w, conv_b_b=m_conv_b_b,
                ln_b_g=m_ln_b_g, ln_b_b=m_ln_b_b, w_out=m_w_out, norm_ffn_g=m_norm_ffn_g, w_gate=m_w_gate,
                w_up=m_w_up, conv_ffn_w=m_conv_ffn_w, w_down=m_w_down, norm_final_g=m_norm_final_g)
    v_in = dict(norm_mix_g=v_norm_mix_g, w_in=v_w_in, conv_a_w=v_conv_a_w, conv_b_w=v_conv_b_w, conv_b_b=v_conv_b_b,
                ln_b_g=v_ln_b_g, ln_b_b=v_ln_b_b, w_out=v_w_out, norm_ffn_g=v_norm_ffn_g, w_gate=v_w_gate,
                w_up=v_w_up, conv_ffn_w=v_conv_ffn_w, w_down=v_w_down, norm_final_g=v_norm_final_g)
    order = list(weights)
    big_names = ("w_in", "w_gate", "w_up", "w_out", "w_down")
    transposed = ("w_in", "w_gate", "w_up")

    def shard2d(name, a):
        if name in transposed:
            return jnp.swapaxes(a[0], 0, 1)
        return a.reshape(1, a.shape[0]) if a.ndim == 1 else a.reshape(a.shape[-2:])

    def unshard2d(name, a2, like):
        if name in transposed:
            return jnp.swapaxes(a2, 0, 1)[None]
        return a2.reshape(like.shape)

    mesh = _Mesh(dict(zip(big_names, _cast_shards([shard2d(n, weights[n]) for n in big_names]))), my_chip, my_core)
    gathered, = _run_exchanges("ag_first", [_all_gather(
        [mesh.shards["w_in"], _pack_small_weights(conv_a_w[0], conv_b_w[0], conv_ffn_w[0])])])
    w_in_t = mesh.whole(gathered[0])
    wa_f, wb_f, wf_f = _unpack_small_weights(gathered[1])

    _, dx, gsum, stot = _step(
        x[0], loss_target[0], norm_mix_g, w_in_t, wa_f, wb_f, conv_b_b, ln_b_g, ln_b_b, None, norm_ffn_g,
        None, None, wf_f, None, norm_final_g.reshape(1, D_MODEL), SEQ_TILE, mesh)

    grads2d = dict(
        norm_mix_g=stot["norm_mix_g"],
        conv_a_w=lax.dynamic_slice(stot["conv_a_w"], (0, my_dev * CONV_A_COLS), (K_A, CONV_A_COLS)),
        conv_b_w=lax.dynamic_slice(stot["conv_b_w"], (0, my_dev * CONV_A_COLS), (K_B, CONV_A_COLS)),
        conv_b_b=stot["conv_b_b"], ln_b_g=stot["ln_b_g"], ln_b_b=stot["ln_b_b"],
        norm_ffn_g=stot["norm_ffn_g"],
        conv_ffn_w=lax.dynamic_slice(stot["conv_ffn_w"], (0, my_dev * W_FF_COLS), (K_F, W_FF_COLS)),
        norm_final_g=stot["norm_final_g"],
        **gsum,
    )

    updates = {}
    small_names = [n for n in order if n not in big_names]
    for group, label in [([n], "adamw_" + n) for n in big_names] + [(small_names, "adamw_small")]:
        outs = _adamw([shard2d(n, weights[n]) for n in group], [grads2d[n] for n in group],
                      [shard2d(n, m_in[n]) for n in group], [shard2d(n, v_in[n]) for n in group], label)
        for k, n in enumerate(group):
            updates[n] = outs[3 * k:3 * k + 3]
    g_out = [unshard2d(n, grads2d[n], weights[n]) for n in order]
    d_out, m_out, v_out = [[unshard2d(n, updates[n][j], weights[n]) for n in order] for j in range(3)]

    return (stot["loss"][0, 0], dx[None], *g_out, *d_out, *m_out, *v_out)
```

```python
import jax
import jax.numpy as jnp
from jax import lax
from jax.experimental import pallas as pl
from jax.experimental.pallas import tpu as pltpu

F32 = jnp.float32
BF16 = jnp.bfloat16

D_MODEL = 1024
D_A = 512
D_B = 512
D_IN = 3 * D_A + 2 * D_B
D_FF = 2816
K_A = 3
K_B = 31
K_F = 3
RMS_EPS = 1e-6
LN_EPS = 1e-5

ADAM_LR = 0.001
ADAM_B1 = 0.9
ADAM_B2 = 0.999
ADAM_EPS = 1e-08
ADAM_WD = 0.01
ADAM_STEP = 10

N_DEV = 8
N_CHIP = 4
LANES = 128
SUBLANES = 8
HALO = 16
ROW_CHUNK = 64
SEQ_TILE = 512
SKEW_TILE = 256
VMEM_LIMIT = 56 * 1024 * 1024

MESH = pl.DeviceIdType.MESH

W_FF_COLS = D_FF // N_DEV
CONV_A_COLS = D_A // N_DEV
SMALL_W_ROWS = 40
SMALL_W_COLS = 384
SMALL_G_COLS = 512
FF_PAD = 3072


def _rows(ts, c):
    return pl.BlockSpec((ts, c), lambda i: (i, 0))


def _const(shape):
    return pl.BlockSpec(shape, lambda i: (0,) * len(shape), pipeline_mode=pl.Buffered(1))


def _acc_out(shape):
    return pl.BlockSpec(shape, lambda i: (0,) * len(shape))


def _rows_at(ts, c, tile):
    return pl.BlockSpec((ts, c), lambda i: (tile(i), 0))


def _prev_at(ts, c, tile):
    return pl.BlockSpec((HALO, c), lambda i: (jnp.maximum(tile(i) * (ts // HALO) - 1, 0), 0))


def _next_at(ts, c, s, tile):
    last = s // HALO - 1
    return pl.BlockSpec((HALO, c), lambda i: (jnp.minimum((tile(i) + 1) * (ts // HALO), last), 0))


def _prev(ts, c):
    return _prev_at(ts, c, lambda i: i)


def _next(ts, c, s):
    return _next_at(ts, c, s, lambda i: i)


MXU_COLS = 256
MXU_ROWS = 256


def _col_pieces(n):
    return [(c0, min(MXU_COLS, n - c0)) for c0 in range(0, n, MXU_COLS)]


def _matmul_pieces(terms, out_ref, k_parts):
    m, n = out_ref.shape
    rows = min(MXU_ROWS, m)
    steps = []
    for lhs_ref, w_ref in terms:
        tiles = lhs_ref.shape[1] // MXU_COLS
        cuts = [MXU_COLS * (tiles * j // k_parts) for j in range(k_parts)] + [lhs_ref.shape[1]]
        steps += [(lhs_ref, w_ref, cuts[j], cuts[j + 1]) for j in range(k_parts)]

    def piece(m0, n0, width, step):
        lhs_ref, w_ref, k0, k1 = steps[step]
        part = jnp.dot(lhs_ref[m0:m0 + rows, k0:k1], w_ref[k0:k1, n0:n0 + width], preferred_element_type=F32)
        if step:
            part = part + out_ref[m0:m0 + rows, n0:n0 + width]
        out_ref[m0:m0 + rows, n0:n0 + width] = part

    return [(piece, (m0, n0, w, j)) for j in range(len(steps)) for n0, w in _col_pieces(n) for m0 in range(0, m, rows)]


def _interleaved(vector_units, matmul_pieces):
    n_u, n_p = len(vector_units), len(matmul_pieces)
    done = 0
    for k, (unit, args) in enumerate(vector_units):
        while done < n_p and done * n_u <= k * n_p:
            matmul_pieces[done][0](*matmul_pieces[done][1])
            done += 1
        unit(*args)
    for fn, args in matmul_pieces[done:]:
        fn(*args)


def _params():
    return pltpu.CompilerParams(dimension_semantics=("arbitrary",), vmem_limit_bytes=VMEM_LIMIT)


def _sds(shape, dtype):
    return jax.ShapeDtypeStruct(shape, dtype)


def _sigmoid(v):
    return 0.5 * jnp.tanh(0.5 * v) + 0.5


def _conv_block(ext_ref, w_ref, r0, rc, l0, k_taps, transposed):
    acc = None
    for k in range(k_taps):
        d = (k_taps // 2 - k) if transposed else (k - k_taps // 2)
        term = ext_ref[l0 // LANES, pl.ds(r0 + HALO + d, rc), :] * w_ref[k:k + 1, l0:l0 + LANES]
        acc = term if acc is None else acc + term
    return acc


def _conv_wgrad_block(acc_ref, dout, ext_ref, r0, rc, l0, k_taps, scale=None):
    for k in range(k_taps):
        prod = dout * ext_ref[l0 // LANES, pl.ds(r0 + HALO + k - k_taps // 2, rc), :]
        part = prod.reshape(rc // SUBLANES, SUBLANES, LANES).sum(axis=0)
        if scale is not None:
            part = part * scale
        acc_ref[k, :, l0:l0 + LANES] = acc_ref[k, :, l0:l0 + LANES] + part


def _reduce_acc(out_ref, acc_ref, k_taps):
    for k in range(k_taps):
        out_ref[k:k + 1, :] = jnp.sum(acc_ref[k], axis=0, keepdims=True)


def _fold8(v):
    rc, c = v.shape
    return v.reshape(rc // SUBLANES, SUBLANES, c).sum(axis=0)


def _ext_scratch(ts, c):
    return pltpu.VMEM((c // LANES, ts + 2 * HALO, LANES), F32)


def _put_rows(ext_ref, r0, rc, val):
    for q in range(val.shape[1] // LANES):
        ext_ref[q, pl.ds(r0 + HALO, rc), :] = val[:, q * LANES:(q + 1) * LANES]


def _fill_halo(ext_ref, vals_prev, vals_next, ts, first, last):
    for q in range(vals_prev.shape[1] // LANES):
        cols = slice(q * LANES, (q + 1) * LANES)
        ext_ref[q, 0:HALO, :] = jnp.where(first, 0.0, vals_prev[:, cols])
        ext_ref[q, HALO + ts:HALO + ts + HALO, :] = jnp.where(last, 0.0, vals_next[:, cols])


def _rms_bwd_rows(dh, xf, g):
    r = lax.rsqrt(jnp.mean(xf * xf, axis=-1, keepdims=True) + RMS_EPS)
    xhat = xf * r
    dxh = dh * g
    dx = r * (dxh - xhat * jnp.mean(dxh * xhat, axis=-1, keepdims=True))
    return dx, dh * xhat


_NT = (((1,), (1,)), ((), ()))
_TN = (((0,), (0,)), ((), ()))


def _fwd_in(x, g1, w_in_t, ts, exchange=None):
    s = x.shape[0]

    def body(x_ref, g_ref, w_ref, z_ref, h_ref):
        xf = x_ref[...]
        r = lax.rsqrt(jnp.mean(xf * xf, axis=-1, keepdims=True) + RMS_EPS)
        h = (xf * r * g_ref[...]).astype(BF16)
        h_ref[...] = h
        for n0 in range(0, D_IN, 512):
            z_ref[:, n0:n0 + 512] = lax.dot_general(h, w_ref[n0:n0 + 512, :], _NT,
                                                    preferred_element_type=F32).astype(BF16)

    return _call(
        body, name="fwd_in", grid=(s // ts,),
        in_specs=[_rows(ts, D_MODEL), _const((1, D_MODEL)), _const((D_IN, D_MODEL))],
        out_specs=[_rows(ts, D_IN), _rows(ts, D_MODEL)],
        out_shape=[_sds((s, D_IN), BF16), _sds((s, D_MODEL), BF16)],
        scratch_shapes=[], args=(x, g1, w_in_t), exchange=exchange, forward_step=(s // ts) * 3 // 4)


def _p_u0(z_ref, rows):
    a_h = z_ref[rows, 0:D_A].astype(F32)
    a_c = z_ref[rows, 2 * D_A:3 * D_A].astype(F32)
    b_v = z_ref[rows, 3 * D_A:3 * D_A + D_B].astype(F32)
    b_g = z_ref[rows, 3 * D_A + D_B:D_IN].astype(F32)
    return a_c * a_h, b_v * _sigmoid(b_g)


def _layernorm_rows(u_blocks):
    tot = None
    for ub in u_blocks:
        sm = jnp.sum(ub, axis=-1, keepdims=True)
        tot = sm if tot is None else tot + sm
    mu = tot * (1.0 / D_B)
    var = None
    for ub in u_blocks:
        sq = jnp.sum((ub - mu) * (ub - mu), axis=-1, keepdims=True)
        var = sq if var is None else var + sq
    rstd = lax.rsqrt(var * (1.0 / D_B) + LN_EPS)
    return mu, rstd


def _fwd_mix(z, x, wa, wb, bb, lg, lb, w_out, ts, exchange=None):
    s = x.shape[0]
    nt = s // ts
    rc = min(ROW_CHUNK, ts)

    def body(z_ref, zp_ref, zn_ref, x_ref, wa_ref, wb_ref, bb_ref, lg_ref, lb_ref, wo_ref,
             x2_ref, y_ref, u_ref, pe_ref, ue_ref):
        i = pl.program_id(0)
        pp, up = _p_u0(zp_ref, slice(None))
        pn, un = _p_u0(zn_ref, slice(None))
        _fill_halo(pe_ref, pp, pn, ts, i == 0, i == nt - 1)
        _fill_halo(ue_ref, up, un, ts, i == 0, i == nt - 1)

        def fill(j, carry):
            r0 = pl.multiple_of(j * rc, rc)
            p, u0 = _p_u0(z_ref, pl.ds(r0, rc))
            _put_rows(pe_ref, r0, rc, p)
            _put_rows(ue_ref, r0, rc, u0)
            return carry

        lax.fori_loop(0, ts // rc, fill, 0)

        def main(j, carry):
            r0 = pl.multiple_of(j * rc, rc)
            rows = pl.ds(r0, rc)
            for l0 in range(0, D_A, LANES):
                ca = _conv_block(pe_ref, wa_ref, r0, rc, l0, K_A, False)
                a_b = z_ref[rows, D_A + l0:D_A + l0 + LANES].astype(F32)
                y_ref[rows, l0:l0 + LANES] = (a_b * ca).astype(BF16)
            ubs = []
            for l0 in range(0, D_B, LANES):
                ub = _conv_block(ue_ref, wb_ref, r0, rc, l0, K_B, False) + bb_ref[:, l0:l0 + LANES]
                u_ref[rows, l0:l0 + LANES] = ub
                ubs.append(ub)
            mu, rstd = _layernorm_rows(ubs)
            for q, l0 in enumerate(range(0, D_B, LANES)):
                t = (ubs[q] - mu) * rstd * lg_ref[:, l0:l0 + LANES] + lb_ref[:, l0:l0 + LANES]
                y_ref[rows, D_A + l0:D_A + l0 + LANES] = (t * _sigmoid(t)).astype(BF16)
            return carry

        lax.fori_loop(0, ts // rc, main, 0, unroll=4)
        for n0 in range(0, D_MODEL, 512):
            x2_ref[:, n0:n0 + 512] = x_ref[:, n0:n0 + 512] + jnp.dot(
                y_ref[...], wo_ref[:, n0:n0 + 512], preferred_element_type=F32)

    return _call(
        body, name="fwd_mix", grid=(nt,),
        in_specs=[_rows(ts, D_IN), _prev(ts, D_IN), _next(ts, D_IN, s), _rows(ts, D_MODEL),
                  _const((K_A, D_A)), _const((K_B, D_B)), _const((1, D_B)), _const((1, D_B)), _const((1, D_B)),
                  _const((D_MODEL, D_MODEL))],
        out_specs=[_rows(ts, D_MODEL), _rows(ts, D_MODEL), _rows(ts, D_B)],
        out_shape=[_sds((s, D_MODEL), F32), _sds((s, D_MODEL), BF16), _sds((s, D_B), F32)],
        scratch_shapes=[_ext_scratch(ts, D_A), _ext_scratch(ts, D_B)],
        args=(z, z, z, x, wa, wb, bb, lg, lb, w_out), exchange=exchange, forward_step=nt * 5 // 8)


def _fwd_ffn_in(x2, g2, w_gate_t, w_up_t, ts, exchange=None):
    s = x2.shape[0]
    half = D_FF // 2

    def body(x_ref, g_ref, wg_ref, wu_ref, g0_ref, v_ref, h_ref):
        xf = x_ref[...]
        r = lax.rsqrt(jnp.mean(xf * xf, axis=-1, keepdims=True) + RMS_EPS)
        h = (xf * r * g_ref[...]).astype(BF16)
        h_ref[...] = h
        for n0 in range(0, D_FF, half):
            g0_ref[:, n0:n0 + half] = lax.dot_general(h, wg_ref[n0:n0 + half, :], _NT,
                                                      preferred_element_type=F32).astype(BF16)
            v_ref[:, n0:n0 + half] = lax.dot_general(h, wu_ref[n0:n0 + half, :], _NT,
                                                     preferred_element_type=F32).astype(BF16)

    return _call(
        body, name="fwd_ffn_in", grid=(s // ts,),
        in_specs=[_rows(ts, D_MODEL), _const((1, D_MODEL)), _const((D_FF, D_MODEL)), _const((D_FF, D_MODEL))],
        out_specs=[_rows(ts, D_FF), _rows(ts, D_FF), _rows(ts, D_MODEL)],
        out_shape=[_sds((s, D_FF), BF16), _sds((s, D_FF), BF16), _sds((s, D_MODEL), BF16)],
        scratch_shapes=[], args=(x2, g2, w_gate_t, w_up_t), exchange=exchange, forward_step=(s // ts) // 2)


def _ffn_out_and_back(g0, v, x2, wf, w_down, g3, target, ts):
    s = x2.shape[0]
    nt = s // ts
    rc = min(ROW_CHUNK, ts // 2)

    def body(g0_ref, gp_ref, gn_ref, v_ref, x2_ref, wf_ref, wd_ref, wdt_ref, g3_ref, t_ref,
             a_ref, dx3_ref, dx3b_ref, loss_ref, dg3_ref, dg_ref, dv_ref, dwf_ref,
             ge_ref, silu_ref, dsv_ref, da_ref, acc_ref, p_ref, sums_ref):
        i = pl.program_id(0)

        @pl.when(i == 0)
        def _():
            acc_ref[...] = jnp.zeros_like(acc_ref)
            sums_ref[...] = jnp.zeros_like(sums_ref)

        _fill_halo(ge_ref, gp_ref[...].astype(F32), gn_ref[...].astype(F32), ts, i == 0, i == nt - 1)

        def fill(j, carry):
            r0 = pl.multiple_of(j * rc, rc)
            _put_rows(ge_ref, r0, rc, g0_ref[pl.ds(r0, rc), :].astype(F32))
            return carry

        lax.fori_loop(0, ts // rc, fill, 0)

        def act(r0, l0):
            rows = slice(r0, r0 + rc)
            g = _conv_block(ge_ref, wf_ref, r0, rc, l0, K_F, False)
            vv = v_ref[rows, l0:l0 + LANES].astype(F32)
            sg = _sigmoid(g)
            silu = g * sg
            a_ref[rows, l0:l0 + LANES] = (silu * vv).astype(BF16)
            silu_ref[rows, l0:l0 + LANES] = silu
            dsv_ref[rows, l0:l0 + LANES] = (sg + silu * (1.0 - sg)) * vv

        def tail(r0):
            rows = slice(r0, r0 + rc)
            x3 = x2_ref[rows, :] + p_ref[rows, :]
            r = lax.rsqrt(jnp.mean(x3 * x3, axis=-1, keepdims=True) + RMS_EPS)
            xhat = x3 * r
            diff = xhat * g3_ref[...] - t_ref[rows, :]
            dout = diff * (1.0 / D_MODEL)
            dxh = dout * g3_ref[...]
            dx3 = r * (dxh - xhat * jnp.mean(dxh * xhat, axis=-1, keepdims=True))
            dx3_ref[rows, :] = dx3
            dx3b_ref[rows, :] = dx3.astype(BF16)
            sums_ref[0] = sums_ref[0] + _fold8(diff * diff)
            sums_ref[1] = sums_ref[1] + _fold8(dout * xhat)

        def back(r0, l0):
            rows = slice(r0, r0 + rc)
            da = da_ref[rows, l0:l0 + LANES]
            dv_ref[rows, l0:l0 + LANES] = (da * silu_ref[rows, l0:l0 + LANES]).astype(BF16)
            dgg = da * dsv_ref[rows, l0:l0 + LANES]
            dg_ref[rows, l0:l0 + LANES] = dgg.astype(BF16)
            _conv_wgrad_block(acc_ref, dgg, ge_ref, r0, rc, l0, K_F)

        hrows = ts // 2

        def units(fn, h, per_lane_block):
            starts = [h * hrows + q * rc for q in range(hrows // rc)]
            if per_lane_block:
                return [(fn, (r0, l0)) for r0 in starts for l0 in range(0, D_FF, LANES)]
            return [(fn, (r0,)) for r0 in starts]

        def rows_of(ref, h):
            return ref.at[pl.ds(h * hrows, hrows), :]

        def product(h):
            return _matmul_pieces([(rows_of(a_ref, h), wd_ref)], rows_of(p_ref, h), 2)

        def grad_a(h):
            return _matmul_pieces([(rows_of(dx3b_ref, h), wdt_ref)], rows_of(da_ref, h), 1)

        _interleaved(units(act, 0, True), [])
        _interleaved(units(act, 1, True), product(0))
        _interleaved(units(tail, 0, False), product(1))
        _interleaved(units(tail, 1, False), grad_a(0))
        _interleaved(units(back, 0, True), grad_a(1))
        _interleaved(units(back, 1, True), [])

        @pl.when(i == nt - 1)
        def _():
            _reduce_acc(dwf_ref, acc_ref, K_F)
            loss_ref[...] = (0.5 / D_MODEL) * jnp.sum(sums_ref[0], keepdims=True)
            dg3_ref[...] = jnp.sum(sums_ref[1], axis=0, keepdims=True)

    return pl.pallas_call(
        body, name="ffn_out_and_back", grid=(nt,),
        in_specs=[_rows(ts, D_FF), _prev(ts, D_FF), _next(ts, D_FF, s), _rows(ts, D_FF), _rows(ts, D_MODEL),
                  _const((K_F, D_FF)), _const((D_FF, D_MODEL)), _const((D_MODEL, D_FF)), _const((1, D_MODEL)),
                  _rows(ts, D_MODEL)],
        out_specs=[_rows(ts, D_FF), _rows(ts, D_MODEL), _rows(ts, D_MODEL), _acc_out((1, 1)), _acc_out((1, D_MODEL)),
                   _rows(ts, D_FF), _rows(ts, D_FF), _acc_out((K_F, D_FF))],
        out_shape=[_sds((s, D_FF), BF16), _sds((s, D_MODEL), F32), _sds((s, D_MODEL), BF16),
                   _sds((1, 1), F32), _sds((1, D_MODEL), F32),
                   _sds((s, D_FF), BF16), _sds((s, D_FF), BF16), _sds((K_F, D_FF), F32)],
        scratch_shapes=[_ext_scratch(ts, D_FF), pltpu.VMEM((ts, D_FF), F32), pltpu.VMEM((ts, D_FF), F32),
                        pltpu.VMEM((ts, D_FF), F32), pltpu.VMEM((K_F, SUBLANES, D_FF), F32),
                        pltpu.VMEM((ts, D_MODEL), F32), pltpu.VMEM((2, SUBLANES, D_MODEL), F32)],
        compiler_params=_params(),
    )(g0, g0, g0, v, x2, wf, w_down, jnp.swapaxes(w_down, 0, 1), g3, target)


def _bwd_ffn_b(dg, dv, wf, w_gate, w_up, x2, g2, dx3, ts, exchange=None):
    s = x2.shape[0]
    nt = s // ts
    rc = min(ROW_CHUNK, ts)
    n_sub = ts // rc

    def body(dg_ref, dgp_ref, dgn_ref, dv_ref, wf_ref, wg_ref, wu_ref, x2_ref, g2_ref, dx3_ref,
             dg0_ref, dx2_ref, dx2b_ref, dgn2_ref, dge_ref, dg8_ref, a0_ref, a1_ref, p0_ref, p1_ref):
        i = pl.program_id(0)
        vt = jnp.minimum(i, nt - 1)
        live = (i >= 2).astype(F32)

        @pl.when(i == 0)
        def _():
            dg8_ref[...] = jnp.zeros_like(dg8_ref)
            a1_ref[...] = jnp.zeros_like(a1_ref)
            p1_ref[...] = jnp.zeros_like(p1_ref)

        _fill_halo(dge_ref, dgp_ref[...].astype(F32), dgn_ref[...].astype(F32), ts, vt == 0, vt == nt - 1)

        def fill(j, carry):
            r0 = pl.multiple_of(j * rc, rc)
            _put_rows(dge_ref, r0, rc, dg_ref[pl.ds(r0, rc), :].astype(F32))
            return carry

        lax.fori_loop(0, n_sub, fill, 0)

        def stage(a_new, a_old, p_new, p_old):
            def conv_t(r0, l0):
                rows = slice(r0, r0 + rc)
                dg0 = _conv_block(dge_ref, wf_ref, r0, rc, l0, K_F, True).astype(BF16)
                dg0_ref[rows, l0:l0 + LANES] = dg0
                a_new[rows, l0:l0 + LANES] = dg0

            def tail(r0):
                rows = slice(r0, r0 + rc)
                dx, dgrow = _rms_bwd_rows(p_old[rows, :], x2_ref[rows, :], g2_ref[...])
                dx2 = dx3_ref[rows, :] + dx
                dx2_ref[rows, :] = dx2
                dx2b_ref[rows, :] = dx2.astype(BF16)
                dg8_ref[...] = dg8_ref[...] + _fold8(dgrow) * live

            units = []
            for q in range(n_sub):
                units += [(conv_t, (q * rc, l0)) for l0 in range(0, D_FF, LANES)]
                units.append((tail, (q * rc,)))
            _interleaved(units, _matmul_pieces([(a_old, wg_ref), (dv_ref, wu_ref)], p_new, 2))

        @pl.when(i % 2 == 0)
        def _():
            stage(a0_ref, a1_ref, p0_ref, p1_ref)

        @pl.when(i % 2 == 1)
        def _():
            stage(a1_ref, a0_ref, p1_ref, p0_ref)

        @pl.when(i == nt + 1)
        def _():
            dgn2_ref[...] = jnp.sum(dg8_ref[...], axis=0, keepdims=True)

    vtile = lambda i: jnp.minimum(i, nt - 1)
    mtile = lambda i: jnp.clip(i - 1, 0, nt - 1)
    ttile = lambda i: jnp.clip(i - 2, 0, nt - 1)
    return _call(
        body, name="bwd_ffn_b", grid=(nt + 2,),
        in_specs=[_rows_at(ts, D_FF, vtile), _prev_at(ts, D_FF, vtile), _next_at(ts, D_FF, s, vtile),
                  _rows_at(ts, D_FF, mtile), _const((K_F, D_FF)),
                  _const((D_FF, D_MODEL)), _const((D_FF, D_MODEL)), _rows_at(ts, D_MODEL, ttile), _const((1, D_MODEL)),
                  _rows_at(ts, D_MODEL, ttile)],
        out_specs=[_rows_at(ts, D_FF, vtile), _rows_at(ts, D_MODEL, ttile), _rows_at(ts, D_MODEL, ttile),
                   _acc_out((1, D_MODEL))],
        out_shape=[_sds((s, D_FF), BF16), _sds((s, D_MODEL), F32), _sds((s, D_MODEL), BF16), _sds((1, D_MODEL), F32)],
        scratch_shapes=[_ext_scratch(ts, D_FF), pltpu.VMEM((SUBLANES, D_MODEL), F32),
                        pltpu.VMEM((ts, D_FF), BF16), pltpu.VMEM((ts, D_FF), BF16),
                        pltpu.VMEM((ts, D_MODEL), F32), pltpu.VMEM((ts, D_MODEL), F32)],
        args=(dg, dg, dg, dv, wf, w_gate, w_up, x2, g2, dx3), exchange=exchange)


def _bwd_mix_a(dx2b, w_out_t, z, u, wa, lg, lb, ts, exchange=None):
    s = dx2b.shape[0]
    nt = s // ts
    rc = min(ROW_CHUNK, ts)
    n_sub = ts // rc

    def body(dx_ref, wo_ref, z_ref, zp_ref, zn_ref, u_ref, wa_ref, lg_ref, lb_ref,
             dca_ref, du_ref, dab_ref, dwa_ref, dlg_ref, dlb_ref, dbb_ref, pe_ref, dy0_ref, dy1_ref, acc_ref, sacc_ref):
        i = pl.program_id(0)
        t = jnp.maximum(i - 1, 0)

        @pl.when(i == 0)
        def _():
            acc_ref[...] = jnp.zeros_like(acc_ref)
            sacc_ref[...] = jnp.zeros_like(sacc_ref)
            dy1_ref[...] = jnp.zeros_like(dy1_ref)

        pp, _ = _p_u0(zp_ref, slice(None))
        pn, _ = _p_u0(zn_ref, slice(None))
        _fill_halo(pe_ref, pp, pn, ts, t == 0, t == nt - 1)

        def fill(j, carry):
            r0 = pl.multiple_of(j * rc, rc)
            rows = pl.ds(r0, rc)
            _put_rows(pe_ref, r0, rc, z_ref[rows, 2 * D_A:3 * D_A].astype(F32) * z_ref[rows, 0:D_A].astype(F32))
            return carry

        lax.fori_loop(0, n_sub, fill, 0)

        def stage(dy_new, dy_old):
            def piece(m0, n0, width):
                dy_new[m0:m0 + MXU_ROWS, n0:n0 + width] = jnp.dot(
                    dx_ref[m0:m0 + MXU_ROWS, :], wo_ref[:, n0:n0 + width], preferred_element_type=F32).astype(BF16)

            units = []
            for q in range(n_sub):
                units += [(mixer_a, (dy_old, q * rc, l0)) for l0 in range(0, D_A, LANES)]
                units.append((mixer_b, (dy_old, q * rc)))
            _interleaved(units, [(piece, (m0, n0, w)) for n0, w in _col_pieces(D_MODEL) for m0 in range(0, ts, MXU_ROWS)])

        def mixer_a(dy_ref, r0, l0):
            rows = slice(r0, r0 + rc)
            ca = _conv_block(pe_ref, wa_ref, r0, rc, l0, K_A, False)
            a_b = z_ref[rows, D_A + l0:D_A + l0 + LANES].astype(F32)
            dya = dy_ref[rows, l0:l0 + LANES].astype(F32)
            dab_ref[rows, l0:l0 + LANES] = (dya * ca).astype(BF16)
            dca = dya * a_b
            dca_ref[rows, l0:l0 + LANES] = dca
            _conv_wgrad_block(acc_ref, dca, pe_ref, r0, rc, l0, K_A)

        def mixer_b(dy_ref, r0):
            rows = slice(r0, r0 + rc)
            ubs = [u_ref[rows, l0:l0 + LANES] for l0 in range(0, D_B, LANES)]
            mu, rstd = _layernorm_rows(ubs)
            ns, dns = [], []
            m1 = None
            m2 = None
            for q, l0 in enumerate(range(0, D_B, LANES)):
                n = (ubs[q] - mu) * rstd
                lgq = lg_ref[:, l0:l0 + LANES]
                t = n * lgq + lb_ref[:, l0:l0 + LANES]
                sg = _sigmoid(t)
                dt = dy_ref[rows, D_A + l0:D_A + l0 + LANES].astype(F32) * (sg * (1.0 + t * (1.0 - sg)))
                dn = dt * lgq
                ns.append(n)
                dns.append(dn)
                s1 = jnp.sum(dn, axis=-1, keepdims=True)
                s2 = jnp.sum(dn * n, axis=-1, keepdims=True)
                m1 = s1 if m1 is None else m1 + s1
                m2 = s2 if m2 is None else m2 + s2
                sacc_ref[0, :, l0:l0 + LANES] = sacc_ref[0, :, l0:l0 + LANES] + _fold8(dt * n)
                sacc_ref[1, :, l0:l0 + LANES] = sacc_ref[1, :, l0:l0 + LANES] + _fold8(dt)
            m1 = m1 * (1.0 / D_B)
            m2 = m2 * (1.0 / D_B)
            for q, l0 in enumerate(range(0, D_B, LANES)):
                du = rstd * (dns[q] - m1 - ns[q] * m2)
                du_ref[rows, l0:l0 + LANES] = du
                sacc_ref[2, :, l0:l0 + LANES] = sacc_ref[2, :, l0:l0 + LANES] + _fold8(du)

        @pl.when(i % 2 == 0)
        def _():
            stage(dy0_ref, dy1_ref)

        @pl.when(i % 2 == 1)
        def _():
            stage(dy1_ref, dy0_ref)

        @pl.when(i == nt)
        def _():
            _reduce_acc(dwa_ref, acc_ref, K_A)
            dlg_ref[...] = jnp.sum(sacc_ref[0], axis=0, keepdims=True)
            dlb_ref[...] = jnp.sum(sacc_ref[1], axis=0, keepdims=True)
            dbb_ref[...] = jnp.sum(sacc_ref[2], axis=0, keepdims=True)

    cur = lambda i: jnp.minimum(i, nt - 1)
    old = lambda i: jnp.maximum(i - 1, 0)
    return _call(
        body, name="bwd_mix_a", grid=(nt + 1,),
        in_specs=[_rows_at(ts, D_MODEL, cur), _const((D_MODEL, D_MODEL)), _rows_at(ts, D_IN, old), _prev_at(ts, D_IN, old),
                  _next_at(ts, D_IN, s, old), _rows_at(ts, D_B, old), _const((K_A, D_A)), _const((1, D_B)), _const((1, D_B))],
        out_specs=[_rows_at(ts, D_A, old), _rows_at(ts, D_B, old), _rows_at(ts, D_A, old), _acc_out((K_A, D_A)),
                   _acc_out((1, D_B)), _acc_out((1, D_B)), _acc_out((1, D_B))],
        out_shape=[_sds((s, D_A), F32), _sds((s, D_B), F32), _sds((s, D_A), BF16), _sds((K_A, D_A), F32),
                   _sds((1, D_B), F32), _sds((1, D_B), F32), _sds((1, D_B), F32)],
        scratch_shapes=[_ext_scratch(ts, D_A), pltpu.VMEM((ts, D_MODEL), BF16), pltpu.VMEM((ts, D_MODEL), BF16),
                        pltpu.VMEM((K_A, SUBLANES, D_A), F32), pltpu.VMEM((3, SUBLANES, D_B), F32)],
        args=(dx2b, w_out_t, z, z, z, u, wa, lg, lb), exchange=exchange)


def _bwd_mix_b(dca, du, z, dab, wa, wb, w_in, x, g1, dx2, ts, exchange=None):
    s = x.shape[0]
    nt = s // ts
    rc = min(ROW_CHUNK, ts)
    n_sub = ts // rc

    def body(dca_ref, dcap_ref, dcan_ref, du_ref, dup_ref, dun_ref, z_ref, zp_ref, zn_ref, dab_ref,
             wa_ref, wb_ref, wi_ref, x_ref, g1_ref, dx2_ref,
             dz_ref, dx_ref, dg1_ref, dwb_ref, dcae_ref, due_ref, ue_ref, acc_ref, dg8_ref,
             dz0_ref, dz1_ref, dh0_ref, dh1_ref):
        i = pl.program_id(0)

        @pl.when(i == 0)
        def _():
            acc_ref[...] = jnp.zeros_like(acc_ref)
            dg8_ref[...] = jnp.zeros_like(dg8_ref)
            dz1_ref[...] = jnp.zeros_like(dz1_ref)
            dh1_ref[...] = jnp.zeros_like(dh1_ref)

        vt = jnp.minimum(i, nt - 1)
        first = vt == 0
        last = vt == nt - 1
        live = (i < nt).astype(F32)
        _fill_halo(dcae_ref, dcap_ref[...], dcan_ref[...], ts, first, last)
        _fill_halo(due_ref, dup_ref[...], dun_ref[...], ts, first, last)
        _, up = _p_u0(zp_ref, slice(None))
        _, un = _p_u0(zn_ref, slice(None))
        _fill_halo(ue_ref, up, un, ts, first, last)

        def fill(j, carry):
            r0 = pl.multiple_of(j * rc, rc)
            rows = pl.ds(r0, rc)
            _put_rows(dcae_ref, r0, rc, dca_ref[rows, :])
            _put_rows(due_ref, r0, rc, du_ref[rows, :])
            b_v = z_ref[rows, 3 * D_A:3 * D_A + D_B].astype(F32)
            b_g = z_ref[rows, 3 * D_A + D_B:D_IN].astype(F32)
            _put_rows(ue_ref, r0, rc, b_v * _sigmoid(b_g))
            return carry

        lax.fori_loop(0, n_sub, fill, 0)

        def stage(dz_new, dz_old, dh_new, dh_old):
            def put(rows, c0, val):
                dz_ref[rows, c0:c0 + LANES] = val
                dz_new[rows, c0:c0 + LANES] = val

            def mixer_a(r0, l0):
                rows = slice(r0, r0 + rc)
                dp = _conv_block(dcae_ref, wa_ref, r0, rc, l0, K_A, True)
                a_h = z_ref[rows, l0:l0 + LANES].astype(F32)
                a_c = z_ref[rows, 2 * D_A + l0:2 * D_A + l0 + LANES].astype(F32)
                put(rows, l0, (dp * a_c).astype(BF16))
                put(rows, D_A + l0, dab_ref[rows, l0:l0 + LANES])
                put(rows, 2 * D_A + l0, (dp * a_h).astype(BF16))

            def mixer_b(r0, l0):
                rows = slice(r0, r0 + rc)
                du0 = _conv_block(due_ref, wb_ref, r0, rc, l0, K_B, True)
                b_v = z_ref[rows, 3 * D_A + l0:3 * D_A + l0 + LANES].astype(F32)
                b_g = z_ref[rows, 3 * D_A + D_B + l0:3 * D_A + D_B + l0 + LANES].astype(F32)
                sg = _sigmoid(b_g)
                put(rows, 3 * D_A + l0, (du0 * sg).astype(BF16))
                put(rows, 3 * D_A + D_B + l0, (du0 * b_v * (sg * (1.0 - sg))).astype(BF16))
                _conv_wgrad_block(acc_ref, du_ref[rows, l0:l0 + LANES], ue_ref, r0, rc, l0, K_B, live)

            def tail(r0):
                rows = slice(r0, r0 + rc)
                dx, dgrow = _rms_bwd_rows(dh_old[rows, :], x_ref[rows, :], g1_ref[...])
                dx_ref[rows, :] = dx2_ref[rows, :] + dx
                dg8_ref[...] = dg8_ref[...] + _fold8(dgrow)

            units = []
            for q in range(n_sub):
                units += [(mixer_a, (q * rc, l0)) for l0 in range(0, D_A, LANES)]
                units += [(mixer_b, (q * rc, l0)) for l0 in range(0, D_B, LANES)]
                units.append((tail, (q * rc,)))
            _interleaved(units, _matmul_pieces([(dz_old, wi_ref)], dh_new, 2))

        @pl.when(i % 2 == 0)
        def _():
            stage(dz0_ref, dz1_ref, dh0_ref, dh1_ref)

        @pl.when(i % 2 == 1)
        def _():
            stage(dz1_ref, dz0_ref, dh1_ref, dh0_ref)

        @pl.when(i == nt + 1)
        def _():
            _reduce_acc(dwb_ref, acc_ref, K_B)
            dg1_ref[...] = jnp.sum(dg8_ref[...], axis=0, keepdims=True)

    vtile = lambda i: jnp.minimum(i, nt - 1)
    ttile = lambda i: jnp.clip(i - 2, 0, nt - 1)
    return _call(
        body, name="bwd_mix_b", grid=(nt + 2,),
        in_specs=[_rows_at(ts, D_A, vtile), _prev_at(ts, D_A, vtile), _next_at(ts, D_A, s, vtile),
                  _rows_at(ts, D_B, vtile), _prev_at(ts, D_B, vtile), _next_at(ts, D_B, s, vtile),
                  _rows_at(ts, D_IN, vtile), _prev_at(ts, D_IN, vtile), _next_at(ts, D_IN, s, vtile), _rows_at(ts, D_A, vtile),
                  _const((K_A, D_A)), _const((K_B, D_B)), _const((D_IN, D_MODEL)), _rows_at(ts, D_MODEL, ttile),
                  _const((1, D_MODEL)), _rows_at(ts, D_MODEL, ttile)],
        out_specs=[_rows_at(ts, D_IN, vtile), _rows_at(ts, D_MODEL, ttile), _acc_out((1, D_MODEL)), _acc_out((K_B, D_B))],
        out_shape=[_sds((s, D_IN), BF16), _sds((s, D_MODEL), F32), _sds((1, D_MODEL), F32), _sds((K_B, D_B), F32)],
        scratch_shapes=[_ext_scratch(ts, D_A), _ext_scratch(ts, D_B), _ext_scratch(ts, D_B),
                        pltpu.VMEM((K_B, SUBLANES, D_B), F32), pltpu.VMEM((SUBLANES, D_MODEL), F32),
                        pltpu.VMEM((ts, D_IN), BF16), pltpu.VMEM((ts, D_IN), BF16),
                        pltpu.VMEM((ts, D_MODEL), F32), pltpu.VMEM((ts, D_MODEL), F32)],
        args=(dca, dca, dca, du, du, du, z, z, z, dab, wa, wb, w_in, x, g1, dx2), exchange=exchange)


def _matmul_tn(a, b, name, exchange=None):
    s, m = a.shape
    n = b.shape[1]
    tk = min(1024, s)
    nk = s // tk
    tm = 256

    def body(a_ref, b_ref, o_ref, acc_ref):
        k = pl.program_id(0)

        @pl.when(k == 0)
        def _():
            acc_ref[...] = jnp.zeros_like(acc_ref)

        for m0 in range(0, m, tm):
            acc_ref[m0:m0 + tm, :] = acc_ref[m0:m0 + tm, :] + lax.dot_general(
                a_ref[:, m0:m0 + tm], b_ref[...], _TN, preferred_element_type=F32)

        @pl.when(k == nk - 1)
        def _():
            o_ref[...] = acc_ref[...].astype(BF16)

    (out,), got = _call(
        body, name=name, grid=(nk,),
        in_specs=[_rows(tk, m), _rows(tk, n)],
        out_specs=[_acc_out((m, n))],
        out_shape=[_sds((m, n), BF16)],
        scratch_shapes=[pltpu.VMEM((m, n), F32)], args=(a, b), exchange=exchange)
    return out if exchange is None else (out, got)


CHIP_RELS = ((1, 0, 0), (0, 1, 0), (1, 1, 0))
CORE_RELS = ((0, 0, 1),)
ALL_RELS = ((0, 0, 1), (0, 1, 0), (0, 1, 1), (1, 0, 0), (1, 0, 1), (1, 1, 0), (1, 1, 1))


def _chip_slot(dev):
    return 2 * dev[0] + dev[1]


def _dev_slot(dev):
    return 4 * dev[0] + 2 * dev[1] + dev[2]


def _me():
    return (lax.axis_index("x"), lax.axis_index("y"), lax.axis_index("c"))


def _peer(me, rel):
    return tuple((1 - me[a]) if rel[a] else me[a] for a in range(3))


_ANY = pl.BlockSpec(memory_space=pl.ANY)


class _Exchange:
    def __init__(self, inputs, out_shape, scratch, start, finish, forward=None):
        self.inputs, self.out_shape, self.scratch = list(inputs), list(out_shape), list(scratch)
        self.start, self.finish, self.forward = start, finish, forward


def _all_gather(payloads):
    n_p = len(payloads)
    n_k = 1 + 2 * len(CHIP_RELS)

    def copy(srcs, dsts, sems, p, k, block_dev, to, from_src):
        blk = dsts[p].at[_dev_slot(block_dev)]
        return pltpu.make_async_remote_copy(
            src_ref=srcs[p] if from_src else blk, dst_ref=blk,
            send_sem=sems[0].at[n_k * p + k], recv_sem=sems[1].at[n_k * p + k], device_id=to, device_id_type=MESH)

    def own_copy(srcs, dsts, sems, p):
        return pltpu.make_async_copy(srcs[p], dsts[p].at[_dev_slot(_me())], sems[2].at[p])

    def start(srcs, dsts, sems):
        me = _me()
        for p in range(n_p):
            own_copy(srcs, dsts, sems, p).start()
        for j, rel in enumerate(CHIP_RELS):
            for p in range(n_p):
                copy(srcs, dsts, sems, p, 1 + j, me, _peer(me, rel), True).start()
        for p in range(n_p):
            copy(srcs, dsts, sems, p, 0, me, _peer(me, CORE_RELS[0]), True).start()

    def forward(srcs, dsts, sems):
        me = _me()
        sibling = _peer(me, CORE_RELS[0])
        for j, rel in enumerate(CHIP_RELS):
            other = _peer(me, rel)
            for p in range(n_p):
                copy(srcs, dsts, sems, p, 1 + j, other, me, False).wait_recv()
                copy(srcs, dsts, sems, p, 4 + j, other, sibling, False).start()

    def finish(srcs, dsts, sems):
        me = _me()
        sibling = _peer(me, CORE_RELS[0])
        for p in range(n_p):
            copy(srcs, dsts, sems, p, 0, sibling, me, False).wait_recv()
        for j, rel in enumerate(CHIP_RELS):
            for p in range(n_p):
                copy(srcs, dsts, sems, p, 4 + j, _peer(sibling, rel), me, False).wait_recv()
        for p in range(n_p):
            own_copy(srcs, dsts, sems, p).wait()
            copy(srcs, dsts, sems, p, 0, me, sibling, True).wait_send()
            for j, rel in enumerate(CHIP_RELS):
                copy(srcs, dsts, sems, p, 1 + j, me, _peer(me, rel), True).wait_send()
                copy(srcs, dsts, sems, p, 4 + j, _peer(me, rel), sibling, False).wait_send()

    return _Exchange(
        payloads, [_sds((N_DEV,) + p.shape, p.dtype) for p in payloads],
        [pltpu.SemaphoreType.DMA((n_p * n_k,)), pltpu.SemaphoreType.DMA((n_p * n_k,)), pltpu.SemaphoreType.DMA((n_p,))],
        start, finish, forward)


def _gather_direct(payload):
    n_r = len(ALL_RELS)

    def copies(srcs, dsts, sems):
        me = _me()
        mine = dsts[0].at[_dev_slot(me)]
        own = pltpu.make_async_copy(srcs[0], mine, sems[2].at[0])
        remote = [pltpu.make_async_remote_copy(src_ref=srcs[0], dst_ref=mine, send_sem=sems[0].at[k], recv_sem=sems[1].at[k],
                                               device_id=_peer(me, rel), device_id_type=MESH)
                  for k, rel in enumerate(ALL_RELS)]
        return [own] + remote

    def start(srcs, dsts, sems):
        for cp in copies(srcs, dsts, sems):
            cp.start()

    def finish(srcs, dsts, sems):
        for cp in copies(srcs, dsts, sems):
            cp.wait()

    return _Exchange([payload], [_sds((N_DEV,) + payload.shape, payload.dtype)],
                     [pltpu.SemaphoreType.DMA((n_r,)), pltpu.SemaphoreType.DMA((n_r,)), pltpu.SemaphoreType.DMA((1,))],
                     start, finish)


def _scatter_exchange(payloads, rels, src_view, view_shapes):
    n_p = len(payloads)
    n_r = len(rels)

    def copies(srcs, dsts, sems):
        me = _me()
        out = []
        for k, rel in enumerate(rels):
            peer = _peer(me, rel)
            for p in range(n_p):
                out.append(pltpu.make_async_remote_copy(
                    src_ref=src_view(srcs[p], peer), dst_ref=dsts[p].at[k],
                    send_sem=sems[0].at[p * n_r + k], recv_sem=sems[1].at[p * n_r + k],
                    device_id=peer, device_id_type=MESH))
        return out

    def start(srcs, dsts, sems):
        for cp in copies(srcs, dsts, sems):
            cp.start()

    def finish(srcs, dsts, sems):
        for cp in copies(srcs, dsts, sems):
            cp.wait()

    return _Exchange(payloads, [_sds((n_r,) + vs, p.dtype) for vs, p in zip(view_shapes, payloads)],
                     [pltpu.SemaphoreType.DMA((n_p * n_r,)), pltpu.SemaphoreType.DMA((n_p * n_r,))], start, finish)


def _split_refs(refs, sizes):
    out, at = [], 0
    for n in sizes:
        out.append(refs[at:at + n])
        at += n
    return out


def _join(exchanges):
    n_in = [len(e.inputs) for e in exchanges]
    n_out = [len(e.out_shape) for e in exchanges]
    n_sc = [len(e.scratch) for e in exchanges]

    def phase(name):
        def run(ins, outs, scs):
            for e, i, o, s in zip(exchanges, _split_refs(ins, n_in), _split_refs(outs, n_out), _split_refs(scs, n_sc)):
                if getattr(e, name) is not None:
                    getattr(e, name)(i, o, s)
        return run

    return _Exchange([a for e in exchanges for a in e.inputs], [s for e in exchanges for s in e.out_shape],
                     [s for e in exchanges for s in e.scratch], phase("start"), phase("finish"),
                     phase("forward") if any(e.forward is not None for e in exchanges) else None)


def _run_exchanges(name, exchanges):
    n_in = [len(e.inputs) for e in exchanges]
    n_out = [len(e.out_shape) for e in exchanges]
    n_sc = [len(e.scratch) for e in exchanges]

    def body(*refs):
        ins, outs, scs = _split_refs(refs, [sum(n_in), sum(n_out), sum(n_sc)])
        parts = list(zip(exchanges, _split_refs(ins, n_in), _split_refs(outs, n_out), _split_refs(scs, n_sc)))
        for e, i, o, s in parts:
            e.start(i, o, s)
        for e, i, o, s in parts:
            if e.forward is not None:
                e.forward(i, o, s)
        for e, i, o, s in parts:
            e.finish(i, o, s)

    outs = pl.pallas_call(
        body, name=name, in_specs=[_ANY] * sum(n_in), out_specs=[_ANY] * sum(n_out),
        out_shape=[sd for e in exchanges for sd in e.out_shape],
        scratch_shapes=[sc for e in exchanges for sc in e.scratch],
    )(*[a for e in exchanges for a in e.inputs])
    return _split_refs(list(outs), n_out)


def _call(body, *, name, grid, in_specs, out_specs, out_shape, scratch_shapes, args, exchange=None, forward_step=None):
    n_in, n_out, n_sc = len(in_specs), len(out_specs), len(scratch_shapes)
    if exchange is None:
        outs = pl.pallas_call(body, name=name, grid=grid, in_specs=in_specs, out_specs=out_specs, out_shape=out_shape,
                              scratch_shapes=scratch_shapes, compiler_params=_params())(*args)
        return list(outs), []
    e = exchange
    sizes = [n_in, len(e.inputs), n_out, len(e.out_shape), n_sc, len(e.scratch)]
    last = grid[0] - 1

    def wrapped(*refs):
        a, ei, o, eo, sc, es = _split_refs(refs, sizes)
        i = pl.program_id(0)

        @pl.when(i == 0)
        def _():
            e.start(ei, eo, es)

        if e.forward is not None:
            @pl.when(i == forward_step)
            def _():
                e.forward(ei, eo, es)

        body(*a, *o, *sc)

        @pl.when(i == last)
        def _():
            e.finish(ei, eo, es)

    outs = pl.pallas_call(
        wrapped, name=name, grid=grid,
        in_specs=list(in_specs) + [_ANY] * len(e.inputs), out_specs=list(out_specs) + [_ANY] * len(e.out_shape),
        out_shape=list(out_shape) + e.out_shape, scratch_shapes=list(scratch_shapes) + e.scratch,
        compiler_params=_params(),
    )(*args, *e.inputs)
    outs = list(outs)
    return outs[:n_out], outs[n_out:]


def _pair_sum(grads, recvd, my_core, name):
    n_p = len(grads)

    def body(c_ref, *refs):
        del c_ref
        for p in range(n_p):
            refs[2 * n_p + p][...] = (refs[p][...].astype(F32) + refs[n_p + p][...].astype(F32)).astype(BF16)

    def blk(g):
        return (None, None) + g.shape[2:]

    return pl.pallas_call(
        body, name=name,
        grid_spec=pltpu.PrefetchScalarGridSpec(
            num_scalar_prefetch=1, grid=(N_CHIP,),
            in_specs=[pl.BlockSpec(blk(g), lambda j, c: (j, c[0], 0, 0)) for g in grads]
            + [pl.BlockSpec(blk(g), lambda j, c: (0, j, 0, 0)) for g in grads],
            out_specs=[pl.BlockSpec((None,) + g.shape[2:], lambda j, c: (j, 0, 0)) for g in grads]),
        out_shape=[_sds((N_CHIP,) + g.shape[2:], BF16) for g in grads],
        compiler_params=pltpu.CompilerParams(dimension_semantics=("arbitrary",), vmem_limit_bytes=VMEM_LIMIT),
    )(my_core, *grads, *recvd)


def _chip_sum(psums, recvd, my_chip, name):
    n_p = len(psums)

    def body(c_ref, *refs):
        del c_ref
        for p in range(n_p):
            acc = refs[p][...].astype(F32)
            for k in range(len(CHIP_RELS)):
                acc = acc + refs[n_p + p][k].astype(F32)
            refs[2 * n_p + p][...] = acc

    return pl.pallas_call(
        body, name=name,
        grid_spec=pltpu.PrefetchScalarGridSpec(
            num_scalar_prefetch=1, grid=(1,),
            in_specs=[pl.BlockSpec((None,) + g.shape[1:], lambda i, c: (c[0], 0, 0)) for g in psums]
            + [pl.BlockSpec(r.shape, lambda i, c: (0, 0, 0)) for r in recvd],
            out_specs=[pl.BlockSpec(g.shape[1:], lambda i, c: (0, 0)) for g in psums]),
        out_shape=[_sds(g.shape[1:], F32) for g in psums],
        compiler_params=pltpu.CompilerParams(dimension_semantics=("arbitrary",), vmem_limit_bytes=VMEM_LIMIT),
    )(my_chip, *psums, *recvd)


def _sum_devices(parts, shapes):
    plan = _small_rows(shapes)

    def body(p_ref, *refs):
        outs, tot_ref = refs[:-1], refs[-1]
        acc = p_ref[0]
        for j in range(1, N_DEV):
            acc = acc + p_ref[j]
        tot_ref[...] = acc
        for idx, r, c0, width, at in plan:
            outs[idx][r:r + 1, c0:c0 + width] = tot_ref[at:at + 1, 0:width]

    return pl.pallas_call(body, name="small_grad_sum", out_shape=[_sds(s, F32) for s in shapes],
                          scratch_shapes=[pltpu.VMEM(parts.shape[1:], F32)])(parts)


def _cast_shards(shards):
    def body(*refs):
        for src, dst in zip(refs[:len(shards)], refs[len(shards):]):
            dst[...] = src[...].astype(BF16)

    return pl.pallas_call(body, name="cast_shards", out_shape=[_sds(a.shape, BF16) for a in shards],
                          compiler_params=pltpu.CompilerParams(vmem_limit_bytes=VMEM_LIMIT))(*shards)


def _adamw(ws, gs, ms, vs, name):
    n_t = len(ws)

    def body(*refs):
        w_refs, g_refs, m_refs, v_refs = (refs[j * n_t:(j + 1) * n_t] for j in range(4))
        outs = refs[4 * n_t:]
        for k in range(n_t):
            gg = g_refs[k][...]
            mn = ADAM_B1 * m_refs[k][...] + (1.0 - ADAM_B1) * gg
            vn = ADAM_B2 * v_refs[k][...] + (1.0 - ADAM_B2) * (gg * gg)
            m_hat = mn / (1.0 - ADAM_B1 ** ADAM_STEP)
            v_hat = vn / (1.0 - ADAM_B2 ** ADAM_STEP)
            outs[3 * k][...] = -ADAM_LR * (m_hat / (jnp.sqrt(v_hat) + ADAM_EPS) + ADAM_WD * w_refs[k][...])
            outs[3 * k + 1][...] = mn
            outs[3 * k + 2][...] = vn

    out_shape = [_sds(w.shape, F32) for w in ws for _ in range(3)]
    return pl.pallas_call(body, name=name, out_shape=out_shape,
                          compiler_params=pltpu.CompilerParams(vmem_limit_bytes=VMEM_LIMIT))(*ws, *gs, *ms, *vs)


class _Mesh:
    def __init__(self, shards, my_chip, my_core):
        self.shards, self.my_chip, self.my_core = shards, my_chip.reshape(1), my_core.reshape(1)

    def gather(self, names):
        return _all_gather([self.shards[n] for n in names])

    @staticmethod
    def whole(gathered):
        return gathered.reshape(N_DEV * gathered.shape[1], gathered.shape[2])

    @staticmethod
    def by_device(grads):
        return [g.reshape(N_CHIP, 2, g.shape[0] // N_DEV, g.shape[1]) for g in grads]

    @staticmethod
    def to_sibling(parts):
        return _scatter_exchange(parts, CORE_RELS, lambda ref, peer: ref.at[:, peer[2]],
                                 [(N_CHIP,) + p.shape[2:] for p in parts])

    @staticmethod
    def to_chips(pair):
        return _scatter_exchange(pair, CHIP_RELS, lambda ref, peer: ref.at[_chip_slot(peer)], [p.shape[1:] for p in pair])


def _step(x, target, g1, w_in_t, wa, wb, bb, lg, lb, w_out, g2, w_gate_t, w_up_t, wf, w_down, g3, ts, mesh=None):
    (z, h1), got = _fwd_in(x, g1, w_in_t, ts, exchange=mesh and mesh.gather(["w_out"]))
    if mesh:
        w_out = mesh.whole(got[0])
    (x2, y, u), got = _fwd_mix(z, x, wa, wb, bb, lg, lb, w_out, ts, exchange=mesh and mesh.gather(["w_gate", "w_up"]))
    if mesh:
        w_gate_t, w_up_t = [mesh.whole(g) for g in got]
    (g0, v, h2), got = _fwd_ffn_in(x2, g2, w_gate_t, w_up_t, ts, exchange=mesh and mesh.gather(["w_down"]))
    if mesh:
        w_down = mesh.whole(got[0])
    tk = min(ts, SKEW_TILE)
    a, dx3, dx3b, loss, dg3, dgc, dv, dwf = _ffn_out_and_back(g0, v, x2, wf, w_down, g3, target, tk)
    one = dict(w_down=_matmul_tn(a, dx3b, "wgrad_down"), w_up=_matmul_tn(dv, h2, "wgrad_up"))
    parts1 = mesh and mesh.by_device(list(one.values()))
    (dg0, dx2, dx2b, dg2), got = _bwd_ffn_b(dgc, dv, wf, w_gate_t, w_up_t, x2, g2, dx3, tk,
                                            exchange=mesh and mesh.to_sibling(parts1))
    pair1 = mesh and _pair_sum(parts1, got, mesh.my_core, "rs_pair_sum_1")
    two = dict(w_gate=_matmul_tn(dg0, h2, "wgrad_gate"), w_out=_matmul_tn(y, dx2b, "wgrad_out"))
    parts2 = mesh and mesh.by_device(list(two.values()))
    (dca, du, dab, dwa, dlg, dlb, dbb), got = _bwd_mix_a(
        dx2b, jnp.swapaxes(w_out, 0, 1), z, u, wa, lg, lb, tk,
        exchange=mesh and _join([mesh.to_chips(pair1), mesh.to_sibling(parts2)]))
    (dz, dx, dg1, dwb), _ = _bwd_mix_b(dca, du, z, dab, wa, wb, w_in_t, x, g1, dx2, tk)
    small = dict(norm_mix_g=dg1, conv_a_w=dwa, conv_b_w=dwb, conv_b_b=dbb, ln_b_g=dlg, ln_b_b=dlb,
                 norm_ffn_g=dg2, conv_ffn_w=dwf, norm_final_g=dg3)
    if not mesh:
        return loss, dx, dict(w_in=_matmul_tn(dz, h1, "wgrad_in"), **one, **two), small
    big = dict(zip(one, _chip_sum(pair1, got[:2], mesh.my_chip, "rs_chip_sum_1")))
    pair2 = _pair_sum(parts2, got[2:], mesh.my_core, "rs_pair_sum_2")
    dw_in_t, got = _matmul_tn(dz, h1, "wgrad_in",
                              exchange=_join([mesh.to_chips(pair2), _gather_direct(_pack_small_grads(small, loss))]))
    big.update(zip(two, _chip_sum(pair2, got[:2], mesh.my_chip, "rs_chip_sum_2")))
    every = got[2]
    parts = mesh.by_device([dw_in_t])
    (got,) = _run_exchanges("rs_cores_last", [mesh.to_sibling(parts)])
    pair = _pair_sum(parts, got, mesh.my_core, "rs_pair_sum_last")
    (got,) = _run_exchanges("rs_chips_last", [mesh.to_chips(pair)])
    (big["w_in"],) = _chip_sum(pair, got, mesh.my_chip, "rs_chip_sum_last")
    shapes = [loss.shape if n == "loss" else small[n].shape for n in _SMALL_NAMES]
    return None, dx, big, dict(zip(_SMALL_NAMES, _sum_devices(every, shapes)))


def _pack_small_weights(conv_a_s, conv_b_s, conv_ffn_s):
    def body(a_ref, b_ref, f_ref, out):
        out[...] = jnp.zeros_like(out)
        at = 0
        for src in (a_ref, b_ref, f_ref):
            rows, cols = src.shape
            for r in range(rows):
                out[at:at + 1, 0:cols] = src[r:r + 1, :]
                at += 1

    return pl.pallas_call(body, name="small_weight_pack", out_shape=_sds((SMALL_W_ROWS, SMALL_W_COLS), F32))(
        conv_a_s, conv_b_s, conv_ffn_s)


def _unpack_small_weights(full):
    def take(r0, k, w):
        return jnp.transpose(full[:, r0:r0 + k, 0:w], (1, 0, 2)).reshape(k, N_DEV * w)

    return take(0, K_A, CONV_A_COLS), take(K_A, K_B, CONV_A_COLS), take(K_A + K_B, K_F, W_FF_COLS)


_SMALL_NAMES = ("conv_b_w", "conv_a_w", "conv_ffn_w", "norm_mix_g", "norm_ffn_g", "norm_final_g",
                "conv_b_b", "ln_b_g", "ln_b_b", "loss")
SMALL_G_ROWS = 64


def _small_rows(shapes):
    plan, at = [], 0
    for idx, (rows, cols) in enumerate(shapes):
        for r in range(rows):
            for c0 in range(0, cols, SMALL_G_COLS):
                plan.append((idx, r, c0, min(SMALL_G_COLS, cols - c0), at))
                at += 1
    assert at <= SMALL_G_ROWS
    return plan


def _pack_small_grads(small, loss):
    srcs = [loss if n == "loss" else small[n] for n in _SMALL_NAMES]
    plan = _small_rows([a.shape for a in srcs])

    def body(*refs):
        out = refs[-1]
        out[...] = jnp.zeros_like(out)
        for idx, r, c0, width, at in plan:
            out[at:at + 1, 0:width] = refs[idx][r:r + 1, c0:c0 + width]

    return pl.pallas_call(body, name="small_grad_pack", out_shape=_sds((SMALL_G_ROWS, SMALL_G_COLS), F32))(*srcs)


def kernel(x, norm_mix_g, w_in, conv_a_w, conv_b_w, conv_b_b, ln_b_g, ln_b_b, w_out, norm_ffn_g, w_gate, w_up, conv_ffn_w, w_down, norm_final_g, loss_target, m_norm_mix_g, m_w_in, m_conv_a_w, m_conv_b_w, m_conv_b_b, m_ln_b_g, m_ln_b_b, m_w_out, m_norm_ffn_g, m_w_gate, m_w_up, m_conv_ffn_w, m_w_down, m_norm_final_g, v_norm_mix_g, v_w_in, v_conv_a_w, v_conv_b_w, v_conv_b_b, v_ln_b_g, v_ln_b_b, v_w_out, v_norm_ffn_g, v_w_gate, v_w_up, v_conv_ffn_w, v_w_down, v_norm_final_g):
    ix, iy, ic = lax.axis_index("x"), lax.axis_index("y"), lax.axis_index("c")
    my_chip = (2 * ix + iy).astype(jnp.int32)
    my_core = ic.astype(jnp.int32)
    my_dev = 2 * my_chip + my_core

    weights = dict(norm_mix_g=norm_mix_g, w_in=w_in, conv_a_w=conv_a_w, conv_b_w=conv_b_w, conv_b_b=conv_b_b,
                   ln_b_g=ln_b_g, ln_b_b=ln_b_b, w_out=w_out, norm_ffn_g=norm_ffn_g, w_gate=w_gate, w_up=w_up,
                   conv_ffn_w=conv_ffn_w, w_down=w_down, norm_final_g=norm_final_g)
    m_in = dict(norm_mix_g=m_norm_mix_g, w_in=m_w_in, conv_a_w=m_conv_a_w, conv_b_w=m_conv_b_w, conv_b_b=m_conv_b_b,
                ln_b_g=m_ln_b_g, ln_b_b=m_ln_b_b, w_out=m_w_out, norm_ffn_g=m_norm_ffn_g, w_gate=m_w_gate,
                w_up=m_w_up, conv_ffn_w=m_conv_ffn_w, w_down=m_w_down, norm_final_g=m_norm_final_g)
    v_in = dict(norm_mix_g=v_norm_mix_g, w_in=v_w_in, conv_a_w=v_conv_a_w, conv_b_w=v_conv_b_w, conv_b_b=v_conv_b_b,
                ln_b_g=v_ln_b_g, ln_b_b=v_ln_b_b, w_out=v_w_out, norm_ffn_g=v_norm_ffn_g, w_gate=v_w_gate,
                w_up=v_w_up, conv_ffn_w=v_conv_ffn_w, w_down=v_w_down, norm_final_g=v_norm_final_g)
    order = list(weights)
    big_names = ("w_in", "w_gate", "w_up", "w_out", "w_down")
    transposed = ("w_in", "w_gate", "w_up")

    def shard2d(name, a):
        if name in transposed:
            return jnp.swapaxes(a[0], 0, 1)
        return a.reshape(1, a.shape[0]) if a.ndim == 1 else a.reshape(a.shape[-2:])

    def unshard2d(name, a2, like):
        if name in transposed:
            return jnp.swapaxes(a2, 0, 1)[None]
        return a2.reshape(like.shape)

    mesh = _Mesh(dict(zip(big_names, _cast_shards([shard2d(n, weights[n]) for n in big_names]))), my_chip, my_core)
    gathered, = _run_exchanges("ag_first", [_all_gather(
        [mesh.shards["w_in"], _pack_small_weights(conv_a_w[0], conv_b_w[0], conv_ffn_w[0])])])
    w_in_t = mesh.whole(gathered[0])
    wa_f, wb_f, wf_f = _unpack_small_weights(gathered[1])

    _, dx, gsum, stot = _step(
        x[0], loss_target[0], norm_mix_g, w_in_t, wa_f, wb_f, conv_b_b, ln_b_g, ln_b_b, None, norm_ffn_g,
        None, None, wf_f, None, norm_final_g.reshape(1, D_MODEL), SEQ_TILE, mesh)

    grads2d = dict(
        norm_mix_g=stot["norm_mix_g"],
        conv_a_w=lax.dynamic_slice(stot["conv_a_w"], (0, my_dev * CONV_A_COLS), (K_A, CONV_A_COLS)),
        conv_b_w=lax.dynamic_slice(stot["conv_b_w"], (0, my_dev * CONV_A_COLS), (K_B, CONV_A_COLS)),
        conv_b_b=stot["conv_b_b"], ln_b_g=stot["ln_b_g"], ln_b_b=stot["ln_b_b"],
        norm_ffn_g=stot["norm_ffn_g"],
        conv_ffn_w=lax.dynamic_slice(stot["conv_ffn_w"], (0, my_dev * W_FF_COLS), (K_F, W_FF_COLS)),
        norm_final_g=stot["norm_final_g"],
        **gsum,
    )

    updates = {}
    small_names = [n for n in order if n not in big_names]
    for group, label in [([n], "adamw_" + n) for n in big_names] + [(small_names, "adamw_small")]:
        outs = _adamw([shard2d(n, weights[n]) for n in group], [grads2d[n] for n in group],
                      [shard2d(n, m_in[n]) for n in group], [shard2d(n, v_in[n]) for n in group], label)
        for k, n in enumerate(group):
            updates[n] = outs[3 * k:3 * k + 3]
    g_out = [unshard2d(n, grads2d[n], weights[n]) for n in order]
    d_out, m_out, v_out = [[unshard2d(n, updates[n][j], weights[n]) for n in order] for j in range(3)]

    return (stot["loss"][0, 0], dx[None], *g_out, *d_out, *m_out, *v_out)
```

```python
import jax
import jax.numpy as jnp
from jax import lax
from jax.experimental import pallas as pl
from jax.experimental.pallas import tpu as pltpu

F32 = jnp.float32
BF16 = jnp.bfloat16

D_MODEL = 1024
D_A = 512
D_B = 512
D_IN = 3 * D_A + 2 * D_B
D_FF = 2816
K_A = 3
K_B = 31
K_F = 3
RMS_EPS = 1e-6
LN_EPS = 1e-5

ADAM_LR = 0.001
ADAM_B1 = 0.9
ADAM_B2 = 0.999
ADAM_EPS = 1e-08
ADAM_WD = 0.01
ADAM_STEP = 10

N_DEV = 8
N_CHIP = 4
LANES = 128
SUBLANES = 8
HALO = 16
ROW_CHUNK = 64
SEQ_TILE = 512
SKEW_TILE = 256
VMEM_LIMIT = 56 * 1024 * 1024

MESH = pl.DeviceIdType.MESH

W_FF_COLS = D_FF // N_DEV
CONV_A_COLS = D_A // N_DEV
SMALL_W_ROWS = 40
SMALL_W_COLS = 384
SMALL_G_COLS = 512
FF_PAD = 3072


def _rows(ts, c):
    return pl.BlockSpec((ts, c), lambda i: (i, 0))


def _const(shape):
    return pl.BlockSpec(shape, lambda i: (0,) * len(shape), pipeline_mode=pl.Buffered(1))


def _acc_out(shape):
    return pl.BlockSpec(shape, lambda i: (0,) * len(shape))


def _rows_at(ts, c, tile):
    return pl.BlockSpec((ts, c), lambda i: (tile(i), 0))


def _prev_at(ts, c, tile):
    return pl.BlockSpec((HALO, c), lambda i: (jnp.maximum(tile(i) * (ts // HALO) - 1, 0), 0))


def _next_at(ts, c, s, tile):
    last = s // HALO - 1
    return pl.BlockSpec((HALO, c), lambda i: (jnp.minimum((tile(i) + 1) * (ts // HALO), last), 0))


def _prev(ts, c):
    return _prev_at(ts, c, lambda i: i)


def _next(ts, c, s):
    return _next_at(ts, c, s, lambda i: i)


MXU_COLS = 256
MXU_ROWS = 256


def _col_pieces(n):
    return [(c0, min(MXU_COLS, n - c0)) for c0 in range(0, n, MXU_COLS)]


def _matmul_pieces(terms, out_ref, k_parts, w_transposed=False):
    m, n = out_ref.shape
    rows = min(MXU_ROWS, m)
    steps = []
    for lhs_ref, w_ref in terms:
        tiles = lhs_ref.shape[1] // MXU_COLS
        cuts = [MXU_COLS * (tiles * j // k_parts) for j in range(k_parts)] + [lhs_ref.shape[1]]
        steps += [(lhs_ref, w_ref, cuts[j], cuts[j + 1]) for j in range(k_parts)]

    def piece(m0, n0, width, step):
        lhs_ref, w_ref, k0, k1 = steps[step]
        if w_transposed:
            part = lax.dot_general(lhs_ref[m0:m0 + rows, k0:k1], w_ref[n0:n0 + width, k0:k1], _NT,
                                   preferred_element_type=F32)
        else:
            part = jnp.dot(lhs_ref[m0:m0 + rows, k0:k1], w_ref[k0:k1, n0:n0 + width], preferred_element_type=F32)
        if step:
            part = part + out_ref[m0:m0 + rows, n0:n0 + width]
        out_ref[m0:m0 + rows, n0:n0 + width] = part

    return [(piece, (m0, n0, w, j)) for j in range(len(steps)) for n0, w in _col_pieces(n) for m0 in range(0, m, rows)]


def _interleaved(vector_units, matmul_pieces):
    n_u, n_p = len(vector_units), len(matmul_pieces)
    done = 0
    for k, (unit, args) in enumerate(vector_units):
        while done < n_p and done * n_u <= k * n_p:
            matmul_pieces[done][0](*matmul_pieces[done][1])
            done += 1
        unit(*args)
    for fn, args in matmul_pieces[done:]:
        fn(*args)


def _params():
    return pltpu.CompilerParams(dimension_semantics=("arbitrary",), vmem_limit_bytes=VMEM_LIMIT)


def _sds(shape, dtype):
    return jax.ShapeDtypeStruct(shape, dtype)


def _sigmoid(v):
    return 0.5 * jnp.tanh(0.5 * v) + 0.5


def _conv_block(ext_ref, w_ref, r0, rc, l0, k_taps, transposed):
    acc = None
    for k in range(k_taps):
        d = (k_taps // 2 - k) if transposed else (k - k_taps // 2)
        term = ext_ref[l0 // LANES, pl.ds(r0 + HALO + d, rc), :] * w_ref[k:k + 1, l0:l0 + LANES]
        acc = term if acc is None else acc + term
    return acc


def _conv_wgrad_block(acc_ref, dout, ext_ref, r0, rc, l0, k_taps, scale=None):
    for k in range(k_taps):
        prod = dout * ext_ref[l0 // LANES, pl.ds(r0 + HALO + k - k_taps // 2, rc), :]
        part = prod.reshape(rc // SUBLANES, SUBLANES, LANES).sum(axis=0)
        if scale is not None:
            part = part * scale
        acc_ref[k, :, l0:l0 + LANES] = acc_ref[k, :, l0:l0 + LANES] + part


def _reduce_acc(out_ref, acc_ref, k_taps):
    for k in range(k_taps):
        out_ref[k:k + 1, :] = jnp.sum(acc_ref[k], axis=0, keepdims=True)


def _fold8(v):
    rc, c = v.shape
    return v.reshape(rc // SUBLANES, SUBLANES, c).sum(axis=0)


def _ext_scratch(ts, c):
    return pltpu.VMEM((c // LANES, ts + 2 * HALO, LANES), F32)


def _put_rows(ext_ref, r0, rc, val):
    for q in range(val.shape[1] // LANES):
        ext_ref[q, pl.ds(r0 + HALO, rc), :] = val[:, q * LANES:(q + 1) * LANES]


def _fill_halo(ext_ref, vals_prev, vals_next, ts, first, last):
    for q in range(vals_prev.shape[1] // LANES):
        cols = slice(q * LANES, (q + 1) * LANES)
        ext_ref[q, 0:HALO, :] = jnp.where(first, 0.0, vals_prev[:, cols])
        ext_ref[q, HALO + ts:HALO + ts + HALO, :] = jnp.where(last, 0.0, vals_next[:, cols])


def _rms_bwd_rows(dh, xf, g):
    r = lax.rsqrt(jnp.mean(xf * xf, axis=-1, keepdims=True) + RMS_EPS)
    xhat = xf * r
    dxh = dh * g
    dx = r * (dxh - xhat * jnp.mean(dxh * xhat, axis=-1, keepdims=True))
    return dx, dh * xhat


_NT = (((1,), (1,)), ((), ()))
_TN = (((0,), (0,)), ((), ()))


def _fwd_in(x, g1, w_in_t, ts, exchange=None):
    s = x.shape[0]

    def body(x_ref, g_ref, w_ref, z_ref, h_ref):
        xf = x_ref[...]
        r = lax.rsqrt(jnp.mean(xf * xf, axis=-1, keepdims=True) + RMS_EPS)
        h = (xf * r * g_ref[...]).astype(BF16)
        h_ref[...] = h
        for n0 in range(0, D_IN, 512):
            z_ref[:, n0:n0 + 512] = lax.dot_general(h, w_ref[n0:n0 + 512, :], _NT,
                                                    preferred_element_type=F32).astype(BF16)

    return _call(
        body, name="fwd_in", grid=(s // ts,),
        in_specs=[_rows(ts, D_MODEL), _const((1, D_MODEL)), _const((D_IN, D_MODEL))],
        out_specs=[_rows(ts, D_IN), _rows(ts, D_MODEL)],
        out_shape=[_sds((s, D_IN), BF16), _sds((s, D_MODEL), BF16)],
        scratch_shapes=[], args=(x, g1, w_in_t), exchange=exchange, forward_step=(s // ts) * 3 // 4)


def _p_u0(z_ref, rows):
    a_h = z_ref[rows, 0:D_A].astype(F32)
    a_c = z_ref[rows, 2 * D_A:3 * D_A].astype(F32)
    b_v = z_ref[rows, 3 * D_A:3 * D_A + D_B].astype(F32)
    b_g = z_ref[rows, 3 * D_A + D_B:D_IN].astype(F32)
    return a_c * a_h, b_v * _sigmoid(b_g)


def _layernorm_rows(u_blocks):
    tot = None
    for ub in u_blocks:
        sm = jnp.sum(ub, axis=-1, keepdims=True)
        tot = sm if tot is None else tot + sm
    mu = tot * (1.0 / D_B)
    var = None
    for ub in u_blocks:
        sq = jnp.sum((ub - mu) * (ub - mu), axis=-1, keepdims=True)
        var = sq if var is None else var + sq
    rstd = lax.rsqrt(var * (1.0 / D_B) + LN_EPS)
    return mu, rstd


def _fwd_mix(z, x, wa, wb, bb, lg, lb, w_out, ts, exchange=None):
    s = x.shape[0]
    nt = s // ts
    rc = min(ROW_CHUNK, ts)

    def body(z_ref, zp_ref, zn_ref, x_ref, wa_ref, wb_ref, bb_ref, lg_ref, lb_ref, wo_ref,
             x2_ref, y_ref, u_ref, pe_ref, ue_ref):
        i = pl.program_id(0)
        pp, up = _p_u0(zp_ref, slice(None))
        pn, un = _p_u0(zn_ref, slice(None))
        _fill_halo(pe_ref, pp, pn, ts, i == 0, i == nt - 1)
        _fill_halo(ue_ref, up, un, ts, i == 0, i == nt - 1)

        def fill(j, carry):
            r0 = pl.multiple_of(j * rc, rc)
            p, u0 = _p_u0(z_ref, pl.ds(r0, rc))
            _put_rows(pe_ref, r0, rc, p)
            _put_rows(ue_ref, r0, rc, u0)
            return carry

        lax.fori_loop(0, ts // rc, fill, 0)

        def mixer_a(r0, l0):
            rows = pl.ds(r0, rc)
            ca = _conv_block(pe_ref, wa_ref, r0, rc, l0, K_A, False)
            a_b = z_ref[rows, D_A + l0:D_A + l0 + LANES].astype(F32)
            y_ref[rows, l0:l0 + LANES] = (a_b * ca).astype(BF16)

        def conv_b(r0, l0):
            u_ref[pl.ds(r0, rc), l0:l0 + LANES] = (_conv_block(ue_ref, wb_ref, r0, rc, l0, K_B, False)
                                                   + bb_ref[:, l0:l0 + LANES])

        def norm_b(r0):
            rows = pl.ds(r0, rc)
            ubs = [u_ref[rows, l0:l0 + LANES] for l0 in range(0, D_B, LANES)]
            mu, rstd = _layernorm_rows(ubs)
            for q, l0 in enumerate(range(0, D_B, LANES)):
                t = (ubs[q] - mu) * rstd * lg_ref[:, l0:l0 + LANES] + lb_ref[:, l0:l0 + LANES]
                y_ref[rows, D_A + l0:D_A + l0 + LANES] = (t * _sigmoid(t)).astype(BF16)

        def chunk_units(r0):
            return ([(mixer_a, (r0, l0)) for l0 in range(0, D_A, LANES)]
                    + [(conv_b, (r0, l0)) for l0 in range(0, D_B, LANES)] + [(norm_b, (r0,))])

        def main(j, carry):
            for fn, args in chunk_units(pl.multiple_of(j * rc, rc)):
                fn(*args)
            return carry

        def project(m0, n0, width):
            rows = slice(m0, m0 + hrows)
            x2_ref[rows, n0:n0 + width] = x_ref[rows, n0:n0 + width] + jnp.dot(
                y_ref[rows, :], wo_ref[:, n0:n0 + width], preferred_element_type=F32)

        hrows = ts // 2
        lax.fori_loop(0, hrows // rc, main, 0, unroll=4)
        _interleaved([u for q in range(hrows // rc) for u in chunk_units(hrows + q * rc)],
                     [(project, (0, n0, w)) for n0, w in _col_pieces(D_MODEL)])
        for n0, w in _col_pieces(D_MODEL):
            project(hrows, n0, w)

    return _call(
        body, name="fwd_mix", grid=(nt,),
        in_specs=[_rows(ts, D_IN), _prev(ts, D_IN), _next(ts, D_IN, s), _rows(ts, D_MODEL),
                  _const((K_A, D_A)), _const((K_B, D_B)), _const((1, D_B)), _const((1, D_B)), _const((1, D_B)),
                  _const((D_MODEL, D_MODEL))],
        out_specs=[_rows(ts, D_MODEL), _rows(ts, D_MODEL), _rows(ts, D_B)],
        out_shape=[_sds((s, D_MODEL), F32), _sds((s, D_MODEL), BF16), _sds((s, D_B), F32)],
        scratch_shapes=[_ext_scratch(ts, D_A), _ext_scratch(ts, D_B)],
        args=(z, z, z, x, wa, wb, bb, lg, lb, w_out), exchange=exchange, forward_step=nt * 5 // 8)


def _fwd_ffn_in(x2, g2, w_gate_t, w_up_t, ts, exchange=None):
    s = x2.shape[0]
    half = D_FF // 2

    def body(x_ref, g_ref, wg_ref, wu_ref, g0_ref, v_ref, h_ref):
        xf = x_ref[...]
        r = lax.rsqrt(jnp.mean(xf * xf, axis=-1, keepdims=True) + RMS_EPS)
        h = (xf * r * g_ref[...]).astype(BF16)
        h_ref[...] = h
        for n0 in range(0, D_FF, half):
            g0_ref[:, n0:n0 + half] = lax.dot_general(h, wg_ref[n0:n0 + half, :], _NT,
                                                      preferred_element_type=F32).astype(BF16)
            v_ref[:, n0:n0 + half] = lax.dot_general(h, wu_ref[n0:n0 + half, :], _NT,
                                                     preferred_element_type=F32).astype(BF16)

    return _call(
        body, name="fwd_ffn_in", grid=(s // ts,),
        in_specs=[_rows(ts, D_MODEL), _const((1, D_MODEL)), _const((D_FF, D_MODEL)), _const((D_FF, D_MODEL))],
        out_specs=[_rows(ts, D_FF), _rows(ts, D_FF), _rows(ts, D_MODEL)],
        out_shape=[_sds((s, D_FF), BF16), _sds((s, D_FF), BF16), _sds((s, D_MODEL), BF16)],
        scratch_shapes=[], args=(x2, g2, w_gate_t, w_up_t), exchange=exchange, forward_step=(s // ts) // 2)


def _ffn_out_and_back(g0, v, x2, wf, w_down, g3, target, ts):
    s = x2.shape[0]
    nt = s // ts
    rc = min(ROW_CHUNK, ts // 2)

    def body(g0_ref, gp_ref, gn_ref, v_ref, x2_ref, wf_ref, wd_ref, g3_ref, t_ref,
             a_ref, dx3_ref, dx3b_ref, loss_ref, dg3_ref, dg_ref, dv_ref, dwf_ref,
             ge_ref, silu_ref, dsv_ref, da_ref, acc_ref, p_ref, sums_ref):
        i = pl.program_id(0)

        @pl.when(i == 0)
        def _():
            acc_ref[...] = jnp.zeros_like(acc_ref)
            sums_ref[...] = jnp.zeros_like(sums_ref)

        _fill_halo(ge_ref, gp_ref[...].astype(F32), gn_ref[...].astype(F32), ts, i == 0, i == nt - 1)

        def fill(j, carry):
            r0 = pl.multiple_of(j * rc, rc)
            _put_rows(ge_ref, r0, rc, g0_ref[pl.ds(r0, rc), :].astype(F32))
            return carry

        lax.fori_loop(0, ts // rc, fill, 0)

        def act(r0, l0):
            rows = slice(r0, r0 + rc)
            g = _conv_block(ge_ref, wf_ref, r0, rc, l0, K_F, False)
            vv = v_ref[rows, l0:l0 + LANES].astype(F32)
            sg = _sigmoid(g)
            silu = g * sg
            a_ref[rows, l0:l0 + LANES] = (silu * vv).astype(BF16)
            silu_ref[rows, l0:l0 + LANES] = silu
            dsv_ref[rows, l0:l0 + LANES] = (sg + silu * (1.0 - sg)) * vv

        def tail(r0):
            rows = slice(r0, r0 + rc)
            x3 = x2_ref[rows, :] + p_ref[rows, :]
            r = lax.rsqrt(jnp.mean(x3 * x3, axis=-1, keepdims=True) + RMS_EPS)
            xhat = x3 * r
            diff = xhat * g3_ref[...] - t_ref[rows, :]
            dout = diff * (1.0 / D_MODEL)
            dxh = dout * g3_ref[...]
            dx3 = r * (dxh - xhat * jnp.mean(dxh * xhat, axis=-1, keepdims=True))
            dx3_ref[rows, :] = dx3
            dx3b_ref[rows, :] = dx3.astype(BF16)
            sums_ref[0] = sums_ref[0] + _fold8(diff * diff)
            sums_ref[1] = sums_ref[1] + _fold8(dout * xhat)

        def back(r0, l0):
            rows = slice(r0, r0 + rc)
            da = da_ref[rows, l0:l0 + LANES]
            dv_ref[rows, l0:l0 + LANES] = (da * silu_ref[rows, l0:l0 + LANES]).astype(BF16)
            dgg = da * dsv_ref[rows, l0:l0 + LANES]
            dg_ref[rows, l0:l0 + LANES] = dgg.astype(BF16)
            _conv_wgrad_block(acc_ref, dgg, ge_ref, r0, rc, l0, K_F)

        hrows = ts // 2

        def units(fn, h, per_lane_block):
            starts = [h * hrows + q * rc for q in range(hrows // rc)]
            if per_lane_block:
                return [(fn, (r0, l0)) for r0 in starts for l0 in range(0, D_FF, LANES)]
            return [(fn, (r0,)) for r0 in starts]

        def rows_of(ref, h):
            return ref.at[pl.ds(h * hrows, hrows), :]

        def product(h):
            return _matmul_pieces([(rows_of(a_ref, h), wd_ref)], rows_of(p_ref, h), 2)

        _interleaved(units(act, 0, True), [])
        _interleaved(units(act, 1, True), product(0))
        _interleaved(units(tail, 0, False), product(1))
        _interleaved(units(tail, 1, False), [])
        grad_a = _matmul_pieces([(dx3b_ref, wd_ref)], da_ref, 1, w_transposed=True)
        ahead = len(grad_a) // len(_col_pieces(D_FF))
        _interleaved([], grad_a[:ahead])
        _interleaved([(back, (r0, l0)) for l0 in range(0, D_FF, LANES) for r0 in range(0, ts, rc)], grad_a[ahead:])

        @pl.when(i == nt - 1)
        def _():
            _reduce_acc(dwf_ref, acc_ref, K_F)
            loss_ref[...] = (0.5 / D_MODEL) * jnp.sum(sums_ref[0], keepdims=True)
            dg3_ref[...] = jnp.sum(sums_ref[1], axis=0, keepdims=True)

    return pl.pallas_call(
        body, name="ffn_out_and_back", grid=(nt,),
        in_specs=[_rows(ts, D_FF), _prev(ts, D_FF), _next(ts, D_FF, s), _rows(ts, D_FF), _rows(ts, D_MODEL),
                  _const((K_F, D_FF)), _const((D_FF, D_MODEL)), _const((1, D_MODEL)), _rows(ts, D_MODEL)],
        out_specs=[_rows(ts, D_FF), _rows(ts, D_MODEL), _rows(ts, D_MODEL), _acc_out((1, 1)), _acc_out((1, D_MODEL)),
                   _rows(ts, D_FF), _rows(ts, D_FF), _acc_out((K_F, D_FF))],
        out_shape=[_sds((s, D_FF), BF16), _sds((s, D_MODEL), F32), _sds((s, D_MODEL), BF16),
                   _sds((1, 1), F32), _sds((1, D_MODEL), F32),
                   _sds((s, D_FF), BF16), _sds((s, D_FF), BF16), _sds((K_F, D_FF), F32)],
        scratch_shapes=[_ext_scratch(ts, D_FF), pltpu.VMEM((ts, D_FF), F32), pltpu.VMEM((ts, D_FF), F32),
                        pltpu.VMEM((ts, D_FF), F32), pltpu.VMEM((K_F, SUBLANES, D_FF), F32),
                        pltpu.VMEM((ts, D_MODEL), F32), pltpu.VMEM((2, SUBLANES, D_MODEL), F32)],
        compiler_params=_params(),
    )(g0, g0, g0, v, x2, wf, w_down, g3, target)


def _bwd_ffn_b(dg, dv, wf, w_gate, w_up, x2, g2, dx3, ts, exchange=None):
    s = x2.shape[0]
    nt = s // ts
    rc = min(ROW_CHUNK, ts)
    n_sub = ts // rc

    def body(dg_ref, dgp_ref, dgn_ref, dv_ref, wf_ref, wg_ref, wu_ref, x2_ref, g2_ref, dx3_ref,
             dg0_ref, dx2_ref, dx2b_ref, dgn2_ref, dge_ref, dg8_ref, a0_ref, a1_ref, p0_ref, p1_ref):
        i = pl.program_id(0)
        vt = jnp.minimum(i, nt - 1)
        live = (i >= 2).astype(F32)

        @pl.when(i == 0)
        def _():
            dg8_ref[...] = jnp.zeros_like(dg8_ref)
            a1_ref[...] = jnp.zeros_like(a1_ref)
            p1_ref[...] = jnp.zeros_like(p1_ref)

        _fill_halo(dge_ref, dgp_ref[...].astype(F32), dgn_ref[...].astype(F32), ts, vt == 0, vt == nt - 1)

        def fill(j, carry):
            r0 = pl.multiple_of(j * rc, rc)
            _put_rows(dge_ref, r0, rc, dg_ref[pl.ds(r0, rc), :].astype(F32))
            return carry

        lax.fori_loop(0, n_sub, fill, 0)

        def stage(a_new, a_old, p_new, p_old):
            def conv_t(r0, l0):
                rows = slice(r0, r0 + rc)
                dg0 = _conv_block(dge_ref, wf_ref, r0, rc, l0, K_F, True).astype(BF16)
                dg0_ref[rows, l0:l0 + LANES] = dg0
                a_new[rows, l0:l0 + LANES] = dg0

            def tail(r0):
                rows = slice(r0, r0 + rc)
                dx, dgrow = _rms_bwd_rows(p_old[rows, :], x2_ref[rows, :], g2_ref[...])
                dx2 = dx3_ref[rows, :] + dx
                dx2_ref[rows, :] = dx2
                dx2b_ref[rows, :] = dx2.astype(BF16)
                dg8_ref[...] = dg8_ref[...] + _fold8(dgrow) * live

            units = []
            for q in range(n_sub):
                units += [(conv_t, (q * rc, l0)) for l0 in range(0, D_FF, LANES)]
                units.append((tail, (q * rc,)))
            _interleaved(units, _matmul_pieces([(a_old, wg_ref), (dv_ref, wu_ref)], p_new, 2))

        @pl.when(i % 2 == 0)
        def _():
            stage(a0_ref, a1_ref, p0_ref, p1_ref)

        @pl.when(i % 2 == 1)
        def _():
            stage(a1_ref, a0_ref, p1_ref, p0_ref)

        @pl.when(i == nt + 1)
        def _():
            dgn2_ref[...] = jnp.sum(dg8_ref[...], axis=0, keepdims=True)

    vtile = lambda i: jnp.minimum(i, nt - 1)
    mtile = lambda i: jnp.clip(i - 1, 0, nt - 1)
    ttile = lambda i: jnp.clip(i - 2, 0, nt - 1)
    return _call(
        body, name="bwd_ffn_b", grid=(nt + 2,),
        in_specs=[_rows_at(ts, D_FF, vtile), _prev_at(ts, D_FF, vtile), _next_at(ts, D_FF, s, vtile),
                  _rows_at(ts, D_FF, mtile), _const((K_F, D_FF)),
                  _const((D_FF, D_MODEL)), _const((D_FF, D_MODEL)), _rows_at(ts, D_MODEL, ttile), _const((1, D_MODEL)),
                  _rows_at(ts, D_MODEL, ttile)],
        out_specs=[_rows_at(ts, D_FF, vtile), _rows_at(ts, D_MODEL, ttile), _rows_at(ts, D_MODEL, ttile),
                   _acc_out((1, D_MODEL))],
        out_shape=[_sds((s, D_FF), BF16), _sds((s, D_MODEL), F32), _sds((s, D_MODEL), BF16), _sds((1, D_MODEL), F32)],
        scratch_shapes=[_ext_scratch(ts, D_FF), pltpu.VMEM((SUBLANES, D_MODEL), F32),
                        pltpu.VMEM((ts, D_FF), BF16), pltpu.VMEM((ts, D_FF), BF16),
                        pltpu.VMEM((ts, D_MODEL), F32), pltpu.VMEM((ts, D_MODEL), F32)],
        args=(dg, dg, dg, dv, wf, w_gate, w_up, x2, g2, dx3), exchange=exchange)


def _bwd_mix_a(dx2b, w_out_t, z, u, wa, lg, lb, ts, exchange=None):
    s = dx2b.shape[0]
    nt = s // ts
    rc = min(ROW_CHUNK, ts)
    n_sub = ts // rc

    def body(dx_ref, wo_ref, z_ref, zp_ref, zn_ref, u_ref, wa_ref, lg_ref, lb_ref,
             dca_ref, du_ref, dab_ref, dwa_ref, dlg_ref, dlb_ref, dbb_ref, pe_ref, dy0_ref, dy1_ref, acc_ref, sacc_ref):
        i = pl.program_id(0)
        t = jnp.maximum(i - 1, 0)

        @pl.when(i == 0)
        def _():
            acc_ref[...] = jnp.zeros_like(acc_ref)
            sacc_ref[...] = jnp.zeros_like(sacc_ref)
            dy1_ref[...] = jnp.zeros_like(dy1_ref)

        pp, _ = _p_u0(zp_ref, slice(None))
        pn, _ = _p_u0(zn_ref, slice(None))
        _fill_halo(pe_ref, pp, pn, ts, t == 0, t == nt - 1)

        def fill(j, carry):
            r0 = pl.multiple_of(j * rc, rc)
            rows = pl.ds(r0, rc)
            _put_rows(pe_ref, r0, rc, z_ref[rows, 2 * D_A:3 * D_A].astype(F32) * z_ref[rows, 0:D_A].astype(F32))
            return carry

        lax.fori_loop(0, n_sub, fill, 0)

        def stage(dy_new, dy_old):
            def piece(m0, n0, width):
                dy_new[m0:m0 + MXU_ROWS, n0:n0 + width] = jnp.dot(
                    dx_ref[m0:m0 + MXU_ROWS, :], wo_ref[:, n0:n0 + width], preferred_element_type=F32).astype(BF16)

            units = []
            for q in range(n_sub):
                units += [(mixer_a, (dy_old, q * rc, l0)) for l0 in range(0, D_A, LANES)]
                units.append((mixer_b, (dy_old, q * rc)))
            _interleaved(units, [(piece, (m0, n0, w)) for n0, w in _col_pieces(D_MODEL) for m0 in range(0, ts, MXU_ROWS)])

        def mixer_a(dy_ref, r0, l0):
            rows = slice(r0, r0 + rc)
            ca = _conv_block(pe_ref, wa_ref, r0, rc, l0, K_A, False)
            a_b = z_ref[rows, D_A + l0:D_A + l0 + LANES].astype(F32)
            dya = dy_ref[rows, l0:l0 + LANES].astype(F32)
            dab_ref[rows, l0:l0 + LANES] = (dya * ca).astype(BF16)
            dca = dya * a_b
            dca_ref[rows, l0:l0 + LANES] = dca
            _conv_wgrad_block(acc_ref, dca, pe_ref, r0, rc, l0, K_A)

        def mixer_b(dy_ref, r0):
            rows = slice(r0, r0 + rc)
            ubs = [u_ref[rows, l0:l0 + LANES] for l0 in range(0, D_B, LANES)]
            mu, rstd = _layernorm_rows(ubs)
            ns, dns = [], []
            m1 = None
            m2 = None
            for q, l0 in enumerate(range(0, D_B, LANES)):
                n = (ubs[q] - mu) * rstd
                lgq = lg_ref[:, l0:l0 + LANES]
                t = n * lgq + lb_ref[:, l0:l0 + LANES]
                sg = _sigmoid(t)
                dt = dy_ref[rows, D_A + l0:D_A + l0 + LANES].astype(F32) * (sg * (1.0 + t * (1.0 - sg)))
                dn = dt * lgq
                ns.append(n)
                dns.append(dn)
                s1 = jnp.sum(dn, axis=-1, keepdims=True)
                s2 = jnp.sum(dn * n, axis=-1, keepdims=True)
                m1 = s1 if m1 is None else m1 + s1
                m2 = s2 if m2 is None else m2 + s2
                sacc_ref[0, :, l0:l0 + LANES] = sacc_ref[0, :, l0:l0 + LANES] + _fold8(dt * n)
                sacc_ref[1, :, l0:l0 + LANES] = sacc_ref[1, :, l0:l0 + LANES] + _fold8(dt)
            m1 = m1 * (1.0 / D_B)
            m2 = m2 * (1.0 / D_B)
            for q, l0 in enumerate(range(0, D_B, LANES)):
                du = rstd * (dns[q] - m1 - ns[q] * m2)
                du_ref[rows, l0:l0 + LANES] = du
                sacc_ref[2, :, l0:l0 + LANES] = sacc_ref[2, :, l0:l0 + LANES] + _fold8(du)

        @pl.when(i % 2 == 0)
        def _():
            stage(dy0_ref, dy1_ref)

        @pl.when(i % 2 == 1)
        def _():
            stage(dy1_ref, dy0_ref)

        @pl.when(i == nt)
        def _():
            _reduce_acc(dwa_ref, acc_ref, K_A)
            dlg_ref[...] = jnp.sum(sacc_ref[0], axis=0, keepdims=True)
            dlb_ref[...] = jnp.sum(sacc_ref[1], axis=0, keepdims=True)
            dbb_ref[...] = jnp.sum(sacc_ref[2], axis=0, keepdims=True)

    cur = lambda i: jnp.minimum(i, nt - 1)
    old = lambda i: jnp.maximum(i - 1, 0)
    return _call(
        body, name="bwd_mix_a", grid=(nt + 1,),
        in_specs=[_rows_at(ts, D_MODEL, cur), _const((D_MODEL, D_MODEL)), _rows_at(ts, D_IN, old), _prev_at(ts, D_IN, old),
                  _next_at(ts, D_IN, s, old), _rows_at(ts, D_B, old), _const((K_A, D_A)), _const((1, D_B)), _const((1, D_B))],
        out_specs=[_rows_at(ts, D_A, old), _rows_at(ts, D_B, old), _rows_at(ts, D_A, old), _acc_out((K_A, D_A)),
                   _acc_out((1, D_B)), _acc_out((1, D_B)), _acc_out((1, D_B))],
        out_shape=[_sds((s, D_A), F32), _sds((s, D_B), F32), _sds((s, D_A), BF16), _sds((K_A, D_A), F32),
                   _sds((1, D_B), F32), _sds((1, D_B), F32), _sds((1, D_B), F32)],
        scratch_shapes=[_ext_scratch(ts, D_A), pltpu.VMEM((ts, D_MODEL), BF16), pltpu.VMEM((ts, D_MODEL), BF16),
                        pltpu.VMEM((K_A, SUBLANES, D_A), F32), pltpu.VMEM((3, SUBLANES, D_B), F32)],
        args=(dx2b, w_out_t, z, z, z, u, wa, lg, lb), exchange=exchange)


def _bwd_mix_b(dca, du, z, dab, wa, wb, w_in, x, g1, dx2, ts, exchange=None):
    s = x.shape[0]
    nt = s // ts
    rc = min(ROW_CHUNK, ts)
    n_sub = ts // rc

    def body(dca_ref, dcap_ref, dcan_ref, du_ref, dup_ref, dun_ref, z_ref, zp_ref, zn_ref, dab_ref,
             wa_ref, wb_ref, wi_ref, x_ref, g1_ref, dx2_ref,
             dz_ref, dx_ref, dg1_ref, dwb_ref, dcae_ref, due_ref, ue_ref, acc_ref, dg8_ref,
             dz0_ref, dz1_ref, dh0_ref, dh1_ref):
        i = pl.program_id(0)

        @pl.when(i == 0)
        def _():
            acc_ref[...] = jnp.zeros_like(acc_ref)
            dg8_ref[...] = jnp.zeros_like(dg8_ref)
            dz1_ref[...] = jnp.zeros_like(dz1_ref)
            dh1_ref[...] = jnp.zeros_like(dh1_ref)

        vt = jnp.minimum(i, nt - 1)
        first = vt == 0
        last = vt == nt - 1
        live = (i < nt).astype(F32)
        _fill_halo(dcae_ref, dcap_ref[...], dcan_ref[...], ts, first, last)
        _fill_halo(due_ref, dup_ref[...], dun_ref[...], ts, first, last)
        _, up = _p_u0(zp_ref, slice(None))
        _, un = _p_u0(zn_ref, slice(None))
        _fill_halo(ue_ref, up, un, ts, first, last)

        def fill(j, carry):
            r0 = pl.multiple_of(j * rc, rc)
            rows = pl.ds(r0, rc)
            _put_rows(dcae_ref, r0, rc, dca_ref[rows, :])
            _put_rows(due_ref, r0, rc, du_ref[rows, :])
            b_v = z_ref[rows, 3 * D_A:3 * D_A + D_B].astype(F32)
            b_g = z_ref[rows, 3 * D_A + D_B:D_IN].astype(F32)
            _put_rows(ue_ref, r0, rc, b_v * _sigmoid(b_g))
            return carry

        lax.fori_loop(0, n_sub, fill, 0)

        def stage(dz_new, dz_old, dh_new, dh_old):
            def put(rows, c0, val):
                dz_ref[rows, c0:c0 + LANES] = val
                dz_new[rows, c0:c0 + LANES] = val

            def mixer_a(r0, l0):
                rows = slice(r0, r0 + rc)
                dp = _conv_block(dcae_ref, wa_ref, r0, rc, l0, K_A, True)
                a_h = z_ref[rows, l0:l0 + LANES].astype(F32)
                a_c = z_ref[rows, 2 * D_A + l0:2 * D_A + l0 + LANES].astype(F32)
                put(rows, l0, (dp * a_c).astype(BF16))
                put(rows, D_A + l0, dab_ref[rows, l0:l0 + LANES])
                put(rows, 2 * D_A + l0, (dp * a_h).astype(BF16))

            def mixer_b(r0, l0):
                rows = slice(r0, r0 + rc)
                du0 = _conv_block(due_ref, wb_ref, r0, rc, l0, K_B, True)
                b_v = z_ref[rows, 3 * D_A + l0:3 * D_A + l0 + LANES].astype(F32)
                b_g = z_ref[rows, 3 * D_A + D_B + l0:3 * D_A + D_B + l0 + LANES].astype(F32)
                sg = _sigmoid(b_g)
                put(rows, 3 * D_A + l0, (du0 * sg).astype(BF16))
                put(rows, 3 * D_A + D_B + l0, (du0 * b_v * (sg * (1.0 - sg))).astype(BF16))
                _conv_wgrad_block(acc_ref, du_ref[rows, l0:l0 + LANES], ue_ref, r0, rc, l0, K_B, live)

            def tail(r0):
                rows = slice(r0, r0 + rc)
                dx, dgrow = _rms_bwd_rows(dh_old[rows, :], x_ref[rows, :], g1_ref[...])
                dx_ref[rows, :] = dx2_ref[rows, :] + dx
                dg8_ref[...] = dg8_ref[...] + _fold8(dgrow)

            units = []
            for q in range(n_sub):
                units += [(mixer_a, (q * rc, l0)) for l0 in range(0, D_A, LANES)]
                units += [(mixer_b, (q * rc, l0)) for l0 in range(0, D_B, LANES)]
                units.append((tail, (q * rc,)))
            _interleaved(units, _matmul_pieces([(dz_old, wi_ref)], dh_new, 2))

        @pl.when(i % 2 == 0)
        def _():
            stage(dz0_ref, dz1_ref, dh0_ref, dh1_ref)

        @pl.when(i % 2 == 1)
        def _():
            stage(dz1_ref, dz0_ref, dh1_ref, dh0_ref)

        @pl.when(i == nt + 1)
        def _():
            _reduce_acc(dwb_ref, acc_ref, K_B)
            dg1_ref[...] = jnp.sum(dg8_ref[...], axis=0, keepdims=True)

    vtile = lambda i: jnp.minimum(i, nt - 1)
    ttile = lambda i: jnp.clip(i - 2, 0, nt - 1)
    return _call(
        body, name="bwd_mix_b", grid=(nt + 2,),
        in_specs=[_rows_at(ts, D_A, vtile), _prev_at(ts, D_A, vtile), _next_at(ts, D_A, s, vtile),
                  _rows_at(ts, D_B, vtile), _prev_at(ts, D_B, vtile), _next_at(ts, D_B, s, vtile),
                  _rows_at(ts, D_IN, vtile), _prev_at(ts, D_IN, vtile), _next_at(ts, D_IN, s, vtile), _rows_at(ts, D_A, vtile),
                  _const((K_A, D_A)), _const((K_B, D_B)), _const((D_IN, D_MODEL)), _rows_at(ts, D_MODEL, ttile),
                  _const((1, D_MODEL)), _rows_at(ts, D_MODEL, ttile)],
        out_specs=[_rows_at(ts, D_IN, vtile), _rows_at(ts, D_MODEL, ttile), _acc_out((1, D_MODEL)), _acc_out((K_B, D_B))],
        out_shape=[_sds((s, D_IN), BF16), _sds((s, D_MODEL), F32), _sds((1, D_MODEL), F32), _sds((K_B, D_B), F32)],
        scratch_shapes=[_ext_scratch(ts, D_A), _ext_scratch(ts, D_B), _ext_scratch(ts, D_B),
                        pltpu.VMEM((K_B, SUBLANES, D_B), F32), pltpu.VMEM((SUBLANES, D_MODEL), F32),
                        pltpu.VMEM((ts, D_IN), BF16), pltpu.VMEM((ts, D_IN), BF16),
                        pltpu.VMEM((ts, D_MODEL), F32), pltpu.VMEM((ts, D_MODEL), F32)],
        args=(dca, dca, dca, du, du, du, z, z, z, dab, wa, wb, w_in, x, g1, dx2), exchange=exchange)


def _matmul_tn(a, b, name, exchange=None):
    s, m = a.shape
    n = b.shape[1]
    tk = min(1024, s)
    nk = s // tk
    tm = 256

    def body(a_ref, b_ref, o_ref, acc_ref):
        k = pl.program_id(0)

        @pl.when(k == 0)
        def _():
            acc_ref[...] = jnp.zeros_like(acc_ref)

        for m0 in range(0, m, tm):
            acc_ref[m0:m0 + tm, :] = acc_ref[m0:m0 + tm, :] + lax.dot_general(
                a_ref[:, m0:m0 + tm], b_ref[...], _TN, preferred_element_type=F32)

        @pl.when(k == nk - 1)
        def _():
            o_ref[...] = acc_ref[...].astype(BF16)

    (out,), got = _call(
        body, name=name, grid=(nk,),
        in_specs=[_rows(tk, m), _rows(tk, n)],
        out_specs=[_acc_out((m, n))],
        out_shape=[_sds((m, n), BF16)],
        scratch_shapes=[pltpu.VMEM((m, n), F32)], args=(a, b), exchange=exchange)
    return out if exchange is None else (out, got)


CHIP_RELS = ((1, 0, 0), (0, 1, 0), (1, 1, 0))
CORE_RELS = ((0, 0, 1),)
ALL_RELS = ((0, 0, 1), (0, 1, 0), (0, 1, 1), (1, 0, 0), (1, 0, 1), (1, 1, 0), (1, 1, 1))


def _chip_slot(dev):
    return 2 * dev[0] + dev[1]


def _dev_slot(dev):
    return 4 * dev[0] + 2 * dev[1] + dev[2]


def _me():
    return (lax.axis_index("x"), lax.axis_index("y"), lax.axis_index("c"))


def _peer(me, rel):
    return tuple((1 - me[a]) if rel[a] else me[a] for a in range(3))


_ANY = pl.BlockSpec(memory_space=pl.ANY)


class _Exchange:
    def __init__(self, inputs, out_shape, scratch, start, finish, forward=None):
        self.inputs, self.out_shape, self.scratch = list(inputs), list(out_shape), list(scratch)
        self.start, self.finish, self.forward = start, finish, forward


def _all_gather(payloads):
    n_p = len(payloads)
    n_k = 1 + 2 * len(CHIP_RELS)

    def copy(srcs, dsts, sems, p, k, block_dev, to, from_src):
        blk = dsts[p].at[_dev_slot(block_dev)]
        return pltpu.make_async_remote_copy(
            src_ref=srcs[p] if from_src else blk, dst_ref=blk,
            send_sem=sems[0].at[n_k * p + k], recv_sem=sems[1].at[n_k * p + k], device_id=to, device_id_type=MESH)

    def own_copy(srcs, dsts, sems, p):
        return pltpu.make_async_copy(srcs[p], dsts[p].at[_dev_slot(_me())], sems[2].at[p])

    def start(srcs, dsts, sems):
        me = _me()
        for p in range(n_p):
            own_copy(srcs, dsts, sems, p).start()
        for j, rel in enumerate(CHIP_RELS):
            for p in range(n_p):
                copy(srcs, dsts, sems, p, 1 + j, me, _peer(me, rel), True).start()
        for p in range(n_p):
            copy(srcs, dsts, sems, p, 0, me, _peer(me, CORE_RELS[0]), True).start()

    def forward(srcs, dsts, sems):
        me = _me()
        sibling = _peer(me, CORE_RELS[0])
        for j, rel in enumerate(CHIP_RELS):
            other = _peer(me, rel)
            for p in range(n_p):
                copy(srcs, dsts, sems, p, 1 + j, other, me, False).wait_recv()
                copy(srcs, dsts, sems, p, 4 + j, other, sibling, False).start()

    def finish(srcs, dsts, sems):
        me = _me()
        sibling = _peer(me, CORE_RELS[0])
        for p in range(n_p):
            copy(srcs, dsts, sems, p, 0, sibling, me, False).wait_recv()
        for j, rel in enumerate(CHIP_RELS):
            for p in range(n_p):
                copy(srcs, dsts, sems, p, 4 + j, _peer(sibling, rel), me, False).wait_recv()
        for p in range(n_p):
            own_copy(srcs, dsts, sems, p).wait()
            copy(srcs, dsts, sems, p, 0, me, sibling, True).wait_send()
            for j, rel in enumerate(CHIP_RELS):
                copy(srcs, dsts, sems, p, 1 + j, me, _peer(me, rel), True).wait_send()
                copy(srcs, dsts, sems, p, 4 + j, _peer(me, rel), sibling, False).wait_send()

    return _Exchange(
        payloads, [_sds((N_DEV,) + p.shape, p.dtype) for p in payloads],
        [pltpu.SemaphoreType.DMA((n_p * n_k,)), pltpu.SemaphoreType.DMA((n_p * n_k,)), pltpu.SemaphoreType.DMA((n_p,))],
        start, finish, forward)


def _gather_direct(payload):
    n_r = len(ALL_RELS)

    def copies(srcs, dsts, sems):
        me = _me()
        mine = dsts[0].at[_dev_slot(me)]
        own = pltpu.make_async_copy(srcs[0], mine, sems[2].at[0])
        remote = [pltpu.make_async_remote_copy(src_ref=srcs[0], dst_ref=mine, send_sem=sems[0].at[k], recv_sem=sems[1].at[k],
                                               device_id=_peer(me, rel), device_id_type=MESH)
                  for k, rel in enumerate(ALL_RELS)]
        return [own] + remote

    def start(srcs, dsts, sems):
        for cp in copies(srcs, dsts, sems):
            cp.start()

    def finish(srcs, dsts, sems):
        for cp in copies(srcs, dsts, sems):
            cp.wait()

    return _Exchange([payload], [_sds((N_DEV,) + payload.shape, payload.dtype)],
                     [pltpu.SemaphoreType.DMA((n_r,)), pltpu.SemaphoreType.DMA((n_r,)), pltpu.SemaphoreType.DMA((1,))],
                     start, finish)


def _scatter_exchange(payloads, rels, src_view, view_shapes):
    n_p = len(payloads)
    n_r = len(rels)

    def copies(srcs, dsts, sems):
        me = _me()
        out = []
        for k, rel in enumerate(rels):
            peer = _peer(me, rel)
            for p in range(n_p):
                out.append(pltpu.make_async_remote_copy(
                    src_ref=src_view(srcs[p], peer), dst_ref=dsts[p].at[k],
                    send_sem=sems[0].at[p * n_r + k], recv_sem=sems[1].at[p * n_r + k],
                    device_id=peer, device_id_type=MESH))
        return out

    def start(srcs, dsts, sems):
        for cp in copies(srcs, dsts, sems):
            cp.start()

    def finish(srcs, dsts, sems):
        for cp in copies(srcs, dsts, sems):
            cp.wait()

    return _Exchange(payloads, [_sds((n_r,) + vs, p.dtype) for vs, p in zip(view_shapes, payloads)],
                     [pltpu.SemaphoreType.DMA((n_p * n_r,)), pltpu.SemaphoreType.DMA((n_p * n_r,))], start, finish)


def _split_refs(refs, sizes):
    out, at = [], 0
    for n in sizes:
        out.append(refs[at:at + n])
        at += n
    return out


def _join(exchanges):
    n_in = [len(e.inputs) for e in exchanges]
    n_out = [len(e.out_shape) for e in exchanges]
    n_sc = [len(e.scratch) for e in exchanges]

    def phase(name):
        def run(ins, outs, scs):
            for e, i, o, s in zip(exchanges, _split_refs(ins, n_in), _split_refs(outs, n_out), _split_refs(scs, n_sc)):
                if getattr(e, name) is not None:
                    getattr(e, name)(i, o, s)
        return run

    return _Exchange([a for e in exchanges for a in e.inputs], [s for e in exchanges for s in e.out_shape],
                     [s for e in exchanges for s in e.scratch], phase("start"), phase("finish"),
                     phase("forward") if any(e.forward is not None for e in exchanges) else None)


def _run_exchanges(name, exchanges):
    n_in = [len(e.inputs) for e in exchanges]
    n_out = [len(e.out_shape) for e in exchanges]
    n_sc = [len(e.scratch) for e in exchanges]

    def body(*refs):
        ins, outs, scs = _split_refs(refs, [sum(n_in), sum(n_out), sum(n_sc)])
        parts = list(zip(exchanges, _split_refs(ins, n_in), _split_refs(outs, n_out), _split_refs(scs, n_sc)))
        for e, i, o, s in parts:
            e.start(i, o, s)
        for e, i, o, s in parts:
            if e.forward is not None:
                e.forward(i, o, s)
        for e, i, o, s in parts:
            e.finish(i, o, s)

    outs = pl.pallas_call(
        body, name=name, in_specs=[_ANY] * sum(n_in), out_specs=[_ANY] * sum(n_out),
        out_shape=[sd for e in exchanges for sd in e.out_shape],
        scratch_shapes=[sc for e in exchanges for sc in e.scratch],
    )(*[a for e in exchanges for a in e.inputs])
    return _split_refs(list(outs), n_out)


def _call(body, *, name, grid, in_specs, out_specs, out_shape, scratch_shapes, args, exchange=None, forward_step=None):
    n_in, n_out, n_sc = len(in_specs), len(out_specs), len(scratch_shapes)
    if exchange is None:
        outs = pl.pallas_call(body, name=name, grid=grid, in_specs=in_specs, out_specs=out_specs, out_shape=out_shape,
                              scratch_shapes=scratch_shapes, compiler_params=_params())(*args)
        return list(outs), []
    e = exchange
    sizes = [n_in, len(e.inputs), n_out, len(e.out_shape), n_sc, len(e.scratch)]
    last = grid[0] - 1

    def wrapped(*refs):
        a, ei, o, eo, sc, es = _split_refs(refs, sizes)
        i = pl.program_id(0)

        @pl.when(i == 0)
        def _():
            e.start(ei, eo, es)

        if e.forward is not None:
            @pl.when(i == forward_step)
            def _():
                e.forward(ei, eo, es)

        body(*a, *o, *sc)

        @pl.when(i == last)
        def _():
            e.finish(ei, eo, es)

    outs = pl.pallas_call(
        wrapped, name=name, grid=grid,
        in_specs=list(in_specs) + [_ANY] * len(e.inputs), out_specs=list(out_specs) + [_ANY] * len(e.out_shape),
        out_shape=list(out_shape) + e.out_shape, scratch_shapes=list(scratch_shapes) + e.scratch,
        compiler_params=_params(),
    )(*args, *e.inputs)
    outs = list(outs)
    return outs[:n_out], outs[n_out:]


def _pair_sum(grads, recvd, my_core, name):
    n_p = len(grads)

    def body(c_ref, *refs):
        del c_ref
        for p in range(n_p):
            refs[2 * n_p + p][...] = (refs[p][...].astype(F32) + refs[n_p + p][...].astype(F32)).astype(BF16)

    def blk(g):
        return (None, None) + g.shape[2:]

    return pl.pallas_call(
        body, name=name,
        grid_spec=pltpu.PrefetchScalarGridSpec(
            num_scalar_prefetch=1, grid=(N_CHIP,),
            in_specs=[pl.BlockSpec(blk(g), lambda j, c: (j, c[0], 0, 0)) for g in grads]
            + [pl.BlockSpec(blk(g), lambda j, c: (0, j, 0, 0)) for g in grads],
            out_specs=[pl.BlockSpec((None,) + g.shape[2:], lambda j, c: (j, 0, 0)) for g in grads]),
        out_shape=[_sds((N_CHIP,) + g.shape[2:], BF16) for g in grads],
        compiler_params=pltpu.CompilerParams(dimension_semantics=("arbitrary",), vmem_limit_bytes=VMEM_LIMIT),
    )(my_core, *grads, *recvd)


def _chip_sum(psums, recvd, my_chip, name):
    n_p = len(psums)

    def body(c_ref, *refs):
        del c_ref
        for p in range(n_p):
            acc = refs[p][...].astype(F32)
            for k in range(len(CHIP_RELS)):
                acc = acc + refs[n_p + p][k].astype(F32)
            refs[2 * n_p + p][...] = acc

    return pl.pallas_call(
        body, name=name,
        grid_spec=pltpu.PrefetchScalarGridSpec(
            num_scalar_prefetch=1, grid=(1,),
            in_specs=[pl.BlockSpec((None,) + g.shape[1:], lambda i, c: (c[0], 0, 0)) for g in psums]
            + [pl.BlockSpec(r.shape, lambda i, c: (0, 0, 0)) for r in recvd],
            out_specs=[pl.BlockSpec(g.shape[1:], lambda i, c: (0, 0)) for g in psums]),
        out_shape=[_sds(g.shape[1:], F32) for g in psums],
        compiler_params=pltpu.CompilerParams(dimension_semantics=("arbitrary",), vmem_limit_bytes=VMEM_LIMIT),
    )(my_chip, *psums, *recvd)


def _sum_devices(parts, shapes):
    plan = _small_rows(shapes)

    def body(p_ref, *refs):
        outs, tot_ref = refs[:-1], refs[-1]
        acc = p_ref[0]
        for j in range(1, N_DEV):
            acc = acc + p_ref[j]
        tot_ref[...] = acc
        for idx, r, c0, width, at in plan:
            outs[idx][r:r + 1, c0:c0 + width] = tot_ref[at:at + 1, 0:width]

    return pl.pallas_call(body, name="small_grad_sum", out_shape=[_sds(s, F32) for s in shapes],
                          scratch_shapes=[pltpu.VMEM(parts.shape[1:], F32)])(parts)


def _cast_shards(shards):
    def body(*refs):
        for src, dst in zip(refs[:len(shards)], refs[len(shards):]):
            dst[...] = src[...].astype(BF16)

    return pl.pallas_call(body, name="cast_shards", out_shape=[_sds(a.shape, BF16) for a in shards],
                          compiler_params=pltpu.CompilerParams(vmem_limit_bytes=VMEM_LIMIT))(*shards)


def _adamw(ws, gs, ms, vs, name):
    n_t = len(ws)

    def body(*refs):
        w_refs, g_refs, m_refs, v_refs = (refs[j * n_t:(j + 1) * n_t] for j in range(4))
        outs = refs[4 * n_t:]
        for k in range(n_t):
            gg = g_refs[k][...]
            mn = ADAM_B1 * m_refs[k][...] + (1.0 - ADAM_B1) * gg
            vn = ADAM_B2 * v_refs[k][...] + (1.0 - ADAM_B2) * (gg * gg)
            m_hat = mn / (1.0 - ADAM_B1 ** ADAM_STEP)
            v_hat = vn / (1.0 - ADAM_B2 ** ADAM_STEP)
            outs[3 * k][...] = -ADAM_LR * (m_hat / (jnp.sqrt(v_hat) + ADAM_EPS) + ADAM_WD * w_refs[k][...])
            outs[3 * k + 1][...] = mn
            outs[3 * k + 2][...] = vn

    out_shape = [_sds(w.shape, F32) for w in ws for _ in range(3)]
    return pl.pallas_call(body, name=name, out_shape=out_shape,
                          compiler_params=pltpu.CompilerParams(vmem_limit_bytes=VMEM_LIMIT))(*ws, *gs, *ms, *vs)


class _Mesh:
    def __init__(self, shards, my_chip, my_core):
        self.shards, self.my_chip, self.my_core = shards, my_chip.reshape(1), my_core.reshape(1)

    def gather(self, names):
        return _all_gather([self.shards[n] for n in names])

    @staticmethod
    def whole(gathered):
        return gathered.reshape(N_DEV * gathered.shape[1], gathered.shape[2])

    @staticmethod
    def by_device(grads):
        return [g.reshape(N_CHIP, 2, g.shape[0] // N_DEV, g.shape[1]) for g in grads]

    @staticmethod
    def to_sibling(parts):
        return _scatter_exchange(parts, CORE_RELS, lambda ref, peer: ref.at[:, peer[2]],
                                 [(N_CHIP,) + p.shape[2:] for p in parts])

    @staticmethod
    def to_chips(pair):
        return _scatter_exchange(pair, CHIP_RELS, lambda ref, peer: ref.at[_chip_slot(peer)], [p.shape[1:] for p in pair])


def _step(x, target, g1, w_in_t, wa, wb, bb, lg, lb, w_out, g2, w_gate_t, w_up_t, wf, w_down, g3, ts, mesh=None):
    (z, h1), got = _fwd_in(x, g1, w_in_t, ts, exchange=mesh and mesh.gather(["w_out"]))
    if mesh:
        w_out = mesh.whole(got[0])
    (x2, y, u), got = _fwd_mix(z, x, wa, wb, bb, lg, lb, w_out, ts, exchange=mesh and mesh.gather(["w_gate", "w_up"]))
    if mesh:
        w_gate_t, w_up_t = [mesh.whole(g) for g in got]
    (g0, v, h2), got = _fwd_ffn_in(x2, g2, w_gate_t, w_up_t, ts, exchange=mesh and mesh.gather(["w_down"]))
    if mesh:
        w_down = mesh.whole(got[0])
    tk = min(ts, SKEW_TILE)
    a, dx3, dx3b, loss, dg3, dgc, dv, dwf = _ffn_out_and_back(g0, v, x2, wf, w_down, g3, target, tk)
    one = dict(w_down=_matmul_tn(a, dx3b, "wgrad_down"), w_up=_matmul_tn(dv, h2, "wgrad_up"))
    parts1 = mesh and mesh.by_device(list(one.values()))
    (dg0, dx2, dx2b, dg2), got = _bwd_ffn_b(dgc, dv, wf, w_gate_t, w_up_t, x2, g2, dx3, tk,
                                            exchange=mesh and mesh.to_sibling(parts1))
    pair1 = mesh and _pair_sum(parts1, got, mesh.my_core, "rs_pair_sum_1")
    two = dict(w_gate=_matmul_tn(dg0, h2, "wgrad_gate"), w_out=_matmul_tn(y, dx2b, "wgrad_out"))
    parts2 = mesh and mesh.by_device(list(two.values()))
    (dca, du, dab, dwa, dlg, dlb, dbb), got = _bwd_mix_a(
        dx2b, jnp.swapaxes(w_out, 0, 1), z, u, wa, lg, lb, tk,
        exchange=mesh and _join([mesh.to_chips(pair1), mesh.to_sibling(parts2)]))
    (dz, dx, dg1, dwb), _ = _bwd_mix_b(dca, du, z, dab, wa, wb, w_in_t, x, g1, dx2, tk)
    small = dict(norm_mix_g=dg1, conv_a_w=dwa, conv_b_w=dwb, conv_b_b=dbb, ln_b_g=dlg, ln_b_b=dlb,
                 norm_ffn_g=dg2, conv_ffn_w=dwf, norm_final_g=dg3)
    if not mesh:
        return loss, dx, dict(w_in=_matmul_tn(dz, h1, "wgrad_in"), **one, **two), small
    big = dict(zip(one, _chip_sum(pair1, got[:2], mesh.my_chip, "rs_chip_sum_1")))
    pair2 = _pair_sum(parts2, got[2:], mesh.my_core, "rs_pair_sum_2")
    dw_in_t, got = _matmul_tn(dz, h1, "wgrad_in",
                              exchange=_join([mesh.to_chips(pair2), _gather_direct(_pack_small_grads(small, loss))]))
    big.update(zip(two, _chip_sum(pair2, got[:2], mesh.my_chip, "rs_chip_sum_2")))
    every = got[2]
    parts = mesh.by_device([dw_in_t])
    (got,) = _run_exchanges("rs_cores_last", [mesh.to_sibling(parts)])
    pair = _pair_sum(parts, got, mesh.my_core, "rs_pair_sum_last")
    (got,) = _run_exchanges("rs_chips_last", [mesh.to_chips(pair)])
    (big["w_in"],) = _chip_sum(pair, got, mesh.my_chip, "rs_chip_sum_last")
    shapes = [loss.shape if n == "loss" else small[n].shape for n in _SMALL_NAMES]
    return None, dx, big, dict(zip(_SMALL_NAMES, _sum_devices(every, shapes)))


def _pack_small_weights(conv_a_s, conv_b_s, conv_ffn_s):
    def body(a_ref, b_ref, f_ref, out):
        out[...] = jnp.zeros_like(out)
        at = 0
        for src in (a_ref, b_ref, f_ref):
            rows, cols = src.shape
            for r in range(rows):
                out[at:at + 1, 0:cols] = src[r:r + 1, :]
                at += 1

    return pl.pallas_call(body, name="small_weight_pack", out_shape=_sds((SMALL_W_ROWS, SMALL_W_COLS), F32))(
        conv_a_s, conv_b_s, conv_ffn_s)


def _unpack_small_weights(full):
    def take(r0, k, w):
        return jnp.transpose(full[:, r0:r0 + k, 0:w], (1, 0, 2)).reshape(k, N_DEV * w)

    return take(0, K_A, CONV_A_COLS), take(K_A, K_B, CONV_A_COLS), take(K_A + K_B, K_F, W_FF_COLS)


_SMALL_NAMES = ("conv_b_w", "conv_a_w", "conv_ffn_w", "norm_mix_g", "norm_ffn_g", "norm_final_g",
                "conv_b_b", "ln_b_g", "ln_b_b", "loss")
SMALL_G_ROWS = 64


def _small_rows(shapes):
    plan, at = [], 0
    for idx, (rows, cols) in enumerate(shapes):
        for r in range(rows):
            for c0 in range(0, cols, SMALL_G_COLS):
                plan.append((idx, r, c0, min(SMALL_G_COLS, cols - c0), at))
                at += 1
    assert at <= SMALL_G_ROWS
    return plan


def _pack_small_grads(small, loss):
    srcs = [loss if n == "loss" else small[n] for n in _SMALL_NAMES]
    plan = _small_rows([a.shape for a in srcs])

    def body(*refs):
        out = refs[-1]
        out[...] = jnp.zeros_like(out)
        for idx, r, c0, width, at in plan:
            out[at:at + 1, 0:width] = refs[idx][r:r + 1, c0:c0 + width]

    return pl.pallas_call(body, name="small_grad_pack", out_shape=_sds((SMALL_G_ROWS, SMALL_G_COLS), F32))(*srcs)


def kernel(x, norm_mix_g, w_in, conv_a_w, conv_b_w, conv_b_b, ln_b_g, ln_b_b, w_out, norm_ffn_g, w_gate, w_up, conv_ffn_w, w_down, norm_final_g, loss_target, m_norm_mix_g, m_w_in, m_conv_a_w, m_conv_b_w, m_conv_b_b, m_ln_b_g, m_ln_b_b, m_w_out, m_norm_ffn_g, m_w_gate, m_w_up, m_conv_ffn_w, m_w_down, m_norm_final_g, v_norm_mix_g, v_w_in, v_conv_a_w, v_conv_b_w, v_conv_b_b, v_ln_b_g, v_ln_b_b, v_w_out, v_norm_ffn_g, v_w_gate, v_w_up, v_conv_ffn_w, v_w_down, v_norm_final_g):
    ix, iy, ic = lax.axis_index("x"), lax.axis_index("y"), lax.axis_index("c")
    my_chip = (2 * ix + iy).astype(jnp.int32)
    my_core = ic.astype(jnp.int32)
    my_dev = 2 * my_chip + my_core

    weights = dict(norm_mix_g=norm_mix_g, w_in=w_in, conv_a_w=conv_a_w, conv_b_w=conv_b_w, conv_b_b=conv_b_b,
                   ln_b_g=ln_b_g, ln_b_b=ln_b_b, w_out=w_out, norm_ffn_g=norm_ffn_g, w_gate=w_gate, w_up=w_up,
                   conv_ffn_w=conv_ffn_w, w_down=w_down, norm_final_g=norm_final_g)
    m_in = dict(norm_mix_g=m_norm_mix_g, w_in=m_w_in, conv_a_w=m_conv_a_w, conv_b_w=m_conv_b_w, conv_b_b=m_conv_b_b,
                ln_b_g=m_ln_b_g, ln_b_b=m_ln_b_b, w_out=m_w_out, norm_ffn_g=m_norm_ffn_g, w_gate=m_w_gate,
                w_up=m_w_up, conv_ffn_w=m_conv_ffn_w, w_down=m_w_down, norm_final_g=m_norm_final_g)
    v_in = dict(norm_mix_g=v_norm_mix_g, w_in=v_w_in, conv_a_w=v_conv_a_w, conv_b_w=v_conv_b_w, conv_b_b=v_conv_b_b,
                ln_b_g=v_ln_b_g, ln_b_b=v_ln_b_b, w_out=v_w_out, norm_ffn_g=v_norm_ffn_g, w_gate=v_w_gate,
                w_up=v_w_up, conv_ffn_w=v_conv_ffn_w, w_down=v_w_down, norm_final_g=v_norm_final_g)
    order = list(weights)
    big_names = ("w_in", "w_gate", "w_up", "w_out", "w_down")
    transposed = ("w_in", "w_gate", "w_up")

    def shard2d(name, a):
        if name in transposed:
            return jnp.swapaxes(a[0], 0, 1)
        return a.reshape(1, a.shape[0]) if a.ndim == 1 else a.reshape(a.shape[-2:])

    def unshard2d(name, a2, like):
        if name in transposed:
            return jnp.swapaxes(a2, 0, 1)[None]
        return a2.reshape(like.shape)

    mesh = _Mesh(dict(zip(big_names, _cast_shards([shard2d(n, weights[n]) for n in big_names]))), my_chip, my_core)
    gathered, = _run_exchanges("ag_first", [_all_gather(
        [mesh.shards["w_in"], _pack_small_weights(conv_a_w[0], conv_b_w[0], conv_ffn_w[0])])])
    w_in_t = mesh.whole(gathered[0])
    wa_f, wb_f, wf_f = _unpack_small_weights(gathered[1])

    _, dx, gsum, stot = _step(
        x[0], loss_target[0], norm_mix_g, w_in_t, wa_f, wb_f, conv_b_b, ln_b_g, ln_b_b, None, norm_ffn_g,
        None, None, wf_f, None, norm_final_g.reshape(1, D_MODEL), SEQ_TILE, mesh)

    grads2d = dict(
        norm_mix_g=stot["norm_mix_g"],
        conv_a_w=lax.dynamic_slice(stot["conv_a_w"], (0, my_dev * CONV_A_COLS), (K_A, CONV_A_COLS)),
        conv_b_w=lax.dynamic_slice(stot["conv_b_w"], (0, my_dev * CONV_A_COLS), (K_B, CONV_A_COLS)),
        conv_b_b=stot["conv_b_b"], ln_b_g=stot["ln_b_g"], ln_b_b=stot["ln_b_b"],
        norm_ffn_g=stot["norm_ffn_g"],
        conv_ffn_w=lax.dynamic_slice(stot["conv_ffn_w"], (0, my_dev * W_FF_COLS), (K_F, W_FF_COLS)),
        norm_final_g=stot["norm_final_g"],
        **gsum,
    )

    updates = {}
    small_names = [n for n in order if n not in big_names]
    for group, label in [([n], "adamw_" + n) for n in big_names] + [(small_names, "adamw_small")]:
        outs = _adamw([shard2d(n, weights[n]) for n in group], [grads2d[n] for n in group],
                      [shard2d(n, m_in[n]) for n in group], [shard2d(n, v_in[n]) for n in group], label)
        for k, n in enumerate(group):
            updates[n] = outs[3 * k:3 * k + 3]
    g_out = [unshard2d(n, grads2d[n], weights[n]) for n in order]
    d_out, m_out, v_out = [[unshard2d(n, updates[n][j], weights[n]) for n in order] for j in range(3)]

    return (stot["loss"][0, 0], dx[None], *g_out, *d_out, *m_out, *v_out)
```

```python
import jax
import jax.numpy as jnp
from jax import lax
from jax.experimental import pallas as pl
from jax.experimental.pallas import tpu as pltpu

F32 = jnp.float32
BF16 = jnp.bfloat16

D_MODEL = 1024
D_A = 512
D_B = 512
D_IN = 3 * D_A + 2 * D_B
D_FF = 2816
K_A = 3
K_B = 31
K_F = 3
RMS_EPS = 1e-6
LN_EPS = 1e-5

ADAM_LR = 0.001
ADAM_B1 = 0.9
ADAM_B2 = 0.999
ADAM_EPS = 1e-08
ADAM_WD = 0.01
ADAM_STEP = 10

N_DEV = 8
N_CHIP = 4
LANES = 128
SUBLANES = 8
HALO = 16
ROW_CHUNK = 64
SEQ_TILE = 512
SKEW_TILE = 256
VMEM_LIMIT = 56 * 1024 * 1024

MESH = pl.DeviceIdType.MESH

W_FF_COLS = D_FF // N_DEV
CONV_A_COLS = D_A // N_DEV
SMALL_W_ROWS = 40
SMALL_W_COLS = 384
SMALL_G_COLS = 512
FF_PAD = 3072


def _rows(ts, c):
    return pl.BlockSpec((ts, c), lambda i: (i, 0))


def _const(shape):
    return pl.BlockSpec(shape, lambda i: (0,) * len(shape), pipeline_mode=pl.Buffered(1))


def _acc_out(shape):
    return pl.BlockSpec(shape, lambda i: (0,) * len(shape))


def _rows_at(ts, c, tile):
    return pl.BlockSpec((ts, c), lambda i: (tile(i), 0))


def _prev_at(ts, c, tile):
    return pl.BlockSpec((HALO, c), lambda i: (jnp.maximum(tile(i) * (ts // HALO) - 1, 0), 0))


def _next_at(ts, c, s, tile):
    last = s // HALO - 1
    return pl.BlockSpec((HALO, c), lambda i: (jnp.minimum((tile(i) + 1) * (ts // HALO), last), 0))


def _prev(ts, c):
    return _prev_at(ts, c, lambda i: i)


def _next(ts, c, s):
    return _next_at(ts, c, s, lambda i: i)


MXU_COLS = 256
MXU_ROWS = 256


def _col_pieces(n):
    return [(c0, min(MXU_COLS, n - c0)) for c0 in range(0, n, MXU_COLS)]


def _matmul_pieces(terms, out_ref, k_parts, w_transposed=False):
    m, n = out_ref.shape
    rows = min(MXU_ROWS, m)
    steps = []
    for lhs_ref, w_ref in terms:
        tiles = lhs_ref.shape[1] // MXU_COLS
        cuts = [MXU_COLS * (tiles * j // k_parts) for j in range(k_parts)] + [lhs_ref.shape[1]]
        steps += [(lhs_ref, w_ref, cuts[j], cuts[j + 1]) for j in range(k_parts)]

    def piece(m0, n0, width, step):
        lhs_ref, w_ref, k0, k1 = steps[step]
        if w_transposed:
            part = lax.dot_general(lhs_ref[m0:m0 + rows, k0:k1], w_ref[n0:n0 + width, k0:k1], _NT,
                                   preferred_element_type=F32)
        else:
            part = jnp.dot(lhs_ref[m0:m0 + rows, k0:k1], w_ref[k0:k1, n0:n0 + width], preferred_element_type=F32)
        if step:
            part = part + out_ref[m0:m0 + rows, n0:n0 + width]
        out_ref[m0:m0 + rows, n0:n0 + width] = part

    return [(piece, (m0, n0, w, j)) for j in range(len(steps)) for n0, w in _col_pieces(n) for m0 in range(0, m, rows)]


def _interleaved(vector_units, matmul_pieces):
    n_u, n_p = len(vector_units), len(matmul_pieces)
    done = 0
    for k, (unit, args) in enumerate(vector_units):
        while done < n_p and done * n_u <= k * n_p:
            matmul_pieces[done][0](*matmul_pieces[done][1])
            done += 1
        unit(*args)
    for fn, args in matmul_pieces[done:]:
        fn(*args)


def _params():
    return pltpu.CompilerParams(dimension_semantics=("arbitrary",), vmem_limit_bytes=VMEM_LIMIT)


def _sds(shape, dtype):
    return jax.ShapeDtypeStruct(shape, dtype)


def _sigmoid(v):
    return 0.5 * jnp.tanh(0.5 * v) + 0.5


def _conv_block(ext_ref, w_ref, r0, rc, l0, k_taps, transposed):
    acc = None
    for k in range(k_taps):
        d = (k_taps // 2 - k) if transposed else (k - k_taps // 2)
        term = ext_ref[l0 // LANES, pl.ds(r0 + HALO + d, rc), :] * w_ref[k:k + 1, l0:l0 + LANES]
        acc = term if acc is None else acc + term
    return acc


def _conv_wgrad_block(acc_ref, dout, ext_ref, r0, rc, l0, k_taps, scale=None):
    for k in range(k_taps):
        prod = dout * ext_ref[l0 // LANES, pl.ds(r0 + HALO + k - k_taps // 2, rc), :]
        part = prod.reshape(rc // SUBLANES, SUBLANES, LANES).sum(axis=0)
        if scale is not None:
            part = part * scale
        acc_ref[k, :, l0:l0 + LANES] = acc_ref[k, :, l0:l0 + LANES] + part


def _reduce_acc(out_ref, acc_ref, k_taps):
    for k in range(k_taps):
        out_ref[k:k + 1, :] = jnp.sum(acc_ref[k], axis=0, keepdims=True)


def _fold8(v):
    rc, c = v.shape
    return v.reshape(rc // SUBLANES, SUBLANES, c).sum(axis=0)


def _ext_scratch(ts, c):
    return pltpu.VMEM((c // LANES, ts + 2 * HALO, LANES), F32)


def _put_rows(ext_ref, r0, rc, val):
    for q in range(val.shape[1] // LANES):
        ext_ref[q, pl.ds(r0 + HALO, rc), :] = val[:, q * LANES:(q + 1) * LANES]


def _fill_halo(ext_ref, vals_prev, vals_next, ts, first, last):
    for q in range(vals_prev.shape[1] // LANES):
        cols = slice(q * LANES, (q + 1) * LANES)
        ext_ref[q, 0:HALO, :] = jnp.where(first, 0.0, vals_prev[:, cols])
        ext_ref[q, HALO + ts:HALO + ts + HALO, :] = jnp.where(last, 0.0, vals_next[:, cols])


def _rms_bwd_rows(dh, xf, g):
    r = lax.rsqrt(jnp.mean(xf * xf, axis=-1, keepdims=True) + RMS_EPS)
    xhat = xf * r
    dxh = dh * g
    dx = r * (dxh - xhat * jnp.mean(dxh * xhat, axis=-1, keepdims=True))
    return dx, dh * xhat


_NT = (((1,), (1,)), ((), ()))
_TN = (((0,), (0,)), ((), ()))


def _fwd_in(x, g1, w_in_t, ts, exchange=None):
    s = x.shape[0]

    def body(x_ref, g_ref, w_ref, z_ref, h_ref):
        xf = x_ref[...]
        r = lax.rsqrt(jnp.mean(xf * xf, axis=-1, keepdims=True) + RMS_EPS)
        h = (xf * r * g_ref[...]).astype(BF16)
        h_ref[...] = h
        for n0 in range(0, D_IN, 512):
            z_ref[:, n0:n0 + 512] = lax.dot_general(h, w_ref[n0:n0 + 512, :], _NT,
                                                    preferred_element_type=F32).astype(BF16)

    return _call(
        body, name="fwd_in", grid=(s // ts,),
        in_specs=[_rows(ts, D_MODEL), _const((1, D_MODEL)), _const((D_IN, D_MODEL))],
        out_specs=[_rows(ts, D_IN), _rows(ts, D_MODEL)],
        out_shape=[_sds((s, D_IN), BF16), _sds((s, D_MODEL), BF16)],
        scratch_shapes=[], args=(x, g1, w_in_t), exchange=exchange, forward_step=(s // ts) * 3 // 4)


def _p_u0(z_ref, rows):
    a_h = z_ref[rows, 0:D_A].astype(F32)
    a_c = z_ref[rows, 2 * D_A:3 * D_A].astype(F32)
    b_v = z_ref[rows, 3 * D_A:3 * D_A + D_B].astype(F32)
    b_g = z_ref[rows, 3 * D_A + D_B:D_IN].astype(F32)
    return a_c * a_h, b_v * _sigmoid(b_g)


def _layernorm_rows(u_blocks):
    tot = None
    for ub in u_blocks:
        sm = jnp.sum(ub, axis=-1, keepdims=True)
        tot = sm if tot is None else tot + sm
    mu = tot * (1.0 / D_B)
    var = None
    for ub in u_blocks:
        sq = jnp.sum((ub - mu) * (ub - mu), axis=-1, keepdims=True)
        var = sq if var is None else var + sq
    rstd = lax.rsqrt(var * (1.0 / D_B) + LN_EPS)
    return mu, rstd


def _fwd_mix(z, x, wa, wb, bb, lg, lb, w_out, ts, exchange=None):
    s = x.shape[0]
    nt = s // ts
    rc = min(ROW_CHUNK, ts)

    def body(z_ref, zp_ref, zn_ref, x_ref, wa_ref, wb_ref, bb_ref, lg_ref, lb_ref, wo_ref,
             x2_ref, y_ref, u_ref, pe_ref, ue_ref):
        i = pl.program_id(0)
        pp, up = _p_u0(zp_ref, slice(None))
        pn, un = _p_u0(zn_ref, slice(None))
        _fill_halo(pe_ref, pp, pn, ts, i == 0, i == nt - 1)
        _fill_halo(ue_ref, up, un, ts, i == 0, i == nt - 1)

        def fill(j, carry):
            r0 = pl.multiple_of(j * rc, rc)
            p, u0 = _p_u0(z_ref, pl.ds(r0, rc))
            _put_rows(pe_ref, r0, rc, p)
            _put_rows(ue_ref, r0, rc, u0)
            return carry

        lax.fori_loop(0, ts // rc, fill, 0)

        def mixer_a(r0, l0):
            rows = pl.ds(r0, rc)
            ca = _conv_block(pe_ref, wa_ref, r0, rc, l0, K_A, False)
            a_b = z_ref[rows, D_A + l0:D_A + l0 + LANES].astype(F32)
            y_ref[rows, l0:l0 + LANES] = (a_b * ca).astype(BF16)

        def conv_b(r0, l0):
            u_ref[pl.ds(r0, rc), l0:l0 + LANES] = (_conv_block(ue_ref, wb_ref, r0, rc, l0, K_B, False)
                                                   + bb_ref[:, l0:l0 + LANES])

        def norm_b(r0):
            rows = pl.ds(r0, rc)
            ubs = [u_ref[rows, l0:l0 + LANES] for l0 in range(0, D_B, LANES)]
            mu, rstd = _layernorm_rows(ubs)
            for q, l0 in enumerate(range(0, D_B, LANES)):
                t = (ubs[q] - mu) * rstd * lg_ref[:, l0:l0 + LANES] + lb_ref[:, l0:l0 + LANES]
                y_ref[rows, D_A + l0:D_A + l0 + LANES] = (t * _sigmoid(t)).astype(BF16)

        def chunk_units(r0):
            return ([(mixer_a, (r0, l0)) for l0 in range(0, D_A, LANES)]
                    + [(conv_b, (r0, l0)) for l0 in range(0, D_B, LANES)] + [(norm_b, (r0,))])

        def main(j, carry):
            for fn, args in chunk_units(pl.multiple_of(j * rc, rc)):
                fn(*args)
            return carry

        def project(m0, n0, width):
            rows = slice(m0, m0 + hrows)
            x2_ref[rows, n0:n0 + width] = x_ref[rows, n0:n0 + width] + jnp.dot(
                y_ref[rows, :], wo_ref[:, n0:n0 + width], preferred_element_type=F32)

        hrows = ts // 2
        lax.fori_loop(0, hrows // rc, main, 0, unroll=4)
        _interleaved([u for q in range(hrows // rc) for u in chunk_units(hrows + q * rc)],
                     [(project, (0, n0, w)) for n0, w in _col_pieces(D_MODEL)])
        for n0, w in _col_pieces(D_MODEL):
            project(hrows, n0, w)

    return _call(
        body, name="fwd_mix", grid=(nt,),
        in_specs=[_rows(ts, D_IN), _prev(ts, D_IN), _next(ts, D_IN, s), _rows(ts, D_MODEL),
                  _const((K_A, D_A)), _const((K_B, D_B)), _const((1, D_B)), _const((1, D_B)), _const((1, D_B)),
                  _const((D_MODEL, D_MODEL))],
        out_specs=[_rows(ts, D_MODEL), _rows(ts, D_MODEL), _rows(ts, D_B)],
        out_shape=[_sds((s, D_MODEL), F32), _sds((s, D_MODEL), BF16), _sds((s, D_B), F32)],
        scratch_shapes=[_ext_scratch(ts, D_A), _ext_scratch(ts, D_B)],
        args=(z, z, z, x, wa, wb, bb, lg, lb, w_out), exchange=exchange, forward_step=nt * 5 // 8)


def _fwd_ffn_in(x2, g2, w_gate_t, w_up_t, ts, exchange=None):
    s = x2.shape[0]
    half = D_FF // 2

    def body(x_ref, g_ref, wg_ref, wu_ref, g0_ref, v_ref, h_ref):
        xf = x_ref[...]
        r = lax.rsqrt(jnp.mean(xf * xf, axis=-1, keepdims=True) + RMS_EPS)
        h = (xf * r * g_ref[...]).astype(BF16)
        h_ref[...] = h
        for n0 in range(0, D_FF, half):
            g0_ref[:, n0:n0 + half] = lax.dot_general(h, wg_ref[n0:n0 + half, :], _NT,
                                                      preferred_element_type=F32).astype(BF16)
            v_ref[:, n0:n0 + half] = lax.dot_general(h, wu_ref[n0:n0 + half, :], _NT,
                                                     preferred_element_type=F32).astype(BF16)

    return _call(
        body, name="fwd_ffn_in", grid=(s // ts,),
        in_specs=[_rows(ts, D_MODEL), _const((1, D_MODEL)), _const((D_FF, D_MODEL)), _const((D_FF, D_MODEL))],
        out_specs=[_rows(ts, D_FF), _rows(ts, D_FF), _rows(ts, D_MODEL)],
        out_shape=[_sds((s, D_FF), BF16), _sds((s, D_FF), BF16), _sds((s, D_MODEL), BF16)],
        scratch_shapes=[], args=(x2, g2, w_gate_t, w_up_t), exchange=exchange, forward_step=(s // ts) // 2)


def _ffn_out_and_back(g0, v, x2, wf, w_down, g3, target, ts):
    s = x2.shape[0]
    nt = s // ts
    rc = min(ROW_CHUNK, ts // 2)

    def body(g0_ref, gp_ref, gn_ref, v_ref, x2_ref, wf_ref, wd_ref, wdt_ref, g3_ref, t_ref,
             a_ref, dx3_ref, dx3b_ref, loss_ref, dg3_ref, dg_ref, dv_ref, dwf_ref,
             ge_ref, silu_ref, dsv_ref, da_ref, acc_ref, p_ref, sums_ref):
        i = pl.program_id(0)

        @pl.when(i == 0)
        def _():
            acc_ref[...] = jnp.zeros_like(acc_ref)
            sums_ref[...] = jnp.zeros_like(sums_ref)

        _fill_halo(ge_ref, gp_ref[...].astype(F32), gn_ref[...].astype(F32), ts, i == 0, i == nt - 1)

        def fill(j, carry):
            r0 = pl.multiple_of(j * rc, rc)
            _put_rows(ge_ref, r0, rc, g0_ref[pl.ds(r0, rc), :].astype(F32))
            return carry

        lax.fori_loop(0, ts // rc, fill, 0)

        def act(r0, l0):
            rows = slice(r0, r0 + rc)
            g = _conv_block(ge_ref, wf_ref, r0, rc, l0, K_F, False)
            vv = v_ref[rows, l0:l0 + LANES].astype(F32)
            sg = _sigmoid(g)
            silu = g * sg
            a_ref[rows, l0:l0 + LANES] = (silu * vv).astype(BF16)
            silu_ref[rows, l0:l0 + LANES] = silu
            dsv_ref[rows, l0:l0 + LANES] = (sg + silu * (1.0 - sg)) * vv

        def tail(r0):
            rows = slice(r0, r0 + rc)
            x3 = x2_ref[rows, :] + p_ref[rows, :]
            r = lax.rsqrt(jnp.mean(x3 * x3, axis=-1, keepdims=True) + RMS_EPS)
            xhat = x3 * r
            diff = xhat * g3_ref[...] - t_ref[rows, :]
            dout = diff * (1.0 / D_MODEL)
            dxh = dout * g3_ref[...]
            dx3 = r * (dxh - xhat * jnp.mean(dxh * xhat, axis=-1, keepdims=True))
            dx3_ref[rows, :] = dx3
            dx3b_ref[rows, :] = dx3.astype(BF16)
            sums_ref[0] = sums_ref[0] + _fold8(diff * diff)
            sums_ref[1] = sums_ref[1] + _fold8(dout * xhat)

        def back(r0, l0):
            rows = slice(r0, r0 + rc)
            da = da_ref[rows, l0:l0 + LANES]
            dv_ref[rows, l0:l0 + LANES] = (da * silu_ref[rows, l0:l0 + LANES]).astype(BF16)
            dgg = da * dsv_ref[rows, l0:l0 + LANES]
            dg_ref[rows, l0:l0 + LANES] = dgg.astype(BF16)
            _conv_wgrad_block(acc_ref, dgg, ge_ref, r0, rc, l0, K_F)

        hrows = ts // 2

        def units(fn, h, per_lane_block):
            starts = [h * hrows + q * rc for q in range(hrows // rc)]
            if per_lane_block:
                return [(fn, (r0, l0)) for r0 in starts for l0 in range(0, D_FF, LANES)]
            return [(fn, (r0,)) for r0 in starts]

        def rows_of(ref, h):
            return ref.at[pl.ds(h * hrows, hrows), :]

        def product(h):
            return _matmul_pieces([(rows_of(a_ref, h), wd_ref)], rows_of(p_ref, h), 2)

        def grad_a(h):
            return _matmul_pieces([(rows_of(dx3b_ref, h), wdt_ref)], rows_of(da_ref, h), 1)

        _interleaved(units(act, 0, True), [])
        _interleaved(units(act, 1, True), product(0))
        _interleaved(units(tail, 0, False), product(1))
        _interleaved(units(tail, 1, False), grad_a(0))
        _interleaved(units(back, 0, True), grad_a(1))
        _interleaved(units(back, 1, True), [])

        @pl.when(i == nt - 1)
        def _():
            _reduce_acc(dwf_ref, acc_ref, K_F)
            loss_ref[...] = (0.5 / D_MODEL) * jnp.sum(sums_ref[0], keepdims=True)
            dg3_ref[...] = jnp.sum(sums_ref[1], axis=0, keepdims=True)

    return pl.pallas_call(
        body, name="ffn_out_and_back", grid=(nt,),
        in_specs=[_rows(ts, D_FF), _prev(ts, D_FF), _next(ts, D_FF, s), _rows(ts, D_FF), _rows(ts, D_MODEL),
                  _const((K_F, D_FF)), _const((D_FF, D_MODEL)), _const((D_MODEL, D_FF)), _const((1, D_MODEL)),
                  _rows(ts, D_MODEL)],
        out_specs=[_rows(ts, D_FF), _rows(ts, D_MODEL), _rows(ts, D_MODEL), _acc_out((1, 1)), _acc_out((1, D_MODEL)),
                   _rows(ts, D_FF), _rows(ts, D_FF), _acc_out((K_F, D_FF))],
        out_shape=[_sds((s, D_FF), BF16), _sds((s, D_MODEL), F32), _sds((s, D_MODEL), BF16),
                   _sds((1, 1), F32), _sds((1, D_MODEL), F32),
                   _sds((s, D_FF), BF16), _sds((s, D_FF), BF16), _sds((K_F, D_FF), F32)],
        scratch_shapes=[_ext_scratch(ts, D_FF), pltpu.VMEM((ts, D_FF), F32), pltpu.VMEM((ts, D_FF), F32),
                        pltpu.VMEM((ts, D_FF), F32), pltpu.VMEM((K_F, SUBLANES, D_FF), F32),
                        pltpu.VMEM((ts, D_MODEL), F32), pltpu.VMEM((2, SUBLANES, D_MODEL), F32)],
        compiler_params=_params(),
    )(g0, g0, g0, v, x2, wf, w_down, jnp.swapaxes(w_down, 0, 1), g3, target)


def _bwd_ffn_b(dg, dv, wf, w_gate, w_up, x2, g2, dx3, ts, exchange=None):
    s = x2.shape[0]
    nt = s // ts
    rc = min(ROW_CHUNK, ts)
    n_sub = ts // rc

    def body(dg_ref, dgp_ref, dgn_ref, dv_ref, wf_ref, wg_ref, wu_ref, x2_ref, g2_ref, dx3_ref,
             dg0_ref, dx2_ref, dx2b_ref, dgn2_ref, dge_ref, dg8_ref, a0_ref, a1_ref, p0_ref, p1_ref):
        i = pl.program_id(0)
        vt = jnp.minimum(i, nt - 1)
        live = (i >= 2).astype(F32)

        @pl.when(i == 0)
        def _():
            dg8_ref[...] = jnp.zeros_like(dg8_ref)
            a1_ref[...] = jnp.zeros_like(a1_ref)
            p1_ref[...] = jnp.zeros_like(p1_ref)

        _fill_halo(dge_ref, dgp_ref[...].astype(F32), dgn_ref[...].astype(F32), ts, vt == 0, vt == nt - 1)

        def fill(j, carry):
            r0 = pl.multiple_of(j * rc, rc)
            _put_rows(dge_ref, r0, rc, dg_ref[pl.ds(r0, rc), :].astype(F32))
            return carry

        lax.fori_loop(0, n_sub, fill, 0)

        def stage(a_new, a_old, p_new, p_old):
            def conv_t(r0, l0):
                rows = slice(r0, r0 + rc)
                dg0 = _conv_block(dge_ref, wf_ref, r0, rc, l0, K_F, True).astype(BF16)
                dg0_ref[rows, l0:l0 + LANES] = dg0
                a_new[rows, l0:l0 + LANES] = dg0

            def tail(r0):
                rows = slice(r0, r0 + rc)
                dx, dgrow = _rms_bwd_rows(p_old[rows, :], x2_ref[rows, :], g2_ref[...])
                dx2 = dx3_ref[rows, :] + dx
                dx2_ref[rows, :] = dx2
                dx2b_ref[rows, :] = dx2.astype(BF16)
                dg8_ref[...] = dg8_ref[...] + _fold8(dgrow) * live

            units = []
            for q in range(n_sub):
                units += [(conv_t, (q * rc, l0)) for l0 in range(0, D_FF, LANES)]
                units.append((tail, (q * rc,)))
            _interleaved(units, _matmul_pieces([(a_old, wg_ref), (dv_ref, wu_ref)], p_new, 2))

        @pl.when(i % 2 == 0)
        def _():
            stage(a0_ref, a1_ref, p0_ref, p1_ref)

        @pl.when(i % 2 == 1)
        def _():
            stage(a1_ref, a0_ref, p1_ref, p0_ref)

        @pl.when(i == nt + 1)
        def _():
            dgn2_ref[...] = jnp.sum(dg8_ref[...], axis=0, keepdims=True)

    vtile = lambda i: jnp.minimum(i, nt - 1)
    mtile = lambda i: jnp.clip(i - 1, 0, nt - 1)
    ttile = lambda i: jnp.clip(i - 2, 0, nt - 1)
    return _call(
        body, name="bwd_ffn_b", grid=(nt + 2,),
        in_specs=[_rows_at(ts, D_FF, vtile), _prev_at(ts, D_FF, vtile), _next_at(ts, D_FF, s, vtile),
                  _rows_at(ts, D_FF, mtile), _const((K_F, D_FF)),
                  _const((D_FF, D_MODEL)), _const((D_FF, D_MODEL)), _rows_at(ts, D_MODEL, ttile), _const((1, D_MODEL)),
                  _rows_at(ts, D_MODEL, ttile)],
        out_specs=[_rows_at(ts, D_FF, vtile), _rows_at(ts, D_MODEL, ttile), _rows_at(ts, D_MODEL, ttile),
                   _acc_out((1, D_MODEL))],
        out_shape=[_sds((s, D_FF), BF16), _sds((s, D_MODEL), F32), _sds((s, D_MODEL), BF16), _sds((1, D_MODEL), F32)],
        scratch_shapes=[_ext_scratch(ts, D_FF), pltpu.VMEM((SUBLANES, D_MODEL), F32),
                        pltpu.VMEM((ts, D_FF), BF16), pltpu.VMEM((ts, D_FF), BF16),
                        pltpu.VMEM((ts, D_MODEL), F32), pltpu.VMEM((ts, D_MODEL), F32)],
        args=(dg, dg, dg, dv, wf, w_gate, w_up, x2, g2, dx3), exchange=exchange)


def _bwd_mix_a(dx2b, w_out_t, z, u, wa, lg, lb, ts, exchange=None):
    s = dx2b.shape[0]
    nt = s // ts
    rc = min(ROW_CHUNK, ts)
    n_sub = ts // rc

    def body(dx_ref, wo_ref, z_ref, zp_ref, zn_ref, u_ref, wa_ref, lg_ref, lb_ref,
             dca_ref, du_ref, dab_ref, dwa_ref, dlg_ref, dlb_ref, dbb_ref, pe_ref, dy0_ref, dy1_ref, acc_ref, sacc_ref):
        i = pl.program_id(0)
        t = jnp.maximum(i - 1, 0)

        @pl.when(i == 0)
        def _():
            acc_ref[...] = jnp.zeros_like(acc_ref)
            sacc_ref[...] = jnp.zeros_like(sacc_ref)
            dy1_ref[...] = jnp.zeros_like(dy1_ref)

        pp, _ = _p_u0(zp_ref, slice(None))
        pn, _ = _p_u0(zn_ref, slice(None))
        _fill_halo(pe_ref, pp, pn, ts, t == 0, t == nt - 1)

        def fill(j, carry):
            r0 = pl.multiple_of(j * rc, rc)
            rows = pl.ds(r0, rc)
            _put_rows(pe_ref, r0, rc, z_ref[rows, 2 * D_A:3 * D_A].astype(F32) * z_ref[rows, 0:D_A].astype(F32))
            return carry

        lax.fori_loop(0, n_sub, fill, 0)

        def stage(dy_new, dy_old):
            def piece(m0, n0, width):
                dy_new[m0:m0 + MXU_ROWS, n0:n0 + width] = jnp.dot(
                    dx_ref[m0:m0 + MXU_ROWS, :], wo_ref[:, n0:n0 + width], preferred_element_type=F32).astype(BF16)

            units = []
            for q in range(n_sub):
                units += [(mixer_a, (dy_old, q * rc, l0)) for l0 in range(0, D_A, LANES)]
                units.append((mixer_b, (dy_old, q * rc)))
            _interleaved(units, [(piece, (m0, n0, w)) for n0, w in _col_pieces(D_MODEL) for m0 in range(0, ts, MXU_ROWS)])

        def mixer_a(dy_ref, r0, l0):
            rows = slice(r0, r0 + rc)
            ca = _conv_block(pe_ref, wa_ref, r0, rc, l0, K_A, False)
            a_b = z_ref[rows, D_A + l0:D_A + l0 + LANES].astype(F32)
            dya = dy_ref[rows, l0:l0 + LANES].astype(F32)
            dab_ref[rows, l0:l0 + LANES] = (dya * ca).astype(BF16)
            dca = dya * a_b
            dca_ref[rows, l0:l0 + LANES] = dca
            _conv_wgrad_block(acc_ref, dca, pe_ref, r0, rc, l0, K_A)

        def mixer_b(dy_ref, r0):
            rows = slice(r0, r0 + rc)
            ubs = [u_ref[rows, l0:l0 + LANES] for l0 in range(0, D_B, LANES)]
            mu, rstd = _layernorm_rows(ubs)
            ns, dns = [], []
            m1 = None
            m2 = None
            for q, l0 in enumerate(range(0, D_B, LANES)):
                n = (ubs[q] - mu) * rstd
                lgq = lg_ref[:, l0:l0 + LANES]
                t = n * lgq + lb_ref[:, l0:l0 + LANES]
                sg = _sigmoid(t)
                dt = dy_ref[rows, D_A + l0:D_A + l0 + LANES].astype(F32) * (sg * (1.0 + t * (1.0 - sg)))
                dn = dt * lgq
                ns.append(n)
                dns.append(dn)
                s1 = jnp.sum(dn, axis=-1, keepdims=True)
                s2 = jnp.sum(dn * n, axis=-1, keepdims=True)
                m1 = s1 if m1 is None else m1 + s1
                m2 = s2 if m2 is None else m2 + s2
                sacc_ref[0, :, l0:l0 + LANES] = sacc_ref[0, :, l0:l0 + LANES] + _fold8(dt * n)
                sacc_ref[1, :, l0:l0 + LANES] = sacc_ref[1, :, l0:l0 + LANES] + _fold8(dt)
            m1 = m1 * (1.0 / D_B)
            m2 = m2 * (1.0 / D_B)
            for q, l0 in enumerate(range(0, D_B, LANES)):
                du = rstd * (dns[q] - m1 - ns[q] * m2)
                du_ref[rows, l0:l0 + LANES] = du
                sacc_ref[2, :, l0:l0 + LANES] = sacc_ref[2, :, l0:l0 + LANES] + _fold8(du)

        @pl.when(i % 2 == 0)
        def _():
            stage(dy0_ref, dy1_ref)

        @pl.when(i % 2 == 1)
        def _():
            stage(dy1_ref, dy0_ref)

        @pl.when(i == nt)
        def _():
            _reduce_acc(dwa_ref, acc_ref, K_A)
            dlg_ref[...] = jnp.sum(sacc_ref[0], axis=0, keepdims=True)
            dlb_ref[...] = jnp.sum(sacc_ref[1], axis=0, keepdims=True)
            dbb_ref[...] = jnp.sum(sacc_ref[2], axis=0, keepdims=True)

    cur = lambda i: jnp.minimum(i, nt - 1)
    old = lambda i: jnp.maximum(i - 1, 0)
    return _call(
        body, name="bwd_mix_a", grid=(nt + 1,),
        in_specs=[_rows_at(ts, D_MODEL, cur), _const((D_MODEL, D_MODEL)), _rows_at(ts, D_IN, old), _prev_at(ts, D_IN, old),
                  _next_at(ts, D_IN, s, old), _rows_at(ts, D_B, old), _const((K_A, D_A)), _const((1, D_B)), _const((1, D_B))],
        out_specs=[_rows_at(ts, D_A, old), _rows_at(ts, D_B, old), _rows_at(ts, D_A, old), _acc_out((K_A, D_A)),
                   _acc_out((1, D_B)), _acc_out((1, D_B)), _acc_out((1, D_B))],
        out_shape=[_sds((s, D_A), F32), _sds((s, D_B), F32), _sds((s, D_A), BF16), _sds((K_A, D_A), F32),
                   _sds((1, D_B), F32), _sds((1, D_B), F32), _sds((1, D_B), F32)],
        scratch_shapes=[_ext_scratch(ts, D_A), pltpu.VMEM((ts, D_MODEL), BF16), pltpu.VMEM((ts, D_MODEL), BF16),
                        pltpu.VMEM((K_A, SUBLANES, D_A), F32), pltpu.VMEM((3, SUBLANES, D_B), F32)],
        args=(dx2b, w_out_t, z, z, z, u, wa, lg, lb), exchange=exchange)


def _bwd_mix_b(dca, du, z, dab, wa, wb, w_in, x, g1, dx2, ts, exchange=None):
    s = x.shape[0]
    nt = s // ts
    rc = min(ROW_CHUNK, ts)
    n_sub = ts // rc

    def body(dca_ref, dcap_ref, dcan_ref, du_ref, dup_ref, dun_ref, z_ref, zp_ref, zn_ref, dab_ref,
             wa_ref, wb_ref, wi_ref, x_ref, g1_ref, dx2_ref,
             dz_ref, dx_ref, dg1_ref, dwb_ref, dcae_ref, due_ref, ue_ref, acc_ref, dg8_ref,
             dz0_ref, dz1_ref, dh0_ref, dh1_ref):
        i = pl.program_id(0)

        @pl.when(i == 0)
        def _():
            acc_ref[...] = jnp.zeros_like(acc_ref)
            dg8_ref[...] = jnp.zeros_like(dg8_ref)
            dz1_ref[...] = jnp.zeros_like(dz1_ref)
            dh1_ref[...] = jnp.zeros_like(dh1_ref)

        vt = jnp.minimum(i, nt - 1)
        first = vt == 0
        last = vt == nt - 1
        live = (i < nt).astype(F32)
        _fill_halo(dcae_ref, dcap_ref[...], dcan_ref[...], ts, first, last)
        _fill_halo(due_ref, dup_ref[...], dun_ref[...], ts, first, last)
        _, up = _p_u0(zp_ref, slice(None))
        _, un = _p_u0(zn_ref, slice(None))
        _fill_halo(ue_ref, up, un, ts, first, last)

        def fill(j, carry):
            r0 = pl.multiple_of(j * rc, rc)
            rows = pl.ds(r0, rc)
            _put_rows(dcae_ref, r0, rc, dca_ref[rows, :])
            _put_rows(due_ref, r0, rc, du_ref[rows, :])
            b_v = z_ref[rows, 3 * D_A:3 * D_A + D_B].astype(F32)
            b_g = z_ref[rows, 3 * D_A + D_B:D_IN].astype(F32)
            _put_rows(ue_ref, r0, rc, b_v * _sigmoid(b_g))
            return carry

        lax.fori_loop(0, n_sub, fill, 0)

        def stage(dz_new, dz_old, dh_new, dh_old):
            def put(rows, c0, val):
                dz_ref[rows, c0:c0 + LANES] = val
                dz_new[rows, c0:c0 + LANES] = val

            def mixer_a(r0, l0):
                rows = slice(r0, r0 + rc)
                dp = _conv_block(dcae_ref, wa_ref, r0, rc, l0, K_A, True)
                a_h = z_ref[rows, l0:l0 + LANES].astype(F32)
                a_c = z_ref[rows, 2 * D_A + l0:2 * D_A + l0 + LANES].astype(F32)
                put(rows, l0, (dp * a_c).astype(BF16))
                put(rows, D_A + l0, dab_ref[rows, l0:l0 + LANES])
                put(rows, 2 * D_A + l0, (dp * a_h).astype(BF16))

            def mixer_b(r0, l0):
                rows = slice(r0, r0 + rc)
                du0 = _conv_block(due_ref, wb_ref, r0, rc, l0, K_B, True)
                b_v = z_ref[rows, 3 * D_A + l0:3 * D_A + l0 + LANES].astype(F32)
                b_g = z_ref[rows, 3 * D_A + D_B + l0:3 * D_A + D_B + l0 + LANES].astype(F32)
                sg = _sigmoid(b_g)
                put(rows, 3 * D_A + l0, (du0 * sg).astype(BF16))
                put(rows, 3 * D_A + D_B + l0, (du0 * b_v * (sg * (1.0 - sg))).astype(BF16))
                _conv_wgrad_block(acc_ref, du_ref[rows, l0:l0 + LANES], ue_ref, r0, rc, l0, K_B, live)

            def tail(r0):
                rows = slice(r0, r0 + rc)
                dx, dgrow = _rms_bwd_rows(dh_old[rows, :], x_ref[rows, :], g1_ref[...])
                dx_ref[rows, :] = dx2_ref[rows, :] + dx
                dg8_ref[...] = dg8_ref[...] + _fold8(dgrow)

            units = []
            for q in range(n_sub):
                units += [(mixer_a, (q * rc, l0)) for l0 in range(0, D_A, LANES)]
                units += [(mixer_b, (q * rc, l0)) for l0 in range(0, D_B, LANES)]
                units.append((tail, (q * rc,)))
            _interleaved(units, _matmul_pieces([(dz_old, wi_ref)], dh_new, 2))

        @pl.when(i % 2 == 0)
        def _():
            stage(dz0_ref, dz1_ref, dh0_ref, dh1_ref)

        @pl.when(i % 2 == 1)
        def _():
            stage(dz1_ref, dz0_ref, dh1_ref, dh0_ref)

        @pl.when(i == nt + 1)
        def _():
            _reduce_acc(dwb_ref, acc_ref, K_B)
            dg1_ref[...] = jnp.sum(dg8_ref[...], axis=0, keepdims=True)

    vtile = lambda i: jnp.minimum(i, nt - 1)
    ttile = lambda i: jnp.clip(i - 2, 0, nt - 1)
    return _call(
        body, name="bwd_mix_b", grid=(nt + 2,),
        in_specs=[_rows_at(ts, D_A, vtile), _prev_at(ts, D_A, vtile), _next_at(ts, D_A, s, vtile),
                  _rows_at(ts, D_B, vtile), _prev_at(ts, D_B, vtile), _next_at(ts, D_B, s, vtile),
                  _rows_at(ts, D_IN, vtile), _prev_at(ts, D_IN, vtile), _next_at(ts, D_IN, s, vtile), _rows_at(ts, D_A, vtile),
                  _const((K_A, D_A)), _const((K_B, D_B)), _const((D_IN, D_MODEL)), _rows_at(ts, D_MODEL, ttile),
                  _const((1, D_MODEL)), _rows_at(ts, D_MODEL, ttile)],
        out_specs=[_rows_at(ts, D_IN, vtile), _rows_at(ts, D_MODEL, ttile), _acc_out((1, D_MODEL)), _acc_out((K_B, D_B))],
        out_shape=[_sds((s, D_IN), BF16), _sds((s, D_MODEL), F32), _sds((1, D_MODEL), F32), _sds((K_B, D_B), F32)],
        scratch_shapes=[_ext_scratch(ts, D_A), _ext_scratch(ts, D_B), _ext_scratch(ts, D_B),
                        pltpu.VMEM((K_B, SUBLANES, D_B), F32), pltpu.VMEM((SUBLANES, D_MODEL), F32),
                        pltpu.VMEM((ts, D_IN), BF16), pltpu.VMEM((ts, D_IN), BF16),
                        pltpu.VMEM((ts, D_MODEL), F32), pltpu.VMEM((ts, D_MODEL), F32)],
        args=(dca, dca, dca, du, du, du, z, z, z, dab, wa, wb, w_in, x, g1, dx2), exchange=exchange)


def _matmul_tn(a, b, name, exchange=None):
    s, m = a.shape
    n = b.shape[1]
    tk = min(1024, s)
    nk = s // tk
    tm = 256

    def body(a_ref, b_ref, o_ref, acc_ref):
        k = pl.program_id(0)

        @pl.when(k == 0)
        def _():
            acc_ref[...] = jnp.zeros_like(acc_ref)

        for m0 in range(0, m, tm):
            acc_ref[m0:m0 + tm, :] = acc_ref[m0:m0 + tm, :] + lax.dot_general(
                a_ref[:, m0:m0 + tm], b_ref[...], _TN, preferred_element_type=F32)

        @pl.when(k == nk - 1)
        def _():
            o_ref[...] = acc_ref[...].astype(BF16)

    (out,), got = _call(
        body, name=name, grid=(nk,),
        in_specs=[_rows(tk, m), _rows(tk, n)],
        out_specs=[_acc_out((m, n))],
        out_shape=[_sds((m, n), BF16)],
        scratch_shapes=[pltpu.VMEM((m, n), F32)], args=(a, b), exchange=exchange)
    return out if exchange is None else (out, got)


CHIP_RELS = ((1, 0, 0), (0, 1, 0), (1, 1, 0))
CORE_RELS = ((0, 0, 1),)
ALL_RELS = ((0, 0, 1), (0, 1, 0), (0, 1, 1), (1, 0, 0), (1, 0, 1), (1, 1, 0), (1, 1, 1))


def _chip_slot(dev):
    return 2 * dev[0] + dev[1]


def _dev_slot(dev):
    return 4 * dev[0] + 2 * dev[1] + dev[2]


def _me():
    return (lax.axis_index("x"), lax.axis_index("y"), lax.axis_index("c"))


def _peer(me, rel):
    return tuple((1 - me[a]) if rel[a] else me[a] for a in range(3))


_ANY = pl.BlockSpec(memory_space=pl.ANY)


class _Exchange:
    def __init__(self, inputs, out_shape, scratch, start, finish, forward=None):
        self.inputs, self.out_shape, self.scratch = list(inputs), list(out_shape), list(scratch)
        self.start, self.finish, self.forward = start, finish, forward


def _all_gather(payloads):
    n_p = len(payloads)
    n_k = 1 + 2 * len(CHIP_RELS)

    def copy(srcs, dsts, sems, p, k, block_dev, to, from_src):
        blk = dsts[p].at[_dev_slot(block_dev)]
        return pltpu.make_async_remote_copy(
            src_ref=srcs[p] if from_src else blk, dst_ref=blk,
            send_sem=sems[0].at[n_k * p + k], recv_sem=sems[1].at[n_k * p + k], device_id=to, device_id_type=MESH)

    def own_copy(srcs, dsts, sems, p):
        return pltpu.make_async_copy(srcs[p], dsts[p].at[_dev_slot(_me())], sems[2].at[p])

    def start(srcs, dsts, sems):
        me = _me()
        for p in range(n_p):
            own_copy(srcs, dsts, sems, p).start()
        for j, rel in enumerate(CHIP_RELS):
            for p in range(n_p):
                copy(srcs, dsts, sems, p, 1 + j, me, _peer(me, rel), True).start()
        for p in range(n_p):
            copy(srcs, dsts, sems, p, 0, me, _peer(me, CORE_RELS[0]), True).start()

    def forward(srcs, dsts, sems):
        me = _me()
        sibling = _peer(me, CORE_RELS[0])
        for j, rel in enumerate(CHIP_RELS):
            other = _peer(me, rel)
            for p in range(n_p):
                copy(srcs, dsts, sems, p, 1 + j, other, me, False).wait_recv()
                copy(srcs, dsts, sems, p, 4 + j, other, sibling, False).start()

    def finish(srcs, dsts, sems):
        me = _me()
        sibling = _peer(me, CORE_RELS[0])
        for p in range(n_p):
            copy(srcs, dsts, sems, p, 0, sibling, me, False).wait_recv()
        for j, rel in enumerate(CHIP_RELS):
            for p in range(n_p):
                copy(srcs, dsts, sems, p, 4 + j, _peer(sibling, rel), me, False).wait_recv()
        for p in range(n_p):
            own_copy(srcs, dsts, sems, p).wait()
            copy(srcs, dsts, sems, p, 0, me, sibling, True).wait_send()
            for j, rel in enumerate(CHIP_RELS):
                copy(srcs, dsts, sems, p, 1 + j, me, _peer(me, rel), True).wait_send()
                copy(srcs, dsts, sems, p, 4 + j, _peer(me, rel), sibling, False).wait_send()

    return _Exchange(
        payloads, [_sds((N_DEV,) + p.shape, p.dtype) for p in payloads],
        [pltpu.SemaphoreType.DMA((n_p * n_k,)), pltpu.SemaphoreType.DMA((n_p * n_k,)), pltpu.SemaphoreType.DMA((n_p,))],
        start, finish, forward)


def _gather_direct(payload):
    n_r = len(ALL_RELS)

    def copies(srcs, dsts, sems):
        me = _me()
        mine = dsts[0].at[_dev_slot(me)]
        own = pltpu.make_async_copy(srcs[0], mine, sems[2].at[0])
        remote = [pltpu.make_async_remote_copy(src_ref=srcs[0], dst_ref=mine, send_sem=sems[0].at[k], recv_sem=sems[1].at[k],
                                               device_id=_peer(me, rel), device_id_type=MESH)
                  for k, rel in enumerate(ALL_RELS)]
        return [own] + remote

    def start(srcs, dsts, sems):
        for cp in copies(srcs, dsts, sems):
            cp.start()

    def finish(srcs, dsts, sems):
        for cp in copies(srcs, dsts, sems):
            cp.wait()

    return _Exchange([payload], [_sds((N_DEV,) + payload.shape, payload.dtype)],
                     [pltpu.SemaphoreType.DMA((n_r,)), pltpu.SemaphoreType.DMA((n_r,)), pltpu.SemaphoreType.DMA((1,))],
                     start, finish)


def _scatter_exchange(payloads, rels, src_view, view_shapes):
    n_p = len(payloads)
    n_r = len(rels)

    def copies(srcs, dsts, sems):
        me = _me()
        out = []
        for k, rel in enumerate(rels):
            peer = _peer(me, rel)
            for p in range(n_p):
                out.append(pltpu.make_async_remote_copy(
                    src_ref=src_view(srcs[p], peer), dst_ref=dsts[p].at[k],
                    send_sem=sems[0].at[p * n_r + k], recv_sem=sems[1].at[p * n_r + k],
                    device_id=peer, device_id_type=MESH))
        return out

    def start(srcs, dsts, sems):
        for cp in copies(srcs, dsts, sems):
            cp.start()

    def finish(srcs, dsts, sems):
        for cp in copies(srcs, dsts, sems):
            cp.wait()

    return _Exchange(payloads, [_sds((n_r,) + vs, p.dtype) for vs, p in zip(view_shapes, payloads)],
                     [pltpu.SemaphoreType.DMA((n_p * n_r,)), pltpu.SemaphoreType.DMA((n_p * n_r,))], start, finish)


def _split_refs(refs, sizes):
    out, at = [], 0
    for n in sizes:
        out.append(refs[at:at + n])
        at += n
    return out


def _join(exchanges):
    n_in = [len(e.inputs) for e in exchanges]
    n_out = [len(e.out_shape) for e in exchanges]
    n_sc = [len(e.scratch) for e in exchanges]

    def phase(name):
        def run(ins, outs, scs):
            for e, i, o, s in zip(exchanges, _split_refs(ins, n_in), _split_refs(outs, n_out), _split_refs(scs, n_sc)):
                if getattr(e, name) is not None:
                    getattr(e, name)(i, o, s)
        return run

    return _Exchange([a for e in exchanges for a in e.inputs], [s for e in exchanges for s in e.out_shape],
                     [s for e in exchanges for s in e.scratch], phase("start"), phase("finish"),
                     phase("forward") if any(e.forward is not None for e in exchanges) else None)


def _run_exchanges(name, exchanges):
    n_in = [len(e.inputs) for e in exchanges]
    n_out = [len(e.out_shape) for e in exchanges]
    n_sc = [len(e.scratch) for e in exchanges]

    def body(*refs):
        ins, outs, scs = _split_refs(refs, [sum(n_in), sum(n_out), sum(n_sc)])
        parts = list(zip(exchanges, _split_refs(ins, n_in), _split_refs(outs, n_out), _split_refs(scs, n_sc)))
        for e, i, o, s in parts:
            e.start(i, o, s)
        for e, i, o, s in parts:
            if e.forward is not None:
                e.forward(i, o, s)
        for e, i, o, s in parts:
            e.finish(i, o, s)

    outs = pl.pallas_call(
        body, name=name, in_specs=[_ANY] * sum(n_in), out_specs=[_ANY] * sum(n_out),
        out_shape=[sd for e in exchanges for sd in e.out_shape],
        scratch_shapes=[sc for e in exchanges for sc in e.scratch],
    )(*[a for e in exchanges for a in e.inputs])
    return _split_refs(list(outs), n_out)


def _call(body, *, name, grid, in_specs, out_specs, out_shape, scratch_shapes, args, exchange=None, forward_step=None):
    n_in, n_out, n_sc = len(in_specs), len(out_specs), len(scratch_shapes)
    if exchange is None:
        outs = pl.pallas_call(body, name=name, grid=grid, in_specs=in_specs, out_specs=out_specs, out_shape=out_shape,
                              scratch_shapes=scratch_shapes, compiler_params=_params())(*args)
        return list(outs), []
    e = exchange
    sizes = [n_in, len(e.inputs), n_out, len(e.out_shape), n_sc, len(e.scratch)]
    last = grid[0] - 1

    def wrapped(*refs):
        a, ei, o, eo, sc, es = _split_refs(refs, sizes)
        i = pl.program_id(0)

        @pl.when(i == 0)
        def _():
            e.start(ei, eo, es)

        if e.forward is not None:
            @pl.when(i == forward_step)
            def _():
                e.forward(ei, eo, es)

        body(*a, *o, *sc)

        @pl.when(i == last)
        def _():
            e.finish(ei, eo, es)

    outs = pl.pallas_call(
        wrapped, name=name, grid=grid,
        in_specs=list(in_specs) + [_ANY] * len(e.inputs), out_specs=list(out_specs) + [_ANY] * len(e.out_shape),
        out_shape=list(out_shape) + e.out_shape, scratch_shapes=list(scratch_shapes) + e.scratch,
        compiler_params=_params(),
    )(*args, *e.inputs)
    outs = list(outs)
    return outs[:n_out], outs[n_out:]


def _pair_sum(grads, recvd, my_core, name):
    n_p = len(grads)

    def body(c_ref, *refs):
        del c_ref
        for p in range(n_p):
            refs[2 * n_p + p][...] = (refs[p][...].astype(F32) + refs[n_p + p][...].astype(F32)).astype(BF16)

    def blk(g):
        return (None, None) + g.shape[2:]

    return pl.pallas_call(
        body, name=name,
        grid_spec=pltpu.PrefetchScalarGridSpec(
            num_scalar_prefetch=1, grid=(N_CHIP,),
            in_specs=[pl.BlockSpec(blk(g), lambda j, c: (j, c[0], 0, 0)) for g in grads]
            + [pl.BlockSpec(blk(g), lambda j, c: (0, j, 0, 0)) for g in grads],
            out_specs=[pl.BlockSpec((None,) + g.shape[2:], lambda j, c: (j, 0, 0)) for g in grads]),
        out_shape=[_sds((N_CHIP,) + g.shape[2:], BF16) for g in grads],
        compiler_params=pltpu.CompilerParams(dimension_semantics=("arbitrary",), vmem_limit_bytes=VMEM_LIMIT),
    )(my_core, *grads, *recvd)


def _chip_sum(psums, recvd, my_chip, name):
    n_p = len(psums)

    def body(c_ref, *refs):
        del c_ref
        for p in range(n_p):
            acc = refs[p][...].astype(F32)
            for k in range(len(CHIP_RELS)):
                acc = acc + refs[n_p + p][k].astype(F32)
            refs[2 * n_p + p][...] = acc

    return pl.pallas_call(
        body, name=name,
        grid_spec=pltpu.PrefetchScalarGridSpec(
            num_scalar_prefetch=1, grid=(1,),
            in_specs=[pl.BlockSpec((None,) + g.shape[1:], lambda i, c: (c[0], 0, 0)) for g in psums]
            + [pl.BlockSpec(r.shape, lambda i, c: (0, 0, 0)) for r in recvd],
            out_specs=[pl.BlockSpec(g.shape[1:], lambda i, c: (0, 0)) for g in psums]),
        out_shape=[_sds(g.shape[1:], F32) for g in psums],
        compiler_params=pltpu.CompilerParams(dimension_semantics=("arbitrary",), vmem_limit_bytes=VMEM_LIMIT),
    )(my_chip, *psums, *recvd)


def _sum_devices(parts, shapes):
    plan = _small_rows(shapes)

    def body(p_ref, *refs):
        outs, tot_ref = refs[:-1], refs[-1]
        acc = p_ref[0]
        for j in range(1, N_DEV):
            acc = acc + p_ref[j]
        tot_ref[...] = acc
        for idx, r, c0, width, at in plan:
            outs[idx][r:r + 1, c0:c0 + width] = tot_ref[at:at + 1, 0:width]

    return pl.pallas_call(body, name="small_grad_sum", out_shape=[_sds(s, F32) for s in shapes],
                          scratch_shapes=[pltpu.VMEM(parts.shape[1:], F32)])(parts)


def _cast_shards(shards):
    def body(*refs):
        for src, dst in zip(refs[:len(shards)], refs[len(shards):]):
            dst[...] = src[...].astype(BF16)

    return pl.pallas_call(body, name="cast_shards", out_shape=[_sds(a.shape, BF16) for a in shards],
                          compiler_params=pltpu.CompilerParams(vmem_limit_bytes=VMEM_LIMIT))(*shards)


def _adamw(ws, gs, ms, vs, name):
    n_t = len(ws)

    def body(*refs):
        w_refs, g_refs, m_refs, v_refs = (refs[j * n_t:(j + 1) * n_t] for j in range(4))
        outs = refs[4 * n_t:]
        for k in range(n_t):
            gg = g_refs[k][...]
            mn = ADAM_B1 * m_refs[k][...] + (1.0 - ADAM_B1) * gg
            vn = ADAM_B2 * v_refs[k][...] + (1.0 - ADAM_B2) * (gg * gg)
            m_hat = mn / (1.0 - ADAM_B1 ** ADAM_STEP)
            v_hat = vn / (1.0 - ADAM_B2 ** ADAM_STEP)
            outs[3 * k][...] = -ADAM_LR * (m_hat / (jnp.sqrt(v_hat) + ADAM_EPS) + ADAM_WD * w_refs[k][...])
            outs[3 * k + 1][...] = mn
            outs[3 * k + 2][...] = vn

    out_shape = [_sds(w.shape, F32) for w in ws for _ in range(3)]
    return pl.pallas_call(body, name=name, out_shape=out_shape,
                          compiler_params=pltpu.CompilerParams(vmem_limit_bytes=VMEM_LIMIT))(*ws, *gs, *ms, *vs)


class _Mesh:
    def __init__(self, shards, my_chip, my_core):
        self.shards, self.my_chip, self.my_core = shards, my_chip.reshape(1), my_core.reshape(1)

    def gather(self, names):
        return _all_gather([self.shards[n] for n in names])

    @staticmethod
    def whole(gathered):
        return gathered.reshape(N_DEV * gathered.shape[1], gathered.shape[2])

    @staticmethod
    def by_device(grads):
        return [g.reshape(N_CHIP, 2, g.shape[0] // N_DEV, g.shape[1]) for g in grads]

    @staticmethod
    def to_sibling(parts):
        return _scatter_exchange(parts, CORE_RELS, lambda ref, peer: ref.at[:, peer[2]],
                                 [(N_CHIP,) + p.shape[2:] for p in parts])

    @staticmethod
    def to_chips(pair):
        return _scatter_exchange(pair, CHIP_RELS, lambda ref, peer: ref.at[_chip_slot(peer)], [p.shape[1:] for p in pair])


def _step(x, target, g1, w_in_t, wa, wb, bb, lg, lb, w_out, g2, w_gate_t, w_up_t, wf, w_down, g3, ts, mesh=None):
    (z, h1), got = _fwd_in(x, g1, w_in_t, ts, exchange=mesh and mesh.gather(["w_out", "w_down"]))
    if mesh:
        w_out, w_down = [mesh.whole(g) for g in got]
    (x2, y, u), got = _fwd_mix(z, x, wa, wb, bb, lg, lb, w_out, ts, exchange=mesh and mesh.gather(["w_gate", "w_up"]))
    if mesh:
        w_gate_t, w_up_t = [mesh.whole(g) for g in got]
    (g0, v, h2), _ = _fwd_ffn_in(x2, g2, w_gate_t, w_up_t, ts)
    tk = min(ts, SKEW_TILE)
    a, dx3, dx3b, loss, dg3, dgc, dv, dwf = _ffn_out_and_back(g0, v, x2, wf, w_down, g3, target, tk)
    one = dict(w_down=_matmul_tn(a, dx3b, "wgrad_down"), w_up=_matmul_tn(dv, h2, "wgrad_up"))
    parts1 = mesh and mesh.by_device(list(one.values()))
    (dg0, dx2, dx2b, dg2), got = _bwd_ffn_b(dgc, dv, wf, w_gate_t, w_up_t, x2, g2, dx3, tk,
                                            exchange=mesh and mesh.to_sibling(parts1))
    pair1 = mesh and _pair_sum(parts1, got, mesh.my_core, "rs_pair_sum_1")
    two = dict(w_gate=_matmul_tn(dg0, h2, "wgrad_gate"), w_out=_matmul_tn(y, dx2b, "wgrad_out"))
    parts2 = mesh and mesh.by_device(list(two.values()))
    (dca, du, dab, dwa, dlg, dlb, dbb), got = _bwd_mix_a(
        dx2b, jnp.swapaxes(w_out, 0, 1), z, u, wa, lg, lb, tk,
        exchange=mesh and _join([mesh.to_chips(pair1), mesh.to_sibling(parts2)]))
    (dz, dx, dg1, dwb), _ = _bwd_mix_b(dca, du, z, dab, wa, wb, w_in_t, x, g1, dx2, tk)
    small = dict(norm_mix_g=dg1, conv_a_w=dwa, conv_b_w=dwb, conv_b_b=dbb, ln_b_g=dlg, ln_b_b=dlb,
                 norm_ffn_g=dg2, conv_ffn_w=dwf, norm_final_g=dg3)
    if not mesh:
        return loss, dx, dict(w_in=_matmul_tn(dz, h1, "wgrad_in"), **one, **two), small
    big = dict(zip(one, _chip_sum(pair1, got[:2], mesh.my_chip, "rs_chip_sum_1")))
    pair2 = _pair_sum(parts2, got[2:], mesh.my_core, "rs_pair_sum_2")
    dw_in_t, got = _matmul_tn(dz, h1, "wgrad_in",
                              exchange=_join([mesh.to_chips(pair2), _gather_direct(_pack_small_grads(small, loss))]))
    big.update(zip(two, _chip_sum(pair2, got[:2], mesh.my_chip, "rs_chip_sum_2")))
    every = got[2]
    parts = mesh.by_device([dw_in_t])
    (got,) = _run_exchanges("rs_cores_last", [mesh.to_sibling(parts)])
    pair = _pair_sum(parts, got, mesh.my_core, "rs_pair_sum_last")
    (got,) = _run_exchanges("rs_chips_last", [mesh.to_chips(pair)])
    (big["w_in"],) = _chip_sum(pair, got, mesh.my_chip, "rs_chip_sum_last")
    shapes = [loss.shape if n == "loss" else small[n].shape for n in _SMALL_NAMES]
    return None, dx, big, dict(zip(_SMALL_NAMES, _sum_devices(every, shapes)))


def _pack_small_weights(conv_a_s, conv_b_s, conv_ffn_s):
    def body(a_ref, b_ref, f_ref, out):
        out[...] = jnp.zeros_like(out)
        at = 0
        for src in (a_ref, b_ref, f_ref):
            rows, cols = src.shape
            for r in range(rows):
                out[at:at + 1, 0:cols] = src[r:r + 1, :]
                at += 1

    return pl.pallas_call(body, name="small_weight_pack", out_shape=_sds((SMALL_W_ROWS, SMALL_W_COLS), F32))(
        conv_a_s, conv_b_s, conv_ffn_s)


def _unpack_small_weights(full):
    def take(r0, k, w):
        return jnp.transpose(full[:, r0:r0 + k, 0:w], (1, 0, 2)).reshape(k, N_DEV * w)

    return take(0, K_A, CONV_A_COLS), take(K_A, K_B, CONV_A_COLS), take(K_A + K_B, K_F, W_FF_COLS)


_SMALL_NAMES = ("conv_b_w", "conv_a_w", "conv_ffn_w", "norm_mix_g", "norm_ffn_g", "norm_final_g",
                "conv_b_b", "ln_b_g", "ln_b_b", "loss")
SMALL_G_ROWS = 64


def _small_rows(shapes):
    plan, at = [], 0
    for idx, (rows, cols) in enumerate(shapes):
        for r in range(rows):
            for c0 in range(0, cols, SMALL_G_COLS):
                plan.append((idx, r, c0, min(SMALL_G_COLS, cols - c0), at))
                at += 1
    assert at <= SMALL_G_ROWS
    return plan


def _pack_small_grads(small, loss):
    srcs = [loss if n == "loss" else small[n] for n in _SMALL_NAMES]
    plan = _small_rows([a.shape for a in srcs])

    def body(*refs):
        out = refs[-1]
        out[...] = jnp.zeros_like(out)
        for idx, r, c0, width, at in plan:
            out[at:at + 1, 0:width] = refs[idx][r:r + 1, c0:c0 + width]

    return pl.pallas_call(body, name="small_grad_pack", out_shape=_sds((SMALL_G_ROWS, SMALL_G_COLS), F32))(*srcs)


def kernel(x, norm_mix_g, w_in, conv_a_w, conv_b_w, conv_b_b, ln_b_g, ln_b_b, w_out, norm_ffn_g, w_gate, w_up, conv_ffn_w, w_down, norm_final_g, loss_target, m_norm_mix_g, m_w_in, m_conv_a_w, m_conv_b_w, m_conv_b_b, m_ln_b_g, m_ln_b_b, m_w_out, m_norm_ffn_g, m_w_gate, m_w_up, m_conv_ffn_w, m_w_down, m_norm_final_g, v_norm_mix_g, v_w_in, v_conv_a_w, v_conv_b_w, v_conv_b_b, v_ln_b_g, v_ln_b_b, v_w_out, v_norm_ffn_g, v_w_gate, v_w_up, v_conv_ffn_w, v_w_down, v_norm_final_g):
    ix, iy, ic = lax.axis_index("x"), lax.axis_index("y"), lax.axis_index("c")
    my_chip = (2 * ix + iy).astype(jnp.int32)
    my_core = ic.astype(jnp.int32)
    my_dev = 2 * my_chip + my_core

    weights = dict(norm_mix_g=norm_mix_g, w_in=w_in, conv_a_w=conv_a_w, conv_b_w=conv_b_w, conv_b_b=conv_b_b,
                   ln_b_g=ln_b_g, ln_b_b=ln_b_b, w_out=w_out, norm_ffn_g=norm_ffn_g, w_gate=w_gate, w_up=w_up,
                   conv_ffn_w=conv_ffn_w, w_down=w_down, norm_final_g=norm_final_g)
    m_in = dict(norm_mix_g=m_norm_mix_g, w_in=m_w_in, conv_a_w=m_conv_a_w, conv_b_w=m_conv_b_w, conv_b_b=m_conv_b_b,
                ln_b_g=m_ln_b_g, ln_b_b=m_ln_b_b, w_out=m_w_out, norm_ffn_g=m_norm_ffn_g, w_gate=m_w_gate,
                w_up=m_w_up, conv_ffn_w=m_conv_ffn_w, w_down=m_w_down, norm_final_g=m_norm_final_g)
    v_in = dict(norm_mix_g=v_norm_mix_g, w_in=v_w_in, conv_a_w=v_conv_a_w, conv_b_w=v_conv_b_w, conv_b_b=v_conv_b_b,
                ln_b_g=v_ln_b_g, ln_b_b=v_ln_b_b, w_out=v_w_out, norm_ffn_g=v_norm_ffn_g, w_gate=v_w_gate,
                w_up=v_w_up, conv_ffn_w=v_conv_ffn_w, w_down=v_w_down, norm_final_g=v_norm_final_g)
    order = list(weights)
    big_names = ("w_in", "w_gate", "w_up", "w_out", "w_down")
    transposed = ("w_in", "w_gate", "w_up")

    def shard2d(name, a):
        if name in transposed:
            return jnp.swapaxes(a[0], 0, 1)
        return a.reshape(1, a.shape[0]) if a.ndim == 1 else a.reshape(a.shape[-2:])

    def unshard2d(name, a2, like):
        if name in transposed:
            return jnp.swapaxes(a2, 0, 1)[None]
        return a2.reshape(like.shape)

    mesh = _Mesh(dict(zip(big_names, _cast_shards([shard2d(n, weights[n]) for n in big_names]))), my_chip, my_core)
    gathered, = _run_exchanges("ag_first", [_all_gather(
        [mesh.shards["w_in"], _pack_small_weights(conv_a_w[0], conv_b_w[0], conv_ffn_w[0])])])
    w_in_t = mesh.whole(gathered[0])
    wa_f, wb_f, wf_f = _unpack_small_weights(gathered[1])

    _, dx, gsum, stot = _step(
        x[0], loss_target[0], norm_mix_g, w_in_t, wa_f, wb_f, conv_b_b, ln_b_g, ln_b_b, None, norm_ffn_g,
        None, None, wf_f, None, norm_final_g.reshape(1, D_MODEL), SEQ_TILE, mesh)

    grads2d = dict(
        norm_mix_g=stot["norm_mix_g"],
        conv_a_w=lax.dynamic_slice(stot["conv_a_w"], (0, my_dev * CONV_A_COLS), (K_A, CONV_A_COLS)),
        conv_b_w=lax.dynamic_slice(stot["conv_b_w"], (0, my_dev * CONV_A_COLS), (K_B, CONV_A_COLS)),
        conv_b_b=stot["conv_b_b"], ln_b_g=stot["ln_b_g"], ln_b_b=stot["ln_b_b"],
        norm_ffn_g=stot["norm_ffn_g"],
        conv_ffn_w=lax.dynamic_slice(stot["conv_ffn_w"], (0, my_dev * W_FF_COLS), (K_F, W_FF_COLS)),
        norm_final_g=stot["norm_final_g"],
        **gsum,
    )

    updates = {}
    small_names = [n for n in order if n not in big_names]
    for group, label in [([n], "adamw_" + n) for n in big_names] + [(small_names, "adamw_small")]:
        outs = _adamw([shard2d(n, weights[n]) for n in group], [grads2d[n] for n in group],
                      [shard2d(n, m_in[n]) for n in group], [shard2d(n, v_in[n]) for n in group], label)
        for k, n in enumerate(group):
            updates[n] = outs[3 * k:3 * k + 3]
    g_out = [unshard2d(n, grads2d[n], weights[n]) for n in order]
    d_out, m_out, v_out = [[unshard2d(n, updates[n][j], weights[n]) for n in order] for j in range(3)]

    return (stot["loss"][0, 0], dx[None], *g_out, *d_out, *m_out, *v_out)
```

```python
import jax
import jax.numpy as jnp
from jax import lax
from jax.experimental import pallas as pl
from jax.experimental.pallas import tpu as pltpu

F32 = jnp.float32
BF16 = jnp.bfloat16

D_MODEL = 1024
D_A = 512
D_B = 512
D_IN = 3 * D_A + 2 * D_B
D_FF = 2816
K_A = 3
K_B = 31
K_F = 3
RMS_EPS = 1e-6
LN_EPS = 1e-5

ADAM_LR = 0.001
ADAM_B1 = 0.9
ADAM_B2 = 0.999
ADAM_EPS = 1e-08
ADAM_WD = 0.01
ADAM_STEP = 10

N_DEV = 8
N_CHIP = 4
LANES = 128
SUBLANES = 8
HALO = 16
ROW_CHUNK = 64
SEQ_TILE = 512
SKEW_TILE = 256
VMEM_LIMIT = 56 * 1024 * 1024

MESH = pl.DeviceIdType.MESH

W_FF_COLS = D_FF // N_DEV
CONV_A_COLS = D_A // N_DEV
SMALL_W_ROWS = 40
SMALL_W_COLS = 384
SMALL_G_COLS = 512
FF_PAD = 3072


def _rows(ts, c):
    return pl.BlockSpec((ts, c), lambda i: (i, 0))


def _const(shape):
    return pl.BlockSpec(shape, lambda i: (0,) * len(shape), pipeline_mode=pl.Buffered(1))


def _acc_out(shape):
    return pl.BlockSpec(shape, lambda i: (0,) * len(shape))


def _rows_at(ts, c, tile):
    return pl.BlockSpec((ts, c), lambda i: (tile(i), 0))


def _prev_at(ts, c, tile):
    return pl.BlockSpec((HALO, c), lambda i: (jnp.maximum(tile(i) * (ts // HALO) - 1, 0), 0))


def _next_at(ts, c, s, tile):
    last = s // HALO - 1
    return pl.BlockSpec((HALO, c), lambda i: (jnp.minimum((tile(i) + 1) * (ts // HALO), last), 0))


def _prev(ts, c):
    return _prev_at(ts, c, lambda i: i)


def _next(ts, c, s):
    return _next_at(ts, c, s, lambda i: i)


MXU_COLS = 256
MXU_ROWS = 256


def _col_pieces(n):
    return [(c0, min(MXU_COLS, n - c0)) for c0 in range(0, n, MXU_COLS)]


def _matmul_pieces(terms, out_ref, k_parts, w_transposed=False):
    m, n = out_ref.shape
    rows = min(MXU_ROWS, m)
    steps = []
    for lhs_ref, w_ref in terms:
        tiles = lhs_ref.shape[1] // MXU_COLS
        cuts = [MXU_COLS * (tiles * j // k_parts) for j in range(k_parts)] + [lhs_ref.shape[1]]
        steps += [(lhs_ref, w_ref, cuts[j], cuts[j + 1]) for j in range(k_parts)]

    def piece(m0, n0, width, step):
        lhs_ref, w_ref, k0, k1 = steps[step]
        if w_transposed:
            part = lax.dot_general(lhs_ref[m0:m0 + rows, k0:k1], w_ref[n0:n0 + width, k0:k1], _NT,
                                   preferred_element_type=F32)
        else:
            part = jnp.dot(lhs_ref[m0:m0 + rows, k0:k1], w_ref[k0:k1, n0:n0 + width], preferred_element_type=F32)
        if step:
            part = part + out_ref[m0:m0 + rows, n0:n0 + width]
        out_ref[m0:m0 + rows, n0:n0 + width] = part

    return [(piece, (m0, n0, w, j)) for j in range(len(steps)) for n0, w in _col_pieces(n) for m0 in range(0, m, rows)]


def _interleaved(vector_units, matmul_pieces):
    n_u, n_p = len(vector_units), len(matmul_pieces)
    done = 0
    for k, (unit, args) in enumerate(vector_units):
        while done < n_p and done * n_u <= k * n_p:
            matmul_pieces[done][0](*matmul_pieces[done][1])
            done += 1
        unit(*args)
    for fn, args in matmul_pieces[done:]:
        fn(*args)


def _params():
    return pltpu.CompilerParams(dimension_semantics=("arbitrary",), vmem_limit_bytes=VMEM_LIMIT)


def _sds(shape, dtype):
    return jax.ShapeDtypeStruct(shape, dtype)


def _sigmoid(v):
    return 0.5 * jnp.tanh(0.5 * v) + 0.5


def _conv_block(ext_ref, w_ref, r0, rc, l0, k_taps, transposed):
    acc = None
    for k in range(k_taps):
        d = (k_taps // 2 - k) if transposed else (k - k_taps // 2)
        term = ext_ref[l0 // LANES, pl.ds(r0 + HALO + d, rc), :] * w_ref[k:k + 1, l0:l0 + LANES]
        acc = term if acc is None else acc + term
    return acc


def _conv_wgrad_block(acc_ref, dout, ext_ref, r0, rc, l0, k_taps, scale=None):
    for k in range(k_taps):
        prod = dout * ext_ref[l0 // LANES, pl.ds(r0 + HALO + k - k_taps // 2, rc), :]
        part = prod.reshape(rc // SUBLANES, SUBLANES, LANES).sum(axis=0)
        if scale is not None:
            part = part * scale
        acc_ref[k, :, l0:l0 + LANES] = acc_ref[k, :, l0:l0 + LANES] + part


def _reduce_acc(out_ref, acc_ref, k_taps):
    for k in range(k_taps):
        out_ref[k:k + 1, :] = jnp.sum(acc_ref[k], axis=0, keepdims=True)


def _fold8(v):
    rc, c = v.shape
    return v.reshape(rc // SUBLANES, SUBLANES, c).sum(axis=0)


def _ext_scratch(ts, c):
    return pltpu.VMEM((c // LANES, ts + 2 * HALO, LANES), F32)


def _put_rows(ext_ref, r0, rc, val):
    for q in range(val.shape[1] // LANES):
        ext_ref[q, pl.ds(r0 + HALO, rc), :] = val[:, q * LANES:(q + 1) * LANES]


def _fill_halo(ext_ref, vals_prev, vals_next, ts, first, last):
    for q in range(vals_prev.shape[1] // LANES):
        cols = slice(q * LANES, (q + 1) * LANES)
        ext_ref[q, 0:HALO, :] = jnp.where(first, 0.0, vals_prev[:, cols])
        ext_ref[q, HALO + ts:HALO + ts + HALO, :] = jnp.where(last, 0.0, vals_next[:, cols])


def _rms_bwd_rows(dh, xf, g):
    r = lax.rsqrt(jnp.mean(xf * xf, axis=-1, keepdims=True) + RMS_EPS)
    xhat = xf * r
    dxh = dh * g
    dx = r * (dxh - xhat * jnp.mean(dxh * xhat, axis=-1, keepdims=True))
    return dx, dh * xhat


_NT = (((1,), (1,)), ((), ()))
_TN = (((0,), (0,)), ((), ()))


def _fwd_in(x, g1, w_in_t, ts, exchange=None):
    s = x.shape[0]

    def body(x_ref, g_ref, w_ref, z_ref, h_ref):
        xf = x_ref[...]
        r = lax.rsqrt(jnp.mean(xf * xf, axis=-1, keepdims=True) + RMS_EPS)
        h = (xf * r * g_ref[...]).astype(BF16)
        h_ref[...] = h
        for n0 in range(0, D_IN, 512):
            z_ref[:, n0:n0 + 512] = lax.dot_general(h, w_ref[n0:n0 + 512, :], _NT,
                                                    preferred_element_type=F32).astype(BF16)

    return _call(
        body, name="fwd_in", grid=(s // ts,),
        in_specs=[_rows(ts, D_MODEL), _const((1, D_MODEL)), _const((D_IN, D_MODEL))],
        out_specs=[_rows(ts, D_IN), _rows(ts, D_MODEL)],
        out_shape=[_sds((s, D_IN), BF16), _sds((s, D_MODEL), BF16)],
        scratch_shapes=[], args=(x, g1, w_in_t), exchange=exchange, forward_step=(s // ts) * 3 // 4)


def _p_u0(z_ref, rows):
    a_h = z_ref[rows, 0:D_A].astype(F32)
    a_c = z_ref[rows, 2 * D_A:3 * D_A].astype(F32)
    b_v = z_ref[rows, 3 * D_A:3 * D_A + D_B].astype(F32)
    b_g = z_ref[rows, 3 * D_A + D_B:D_IN].astype(F32)
    return a_c * a_h, b_v * _sigmoid(b_g)


def _layernorm_rows(u_blocks):
    tot = None
    for ub in u_blocks:
        sm = jnp.sum(ub, axis=-1, keepdims=True)
        tot = sm if tot is None else tot + sm
    mu = tot * (1.0 / D_B)
    var = None
    for ub in u_blocks:
        sq = jnp.sum((ub - mu) * (ub - mu), axis=-1, keepdims=True)
        var = sq if var is None else var + sq
    rstd = lax.rsqrt(var * (1.0 / D_B) + LN_EPS)
    return mu, rstd


def _fwd_mix(z, x, wa, wb, bb, lg, lb, w_out, ts, exchange=None):
    s = x.shape[0]
    nt = s // ts
    rc = min(ROW_CHUNK, ts)

    def body(z_ref, zp_ref, zn_ref, x_ref, wa_ref, wb_ref, bb_ref, lg_ref, lb_ref, wo_ref,
             x2_ref, y_ref, u_ref, pe_ref, ue_ref):
        i = pl.program_id(0)
        pp, up = _p_u0(zp_ref, slice(None))
        pn, un = _p_u0(zn_ref, slice(None))
        _fill_halo(pe_ref, pp, pn, ts, i == 0, i == nt - 1)
        _fill_halo(ue_ref, up, un, ts, i == 0, i == nt - 1)

        def fill(j, carry):
            r0 = pl.multiple_of(j * rc, rc)
            p, u0 = _p_u0(z_ref, pl.ds(r0, rc))
            _put_rows(pe_ref, r0, rc, p)
            _put_rows(ue_ref, r0, rc, u0)
            return carry

        lax.fori_loop(0, ts // rc, fill, 0)

        def mixer_a(r0, l0):
            rows = pl.ds(r0, rc)
            ca = _conv_block(pe_ref, wa_ref, r0, rc, l0, K_A, False)
            a_b = z_ref[rows, D_A + l0:D_A + l0 + LANES].astype(F32)
            y_ref[rows, l0:l0 + LANES] = (a_b * ca).astype(BF16)

        def conv_b(r0, l0):
            u_ref[pl.ds(r0, rc), l0:l0 + LANES] = (_conv_block(ue_ref, wb_ref, r0, rc, l0, K_B, False)
                                                   + bb_ref[:, l0:l0 + LANES])

        def norm_b(r0):
            rows = pl.ds(r0, rc)
            ubs = [u_ref[rows, l0:l0 + LANES] for l0 in range(0, D_B, LANES)]
            mu, rstd = _layernorm_rows(ubs)
            for q, l0 in enumerate(range(0, D_B, LANES)):
                t = (ubs[q] - mu) * rstd * lg_ref[:, l0:l0 + LANES] + lb_ref[:, l0:l0 + LANES]
                y_ref[rows, D_A + l0:D_A + l0 + LANES] = (t * _sigmoid(t)).astype(BF16)

        def chunk_units(r0):
            return ([(mixer_a, (r0, l0)) for l0 in range(0, D_A, LANES)]
                    + [(conv_b, (r0, l0)) for l0 in range(0, D_B, LANES)] + [(norm_b, (r0,))])

        def main(j, carry):
            for fn, args in chunk_units(pl.multiple_of(j * rc, rc)):
                fn(*args)
            return carry

        def project(m0, n0, width):
            rows = slice(m0, m0 + hrows)
            x2_ref[rows, n0:n0 + width] = x_ref[rows, n0:n0 + width] + jnp.dot(
                y_ref[rows, :], wo_ref[:, n0:n0 + width], preferred_element_type=F32)

        hrows = ts // 2
        lax.fori_loop(0, hrows // rc, main, 0, unroll=4)
        _interleaved([u for q in range(hrows // rc) for u in chunk_units(hrows + q * rc)],
                     [(project, (0, n0, w)) for n0, w in _col_pieces(D_MODEL)])
        for n0, w in _col_pieces(D_MODEL):
            project(hrows, n0, w)

    return _call(
        body, name="fwd_mix", grid=(nt,),
        in_specs=[_rows(ts, D_IN), _prev(ts, D_IN), _next(ts, D_IN, s), _rows(ts, D_MODEL),
                  _const((K_A, D_A)), _const((K_B, D_B)), _const((1, D_B)), _const((1, D_B)), _const((1, D_B)),
                  _const((D_MODEL, D_MODEL))],
        out_specs=[_rows(ts, D_MODEL), _rows(ts, D_MODEL), _rows(ts, D_B)],
        out_shape=[_sds((s, D_MODEL), F32), _sds((s, D_MODEL), BF16), _sds((s, D_B), F32)],
        scratch_shapes=[_ext_scratch(ts, D_A), _ext_scratch(ts, D_B)],
        args=(z, z, z, x, wa, wb, bb, lg, lb, w_out), exchange=exchange, forward_step=nt * 5 // 8)


def _fwd_ffn_in(x2, g2, w_gate_t, w_up_t, ts, exchange=None):
    s = x2.shape[0]
    half = D_FF // 2

    def body(x_ref, g_ref, wg_ref, wu_ref, g0_ref, v_ref, h_ref):
        xf = x_ref[...]
        r = lax.rsqrt(jnp.mean(xf * xf, axis=-1, keepdims=True) + RMS_EPS)
        h = (xf * r * g_ref[...]).astype(BF16)
        h_ref[...] = h
        for n0 in range(0, D_FF, half):
            g0_ref[:, n0:n0 + half] = lax.dot_general(h, wg_ref[n0:n0 + half, :], _NT,
                                                      preferred_element_type=F32).astype(BF16)
            v_ref[:, n0:n0 + half] = lax.dot_general(h, wu_ref[n0:n0 + half, :], _NT,
                                                     preferred_element_type=F32).astype(BF16)

    return _call(
        body, name="fwd_ffn_in", grid=(s // ts,),
        in_specs=[_rows(ts, D_MODEL), _const((1, D_MODEL)), _const((D_FF, D_MODEL)), _const((D_FF, D_MODEL))],
        out_specs=[_rows(ts, D_FF), _rows(ts, D_FF), _rows(ts, D_MODEL)],
        out_shape=[_sds((s, D_FF), BF16), _sds((s, D_FF), BF16), _sds((s, D_MODEL), BF16)],
        scratch_shapes=[], args=(x2, g2, w_gate_t, w_up_t), exchange=exchange, forward_step=(s // ts) // 2)


def _ffn_out_and_back(g0, v, x2, wf, w_down, g3, target, ts):
    s = x2.shape[0]
    nt = s // ts
    rc = min(ROW_CHUNK, ts // 2)

    def body(g0_ref, gp_ref, gn_ref, v_ref, x2_ref, wf_ref, wd_ref, wdt_ref, g3_ref, t_ref,
             a_ref, dx3_ref, dx3b_ref, loss_ref, dg3_ref, dg_ref, dv_ref, dwf_ref,
             ge_ref, silu_ref, dsv_ref, da_ref, acc_ref, p_ref, sums_ref):
        i = pl.program_id(0)

        @pl.when(i == 0)
        def _():
            acc_ref[...] = jnp.zeros_like(acc_ref)
            sums_ref[...] = jnp.zeros_like(sums_ref)

        _fill_halo(ge_ref, gp_ref[...].astype(F32), gn_ref[...].astype(F32), ts, i == 0, i == nt - 1)

        def fill(j, carry):
            r0 = pl.multiple_of(j * rc, rc)
            _put_rows(ge_ref, r0, rc, g0_ref[pl.ds(r0, rc), :].astype(F32))
            return carry

        lax.fori_loop(0, ts // rc, fill, 0)

        def act(r0, l0):
            rows = slice(r0, r0 + rc)
            g = _conv_block(ge_ref, wf_ref, r0, rc, l0, K_F, False)
            vv = v_ref[rows, l0:l0 + LANES].astype(F32)
            sg = _sigmoid(g)
            silu = g * sg
            a_ref[rows, l0:l0 + LANES] = (silu * vv).astype(BF16)
            silu_ref[rows, l0:l0 + LANES] = silu
            dsv_ref[rows, l0:l0 + LANES] = (sg + silu * (1.0 - sg)) * vv

        def tail(r0):
            rows = slice(r0, r0 + rc)
            x3 = x2_ref[rows, :] + p_ref[rows, :]
            r = lax.rsqrt(jnp.mean(x3 * x3, axis=-1, keepdims=True) + RMS_EPS)
            xhat = x3 * r
            diff = xhat * g3_ref[...] - t_ref[rows, :]
            dout = diff * (1.0 / D_MODEL)
            dxh = dout * g3_ref[...]
            dx3 = r * (dxh - xhat * jnp.mean(dxh * xhat, axis=-1, keepdims=True))
            dx3_ref[rows, :] = dx3
            dx3b_ref[rows, :] = dx3.astype(BF16)
            sums_ref[0] = sums_ref[0] + _fold8(diff * diff)
            sums_ref[1] = sums_ref[1] + _fold8(dout * xhat)

        def back(r0, l0):
            rows = slice(r0, r0 + rc)
            da = da_ref[rows, l0:l0 + LANES]
            dv_ref[rows, l0:l0 + LANES] = (da * silu_ref[rows, l0:l0 + LANES]).astype(BF16)
            dgg = da * dsv_ref[rows, l0:l0 + LANES]
            dg_ref[rows, l0:l0 + LANES] = dgg.astype(BF16)
            _conv_wgrad_block(acc_ref, dgg, ge_ref, r0, rc, l0, K_F)

        hrows = ts // 2

        def units(fn, h, per_lane_block):
            starts = [h * hrows + q * rc for q in range(hrows // rc)]
            if per_lane_block:
                return [(fn, (r0, l0)) for r0 in starts for l0 in range(0, D_FF, LANES)]
            return [(fn, (r0,)) for r0 in starts]

        def rows_of(ref, h):
            return ref.at[pl.ds(h * hrows, hrows), :]

        def product(h):
            return _matmul_pieces([(rows_of(a_ref, h), wd_ref)], rows_of(p_ref, h), 2)

        def grad_a(h):
            return _matmul_pieces([(rows_of(dx3b_ref, h), wdt_ref)], rows_of(da_ref, h), 1)

        _interleaved(units(act, 0, True), [])
        _interleaved(units(act, 1, True), product(0))
        _interleaved(units(tail, 0, False), product(1))
        _interleaved(units(tail, 1, False), grad_a(0))
        _interleaved(units(back, 0, True), grad_a(1))
        _interleaved(units(back, 1, True), [])

        @pl.when(i == nt - 1)
        def _():
            _reduce_acc(dwf_ref, acc_ref, K_F)
            loss_ref[...] = (0.5 / D_MODEL) * jnp.sum(sums_ref[0], keepdims=True)
            dg3_ref[...] = jnp.sum(sums_ref[1], axis=0, keepdims=True)

    return pl.pallas_call(
        body, name="ffn_out_and_back", grid=(nt,),
        in_specs=[_rows(ts, D_FF), _prev(ts, D_FF), _next(ts, D_FF, s), _rows(ts, D_FF), _rows(ts, D_MODEL),
                  _const((K_F, D_FF)), _const((D_FF, D_MODEL)), _const((D_MODEL, D_FF)), _const((1, D_MODEL)),
                  _rows(ts, D_MODEL)],
        out_specs=[_rows(ts, D_FF), _rows(ts, D_MODEL), _rows(ts, D_MODEL), _acc_out((1, 1)), _acc_out((1, D_MODEL)),
                   _rows(ts, D_FF), _rows(ts, D_FF), _acc_out((K_F, D_FF))],
        out_shape=[_sds((s, D_FF), BF16), _sds((s, D_MODEL), F32), _sds((s, D_MODEL), BF16),
                   _sds((1, 1), F32), _sds((1, D_MODEL), F32),
                   _sds((s, D_FF), BF16), _sds((s, D_FF), BF16), _sds((K_F, D_FF), F32)],
        scratch_shapes=[_ext_scratch(ts, D_FF), pltpu.VMEM((ts, D_FF), F32), pltpu.VMEM((ts, D_FF), F32),
                        pltpu.VMEM((ts, D_FF), F32), pltpu.VMEM((K_F, SUBLANES, D_FF), F32),
                        pltpu.VMEM((ts, D_MODEL), F32), pltpu.VMEM((2, SUBLANES, D_MODEL), F32)],
        compiler_params=_params(),
    )(g0, g0, g0, v, x2, wf, w_down, _transposed(w_down), g3, target)


def _bwd_ffn_b(dg, dv, wf, w_gate, w_up, x2, g2, dx3, ts, exchange=None):
    s = x2.shape[0]
    nt = s // ts
    rc = min(ROW_CHUNK, ts)
    n_sub = ts // rc

    def body(dg_ref, dgp_ref, dgn_ref, dv_ref, wf_ref, wg_ref, wu_ref, x2_ref, g2_ref, dx3_ref,
             dg0_ref, dx2_ref, dx2b_ref, dgn2_ref, dge_ref, dg8_ref, a0_ref, a1_ref, p0_ref, p1_ref):
        i = pl.program_id(0)
        vt = jnp.minimum(i, nt - 1)
        live = (i >= 2).astype(F32)

        @pl.when(i == 0)
        def _():
            dg8_ref[...] = jnp.zeros_like(dg8_ref)
            a1_ref[...] = jnp.zeros_like(a1_ref)
            p1_ref[...] = jnp.zeros_like(p1_ref)

        _fill_halo(dge_ref, dgp_ref[...].astype(F32), dgn_ref[...].astype(F32), ts, vt == 0, vt == nt - 1)

        def fill(j, carry):
            r0 = pl.multiple_of(j * rc, rc)
            _put_rows(dge_ref, r0, rc, dg_ref[pl.ds(r0, rc), :].astype(F32))
            return carry

        lax.fori_loop(0, n_sub, fill, 0)

        def stage(a_new, a_old, p_new, p_old):
            def conv_t(r0, l0):
                rows = slice(r0, r0 + rc)
                dg0 = _conv_block(dge_ref, wf_ref, r0, rc, l0, K_F, True).astype(BF16)
                dg0_ref[rows, l0:l0 + LANES] = dg0
                a_new[rows, l0:l0 + LANES] = dg0

            def tail(r0):
                rows = slice(r0, r0 + rc)
                dx, dgrow = _rms_bwd_rows(p_old[rows, :], x2_ref[rows, :], g2_ref[...])
                dx2 = dx3_ref[rows, :] + dx
                dx2_ref[rows, :] = dx2
                dx2b_ref[rows, :] = dx2.astype(BF16)
                dg8_ref[...] = dg8_ref[...] + _fold8(dgrow) * live

            units = []
            for q in range(n_sub):
                units += [(conv_t, (q * rc, l0)) for l0 in range(0, D_FF, LANES)]
                units.append((tail, (q * rc,)))
            _interleaved(units, _matmul_pieces([(a_old, wg_ref), (dv_ref, wu_ref)], p_new, 2))

        @pl.when(i % 2 == 0)
        def _():
            stage(a0_ref, a1_ref, p0_ref, p1_ref)

        @pl.when(i % 2 == 1)
        def _():
            stage(a1_ref, a0_ref, p1_ref, p0_ref)

        @pl.when(i == nt + 1)
        def _():
            dgn2_ref[...] = jnp.sum(dg8_ref[...], axis=0, keepdims=True)

    vtile = lambda i: jnp.minimum(i, nt - 1)
    mtile = lambda i: jnp.clip(i - 1, 0, nt - 1)
    ttile = lambda i: jnp.clip(i - 2, 0, nt - 1)
    return _call(
        body, name="bwd_ffn_b", grid=(nt + 2,),
        in_specs=[_rows_at(ts, D_FF, vtile), _prev_at(ts, D_FF, vtile), _next_at(ts, D_FF, s, vtile),
                  _rows_at(ts, D_FF, mtile), _const((K_F, D_FF)),
                  _const((D_FF, D_MODEL)), _const((D_FF, D_MODEL)), _rows_at(ts, D_MODEL, ttile), _const((1, D_MODEL)),
                  _rows_at(ts, D_MODEL, ttile)],
        out_specs=[_rows_at(ts, D_FF, vtile), _rows_at(ts, D_MODEL, ttile), _rows_at(ts, D_MODEL, ttile),
                   _acc_out((1, D_MODEL))],
        out_shape=[_sds((s, D_FF), BF16), _sds((s, D_MODEL), F32), _sds((s, D_MODEL), BF16), _sds((1, D_MODEL), F32)],
        scratch_shapes=[_ext_scratch(ts, D_FF), pltpu.VMEM((SUBLANES, D_MODEL), F32),
                        pltpu.VMEM((ts, D_FF), BF16), pltpu.VMEM((ts, D_FF), BF16),
                        pltpu.VMEM((ts, D_MODEL), F32), pltpu.VMEM((ts, D_MODEL), F32)],
        args=(dg, dg, dg, dv, wf, w_gate, w_up, x2, g2, dx3), exchange=exchange)


def _bwd_mix_a(dx2b, w_out_t, z, u, wa, lg, lb, ts, exchange=None):
    s = dx2b.shape[0]
    nt = s // ts
    rc = min(ROW_CHUNK, ts)
    n_sub = ts // rc

    def body(dx_ref, wo_ref, z_ref, zp_ref, zn_ref, u_ref, wa_ref, lg_ref, lb_ref,
             dca_ref, du_ref, dab_ref, dwa_ref, dlg_ref, dlb_ref, dbb_ref, pe_ref, dy0_ref, dy1_ref, acc_ref, sacc_ref):
        i = pl.program_id(0)
        t = jnp.maximum(i - 1, 0)

        @pl.when(i == 0)
        def _():
            acc_ref[...] = jnp.zeros_like(acc_ref)
            sacc_ref[...] = jnp.zeros_like(sacc_ref)
            dy1_ref[...] = jnp.zeros_like(dy1_ref)

        pp, _ = _p_u0(zp_ref, slice(None))
        pn, _ = _p_u0(zn_ref, slice(None))
        _fill_halo(pe_ref, pp, pn, ts, t == 0, t == nt - 1)

        def fill(j, carry):
            r0 = pl.multiple_of(j * rc, rc)
            rows = pl.ds(r0, rc)
            _put_rows(pe_ref, r0, rc, z_ref[rows, 2 * D_A:3 * D_A].astype(F32) * z_ref[rows, 0:D_A].astype(F32))
            return carry

        lax.fori_loop(0, n_sub, fill, 0)

        def stage(dy_new, dy_old):
            def piece(m0, n0, width):
                dy_new[m0:m0 + MXU_ROWS, n0:n0 + width] = jnp.dot(
                    dx_ref[m0:m0 + MXU_ROWS, :], wo_ref[:, n0:n0 + width], preferred_element_type=F32).astype(BF16)

            units = []
            for q in range(n_sub):
                units += [(mixer_a, (dy_old, q * rc, l0)) for l0 in range(0, D_A, LANES)]
                units.append((mixer_b, (dy_old, q * rc)))
            _interleaved(units, [(piece, (m0, n0, w)) for n0, w in _col_pieces(D_MODEL) for m0 in range(0, ts, MXU_ROWS)])

        def mixer_a(dy_ref, r0, l0):
            rows = slice(r0, r0 + rc)
            ca = _conv_block(pe_ref, wa_ref, r0, rc, l0, K_A, False)
            a_b = z_ref[rows, D_A + l0:D_A + l0 + LANES].astype(F32)
            dya = dy_ref[rows, l0:l0 + LANES].astype(F32)
            dab_ref[rows, l0:l0 + LANES] = (dya * ca).astype(BF16)
            dca = dya * a_b
            dca_ref[rows, l0:l0 + LANES] = dca
            _conv_wgrad_block(acc_ref, dca, pe_ref, r0, rc, l0, K_A)

        def mixer_b(dy_ref, r0):
            rows = slice(r0, r0 + rc)
            ubs = [u_ref[rows, l0:l0 + LANES] for l0 in range(0, D_B, LANES)]
            mu, rstd = _layernorm_rows(ubs)
            ns, dns = [], []
            m1 = None
            m2 = None
            for q, l0 in enumerate(range(0, D_B, LANES)):
                n = (ubs[q] - mu) * rstd
                lgq = lg_ref[:, l0:l0 + LANES]
                t = n * lgq + lb_ref[:, l0:l0 + LANES]
                sg = _sigmoid(t)
                dt = dy_ref[rows, D_A + l0:D_A + l0 + LANES].astype(F32) * (sg * (1.0 + t * (1.0 - sg)))
                dn = dt * lgq
                ns.append(n)
                dns.append(dn)
                s1 = jnp.sum(dn, axis=-1, keepdims=True)
                s2 = jnp.sum(dn * n, axis=-1, keepdims=True)
                m1 = s1 if m1 is None else m1 + s1
                m2 = s2 if m2 is None else m2 + s2
                sacc_ref[0, :, l0:l0 + LANES] = sacc_ref[0, :, l0:l0 + LANES] + _fold8(dt * n)
                sacc_ref[1, :, l0:l0 + LANES] = sacc_ref[1, :, l0:l0 + LANES] + _fold8(dt)
            m1 = m1 * (1.0 / D_B)
            m2 = m2 * (1.0 / D_B)
            for q, l0 in enumerate(range(0, D_B, LANES)):
                du = rstd * (dns[q] - m1 - ns[q] * m2)
                du_ref[rows, l0:l0 + LANES] = du
                sacc_ref[2, :, l0:l0 + LANES] = sacc_ref[2, :, l0:l0 + LANES] + _fold8(du)

        @pl.when(i % 2 == 0)
        def _():
            stage(dy0_ref, dy1_ref)

        @pl.when(i % 2 == 1)
        def _():
            stage(dy1_ref, dy0_ref)

        @pl.when(i == nt)
        def _():
            _reduce_acc(dwa_ref, acc_ref, K_A)
            dlg_ref[...] = jnp.sum(sacc_ref[0], axis=0, keepdims=True)
            dlb_ref[...] = jnp.sum(sacc_ref[1], axis=0, keepdims=True)
            dbb_ref[...] = jnp.sum(sacc_ref[2], axis=0, keepdims=True)

    cur = lambda i: jnp.minimum(i, nt - 1)
    old = lambda i: jnp.maximum(i - 1, 0)
    return _call(
        body, name="bwd_mix_a", grid=(nt + 1,),
        in_specs=[_rows_at(ts, D_MODEL, cur), _const((D_MODEL, D_MODEL)), _rows_at(ts, D_IN, old), _prev_at(ts, D_IN, old),
                  _next_at(ts, D_IN, s, old), _rows_at(ts, D_B, old), _const((K_A, D_A)), _const((1, D_B)), _const((1, D_B))],
        out_specs=[_rows_at(ts, D_A, old), _rows_at(ts, D_B, old), _rows_at(ts, D_A, old), _acc_out((K_A, D_A)),
                   _acc_out((1, D_B)), _acc_out((1, D_B)), _acc_out((1, D_B))],
        out_shape=[_sds((s, D_A), F32), _sds((s, D_B), F32), _sds((s, D_A), BF16), _sds((K_A, D_A), F32),
                   _sds((1, D_B), F32), _sds((1, D_B), F32), _sds((1, D_B), F32)],
        scratch_shapes=[_ext_scratch(ts, D_A), pltpu.VMEM((ts, D_MODEL), BF16), pltpu.VMEM((ts, D_MODEL), BF16),
                        pltpu.VMEM((K_A, SUBLANES, D_A), F32), pltpu.VMEM((3, SUBLANES, D_B), F32)],
        args=(dx2b, w_out_t, z, z, z, u, wa, lg, lb), exchange=exchange)


def _bwd_mix_b(dca, du, z, dab, wa, wb, w_in, x, g1, dx2, ts, exchange=None):
    s = x.shape[0]
    nt = s // ts
    rc = min(ROW_CHUNK, ts)
    n_sub = ts // rc

    def body(dca_ref, dcap_ref, dcan_ref, du_ref, dup_ref, dun_ref, z_ref, zp_ref, zn_ref, dab_ref,
             wa_ref, wb_ref, wi_ref, x_ref, g1_ref, dx2_ref,
             dz_ref, dx_ref, dg1_ref, dwb_ref, dcae_ref, due_ref, ue_ref, acc_ref, dg8_ref,
             dz0_ref, dz1_ref, dh0_ref, dh1_ref):
        i = pl.program_id(0)

        @pl.when(i == 0)
        def _():
            acc_ref[...] = jnp.zeros_like(acc_ref)
            dg8_ref[...] = jnp.zeros_like(dg8_ref)
            dz1_ref[...] = jnp.zeros_like(dz1_ref)
            dh1_ref[...] = jnp.zeros_like(dh1_ref)

        vt = jnp.minimum(i, nt - 1)
        first = vt == 0
        last = vt == nt - 1
        live = (i < nt).astype(F32)
        _fill_halo(dcae_ref, dcap_ref[...], dcan_ref[...], ts, first, last)
        _fill_halo(due_ref, dup_ref[...], dun_ref[...], ts, first, last)
        _, up = _p_u0(zp_ref, slice(None))
        _, un = _p_u0(zn_ref, slice(None))
        _fill_halo(ue_ref, up, un, ts, first, last)

        def fill(j, carry):
            r0 = pl.multiple_of(j * rc, rc)
            rows = pl.ds(r0, rc)
            _put_rows(dcae_ref, r0, rc, dca_ref[rows, :])
            _put_rows(due_ref, r0, rc, du_ref[rows, :])
            b_v = z_ref[rows, 3 * D_A:3 * D_A + D_B].astype(F32)
            b_g = z_ref[rows, 3 * D_A + D_B:D_IN].astype(F32)
            _put_rows(ue_ref, r0, rc, b_v * _sigmoid(b_g))
            return carry

        lax.fori_loop(0, n_sub, fill, 0)

        def stage(dz_new, dz_old, dh_new, dh_old):
            def put(rows, c0, val):
                dz_ref[rows, c0:c0 + LANES] = val
                dz_new[rows, c0:c0 + LANES] = val

            def mixer_a(r0, l0):
                rows = slice(r0, r0 + rc)
                dp = _conv_block(dcae_ref, wa_ref, r0, rc, l0, K_A, True)
                a_h = z_ref[rows, l0:l0 + LANES].astype(F32)
                a_c = z_ref[rows, 2 * D_A + l0:2 * D_A + l0 + LANES].astype(F32)
                put(rows, l0, (dp * a_c).astype(BF16))
                put(rows, D_A + l0, dab_ref[rows, l0:l0 + LANES])
                put(rows, 2 * D_A + l0, (dp * a_h).astype(BF16))

            def mixer_b(r0, l0):
                rows = slice(r0, r0 + rc)
                du0 = _conv_block(due_ref, wb_ref, r0, rc, l0, K_B, True)
                b_v = z_ref[rows, 3 * D_A + l0:3 * D_A + l0 + LANES].astype(F32)
                b_g = z_ref[rows, 3 * D_A + D_B + l0:3 * D_A + D_B + l0 + LANES].astype(F32)
                sg = _sigmoid(b_g)
                put(rows, 3 * D_A + l0, (du0 * sg).astype(BF16))
                put(rows, 3 * D_A + D_B + l0, (du0 * b_v * (sg * (1.0 - sg))).astype(BF16))
                _conv_wgrad_block(acc_ref, du_ref[rows, l0:l0 + LANES], ue_ref, r0, rc, l0, K_B, live)

            def tail(r0):
                rows = slice(r0, r0 + rc)
                dx, dgrow = _rms_bwd_rows(dh_old[rows, :], x_ref[rows, :], g1_ref[...])
                dx_ref[rows, :] = dx2_ref[rows, :] + dx
                dg8_ref[...] = dg8_ref[...] + _fold8(dgrow)

            units = []
            for q in range(n_sub):
                units += [(mixer_a, (q * rc, l0)) for l0 in range(0, D_A, LANES)]
                units += [(mixer_b, (q * rc, l0)) for l0 in range(0, D_B, LANES)]
                units.append((tail, (q * rc,)))
            _interleaved(units, _matmul_pieces([(dz_old, wi_ref)], dh_new, 2))

        @pl.when(i % 2 == 0)
        def _():
            stage(dz0_ref, dz1_ref, dh0_ref, dh1_ref)

        @pl.when(i % 2 == 1)
        def _():
            stage(dz1_ref, dz0_ref, dh1_ref, dh0_ref)

        @pl.when(i == nt + 1)
        def _():
            _reduce_acc(dwb_ref, acc_ref, K_B)
            dg1_ref[...] = jnp.sum(dg8_ref[...], axis=0, keepdims=True)

    vtile = lambda i: jnp.minimum(i, nt - 1)
    ttile = lambda i: jnp.clip(i - 2, 0, nt - 1)
    return _call(
        body, name="bwd_mix_b", grid=(nt + 2,),
        in_specs=[_rows_at(ts, D_A, vtile), _prev_at(ts, D_A, vtile), _next_at(ts, D_A, s, vtile),
                  _rows_at(ts, D_B, vtile), _prev_at(ts, D_B, vtile), _next_at(ts, D_B, s, vtile),
                  _rows_at(ts, D_IN, vtile), _prev_at(ts, D_IN, vtile), _next_at(ts, D_IN, s, vtile), _rows_at(ts, D_A, vtile),
                  _const((K_A, D_A)), _const((K_B, D_B)), _const((D_IN, D_MODEL)), _rows_at(ts, D_MODEL, ttile),
                  _const((1, D_MODEL)), _rows_at(ts, D_MODEL, ttile)],
        out_specs=[_rows_at(ts, D_IN, vtile), _rows_at(ts, D_MODEL, ttile), _acc_out((1, D_MODEL)), _acc_out((K_B, D_B))],
        out_shape=[_sds((s, D_IN), BF16), _sds((s, D_MODEL), F32), _sds((1, D_MODEL), F32), _sds((K_B, D_B), F32)],
        scratch_shapes=[_ext_scratch(ts, D_A), _ext_scratch(ts, D_B), _ext_scratch(ts, D_B),
                        pltpu.VMEM((K_B, SUBLANES, D_B), F32), pltpu.VMEM((SUBLANES, D_MODEL), F32),
                        pltpu.VMEM((ts, D_IN), BF16), pltpu.VMEM((ts, D_IN), BF16),
                        pltpu.VMEM((ts, D_MODEL), F32), pltpu.VMEM((ts, D_MODEL), F32)],
        args=(dca, dca, dca, du, du, du, z, z, z, dab, wa, wb, w_in, x, g1, dx2), exchange=exchange)


def _matmul_tn(a, b, name, exchange=None):
    s, m = a.shape
    n = b.shape[1]
    tk = min(1024, s)
    nk = s // tk
    tm = 256

    def body(a_ref, b_ref, o_ref, acc_ref):
        k = pl.program_id(0)

        @pl.when(k == 0)
        def _():
            acc_ref[...] = jnp.zeros_like(acc_ref)

        for m0 in range(0, m, tm):
            acc_ref[m0:m0 + tm, :] = acc_ref[m0:m0 + tm, :] + lax.dot_general(
                a_ref[:, m0:m0 + tm], b_ref[...], _TN, preferred_element_type=F32)

        @pl.when(k == nk - 1)
        def _():
            o_ref[...] = acc_ref[...].astype(BF16)

    (out,), got = _call(
        body, name=name, grid=(nk,),
        in_specs=[_rows(tk, m), _rows(tk, n)],
        out_specs=[_acc_out((m, n))],
        out_shape=[_sds((m, n), BF16)],
        scratch_shapes=[pltpu.VMEM((m, n), F32)], args=(a, b), exchange=exchange)
    return out if exchange is None else (out, got)


CHIP_RELS = ((1, 0, 0), (0, 1, 0), (1, 1, 0))
CORE_RELS = ((0, 0, 1),)
ALL_RELS = ((0, 0, 1), (0, 1, 0), (0, 1, 1), (1, 0, 0), (1, 0, 1), (1, 1, 0), (1, 1, 1))


def _chip_slot(dev):
    return 2 * dev[0] + dev[1]


def _dev_slot(dev):
    return 4 * dev[0] + 2 * dev[1] + dev[2]


def _me():
    return (lax.axis_index("x"), lax.axis_index("y"), lax.axis_index("c"))


def _peer(me, rel):
    return tuple((1 - me[a]) if rel[a] else me[a] for a in range(3))


_ANY = pl.BlockSpec(memory_space=pl.ANY)


class _Exchange:
    def __init__(self, inputs, out_shape, scratch, start, finish, forward=None):
        self.inputs, self.out_shape, self.scratch = list(inputs), list(out_shape), list(scratch)
        self.start, self.finish, self.forward = start, finish, forward


def _all_gather(payloads):
    n_p = len(payloads)
    n_k = 1 + 2 * len(CHIP_RELS)

    def copy(srcs, dsts, sems, p, k, block_dev, to, from_src):
        blk = dsts[p].at[_dev_slot(block_dev)]
        return pltpu.make_async_remote_copy(
            src_ref=srcs[p] if from_src else blk, dst_ref=blk,
            send_sem=sems[0].at[n_k * p + k], recv_sem=sems[1].at[n_k * p + k], device_id=to, device_id_type=MESH)

    def own_copy(srcs, dsts, sems, p):
        return pltpu.make_async_copy(srcs[p], dsts[p].at[_dev_slot(_me())], sems[2].at[p])

    def start(srcs, dsts, sems):
        me = _me()
        for p in range(n_p):
            own_copy(srcs, dsts, sems, p).start()
        for j, rel in enumerate(CHIP_RELS):
            for p in range(n_p):
                copy(srcs, dsts, sems, p, 1 + j, me, _peer(me, rel), True).start()
        for p in range(n_p):
            copy(srcs, dsts, sems, p, 0, me, _peer(me, CORE_RELS[0]), True).start()

    def forward(srcs, dsts, sems):
        me = _me()
        sibling = _peer(me, CORE_RELS[0])
        for j, rel in enumerate(CHIP_RELS):
            other = _peer(me, rel)
            for p in range(n_p):
                copy(srcs, dsts, sems, p, 1 + j, other, me, False).wait_recv()
                copy(srcs, dsts, sems, p, 4 + j, other, sibling, False).start()

    def finish(srcs, dsts, sems):
        me = _me()
        sibling = _peer(me, CORE_RELS[0])
        for p in range(n_p):
            copy(srcs, dsts, sems, p, 0, sibling, me, False).wait_recv()
        for j, rel in enumerate(CHIP_RELS):
            for p in range(n_p):
                copy(srcs, dsts, sems, p, 4 + j, _peer(sibling, rel), me, False).wait_recv()
        for p in range(n_p):
            own_copy(srcs, dsts, sems, p).wait()
            copy(srcs, dsts, sems, p, 0, me, sibling, True).wait_send()
            for j, rel in enumerate(CHIP_RELS):
                copy(srcs, dsts, sems, p, 1 + j, me, _peer(me, rel), True).wait_send()
                copy(srcs, dsts, sems, p, 4 + j, _peer(me, rel), sibling, False).wait_send()

    return _Exchange(
        payloads, [_sds((N_DEV,) + p.shape, p.dtype) for p in payloads],
        [pltpu.SemaphoreType.DMA((n_p * n_k,)), pltpu.SemaphoreType.DMA((n_p * n_k,)), pltpu.SemaphoreType.DMA((n_p,))],
        start, finish, forward)


def _gather_direct(payload):
    n_r = len(ALL_RELS)

    def copies(srcs, dsts, sems):
        me = _me()
        mine = dsts[0].at[_dev_slot(me)]
        own = pltpu.make_async_copy(srcs[0], mine, sems[2].at[0])
        remote = [pltpu.make_async_remote_copy(src_ref=srcs[0], dst_ref=mine, send_sem=sems[0].at[k], recv_sem=sems[1].at[k],
                                               device_id=_peer(me, rel), device_id_type=MESH)
                  for k, rel in enumerate(ALL_RELS)]
        return [own] + remote

    def start(srcs, dsts, sems):
        for cp in copies(srcs, dsts, sems):
            cp.start()

    def finish(srcs, dsts, sems):
        for cp in copies(srcs, dsts, sems):
            cp.wait()

    return _Exchange([payload], [_sds((N_DEV,) + payload.shape, payload.dtype)],
                     [pltpu.SemaphoreType.DMA((n_r,)), pltpu.SemaphoreType.DMA((n_r,)), pltpu.SemaphoreType.DMA((1,))],
                     start, finish)


def _scatter_exchange(payloads, rels, src_view, view_shapes):
    n_p = len(payloads)
    n_r = len(rels)

    def copies(srcs, dsts, sems):
        me = _me()
        out = []
        for k, rel in enumerate(rels):
            peer = _peer(me, rel)
            for p in range(n_p):
                out.append(pltpu.make_async_remote_copy(
                    src_ref=src_view(srcs[p], peer), dst_ref=dsts[p].at[k],
                    send_sem=sems[0].at[p * n_r + k], recv_sem=sems[1].at[p * n_r + k],
                    device_id=peer, device_id_type=MESH))
        return out

    def start(srcs, dsts, sems):
        for cp in copies(srcs, dsts, sems):
            cp.start()

    def finish(srcs, dsts, sems):
        for cp in copies(srcs, dsts, sems):
            cp.wait()

    return _Exchange(payloads, [_sds((n_r,) + vs, p.dtype) for vs, p in zip(view_shapes, payloads)],
                     [pltpu.SemaphoreType.DMA((n_p * n_r,)), pltpu.SemaphoreType.DMA((n_p * n_r,))], start, finish)


def _split_refs(refs, sizes):
    out, at = [], 0
    for n in sizes:
        out.append(refs[at:at + n])
        at += n
    return out


def _join(exchanges):
    n_in = [len(e.inputs) for e in exchanges]
    n_out = [len(e.out_shape) for e in exchanges]
    n_sc = [len(e.scratch) for e in exchanges]

    def phase(name):
        def run(ins, outs, scs):
            for e, i, o, s in zip(exchanges, _split_refs(ins, n_in), _split_refs(outs, n_out), _split_refs(scs, n_sc)):
                if getattr(e, name) is not None:
                    getattr(e, name)(i, o, s)
        return run

    return _Exchange([a for e in exchanges for a in e.inputs], [s for e in exchanges for s in e.out_shape],
                     [s for e in exchanges for s in e.scratch], phase("start"), phase("finish"),
                     phase("forward") if any(e.forward is not None for e in exchanges) else None)


def _run_exchanges(name, exchanges):
    n_in = [len(e.inputs) for e in exchanges]
    n_out = [len(e.out_shape) for e in exchanges]
    n_sc = [len(e.scratch) for e in exchanges]

    def body(*refs):
        ins, outs, scs = _split_refs(refs, [sum(n_in), sum(n_out), sum(n_sc)])
        parts = list(zip(exchanges, _split_refs(ins, n_in), _split_refs(outs, n_out), _split_refs(scs, n_sc)))
        for e, i, o, s in parts:
            e.start(i, o, s)
        for e, i, o, s in parts:
            if e.forward is not None:
                e.forward(i, o, s)
        for e, i, o, s in parts:
            e.finish(i, o, s)

    outs = pl.pallas_call(
        body, name=name, in_specs=[_ANY] * sum(n_in), out_specs=[_ANY] * sum(n_out),
        out_shape=[sd for e in exchanges for sd in e.out_shape],
        scratch_shapes=[sc for e in exchanges for sc in e.scratch],
    )(*[a for e in exchanges for a in e.inputs])
    return _split_refs(list(outs), n_out)


def _call(body, *, name, grid, in_specs, out_specs, out_shape, scratch_shapes, args, exchange=None, forward_step=None):
    n_in, n_out, n_sc = len(in_specs), len(out_specs), len(scratch_shapes)
    if exchange is None:
        outs = pl.pallas_call(body, name=name, grid=grid, in_specs=in_specs, out_specs=out_specs, out_shape=out_shape,
                              scratch_shapes=scratch_shapes, compiler_params=_params())(*args)
        return list(outs), []
    e = exchange
    sizes = [n_in, len(e.inputs), n_out, len(e.out_shape), n_sc, len(e.scratch)]
    last = grid[0] - 1

    def wrapped(*refs):
        a, ei, o, eo, sc, es = _split_refs(refs, sizes)
        i = pl.program_id(0)

        @pl.when(i == 0)
        def _():
            e.start(ei, eo, es)

        if e.forward is not None:
            @pl.when(i == forward_step)
            def _():
                e.forward(ei, eo, es)

        body(*a, *o, *sc)

        @pl.when(i == last)
        def _():
            e.finish(ei, eo, es)

    outs = pl.pallas_call(
        wrapped, name=name, grid=grid,
        in_specs=list(in_specs) + [_ANY] * len(e.inputs), out_specs=list(out_specs) + [_ANY] * len(e.out_shape),
        out_shape=list(out_shape) + e.out_shape, scratch_shapes=list(scratch_shapes) + e.scratch,
        compiler_params=_params(),
    )(*args, *e.inputs)
    outs = list(outs)
    return outs[:n_out], outs[n_out:]


def _pair_sum(grads, recvd, my_core, name):
    n_p = len(grads)

    def body(c_ref, *refs):
        del c_ref
        for p in range(n_p):
            refs[2 * n_p + p][...] = (refs[p][...].astype(F32) + refs[n_p + p][...].astype(F32)).astype(BF16)

    def blk(g):
        return (None, None) + g.shape[2:]

    return pl.pallas_call(
        body, name=name,
        grid_spec=pltpu.PrefetchScalarGridSpec(
            num_scalar_prefetch=1, grid=(N_CHIP,),
            in_specs=[pl.BlockSpec(blk(g), lambda j, c: (j, c[0], 0, 0)) for g in grads]
            + [pl.BlockSpec(blk(g), lambda j, c: (0, j, 0, 0)) for g in grads],
            out_specs=[pl.BlockSpec((None,) + g.shape[2:], lambda j, c: (j, 0, 0)) for g in grads]),
        out_shape=[_sds((N_CHIP,) + g.shape[2:], BF16) for g in grads],
        compiler_params=pltpu.CompilerParams(dimension_semantics=("arbitrary",), vmem_limit_bytes=VMEM_LIMIT),
    )(my_core, *grads, *recvd)


def _chip_sum(psums, recvd, my_chip, name):
    n_p = len(psums)

    def body(c_ref, *refs):
        del c_ref
        for p in range(n_p):
            acc = refs[p][...].astype(F32)
            for k in range(len(CHIP_RELS)):
                acc = acc + refs[n_p + p][k].astype(F32)
            refs[2 * n_p + p][...] = acc

    return pl.pallas_call(
        body, name=name,
        grid_spec=pltpu.PrefetchScalarGridSpec(
            num_scalar_prefetch=1, grid=(1,),
            in_specs=[pl.BlockSpec((None,) + g.shape[1:], lambda i, c: (c[0], 0, 0)) for g in psums]
            + [pl.BlockSpec(r.shape, lambda i, c: (0, 0, 0)) for r in recvd],
            out_specs=[pl.BlockSpec(g.shape[1:], lambda i, c: (0, 0)) for g in psums]),
        out_shape=[_sds(g.shape[1:], F32) for g in psums],
        compiler_params=pltpu.CompilerParams(dimension_semantics=("arbitrary",), vmem_limit_bytes=VMEM_LIMIT),
    )(my_chip, *psums, *recvd)


def _sum_devices(parts, shapes):
    plan = _small_rows(shapes)

    def body(p_ref, *refs):
        outs, tot_ref = refs[:-1], refs[-1]
        acc = p_ref[0]
        for j in range(1, N_DEV):
            acc = acc + p_ref[j]
        tot_ref[...] = acc
        for idx, r, c0, width, at in plan:
            outs[idx][r:r + 1, c0:c0 + width] = tot_ref[at:at + 1, 0:width]

    return pl.pallas_call(body, name="small_grad_sum", out_shape=[_sds(s, F32) for s in shapes],
                          scratch_shapes=[pltpu.VMEM(parts.shape[1:], F32)])(parts)


def _transposed(w):
    r, c = w.shape
    tr = MXU_COLS

    def body(w_ref, o_ref):
        o_ref[...] = w_ref[...].T

    return pl.pallas_call(
        body, name="transpose_w_down", grid=(r // tr,),
        in_specs=[pl.BlockSpec((tr, c), lambda i: (i, 0))], out_specs=pl.BlockSpec((c, tr), lambda i: (0, i)),
        out_shape=_sds((c, r), w.dtype), compiler_params=_params())(w)


def _cast_shards(shards):
    def body(*refs):
        for src, dst in zip(refs[:len(shards)], refs[len(shards):]):
            dst[...] = src[...].astype(BF16)

    return pl.pallas_call(body, name="cast_shards", out_shape=[_sds(a.shape, BF16) for a in shards],
                          compiler_params=pltpu.CompilerParams(vmem_limit_bytes=VMEM_LIMIT))(*shards)


def _adamw(ws, gs, ms, vs, name):
    n_t = len(ws)

    def body(*refs):
        w_refs, g_refs, m_refs, v_refs = (refs[j * n_t:(j + 1) * n_t] for j in range(4))
        outs = refs[4 * n_t:]
        for k in range(n_t):
            gg = g_refs[k][...]
            mn = ADAM_B1 * m_refs[k][...] + (1.0 - ADAM_B1) * gg
            vn = ADAM_B2 * v_refs[k][...] + (1.0 - ADAM_B2) * (gg * gg)
            m_hat = mn / (1.0 - ADAM_B1 ** ADAM_STEP)
            v_hat = vn / (1.0 - ADAM_B2 ** ADAM_STEP)
            outs[3 * k][...] = -ADAM_LR * (m_hat / (jnp.sqrt(v_hat) + ADAM_EPS) + ADAM_WD * w_refs[k][...])
            outs[3 * k + 1][...] = mn
            outs[3 * k + 2][...] = vn

    out_shape = [_sds(w.shape, F32) for w in ws for _ in range(3)]
    return pl.pallas_call(body, name=name, out_shape=out_shape,
                          compiler_params=pltpu.CompilerParams(vmem_limit_bytes=VMEM_LIMIT))(*ws, *gs, *ms, *vs)


class _Mesh:
    def __init__(self, shards, my_chip, my_core):
        self.shards, self.my_chip, self.my_core = shards, my_chip.reshape(1), my_core.reshape(1)

    def gather(self, names):
        return _all_gather([self.shards[n] for n in names])

    @staticmethod
    def whole(gathered):
        return gathered.reshape(N_DEV * gathered.shape[1], gathered.shape[2])

    @staticmethod
    def by_device(grads):
        return [g.reshape(N_CHIP, 2, g.shape[0] // N_DEV, g.shape[1]) for g in grads]

    @staticmethod
    def to_sibling(parts):
        return _scatter_exchange(parts, CORE_RELS, lambda ref, peer: ref.at[:, peer[2]],
                                 [(N_CHIP,) + p.shape[2:] for p in parts])

    @staticmethod
    def to_chips(pair):
        return _scatter_exchange(pair, CHIP_RELS, lambda ref, peer: ref.at[_chip_slot(peer)], [p.shape[1:] for p in pair])


def _step(x, target, g1, w_in_t, wa, wb, bb, lg, lb, w_out, g2, w_gate_t, w_up_t, wf, w_down, g3, ts, mesh=None):
    (z, h1), got = _fwd_in(x, g1, w_in_t, ts, exchange=mesh and mesh.gather(["w_out", "w_down"]))
    if mesh:
        w_out, w_down = [mesh.whole(g) for g in got]
    (x2, y, u), got = _fwd_mix(z, x, wa, wb, bb, lg, lb, w_out, ts, exchange=mesh and mesh.gather(["w_gate", "w_up"]))
    if mesh:
        w_gate_t, w_up_t = [mesh.whole(g) for g in got]
    (g0, v, h2), _ = _fwd_ffn_in(x2, g2, w_gate_t, w_up_t, ts)
    tk = min(ts, SKEW_TILE)
    a, dx3, dx3b, loss, dg3, dgc, dv, dwf = _ffn_out_and_back(g0, v, x2, wf, w_down, g3, target, tk)
    one = dict(w_down=_matmul_tn(a, dx3b, "wgrad_down"), w_up=_matmul_tn(dv, h2, "wgrad_up"))
    parts1 = mesh and mesh.by_device(list(one.values()))
    (dg0, dx2, dx2b, dg2), got = _bwd_ffn_b(dgc, dv, wf, w_gate_t, w_up_t, x2, g2, dx3, tk,
                                            exchange=mesh and mesh.to_sibling(parts1))
    pair1 = mesh and _pair_sum(parts1, got, mesh.my_core, "rs_pair_sum_1")
    two = dict(w_gate=_matmul_tn(dg0, h2, "wgrad_gate"), w_out=_matmul_tn(y, dx2b, "wgrad_out"))
    parts2 = mesh and mesh.by_device(list(two.values()))
    (dca, du, dab, dwa, dlg, dlb, dbb), got = _bwd_mix_a(
        dx2b, jnp.swapaxes(w_out, 0, 1), z, u, wa, lg, lb, tk,
        exchange=mesh and _join([mesh.to_chips(pair1), mesh.to_sibling(parts2)]))
    (dz, dx, dg1, dwb), _ = _bwd_mix_b(dca, du, z, dab, wa, wb, w_in_t, x, g1, dx2, tk)
    small = dict(norm_mix_g=dg1, conv_a_w=dwa, conv_b_w=dwb, conv_b_b=dbb, ln_b_g=dlg, ln_b_b=dlb,
                 norm_ffn_g=dg2, conv_ffn_w=dwf, norm_final_g=dg3)
    if not mesh:
        return loss, dx, dict(w_in=_matmul_tn(dz, h1, "wgrad_in"), **one, **two), small
    big = dict(zip(one, _chip_sum(pair1, got[:2], mesh.my_chip, "rs_chip_sum_1")))
    pair2 = _pair_sum(parts2, got[2:], mesh.my_core, "rs_pair_sum_2")
    dw_in_t, got = _matmul_tn(dz, h1, "wgrad_in",
                              exchange=_join([mesh.to_chips(pair2), _gather_direct(_pack_small_grads(small, loss))]))
    big.update(zip(two, _chip_sum(pair2, got[:2], mesh.my_chip, "rs_chip_sum_2")))
    every = got[2]
    parts = mesh.by_device([dw_in_t])
    (got,) = _run_exchanges("rs_cores_last", [mesh.to_sibling(parts)])
    pair = _pair_sum(parts, got, mesh.my_core, "rs_pair_sum_last")
    (got,) = _run_exchanges("rs_chips_last", [mesh.to_chips(pair)])
    (big["w_in"],) = _chip_sum(pair, got, mesh.my_chip, "rs_chip_sum_last")
    shapes = [loss.shape if n == "loss" else small[n].shape for n in _SMALL_NAMES]
    return None, dx, big, dict(zip(_SMALL_NAMES, _sum_devices(every, shapes)))


def _pack_small_weights(conv_a_s, conv_b_s, conv_ffn_s):
    def body(a_ref, b_ref, f_ref, out):
        out[...] = jnp.zeros_like(out)
        at = 0
        for src in (a_ref, b_ref, f_ref):
            rows, cols = src.shape
            for r in range(rows):
                out[at:at + 1, 0:cols] = src[r:r + 1, :]
                at += 1

    return pl.pallas_call(body, name="small_weight_pack", out_shape=_sds((SMALL_W_ROWS, SMALL_W_COLS), F32))(
        conv_a_s, conv_b_s, conv_ffn_s)


def _unpack_small_weights(full):
    def take(r0, k, w):
        return jnp.transpose(full[:, r0:r0 + k, 0:w], (1, 0, 2)).reshape(k, N_DEV * w)

    return take(0, K_A, CONV_A_COLS), take(K_A, K_B, CONV_A_COLS), take(K_A + K_B, K_F, W_FF_COLS)


_SMALL_NAMES = ("conv_b_w", "conv_a_w", "conv_ffn_w", "norm_mix_g", "norm_ffn_g", "norm_final_g",
                "conv_b_b", "ln_b_g", "ln_b_b", "loss")
SMALL_G_ROWS = 64


def _small_rows(shapes):
    plan, at = [], 0
    for idx, (rows, cols) in enumerate(shapes):
        for r in range(rows):
            for c0 in range(0, cols, SMALL_G_COLS):
                plan.append((idx, r, c0, min(SMALL_G_COLS, cols - c0), at))
                at += 1
    assert at <= SMALL_G_ROWS
    return plan


def _pack_small_grads(small, loss):
    srcs = [loss if n == "loss" else small[n] for n in _SMALL_NAMES]
    plan = _small_rows([a.shape for a in srcs])

    def body(*refs):
        out = refs[-1]
        out[...] = jnp.zeros_like(out)
        for idx, r, c0, width, at in plan:
            out[at:at + 1, 0:width] = refs[idx][r:r + 1, c0:c0 + width]

    return pl.pallas_call(body, name="small_grad_pack", out_shape=_sds((SMALL_G_ROWS, SMALL_G_COLS), F32))(*srcs)


def kernel(x, norm_mix_g, w_in, conv_a_w, conv_b_w, conv_b_b, ln_b_g, ln_b_b, w_out, norm_ffn_g, w_gate, w_up, conv_ffn_w, w_down, norm_final_g, loss_target, m_norm_mix_g, m_w_in, m_conv_a_w, m_conv_b_w, m_conv_b_b, m_ln_b_g, m_ln_b_b, m_w_out, m_norm_ffn_g, m_w_gate, m_w_up, m_conv_ffn_w, m_w_down, m_norm_final_g, v_norm_mix_g, v_w_in, v_conv_a_w, v_conv_b_w, v_conv_b_b, v_ln_b_g, v_ln_b_b, v_w_out, v_norm_ffn_g, v_w_gate, v_w_up, v_conv_ffn_w, v_w_down, v_norm_final_g):
    ix, iy, ic = lax.axis_index("x"), lax.axis_index("y"), lax.axis_index("c")
    my_chip = (2 * ix + iy).astype(jnp.int32)
    my_core = ic.astype(jnp.int32)
    my_dev = 2 * my_chip + my_core

    weights = dict(norm_mix_g=norm_mix_g, w_in=w_in, conv_a_w=conv_a_w, conv_b_w=conv_b_w, conv_b_b=conv_b_b,
                   ln_b_g=ln_b_g, ln_b_b=ln_b_b, w_out=w_out, norm_ffn_g=norm_ffn_g, w_gate=w_gate, w_up=w_up,
                   conv_ffn_w=conv_ffn_w, w_down=w_down, norm_final_g=norm_final_g)
    m_in = dict(norm_mix_g=m_norm_mix_g, w_in=m_w_in, conv_a_w=m_conv_a_w, conv_b_w=m_conv_b_w, conv_b_b=m_conv_b_b,
                ln_b_g=m_ln_b_g, ln_b_b=m_ln_b_b, w_out=m_w_out, norm_ffn_g=m_norm_ffn_g, w_gate=m_w_gate,
                w_up=m_w_up, conv_ffn_w=m_conv_ffn_w, w_down=m_w_down, norm_final_g=m_norm_final_g)
    v_in = dict(norm_mix_g=v_norm_mix_g, w_in=v_w_in, conv_a_w=v_conv_a_w, conv_b_w=v_conv_b_w, conv_b_b=v_conv_b_b,
                ln_b_g=v_ln_b_g, ln_b_b=v_ln_b_b, w_out=v_w_out, norm_ffn_g=v_norm_ffn_g, w_gate=v_w_gate,
                w_up=v_w_up, conv_ffn_w=v_conv_ffn_w, w_down=v_w_down, norm_final_g=v_norm_final_g)
    order = list(weights)
    big_names = ("w_in", "w_gate", "w_up", "w_out", "w_down")
    transposed = ("w_in", "w_gate", "w_up")

    def shard2d(name, a):
        if name in transposed:
            return jnp.swapaxes(a[0], 0, 1)
        return a.reshape(1, a.shape[0]) if a.ndim == 1 else a.reshape(a.shape[-2:])

    def unshard2d(name, a2, like):
        if name in transposed:
            return jnp.swapaxes(a2, 0, 1)[None]
        return a2.reshape(like.shape)

    mesh = _Mesh(dict(zip(big_names, _cast_shards([shard2d(n, weights[n]) for n in big_names]))), my_chip, my_core)
    gathered, = _run_exchanges("ag_first", [_all_gather(
        [mesh.shards["w_in"], _pack_small_weights(conv_a_w[0], conv_b_w[0], conv_ffn_w[0])])])
    w_in_t = mesh.whole(gathered[0])
    wa_f, wb_f, wf_f = _unpack_small_weights(gathered[1])

    _, dx, gsum, stot = _step(
        x[0], loss_target[0], norm_mix_g, w_in_t, wa_f, wb_f, conv_b_b, ln_b_g, ln_b_b, None, norm_ffn_g,
        None, None, wf_f, None, norm_final_g.reshape(1, D_MODEL), SEQ_TILE, mesh)

    grads2d = dict(
        norm_mix_g=stot["norm_mix_g"],
        conv_a_w=lax.dynamic_slice(stot["conv_a_w"], (0, my_dev * CONV_A_COLS), (K_A, CONV_A_COLS)),
        conv_b_w=lax.dynamic_slice(stot["conv_b_w"], (0, my_dev * CONV_A_COLS), (K_B, CONV_A_COLS)),
        conv_b_b=stot["conv_b_b"], ln_b_g=stot["ln_b_g"], ln_b_b=stot["ln_b_b"],
        norm_ffn_g=stot["norm_ffn_g"],
        conv_ffn_w=lax.dynamic_slice(stot["conv_ffn_w"], (0, my_dev * W_FF_COLS), (K_F, W_FF_COLS)),
        norm_final_g=stot["norm_final_g"],
        **gsum,
    )

    updates = {}
    small_names = [n for n in order if n not in big_names]
    for group, label in [([n], "adamw_" + n) for n in big_names] + [(small_names, "adamw_small")]:
        outs = _adamw([shard2d(n, weights[n]) for n in group], [grads2d[n] for n in group],
                      [shard2d(n, m_in[n]) for n in group], [shard2d(n, v_in[n]) for n in group], label)
        for k, n in enumerate(group):
            updates[n] = outs[3 * k:3 * k + 3]
    g_out = [unshard2d(n, grads2d[n], weights[n]) for n in order]
    d_out, m_out, v_out = [[unshard2d(n, updates[n][j], weights[n]) for n in order] for j in range(3)]

    return (stot["loss"][0, 0], dx[None], *g_out, *d_out, *m_out, *v_out)
```

```python
import jax
import jax.numpy as jnp
from jax import lax
from jax.experimental import pallas as pl
from jax.experimental.pallas import tpu as pltpu

F32 = jnp.float32
BF16 = jnp.bfloat16

D_MODEL = 1024
D_A = 512
D_B = 512
D_IN = 3 * D_A + 2 * D_B
D_FF = 2816
K_A = 3
K_B = 31
K_F = 3
RMS_EPS = 1e-6
LN_EPS = 1e-5

ADAM_LR = 0.001
ADAM_B1 = 0.9
ADAM_B2 = 0.999
ADAM_EPS = 1e-08
ADAM_WD = 0.01
ADAM_STEP = 10

N_DEV = 8
N_CHIP = 4
LANES = 128
SUBLANES = 8
HALO = 16
ROW_CHUNK = 64
SEQ_TILE = 512
SKEW_TILE = 256
VMEM_LIMIT = 56 * 1024 * 1024

MESH = pl.DeviceIdType.MESH

W_FF_COLS = D_FF // N_DEV
CONV_A_COLS = D_A // N_DEV
SMALL_W_ROWS = 40
SMALL_W_COLS = 384
SMALL_G_COLS = 512
FF_PAD = 3072


def _rows(ts, c):
    return pl.BlockSpec((ts, c), lambda i: (i, 0))


def _const(shape):
    return pl.BlockSpec(shape, lambda i: (0,) * len(shape), pipeline_mode=pl.Buffered(1))


def _acc_out(shape):
    return pl.BlockSpec(shape, lambda i: (0,) * len(shape))


def _rows_at(ts, c, tile):
    return pl.BlockSpec((ts, c), lambda i: (tile(i), 0))


def _prev_at(ts, c, tile):
    return pl.BlockSpec((HALO, c), lambda i: (jnp.maximum(tile(i) * (ts // HALO) - 1, 0), 0))


def _next_at(ts, c, s, tile):
    last = s // HALO - 1
    return pl.BlockSpec((HALO, c), lambda i: (jnp.minimum((tile(i) + 1) * (ts // HALO), last), 0))


def _prev(ts, c):
    return _prev_at(ts, c, lambda i: i)


def _next(ts, c, s):
    return _next_at(ts, c, s, lambda i: i)


MXU_COLS = 256
MXU_ROWS = 256


def _col_pieces(n):
    return [(c0, min(MXU_COLS, n - c0)) for c0 in range(0, n, MXU_COLS)]


def _matmul_pieces(terms, out_ref, k_parts, w_transposed=False):
    m, n = out_ref.shape
    rows = min(MXU_ROWS, m)
    steps = []
    for lhs_ref, w_ref in terms:
        tiles = lhs_ref.shape[1] // MXU_COLS
        cuts = [MXU_COLS * (tiles * j // k_parts) for j in range(k_parts)] + [lhs_ref.shape[1]]
        steps += [(lhs_ref, w_ref, cuts[j], cuts[j + 1]) for j in range(k_parts)]

    def piece(m0, n0, width, step):
        lhs_ref, w_ref, k0, k1 = steps[step]
        if w_transposed:
            part = lax.dot_general(lhs_ref[m0:m0 + rows, k0:k1], w_ref[n0:n0 + width, k0:k1], _NT,
                                   preferred_element_type=F32)
        else:
            part = jnp.dot(lhs_ref[m0:m0 + rows, k0:k1], w_ref[k0:k1, n0:n0 + width], preferred_element_type=F32)
        if step:
            part = part + out_ref[m0:m0 + rows, n0:n0 + width]
        out_ref[m0:m0 + rows, n0:n0 + width] = part

    return [(piece, (m0, n0, w, j)) for j in range(len(steps)) for n0, w in _col_pieces(n) for m0 in range(0, m, rows)]


def _interleaved(vector_units, matmul_pieces):
    n_u, n_p = len(vector_units), len(matmul_pieces)
    done = 0
    for k, (unit, args) in enumerate(vector_units):
        while done < n_p and done * n_u <= k * n_p:
            matmul_pieces[done][0](*matmul_pieces[done][1])
            done += 1
        unit(*args)
    for fn, args in matmul_pieces[done:]:
        fn(*args)


def _params():
    return pltpu.CompilerParams(dimension_semantics=("arbitrary",), vmem_limit_bytes=VMEM_LIMIT)


def _sds(shape, dtype):
    return jax.ShapeDtypeStruct(shape, dtype)


def _sigmoid(v):
    return 0.5 * jnp.tanh(0.5 * v) + 0.5


def _conv_block(ext_ref, w_ref, r0, rc, l0, k_taps, transposed):
    acc = None
    for k in range(k_taps):
        d = (k_taps // 2 - k) if transposed else (k - k_taps // 2)
        term = ext_ref[l0 // LANES, pl.ds(r0 + HALO + d, rc), :] * w_ref[k:k + 1, l0:l0 + LANES]
        acc = term if acc is None else acc + term
    return acc


def _conv_wgrad_block(acc_ref, dout, ext_ref, r0, rc, l0, k_taps, scale=None):
    for k in range(k_taps):
        prod = dout * ext_ref[l0 // LANES, pl.ds(r0 + HALO + k - k_taps // 2, rc), :]
        part = prod.reshape(rc // SUBLANES, SUBLANES, LANES).sum(axis=0)
        if scale is not None:
            part = part * scale
        acc_ref[k, :, l0:l0 + LANES] = acc_ref[k, :, l0:l0 + LANES] + part


def _reduce_acc(out_ref, acc_ref, k_taps):
    for k in range(k_taps):
        out_ref[k:k + 1, :] = jnp.sum(acc_ref[k], axis=0, keepdims=True)


def _fold8(v):
    rc, c = v.shape
    return v.reshape(rc // SUBLANES, SUBLANES, c).sum(axis=0)


def _ext_scratch(ts, c):
    return pltpu.VMEM((c // LANES, ts + 2 * HALO, LANES), F32)


def _put_rows(ext_ref, r0, rc, val):
    for q in range(val.shape[1] // LANES):
        ext_ref[q, pl.ds(r0 + HALO, rc), :] = val[:, q * LANES:(q + 1) * LANES]


def _fill_halo(ext_ref, vals_prev, vals_next, ts, first, last):
    for q in range(vals_prev.shape[1] // LANES):
        cols = slice(q * LANES, (q + 1) * LANES)
        ext_ref[q, 0:HALO, :] = jnp.where(first, 0.0, vals_prev[:, cols])
        ext_ref[q, HALO + ts:HALO + ts + HALO, :] = jnp.where(last, 0.0, vals_next[:, cols])


def _rms_bwd_rows(dh, xf, g):
    r = lax.rsqrt(jnp.mean(xf * xf, axis=-1, keepdims=True) + RMS_EPS)
    xhat = xf * r
    dxh = dh * g
    dx = r * (dxh - xhat * jnp.mean(dxh * xhat, axis=-1, keepdims=True))
    return dx, dh * xhat


_NT = (((1,), (1,)), ((), ()))
_TN = (((0,), (0,)), ((), ()))


def _fwd_in(x, g1, w_in_t, ts, exchange=None):
    s = x.shape[0]

    def body(x_ref, g_ref, w_ref, z_ref, h_ref):
        xf = x_ref[...]
        r = lax.rsqrt(jnp.mean(xf * xf, axis=-1, keepdims=True) + RMS_EPS)
        h = (xf * r * g_ref[...]).astype(BF16)
        h_ref[...] = h
        for n0 in range(0, D_IN, 512):
            z_ref[:, n0:n0 + 512] = lax.dot_general(h, w_ref[n0:n0 + 512, :], _NT,
                                                    preferred_element_type=F32).astype(BF16)

    return _call(
        body, name="fwd_in", grid=(s // ts,),
        in_specs=[_rows(ts, D_MODEL), _const((1, D_MODEL)), _const((D_IN, D_MODEL))],
        out_specs=[_rows(ts, D_IN), _rows(ts, D_MODEL)],
        out_shape=[_sds((s, D_IN), BF16), _sds((s, D_MODEL), BF16)],
        scratch_shapes=[], args=(x, g1, w_in_t), exchange=exchange, forward_step=(s // ts) * 3 // 4)


def _p_u0(z_ref, rows):
    a_h = z_ref[rows, 0:D_A].astype(F32)
    a_c = z_ref[rows, 2 * D_A:3 * D_A].astype(F32)
    b_v = z_ref[rows, 3 * D_A:3 * D_A + D_B].astype(F32)
    b_g = z_ref[rows, 3 * D_A + D_B:D_IN].astype(F32)
    return a_c * a_h, b_v * _sigmoid(b_g)


def _layernorm_rows(u_blocks):
    tot = None
    for ub in u_blocks:
        sm = jnp.sum(ub, axis=-1, keepdims=True)
        tot = sm if tot is None else tot + sm
    mu = tot * (1.0 / D_B)
    var = None
    for ub in u_blocks:
        sq = jnp.sum((ub - mu) * (ub - mu), axis=-1, keepdims=True)
        var = sq if var is None else var + sq
    rstd = lax.rsqrt(var * (1.0 / D_B) + LN_EPS)
    return mu, rstd


def _fwd_mix(z, x, wa, wb, bb, lg, lb, w_out, ts, exchange=None):
    s = x.shape[0]
    nt = s // ts
    rc = min(ROW_CHUNK, ts)

    def body(z_ref, zp_ref, zn_ref, x_ref, wa_ref, wb_ref, bb_ref, lg_ref, lb_ref, wo_ref,
             x2_ref, y_ref, u_ref, pe_ref, ue_ref):
        i = pl.program_id(0)
        pp, up = _p_u0(zp_ref, slice(None))
        pn, un = _p_u0(zn_ref, slice(None))
        _fill_halo(pe_ref, pp, pn, ts, i == 0, i == nt - 1)
        _fill_halo(ue_ref, up, un, ts, i == 0, i == nt - 1)

        def fill(j, carry):
            r0 = pl.multiple_of(j * rc, rc)
            p, u0 = _p_u0(z_ref, pl.ds(r0, rc))
            _put_rows(pe_ref, r0, rc, p)
            _put_rows(ue_ref, r0, rc, u0)
            return carry

        lax.fori_loop(0, ts // rc, fill, 0)

        def mixer_a(r0, l0):
            rows = pl.ds(r0, rc)
            ca = _conv_block(pe_ref, wa_ref, r0, rc, l0, K_A, False)
            a_b = z_ref[rows, D_A + l0:D_A + l0 + LANES].astype(F32)
            y_ref[rows, l0:l0 + LANES] = (a_b * ca).astype(BF16)

        def conv_b(r0, l0):
            u_ref[pl.ds(r0, rc), l0:l0 + LANES] = (_conv_block(ue_ref, wb_ref, r0, rc, l0, K_B, False)
                                                   + bb_ref[:, l0:l0 + LANES])

        def norm_b(r0):
            rows = pl.ds(r0, rc)
            ubs = [u_ref[rows, l0:l0 + LANES] for l0 in range(0, D_B, LANES)]
            mu, rstd = _layernorm_rows(ubs)
            for q, l0 in enumerate(range(0, D_B, LANES)):
                t = (ubs[q] - mu) * rstd * lg_ref[:, l0:l0 + LANES] + lb_ref[:, l0:l0 + LANES]
                y_ref[rows, D_A + l0:D_A + l0 + LANES] = (t * _sigmoid(t)).astype(BF16)

        def chunk_units(r0):
            return ([(mixer_a, (r0, l0)) for l0 in range(0, D_A, LANES)]
                    + [(conv_b, (r0, l0)) for l0 in range(0, D_B, LANES)] + [(norm_b, (r0,))])

        def main(j, carry):
            for fn, args in chunk_units(pl.multiple_of(j * rc, rc)):
                fn(*args)
            return carry

        def project(m0, n0, width):
            rows = slice(m0, m0 + hrows)
            x2_ref[rows, n0:n0 + width] = x_ref[rows, n0:n0 + width] + jnp.dot(
                y_ref[rows, :], wo_ref[:, n0:n0 + width], preferred_element_type=F32)

        hrows = ts // 2
        lax.fori_loop(0, hrows // rc, main, 0, unroll=4)
        _interleaved([u for q in range(hrows // rc) for u in chunk_units(hrows + q * rc)],
                     [(project, (0, n0, w)) for n0, w in _col_pieces(D_MODEL)])
        for n0, w in _col_pieces(D_MODEL):
            project(hrows, n0, w)

    return _call(
        body, name="fwd_mix", grid=(nt,),
        in_specs=[_rows(ts, D_IN), _prev(ts, D_IN), _next(ts, D_IN, s), _rows(ts, D_MODEL),
                  _const((K_A, D_A)), _const((K_B, D_B)), _const((1, D_B)), _const((1, D_B)), _const((1, D_B)),
                  _const((D_MODEL, D_MODEL))],
        out_specs=[_rows(ts, D_MODEL), _rows(ts, D_MODEL), _rows(ts, D_B)],
        out_shape=[_sds((s, D_MODEL), F32), _sds((s, D_MODEL), BF16), _sds((s, D_B), F32)],
        scratch_shapes=[_ext_scratch(ts, D_A), _ext_scratch(ts, D_B)],
        args=(z, z, z, x, wa, wb, bb, lg, lb, w_out), exchange=exchange, forward_step=nt * 5 // 8)


def _fwd_ffn_in(x2, g2, w_gate_t, w_up_t, ts, exchange=None, w_down=None):
    s = x2.shape[0]
    half = D_FF // 2

    def body(x_ref, g_ref, wg_ref, wu_ref, g0_ref, v_ref, h_ref):
        xf = x_ref[...]
        r = lax.rsqrt(jnp.mean(xf * xf, axis=-1, keepdims=True) + RMS_EPS)
        h = (xf * r * g_ref[...]).astype(BF16)
        h_ref[...] = h
        for n0 in range(0, D_FF, half):
            g0_ref[:, n0:n0 + half] = lax.dot_general(h, wg_ref[n0:n0 + half, :], _NT,
                                                      preferred_element_type=F32).astype(BF16)
            v_ref[:, n0:n0 + half] = lax.dot_general(h, wu_ref[n0:n0 + half, :], _NT,
                                                     preferred_element_type=F32).astype(BF16)

    def body_and_transpose(x_ref, g_ref, wg_ref, wu_ref, wd_ref, g0_ref, v_ref, h_ref, wdt_ref):
        @pl.when(pl.program_id(0) < n_blk)
        def _():
            wdt_ref[...] = wd_ref[...].T

        body(x_ref, g_ref, wg_ref, wu_ref, g0_ref, v_ref, h_ref)

    in_specs = [_rows(ts, D_MODEL), _const((1, D_MODEL)), _const((D_FF, D_MODEL)), _const((D_FF, D_MODEL))]
    out_specs = [_rows(ts, D_FF), _rows(ts, D_FF), _rows(ts, D_MODEL)]
    out_shape = [_sds((s, D_FF), BF16), _sds((s, D_FF), BF16), _sds((s, D_MODEL), BF16)]
    n_blk = D_FF // MXU_COLS
    if w_down is None or s // ts < n_blk:
        outs, got = _call(body, name="fwd_ffn_in", grid=(s // ts,), in_specs=in_specs, out_specs=out_specs,
                          out_shape=out_shape, scratch_shapes=[], args=(x2, g2, w_gate_t, w_up_t), exchange=exchange,
                          forward_step=(s // ts) // 2)
        return outs + [None if w_down is None else _transposed(w_down)], got
    blk = lambda i: jnp.minimum(i, n_blk - 1)
    return _call(
        body_and_transpose, name="fwd_ffn_in", grid=(s // ts,),
        in_specs=in_specs + [pl.BlockSpec((MXU_COLS, D_MODEL), lambda i: (blk(i), 0))],
        out_specs=out_specs + [pl.BlockSpec((D_MODEL, MXU_COLS), lambda i: (0, blk(i)))],
        out_shape=out_shape + [_sds((D_MODEL, D_FF), BF16)],
        scratch_shapes=[], args=(x2, g2, w_gate_t, w_up_t, w_down), exchange=exchange, forward_step=(s // ts) // 2)


def _ffn_out_and_back(g0, v, x2, wf, w_down, w_down_t, g3, target, ts):
    s = x2.shape[0]
    nt = s // ts
    rc = min(ROW_CHUNK, ts // 2)

    def body(g0_ref, gp_ref, gn_ref, v_ref, x2_ref, wf_ref, wd_ref, wdt_ref, g3_ref, t_ref,
             a_ref, dx3_ref, dx3b_ref, loss_ref, dg3_ref, dg_ref, dv_ref, dwf_ref,
             ge_ref, silu_ref, dsv_ref, da_ref, acc_ref, p_ref, sums_ref):
        i = pl.program_id(0)

        @pl.when(i == 0)
        def _():
            acc_ref[...] = jnp.zeros_like(acc_ref)
            sums_ref[...] = jnp.zeros_like(sums_ref)

        _fill_halo(ge_ref, gp_ref[...].astype(F32), gn_ref[...].astype(F32), ts, i == 0, i == nt - 1)

        def fill(j, carry):
            r0 = pl.multiple_of(j * rc, rc)
            _put_rows(ge_ref, r0, rc, g0_ref[pl.ds(r0, rc), :].astype(F32))
            return carry

        lax.fori_loop(0, ts // rc, fill, 0)

        def act(r0, l0):
            rows = slice(r0, r0 + rc)
            g = _conv_block(ge_ref, wf_ref, r0, rc, l0, K_F, False)
            vv = v_ref[rows, l0:l0 + LANES].astype(F32)
            sg = _sigmoid(g)
            silu = g * sg
            a_ref[rows, l0:l0 + LANES] = (silu * vv).astype(BF16)
            silu_ref[rows, l0:l0 + LANES] = silu
            dsv_ref[rows, l0:l0 + LANES] = (sg + silu * (1.0 - sg)) * vv

        def tail(r0):
            rows = slice(r0, r0 + rc)
            x3 = x2_ref[rows, :] + p_ref[rows, :]
            r = lax.rsqrt(jnp.mean(x3 * x3, axis=-1, keepdims=True) + RMS_EPS)
            xhat = x3 * r
            diff = xhat * g3_ref[...] - t_ref[rows, :]
            dout = diff * (1.0 / D_MODEL)
            dxh = dout * g3_ref[...]
            dx3 = r * (dxh - xhat * jnp.mean(dxh * xhat, axis=-1, keepdims=True))
            dx3_ref[rows, :] = dx3
            dx3b_ref[rows, :] = dx3.astype(BF16)
            sums_ref[0] = sums_ref[0] + _fold8(diff * diff)
            sums_ref[1] = sums_ref[1] + _fold8(dout * xhat)

        def back(r0, l0):
            rows = slice(r0, r0 + rc)
            da = da_ref[rows, l0:l0 + LANES]
            dv_ref[rows, l0:l0 + LANES] = (da * silu_ref[rows, l0:l0 + LANES]).astype(BF16)
            dgg = da * dsv_ref[rows, l0:l0 + LANES]
            dg_ref[rows, l0:l0 + LANES] = dgg.astype(BF16)
            _conv_wgrad_block(acc_ref, dgg, ge_ref, r0, rc, l0, K_F)

        hrows = ts // 2

        def units(fn, h, per_lane_block):
            starts = [h * hrows + q * rc for q in range(hrows // rc)]
            if per_lane_block:
                return [(fn, (r0, l0)) for r0 in starts for l0 in range(0, D_FF, LANES)]
            return [(fn, (r0,)) for r0 in starts]

        def rows_of(ref, h):
            return ref.at[pl.ds(h * hrows, hrows), :]

        def product(h):
            return _matmul_pieces([(rows_of(a_ref, h), wd_ref)], rows_of(p_ref, h), 2)

        def grad_a(h):
            return _matmul_pieces([(rows_of(dx3b_ref, h), wdt_ref)], rows_of(da_ref, h), 1)

        _interleaved(units(act, 0, True), [])
        _interleaved(units(act, 1, True), product(0))
        _interleaved(units(tail, 0, False), product(1))
        _interleaved(units(tail, 1, False), grad_a(0))
        _interleaved(units(back, 0, True), grad_a(1))
        _interleaved(units(back, 1, True), [])

        @pl.when(i == nt - 1)
        def _():
            _reduce_acc(dwf_ref, acc_ref, K_F)
            loss_ref[...] = (0.5 / D_MODEL) * jnp.sum(sums_ref[0], keepdims=True)
            dg3_ref[...] = jnp.sum(sums_ref[1], axis=0, keepdims=True)

    return pl.pallas_call(
        body, name="ffn_out_and_back", grid=(nt,),
        in_specs=[_rows(ts, D_FF), _prev(ts, D_FF), _next(ts, D_FF, s), _rows(ts, D_FF), _rows(ts, D_MODEL),
                  _const((K_F, D_FF)), _const((D_FF, D_MODEL)), _const((D_MODEL, D_FF)), _const((1, D_MODEL)),
                  _rows(ts, D_MODEL)],
        out_specs=[_rows(ts, D_FF), _rows(ts, D_MODEL), _rows(ts, D_MODEL), _acc_out((1, 1)), _acc_out((1, D_MODEL)),
                   _rows(ts, D_FF), _rows(ts, D_FF), _acc_out((K_F, D_FF))],
        out_shape=[_sds((s, D_FF), BF16), _sds((s, D_MODEL), F32), _sds((s, D_MODEL), BF16),
                   _sds((1, 1), F32), _sds((1, D_MODEL), F32),
                   _sds((s, D_FF), BF16), _sds((s, D_FF), BF16), _sds((K_F, D_FF), F32)],
        scratch_shapes=[_ext_scratch(ts, D_FF), pltpu.VMEM((ts, D_FF), F32), pltpu.VMEM((ts, D_FF), F32),
                        pltpu.VMEM((ts, D_FF), F32), pltpu.VMEM((K_F, SUBLANES, D_FF), F32),
                        pltpu.VMEM((ts, D_MODEL), F32), pltpu.VMEM((2, SUBLANES, D_MODEL), F32)],
        compiler_params=_params(),
    )(g0, g0, g0, v, x2, wf, w_down, w_down_t, g3, target)


def _bwd_ffn_b(dg, dv, wf, w_gate, w_up, x2, g2, dx3, ts, exchange=None):
    s = x2.shape[0]
    nt = s // ts
    rc = min(ROW_CHUNK, ts)
    n_sub = ts // rc

    def body(dg_ref, dgp_ref, dgn_ref, dv_ref, wf_ref, wg_ref, wu_ref, x2_ref, g2_ref, dx3_ref,
             dg0_ref, dx2_ref, dx2b_ref, dgn2_ref, dge_ref, dg8_ref, a0_ref, a1_ref, p0_ref, p1_ref):
        i = pl.program_id(0)
        vt = jnp.minimum(i, nt - 1)
        live = (i >= 2).astype(F32)

        @pl.when(i == 0)
        def _():
            dg8_ref[...] = jnp.zeros_like(dg8_ref)
            a1_ref[...] = jnp.zeros_like(a1_ref)
            p1_ref[...] = jnp.zeros_like(p1_ref)

        _fill_halo(dge_ref, dgp_ref[...].astype(F32), dgn_ref[...].astype(F32), ts, vt == 0, vt == nt - 1)

        def fill(j, carry):
            r0 = pl.multiple_of(j * rc, rc)
            _put_rows(dge_ref, r0, rc, dg_ref[pl.ds(r0, rc), :].astype(F32))
            return carry

        lax.fori_loop(0, n_sub, fill, 0)

        def stage(a_new, a_old, p_new, p_old):
            def conv_t(r0, l0):
                rows = slice(r0, r0 + rc)
                dg0 = _conv_block(dge_ref, wf_ref, r0, rc, l0, K_F, True).astype(BF16)
                dg0_ref[rows, l0:l0 + LANES] = dg0
                a_new[rows, l0:l0 + LANES] = dg0

            def tail(r0):
                rows = slice(r0, r0 + rc)
                dx, dgrow = _rms_bwd_rows(p_old[rows, :], x2_ref[rows, :], g2_ref[...])
                dx2 = dx3_ref[rows, :] + dx
                dx2_ref[rows, :] = dx2
                dx2b_ref[rows, :] = dx2.astype(BF16)
                dg8_ref[...] = dg8_ref[...] + _fold8(dgrow) * live

            units = []
            for q in range(n_sub):
                units += [(conv_t, (q * rc, l0)) for l0 in range(0, D_FF, LANES)]
                units.append((tail, (q * rc,)))
            _interleaved(units, _matmul_pieces([(a_old, wg_ref), (dv_ref, wu_ref)], p_new, 2))

        @pl.when(i % 2 == 0)
        def _():
            stage(a0_ref, a1_ref, p0_ref, p1_ref)

        @pl.when(i % 2 == 1)
        def _():
            stage(a1_ref, a0_ref, p1_ref, p0_ref)

        @pl.when(i == nt + 1)
        def _():
            dgn2_ref[...] = jnp.sum(dg8_ref[...], axis=0, keepdims=True)

    vtile = lambda i: jnp.minimum(i, nt - 1)
    mtile = lambda i: jnp.clip(i - 1, 0, nt - 1)
    ttile = lambda i: jnp.clip(i - 2, 0, nt - 1)
    return _call(
        body, name="bwd_ffn_b", grid=(nt + 2,),
        in_specs=[_rows_at(ts, D_FF, vtile), _prev_at(ts, D_FF, vtile), _next_at(ts, D_FF, s, vtile),
                  _rows_at(ts, D_FF, mtile), _const((K_F, D_FF)),
                  _const((D_FF, D_MODEL)), _const((D_FF, D_MODEL)), _rows_at(ts, D_MODEL, ttile), _const((1, D_MODEL)),
                  _rows_at(ts, D_MODEL, ttile)],
        out_specs=[_rows_at(ts, D_FF, vtile), _rows_at(ts, D_MODEL, ttile), _rows_at(ts, D_MODEL, ttile),
                   _acc_out((1, D_MODEL))],
        out_shape=[_sds((s, D_FF), BF16), _sds((s, D_MODEL), F32), _sds((s, D_MODEL), BF16), _sds((1, D_MODEL), F32)],
        scratch_shapes=[_ext_scratch(ts, D_FF), pltpu.VMEM((SUBLANES, D_MODEL), F32),
                        pltpu.VMEM((ts, D_FF), BF16), pltpu.VMEM((ts, D_FF), BF16),
                        pltpu.VMEM((ts, D_MODEL), F32), pltpu.VMEM((ts, D_MODEL), F32)],
        args=(dg, dg, dg, dv, wf, w_gate, w_up, x2, g2, dx3), exchange=exchange)


def _bwd_mix_a(dx2b, w_out_t, z, u, wa, lg, lb, ts, exchange=None):
    s = dx2b.shape[0]
    nt = s // ts
    rc = min(ROW_CHUNK, ts)
    n_sub = ts // rc

    def body(dx_ref, wo_ref, z_ref, zp_ref, zn_ref, u_ref, wa_ref, lg_ref, lb_ref,
             dca_ref, du_ref, dab_ref, dwa_ref, dlg_ref, dlb_ref, dbb_ref, pe_ref, dy0_ref, dy1_ref, acc_ref, sacc_ref):
        i = pl.program_id(0)
        t = jnp.maximum(i - 1, 0)

        @pl.when(i == 0)
        def _():
            acc_ref[...] = jnp.zeros_like(acc_ref)
            sacc_ref[...] = jnp.zeros_like(sacc_ref)
            dy1_ref[...] = jnp.zeros_like(dy1_ref)

        pp, _ = _p_u0(zp_ref, slice(None))
        pn, _ = _p_u0(zn_ref, slice(None))
        _fill_halo(pe_ref, pp, pn, ts, t == 0, t == nt - 1)

        def fill(j, carry):
            r0 = pl.multiple_of(j * rc, rc)
            rows = pl.ds(r0, rc)
            _put_rows(pe_ref, r0, rc, z_ref[rows, 2 * D_A:3 * D_A].astype(F32) * z_ref[rows, 0:D_A].astype(F32))
            return carry

        lax.fori_loop(0, n_sub, fill, 0)

        def stage(dy_new, dy_old):
            def piece(m0, n0, width):
                dy_new[m0:m0 + MXU_ROWS, n0:n0 + width] = jnp.dot(
                    dx_ref[m0:m0 + MXU_ROWS, :], wo_ref[:, n0:n0 + width], preferred_element_type=F32).astype(BF16)

            units = []
            for q in range(n_sub):
                units += [(mixer_a, (dy_old, q * rc, l0)) for l0 in range(0, D_A, LANES)]
                units.append((mixer_b, (dy_old, q * rc)))
            _interleaved(units, [(piece, (m0, n0, w)) for n0, w in _col_pieces(D_MODEL) for m0 in range(0, ts, MXU_ROWS)])

        def mixer_a(dy_ref, r0, l0):
            rows = slice(r0, r0 + rc)
            ca = _conv_block(pe_ref, wa_ref, r0, rc, l0, K_A, False)
            a_b = z_ref[rows, D_A + l0:D_A + l0 + LANES].astype(F32)
            dya = dy_ref[rows, l0:l0 + LANES].astype(F32)
            dab_ref[rows, l0:l0 + LANES] = (dya * ca).astype(BF16)
            dca = dya * a_b
            dca_ref[rows, l0:l0 + LANES] = dca
            _conv_wgrad_block(acc_ref, dca, pe_ref, r0, rc, l0, K_A)

        def mixer_b(dy_ref, r0):
            rows = slice(r0, r0 + rc)
            ubs = [u_ref[rows, l0:l0 + LANES] for l0 in range(0, D_B, LANES)]
            mu, rstd = _layernorm_rows(ubs)
            ns, dns = [], []
            m1 = None
            m2 = None
            for q, l0 in enumerate(range(0, D_B, LANES)):
                n = (ubs[q] - mu) * rstd
                lgq = lg_ref[:, l0:l0 + LANES]
                t = n * lgq + lb_ref[:, l0:l0 + LANES]
                sg = _sigmoid(t)
                dt = dy_ref[rows, D_A + l0:D_A + l0 + LANES].astype(F32) * (sg * (1.0 + t * (1.0 - sg)))
                dn = dt * lgq
                ns.append(n)
                dns.append(dn)
                s1 = jnp.sum(dn, axis=-1, keepdims=True)
                s2 = jnp.sum(dn * n, axis=-1, keepdims=True)
                m1 = s1 if m1 is None else m1 + s1
                m2 = s2 if m2 is None else m2 + s2
                sacc_ref[0, :, l0:l0 + LANES] = sacc_ref[0, :, l0:l0 + LANES] + _fold8(dt * n)
                sacc_ref[1, :, l0:l0 + LANES] = sacc_ref[1, :, l0:l0 + LANES] + _fold8(dt)
            m1 = m1 * (1.0 / D_B)
            m2 = m2 * (1.0 / D_B)
            for q, l0 in enumerate(range(0, D_B, LANES)):
                du = rstd * (dns[q] - m1 - ns[q] * m2)
                du_ref[rows, l0:l0 + LANES] = du
                sacc_ref[2, :, l0:l0 + LANES] = sacc_ref[2, :, l0:l0 + LANES] + _fold8(du)

        @pl.when(i % 2 == 0)
        def _():
            stage(dy0_ref, dy1_ref)

        @pl.when(i % 2 == 1)
        def _():
            stage(dy1_ref, dy0_ref)

        @pl.when(i == nt)
        def _():
            _reduce_acc(dwa_ref, acc_ref, K_A)
            dlg_ref[...] = jnp.sum(sacc_ref[0], axis=0, keepdims=True)
            dlb_ref[...] = jnp.sum(sacc_ref[1], axis=0, keepdims=True)
            dbb_ref[...] = jnp.sum(sacc_ref[2], axis=0, keepdims=True)

    cur = lambda i: jnp.minimum(i, nt - 1)
    old = lambda i: jnp.maximum(i - 1, 0)
    return _call(
        body, name="bwd_mix_a", grid=(nt + 1,),
        in_specs=[_rows_at(ts, D_MODEL, cur), _const((D_MODEL, D_MODEL)), _rows_at(ts, D_IN, old), _prev_at(ts, D_IN, old),
                  _next_at(ts, D_IN, s, old), _rows_at(ts, D_B, old), _const((K_A, D_A)), _const((1, D_B)), _const((1, D_B))],
        out_specs=[_rows_at(ts, D_A, old), _rows_at(ts, D_B, old), _rows_at(ts, D_A, old), _acc_out((K_A, D_A)),
                   _acc_out((1, D_B)), _acc_out((1, D_B)), _acc_out((1, D_B))],
        out_shape=[_sds((s, D_A), F32), _sds((s, D_B), F32), _sds((s, D_A), BF16), _sds((K_A, D_A), F32),
                   _sds((1, D_B), F32), _sds((1, D_B), F32), _sds((1, D_B), F32)],
        scratch_shapes=[_ext_scratch(ts, D_A), pltpu.VMEM((ts, D_MODEL), BF16), pltpu.VMEM((ts, D_MODEL), BF16),
                        pltpu.VMEM((K_A, SUBLANES, D_A), F32), pltpu.VMEM((3, SUBLANES, D_B), F32)],
        args=(dx2b, w_out_t, z, z, z, u, wa, lg, lb), exchange=exchange)


def _bwd_mix_b(dca, du, z, dab, wa, wb, w_in, x, g1, dx2, ts, exchange=None):
    s = x.shape[0]
    nt = s // ts
    rc = min(ROW_CHUNK, ts)
    n_sub = ts // rc

    def body(dca_ref, dcap_ref, dcan_ref, du_ref, dup_ref, dun_ref, z_ref, zp_ref, zn_ref, dab_ref,
             wa_ref, wb_ref, wi_ref, x_ref, g1_ref, dx2_ref,
             dz_ref, dx_ref, dg1_ref, dwb_ref, dcae_ref, due_ref, ue_ref, acc_ref, dg8_ref,
             dz0_ref, dz1_ref, dh0_ref, dh1_ref):
        i = pl.program_id(0)

        @pl.when(i == 0)
        def _():
            acc_ref[...] = jnp.zeros_like(acc_ref)
            dg8_ref[...] = jnp.zeros_like(dg8_ref)
            dz1_ref[...] = jnp.zeros_like(dz1_ref)
            dh1_ref[...] = jnp.zeros_like(dh1_ref)

        vt = jnp.minimum(i, nt - 1)
        first = vt == 0
        last = vt == nt - 1
        live = (i < nt).astype(F32)
        _fill_halo(dcae_ref, dcap_ref[...], dcan_ref[...], ts, first, last)
        _fill_halo(due_ref, dup_ref[...], dun_ref[...], ts, first, last)
        _, up = _p_u0(zp_ref, slice(None))
        _, un = _p_u0(zn_ref, slice(None))
        _fill_halo(ue_ref, up, un, ts, first, last)

        def fill(j, carry):
            r0 = pl.multiple_of(j * rc, rc)
            rows = pl.ds(r0, rc)
            _put_rows(dcae_ref, r0, rc, dca_ref[rows, :])
            _put_rows(due_ref, r0, rc, du_ref[rows, :])
            b_v = z_ref[rows, 3 * D_A:3 * D_A + D_B].astype(F32)
            b_g = z_ref[rows, 3 * D_A + D_B:D_IN].astype(F32)
            _put_rows(ue_ref, r0, rc, b_v * _sigmoid(b_g))
            return carry

        lax.fori_loop(0, n_sub, fill, 0)

        def stage(dz_new, dz_old, dh_new, dh_old):
            def put(rows, c0, val):
                dz_ref[rows, c0:c0 + LANES] = val
                dz_new[rows, c0:c0 + LANES] = val

            def mixer_a(r0, l0):
                rows = slice(r0, r0 + rc)
                dp = _conv_block(dcae_ref, wa_ref, r0, rc, l0, K_A, True)
                a_h = z_ref[rows, l0:l0 + LANES].astype(F32)
                a_c = z_ref[rows, 2 * D_A + l0:2 * D_A + l0 + LANES].astype(F32)
                put(rows, l0, (dp * a_c).astype(BF16))
                put(rows, D_A + l0, dab_ref[rows, l0:l0 + LANES])
                put(rows, 2 * D_A + l0, (dp * a_h).astype(BF16))

            def mixer_b(r0, l0):
                rows = slice(r0, r0 + rc)
                du0 = _conv_block(due_ref, wb_ref, r0, rc, l0, K_B, True)
                b_v = z_ref[rows, 3 * D_A + l0:3 * D_A + l0 + LANES].astype(F32)
                b_g = z_ref[rows, 3 * D_A + D_B + l0:3 * D_A + D_B + l0 + LANES].astype(F32)
                sg = _sigmoid(b_g)
                put(rows, 3 * D_A + l0, (du0 * sg).astype(BF16))
                put(rows, 3 * D_A + D_B + l0, (du0 * b_v * (sg * (1.0 - sg))).astype(BF16))
                _conv_wgrad_block(acc_ref, du_ref[rows, l0:l0 + LANES], ue_ref, r0, rc, l0, K_B, live)

            def tail(r0):
                rows = slice(r0, r0 + rc)
                dx, dgrow = _rms_bwd_rows(dh_old[rows, :], x_ref[rows, :], g1_ref[...])
                dx_ref[rows, :] = dx2_ref[rows, :] + dx
                dg8_ref[...] = dg8_ref[...] + _fold8(dgrow)

            units = []
            for q in range(n_sub):
                units += [(mixer_a, (q * rc, l0)) for l0 in range(0, D_A, LANES)]
                units += [(mixer_b, (q * rc, l0)) for l0 in range(0, D_B, LANES)]
                units.append((tail, (q * rc,)))
            _interleaved(units, _matmul_pieces([(dz_old, wi_ref)], dh_new, 2))

        @pl.when(i % 2 == 0)
        def _():
            stage(dz0_ref, dz1_ref, dh0_ref, dh1_ref)

        @pl.when(i % 2 == 1)
        def _():
            stage(dz1_ref, dz0_ref, dh1_ref, dh0_ref)

        @pl.when(i == nt + 1)
        def _():
            _reduce_acc(dwb_ref, acc_ref, K_B)
            dg1_ref[...] = jnp.sum(dg8_ref[...], axis=0, keepdims=True)

    vtile = lambda i: jnp.minimum(i, nt - 1)
    ttile = lambda i: jnp.clip(i - 2, 0, nt - 1)
    return _call(
        body, name="bwd_mix_b", grid=(nt + 2,),
        in_specs=[_rows_at(ts, D_A, vtile), _prev_at(ts, D_A, vtile), _next_at(ts, D_A, s, vtile),
                  _rows_at(ts, D_B, vtile), _prev_at(ts, D_B, vtile), _next_at(ts, D_B, s, vtile),
                  _rows_at(ts, D_IN, vtile), _prev_at(ts, D_IN, vtile), _next_at(ts, D_IN, s, vtile), _rows_at(ts, D_A, vtile),
                  _const((K_A, D_A)), _const((K_B, D_B)), _const((D_IN, D_MODEL)), _rows_at(ts, D_MODEL, ttile),
                  _const((1, D_MODEL)), _rows_at(ts, D_MODEL, ttile)],
        out_specs=[_rows_at(ts, D_IN, vtile), _rows_at(ts, D_MODEL, ttile), _acc_out((1, D_MODEL)), _acc_out((K_B, D_B))],
        out_shape=[_sds((s, D_IN), BF16), _sds((s, D_MODEL), F32), _sds((1, D_MODEL), F32), _sds((K_B, D_B), F32)],
        scratch_shapes=[_ext_scratch(ts, D_A), _ext_scratch(ts, D_B), _ext_scratch(ts, D_B),
                        pltpu.VMEM((K_B, SUBLANES, D_B), F32), pltpu.VMEM((SUBLANES, D_MODEL), F32),
                        pltpu.VMEM((ts, D_IN), BF16), pltpu.VMEM((ts, D_IN), BF16),
                        pltpu.VMEM((ts, D_MODEL), F32), pltpu.VMEM((ts, D_MODEL), F32)],
        args=(dca, dca, dca, du, du, du, z, z, z, dab, wa, wb, w_in, x, g1, dx2), exchange=exchange)


def _matmul_tn(a, b, name, exchange=None):
    s, m = a.shape
    n = b.shape[1]
    tk = min(1024, s)
    nk = s // tk
    tm = 256

    def body(a_ref, b_ref, o_ref, acc_ref):
        k = pl.program_id(0)

        @pl.when(k == 0)
        def _():
            acc_ref[...] = jnp.zeros_like(acc_ref)

        for m0 in range(0, m, tm):
            acc_ref[m0:m0 + tm, :] = acc_ref[m0:m0 + tm, :] + lax.dot_general(
                a_ref[:, m0:m0 + tm], b_ref[...], _TN, preferred_element_type=F32)

        @pl.when(k == nk - 1)
        def _():
            o_ref[...] = acc_ref[...].astype(BF16)

    (out,), got = _call(
        body, name=name, grid=(nk,),
        in_specs=[_rows(tk, m), _rows(tk, n)],
        out_specs=[_acc_out((m, n))],
        out_shape=[_sds((m, n), BF16)],
        scratch_shapes=[pltpu.VMEM((m, n), F32)], args=(a, b), exchange=exchange)
    return out if exchange is None else (out, got)


CHIP_RELS = ((1, 0, 0), (0, 1, 0), (1, 1, 0))
CORE_RELS = ((0, 0, 1),)
ALL_RELS = ((0, 0, 1), (0, 1, 0), (0, 1, 1), (1, 0, 0), (1, 0, 1), (1, 1, 0), (1, 1, 1))


def _chip_slot(dev):
    return 2 * dev[0] + dev[1]


def _dev_slot(dev):
    return 4 * dev[0] + 2 * dev[1] + dev[2]


def _me():
    return (lax.axis_index("x"), lax.axis_index("y"), lax.axis_index("c"))


def _peer(me, rel):
    return tuple((1 - me[a]) if rel[a] else me[a] for a in range(3))


_ANY = pl.BlockSpec(memory_space=pl.ANY)


class _Exchange:
    def __init__(self, inputs, out_shape, scratch, start, finish, forward=None):
        self.inputs, self.out_shape, self.scratch = list(inputs), list(out_shape), list(scratch)
        self.start, self.finish, self.forward = start, finish, forward


def _all_gather(payloads):
    n_p = len(payloads)
    n_k = 1 + 2 * len(CHIP_RELS)

    def copy(srcs, dsts, sems, p, k, block_dev, to, from_src):
        blk = dsts[p].at[_dev_slot(block_dev)]
        return pltpu.make_async_remote_copy(
            src_ref=srcs[p] if from_src else blk, dst_ref=blk,
            send_sem=sems[0].at[n_k * p + k], recv_sem=sems[1].at[n_k * p + k], device_id=to, device_id_type=MESH)

    def own_copy(srcs, dsts, sems, p):
        return pltpu.make_async_copy(srcs[p], dsts[p].at[_dev_slot(_me())], sems[2].at[p])

    def start(srcs, dsts, sems):
        me = _me()
        for p in range(n_p):
            own_copy(srcs, dsts, sems, p).start()
        for j, rel in enumerate(CHIP_RELS):
            for p in range(n_p):
                copy(srcs, dsts, sems, p, 1 + j, me, _peer(me, rel), True).start()
        for p in range(n_p):
            copy(srcs, dsts, sems, p, 0, me, _peer(me, CORE_RELS[0]), True).start()

    def forward(srcs, dsts, sems):
        me = _me()
        sibling = _peer(me, CORE_RELS[0])
        for j, rel in enumerate(CHIP_RELS):
            other = _peer(me, rel)
            for p in range(n_p):
                copy(srcs, dsts, sems, p, 1 + j, other, me, False).wait_recv()
                copy(srcs, dsts, sems, p, 4 + j, other, sibling, False).start()

    def finish(srcs, dsts, sems):
        me = _me()
        sibling = _peer(me, CORE_RELS[0])
        for p in range(n_p):
            copy(srcs, dsts, sems, p, 0, sibling, me, False).wait_recv()
        for j, rel in enumerate(CHIP_RELS):
            for p in range(n_p):
                copy(srcs, dsts, sems, p, 4 + j, _peer(sibling, rel), me, False).wait_recv()
        for p in range(n_p):
            own_copy(srcs, dsts, sems, p).wait()
            copy(srcs, dsts, sems, p, 0, me, sibling, True).wait_send()
            for j, rel in enumerate(CHIP_RELS):
                copy(srcs, dsts, sems, p, 1 + j, me, _peer(me, rel), True).wait_send()
                copy(srcs, dsts, sems, p, 4 + j, _peer(me, rel), sibling, False).wait_send()

    return _Exchange(
        payloads, [_sds((N_DEV,) + p.shape, p.dtype) for p in payloads],
        [pltpu.SemaphoreType.DMA((n_p * n_k,)), pltpu.SemaphoreType.DMA((n_p * n_k,)), pltpu.SemaphoreType.DMA((n_p,))],
        start, finish, forward)


def _gather_direct(payload):
    n_r = len(ALL_RELS)

    def copies(srcs, dsts, sems):
        me = _me()
        mine = dsts[0].at[_dev_slot(me)]
        own = pltpu.make_async_copy(srcs[0], mine, sems[2].at[0])
        remote = [pltpu.make_async_remote_copy(src_ref=srcs[0], dst_ref=mine, send_sem=sems[0].at[k], recv_sem=sems[1].at[k],
                                               device_id=_peer(me, rel), device_id_type=MESH)
                  for k, rel in enumerate(ALL_RELS)]
        return [own] + remote

    def start(srcs, dsts, sems):
        for cp in copies(srcs, dsts, sems):
            cp.start()

    def finish(srcs, dsts, sems):
        for cp in copies(srcs, dsts, sems):
            cp.wait()

    return _Exchange([payload], [_sds((N_DEV,) + payload.shape, payload.dtype)],
                     [pltpu.SemaphoreType.DMA((n_r,)), pltpu.SemaphoreType.DMA((n_r,)), pltpu.SemaphoreType.DMA((1,))],
                     start, finish)


def _scatter_exchange(payloads, rels, src_view, view_shapes):
    n_p = len(payloads)
    n_r = len(rels)

    def copies(srcs, dsts, sems):
        me = _me()
        out = []
        for k, rel in enumerate(rels):
            peer = _peer(me, rel)
            for p in range(n_p):
                out.append(pltpu.make_async_remote_copy(
                    src_ref=src_view(srcs[p], peer), dst_ref=dsts[p].at[k],
                    send_sem=sems[0].at[p * n_r + k], recv_sem=sems[1].at[p * n_r + k],
                    device_id=peer, device_id_type=MESH))
        return out

    def start(srcs, dsts, sems):
        for cp in copies(srcs, dsts, sems):
            cp.start()

    def finish(srcs, dsts, sems):
        for cp in copies(srcs, dsts, sems):
            cp.wait()

    return _Exchange(payloads, [_sds((n_r,) + vs, p.dtype) for vs, p in zip(view_shapes, payloads)],
                     [pltpu.SemaphoreType.DMA((n_p * n_r,)), pltpu.SemaphoreType.DMA((n_p * n_r,))], start, finish)


def _split_refs(refs, sizes):
    out, at = [], 0
    for n in sizes:
        out.append(refs[at:at + n])
        at += n
    return out


def _join(exchanges):
    n_in = [len(e.inputs) for e in exchanges]
    n_out = [len(e.out_shape) for e in exchanges]
    n_sc = [len(e.scratch) for e in exchanges]

    def phase(name):
        def run(ins, outs, scs):
            for e, i, o, s in zip(exchanges, _split_refs(ins, n_in), _split_refs(outs, n_out), _split_refs(scs, n_sc)):
                if getattr(e, name) is not None:
                    getattr(e, name)(i, o, s)
        return run

    return _Exchange([a for e in exchanges for a in e.inputs], [s for e in exchanges for s in e.out_shape],
                     [s for e in exchanges for s in e.scratch], phase("start"), phase("finish"),
                     phase("forward") if any(e.forward is not None for e in exchanges) else None)


def _run_exchanges(name, exchanges):
    n_in = [len(e.inputs) for e in exchanges]
    n_out = [len(e.out_shape) for e in exchanges]
    n_sc = [len(e.scratch) for e in exchanges]

    def body(*refs):
        ins, outs, scs = _split_refs(refs, [sum(n_in), sum(n_out), sum(n_sc)])
        parts = list(zip(exchanges, _split_refs(ins, n_in), _split_refs(outs, n_out), _split_refs(scs, n_sc)))
        for e, i, o, s in parts:
            e.start(i, o, s)
        for e, i, o, s in parts:
            if e.forward is not None:
                e.forward(i, o, s)
        for e, i, o, s in parts:
            e.finish(i, o, s)

    outs = pl.pallas_call(
        body, name=name, in_specs=[_ANY] * sum(n_in), out_specs=[_ANY] * sum(n_out),
        out_shape=[sd for e in exchanges for sd in e.out_shape],
        scratch_shapes=[sc for e in exchanges for sc in e.scratch],
    )(*[a for e in exchanges for a in e.inputs])
    return _split_refs(list(outs), n_out)


def _call(body, *, name, grid, in_specs, out_specs, out_shape, scratch_shapes, args, exchange=None, forward_step=None):
    n_in, n_out, n_sc = len(in_specs), len(out_specs), len(scratch_shapes)
    if exchange is None:
        outs = pl.pallas_call(body, name=name, grid=grid, in_specs=in_specs, out_specs=out_specs, out_shape=out_shape,
                              scratch_shapes=scratch_shapes, compiler_params=_params())(*args)
        return list(outs), []
    e = exchange
    sizes = [n_in, len(e.inputs), n_out, len(e.out_shape), n_sc, len(e.scratch)]
    last = grid[0] - 1

    def wrapped(*refs):
        a, ei, o, eo, sc, es = _split_refs(refs, sizes)
        i = pl.program_id(0)

        @pl.when(i == 0)
        def _():
            e.start(ei, eo, es)

        if e.forward is not None:
            @pl.when(i == forward_step)
            def _():
                e.forward(ei, eo, es)

        body(*a, *o, *sc)

        @pl.when(i == last)
        def _():
            e.finish(ei, eo, es)

    outs = pl.pallas_call(
        wrapped, name=name, grid=grid,
        in_specs=list(in_specs) + [_ANY] * len(e.inputs), out_specs=list(out_specs) + [_ANY] * len(e.out_shape),
        out_shape=list(out_shape) + e.out_shape, scratch_shapes=list(scratch_shapes) + e.scratch,
        compiler_params=_params(),
    )(*args, *e.inputs)
    outs = list(outs)
    return outs[:n_out], outs[n_out:]


def _pair_sum(grads, recvd, my_core, name):
    n_p = len(grads)

    def body(c_ref, *refs):
        del c_ref
        for p in range(n_p):
            refs[2 * n_p + p][...] = (refs[p][...].astype(F32) + refs[n_p + p][...].astype(F32)).astype(BF16)

    def blk(g):
        return (None, None) + g.shape[2:]

    return pl.pallas_call(
        body, name=name,
        grid_spec=pltpu.PrefetchScalarGridSpec(
            num_scalar_prefetch=1, grid=(N_CHIP,),
            in_specs=[pl.BlockSpec(blk(g), lambda j, c: (j, c[0], 0, 0)) for g in grads]
            + [pl.BlockSpec(blk(g), lambda j, c: (0, j, 0, 0)) for g in grads],
            out_specs=[pl.BlockSpec((None,) + g.shape[2:], lambda j, c: (j, 0, 0)) for g in grads]),
        out_shape=[_sds((N_CHIP,) + g.shape[2:], BF16) for g in grads],
        compiler_params=pltpu.CompilerParams(dimension_semantics=("arbitrary",), vmem_limit_bytes=VMEM_LIMIT),
    )(my_core, *grads, *recvd)


def _chip_sum(psums, recvd, my_chip, name):
    n_p = len(psums)

    def body(c_ref, *refs):
        del c_ref
        for p in range(n_p):
            acc = refs[p][...].astype(F32)
            for k in range(len(CHIP_RELS)):
                acc = acc + refs[n_p + p][k].astype(F32)
            refs[2 * n_p + p][...] = acc

    return pl.pallas_call(
        body, name=name,
        grid_spec=pltpu.PrefetchScalarGridSpec(
            num_scalar_prefetch=1, grid=(1,),
            in_specs=[pl.BlockSpec((None,) + g.shape[1:], lambda i, c: (c[0], 0, 0)) for g in psums]
            + [pl.BlockSpec(r.shape, lambda i, c: (0, 0, 0)) for r in recvd],
            out_specs=[pl.BlockSpec(g.shape[1:], lambda i, c: (0, 0)) for g in psums]),
        out_shape=[_sds(g.shape[1:], F32) for g in psums],
        compiler_params=pltpu.CompilerParams(dimension_semantics=("arbitrary",), vmem_limit_bytes=VMEM_LIMIT),
    )(my_chip, *psums, *recvd)


def _sum_devices(parts, shapes):
    plan = _small_rows(shapes)

    def body(p_ref, *refs):
        outs, tot_ref = refs[:-1], refs[-1]
        acc = p_ref[0]
        for j in range(1, N_DEV):
            acc = acc + p_ref[j]
        tot_ref[...] = acc
        for idx, r, c0, width, at in plan:
            outs[idx][r:r + 1, c0:c0 + width] = tot_ref[at:at + 1, 0:width]

    return pl.pallas_call(body, name="small_grad_sum", out_shape=[_sds(s, F32) for s in shapes],
                          scratch_shapes=[pltpu.VMEM(parts.shape[1:], F32)])(parts)


def _transposed(w):
    r, c = w.shape
    tr = MXU_COLS

    def body(w_ref, o_ref):
        o_ref[...] = w_ref[...].T

    return pl.pallas_call(
        body, name="transpose_w_down", grid=(r // tr,),
        in_specs=[pl.BlockSpec((tr, c), lambda i: (i, 0))], out_specs=pl.BlockSpec((c, tr), lambda i: (0, i)),
        out_shape=_sds((c, r), w.dtype), compiler_params=_params())(w)


def _cast_shards(shards):
    def body(*refs):
        for src, dst in zip(refs[:len(shards)], refs[len(shards):]):
            dst[...] = src[...].astype(BF16)

    return pl.pallas_call(body, name="cast_shards", out_shape=[_sds(a.shape, BF16) for a in shards],
                          compiler_params=pltpu.CompilerParams(vmem_limit_bytes=VMEM_LIMIT))(*shards)


def _adamw(ws, gs, ms, vs, name):
    n_t = len(ws)

    def body(*refs):
        w_refs, g_refs, m_refs, v_refs = (refs[j * n_t:(j + 1) * n_t] for j in range(4))
        outs = refs[4 * n_t:]
        for k in range(n_t):
            gg = g_refs[k][...]
            mn = ADAM_B1 * m_refs[k][...] + (1.0 - ADAM_B1) * gg
            vn = ADAM_B2 * v_refs[k][...] + (1.0 - ADAM_B2) * (gg * gg)
            m_hat = mn / (1.0 - ADAM_B1 ** ADAM_STEP)
            v_hat = vn / (1.0 - ADAM_B2 ** ADAM_STEP)
            outs[3 * k][...] = -ADAM_LR * (m_hat / (jnp.sqrt(v_hat) + ADAM_EPS) + ADAM_WD * w_refs[k][...])
            outs[3 * k + 1][...] = mn
            outs[3 * k + 2][...] = vn

    out_shape = [_sds(w.shape, F32) for w in ws for _ in range(3)]
    return pl.pallas_call(body, name=name, out_shape=out_shape,
                          compiler_params=pltpu.CompilerParams(vmem_limit_bytes=VMEM_LIMIT))(*ws, *gs, *ms, *vs)


class _Mesh:
    def __init__(self, shards, my_chip, my_core):
        self.shards, self.my_chip, self.my_core = shards, my_chip.reshape(1), my_core.reshape(1)

    def gather(self, names):
        return _all_gather([self.shards[n] for n in names])

    @staticmethod
    def whole(gathered):
        return gathered.reshape(N_DEV * gathered.shape[1], gathered.shape[2])

    @staticmethod
    def by_device(grads):
        return [g.reshape(N_CHIP, 2, g.shape[0] // N_DEV, g.shape[1]) for g in grads]

    @staticmethod
    def to_sibling(parts):
        return _scatter_exchange(parts, CORE_RELS, lambda ref, peer: ref.at[:, peer[2]],
                                 [(N_CHIP,) + p.shape[2:] for p in parts])

    @staticmethod
    def to_chips(pair):
        return _scatter_exchange(pair, CHIP_RELS, lambda ref, peer: ref.at[_chip_slot(peer)], [p.shape[1:] for p in pair])


def _step(x, target, g1, w_in_t, wa, wb, bb, lg, lb, w_out, g2, w_gate_t, w_up_t, wf, w_down, g3, ts, mesh=None):
    (z, h1), got = _fwd_in(x, g1, w_in_t, ts, exchange=mesh and mesh.gather(["w_out", "w_down"]))
    if mesh:
        w_out, w_down = [mesh.whole(g) for g in got]
    (x2, y, u), got = _fwd_mix(z, x, wa, wb, bb, lg, lb, w_out, ts, exchange=mesh and mesh.gather(["w_gate", "w_up"]))
    if mesh:
        w_gate_t, w_up_t = [mesh.whole(g) for g in got]
    (g0, v, h2, w_down_t), _ = _fwd_ffn_in(x2, g2, w_gate_t, w_up_t, ts, w_down=w_down)
    tk = min(ts, SKEW_TILE)
    a, dx3, dx3b, loss, dg3, dgc, dv, dwf = _ffn_out_and_back(g0, v, x2, wf, w_down, w_down_t, g3, target, tk)
    one = dict(w_down=_matmul_tn(a, dx3b, "wgrad_down"), w_up=_matmul_tn(dv, h2, "wgrad_up"))
    parts1 = mesh and mesh.by_device(list(one.values()))
    (dg0, dx2, dx2b, dg2), got = _bwd_ffn_b(dgc, dv, wf, w_gate_t, w_up_t, x2, g2, dx3, tk,
                                            exchange=mesh and mesh.to_sibling(parts1))
    pair1 = mesh and _pair_sum(parts1, got, mesh.my_core, "rs_pair_sum_1")
    two = dict(w_gate=_matmul_tn(dg0, h2, "wgrad_gate"), w_out=_matmul_tn(y, dx2b, "wgrad_out"))
    parts2 = mesh and mesh.by_device(list(two.values()))
    (dca, du, dab, dwa, dlg, dlb, dbb), got = _bwd_mix_a(
        dx2b, jnp.swapaxes(w_out, 0, 1), z, u, wa, lg, lb, tk,
        exchange=mesh and _join([mesh.to_chips(pair1), mesh.to_sibling(parts2)]))
    (dz, dx, dg1, dwb), _ = _bwd_mix_b(dca, du, z, dab, wa, wb, w_in_t, x, g1, dx2, tk)
    small = dict(norm_mix_g=dg1, conv_a_w=dwa, conv_b_w=dwb, conv_b_b=dbb, ln_b_g=dlg, ln_b_b=dlb,
                 norm_ffn_g=dg2, conv_ffn_w=dwf, norm_final_g=dg3)
    if not mesh:
        return loss, dx, dict(w_in=_matmul_tn(dz, h1, "wgrad_in"), **one, **two), small
    big = dict(zip(one, _chip_sum(pair1, got[:2], mesh.my_chip, "rs_chip_sum_1")))
    pair2 = _pair_sum(parts2, got[2:], mesh.my_core, "rs_pair_sum_2")
    dw_in_t, got = _matmul_tn(dz, h1, "wgrad_in",
                              exchange=_join([mesh.to_chips(pair2), _gather_direct(_pack_small_grads(small, loss))]))
    big.update(zip(two, _chip_sum(pair2, got[:2], mesh.my_chip, "rs_chip_sum_2")))
    every = got[2]
    parts = mesh.by_device([dw_in_t])
    (got,) = _run_exchanges("rs_cores_last", [mesh.to_sibling(parts)])
    pair = _pair_sum(parts, got, mesh.my_core, "rs_pair_sum_last")
    (got,) = _run_exchanges("rs_chips_last", [mesh.to_chips(pair)])
    (big["w_in"],) = _chip_sum(pair, got, mesh.my_chip, "rs_chip_sum_last")
    shapes = [loss.shape if n == "loss" else small[n].shape for n in _SMALL_NAMES]
    return None, dx, big, dict(zip(_SMALL_NAMES, _sum_devices(every, shapes)))


def _pack_small_weights(conv_a_s, conv_b_s, conv_ffn_s):
    def body(a_ref, b_ref, f_ref, out):
        out[...] = jnp.zeros_like(out)
        at = 0
        for src in (a_ref, b_ref, f_ref):
            rows, cols = src.shape
            for r in range(rows):
                out[at:at + 1, 0:cols] = src[r:r + 1, :]
                at += 1

    return pl.pallas_call(body, name="small_weight_pack", out_shape=_sds((SMALL_W_ROWS, SMALL_W_COLS), F32))(
        conv_a_s, conv_b_s, conv_ffn_s)


def _unpack_small_weights(full):
    def take(r0, k, w):
        return jnp.transpose(full[:, r0:r0 + k, 0:w], (1, 0, 2)).reshape(k, N_DEV * w)

    return take(0, K_A, CONV_A_COLS), take(K_A, K_B, CONV_A_COLS), take(K_A + K_B, K_F, W_FF_COLS)


_SMALL_NAMES = ("conv_b_w", "conv_a_w", "conv_ffn_w", "norm_mix_g", "norm_ffn_g", "norm_final_g",
                "conv_b_b", "ln_b_g", "ln_b_b", "loss")
SMALL_G_ROWS = 64


def _small_rows(shapes):
    plan, at = [], 0
    for idx, (rows, cols) in enumerate(shapes):
        for r in range(rows):
            for c0 in range(0, cols, SMALL_G_COLS):
                plan.append((idx, r, c0, min(SMALL_G_COLS, cols - c0), at))
                at += 1
    assert at <= SMALL_G_ROWS
    return plan


def _pack_small_grads(small, loss):
    srcs = [loss if n == "loss" else small[n] for n in _SMALL_NAMES]
    plan = _small_rows([a.shape for a in srcs])

    def body(*refs):
        out = refs[-1]
        out[...] = jnp.zeros_like(out)
        for idx, r, c0, width, at in plan:
            out[at:at + 1, 0:width] = refs[idx][r:r + 1, c0:c0 + width]

    return pl.pallas_call(body, name="small_grad_pack", out_shape=_sds((SMALL_G_ROWS, SMALL_G_COLS), F32))(*srcs)


def kernel(x, norm_mix_g, w_in, conv_a_w, conv_b_w, conv_b_b, ln_b_g, ln_b_b, w_out, norm_ffn_g, w_gate, w_up, conv_ffn_w, w_down, norm_final_g, loss_target, m_norm_mix_g, m_w_in, m_conv_a_w, m_conv_b_w, m_conv_b_b, m_ln_b_g, m_ln_b_b, m_w_out, m_norm_ffn_g, m_w_gate, m_w_up, m_conv_ffn_w, m_w_down, m_norm_final_g, v_norm_mix_g, v_w_in, v_conv_a_w, v_conv_b_w, v_conv_b_b, v_ln_b_g, v_ln_b_b, v_w_out, v_norm_ffn_g, v_w_gate, v_w_up, v_conv_ffn_w, v_w_down, v_norm_final_g):
    ix, iy, ic = lax.axis_index("x"), lax.axis_index("y"), lax.axis_index("c")
    my_chip = (2 * ix + iy).astype(jnp.int32)
    my_core = ic.astype(jnp.int32)
    my_dev = 2 * my_chip + my_core

    weights = dict(norm_mix_g=norm_mix_g, w_in=w_in, conv_a_w=conv_a_w, conv_b_w=conv_b_w, conv_b_b=conv_b_b,
                   ln_b_g=ln_b_g, ln_b_b=ln_b_b, w_out=w_out, norm_ffn_g=norm_ffn_g, w_gate=w_gate, w_up=w_up,
                   conv_ffn_w=conv_ffn_w, w_down=w_down, norm_final_g=norm_final_g)
    m_in = dict(norm_mix_g=m_norm_mix_g, w_in=m_w_in, conv_a_w=m_conv_a_w, conv_b_w=m_conv_b_w, conv_b_b=m_conv_b_b,
                ln_b_g=m_ln_b_g, ln_b_b=m_ln_b_b, w_out=m_w_out, norm_ffn_g=m_norm_ffn_g, w_gate=m_w_gate,
                w_up=m_w_up, conv_ffn_w=m_conv_ffn_w, w_down=m_w_down, norm_final_g=m_norm_final_g)
    v_in = dict(norm_mix_g=v_norm_mix_g, w_in=v_w_in, conv_a_w=v_conv_a_w, conv_b_w=v_conv_b_w, conv_b_b=v_conv_b_b,
                ln_b_g=v_ln_b_g, ln_b_b=v_ln_b_b, w_out=v_w_out, norm_ffn_g=v_norm_ffn_g, w_gate=v_w_gate,
                w_up=v_w_up, conv_ffn_w=v_conv_ffn_w, w_down=v_w_down, norm_final_g=v_norm_final_g)
    order = list(weights)
    big_names = ("w_in", "w_gate", "w_up", "w_out", "w_down")
    transposed = ("w_in", "w_gate", "w_up")

    def shard2d(name, a):
        if name in transposed:
            return jnp.swapaxes(a[0], 0, 1)
        return a.reshape(1, a.shape[0]) if a.ndim == 1 else a.reshape(a.shape[-2:])

    def unshard2d(name, a2, like):
        if name in transposed:
            return jnp.swapaxes(a2, 0, 1)[None]
        return a2.reshape(like.shape)

    mesh = _Mesh(dict(zip(big_names, _cast_shards([shard2d(n, weights[n]) for n in big_names]))), my_chip, my_core)
    gathered, = _run_exchanges("ag_first", [_all_gather(
        [mesh.shards["w_in"], _pack_small_weights(conv_a_w[0], conv_b_w[0], conv_ffn_w[0])])])
    w_in_t = mesh.whole(gathered[0])
    wa_f, wb_f, wf_f = _unpack_small_weights(gathered[1])

    _, dx, gsum, stot = _step(
        x[0], loss_target[0], norm_mix_g, w_in_t, wa_f, wb_f, conv_b_b, ln_b_g, ln_b_b, None, norm_ffn_g,
        None, None, wf_f, None, norm_final_g.reshape(1, D_MODEL), SEQ_TILE, mesh)

    grads2d = dict(
        norm_mix_g=stot["norm_mix_g"],
        conv_a_w=lax.dynamic_slice(stot["conv_a_w"], (0, my_dev * CONV_A_COLS), (K_A, CONV_A_COLS)),
        conv_b_w=lax.dynamic_slice(stot["conv_b_w"], (0, my_dev * CONV_A_COLS), (K_B, CONV_A_COLS)),
        conv_b_b=stot["conv_b_b"], ln_b_g=stot["ln_b_g"], ln_b_b=stot["ln_b_b"],
        norm_ffn_g=stot["norm_ffn_g"],
        conv_ffn_w=lax.dynamic_slice(stot["conv_ffn_w"], (0, my_dev * W_FF_COLS), (K_F, W_FF_COLS)),
        norm_final_g=stot["norm_final_g"],
        **gsum,
    )

    updates = {}
    small_names = [n for n in order if n not in big_names]
    for group, label in [([n], "adamw_" + n) for n in big_names] + [(small_names, "adamw_small")]:
        outs = _adamw([shard2d(n, weights[n]) for n in group], [grads2d[n] for n in group],
                      [shard2d(n, m_in[n]) for n in group], [shard2d(n, v_in[n]) for n in group], label)
        for k, n in enumerate(group):
            updates[n] = outs[3 * k:3 * k + 3]
    g_out = [unshard2d(n, grads2d[n], weights[n]) for n in order]
    d_out, m_out, v_out = [[unshard2d(n, updates[n][j], weights[n]) for n in order] for j in range(3)]

    return (stot["loss"][0, 0], dx[None], *g_out, *d_out, *m_out, *v_out)
```

```python
import jax
import jax.numpy as jnp
from jax import lax
from jax.experimental import pallas as pl
from jax.experimental.pallas import tpu as pltpu

F32 = jnp.float32
BF16 = jnp.bfloat16

D_MODEL = 1024
D_A = 512
D_B = 512
D_IN = 3 * D_A + 2 * D_B
D_FF = 2816
K_A = 3
K_B = 31
K_F = 3
RMS_EPS = 1e-6
LN_EPS = 1e-5

ADAM_LR = 0.001
ADAM_B1 = 0.9
ADAM_B2 = 0.999
ADAM_EPS = 1e-08
ADAM_WD = 0.01
ADAM_STEP = 10

N_DEV = 8
N_CHIP = 4
LANES = 128
SUBLANES = 8
HALO = 16
ROW_CHUNK = 64
SEQ_TILE = 512
SKEW_TILE = 256
VMEM_LIMIT = 56 * 1024 * 1024

MESH = pl.DeviceIdType.MESH

W_FF_COLS = D_FF // N_DEV
CONV_A_COLS = D_A // N_DEV
SMALL_W_ROWS = 40
SMALL_W_COLS = 384
SMALL_G_COLS = 512


def _rows(ts, c):
    return pl.BlockSpec((ts, c), lambda i: (i, 0))


def _const(shape):
    return pl.BlockSpec(shape, lambda i: (0,) * len(shape), pipeline_mode=pl.Buffered(1))


def _acc_out(shape):
    return pl.BlockSpec(shape, lambda i: (0,) * len(shape))


def _rows_at(ts, c, tile):
    return pl.BlockSpec((ts, c), lambda i: (tile(i), 0))


def _prev_at(ts, c, tile):
    return pl.BlockSpec((HALO, c), lambda i: (jnp.maximum(tile(i) * (ts // HALO) - 1, 0), 0))


def _next_at(ts, c, s, tile):
    last = s // HALO - 1
    return pl.BlockSpec((HALO, c), lambda i: (jnp.minimum((tile(i) + 1) * (ts // HALO), last), 0))


def _prev(ts, c):
    return _prev_at(ts, c, lambda i: i)


def _next(ts, c, s):
    return _next_at(ts, c, s, lambda i: i)


MXU_COLS = 256
MXU_ROWS = 256


def _col_pieces(n):
    return [(c0, min(MXU_COLS, n - c0)) for c0 in range(0, n, MXU_COLS)]


def _matmul_pieces(terms, out_ref, k_parts, w_transposed=False):
    m, n = out_ref.shape
    rows = min(MXU_ROWS, m)
    steps = []
    for lhs_ref, w_ref in terms:
        tiles = lhs_ref.shape[1] // MXU_COLS
        cuts = [MXU_COLS * (tiles * j // k_parts) for j in range(k_parts)] + [lhs_ref.shape[1]]
        steps += [(lhs_ref, w_ref, cuts[j], cuts[j + 1]) for j in range(k_parts)]

    def piece(m0, n0, width, step):
        lhs_ref, w_ref, k0, k1 = steps[step]
        if w_transposed:
            part = lax.dot_general(lhs_ref[m0:m0 + rows, k0:k1], w_ref[n0:n0 + width, k0:k1], _NT,
                                   preferred_element_type=F32)
        else:
            part = jnp.dot(lhs_ref[m0:m0 + rows, k0:k1], w_ref[k0:k1, n0:n0 + width], preferred_element_type=F32)
        if step:
            part = part + out_ref[m0:m0 + rows, n0:n0 + width]
        out_ref[m0:m0 + rows, n0:n0 + width] = part

    return [(piece, (m0, n0, w, j)) for j in range(len(steps)) for n0, w in _col_pieces(n) for m0 in range(0, m, rows)]


def _interleaved(vector_units, matmul_pieces):
    n_u, n_p = len(vector_units), len(matmul_pieces)
    done = 0
    for k, (unit, args) in enumerate(vector_units):
        while done < n_p and done * n_u <= k * n_p:
            matmul_pieces[done][0](*matmul_pieces[done][1])
            done += 1
        unit(*args)
    for fn, args in matmul_pieces[done:]:
        fn(*args)


def _params():
    return pltpu.CompilerParams(dimension_semantics=("arbitrary",), vmem_limit_bytes=VMEM_LIMIT)


def _sds(shape, dtype):
    return jax.ShapeDtypeStruct(shape, dtype)


def _sigmoid(v):
    return 0.5 * jnp.tanh(0.5 * v) + 0.5


def _conv_block(ext_ref, w_ref, r0, rc, l0, k_taps, transposed):
    acc = None
    for k in range(k_taps):
        d = (k_taps // 2 - k) if transposed else (k - k_taps // 2)
        term = ext_ref[l0 // LANES, pl.ds(r0 + HALO + d, rc), :] * w_ref[k:k + 1, l0:l0 + LANES]
        acc = term if acc is None else acc + term
    return acc


def _conv_wgrad_block(acc_ref, dout, ext_ref, r0, rc, l0, k_taps, scale=None):
    for k in range(k_taps):
        prod = dout * ext_ref[l0 // LANES, pl.ds(r0 + HALO + k - k_taps // 2, rc), :]
        part = prod.reshape(rc // SUBLANES, SUBLANES, LANES).sum(axis=0)
        if scale is not None:
            part = part * scale
        acc_ref[k, :, l0:l0 + LANES] = acc_ref[k, :, l0:l0 + LANES] + part


def _reduce_acc(out_ref, acc_ref, k_taps):
    for k in range(k_taps):
        out_ref[k:k + 1, :] = jnp.sum(acc_ref[k], axis=0, keepdims=True)


def _fold8(v):
    rc, c = v.shape
    return v.reshape(rc // SUBLANES, SUBLANES, c).sum(axis=0)


def _ext_scratch(ts, c):
    return pltpu.VMEM((c // LANES, ts + 2 * HALO, LANES), F32)


def _put_rows(ext_ref, r0, rc, val):
    for q in range(val.shape[1] // LANES):
        ext_ref[q, pl.ds(r0 + HALO, rc), :] = val[:, q * LANES:(q + 1) * LANES]


def _fill_halo(ext_ref, vals_prev, vals_next, ts, first, last):
    for q in range(vals_prev.shape[1] // LANES):
        cols = slice(q * LANES, (q + 1) * LANES)
        ext_ref[q, 0:HALO, :] = jnp.where(first, 0.0, vals_prev[:, cols])
        ext_ref[q, HALO + ts:HALO + ts + HALO, :] = jnp.where(last, 0.0, vals_next[:, cols])


def _rms_bwd_rows(dh, xf, g):
    r = lax.rsqrt(jnp.mean(xf * xf, axis=-1, keepdims=True) + RMS_EPS)
    xhat = xf * r
    dxh = dh * g
    dx = r * (dxh - xhat * jnp.mean(dxh * xhat, axis=-1, keepdims=True))
    return dx, dh * xhat


_NT = (((1,), (1,)), ((), ()))
_TN = (((0,), (0,)), ((), ()))


def _fwd_in(x, g1, w_in_t, ts, exchange=None):
    s = x.shape[0]

    def body(x_ref, g_ref, w_ref, z_ref, h_ref):
        xf = x_ref[...]
        r = lax.rsqrt(jnp.mean(xf * xf, axis=-1, keepdims=True) + RMS_EPS)
        h = (xf * r * g_ref[...]).astype(BF16)
        h_ref[...] = h
        for n0 in range(0, D_IN, 512):
            z_ref[:, n0:n0 + 512] = lax.dot_general(h, w_ref[n0:n0 + 512, :], _NT,
                                                    preferred_element_type=F32).astype(BF16)

    return _call(
        body, name="fwd_in", grid=(s // ts,),
        in_specs=[_rows(ts, D_MODEL), _const((1, D_MODEL)), _const((D_IN, D_MODEL))],
        out_specs=[_rows(ts, D_IN), _rows(ts, D_MODEL)],
        out_shape=[_sds((s, D_IN), BF16), _sds((s, D_MODEL), BF16)],
        scratch_shapes=[], args=(x, g1, w_in_t), exchange=exchange, forward_step=(s // ts) * 3 // 4)


def _p_u0(z_ref, rows):
    a_h = z_ref[rows, 0:D_A].astype(F32)
    a_c = z_ref[rows, 2 * D_A:3 * D_A].astype(F32)
    b_v = z_ref[rows, 3 * D_A:3 * D_A + D_B].astype(F32)
    b_g = z_ref[rows, 3 * D_A + D_B:D_IN].astype(F32)
    return a_c * a_h, b_v * _sigmoid(b_g)


def _layernorm_rows(u_blocks):
    tot = None
    for ub in u_blocks:
        sm = jnp.sum(ub, axis=-1, keepdims=True)
        tot = sm if tot is None else tot + sm
    mu = tot * (1.0 / D_B)
    var = None
    for ub in u_blocks:
        sq = jnp.sum((ub - mu) * (ub - mu), axis=-1, keepdims=True)
        var = sq if var is None else var + sq
    rstd = lax.rsqrt(var * (1.0 / D_B) + LN_EPS)
    return mu, rstd


def _fwd_mix(z, x, wa, wb, bb, lg, lb, w_out, ts, exchange=None):
    s = x.shape[0]
    nt = s // ts
    rc = min(ROW_CHUNK, ts)

    def body(z_ref, zp_ref, zn_ref, x_ref, wa_ref, wb_ref, bb_ref, lg_ref, lb_ref, wo_ref,
             x2_ref, y_ref, u_ref, pe_ref, ue_ref):
        i = pl.program_id(0)
        pp, up = _p_u0(zp_ref, slice(None))
        pn, un = _p_u0(zn_ref, slice(None))
        _fill_halo(pe_ref, pp, pn, ts, i == 0, i == nt - 1)
        _fill_halo(ue_ref, up, un, ts, i == 0, i == nt - 1)

        def fill(j, carry):
            r0 = pl.multiple_of(j * rc, rc)
            p, u0 = _p_u0(z_ref, pl.ds(r0, rc))
            _put_rows(pe_ref, r0, rc, p)
            _put_rows(ue_ref, r0, rc, u0)
            return carry

        lax.fori_loop(0, ts // rc, fill, 0)

        def mixer_a(r0, l0):
            rows = pl.ds(r0, rc)
            ca = _conv_block(pe_ref, wa_ref, r0, rc, l0, K_A, False)
            a_b = z_ref[rows, D_A + l0:D_A + l0 + LANES].astype(F32)
            y_ref[rows, l0:l0 + LANES] = (a_b * ca).astype(BF16)

        def conv_b(r0, l0):
            u_ref[pl.ds(r0, rc), l0:l0 + LANES] = (_conv_block(ue_ref, wb_ref, r0, rc, l0, K_B, False)
                                                   + bb_ref[:, l0:l0 + LANES])

        def norm_b(r0):
            rows = pl.ds(r0, rc)
            ubs = [u_ref[rows, l0:l0 + LANES] for l0 in range(0, D_B, LANES)]
            mu, rstd = _layernorm_rows(ubs)
            for q, l0 in enumerate(range(0, D_B, LANES)):
                t = (ubs[q] - mu) * rstd * lg_ref[:, l0:l0 + LANES] + lb_ref[:, l0:l0 + LANES]
                y_ref[rows, D_A + l0:D_A + l0 + LANES] = (t * _sigmoid(t)).astype(BF16)

        def chunk_units(r0):
            return ([(mixer_a, (r0, l0)) for l0 in range(0, D_A, LANES)]
                    + [(conv_b, (r0, l0)) for l0 in range(0, D_B, LANES)] + [(norm_b, (r0,))])

        def main(j, carry):
            for fn, args in chunk_units(pl.multiple_of(j * rc, rc)):
                fn(*args)
            return carry

        def project(m0, n0, width):
            rows = slice(m0, m0 + hrows)
            x2_ref[rows, n0:n0 + width] = x_ref[rows, n0:n0 + width] + jnp.dot(
                y_ref[rows, :], wo_ref[:, n0:n0 + width], preferred_element_type=F32)

        hrows = ts // 2
        lax.fori_loop(0, hrows // rc, main, 0, unroll=4)
        _interleaved([u for q in range(hrows // rc) for u in chunk_units(hrows + q * rc)],
                     [(project, (0, n0, w)) for n0, w in _col_pieces(D_MODEL)])
        for n0, w in _col_pieces(D_MODEL):
            project(hrows, n0, w)

    return _call(
        body, name="fwd_mix", grid=(nt,),
        in_specs=[_rows(ts, D_IN), _prev(ts, D_IN), _next(ts, D_IN, s), _rows(ts, D_MODEL),
                  _const((K_A, D_A)), _const((K_B, D_B)), _const((1, D_B)), _const((1, D_B)), _const((1, D_B)),
                  _const((D_MODEL, D_MODEL))],
        out_specs=[_rows(ts, D_MODEL), _rows(ts, D_MODEL), _rows(ts, D_B)],
        out_shape=[_sds((s, D_MODEL), F32), _sds((s, D_MODEL), BF16), _sds((s, D_B), F32)],
        scratch_shapes=[_ext_scratch(ts, D_A), _ext_scratch(ts, D_B)],
        args=(z, z, z, x, wa, wb, bb, lg, lb, w_out), exchange=exchange, forward_step=nt * 5 // 8)


def _fwd_ffn_in(x2, g2, w_gate_t, w_up_t, ts, exchange=None, w_down=None):
    s = x2.shape[0]
    half = D_FF // 2

    def body(x_ref, g_ref, wg_ref, wu_ref, g0_ref, v_ref, h_ref):
        xf = x_ref[...]
        r = lax.rsqrt(jnp.mean(xf * xf, axis=-1, keepdims=True) + RMS_EPS)
        h = (xf * r * g_ref[...]).astype(BF16)
        h_ref[...] = h
        for n0 in range(0, D_FF, half):
            g0_ref[:, n0:n0 + half] = lax.dot_general(h, wg_ref[n0:n0 + half, :], _NT,
                                                      preferred_element_type=F32).astype(BF16)
            v_ref[:, n0:n0 + half] = lax.dot_general(h, wu_ref[n0:n0 + half, :], _NT,
                                                     preferred_element_type=F32).astype(BF16)

    def body_and_transpose(x_ref, g_ref, wg_ref, wu_ref, wd_ref, g0_ref, v_ref, h_ref, wdt_ref):
        @pl.when(pl.program_id(0) < n_blk)
        def _():
            wdt_ref[...] = wd_ref[...].T

        body(x_ref, g_ref, wg_ref, wu_ref, g0_ref, v_ref, h_ref)

    in_specs = [_rows(ts, D_MODEL), _const((1, D_MODEL)), _const((D_FF, D_MODEL)), _const((D_FF, D_MODEL))]
    out_specs = [_rows(ts, D_FF), _rows(ts, D_FF), _rows(ts, D_MODEL)]
    out_shape = [_sds((s, D_FF), BF16), _sds((s, D_FF), BF16), _sds((s, D_MODEL), BF16)]
    n_blk = D_FF // MXU_COLS
    if w_down is None or s // ts < n_blk:
        outs, got = _call(body, name="fwd_ffn_in", grid=(s // ts,), in_specs=in_specs, out_specs=out_specs,
                          out_shape=out_shape, scratch_shapes=[], args=(x2, g2, w_gate_t, w_up_t), exchange=exchange,
                          forward_step=(s // ts) // 2)
        return outs + [None if w_down is None else _transposed(w_down)], got
    blk = lambda i: jnp.minimum(i, n_blk - 1)
    return _call(
        body_and_transpose, name="fwd_ffn_in", grid=(s // ts,),
        in_specs=in_specs + [pl.BlockSpec((MXU_COLS, D_MODEL), lambda i: (blk(i), 0))],
        out_specs=out_specs + [pl.BlockSpec((D_MODEL, MXU_COLS), lambda i: (0, blk(i)))],
        out_shape=out_shape + [_sds((D_MODEL, D_FF), BF16)],
        scratch_shapes=[], args=(x2, g2, w_gate_t, w_up_t, w_down), exchange=exchange, forward_step=(s // ts) // 2)


def _ffn_out_and_back(g0, v, x2, wf, w_down, w_down_t, g3, target, ts):
    s = x2.shape[0]
    nt = s // ts
    rc = min(ROW_CHUNK, ts // 2)

    def body(g0_ref, gp_ref, gn_ref, v_ref, x2_ref, wf_ref, wd_ref, wdt_ref, g3_ref, t_ref,
             a_ref, dx3_ref, dx3b_ref, loss_ref, dg3_ref, dg_ref, dv_ref, dwf_ref,
             ge_ref, silu_ref, dsv_ref, da_ref, acc_ref, p_ref, sums_ref):
        i = pl.program_id(0)

        @pl.when(i == 0)
        def _():
            acc_ref[...] = jnp.zeros_like(acc_ref)
            sums_ref[...] = jnp.zeros_like(sums_ref)

        _fill_halo(ge_ref, gp_ref[...].astype(F32), gn_ref[...].astype(F32), ts, i == 0, i == nt - 1)

        def fill(j, carry):
            r0 = pl.multiple_of(j * rc, rc)
            _put_rows(ge_ref, r0, rc, g0_ref[pl.ds(r0, rc), :].astype(F32))
            return carry

        lax.fori_loop(0, ts // rc, fill, 0)

        def act(r0, l0):
            rows = slice(r0, r0 + rc)
            g = _conv_block(ge_ref, wf_ref, r0, rc, l0, K_F, False)
            vv = v_ref[rows, l0:l0 + LANES].astype(F32)
            sg = _sigmoid(g)
            silu = g * sg
            a_ref[rows, l0:l0 + LANES] = (silu * vv).astype(BF16)
            silu_ref[rows, l0:l0 + LANES] = silu
            dsv_ref[rows, l0:l0 + LANES] = (sg + silu * (1.0 - sg)) * vv

        def tail(r0):
            rows = slice(r0, r0 + rc)
            x3 = x2_ref[rows, :] + p_ref[rows, :]
            r = lax.rsqrt(jnp.mean(x3 * x3, axis=-1, keepdims=True) + RMS_EPS)
            xhat = x3 * r
            diff = xhat * g3_ref[...] - t_ref[rows, :]
            dout = diff * (1.0 / D_MODEL)
            dxh = dout * g3_ref[...]
            dx3 = r * (dxh - xhat * jnp.mean(dxh * xhat, axis=-1, keepdims=True))
            dx3_ref[rows, :] = dx3
            dx3b_ref[rows, :] = dx3.astype(BF16)
            sums_ref[0] = sums_ref[0] + _fold8(diff * diff)
            sums_ref[1] = sums_ref[1] + _fold8(dout * xhat)

        def back(r0, l0):
            rows = slice(r0, r0 + rc)
            da = da_ref[rows, l0:l0 + LANES]
            dv_ref[rows, l0:l0 + LANES] = (da * silu_ref[rows, l0:l0 + LANES]).astype(BF16)
            dgg = da * dsv_ref[rows, l0:l0 + LANES]
            dg_ref[rows, l0:l0 + LANES] = dgg.astype(BF16)
            _conv_wgrad_block(acc_ref, dgg, ge_ref, r0, rc, l0, K_F)

        hrows = ts // 2

        def units(fn, h, per_lane_block):
            starts = [h * hrows + q * rc for q in range(hrows // rc)]
            if per_lane_block:
                return [(fn, (r0, l0)) for r0 in starts for l0 in range(0, D_FF, LANES)]
            return [(fn, (r0,)) for r0 in starts]

        def rows_of(ref, h):
            return ref.at[pl.ds(h * hrows, hrows), :]

        def product(h):
            return _matmul_pieces([(rows_of(a_ref, h), wd_ref)], rows_of(p_ref, h), 2)

        def grad_a(h):
            return _matmul_pieces([(rows_of(dx3b_ref, h), wdt_ref)], rows_of(da_ref, h), 1)

        _interleaved(units(act, 0, True), [])
        _interleaved(units(act, 1, True), product(0))
        _interleaved(units(tail, 0, False), product(1))
        _interleaved(units(tail, 1, False), grad_a(0))
        _interleaved(units(back, 0, True), grad_a(1))
        _interleaved(units(back, 1, True), [])

        @pl.when(i == nt - 1)
        def _():
            _reduce_acc(dwf_ref, acc_ref, K_F)
            loss_ref[...] = (0.5 / D_MODEL) * jnp.sum(sums_ref[0], keepdims=True)
            dg3_ref[...] = jnp.sum(sums_ref[1], axis=0, keepdims=True)

    return pl.pallas_call(
        body, name="ffn_out_and_back", grid=(nt,),
        in_specs=[_rows(ts, D_FF), _prev(ts, D_FF), _next(ts, D_FF, s), _rows(ts, D_FF), _rows(ts, D_MODEL),
                  _const((K_F, D_FF)), _const((D_FF, D_MODEL)), _const((D_MODEL, D_FF)), _const((1, D_MODEL)),
                  _rows(ts, D_MODEL)],
        out_specs=[_rows(ts, D_FF), _rows(ts, D_MODEL), _rows(ts, D_MODEL), _acc_out((1, 1)), _acc_out((1, D_MODEL)),
                   _rows(ts, D_FF), _rows(ts, D_FF), _acc_out((K_F, D_FF))],
        out_shape=[_sds((s, D_FF), BF16), _sds((s, D_MODEL), F32), _sds((s, D_MODEL), BF16),
                   _sds((1, 1), F32), _sds((1, D_MODEL), F32),
                   _sds((s, D_FF), BF16), _sds((s, D_FF), BF16), _sds((K_F, D_FF), F32)],
        scratch_shapes=[_ext_scratch(ts, D_FF), pltpu.VMEM((ts, D_FF), F32), pltpu.VMEM((ts, D_FF), F32),
                        pltpu.VMEM((ts, D_FF), F32), pltpu.VMEM((K_F, SUBLANES, D_FF), F32),
                        pltpu.VMEM((ts, D_MODEL), F32), pltpu.VMEM((2, SUBLANES, D_MODEL), F32)],
        compiler_params=_params(),
    )(g0, g0, g0, v, x2, wf, w_down, w_down_t, g3, target)


def _bwd_ffn_b(dg, dv, wf, w_gate, w_up, x2, g2, dx3, ts, exchange=None):
    s = x2.shape[0]
    nt = s // ts
    rc = min(ROW_CHUNK, ts)
    n_sub = ts // rc

    def body(dg_ref, dgp_ref, dgn_ref, dv_ref, wf_ref, wg_ref, wu_ref, x2_ref, g2_ref, dx3_ref,
             dg0_ref, dx2_ref, dx2b_ref, dgn2_ref, dge_ref, dg8_ref, a0_ref, a1_ref, p0_ref, p1_ref):
        i = pl.program_id(0)
        vt = jnp.minimum(i, nt - 1)
        live = (i >= 2).astype(F32)

        @pl.when(i == 0)
        def _():
            dg8_ref[...] = jnp.zeros_like(dg8_ref)
            a1_ref[...] = jnp.zeros_like(a1_ref)
            p1_ref[...] = jnp.zeros_like(p1_ref)

        _fill_halo(dge_ref, dgp_ref[...].astype(F32), dgn_ref[...].astype(F32), ts, vt == 0, vt == nt - 1)

        def fill(j, carry):
            r0 = pl.multiple_of(j * rc, rc)
            _put_rows(dge_ref, r0, rc, dg_ref[pl.ds(r0, rc), :].astype(F32))
            return carry

        lax.fori_loop(0, n_sub, fill, 0)

        def stage(a_new, a_old, p_new, p_old):
            def conv_t(r0, l0):
                rows = slice(r0, r0 + rc)
                dg0 = _conv_block(dge_ref, wf_ref, r0, rc, l0, K_F, True).astype(BF16)
                dg0_ref[rows, l0:l0 + LANES] = dg0
                a_new[rows, l0:l0 + LANES] = dg0

            def tail(r0):
                rows = slice(r0, r0 + rc)
                dx, dgrow = _rms_bwd_rows(p_old[rows, :], x2_ref[rows, :], g2_ref[...])
                dx2 = dx3_ref[rows, :] + dx
                dx2_ref[rows, :] = dx2
                dx2b_ref[rows, :] = dx2.astype(BF16)
                dg8_ref[...] = dg8_ref[...] + _fold8(dgrow) * live

            units = []
            for q in range(n_sub):
                units += [(conv_t, (q * rc, l0)) for l0 in range(0, D_FF, LANES)]
                units.append((tail, (q * rc,)))
            _interleaved(units, _matmul_pieces([(a_old, wg_ref), (dv_ref, wu_ref)], p_new, 2))

        @pl.when(i % 2 == 0)
        def _():
            stage(a0_ref, a1_ref, p0_ref, p1_ref)

        @pl.when(i % 2 == 1)
        def _():
            stage(a1_ref, a0_ref, p1_ref, p0_ref)

        @pl.when(i == nt + 1)
        def _():
            dgn2_ref[...] = jnp.sum(dg8_ref[...], axis=0, keepdims=True)

    vtile = lambda i: jnp.minimum(i, nt - 1)
    mtile = lambda i: jnp.clip(i - 1, 0, nt - 1)
    ttile = lambda i: jnp.clip(i - 2, 0, nt - 1)
    return _call(
        body, name="bwd_ffn_b", grid=(nt + 2,),
        in_specs=[_rows_at(ts, D_FF, vtile), _prev_at(ts, D_FF, vtile), _next_at(ts, D_FF, s, vtile),
                  _rows_at(ts, D_FF, mtile), _const((K_F, D_FF)),
                  _const((D_FF, D_MODEL)), _const((D_FF, D_MODEL)), _rows_at(ts, D_MODEL, ttile), _const((1, D_MODEL)),
                  _rows_at(ts, D_MODEL, ttile)],
        out_specs=[_rows_at(ts, D_FF, vtile), _rows_at(ts, D_MODEL, ttile), _rows_at(ts, D_MODEL, ttile),
                   _acc_out((1, D_MODEL))],
        out_shape=[_sds((s, D_FF), BF16), _sds((s, D_MODEL), F32), _sds((s, D_MODEL), BF16), _sds((1, D_MODEL), F32)],
        scratch_shapes=[_ext_scratch(ts, D_FF), pltpu.VMEM((SUBLANES, D_MODEL), F32),
                        pltpu.VMEM((ts, D_FF), BF16), pltpu.VMEM((ts, D_FF), BF16),
                        pltpu.VMEM((ts, D_MODEL), F32), pltpu.VMEM((ts, D_MODEL), F32)],
        args=(dg, dg, dg, dv, wf, w_gate, w_up, x2, g2, dx3), exchange=exchange)


def _bwd_mix_a(dx2b, w_out_t, z, u, wa, lg, lb, ts, exchange=None):
    s = dx2b.shape[0]
    nt = s // ts
    rc = min(ROW_CHUNK, ts)
    n_sub = ts // rc

    def body(dx_ref, wo_ref, z_ref, zp_ref, zn_ref, u_ref, wa_ref, lg_ref, lb_ref,
             dca_ref, du_ref, dab_ref, dwa_ref, dlg_ref, dlb_ref, dbb_ref, pe_ref, dy0_ref, dy1_ref, acc_ref, sacc_ref):
        i = pl.program_id(0)
        t = jnp.maximum(i - 1, 0)

        @pl.when(i == 0)
        def _():
            acc_ref[...] = jnp.zeros_like(acc_ref)
            sacc_ref[...] = jnp.zeros_like(sacc_ref)
            dy1_ref[...] = jnp.zeros_like(dy1_ref)

        pp, _ = _p_u0(zp_ref, slice(None))
        pn, _ = _p_u0(zn_ref, slice(None))
        _fill_halo(pe_ref, pp, pn, ts, t == 0, t == nt - 1)

        def fill(j, carry):
            r0 = pl.multiple_of(j * rc, rc)
            rows = pl.ds(r0, rc)
            _put_rows(pe_ref, r0, rc, z_ref[rows, 2 * D_A:3 * D_A].astype(F32) * z_ref[rows, 0:D_A].astype(F32))
            return carry

        lax.fori_loop(0, n_sub, fill, 0)

        def stage(dy_new, dy_old):
            def piece(m0, n0, width):
                dy_new[m0:m0 + MXU_ROWS, n0:n0 + width] = jnp.dot(
                    dx_ref[m0:m0 + MXU_ROWS, :], wo_ref[:, n0:n0 + width], preferred_element_type=F32).astype(BF16)

            units = []
            for q in range(n_sub):
                units += [(mixer_a, (dy_old, q * rc, l0)) for l0 in range(0, D_A, LANES)]
                units.append((mixer_b, (dy_old, q * rc)))
            _interleaved(units, [(piece, (m0, n0, w)) for n0, w in _col_pieces(D_MODEL) for m0 in range(0, ts, MXU_ROWS)])

        def mixer_a(dy_ref, r0, l0):
            rows = slice(r0, r0 + rc)
            ca = _conv_block(pe_ref, wa_ref, r0, rc, l0, K_A, False)
            a_b = z_ref[rows, D_A + l0:D_A + l0 + LANES].astype(F32)
            dya = dy_ref[rows, l0:l0 + LANES].astype(F32)
            dab_ref[rows, l0:l0 + LANES] = (dya * ca).astype(BF16)
            dca = dya * a_b
            dca_ref[rows, l0:l0 + LANES] = dca
            _conv_wgrad_block(acc_ref, dca, pe_ref, r0, rc, l0, K_A)

        def mixer_b(dy_ref, r0):
            rows = slice(r0, r0 + rc)
            ubs = [u_ref[rows, l0:l0 + LANES] for l0 in range(0, D_B, LANES)]
            mu, rstd = _layernorm_rows(ubs)
            ns, dns = [], []
            m1 = None
            m2 = None
            for q, l0 in enumerate(range(0, D_B, LANES)):
                n = (ubs[q] - mu) * rstd
                lgq = lg_ref[:, l0:l0 + LANES]
                t = n * lgq + lb_ref[:, l0:l0 + LANES]
                sg = _sigmoid(t)
                dt = dy_ref[rows, D_A + l0:D_A + l0 + LANES].astype(F32) * (sg * (1.0 + t * (1.0 - sg)))
                dn = dt * lgq
                ns.append(n)
                dns.append(dn)
                s1 = jnp.sum(dn, axis=-1, keepdims=True)
                s2 = jnp.sum(dn * n, axis=-1, keepdims=True)
                m1 = s1 if m1 is None else m1 + s1
                m2 = s2 if m2 is None else m2 + s2
                sacc_ref[0, :, l0:l0 + LANES] = sacc_ref[0, :, l0:l0 + LANES] + _fold8(dt * n)
                sacc_ref[1, :, l0:l0 + LANES] = sacc_ref[1, :, l0:l0 + LANES] + _fold8(dt)
            m1 = m1 * (1.0 / D_B)
            m2 = m2 * (1.0 / D_B)
            for q, l0 in enumerate(range(0, D_B, LANES)):
                du = rstd * (dns[q] - m1 - ns[q] * m2)
                du_ref[rows, l0:l0 + LANES] = du
                sacc_ref[2, :, l0:l0 + LANES] = sacc_ref[2, :, l0:l0 + LANES] + _fold8(du)

        @pl.when(i % 2 == 0)
        def _():
            stage(dy0_ref, dy1_ref)

        @pl.when(i % 2 == 1)
        def _():
            stage(dy1_ref, dy0_ref)

        @pl.when(i == nt)
        def _():
            _reduce_acc(dwa_ref, acc_ref, K_A)
            dlg_ref[...] = jnp.sum(sacc_ref[0], axis=0, keepdims=True)
            dlb_ref[...] = jnp.sum(sacc_ref[1], axis=0, keepdims=True)
            dbb_ref[...] = jnp.sum(sacc_ref[2], axis=0, keepdims=True)

    cur = lambda i: jnp.minimum(i, nt - 1)
    old = lambda i: jnp.maximum(i - 1, 0)
    return _call(
        body, name="bwd_mix_a", grid=(nt + 1,),
        in_specs=[_rows_at(ts, D_MODEL, cur), _const((D_MODEL, D_MODEL)), _rows_at(ts, D_IN, old), _prev_at(ts, D_IN, old),
                  _next_at(ts, D_IN, s, old), _rows_at(ts, D_B, old), _const((K_A, D_A)), _const((1, D_B)), _const((1, D_B))],
        out_specs=[_rows_at(ts, D_A, old), _rows_at(ts, D_B, old), _rows_at(ts, D_A, old), _acc_out((K_A, D_A)),
                   _acc_out((1, D_B)), _acc_out((1, D_B)), _acc_out((1, D_B))],
        out_shape=[_sds((s, D_A), F32), _sds((s, D_B), F32), _sds((s, D_A), BF16), _sds((K_A, D_A), F32),
                   _sds((1, D_B), F32), _sds((1, D_B), F32), _sds((1, D_B), F32)],
        scratch_shapes=[_ext_scratch(ts, D_A), pltpu.VMEM((ts, D_MODEL), BF16), pltpu.VMEM((ts, D_MODEL), BF16),
                        pltpu.VMEM((K_A, SUBLANES, D_A), F32), pltpu.VMEM((3, SUBLANES, D_B), F32)],
        args=(dx2b, w_out_t, z, z, z, u, wa, lg, lb), exchange=exchange)


def _bwd_mix_b(dca, du, z, dab, wa, wb, w_in, x, g1, dx2, ts, exchange=None):
    s = x.shape[0]
    nt = s // ts
    rc = min(ROW_CHUNK, ts)
    n_sub = ts // rc

    def body(dca_ref, dcap_ref, dcan_ref, du_ref, dup_ref, dun_ref, z_ref, zp_ref, zn_ref, dab_ref,
             wa_ref, wb_ref, wi_ref, x_ref, g1_ref, dx2_ref,
             dz_ref, dx_ref, dg1_ref, dwb_ref, dcae_ref, due_ref, ue_ref, acc_ref, dg8_ref,
             dz0_ref, dz1_ref, dh0_ref, dh1_ref):
        i = pl.program_id(0)

        @pl.when(i == 0)
        def _():
            acc_ref[...] = jnp.zeros_like(acc_ref)
            dg8_ref[...] = jnp.zeros_like(dg8_ref)
            dz1_ref[...] = jnp.zeros_like(dz1_ref)
            dh1_ref[...] = jnp.zeros_like(dh1_ref)

        vt = jnp.minimum(i, nt - 1)
        first = vt == 0
        last = vt == nt - 1
        live = (i < nt).astype(F32)
        _fill_halo(dcae_ref, dcap_ref[...], dcan_ref[...], ts, first, last)
        _fill_halo(due_ref, dup_ref[...], dun_ref[...], ts, first, last)
        _, up = _p_u0(zp_ref, slice(None))
        _, un = _p_u0(zn_ref, slice(None))
        _fill_halo(ue_ref, up, un, ts, first, last)

        def fill(j, carry):
            r0 = pl.multiple_of(j * rc, rc)
            rows = pl.ds(r0, rc)
            _put_rows(dcae_ref, r0, rc, dca_ref[rows, :])
            _put_rows(due_ref, r0, rc, du_ref[rows, :])
            b_v = z_ref[rows, 3 * D_A:3 * D_A + D_B].astype(F32)
            b_g = z_ref[rows, 3 * D_A + D_B:D_IN].astype(F32)
            _put_rows(ue_ref, r0, rc, b_v * _sigmoid(b_g))
            return carry

        lax.fori_loop(0, n_sub, fill, 0)

        def stage(dz_new, dz_old, dh_new, dh_old):
            def put(rows, c0, val):
                dz_ref[rows, c0:c0 + LANES] = val
                dz_new[rows, c0:c0 + LANES] = val

            def mixer_a(r0, l0):
                rows = slice(r0, r0 + rc)
                dp = _conv_block(dcae_ref, wa_ref, r0, rc, l0, K_A, True)
                a_h = z_ref[rows, l0:l0 + LANES].astype(F32)
                a_c = z_ref[rows, 2 * D_A + l0:2 * D_A + l0 + LANES].astype(F32)
                put(rows, l0, (dp * a_c).astype(BF16))
                put(rows, D_A + l0, dab_ref[rows, l0:l0 + LANES])
                put(rows, 2 * D_A + l0, (dp * a_h).astype(BF16))

            def mixer_b(r0, l0):
                rows = slice(r0, r0 + rc)
                du0 = _conv_block(due_ref, wb_ref, r0, rc, l0, K_B, True)
                b_v = z_ref[rows, 3 * D_A + l0:3 * D_A + l0 + LANES].astype(F32)
                b_g = z_ref[rows, 3 * D_A + D_B + l0:3 * D_A + D_B + l0 + LANES].astype(F32)
                sg = _sigmoid(b_g)
                put(rows, 3 * D_A + l0, (du0 * sg).astype(BF16))
                put(rows, 3 * D_A + D_B + l0, (du0 * b_v * (sg * (1.0 - sg))).astype(BF16))
                _conv_wgrad_block(acc_ref, du_ref[rows, l0:l0 + LANES], ue_ref, r0, rc, l0, K_B, live)

            def tail(r0):
                rows = slice(r0, r0 + rc)
                dx, dgrow = _rms_bwd_rows(dh_old[rows, :], x_ref[rows, :], g1_ref[...])
                dx_ref[rows, :] = dx2_ref[rows, :] + dx
                dg8_ref[...] = dg8_ref[...] + _fold8(dgrow)

            units = []
            for q in range(n_sub):
                units += [(mixer_a, (q * rc, l0)) for l0 in range(0, D_A, LANES)]
                units += [(mixer_b, (q * rc, l0)) for l0 in range(0, D_B, LANES)]
                units.append((tail, (q * rc,)))
            _interleaved(units, _matmul_pieces([(dz_old, wi_ref)], dh_new, 2))

        @pl.when(i % 2 == 0)
        def _():
            stage(dz0_ref, dz1_ref, dh0_ref, dh1_ref)

        @pl.when(i % 2 == 1)
        def _():
            stage(dz1_ref, dz0_ref, dh1_ref, dh0_ref)

        @pl.when(i == nt + 1)
        def _():
            _reduce_acc(dwb_ref, acc_ref, K_B)
            dg1_ref[...] = jnp.sum(dg8_ref[...], axis=0, keepdims=True)

    vtile = lambda i: jnp.minimum(i, nt - 1)
    ttile = lambda i: jnp.clip(i - 2, 0, nt - 1)
    return _call(
        body, name="bwd_mix_b", grid=(nt + 2,),
        in_specs=[_rows_at(ts, D_A, vtile), _prev_at(ts, D_A, vtile), _next_at(ts, D_A, s, vtile),
                  _rows_at(ts, D_B, vtile), _prev_at(ts, D_B, vtile), _next_at(ts, D_B, s, vtile),
                  _rows_at(ts, D_IN, vtile), _prev_at(ts, D_IN, vtile), _next_at(ts, D_IN, s, vtile), _rows_at(ts, D_A, vtile),
                  _const((K_A, D_A)), _const((K_B, D_B)), _const((D_IN, D_MODEL)), _rows_at(ts, D_MODEL, ttile),
                  _const((1, D_MODEL)), _rows_at(ts, D_MODEL, ttile)],
        out_specs=[_rows_at(ts, D_IN, vtile), _rows_at(ts, D_MODEL, ttile), _acc_out((1, D_MODEL)), _acc_out((K_B, D_B))],
        out_shape=[_sds((s, D_IN), BF16), _sds((s, D_MODEL), F32), _sds((1, D_MODEL), F32), _sds((K_B, D_B), F32)],
        scratch_shapes=[_ext_scratch(ts, D_A), _ext_scratch(ts, D_B), _ext_scratch(ts, D_B),
                        pltpu.VMEM((K_B, SUBLANES, D_B), F32), pltpu.VMEM((SUBLANES, D_MODEL), F32),
                        pltpu.VMEM((ts, D_IN), BF16), pltpu.VMEM((ts, D_IN), BF16),
                        pltpu.VMEM((ts, D_MODEL), F32), pltpu.VMEM((ts, D_MODEL), F32)],
        args=(dca, dca, dca, du, du, du, z, z, z, dab, wa, wb, w_in, x, g1, dx2), exchange=exchange)


def _matmul_tn(a, b, name, exchange=None):
    s, m = a.shape
    n = b.shape[1]
    tk = min(1024, s)
    nk = s // tk
    tm = 256

    def body(a_ref, b_ref, o_ref, acc_ref):
        k = pl.program_id(0)

        @pl.when(k == 0)
        def _():
            acc_ref[...] = jnp.zeros_like(acc_ref)

        for m0 in range(0, m, tm):
            acc_ref[m0:m0 + tm, :] = acc_ref[m0:m0 + tm, :] + lax.dot_general(
                a_ref[:, m0:m0 + tm], b_ref[...], _TN, preferred_element_type=F32)

        @pl.when(k == nk - 1)
        def _():
            o_ref[...] = acc_ref[...].astype(BF16)

    (out,), got = _call(
        body, name=name, grid=(nk,),
        in_specs=[_rows(tk, m), _rows(tk, n)],
        out_specs=[_acc_out((m, n))],
        out_shape=[_sds((m, n), BF16)],
        scratch_shapes=[pltpu.VMEM((m, n), F32)], args=(a, b), exchange=exchange)
    return out if exchange is None else (out, got)


CHIP_RELS = ((1, 0, 0), (0, 1, 0), (1, 1, 0))
CORE_RELS = ((0, 0, 1),)
ALL_RELS = ((0, 0, 1), (0, 1, 0), (0, 1, 1), (1, 0, 0), (1, 0, 1), (1, 1, 0), (1, 1, 1))


def _chip_slot(dev):
    return 2 * dev[0] + dev[1]


def _dev_slot(dev):
    return 4 * dev[0] + 2 * dev[1] + dev[2]


def _me():
    return (lax.axis_index("x"), lax.axis_index("y"), lax.axis_index("c"))


def _peer(me, rel):
    return tuple((1 - me[a]) if rel[a] else me[a] for a in range(3))


_ANY = pl.BlockSpec(memory_space=pl.ANY)


class _Exchange:
    def __init__(self, inputs, out_shape, scratch, start, finish, forward=None):
        self.inputs, self.out_shape, self.scratch = list(inputs), list(out_shape), list(scratch)
        self.start, self.finish, self.forward = start, finish, forward


def _all_gather(payloads):
    n_p = len(payloads)
    n_k = 1 + 2 * len(CHIP_RELS)

    def copy(srcs, dsts, sems, p, k, block_dev, to, from_src):
        blk = dsts[p].at[_dev_slot(block_dev)]
        return pltpu.make_async_remote_copy(
            src_ref=srcs[p] if from_src else blk, dst_ref=blk,
            send_sem=sems[0].at[n_k * p + k], recv_sem=sems[1].at[n_k * p + k], device_id=to, device_id_type=MESH)

    def own_copy(srcs, dsts, sems, p):
        return pltpu.make_async_copy(srcs[p], dsts[p].at[_dev_slot(_me())], sems[2].at[p])

    def start(srcs, dsts, sems):
        me = _me()
        for p in range(n_p):
            own_copy(srcs, dsts, sems, p).start()
        for j, rel in enumerate(CHIP_RELS):
            for p in range(n_p):
                copy(srcs, dsts, sems, p, 1 + j, me, _peer(me, rel), True).start()
        for p in range(n_p):
            copy(srcs, dsts, sems, p, 0, me, _peer(me, CORE_RELS[0]), True).start()

    def forward(srcs, dsts, sems):
        me = _me()
        sibling = _peer(me, CORE_RELS[0])
        for j, rel in enumerate(CHIP_RELS):
            other = _peer(me, rel)
            for p in range(n_p):
                copy(srcs, dsts, sems, p, 1 + j, other, me, False).wait_recv()
                copy(srcs, dsts, sems, p, 4 + j, other, sibling, False).start()

    def finish(srcs, dsts, sems):
        me = _me()
        sibling = _peer(me, CORE_RELS[0])
        for p in range(n_p):
            copy(srcs, dsts, sems, p, 0, sibling, me, False).wait_recv()
        for j, rel in enumerate(CHIP_RELS):
            for p in range(n_p):
                copy(srcs, dsts, sems, p, 4 + j, _peer(sibling, rel), me, False).wait_recv()
        for p in range(n_p):
            own_copy(srcs, dsts, sems, p).wait()
            copy(srcs, dsts, sems, p, 0, me, sibling, True).wait_send()
            for j, rel in enumerate(CHIP_RELS):
                copy(srcs, dsts, sems, p, 1 + j, me, _peer(me, rel), True).wait_send()
                copy(srcs, dsts, sems, p, 4 + j, _peer(me, rel), sibling, False).wait_send()

    return _Exchange(
        payloads, [_sds((N_DEV,) + p.shape, p.dtype) for p in payloads],
        [pltpu.SemaphoreType.DMA((n_p * n_k,)), pltpu.SemaphoreType.DMA((n_p * n_k,)), pltpu.SemaphoreType.DMA((n_p,))],
        start, finish, forward)


def _gather_direct(payload):
    n_r = len(ALL_RELS)

    def copies(srcs, dsts, sems):
        me = _me()
        mine = dsts[0].at[_dev_slot(me)]
        own = pltpu.make_async_copy(srcs[0], mine, sems[2].at[0])
        remote = [pltpu.make_async_remote_copy(src_ref=srcs[0], dst_ref=mine, send_sem=sems[0].at[k], recv_sem=sems[1].at[k],
                                               device_id=_peer(me, rel), device_id_type=MESH)
                  for k, rel in enumerate(ALL_RELS)]
        return [own] + remote

    def start(srcs, dsts, sems):
        for cp in copies(srcs, dsts, sems):
            cp.start()

    def finish(srcs, dsts, sems):
        for cp in copies(srcs, dsts, sems):
            cp.wait()

    return _Exchange([payload], [_sds((N_DEV,) + payload.shape, payload.dtype)],
                     [pltpu.SemaphoreType.DMA((n_r,)), pltpu.SemaphoreType.DMA((n_r,)), pltpu.SemaphoreType.DMA((1,))],
                     start, finish)


def _scatter_exchange(payloads, rels, src_view, view_shapes):
    n_p = len(payloads)
    n_r = len(rels)

    def copies(srcs, dsts, sems):
        me = _me()
        out = []
        for k, rel in enumerate(rels):
            peer = _peer(me, rel)
            for p in range(n_p):
                out.append(pltpu.make_async_remote_copy(
                    src_ref=src_view(srcs[p], peer), dst_ref=dsts[p].at[k],
                    send_sem=sems[0].at[p * n_r + k], recv_sem=sems[1].at[p * n_r + k],
                    device_id=peer, device_id_type=MESH))
        return out

    def start(srcs, dsts, sems):
        for cp in copies(srcs, dsts, sems):
            cp.start()

    def finish(srcs, dsts, sems):
        for cp in copies(srcs, dsts, sems):
            cp.wait()

    return _Exchange(payloads, [_sds((n_r,) + vs, p.dtype) for vs, p in zip(view_shapes, payloads)],
                     [pltpu.SemaphoreType.DMA((n_p * n_r,)), pltpu.SemaphoreType.DMA((n_p * n_r,))], start, finish)


def _split_refs(refs, sizes):
    out, at = [], 0
    for n in sizes:
        out.append(refs[at:at + n])
        at += n
    return out


def _join(exchanges):
    n_in = [len(e.inputs) for e in exchanges]
    n_out = [len(e.out_shape) for e in exchanges]
    n_sc = [len(e.scratch) for e in exchanges]

    def phase(name):
        def run(ins, outs, scs):
            for e, i, o, s in zip(exchanges, _split_refs(ins, n_in), _split_refs(outs, n_out), _split_refs(scs, n_sc)):
                if getattr(e, name) is not None:
                    getattr(e, name)(i, o, s)
        return run

    return _Exchange([a for e in exchanges for a in e.inputs], [s for e in exchanges for s in e.out_shape],
                     [s for e in exchanges for s in e.scratch], phase("start"), phase("finish"),
                     phase("forward") if any(e.forward is not None for e in exchanges) else None)


def _run_exchanges(name, exchanges):
    n_in = [len(e.inputs) for e in exchanges]
    n_out = [len(e.out_shape) for e in exchanges]
    n_sc = [len(e.scratch) for e in exchanges]

    def body(*refs):
        ins, outs, scs = _split_refs(refs, [sum(n_in), sum(n_out), sum(n_sc)])
        parts = list(zip(exchanges, _split_refs(ins, n_in), _split_refs(outs, n_out), _split_refs(scs, n_sc)))
        for e, i, o, s in parts:
            e.start(i, o, s)
        for e, i, o, s in parts:
            if e.forward is not None:
                e.forward(i, o, s)
        for e, i, o, s in parts:
            e.finish(i, o, s)

    outs = pl.pallas_call(
        body, name=name, in_specs=[_ANY] * sum(n_in), out_specs=[_ANY] * sum(n_out),
        out_shape=[sd for e in exchanges for sd in e.out_shape],
        scratch_shapes=[sc for e in exchanges for sc in e.scratch],
    )(*[a for e in exchanges for a in e.inputs])
    return _split_refs(list(outs), n_out)


def _call(body, *, name, grid, in_specs, out_specs, out_shape, scratch_shapes, args, exchange=None, forward_step=None):
    n_in, n_out, n_sc = len(in_specs), len(out_specs), len(scratch_shapes)
    if exchange is None:
        outs = pl.pallas_call(body, name=name, grid=grid, in_specs=in_specs, out_specs=out_specs, out_shape=out_shape,
                              scratch_shapes=scratch_shapes, compiler_params=_params())(*args)
        return list(outs), []
    e = exchange
    sizes = [n_in, len(e.inputs), n_out, len(e.out_shape), n_sc, len(e.scratch)]
    last = grid[0] - 1

    def wrapped(*refs):
        a, ei, o, eo, sc, es = _split_refs(refs, sizes)
        i = pl.program_id(0)

        @pl.when(i == 0)
        def _():
            e.start(ei, eo, es)

        if e.forward is not None:
            @pl.when(i == forward_step)
            def _():
                e.forward(ei, eo, es)

        body(*a, *o, *sc)

        @pl.when(i == last)
        def _():
            e.finish(ei, eo, es)

    outs = pl.pallas_call(
        wrapped, name=name, grid=grid,
        in_specs=list(in_specs) + [_ANY] * len(e.inputs), out_specs=list(out_specs) + [_ANY] * len(e.out_shape),
        out_shape=list(out_shape) + e.out_shape, scratch_shapes=list(scratch_shapes) + e.scratch,
        compiler_params=_params(),
    )(*args, *e.inputs)
    outs = list(outs)
    return outs[:n_out], outs[n_out:]


def _pair_sum(grads, recvd, my_core, name):
    n_p = len(grads)

    def body(c_ref, *refs):
        del c_ref
        for p in range(n_p):
            refs[2 * n_p + p][...] = (refs[p][...].astype(F32) + refs[n_p + p][...].astype(F32)).astype(BF16)

    def blk(g):
        return (None, None) + g.shape[2:]

    return pl.pallas_call(
        body, name=name,
        grid_spec=pltpu.PrefetchScalarGridSpec(
            num_scalar_prefetch=1, grid=(N_CHIP,),
            in_specs=[pl.BlockSpec(blk(g), lambda j, c: (j, c[0], 0, 0)) for g in grads]
            + [pl.BlockSpec(blk(g), lambda j, c: (0, j, 0, 0)) for g in grads],
            out_specs=[pl.BlockSpec((None,) + g.shape[2:], lambda j, c: (j, 0, 0)) for g in grads]),
        out_shape=[_sds((N_CHIP,) + g.shape[2:], BF16) for g in grads],
        compiler_params=pltpu.CompilerParams(dimension_semantics=("arbitrary",), vmem_limit_bytes=VMEM_LIMIT),
    )(my_core, *grads, *recvd)


def _chip_sum(psums, recvd, my_chip, name):
    n_p = len(psums)

    def body(c_ref, *refs):
        del c_ref
        for p in range(n_p):
            acc = refs[p][...].astype(F32)
            for k in range(len(CHIP_RELS)):
                acc = acc + refs[n_p + p][k].astype(F32)
            refs[2 * n_p + p][...] = acc

    return pl.pallas_call(
        body, name=name,
        grid_spec=pltpu.PrefetchScalarGridSpec(
            num_scalar_prefetch=1, grid=(1,),
            in_specs=[pl.BlockSpec((None,) + g.shape[1:], lambda i, c: (c[0], 0, 0)) for g in psums]
            + [pl.BlockSpec(r.shape, lambda i, c: (0, 0, 0)) for r in recvd],
            out_specs=[pl.BlockSpec(g.shape[1:], lambda i, c: (0, 0)) for g in psums]),
        out_shape=[_sds(g.shape[1:], F32) for g in psums],
        compiler_params=pltpu.CompilerParams(dimension_semantics=("arbitrary",), vmem_limit_bytes=VMEM_LIMIT),
    )(my_chip, *psums, *recvd)


def _sum_devices(parts, shapes):
    plan = _small_rows(shapes)

    def body(p_ref, *refs):
        outs, tot_ref = refs[:-1], refs[-1]
        acc = p_ref[0]
        for j in range(1, N_DEV):
            acc = acc + p_ref[j]
        tot_ref[...] = acc
        for idx, r, c0, width, at in plan:
            outs[idx][r:r + 1, c0:c0 + width] = tot_ref[at:at + 1, 0:width]

    return pl.pallas_call(body, name="small_grad_sum", out_shape=[_sds(s, F32) for s in shapes],
                          scratch_shapes=[pltpu.VMEM(parts.shape[1:], F32)])(parts)


def _transposed(w):
    r, c = w.shape
    tr = MXU_COLS

    def body(w_ref, o_ref):
        o_ref[...] = w_ref[...].T

    return pl.pallas_call(
        body, name="transpose_w_down", grid=(r // tr,),
        in_specs=[pl.BlockSpec((tr, c), lambda i: (i, 0))], out_specs=pl.BlockSpec((c, tr), lambda i: (0, i)),
        out_shape=_sds((c, r), w.dtype), compiler_params=_params())(w)


def _cast_shards(shards):
    def body(*refs):
        for src, dst in zip(refs[:len(shards)], refs[len(shards):]):
            dst[...] = src[...].astype(BF16)

    return pl.pallas_call(body, name="cast_shards", out_shape=[_sds(a.shape, BF16) for a in shards],
                          compiler_params=pltpu.CompilerParams(vmem_limit_bytes=VMEM_LIMIT))(*shards)


def _adamw(ws, gs, ms, vs, name):
    n_t = len(ws)

    def body(*refs):
        w_refs, g_refs, m_refs, v_refs = (refs[j * n_t:(j + 1) * n_t] for j in range(4))
        outs = refs[4 * n_t:]
        for k in range(n_t):
            gg = g_refs[k][...]
            mn = ADAM_B1 * m_refs[k][...] + (1.0 - ADAM_B1) * gg
            vn = ADAM_B2 * v_refs[k][...] + (1.0 - ADAM_B2) * (gg * gg)
            m_hat = mn / (1.0 - ADAM_B1 ** ADAM_STEP)
            v_hat = vn / (1.0 - ADAM_B2 ** ADAM_STEP)
            outs[3 * k][...] = -ADAM_LR * (m_hat / (jnp.sqrt(v_hat) + ADAM_EPS) + ADAM_WD * w_refs[k][...])
            outs[3 * k + 1][...] = mn
            outs[3 * k + 2][...] = vn

    out_shape = [_sds(w.shape, F32) for w in ws for _ in range(3)]
    return pl.pallas_call(body, name=name, out_shape=out_shape,
                          compiler_params=pltpu.CompilerParams(vmem_limit_bytes=VMEM_LIMIT))(*ws, *gs, *ms, *vs)


class _Mesh:
    def __init__(self, shards, my_chip, my_core):
        self.shards, self.my_chip, self.my_core = shards, my_chip.reshape(1), my_core.reshape(1)

    def gather(self, names):
        return _all_gather([self.shards[n] for n in names])

    @staticmethod
    def whole(gathered):
        return gathered.reshape(N_DEV * gathered.shape[1], gathered.shape[2])

    @staticmethod
    def by_device(grads):
        return [g.reshape(N_CHIP, 2, g.shape[0] // N_DEV, g.shape[1]) for g in grads]

    @staticmethod
    def to_sibling(parts):
        return _scatter_exchange(parts, CORE_RELS, lambda ref, peer: ref.at[:, peer[2]],
                                 [(N_CHIP,) + p.shape[2:] for p in parts])

    @staticmethod
    def to_chips(pair):
        return _scatter_exchange(pair, CHIP_RELS, lambda ref, peer: ref.at[_chip_slot(peer)], [p.shape[1:] for p in pair])


def _step(x, target, g1, w_in_t, wa, wb, bb, lg, lb, w_out, g2, w_gate_t, w_up_t, wf, w_down, g3, ts, mesh=None):
    (z, h1), got = _fwd_in(x, g1, w_in_t, ts, exchange=mesh and mesh.gather(["w_out", "w_down"]))
    if mesh:
        w_out, w_down = [mesh.whole(g) for g in got]
    (x2, y, u), got = _fwd_mix(z, x, wa, wb, bb, lg, lb, w_out, ts, exchange=mesh and mesh.gather(["w_gate", "w_up"]))
    if mesh:
        w_gate_t, w_up_t = [mesh.whole(g) for g in got]
    (g0, v, h2, w_down_t), _ = _fwd_ffn_in(x2, g2, w_gate_t, w_up_t, ts, w_down=w_down)
    tk = min(ts, SKEW_TILE)
    a, dx3, dx3b, loss, dg3, dgc, dv, dwf = _ffn_out_and_back(g0, v, x2, wf, w_down, w_down_t, g3, target, tk)
    one = dict(w_down=_matmul_tn(a, dx3b, "wgrad_down"), w_up=_matmul_tn(dv, h2, "wgrad_up"))
    parts1 = mesh and mesh.by_device(list(one.values()))
    (dg0, dx2, dx2b, dg2), got = _bwd_ffn_b(dgc, dv, wf, w_gate_t, w_up_t, x2, g2, dx3, tk,
                                            exchange=mesh and mesh.to_sibling(parts1))
    pair1 = mesh and _pair_sum(parts1, got, mesh.my_core, "rs_pair_sum_1")
    two = dict(w_gate=_matmul_tn(dg0, h2, "wgrad_gate"), w_out=_matmul_tn(y, dx2b, "wgrad_out"))
    parts2 = mesh and mesh.by_device(list(two.values()))
    (dca, du, dab, dwa, dlg, dlb, dbb), got = _bwd_mix_a(
        dx2b, jnp.swapaxes(w_out, 0, 1), z, u, wa, lg, lb, tk,
        exchange=mesh and _join([mesh.to_chips(pair1), mesh.to_sibling(parts2)]))
    (dz, dx, dg1, dwb), _ = _bwd_mix_b(dca, du, z, dab, wa, wb, w_in_t, x, g1, dx2, tk)
    small = dict(norm_mix_g=dg1, conv_a_w=dwa, conv_b_w=dwb, conv_b_b=dbb, ln_b_g=dlg, ln_b_b=dlb,
                 norm_ffn_g=dg2, conv_ffn_w=dwf, norm_final_g=dg3)
    if not mesh:
        return loss, dx, dict(w_in=_matmul_tn(dz, h1, "wgrad_in"), **one, **two), small
    big = dict(zip(one, _chip_sum(pair1, got[:2], mesh.my_chip, "rs_chip_sum_1")))
    pair2 = _pair_sum(parts2, got[2:], mesh.my_core, "rs_pair_sum_2")
    dw_in_t, got = _matmul_tn(dz, h1, "wgrad_in",
                              exchange=_join([mesh.to_chips(pair2), _gather_direct(_pack_small_grads(small, loss))]))
    big.update(zip(two, _chip_sum(pair2, got[:2], mesh.my_chip, "rs_chip_sum_2")))
    every = got[2]
    parts = mesh.by_device([dw_in_t])
    (got,) = _run_exchanges("rs_cores_last", [mesh.to_sibling(parts)])
    pair = _pair_sum(parts, got, mesh.my_core, "rs_pair_sum_last")
    (got,) = _run_exchanges("rs_chips_last", [mesh.to_chips(pair)])
    (big["w_in"],) = _chip_sum(pair, got, mesh.my_chip, "rs_chip_sum_last")
    shapes = [loss.shape if n == "loss" else small[n].shape for n in _SMALL_NAMES]
    return None, dx, big, dict(zip(_SMALL_NAMES, _sum_devices(every, shapes)))


def _pack_small_weights(conv_a_s, conv_b_s, conv_ffn_s):
    def body(a_ref, b_ref, f_ref, out):
        out[...] = jnp.zeros_like(out)
        at = 0
        for src in (a_ref, b_ref, f_ref):
            rows, cols = src.shape
            for r in range(rows):
                out[at:at + 1, 0:cols] = src[r:r + 1, :]
                at += 1

    return pl.pallas_call(body, name="small_weight_pack", out_shape=_sds((SMALL_W_ROWS, SMALL_W_COLS), F32))(
        conv_a_s, conv_b_s, conv_ffn_s)


def _unpack_small_weights(full):
    def take(r0, k, w):
        return jnp.transpose(full[:, r0:r0 + k, 0:w], (1, 0, 2)).reshape(k, N_DEV * w)

    return take(0, K_A, CONV_A_COLS), take(K_A, K_B, CONV_A_COLS), take(K_A + K_B, K_F, W_FF_COLS)


_SMALL_NAMES = ("conv_b_w", "conv_a_w", "conv_ffn_w", "norm_mix_g", "norm_ffn_g", "norm_final_g",
                "conv_b_b", "ln_b_g", "ln_b_b", "loss")
SMALL_G_ROWS = 64


def _small_rows(shapes):
    plan, at = [], 0
    for idx, (rows, cols) in enumerate(shapes):
        for r in range(rows):
            for c0 in range(0, cols, SMALL_G_COLS):
                plan.append((idx, r, c0, min(SMALL_G_COLS, cols - c0), at))
                at += 1
    assert at <= SMALL_G_ROWS
    return plan


def _pack_small_grads(small, loss):
    srcs = [loss if n == "loss" else small[n] for n in _SMALL_NAMES]
    plan = _small_rows([a.shape for a in srcs])

    def body(*refs):
        out = refs[-1]
        out[...] = jnp.zeros_like(out)
        for idx, r, c0, width, at in plan:
            out[at:at + 1, 0:width] = refs[idx][r:r + 1, c0:c0 + width]

    return pl.pallas_call(body, name="small_grad_pack", out_shape=_sds((SMALL_G_ROWS, SMALL_G_COLS), F32))(*srcs)


def kernel(x, norm_mix_g, w_in, conv_a_w, conv_b_w, conv_b_b, ln_b_g, ln_b_b, w_out, norm_ffn_g, w_gate, w_up, conv_ffn_w, w_down, norm_final_g, loss_target, m_norm_mix_g, m_w_in, m_conv_a_w, m_conv_b_w, m_conv_b_b, m_ln_b_g, m_ln_b_b, m_w_out, m_norm_ffn_g, m_w_gate, m_w_up, m_conv_ffn_w, m_w_down, m_norm_final_g, v_norm_mix_g, v_w_in, v_conv_a_w, v_conv_b_w, v_conv_b_b, v_ln_b_g, v_ln_b_b, v_w_out, v_norm_ffn_g, v_w_gate, v_w_up, v_conv_ffn_w, v_w_down, v_norm_final_g):
    ix, iy, ic = lax.axis_index("x"), lax.axis_index("y"), lax.axis_index("c")
    my_chip = (2 * ix + iy).astype(jnp.int32)
    my_core = ic.astype(jnp.int32)
    my_dev = 2 * my_chip + my_core

    weights = dict(norm_mix_g=norm_mix_g, w_in=w_in, conv_a_w=conv_a_w, conv_b_w=conv_b_w, conv_b_b=conv_b_b,
                   ln_b_g=ln_b_g, ln_b_b=ln_b_b, w_out=w_out, norm_ffn_g=norm_ffn_g, w_gate=w_gate, w_up=w_up,
                   conv_ffn_w=conv_ffn_w, w_down=w_down, norm_final_g=norm_final_g)
    m_in = dict(norm_mix_g=m_norm_mix_g, w_in=m_w_in, conv_a_w=m_conv_a_w, conv_b_w=m_conv_b_w, conv_b_b=m_conv_b_b,
                ln_b_g=m_ln_b_g, ln_b_b=m_ln_b_b, w_out=m_w_out, norm_ffn_g=m_norm_ffn_g, w_gate=m_w_gate,
                w_up=m_w_up, conv_ffn_w=m_conv_ffn_w, w_down=m_w_down, norm_final_g=m_norm_final_g)
    v_in = dict(norm_mix_g=v_norm_mix_g, w_in=v_w_in, conv_a_w=v_conv_a_w, conv_b_w=v_conv_b_w, conv_b_b=v_conv_b_b,
                ln_b_g=v_ln_b_g, ln_b_b=v_ln_b_b, w_out=v_w_out, norm_ffn_g=v_norm_ffn_g, w_gate=v_w_gate,
                w_up=v_w_up, conv_ffn_w=v_conv_ffn_w, w_down=v_w_down, norm_final_g=v_norm_final_g)
    order = list(weights)
    big_names = ("w_in", "w_gate", "w_up", "w_out", "w_down")
    transposed = ("w_in", "w_gate", "w_up")

    def shard2d(name, a):
        if name in transposed:
            return jnp.swapaxes(a[0], 0, 1)
        return a.reshape(1, a.shape[0]) if a.ndim == 1 else a.reshape(a.shape[-2:])

    def unshard2d(name, a2, like):
        if name in transposed:
            return jnp.swapaxes(a2, 0, 1)[None]
        return a2.reshape(like.shape)

    mesh = _Mesh(dict(zip(big_names, _cast_shards([shard2d(n, weights[n]) for n in big_names]))), my_chip, my_core)
    gathered, = _run_exchanges("ag_first", [_all_gather(
        [mesh.shards["w_in"], _pack_small_weights(conv_a_w[0], conv_b_w[0], conv_ffn_w[0])])])
    w_in_t = mesh.whole(gathered[0])
    wa_f, wb_f, wf_f = _unpack_small_weights(gathered[1])

    _, dx, gsum, stot = _step(
        x[0], loss_target[0], norm_mix_g, w_in_t, wa_f, wb_f, conv_b_b, ln_b_g, ln_b_b, None, norm_ffn_g,
        None, None, wf_f, None, norm_final_g.reshape(1, D_MODEL), SEQ_TILE, mesh)

    grads2d = dict(
        norm_mix_g=stot["norm_mix_g"],
        conv_a_w=lax.dynamic_slice(stot["conv_a_w"], (0, my_dev * CONV_A_COLS), (K_A, CONV_A_COLS)),
        conv_b_w=lax.dynamic_slice(stot["conv_b_w"], (0, my_dev * CONV_A_COLS), (K_B, CONV_A_COLS)),
        conv_b_b=stot["conv_b_b"], ln_b_g=stot["ln_b_g"], ln_b_b=stot["ln_b_b"],
        norm_ffn_g=stot["norm_ffn_g"],
        conv_ffn_w=lax.dynamic_slice(stot["conv_ffn_w"], (0, my_dev * W_FF_COLS), (K_F, W_FF_COLS)),
        norm_final_g=stot["norm_final_g"],
        **gsum,
    )

    def as_stored(name, a):
        return shard2d(name, a) if (name in transposed or a.ndim == 1) else a

    def back(name, a, like):
        return unshard2d(name, a, like) if (name in transposed or like.ndim == 1) else a

    updates = {}
    small_names = [n for n in order if n not in big_names]
    g_in = {n: grads2d[n].reshape(as_stored(n, weights[n]).shape) for n in order}
    for group, label in [([n], "adamw_" + n) for n in big_names] + [(small_names, "adamw_small")]:
        outs = _adamw([as_stored(n, weights[n]) for n in group], [g_in[n] for n in group],
                      [as_stored(n, m_in[n]) for n in group], [as_stored(n, v_in[n]) for n in group], label)
        for k, n in enumerate(group):
            updates[n] = outs[3 * k:3 * k + 3]
    g_out = [back(n, g_in[n], weights[n]) for n in order]
    d_out, m_out, v_out = [[back(n, updates[n][j], weights[n]) for n in order] for j in range(3)]

    return (stot["loss"][0, 0], dx[None], *g_out, *d_out, *m_out, *v_out)
```

```python
import jax
import jax.numpy as jnp
from jax import lax
from jax.experimental import pallas as pl
from jax.experimental.pallas import tpu as pltpu

F32 = jnp.float32
BF16 = jnp.bfloat16

D_MODEL = 1024
D_A = 512
D_B = 512
D_IN = 3 * D_A + 2 * D_B
D_FF = 2816
K_A = 3
K_B = 31
K_F = 3
RMS_EPS = 1e-6
LN_EPS = 1e-5

ADAM_LR = 0.001
ADAM_B1 = 0.9
ADAM_B2 = 0.999
ADAM_EPS = 1e-08
ADAM_WD = 0.01
ADAM_STEP = 10

N_DEV = 8
N_CHIP = 4
LANES = 128
SUBLANES = 8
HALO = 16
ROW_CHUNK = 64
SEQ_TILE = 512
SKEW_TILE = 256
VMEM_LIMIT = 56 * 1024 * 1024

MESH = pl.DeviceIdType.MESH

W_FF_COLS = D_FF // N_DEV
CONV_A_COLS = D_A // N_DEV
SMALL_W_ROWS = 40
SMALL_W_COLS = 384
SMALL_G_COLS = 512


def _rows(ts, c):
    return pl.BlockSpec((ts, c), lambda i: (i, 0))


def _const(shape):
    return pl.BlockSpec(shape, lambda i: (0,) * len(shape), pipeline_mode=pl.Buffered(1))


def _acc_out(shape):
    return pl.BlockSpec(shape, lambda i: (0,) * len(shape))


def _rows_at(ts, c, tile):
    return pl.BlockSpec((ts, c), lambda i: (tile(i), 0))


def _prev_at(ts, c, tile):
    return pl.BlockSpec((HALO, c), lambda i: (jnp.maximum(tile(i) * (ts // HALO) - 1, 0), 0))


def _next_at(ts, c, s, tile):
    last = s // HALO - 1
    return pl.BlockSpec((HALO, c), lambda i: (jnp.minimum((tile(i) + 1) * (ts // HALO), last), 0))


def _prev(ts, c):
    return _prev_at(ts, c, lambda i: i)


def _next(ts, c, s):
    return _next_at(ts, c, s, lambda i: i)


MXU_COLS = 256
MXU_ROWS = 256


def _col_pieces(n):
    return [(c0, min(MXU_COLS, n - c0)) for c0 in range(0, n, MXU_COLS)]


def _matmul_pieces(terms, out_ref, k_parts, w_transposed=False):
    m, n = out_ref.shape
    rows = min(MXU_ROWS, m)
    steps = []
    for lhs_ref, w_ref in terms:
        tiles = lhs_ref.shape[1] // MXU_COLS
        cuts = [MXU_COLS * (tiles * j // k_parts) for j in range(k_parts)] + [lhs_ref.shape[1]]
        steps += [(lhs_ref, w_ref, cuts[j], cuts[j + 1]) for j in range(k_parts)]

    def piece(m0, n0, width, step):
        lhs_ref, w_ref, k0, k1 = steps[step]
        if w_transposed:
            part = lax.dot_general(lhs_ref[m0:m0 + rows, k0:k1], w_ref[n0:n0 + width, k0:k1], _NT,
                                   preferred_element_type=F32)
        else:
            part = jnp.dot(lhs_ref[m0:m0 + rows, k0:k1], w_ref[k0:k1, n0:n0 + width], preferred_element_type=F32)
        if step:
            part = part + out_ref[m0:m0 + rows, n0:n0 + width]
        out_ref[m0:m0 + rows, n0:n0 + width] = part

    return [(piece, (m0, n0, w, j)) for j in range(len(steps)) for n0, w in _col_pieces(n) for m0 in range(0, m, rows)]


def _interleaved(vector_units, matmul_pieces):
    n_u, n_p = len(vector_units), len(matmul_pieces)
    done = 0
    for k, (unit, args) in enumerate(vector_units):
        while done < n_p and done * n_u <= k * n_p:
            matmul_pieces[done][0](*matmul_pieces[done][1])
            done += 1
        unit(*args)
    for fn, args in matmul_pieces[done:]:
        fn(*args)


def _params():
    return pltpu.CompilerParams(dimension_semantics=("arbitrary",), vmem_limit_bytes=VMEM_LIMIT)


def _sds(shape, dtype):
    return jax.ShapeDtypeStruct(shape, dtype)


def _sigmoid(v):
    return 0.5 * jnp.tanh(0.5 * v) + 0.5


def _conv_block(ext_ref, w_ref, r0, rc, l0, k_taps, transposed):
    acc = None
    for k in range(k_taps):
        d = (k_taps // 2 - k) if transposed else (k - k_taps // 2)
        term = ext_ref[l0 // LANES, pl.ds(r0 + HALO + d, rc), :] * w_ref[k:k + 1, l0:l0 + LANES]
        acc = term if acc is None else acc + term
    return acc


def _conv_wgrad_block(acc_ref, dout, ext_ref, r0, rc, l0, k_taps, scale=None):
    for k in range(k_taps):
        prod = dout * ext_ref[l0 // LANES, pl.ds(r0 + HALO + k - k_taps // 2, rc), :]
        part = prod.reshape(rc // SUBLANES, SUBLANES, LANES).sum(axis=0)
        if scale is not None:
            part = part * scale
        acc_ref[k, :, l0:l0 + LANES] = acc_ref[k, :, l0:l0 + LANES] + part


def _reduce_acc(out_ref, acc_ref, k_taps):
    for k in range(k_taps):
        out_ref[k:k + 1, :] = jnp.sum(acc_ref[k], axis=0, keepdims=True)


def _fold8(v):
    rc, c = v.shape
    return v.reshape(rc // SUBLANES, SUBLANES, c).sum(axis=0)


def _ext_scratch(ts, c):
    return pltpu.VMEM((c // LANES, ts + 2 * HALO, LANES), F32)


def _put_rows(ext_ref, r0, rc, val):
    for q in range(val.shape[1] // LANES):
        ext_ref[q, pl.ds(r0 + HALO, rc), :] = val[:, q * LANES:(q + 1) * LANES]


def _fill_halo(ext_ref, vals_prev, vals_next, ts, first, last):
    for q in range(vals_prev.shape[1] // LANES):
        cols = slice(q * LANES, (q + 1) * LANES)
        ext_ref[q, 0:HALO, :] = jnp.where(first, 0.0, vals_prev[:, cols])
        ext_ref[q, HALO + ts:HALO + ts + HALO, :] = jnp.where(last, 0.0, vals_next[:, cols])


def _rms_bwd_rows(dh, xf, g):
    r = lax.rsqrt(jnp.mean(xf * xf, axis=-1, keepdims=True) + RMS_EPS)
    xhat = xf * r
    dxh = dh * g
    dx = r * (dxh - xhat * jnp.mean(dxh * xhat, axis=-1, keepdims=True))
    return dx, dh * xhat


_NT = (((1,), (1,)), ((), ()))
_TN = (((0,), (0,)), ((), ()))


def _fwd_in(x, g1, w_in_t, ts, exchange=None):
    s = x.shape[0]

    def body(x_ref, g_ref, w_ref, z_ref, h_ref):
        xf = x_ref[...]
        r = lax.rsqrt(jnp.mean(xf * xf, axis=-1, keepdims=True) + RMS_EPS)
        h = (xf * r * g_ref[...]).astype(BF16)
        h_ref[...] = h
        for n0 in range(0, D_IN, 512):
            z_ref[:, n0:n0 + 512] = lax.dot_general(h, w_ref[n0:n0 + 512, :], _NT,
                                                    preferred_element_type=F32).astype(BF16)

    return _call(
        body, name="fwd_in", grid=(s // ts,),
        in_specs=[_rows(ts, D_MODEL), _const((1, D_MODEL)), _const((D_IN, D_MODEL))],
        out_specs=[_rows(ts, D_IN), _rows(ts, D_MODEL)],
        out_shape=[_sds((s, D_IN), BF16), _sds((s, D_MODEL), BF16)],
        scratch_shapes=[], args=(x, g1, w_in_t), exchange=exchange, forward_step=(s // ts) * 3 // 4)


def _p_u0(z_ref, rows):
    a_h = z_ref[rows, 0:D_A].astype(F32)
    a_c = z_ref[rows, 2 * D_A:3 * D_A].astype(F32)
    b_v = z_ref[rows, 3 * D_A:3 * D_A + D_B].astype(F32)
    b_g = z_ref[rows, 3 * D_A + D_B:D_IN].astype(F32)
    return a_c * a_h, b_v * _sigmoid(b_g)


def _layernorm_rows(u_blocks):
    tot = None
    for ub in u_blocks:
        sm = jnp.sum(ub, axis=-1, keepdims=True)
        tot = sm if tot is None else tot + sm
    mu = tot * (1.0 / D_B)
    var = None
    for ub in u_blocks:
        sq = jnp.sum((ub - mu) * (ub - mu), axis=-1, keepdims=True)
        var = sq if var is None else var + sq
    rstd = lax.rsqrt(var * (1.0 / D_B) + LN_EPS)
    return mu, rstd


def _fwd_mix(z, x, wa, wb, bb, lg, lb, w_out, ts, exchange=None):
    s = x.shape[0]
    nt = s // ts
    rc = min(ROW_CHUNK, ts)

    def body(z_ref, zp_ref, zn_ref, x_ref, wa_ref, wb_ref, bb_ref, lg_ref, lb_ref, wo_ref,
             x2_ref, y_ref, u_ref, pe_ref, ue_ref):
        i = pl.program_id(0)
        pp, up = _p_u0(zp_ref, slice(None))
        pn, un = _p_u0(zn_ref, slice(None))
        _fill_halo(pe_ref, pp, pn, ts, i == 0, i == nt - 1)
        _fill_halo(ue_ref, up, un, ts, i == 0, i == nt - 1)

        def fill(j, carry):
            r0 = pl.multiple_of(j * rc, rc)
            p, u0 = _p_u0(z_ref, pl.ds(r0, rc))
            _put_rows(pe_ref, r0, rc, p)
            _put_rows(ue_ref, r0, rc, u0)
            return carry

        lax.fori_loop(0, ts // rc, fill, 0)

        def mixer_a(r0, l0):
            rows = pl.ds(r0, rc)
            ca = _conv_block(pe_ref, wa_ref, r0, rc, l0, K_A, False)
            a_b = z_ref[rows, D_A + l0:D_A + l0 + LANES].astype(F32)
            y_ref[rows, l0:l0 + LANES] = (a_b * ca).astype(BF16)

        def conv_b(r0, l0):
            u_ref[pl.ds(r0, rc), l0:l0 + LANES] = (_conv_block(ue_ref, wb_ref, r0, rc, l0, K_B, False)
                                                   + bb_ref[:, l0:l0 + LANES])

        def norm_b(r0):
            rows = pl.ds(r0, rc)
            ubs = [u_ref[rows, l0:l0 + LANES] for l0 in range(0, D_B, LANES)]
            mu, rstd = _layernorm_rows(ubs)
            for q, l0 in enumerate(range(0, D_B, LANES)):
                t = (ubs[q] - mu) * rstd * lg_ref[:, l0:l0 + LANES] + lb_ref[:, l0:l0 + LANES]
                y_ref[rows, D_A + l0:D_A + l0 + LANES] = (t * _sigmoid(t)).astype(BF16)

        def chunk_units(r0):
            return ([(mixer_a, (r0, l0)) for l0 in range(0, D_A, LANES)]
                    + [(conv_b, (r0, l0)) for l0 in range(0, D_B, LANES)] + [(norm_b, (r0,))])

        def main(j, carry):
            for fn, args in chunk_units(pl.multiple_of(j * rc, rc)):
                fn(*args)
            return carry

        def project(m0, n0, width):
            rows = slice(m0, m0 + hrows)
            x2_ref[rows, n0:n0 + width] = x_ref[rows, n0:n0 + width] + jnp.dot(
                y_ref[rows, :], wo_ref[:, n0:n0 + width], preferred_element_type=F32)

        hrows = ts // 2
        lax.fori_loop(0, hrows // rc, main, 0, unroll=4)
        _interleaved([u for q in range(hrows // rc) for u in chunk_units(hrows + q * rc)],
                     [(project, (0, n0, w)) for n0, w in _col_pieces(D_MODEL)])
        for n0, w in _col_pieces(D_MODEL):
            project(hrows, n0, w)

    return _call(
        body, name="fwd_mix", grid=(nt,),
        in_specs=[_rows(ts, D_IN), _prev(ts, D_IN), _next(ts, D_IN, s), _rows(ts, D_MODEL),
                  _const((K_A, D_A)), _const((K_B, D_B)), _const((1, D_B)), _const((1, D_B)), _const((1, D_B)),
                  _const((D_MODEL, D_MODEL))],
        out_specs=[_rows(ts, D_MODEL), _rows(ts, D_MODEL), _rows(ts, D_B)],
        out_shape=[_sds((s, D_MODEL), F32), _sds((s, D_MODEL), BF16), _sds((s, D_B), F32)],
        scratch_shapes=[_ext_scratch(ts, D_A), _ext_scratch(ts, D_B)],
        args=(z, z, z, x, wa, wb, bb, lg, lb, w_out), exchange=exchange, forward_step=nt * 5 // 8)


def _fwd_ffn_in(x2, g2, w_gate_t, w_up_t, ts, exchange=None, w_down=None):
    s = x2.shape[0]
    half = D_FF // 2

    def body(x_ref, g_ref, wg_ref, wu_ref, g0_ref, v_ref, h_ref):
        xf = x_ref[...]
        r = lax.rsqrt(jnp.mean(xf * xf, axis=-1, keepdims=True) + RMS_EPS)
        h = (xf * r * g_ref[...]).astype(BF16)
        h_ref[...] = h
        for n0 in range(0, D_FF, half):
            g0_ref[:, n0:n0 + half] = lax.dot_general(h, wg_ref[n0:n0 + half, :], _NT,
                                                      preferred_element_type=F32).astype(BF16)
            v_ref[:, n0:n0 + half] = lax.dot_general(h, wu_ref[n0:n0 + half, :], _NT,
                                                     preferred_element_type=F32).astype(BF16)

    def body_and_transpose(x_ref, g_ref, wg_ref, wu_ref, wd_ref, g0_ref, v_ref, h_ref, wdt_ref):
        @pl.when(pl.program_id(0) < n_blk)
        def _():
            wdt_ref[...] = wd_ref[...].T

        body(x_ref, g_ref, wg_ref, wu_ref, g0_ref, v_ref, h_ref)

    in_specs = [_rows(ts, D_MODEL), _const((1, D_MODEL)), _const((D_FF, D_MODEL)), _const((D_FF, D_MODEL))]
    out_specs = [_rows(ts, D_FF), _rows(ts, D_FF), _rows(ts, D_MODEL)]
    out_shape = [_sds((s, D_FF), BF16), _sds((s, D_FF), BF16), _sds((s, D_MODEL), BF16)]
    n_blk = D_FF // MXU_COLS
    if w_down is None or s // ts < n_blk:
        outs, got = _call(body, name="fwd_ffn_in", grid=(s // ts,), in_specs=in_specs, out_specs=out_specs,
                          out_shape=out_shape, scratch_shapes=[], args=(x2, g2, w_gate_t, w_up_t), exchange=exchange,
                          forward_step=(s // ts) // 2)
        return outs + [None if w_down is None else _transposed(w_down)], got
    blk = lambda i: jnp.minimum(i, n_blk - 1)
    return _call(
        body_and_transpose, name="fwd_ffn_in", grid=(s // ts,),
        in_specs=in_specs + [pl.BlockSpec((MXU_COLS, D_MODEL), lambda i: (blk(i), 0))],
        out_specs=out_specs + [pl.BlockSpec((D_MODEL, MXU_COLS), lambda i: (0, blk(i)))],
        out_shape=out_shape + [_sds((D_MODEL, D_FF), BF16)],
        scratch_shapes=[], args=(x2, g2, w_gate_t, w_up_t, w_down), exchange=exchange, forward_step=(s // ts) // 2)


def _ffn_out_and_back(g0, v, x2, wf, w_down, w_down_t, g3, target, ts):
    s = x2.shape[0]
    nt = s // ts
    rc = min(ROW_CHUNK, ts // 2)

    def body(g0_ref, gp_ref, gn_ref, v_ref, x2_ref, wf_ref, wd_ref, wdt_ref, g3_ref, t_ref,
             a_ref, dx3_ref, dx3b_ref, loss_ref, dg3_ref, dg_ref, dv_ref, dwf_ref,
             ge_ref, silu_ref, dsv_ref, da_ref, acc_ref, p_ref, sums_ref):
        i = pl.program_id(0)

        @pl.when(i == 0)
        def _():
            acc_ref[...] = jnp.zeros_like(acc_ref)
            sums_ref[...] = jnp.zeros_like(sums_ref)

        _fill_halo(ge_ref, gp_ref[...].astype(F32), gn_ref[...].astype(F32), ts, i == 0, i == nt - 1)

        def fill(j, carry):
            r0 = pl.multiple_of(j * rc, rc)
            _put_rows(ge_ref, r0, rc, g0_ref[pl.ds(r0, rc), :].astype(F32))
            return carry

        lax.fori_loop(0, ts // rc, fill, 0)

        def act(r0, l0):
            rows = slice(r0, r0 + rc)
            g = _conv_block(ge_ref, wf_ref, r0, rc, l0, K_F, False)
            vv = v_ref[rows, l0:l0 + LANES].astype(F32)
            sg = _sigmoid(g)
            silu = g * sg
            a_ref[rows, l0:l0 + LANES] = (silu * vv).astype(BF16)
            silu_ref[rows, l0:l0 + LANES] = silu
            dsv_ref[rows, l0:l0 + LANES] = (sg + silu * (1.0 - sg)) * vv

        def tail(r0):
            rows = slice(r0, r0 + rc)
            x3 = x2_ref[rows, :] + p_ref[rows, :]
            r = lax.rsqrt(jnp.mean(x3 * x3, axis=-1, keepdims=True) + RMS_EPS)
            xhat = x3 * r
            diff = xhat * g3_ref[...] - t_ref[rows, :]
            dout = diff * (1.0 / D_MODEL)
            dxh = dout * g3_ref[...]
            dx3 = r * (dxh - xhat * jnp.mean(dxh * xhat, axis=-1, keepdims=True))
            dx3_ref[rows, :] = dx3
            dx3b_ref[rows, :] = dx3.astype(BF16)
            sums_ref[0] = sums_ref[0] + _fold8(diff * diff)
            sums_ref[1] = sums_ref[1] + _fold8(dout * xhat)

        def back(r0, l0):
            rows = slice(r0, r0 + rc)
            da = da_ref[rows, l0:l0 + LANES]
            dv_ref[rows, l0:l0 + LANES] = (da * silu_ref[rows, l0:l0 + LANES]).astype(BF16)
            dgg = da * dsv_ref[rows, l0:l0 + LANES]
            dg_ref[rows, l0:l0 + LANES] = dgg.astype(BF16)
            _conv_wgrad_block(acc_ref, dgg, ge_ref, r0, rc, l0, K_F)

        hrows = ts // 2

        def units(fn, h, per_lane_block):
            starts = [h * hrows + q * rc for q in range(hrows // rc)]
            if per_lane_block:
                return [(fn, (r0, l0)) for r0 in starts for l0 in range(0, D_FF, LANES)]
            return [(fn, (r0,)) for r0 in starts]

        def rows_of(ref, h):
            return ref.at[pl.ds(h * hrows, hrows), :]

        def product(h):
            return _matmul_pieces([(rows_of(a_ref, h), wd_ref)], rows_of(p_ref, h), 2)

        def grad_a(h):
            return _matmul_pieces([(rows_of(dx3b_ref, h), wdt_ref)], rows_of(da_ref, h), 1)

        _interleaved(units(act, 0, True), [])
        _interleaved(units(act, 1, True), product(0))
        _interleaved(units(tail, 0, False), product(1))
        _interleaved(units(tail, 1, False), grad_a(0))
        _interleaved(units(back, 0, True), grad_a(1))
        _interleaved(units(back, 1, True), [])

        @pl.when(i == nt - 1)
        def _():
            _reduce_acc(dwf_ref, acc_ref, K_F)
            loss_ref[...] = (0.5 / D_MODEL) * jnp.sum(sums_ref[0], keepdims=True)
            dg3_ref[...] = jnp.sum(sums_ref[1], axis=0, keepdims=True)

    return pl.pallas_call(
        body, name="ffn_out_and_back", grid=(nt,),
        in_specs=[_rows(ts, D_FF), _prev(ts, D_FF), _next(ts, D_FF, s), _rows(ts, D_FF), _rows(ts, D_MODEL),
                  _const((K_F, D_FF)), _const((D_FF, D_MODEL)), _const((D_MODEL, D_FF)), _const((1, D_MODEL)),
                  _rows(ts, D_MODEL)],
        out_specs=[_rows(ts, D_FF), _rows(ts, D_MODEL), _rows(ts, D_MODEL), _acc_out((1, 1)), _acc_out((1, D_MODEL)),
                   _rows(ts, D_FF), _rows(ts, D_FF), _acc_out((K_F, D_FF))],
        out_shape=[_sds((s, D_FF), BF16), _sds((s, D_MODEL), F32), _sds((s, D_MODEL), BF16),
                   _sds((1, 1), F32), _sds((1, D_MODEL), F32),
                   _sds((s, D_FF), BF16), _sds((s, D_FF), BF16), _sds((K_F, D_FF), F32)],
        scratch_shapes=[_ext_scratch(ts, D_FF), pltpu.VMEM((ts, D_FF), F32), pltpu.VMEM((ts, D_FF), F32),
                        pltpu.VMEM((ts, D_FF), F32), pltpu.VMEM((K_F, SUBLANES, D_FF), F32),
                        pltpu.VMEM((ts, D_MODEL), F32), pltpu.VMEM((2, SUBLANES, D_MODEL), F32)],
        compiler_params=_params(),
    )(g0, g0, g0, v, x2, wf, w_down, w_down_t, g3, target)


def _bwd_ffn_b(dg, dv, wf, w_gate, w_up, x2, g2, dx3, ts, exchange=None):
    s = x2.shape[0]
    nt = s // ts
    rc = min(ROW_CHUNK, ts)
    n_sub = ts // rc

    def body(dg_ref, dgp_ref, dgn_ref, dv_ref, wf_ref, wg_ref, wu_ref, x2_ref, g2_ref, dx3_ref,
             dg0_ref, dx2_ref, dx2b_ref, dgn2_ref, dge_ref, dg8_ref, a0_ref, a1_ref, p0_ref, p1_ref):
        i = pl.program_id(0)
        vt = jnp.minimum(i, nt - 1)
        live = (i >= 2).astype(F32)

        @pl.when(i == 0)
        def _():
            dg8_ref[...] = jnp.zeros_like(dg8_ref)
            a1_ref[...] = jnp.zeros_like(a1_ref)
            p1_ref[...] = jnp.zeros_like(p1_ref)

        _fill_halo(dge_ref, dgp_ref[...].astype(F32), dgn_ref[...].astype(F32), ts, vt == 0, vt == nt - 1)

        def fill(j, carry):
            r0 = pl.multiple_of(j * rc, rc)
            _put_rows(dge_ref, r0, rc, dg_ref[pl.ds(r0, rc), :].astype(F32))
            return carry

        lax.fori_loop(0, n_sub, fill, 0)

        def stage(a_new, a_old, p_new, p_old):
            def conv_t(r0, l0):
                rows = slice(r0, r0 + rc)
                dg0 = _conv_block(dge_ref, wf_ref, r0, rc, l0, K_F, True).astype(BF16)
                dg0_ref[rows, l0:l0 + LANES] = dg0
                a_new[rows, l0:l0 + LANES] = dg0

            def tail(r0):
                rows = slice(r0, r0 + rc)
                dx, dgrow = _rms_bwd_rows(p_old[rows, :], x2_ref[rows, :], g2_ref[...])
                dx2 = dx3_ref[rows, :] + dx
                dx2_ref[rows, :] = dx2
                dx2b_ref[rows, :] = dx2.astype(BF16)
                dg8_ref[...] = dg8_ref[...] + _fold8(dgrow) * live

            units = []
            for q in range(n_sub):
                units += [(conv_t, (q * rc, l0)) for l0 in range(0, D_FF, LANES)]
                units.append((tail, (q * rc,)))
            _interleaved(units, _matmul_pieces([(a_old, wg_ref), (dv_ref, wu_ref)], p_new, 2))

        @pl.when(i % 2 == 0)
        def _():
            stage(a0_ref, a1_ref, p0_ref, p1_ref)

        @pl.when(i % 2 == 1)
        def _():
            stage(a1_ref, a0_ref, p1_ref, p0_ref)

        @pl.when(i == nt + 1)
        def _():
            dgn2_ref[...] = jnp.sum(dg8_ref[...], axis=0, keepdims=True)

    vtile = lambda i: jnp.minimum(i, nt - 1)
    mtile = lambda i: jnp.clip(i - 1, 0, nt - 1)
    ttile = lambda i: jnp.clip(i - 2, 0, nt - 1)
    return _call(
        body, name="bwd_ffn_b", grid=(nt + 2,),
        in_specs=[_rows_at(ts, D_FF, vtile), _prev_at(ts, D_FF, vtile), _next_at(ts, D_FF, s, vtile),
                  _rows_at(ts, D_FF, mtile), _const((K_F, D_FF)),
                  _const((D_FF, D_MODEL)), _const((D_FF, D_MODEL)), _rows_at(ts, D_MODEL, ttile), _const((1, D_MODEL)),
                  _rows_at(ts, D_MODEL, ttile)],
        out_specs=[_rows_at(ts, D_FF, vtile), _rows_at(ts, D_MODEL, ttile), _rows_at(ts, D_MODEL, ttile),
                   _acc_out((1, D_MODEL))],
        out_shape=[_sds((s, D_FF), BF16), _sds((s, D_MODEL), F32), _sds((s, D_MODEL), BF16), _sds((1, D_MODEL), F32)],
        scratch_shapes=[_ext_scratch(ts, D_FF), pltpu.VMEM((SUBLANES, D_MODEL), F32),
                        pltpu.VMEM((ts, D_FF), BF16), pltpu.VMEM((ts, D_FF), BF16),
                        pltpu.VMEM((ts, D_MODEL), F32), pltpu.VMEM((ts, D_MODEL), F32)],
        args=(dg, dg, dg, dv, wf, w_gate, w_up, x2, g2, dx3), exchange=exchange)


def _bwd_mix_a(dx2b, w_out_t, z, u, wa, lg, lb, ts, exchange=None):
    s = dx2b.shape[0]
    nt = s // ts
    rc = min(ROW_CHUNK, ts)
    n_sub = ts // rc

    def body(dx_ref, wo_ref, z_ref, zp_ref, zn_ref, u_ref, wa_ref, lg_ref, lb_ref,
             dca_ref, du_ref, dab_ref, dwa_ref, dlg_ref, dlb_ref, dbb_ref, pe_ref, dy0_ref, dy1_ref, acc_ref, sacc_ref):
        i = pl.program_id(0)
        t = jnp.maximum(i - 1, 0)

        @pl.when(i == 0)
        def _():
            acc_ref[...] = jnp.zeros_like(acc_ref)
            sacc_ref[...] = jnp.zeros_like(sacc_ref)
            dy1_ref[...] = jnp.zeros_like(dy1_ref)

        pp, _ = _p_u0(zp_ref, slice(None))
        pn, _ = _p_u0(zn_ref, slice(None))
        _fill_halo(pe_ref, pp, pn, ts, t == 0, t == nt - 1)

        def fill(j, carry):
            r0 = pl.multiple_of(j * rc, rc)
            rows = pl.ds(r0, rc)
            _put_rows(pe_ref, r0, rc, z_ref[rows, 2 * D_A:3 * D_A].astype(F32) * z_ref[rows, 0:D_A].astype(F32))
            return carry

        lax.fori_loop(0, n_sub, fill, 0)

        def stage(dy_new, dy_old):
            def piece(m0, n0, width):
                dy_new[m0:m0 + MXU_ROWS, n0:n0 + width] = jnp.dot(
                    dx_ref[m0:m0 + MXU_ROWS, :], wo_ref[:, n0:n0 + width], preferred_element_type=F32).astype(BF16)

            units = []
            for q in range(n_sub):
                units += [(mixer_a, (dy_old, q * rc, l0)) for l0 in range(0, D_A, LANES)]
                units.append((mixer_b, (dy_old, q * rc)))
            _interleaved(units, [(piece, (m0, n0, w)) for n0, w in _col_pieces(D_MODEL) for m0 in range(0, ts, MXU_ROWS)])

        def mixer_a(dy_ref, r0, l0):
            rows = slice(r0, r0 + rc)
            ca = _conv_block(pe_ref, wa_ref, r0, rc, l0, K_A, False)
            a_b = z_ref[rows, D_A + l0:D_A + l0 + LANES].astype(F32)
            dya = dy_ref[rows, l0:l0 + LANES].astype(F32)
            dab_ref[rows, l0:l0 + LANES] = (dya * ca).astype(BF16)
            dca = dya * a_b
            dca_ref[rows, l0:l0 + LANES] = dca
            _conv_wgrad_block(acc_ref, dca, pe_ref, r0, rc, l0, K_A)

        def mixer_b(dy_ref, r0):
            rows = slice(r0, r0 + rc)
            ubs = [u_ref[rows, l0:l0 + LANES] for l0 in range(0, D_B, LANES)]
            mu, rstd = _layernorm_rows(ubs)
            ns, dns = [], []
            m1 = None
            m2 = None
            for q, l0 in enumerate(range(0, D_B, LANES)):
                n = (ubs[q] - mu) * rstd
                lgq = lg_ref[:, l0:l0 + LANES]
                t = n * lgq + lb_ref[:, l0:l0 + LANES]
                sg = _sigmoid(t)
                dt = dy_ref[rows, D_A + l0:D_A + l0 + LANES].astype(F32) * (sg * (1.0 + t * (1.0 - sg)))
                dn = dt * lgq
                ns.append(n)
                dns.append(dn)
                s1 = jnp.sum(dn, axis=-1, keepdims=True)
                s2 = jnp.sum(dn * n, axis=-1, keepdims=True)
                m1 = s1 if m1 is None else m1 + s1
                m2 = s2 if m2 is None else m2 + s2
                sacc_ref[0, :, l0:l0 + LANES] = sacc_ref[0, :, l0:l0 + LANES] + _fold8(dt * n)
                sacc_ref[1, :, l0:l0 + LANES] = sacc_ref[1, :, l0:l0 + LANES] + _fold8(dt)
            m1 = m1 * (1.0 / D_B)
            m2 = m2 * (1.0 / D_B)
            for q, l0 in enumerate(range(0, D_B, LANES)):
                du = rstd * (dns[q] - m1 - ns[q] * m2)
                du_ref[rows, l0:l0 + LANES] = du
                sacc_ref[2, :, l0:l0 + LANES] = sacc_ref[2, :, l0:l0 + LANES] + _fold8(du)

        @pl.when(i % 2 == 0)
        def _():
            stage(dy0_ref, dy1_ref)

        @pl.when(i % 2 == 1)
        def _():
            stage(dy1_ref, dy0_ref)

        @pl.when(i == nt)
        def _():
            _reduce_acc(dwa_ref, acc_ref, K_A)
            dlg_ref[...] = jnp.sum(sacc_ref[0], axis=0, keepdims=True)
            dlb_ref[...] = jnp.sum(sacc_ref[1], axis=0, keepdims=True)
            dbb_ref[...] = jnp.sum(sacc_ref[2], axis=0, keepdims=True)

    cur = lambda i: jnp.minimum(i, nt - 1)
    old = lambda i: jnp.maximum(i - 1, 0)
    return _call(
        body, name="bwd_mix_a", grid=(nt + 1,),
        in_specs=[_rows_at(ts, D_MODEL, cur), _const((D_MODEL, D_MODEL)), _rows_at(ts, D_IN, old), _prev_at(ts, D_IN, old),
                  _next_at(ts, D_IN, s, old), _rows_at(ts, D_B, old), _const((K_A, D_A)), _const((1, D_B)), _const((1, D_B))],
        out_specs=[_rows_at(ts, D_A, old), _rows_at(ts, D_B, old), _rows_at(ts, D_A, old), _acc_out((K_A, D_A)),
                   _acc_out((1, D_B)), _acc_out((1, D_B)), _acc_out((1, D_B))],
        out_shape=[_sds((s, D_A), F32), _sds((s, D_B), F32), _sds((s, D_A), BF16), _sds((K_A, D_A), F32),
                   _sds((1, D_B), F32), _sds((1, D_B), F32), _sds((1, D_B), F32)],
        scratch_shapes=[_ext_scratch(ts, D_A), pltpu.VMEM((ts, D_MODEL), BF16), pltpu.VMEM((ts, D_MODEL), BF16),
                        pltpu.VMEM((K_A, SUBLANES, D_A), F32), pltpu.VMEM((3, SUBLANES, D_B), F32)],
        args=(dx2b, w_out_t, z, z, z, u, wa, lg, lb), exchange=exchange)


def _bwd_mix_b(dca, du, z, dab, wa, wb, w_in, x, g1, dx2, ts, exchange=None):
    s = x.shape[0]
    nt = s // ts
    rc = min(ROW_CHUNK, ts)
    n_sub = ts // rc

    def body(dca_ref, dcap_ref, dcan_ref, du_ref, dup_ref, dun_ref, z_ref, zp_ref, zn_ref, dab_ref,
             wa_ref, wb_ref, wi_ref, x_ref, g1_ref, dx2_ref,
             dz_ref, dx_ref, dg1_ref, dwb_ref, dcae_ref, due_ref, ue_ref, acc_ref, dg8_ref,
             dz0_ref, dz1_ref, dh0_ref, dh1_ref):
        i = pl.program_id(0)

        @pl.when(i == 0)
        def _():
            acc_ref[...] = jnp.zeros_like(acc_ref)
            dg8_ref[...] = jnp.zeros_like(dg8_ref)
            dz1_ref[...] = jnp.zeros_like(dz1_ref)
            dh1_ref[...] = jnp.zeros_like(dh1_ref)

        vt = jnp.minimum(i, nt - 1)
        first = vt == 0
        last = vt == nt - 1
        live = (i < nt).astype(F32)
        _fill_halo(dcae_ref, dcap_ref[...], dcan_ref[...], ts, first, last)
        _fill_halo(due_ref, dup_ref[...], dun_ref[...], ts, first, last)
        _, up = _p_u0(zp_ref, slice(None))
        _, un = _p_u0(zn_ref, slice(None))
        _fill_halo(ue_ref, up, un, ts, first, last)

        def fill(j, carry):
            r0 = pl.multiple_of(j * rc, rc)
            rows = pl.ds(r0, rc)
            _put_rows(dcae_ref, r0, rc, dca_ref[rows, :])
            _put_rows(due_ref, r0, rc, du_ref[rows, :])
            b_v = z_ref[rows, 3 * D_A:3 * D_A + D_B].astype(F32)
            b_g = z_ref[rows, 3 * D_A + D_B:D_IN].astype(F32)
            _put_rows(ue_ref, r0, rc, b_v * _sigmoid(b_g))
            return carry

        lax.fori_loop(0, n_sub, fill, 0)

        def stage(dz_new, dz_old, dh_new, dh_old):
            def put(rows, c0, val):
                dz_ref[rows, c0:c0 + LANES] = val
                dz_new[rows, c0:c0 + LANES] = val

            def mixer_a(r0, l0):
                rows = slice(r0, r0 + rc)
                dp = _conv_block(dcae_ref, wa_ref, r0, rc, l0, K_A, True)
                a_h = z_ref[rows, l0:l0 + LANES].astype(F32)
                a_c = z_ref[rows, 2 * D_A + l0:2 * D_A + l0 + LANES].astype(F32)
                put(rows, l0, (dp * a_c).astype(BF16))
                put(rows, D_A + l0, dab_ref[rows, l0:l0 + LANES])
                put(rows, 2 * D_A + l0, (dp * a_h).astype(BF16))

            def mixer_b(r0, l0):
                rows = slice(r0, r0 + rc)
                du0 = _conv_block(due_ref, wb_ref, r0, rc, l0, K_B, True)
                b_v = z_ref[rows, 3 * D_A + l0:3 * D_A + l0 + LANES].astype(F32)
                b_g = z_ref[rows, 3 * D_A + D_B + l0:3 * D_A + D_B + l0 + LANES].astype(F32)
                sg = _sigmoid(b_g)
                put(rows, 3 * D_A + l0, (du0 * sg).astype(BF16))
                put(rows, 3 * D_A + D_B + l0, (du0 * b_v * (sg * (1.0 - sg))).astype(BF16))
                _conv_wgrad_block(acc_ref, du_ref[rows, l0:l0 + LANES], ue_ref, r0, rc, l0, K_B, live)

            def tail(r0):
                rows = slice(r0, r0 + rc)
                dx, dgrow = _rms_bwd_rows(dh_old[rows, :], x_ref[rows, :], g1_ref[...])
                dx_ref[rows, :] = dx2_ref[rows, :] + dx
                dg8_ref[...] = dg8_ref[...] + _fold8(dgrow)

            units = []
            for q in range(n_sub):
                units += [(mixer_a, (q * rc, l0)) for l0 in range(0, D_A, LANES)]
                units += [(mixer_b, (q * rc, l0)) for l0 in range(0, D_B, LANES)]
                units.append((tail, (q * rc,)))
            _interleaved(units, _matmul_pieces([(dz_old, wi_ref)], dh_new, 2))

        @pl.when(i % 2 == 0)
        def _():
            stage(dz0_ref, dz1_ref, dh0_ref, dh1_ref)

        @pl.when(i % 2 == 1)
        def _():
            stage(dz1_ref, dz0_ref, dh1_ref, dh0_ref)

        @pl.when(i == nt + 1)
        def _():
            _reduce_acc(dwb_ref, acc_ref, K_B)
            dg1_ref[...] = jnp.sum(dg8_ref[...], axis=0, keepdims=True)

    vtile = lambda i: jnp.minimum(i, nt - 1)
    ttile = lambda i: jnp.clip(i - 2, 0, nt - 1)
    return _call(
        body, name="bwd_mix_b", grid=(nt + 2,),
        in_specs=[_rows_at(ts, D_A, vtile), _prev_at(ts, D_A, vtile), _next_at(ts, D_A, s, vtile),
                  _rows_at(ts, D_B, vtile), _prev_at(ts, D_B, vtile), _next_at(ts, D_B, s, vtile),
                  _rows_at(ts, D_IN, vtile), _prev_at(ts, D_IN, vtile), _next_at(ts, D_IN, s, vtile), _rows_at(ts, D_A, vtile),
                  _const((K_A, D_A)), _const((K_B, D_B)), _const((D_IN, D_MODEL)), _rows_at(ts, D_MODEL, ttile),
                  _const((1, D_MODEL)), _rows_at(ts, D_MODEL, ttile)],
        out_specs=[_rows_at(ts, D_IN, vtile), _rows_at(ts, D_MODEL, ttile), _acc_out((1, D_MODEL)), _acc_out((K_B, D_B))],
        out_shape=[_sds((s, D_IN), BF16), _sds((s, D_MODEL), F32), _sds((1, D_MODEL), F32), _sds((K_B, D_B), F32)],
        scratch_shapes=[_ext_scratch(ts, D_A), _ext_scratch(ts, D_B), _ext_scratch(ts, D_B),
                        pltpu.VMEM((K_B, SUBLANES, D_B), F32), pltpu.VMEM((SUBLANES, D_MODEL), F32),
                        pltpu.VMEM((ts, D_IN), BF16), pltpu.VMEM((ts, D_IN), BF16),
                        pltpu.VMEM((ts, D_MODEL), F32), pltpu.VMEM((ts, D_MODEL), F32)],
        args=(dca, dca, dca, du, du, du, z, z, z, dab, wa, wb, w_in, x, g1, dx2), exchange=exchange)


def _matmul_tn(a, b, name, exchange=None):
    s, m = a.shape
    n = b.shape[1]
    tk = min(1024, s)
    nk = s // tk
    tm = 256

    def body(a_ref, b_ref, o_ref, acc_ref):
        k = pl.program_id(0)

        @pl.when(k == 0)
        def _():
            acc_ref[...] = jnp.zeros_like(acc_ref)

        for m0 in range(0, m, tm):
            acc_ref[m0:m0 + tm, :] = acc_ref[m0:m0 + tm, :] + lax.dot_general(
                a_ref[:, m0:m0 + tm], b_ref[...], _TN, preferred_element_type=F32)

        @pl.when(k == nk - 1)
        def _():
            o_ref[...] = acc_ref[...].astype(BF16)

    (out,), got = _call(
        body, name=name, grid=(nk,),
        in_specs=[_rows(tk, m), _rows(tk, n)],
        out_specs=[_acc_out((m, n))],
        out_shape=[_sds((m, n), BF16)],
        scratch_shapes=[pltpu.VMEM((m, n), F32)], args=(a, b), exchange=exchange)
    return out if exchange is None else (out, got)


CHIP_RELS = ((1, 0, 0), (0, 1, 0), (1, 1, 0))
CORE_RELS = ((0, 0, 1),)
ALL_RELS = ((0, 0, 1), (0, 1, 0), (0, 1, 1), (1, 0, 0), (1, 0, 1), (1, 1, 0), (1, 1, 1))


def _chip_slot(dev):
    return 2 * dev[0] + dev[1]


def _dev_slot(dev):
    return 4 * dev[0] + 2 * dev[1] + dev[2]


def _me():
    return (lax.axis_index("x"), lax.axis_index("y"), lax.axis_index("c"))


def _peer(me, rel):
    return tuple((1 - me[a]) if rel[a] else me[a] for a in range(3))


_ANY = pl.BlockSpec(memory_space=pl.ANY)


class _Exchange:
    def __init__(self, inputs, out_shape, scratch, start, finish, forward=None):
        self.inputs, self.out_shape, self.scratch = list(inputs), list(out_shape), list(scratch)
        self.start, self.finish, self.forward = start, finish, forward


def _all_gather(payloads):
    n_p = len(payloads)
    n_k = 1 + 2 * len(CHIP_RELS)

    def copy(srcs, dsts, sems, p, k, block_dev, to, from_src):
        blk = dsts[p].at[_dev_slot(block_dev)]
        return pltpu.make_async_remote_copy(
            src_ref=srcs[p] if from_src else blk, dst_ref=blk,
            send_sem=sems[0].at[n_k * p + k], recv_sem=sems[1].at[n_k * p + k], device_id=to, device_id_type=MESH)

    def own_copy(srcs, dsts, sems, p):
        return pltpu.make_async_copy(srcs[p], dsts[p].at[_dev_slot(_me())], sems[2].at[p])

    def start(srcs, dsts, sems):
        me = _me()
        for p in range(n_p):
            own_copy(srcs, dsts, sems, p).start()
        for j, rel in enumerate(CHIP_RELS):
            for p in range(n_p):
                copy(srcs, dsts, sems, p, 1 + j, me, _peer(me, rel), True).start()
        for p in range(n_p):
            copy(srcs, dsts, sems, p, 0, me, _peer(me, CORE_RELS[0]), True).start()

    def forward(srcs, dsts, sems):
        me = _me()
        sibling = _peer(me, CORE_RELS[0])
        for j, rel in enumerate(CHIP_RELS):
            other = _peer(me, rel)
            for p in range(n_p):
                copy(srcs, dsts, sems, p, 1 + j, other, me, False).wait_recv()
                copy(srcs, dsts, sems, p, 4 + j, other, sibling, False).start()

    def finish(srcs, dsts, sems):
        me = _me()
        sibling = _peer(me, CORE_RELS[0])
        for p in range(n_p):
            copy(srcs, dsts, sems, p, 0, sibling, me, False).wait_recv()
        for j, rel in enumerate(CHIP_RELS):
            for p in range(n_p):
                copy(srcs, dsts, sems, p, 4 + j, _peer(sibling, rel), me, False).wait_recv()
        for p in range(n_p):
            own_copy(srcs, dsts, sems, p).wait()
            copy(srcs, dsts, sems, p, 0, me, sibling, True).wait_send()
            for j, rel in enumerate(CHIP_RELS):
                copy(srcs, dsts, sems, p, 1 + j, me, _peer(me, rel), True).wait_send()
                copy(srcs, dsts, sems, p, 4 + j, _peer(me, rel), sibling, False).wait_send()

    return _Exchange(
        payloads, [_sds((N_DEV,) + p.shape, p.dtype) for p in payloads],
        [pltpu.SemaphoreType.DMA((n_p * n_k,)), pltpu.SemaphoreType.DMA((n_p * n_k,)), pltpu.SemaphoreType.DMA((n_p,))],
        start, finish, forward)


def _gather_direct(payload):
    n_r = len(ALL_RELS)

    def copies(srcs, dsts, sems):
        me = _me()
        mine = dsts[0].at[_dev_slot(me)]
        own = pltpu.make_async_copy(srcs[0], mine, sems[2].at[0])
        remote = [pltpu.make_async_remote_copy(src_ref=srcs[0], dst_ref=mine, send_sem=sems[0].at[k], recv_sem=sems[1].at[k],
                                               device_id=_peer(me, rel), device_id_type=MESH)
                  for k, rel in enumerate(ALL_RELS)]
        return [own] + remote

    def start(srcs, dsts, sems):
        for cp in copies(srcs, dsts, sems):
            cp.start()

    def finish(srcs, dsts, sems):
        for cp in copies(srcs, dsts, sems):
            cp.wait()

    return _Exchange([payload], [_sds((N_DEV,) + payload.shape, payload.dtype)],
                     [pltpu.SemaphoreType.DMA((n_r,)), pltpu.SemaphoreType.DMA((n_r,)), pltpu.SemaphoreType.DMA((1,))],
                     start, finish)


def _scatter_exchange(payloads, rels, src_view, view_shapes):
    n_p = len(payloads)
    n_r = len(rels)

    def copies(srcs, dsts, sems):
        me = _me()
        out = []
        for k, rel in enumerate(rels):
            peer = _peer(me, rel)
            for p in range(n_p):
                out.append(pltpu.make_async_remote_copy(
                    src_ref=src_view(srcs[p], peer), dst_ref=dsts[p].at[k],
                    send_sem=sems[0].at[p * n_r + k], recv_sem=sems[1].at[p * n_r + k],
                    device_id=peer, device_id_type=MESH))
        return out

    def start(srcs, dsts, sems):
        for cp in copies(srcs, dsts, sems):
            cp.start()

    def finish(srcs, dsts, sems):
        for cp in copies(srcs, dsts, sems):
            cp.wait()

    return _Exchange(payloads, [_sds((n_r,) + vs, p.dtype) for vs, p in zip(view_shapes, payloads)],
                     [pltpu.SemaphoreType.DMA((n_p * n_r,)), pltpu.SemaphoreType.DMA((n_p * n_r,))], start, finish)


def _split_refs(refs, sizes):
    out, at = [], 0
    for n in sizes:
        out.append(refs[at:at + n])
        at += n
    return out


def _join(exchanges):
    n_in = [len(e.inputs) for e in exchanges]
    n_out = [len(e.out_shape) for e in exchanges]
    n_sc = [len(e.scratch) for e in exchanges]

    def phase(name):
        def run(ins, outs, scs):
            for e, i, o, s in zip(exchanges, _split_refs(ins, n_in), _split_refs(outs, n_out), _split_refs(scs, n_sc)):
                if getattr(e, name) is not None:
                    getattr(e, name)(i, o, s)
        return run

    return _Exchange([a for e in exchanges for a in e.inputs], [s for e in exchanges for s in e.out_shape],
                     [s for e in exchanges for s in e.scratch], phase("start"), phase("finish"),
                     phase("forward") if any(e.forward is not None for e in exchanges) else None)


def _run_exchanges(name, exchanges):
    n_in = [len(e.inputs) for e in exchanges]
    n_out = [len(e.out_shape) for e in exchanges]
    n_sc = [len(e.scratch) for e in exchanges]

    def body(*refs):
        ins, outs, scs = _split_refs(refs, [sum(n_in), sum(n_out), sum(n_sc)])
        parts = list(zip(exchanges, _split_refs(ins, n_in), _split_refs(outs, n_out), _split_refs(scs, n_sc)))
        for e, i, o, s in parts:
            e.start(i, o, s)
        for e, i, o, s in parts:
            if e.forward is not None:
                e.forward(i, o, s)
        for e, i, o, s in parts:
            e.finish(i, o, s)

    outs = pl.pallas_call(
        body, name=name, in_specs=[_ANY] * sum(n_in), out_specs=[_ANY] * sum(n_out),
        out_shape=[sd for e in exchanges for sd in e.out_shape],
        scratch_shapes=[sc for e in exchanges for sc in e.scratch],
    )(*[a for e in exchanges for a in e.inputs])
    return _split_refs(list(outs), n_out)


def _call(body, *, name, grid, in_specs, out_specs, out_shape, scratch_shapes, args, exchange=None, forward_step=None):
    n_in, n_out, n_sc = len(in_specs), len(out_specs), len(scratch_shapes)
    if exchange is None:
        outs = pl.pallas_call(body, name=name, grid=grid, in_specs=in_specs, out_specs=out_specs, out_shape=out_shape,
                              scratch_shapes=scratch_shapes, compiler_params=_params())(*args)
        return list(outs), []
    e = exchange
    sizes = [n_in, len(e.inputs), n_out, len(e.out_shape), n_sc, len(e.scratch)]
    last = grid[0] - 1

    def wrapped(*refs):
        a, ei, o, eo, sc, es = _split_refs(refs, sizes)
        i = pl.program_id(0)

        @pl.when(i == 0)
        def _():
            e.start(ei, eo, es)

        if e.forward is not None:
            @pl.when(i == forward_step)
            def _():
                e.forward(ei, eo, es)

        body(*a, *o, *sc)

        @pl.when(i == last)
        def _():
            e.finish(ei, eo, es)

    outs = pl.pallas_call(
        wrapped, name=name, grid=grid,
        in_specs=list(in_specs) + [_ANY] * len(e.inputs), out_specs=list(out_specs) + [_ANY] * len(e.out_shape),
        out_shape=list(out_shape) + e.out_shape, scratch_shapes=list(scratch_shapes) + e.scratch,
        compiler_params=_params(),
    )(*args, *e.inputs)
    outs = list(outs)
    return outs[:n_out], outs[n_out:]


def _pair_sum(grads, recvd, my_core, name):
    n_p = len(grads)

    def body(c_ref, *refs):
        del c_ref
        for p in range(n_p):
            refs[2 * n_p + p][...] = (refs[p][...].astype(F32) + refs[n_p + p][...].astype(F32)).astype(BF16)

    def blk(g):
        return (None, None) + g.shape[2:]

    return pl.pallas_call(
        body, name=name,
        grid_spec=pltpu.PrefetchScalarGridSpec(
            num_scalar_prefetch=1, grid=(N_CHIP,),
            in_specs=[pl.BlockSpec(blk(g), lambda j, c: (j, c[0], 0, 0)) for g in grads]
            + [pl.BlockSpec(blk(g), lambda j, c: (0, j, 0, 0)) for g in grads],
            out_specs=[pl.BlockSpec((None,) + g.shape[2:], lambda j, c: (j, 0, 0)) for g in grads]),
        out_shape=[_sds((N_CHIP,) + g.shape[2:], BF16) for g in grads],
        compiler_params=pltpu.CompilerParams(dimension_semantics=("arbitrary",), vmem_limit_bytes=VMEM_LIMIT),
    )(my_core, *grads, *recvd)


def _chip_sum(psums, recvd, my_chip, name):
    n_p = len(psums)

    def body(c_ref, *refs):
        del c_ref
        for p in range(n_p):
            acc = refs[p][...].astype(F32)
            for k in range(len(CHIP_RELS)):
                acc = acc + refs[n_p + p][k].astype(F32)
            refs[2 * n_p + p][...] = acc

    return pl.pallas_call(
        body, name=name,
        grid_spec=pltpu.PrefetchScalarGridSpec(
            num_scalar_prefetch=1, grid=(1,),
            in_specs=[pl.BlockSpec((None,) + g.shape[1:], lambda i, c: (c[0], 0, 0)) for g in psums]
            + [pl.BlockSpec(r.shape, lambda i, c: (0, 0, 0)) for r in recvd],
            out_specs=[pl.BlockSpec(g.shape[1:], lambda i, c: (0, 0)) for g in psums]),
        out_shape=[_sds(g.shape[1:], F32) for g in psums],
        compiler_params=pltpu.CompilerParams(dimension_semantics=("arbitrary",), vmem_limit_bytes=VMEM_LIMIT),
    )(my_chip, *psums, *recvd)


def _sum_devices(parts, shapes):
    plan = _small_rows(shapes)

    def body(p_ref, *refs):
        outs, tot_ref = refs[:-1], refs[-1]
        acc = p_ref[0]
        for j in range(1, N_DEV):
            acc = acc + p_ref[j]
        tot_ref[...] = acc
        for idx, r, c0, width, at in plan:
            outs[idx][r:r + 1, c0:c0 + width] = tot_ref[at:at + 1, 0:width]

    return pl.pallas_call(body, name="small_grad_sum", out_shape=[_sds(s, F32) for s in shapes],
                          scratch_shapes=[pltpu.VMEM(parts.shape[1:], F32)])(parts)


def _transposed(w):
    r, c = w.shape
    tr = MXU_COLS

    def body(w_ref, o_ref):
        o_ref[...] = w_ref[...].T

    return pl.pallas_call(
        body, name="transpose_w_down", grid=(r // tr,),
        in_specs=[pl.BlockSpec((tr, c), lambda i: (i, 0))], out_specs=pl.BlockSpec((c, tr), lambda i: (0, i)),
        out_shape=_sds((c, r), w.dtype), compiler_params=_params())(w)


def _cast_shards(shards):
    def body(*refs):
        for src, dst in zip(refs[:len(shards)], refs[len(shards):]):
            dst[...] = src[...].astype(BF16)

    return pl.pallas_call(body, name="cast_shards", out_shape=[_sds(a.shape, BF16) for a in shards],
                          compiler_params=pltpu.CompilerParams(vmem_limit_bytes=VMEM_LIMIT))(*shards)


def _adamw(ws, gs, ms, vs, name):
    n_t = len(ws)

    def body(*refs):
        w_refs, g_refs, m_refs, v_refs = (refs[j * n_t:(j + 1) * n_t] for j in range(4))
        outs = refs[4 * n_t:]
        for k in range(n_t):
            gg = g_refs[k][...]
            mn = ADAM_B1 * m_refs[k][...] + (1.0 - ADAM_B1) * gg
            vn = ADAM_B2 * v_refs[k][...] + (1.0 - ADAM_B2) * (gg * gg)
            m_hat = mn / (1.0 - ADAM_B1 ** ADAM_STEP)
            v_hat = vn / (1.0 - ADAM_B2 ** ADAM_STEP)
            outs[3 * k][...] = -ADAM_LR * (m_hat / (jnp.sqrt(v_hat) + ADAM_EPS) + ADAM_WD * w_refs[k][...])
            outs[3 * k + 1][...] = mn
            outs[3 * k + 2][...] = vn

    out_shape = [_sds(w.shape, F32) for w in ws for _ in range(3)]
    return pl.pallas_call(body, name=name, out_shape=out_shape,
                          compiler_params=pltpu.CompilerParams(vmem_limit_bytes=VMEM_LIMIT))(*ws, *gs, *ms, *vs)


class _Mesh:
    def __init__(self, shards, my_chip, my_core):
        self.shards, self.my_chip, self.my_core = shards, my_chip.reshape(1), my_core.reshape(1)

    def gather(self, names):
        return _all_gather([self.shards[n] for n in names])

    @staticmethod
    def whole(gathered):
        return gathered.reshape(N_DEV * gathered.shape[1], gathered.shape[2])

    @staticmethod
    def by_device(grads):
        return [g.reshape(N_CHIP, 2, g.shape[0] // N_DEV, g.shape[1]) for g in grads]

    @staticmethod
    def to_sibling(parts):
        return _scatter_exchange(parts, CORE_RELS, lambda ref, peer: ref.at[:, peer[2]],
                                 [(N_CHIP,) + p.shape[2:] for p in parts])

    @staticmethod
    def to_chips(pair):
        return _scatter_exchange(pair, CHIP_RELS, lambda ref, peer: ref.at[_chip_slot(peer)], [p.shape[1:] for p in pair])


def _step(x, target, g1, w_in_t, wa, wb, bb, lg, lb, w_out, g2, w_gate_t, w_up_t, wf, w_down, g3, ts, mesh=None):
    (z, h1), got = _fwd_in(x, g1, w_in_t, ts, exchange=mesh and mesh.gather(["w_out", "w_down"]))
    if mesh:
        w_out, w_down = [mesh.whole(g) for g in got]
    (x2, y, u), got = _fwd_mix(z, x, wa, wb, bb, lg, lb, w_out, ts, exchange=mesh and mesh.gather(["w_gate", "w_up"]))
    if mesh:
        w_gate_t, w_up_t = [mesh.whole(g) for g in got]
    (g0, v, h2, w_down_t), _ = _fwd_ffn_in(x2, g2, w_gate_t, w_up_t, ts, w_down=w_down)
    tk = min(ts, SKEW_TILE)
    a, dx3, dx3b, loss, dg3, dgc, dv, dwf = _ffn_out_and_back(g0, v, x2, wf, w_down, w_down_t, g3, target, tk)
    one = dict(w_down=_matmul_tn(a, dx3b, "wgrad_down"), w_up=_matmul_tn(dv, h2, "wgrad_up"))
    parts1 = mesh and mesh.by_device(list(one.values()))
    (dg0, dx2, dx2b, dg2), got = _bwd_ffn_b(dgc, dv, wf, w_gate_t, w_up_t, x2, g2, dx3, tk,
                                            exchange=mesh and mesh.to_sibling(parts1))
    pair1 = mesh and _pair_sum(parts1, got, mesh.my_core, "rs_pair_sum_1")
    if mesh:
        dw_gate, down_from_chips = _matmul_tn(dg0, h2, "wgrad_gate", exchange=mesh.to_chips(pair1[:1]))
    else:
        dw_gate = _matmul_tn(dg0, h2, "wgrad_gate")
    two = dict(w_gate=dw_gate, w_out=_matmul_tn(y, dx2b, "wgrad_out"))
    parts2 = mesh and mesh.by_device(list(two.values()))
    (dca, du, dab, dwa, dlg, dlb, dbb), got = _bwd_mix_a(
        dx2b, jnp.swapaxes(w_out, 0, 1), z, u, wa, lg, lb, tk,
        exchange=mesh and _join([mesh.to_chips(pair1[1:]), mesh.to_sibling(parts2)]))
    (dz, dx, dg1, dwb), _ = _bwd_mix_b(dca, du, z, dab, wa, wb, w_in_t, x, g1, dx2, tk)
    small = dict(norm_mix_g=dg1, conv_a_w=dwa, conv_b_w=dwb, conv_b_b=dbb, ln_b_g=dlg, ln_b_b=dlb,
                 norm_ffn_g=dg2, conv_ffn_w=dwf, norm_final_g=dg3)
    if not mesh:
        return loss, dx, dict(w_in=_matmul_tn(dz, h1, "wgrad_in"), **one, **two), small
    big = dict(zip(one, _chip_sum(pair1, down_from_chips + got[:1], mesh.my_chip, "rs_chip_sum_1")))
    pair2 = _pair_sum(parts2, got[1:], mesh.my_core, "rs_pair_sum_2")
    dw_in_t, got = _matmul_tn(dz, h1, "wgrad_in",
                              exchange=_join([mesh.to_chips(pair2), _gather_direct(_pack_small_grads(small, loss))]))
    big.update(zip(two, _chip_sum(pair2, got[:2], mesh.my_chip, "rs_chip_sum_2")))
    every = got[2]
    parts = mesh.by_device([dw_in_t])
    (got,) = _run_exchanges("rs_cores_last", [mesh.to_sibling(parts)])
    pair = _pair_sum(parts, got, mesh.my_core, "rs_pair_sum_last")
    (got,) = _run_exchanges("rs_chips_last", [mesh.to_chips(pair)])
    (big["w_in"],) = _chip_sum(pair, got, mesh.my_chip, "rs_chip_sum_last")
    shapes = [loss.shape if n == "loss" else small[n].shape for n in _SMALL_NAMES]
    return None, dx, big, dict(zip(_SMALL_NAMES, _sum_devices(every, shapes)))


def _pack_small_weights(conv_a_s, conv_b_s, conv_ffn_s):
    def body(a_ref, b_ref, f_ref, out):
        out[...] = jnp.zeros_like(out)
        at = 0
        for src in (a_ref, b_ref, f_ref):
            rows, cols = src.shape
            for r in range(rows):
                out[at:at + 1, 0:cols] = src[r:r + 1, :]
                at += 1

    return pl.pallas_call(body, name="small_weight_pack", out_shape=_sds((SMALL_W_ROWS, SMALL_W_COLS), F32))(
        conv_a_s, conv_b_s, conv_ffn_s)


def _unpack_small_weights(full):
    def take(r0, k, w):
        return jnp.transpose(full[:, r0:r0 + k, 0:w], (1, 0, 2)).reshape(k, N_DEV * w)

    return take(0, K_A, CONV_A_COLS), take(K_A, K_B, CONV_A_COLS), take(K_A + K_B, K_F, W_FF_COLS)


_SMALL_NAMES = ("conv_b_w", "conv_a_w", "conv_ffn_w", "norm_mix_g", "norm_ffn_g", "norm_final_g",
                "conv_b_b", "ln_b_g", "ln_b_b", "loss")
SMALL_G_ROWS = 64


def _small_rows(shapes):
    plan, at = [], 0
    for idx, (rows, cols) in enumerate(shapes):
        for r in range(rows):
            for c0 in range(0, cols, SMALL_G_COLS):
                plan.append((idx, r, c0, min(SMALL_G_COLS, cols - c0), at))
                at += 1
    assert at <= SMALL_G_ROWS
    return plan


def _pack_small_grads(small, loss):
    srcs = [loss if n == "loss" else small[n] for n in _SMALL_NAMES]
    plan = _small_rows([a.shape for a in srcs])

    def body(*refs):
        out = refs[-1]
        out[...] = jnp.zeros_like(out)
        for idx, r, c0, width, at in plan:
            out[at:at + 1, 0:width] = refs[idx][r:r + 1, c0:c0 + width]

    return pl.pallas_call(body, name="small_grad_pack", out_shape=_sds((SMALL_G_ROWS, SMALL_G_COLS), F32))(*srcs)


def kernel(x, norm_mix_g, w_in, conv_a_w, conv_b_w, conv_b_b, ln_b_g, ln_b_b, w_out, norm_ffn_g, w_gate, w_up, conv_ffn_w, w_down, norm_final_g, loss_target, m_norm_mix_g, m_w_in, m_conv_a_w, m_conv_b_w, m_conv_b_b, m_ln_b_g, m_ln_b_b, m_w_out, m_norm_ffn_g, m_w_gate, m_w_up, m_conv_ffn_w, m_w_down, m_norm_final_g, v_norm_mix_g, v_w_in, v_conv_a_w, v_conv_b_w, v_conv_b_b, v_ln_b_g, v_ln_b_b, v_w_out, v_norm_ffn_g, v_w_gate, v_w_up, v_conv_ffn_w, v_w_down, v_norm_final_g):
    ix, iy, ic = lax.axis_index("x"), lax.axis_index("y"), lax.axis_index("c")
    my_chip = (2 * ix + iy).astype(jnp.int32)
    my_core = ic.astype(jnp.int32)
    my_dev = 2 * my_chip + my_core

    weights = dict(norm_mix_g=norm_mix_g, w_in=w_in, conv_a_w=conv_a_w, conv_b_w=conv_b_w, conv_b_b=conv_b_b,
                   ln_b_g=ln_b_g, ln_b_b=ln_b_b, w_out=w_out, norm_ffn_g=norm_ffn_g, w_gate=w_gate, w_up=w_up,
                   conv_ffn_w=conv_ffn_w, w_down=w_down, norm_final_g=norm_final_g)
    m_in = dict(norm_mix_g=m_norm_mix_g, w_in=m_w_in, conv_a_w=m_conv_a_w, conv_b_w=m_conv_b_w, conv_b_b=m_conv_b_b,
                ln_b_g=m_ln_b_g, ln_b_b=m_ln_b_b, w_out=m_w_out, norm_ffn_g=m_norm_ffn_g, w_gate=m_w_gate,
                w_up=m_w_up, conv_ffn_w=m_conv_ffn_w, w_down=m_w_down, norm_final_g=m_norm_final_g)
    v_in = dict(norm_mix_g=v_norm_mix_g, w_in=v_w_in, conv_a_w=v_conv_a_w, conv_b_w=v_conv_b_w, conv_b_b=v_conv_b_b,
                ln_b_g=v_ln_b_g, ln_b_b=v_ln_b_b, w_out=v_w_out, norm_ffn_g=v_norm_ffn_g, w_gate=v_w_gate,
                w_up=v_w_up, conv_ffn_w=v_conv_ffn_w, w_down=v_w_down, norm_final_g=v_norm_final_g)
    order = list(weights)
    big_names = ("w_in", "w_gate", "w_up", "w_out", "w_down")
    transposed = ("w_in", "w_gate", "w_up")

    def shard2d(name, a):
        if name in transposed:
            return jnp.swapaxes(a[0], 0, 1)
        return a.reshape(1, a.shape[0]) if a.ndim == 1 else a.reshape(a.shape[-2:])

    def unshard2d(name, a2, like):
        if name in transposed:
            return jnp.swapaxes(a2, 0, 1)[None]
        return a2.reshape(like.shape)

    mesh = _Mesh(dict(zip(big_names, _cast_shards([shard2d(n, weights[n]) for n in big_names]))), my_chip, my_core)
    gathered, = _run_exchanges("ag_first", [_all_gather(
        [mesh.shards["w_in"], _pack_small_weights(conv_a_w[0], conv_b_w[0], conv_ffn_w[0])])])
    w_in_t = mesh.whole(gathered[0])
    wa_f, wb_f, wf_f = _unpack_small_weights(gathered[1])

    _, dx, gsum, stot = _step(
        x[0], loss_target[0], norm_mix_g, w_in_t, wa_f, wb_f, conv_b_b, ln_b_g, ln_b_b, None, norm_ffn_g,
        None, None, wf_f, None, norm_final_g.reshape(1, D_MODEL), SEQ_TILE, mesh)

    grads2d = dict(
        norm_mix_g=stot["norm_mix_g"],
        conv_a_w=lax.dynamic_slice(stot["conv_a_w"], (0, my_dev * CONV_A_COLS), (K_A, CONV_A_COLS)),
        conv_b_w=lax.dynamic_slice(stot["conv_b_w"], (0, my_dev * CONV_A_COLS), (K_B, CONV_A_COLS)),
        conv_b_b=stot["conv_b_b"], ln_b_g=stot["ln_b_g"], ln_b_b=stot["ln_b_b"],
        norm_ffn_g=stot["norm_ffn_g"],
        conv_ffn_w=lax.dynamic_slice(stot["conv_ffn_w"], (0, my_dev * W_FF_COLS), (K_F, W_FF_COLS)),
        norm_final_g=stot["norm_final_g"],
        **gsum,
    )

    updates = {}
    small_names = [n for n in order if n not in big_names]
    for group, label in [([n], "adamw_" + n) for n in big_names] + [(small_names, "adamw_small")]:
        outs = _adamw([shard2d(n, weights[n]) for n in group], [grads2d[n] for n in group],
                      [shard2d(n, m_in[n]) for n in group], [shard2d(n, v_in[n]) for n in group], label)
        for k, n in enumerate(group):
            updates[n] = outs[3 * k:3 * k + 3]
    g_out = [unshard2d(n, grads2d[n], weights[n]) for n in order]
    d_out, m_out, v_out = [[unshard2d(n, updates[n][j], weights[n]) for n in order] for j in range(3)]

    return (stot["loss"][0, 0], dx[None], *g_out, *d_out, *m_out, *v_out)
```

```python
import jax
import jax.numpy as jnp
from jax import lax
from jax.experimental import pallas as pl
from jax.experimental.pallas import tpu as pltpu

F32 = jnp.float32
BF16 = jnp.bfloat16

D_MODEL = 1024
D_A = 512
D_B = 512
D_IN = 3 * D_A + 2 * D_B
D_FF = 2816
K_A = 3
K_B = 31
K_F = 3
RMS_EPS = 1e-6
LN_EPS = 1e-5

ADAM_LR = 0.001
ADAM_B1 = 0.9
ADAM_B2 = 0.999
ADAM_EPS = 1e-08
ADAM_WD = 0.01
ADAM_STEP = 10

N_DEV = 8
N_CHIP = 4
LANES = 128
SUBLANES = 8
HALO = 16
ROW_CHUNK = 64
SEQ_TILE = 512
SKEW_TILE = 256
VMEM_LIMIT = 56 * 1024 * 1024

MESH = pl.DeviceIdType.MESH

W_FF_COLS = D_FF // N_DEV
CONV_A_COLS = D_A // N_DEV
SMALL_W_ROWS = 40
SMALL_W_COLS = 384
SMALL_G_COLS = 512


def _rows(ts, c):
    return pl.BlockSpec((ts, c), lambda i: (i, 0))


def _const(shape):
    return pl.BlockSpec(shape, lambda i: (0,) * len(shape), pipeline_mode=pl.Buffered(1))


def _acc_out(shape):
    return pl.BlockSpec(shape, lambda i: (0,) * len(shape))


def _rows_at(ts, c, tile):
    return pl.BlockSpec((ts, c), lambda i: (tile(i), 0))


def _prev_at(ts, c, tile):
    return pl.BlockSpec((HALO, c), lambda i: (jnp.maximum(tile(i) * (ts // HALO) - 1, 0), 0))


def _next_at(ts, c, s, tile):
    last = s // HALO - 1
    return pl.BlockSpec((HALO, c), lambda i: (jnp.minimum((tile(i) + 1) * (ts // HALO), last), 0))


def _prev(ts, c):
    return _prev_at(ts, c, lambda i: i)


def _next(ts, c, s):
    return _next_at(ts, c, s, lambda i: i)


MXU_COLS = 256
MXU_ROWS = 256


def _col_pieces(n):
    return [(c0, min(MXU_COLS, n - c0)) for c0 in range(0, n, MXU_COLS)]


def _matmul_pieces(terms, out_ref, k_parts, w_transposed=False):
    m, n = out_ref.shape
    rows = min(MXU_ROWS, m)
    steps = []
    for lhs_ref, w_ref in terms:
        tiles = lhs_ref.shape[1] // MXU_COLS
        cuts = [MXU_COLS * (tiles * j // k_parts) for j in range(k_parts)] + [lhs_ref.shape[1]]
        steps += [(lhs_ref, w_ref, cuts[j], cuts[j + 1]) for j in range(k_parts)]

    def piece(m0, n0, width, step):
        lhs_ref, w_ref, k0, k1 = steps[step]
        if w_transposed:
            part = lax.dot_general(lhs_ref[m0:m0 + rows, k0:k1], w_ref[n0:n0 + width, k0:k1], _NT,
                                   preferred_element_type=F32)
        else:
            part = jnp.dot(lhs_ref[m0:m0 + rows, k0:k1], w_ref[k0:k1, n0:n0 + width], preferred_element_type=F32)
        if step:
            part = part + out_ref[m0:m0 + rows, n0:n0 + width]
        out_ref[m0:m0 + rows, n0:n0 + width] = part

    return [(piece, (m0, n0, w, j)) for j in range(len(steps)) for n0, w in _col_pieces(n) for m0 in range(0, m, rows)]


def _interleaved(vector_units, matmul_pieces):
    n_u, n_p = len(vector_units), len(matmul_pieces)
    done = 0
    for k, (unit, args) in enumerate(vector_units):
        while done < n_p and done * n_u <= k * n_p:
            matmul_pieces[done][0](*matmul_pieces[done][1])
            done += 1
        unit(*args)
    for fn, args in matmul_pieces[done:]:
        fn(*args)


def _params():
    return pltpu.CompilerParams(dimension_semantics=("arbitrary",), vmem_limit_bytes=VMEM_LIMIT)


def _sds(shape, dtype):
    return jax.ShapeDtypeStruct(shape, dtype)


def _sigmoid(v):
    return 0.5 * jnp.tanh(0.5 * v) + 0.5


def _conv_block(ext_ref, w_ref, r0, rc, l0, k_taps, transposed):
    acc = None
    for k in range(k_taps):
        d = (k_taps // 2 - k) if transposed else (k - k_taps // 2)
        term = ext_ref[l0 // LANES, pl.ds(r0 + HALO + d, rc), :] * w_ref[k:k + 1, l0:l0 + LANES]
        acc = term if acc is None else acc + term
    return acc


def _conv_wgrad_block(acc_ref, dout, ext_ref, r0, rc, l0, k_taps, scale=None):
    for k in range(k_taps):
        prod = dout * ext_ref[l0 // LANES, pl.ds(r0 + HALO + k - k_taps // 2, rc), :]
        part = prod.reshape(rc // SUBLANES, SUBLANES, LANES).sum(axis=0)
        if scale is not None:
            part = part * scale
        acc_ref[k, :, l0:l0 + LANES] = acc_ref[k, :, l0:l0 + LANES] + part


def _reduce_acc(out_ref, acc_ref, k_taps):
    for k in range(k_taps):
        out_ref[k:k + 1, :] = jnp.sum(acc_ref[k], axis=0, keepdims=True)


def _fold8(v):
    rc, c = v.shape
    return v.reshape(rc // SUBLANES, SUBLANES, c).sum(axis=0)


def _ext_scratch(ts, c):
    return pltpu.VMEM((c // LANES, ts + 2 * HALO, LANES), F32)


def _put_rows(ext_ref, r0, rc, val):
    for q in range(val.shape[1] // LANES):
        ext_ref[q, pl.ds(r0 + HALO, rc), :] = val[:, q * LANES:(q + 1) * LANES]


def _fill_halo(ext_ref, vals_prev, vals_next, ts, first, last):
    for q in range(vals_prev.shape[1] // LANES):
        cols = slice(q * LANES, (q + 1) * LANES)
        ext_ref[q, 0:HALO, :] = jnp.where(first, 0.0, vals_prev[:, cols])
        ext_ref[q, HALO + ts:HALO + ts + HALO, :] = jnp.where(last, 0.0, vals_next[:, cols])


def _rms_bwd_rows(dh, xf, g):
    r = lax.rsqrt(jnp.mean(xf * xf, axis=-1, keepdims=True) + RMS_EPS)
    xhat = xf * r
    dxh = dh * g
    dx = r * (dxh - xhat * jnp.mean(dxh * xhat, axis=-1, keepdims=True))
    return dx, dh * xhat


_NT = (((1,), (1,)), ((), ()))
_TN = (((0,), (0,)), ((), ()))


def _fwd_in(x, g1, w_in_t, ts, exchange=None):
    s = x.shape[0]

    def body(x_ref, g_ref, w_ref, z_ref, h_ref):
        xf = x_ref[...]
        r = lax.rsqrt(jnp.mean(xf * xf, axis=-1, keepdims=True) + RMS_EPS)
        h = (xf * r * g_ref[...]).astype(BF16)
        h_ref[...] = h
        for n0 in range(0, D_IN, 512):
            z_ref[:, n0:n0 + 512] = lax.dot_general(h, w_ref[n0:n0 + 512, :], _NT,
                                                    preferred_element_type=F32).astype(BF16)

    return _call(
        body, name="fwd_in", grid=(s // ts,),
        in_specs=[_rows(ts, D_MODEL), _const((1, D_MODEL)), _const((D_IN, D_MODEL))],
        out_specs=[_rows(ts, D_IN), _rows(ts, D_MODEL)],
        out_shape=[_sds((s, D_IN), BF16), _sds((s, D_MODEL), BF16)],
        scratch_shapes=[], args=(x, g1, w_in_t), exchange=exchange, forward_step=(s // ts) * 3 // 4)


def _p_u0(z_ref, rows):
    a_h = z_ref[rows, 0:D_A].astype(F32)
    a_c = z_ref[rows, 2 * D_A:3 * D_A].astype(F32)
    b_v = z_ref[rows, 3 * D_A:3 * D_A + D_B].astype(F32)
    b_g = z_ref[rows, 3 * D_A + D_B:D_IN].astype(F32)
    return a_c * a_h, b_v * _sigmoid(b_g)


def _layernorm_rows(u_blocks):
    tot = None
    for ub in u_blocks:
        sm = jnp.sum(ub, axis=-1, keepdims=True)
        tot = sm if tot is None else tot + sm
    mu = tot * (1.0 / D_B)
    var = None
    for ub in u_blocks:
        sq = jnp.sum((ub - mu) * (ub - mu), axis=-1, keepdims=True)
        var = sq if var is None else var + sq
    rstd = lax.rsqrt(var * (1.0 / D_B) + LN_EPS)
    return mu, rstd


def _fwd_mix(z, x, wa, wb, bb, lg, lb, w_out, ts, exchange=None):
    s = x.shape[0]
    nt = s // ts
    rc = min(ROW_CHUNK, ts)

    def body(z_ref, zp_ref, zn_ref, x_ref, wa_ref, wb_ref, bb_ref, lg_ref, lb_ref, wo_ref,
             x2_ref, y_ref, u_ref, pe_ref, ue_ref):
        i = pl.program_id(0)
        pp, up = _p_u0(zp_ref, slice(None))
        pn, un = _p_u0(zn_ref, slice(None))
        _fill_halo(pe_ref, pp, pn, ts, i == 0, i == nt - 1)
        _fill_halo(ue_ref, up, un, ts, i == 0, i == nt - 1)

        def fill(j, carry):
            r0 = pl.multiple_of(j * rc, rc)
            p, u0 = _p_u0(z_ref, pl.ds(r0, rc))
            _put_rows(pe_ref, r0, rc, p)
            _put_rows(ue_ref, r0, rc, u0)
            return carry

        lax.fori_loop(0, ts // rc, fill, 0)

        def mixer_a(r0, l0):
            rows = pl.ds(r0, rc)
            ca = _conv_block(pe_ref, wa_ref, r0, rc, l0, K_A, False)
            a_b = z_ref[rows, D_A + l0:D_A + l0 + LANES].astype(F32)
            y_ref[rows, l0:l0 + LANES] = (a_b * ca).astype(BF16)

        def conv_b(r0, l0):
            u_ref[pl.ds(r0, rc), l0:l0 + LANES] = (_conv_block(ue_ref, wb_ref, r0, rc, l0, K_B, False)
                                                   + bb_ref[:, l0:l0 + LANES])

        def norm_b(r0):
            rows = pl.ds(r0, rc)
            ubs = [u_ref[rows, l0:l0 + LANES] for l0 in range(0, D_B, LANES)]
            mu, rstd = _layernorm_rows(ubs)
            for q, l0 in enumerate(range(0, D_B, LANES)):
                t = (ubs[q] - mu) * rstd * lg_ref[:, l0:l0 + LANES] + lb_ref[:, l0:l0 + LANES]
                y_ref[rows, D_A + l0:D_A + l0 + LANES] = (t * _sigmoid(t)).astype(BF16)

        def chunk_units(r0):
            return ([(mixer_a, (r0, l0)) for l0 in range(0, D_A, LANES)]
                    + [(conv_b, (r0, l0)) for l0 in range(0, D_B, LANES)] + [(norm_b, (r0,))])

        def main(j, carry):
            for fn, args in chunk_units(pl.multiple_of(j * rc, rc)):
                fn(*args)
            return carry

        def project(m0, n0, width):
            rows = slice(m0, m0 + hrows)
            x2_ref[rows, n0:n0 + width] = x_ref[rows, n0:n0 + width] + jnp.dot(
                y_ref[rows, :], wo_ref[:, n0:n0 + width], preferred_element_type=F32)

        hrows = ts // 2
        lax.fori_loop(0, hrows // rc, main, 0, unroll=4)
        _interleaved([u for q in range(hrows // rc) for u in chunk_units(hrows + q * rc)],
                     [(project, (0, n0, w)) for n0, w in _col_pieces(D_MODEL)])
        for n0, w in _col_pieces(D_MODEL):
            project(hrows, n0, w)

    return _call(
        body, name="fwd_mix", grid=(nt,),
        in_specs=[_rows(ts, D_IN), _prev(ts, D_IN), _next(ts, D_IN, s), _rows(ts, D_MODEL),
                  _const((K_A, D_A)), _const((K_B, D_B)), _const((1, D_B)), _const((1, D_B)), _const((1, D_B)),
                  _const((D_MODEL, D_MODEL))],
        out_specs=[_rows(ts, D_MODEL), _rows(ts, D_MODEL), _rows(ts, D_B)],
        out_shape=[_sds((s, D_MODEL), F32), _sds((s, D_MODEL), BF16), _sds((s, D_B), F32)],
        scratch_shapes=[_ext_scratch(ts, D_A), _ext_scratch(ts, D_B)],
        args=(z, z, z, x, wa, wb, bb, lg, lb, w_out), exchange=exchange, forward_step=nt * 5 // 8)


def _fwd_ffn_in(x2, g2, w_gate_t, w_up_t, ts, exchange=None, w_down=None):
    s = x2.shape[0]
    half = D_FF // 2

    def body(x_ref, g_ref, wg_ref, wu_ref, g0_ref, v_ref, h_ref):
        xf = x_ref[...]
        r = lax.rsqrt(jnp.mean(xf * xf, axis=-1, keepdims=True) + RMS_EPS)
        h = (xf * r * g_ref[...]).astype(BF16)
        h_ref[...] = h
        for n0 in range(0, D_FF, half):
            g0_ref[:, n0:n0 + half] = lax.dot_general(h, wg_ref[n0:n0 + half, :], _NT,
                                                      preferred_element_type=F32).astype(BF16)
            v_ref[:, n0:n0 + half] = lax.dot_general(h, wu_ref[n0:n0 + half, :], _NT,
                                                     preferred_element_type=F32).astype(BF16)

    def body_and_transpose(x_ref, g_ref, wg_ref, wu_ref, wd_ref, g0_ref, v_ref, h_ref, wdt_ref):
        @pl.when(pl.program_id(0) < n_blk)
        def _():
            wdt_ref[...] = wd_ref[...].T

        body(x_ref, g_ref, wg_ref, wu_ref, g0_ref, v_ref, h_ref)

    in_specs = [_rows(ts, D_MODEL), _const((1, D_MODEL)), _const((D_FF, D_MODEL)), _const((D_FF, D_MODEL))]
    out_specs = [_rows(ts, D_FF), _rows(ts, D_FF), _rows(ts, D_MODEL)]
    out_shape = [_sds((s, D_FF), BF16), _sds((s, D_FF), BF16), _sds((s, D_MODEL), BF16)]
    n_blk = D_FF // MXU_COLS
    if w_down is None or s // ts < n_blk:
        outs, got = _call(body, name="fwd_ffn_in", grid=(s // ts,), in_specs=in_specs, out_specs=out_specs,
                          out_shape=out_shape, scratch_shapes=[], args=(x2, g2, w_gate_t, w_up_t), exchange=exchange,
                          forward_step=(s // ts) // 2)
        return outs + [None if w_down is None else _transposed(w_down)], got
    blk = lambda i: jnp.minimum(i, n_blk - 1)
    return _call(
        body_and_transpose, name="fwd_ffn_in", grid=(s // ts,),
        in_specs=in_specs + [pl.BlockSpec((MXU_COLS, D_MODEL), lambda i: (blk(i), 0))],
        out_specs=out_specs + [pl.BlockSpec((D_MODEL, MXU_COLS), lambda i: (0, blk(i)))],
        out_shape=out_shape + [_sds((D_MODEL, D_FF), BF16)],
        scratch_shapes=[], args=(x2, g2, w_gate_t, w_up_t, w_down), exchange=exchange, forward_step=(s // ts) // 2)


def _ffn_out_and_back(g0, v, x2, wf, w_down, w_down_t, g3, target, ts):
    s = x2.shape[0]
    nt = s // ts
    rc = min(ROW_CHUNK, ts // 2)

    def body(g0_ref, gp_ref, gn_ref, v_ref, x2_ref, wf_ref, wd_ref, wdt_ref, g3_ref, t_ref,
             a_ref, dx3_ref, dx3b_ref, loss_ref, dg3_ref, dg_ref, dv_ref, dwf_ref,
             ge_ref, silu_ref, dsv_ref, da_ref, acc_ref, p_ref, sums_ref):
        i = pl.program_id(0)

        @pl.when(i == 0)
        def _():
            acc_ref[...] = jnp.zeros_like(acc_ref)
            sums_ref[...] = jnp.zeros_like(sums_ref)

        _fill_halo(ge_ref, gp_ref[...].astype(F32), gn_ref[...].astype(F32), ts, i == 0, i == nt - 1)

        def fill(j, carry):
            r0 = pl.multiple_of(j * rc, rc)
            _put_rows(ge_ref, r0, rc, g0_ref[pl.ds(r0, rc), :].astype(F32))
            return carry

        lax.fori_loop(0, ts // rc, fill, 0)

        def act(r0, l0):
            rows = slice(r0, r0 + rc)
            g = _conv_block(ge_ref, wf_ref, r0, rc, l0, K_F, False).astype(BF16)
            vv = v_ref[rows, l0:l0 + LANES]
            sg = _sigmoid(g)
            silu = g * sg
            a_ref[rows, l0:l0 + LANES] = silu * vv
            silu_ref[rows, l0:l0 + LANES] = silu
            dsv_ref[rows, l0:l0 + LANES] = (sg + silu * (1.0 - sg)) * vv

        def tail(r0):
            rows = slice(r0, r0 + rc)
            x3 = x2_ref[rows, :] + p_ref[rows, :]
            r = lax.rsqrt(jnp.mean(x3 * x3, axis=-1, keepdims=True) + RMS_EPS)
            xhat = x3 * r
            diff = xhat * g3_ref[...] - t_ref[rows, :]
            dout = diff * (1.0 / D_MODEL)
            dxh = dout * g3_ref[...]
            dx3 = r * (dxh - xhat * jnp.mean(dxh * xhat, axis=-1, keepdims=True))
            dx3_ref[rows, :] = dx3
            dx3b_ref[rows, :] = dx3.astype(BF16)
            sums_ref[0] = sums_ref[0] + _fold8(diff * diff)
            sums_ref[1] = sums_ref[1] + _fold8(dout * xhat)

        def back(r0, l0):
            rows = slice(r0, r0 + rc)
            da = da_ref[rows, l0:l0 + LANES]
            dv_ref[rows, l0:l0 + LANES] = (da * silu_ref[rows, l0:l0 + LANES]).astype(BF16)
            dgg = da * dsv_ref[rows, l0:l0 + LANES]
            dg_ref[rows, l0:l0 + LANES] = dgg.astype(BF16)
            _conv_wgrad_block(acc_ref, dgg, ge_ref, r0, rc, l0, K_F)

        hrows = ts // 2

        def units(fn, h, per_lane_block):
            starts = [h * hrows + q * rc for q in range(hrows // rc)]
            if per_lane_block:
                return [(fn, (r0, l0)) for r0 in starts for l0 in range(0, D_FF, LANES)]
            return [(fn, (r0,)) for r0 in starts]

        def rows_of(ref, h):
            return ref.at[pl.ds(h * hrows, hrows), :]

        def product(h):
            return _matmul_pieces([(rows_of(a_ref, h), wd_ref)], rows_of(p_ref, h), 2)

        def grad_a(h):
            return _matmul_pieces([(rows_of(dx3b_ref, h), wdt_ref)], rows_of(da_ref, h), 1)

        _interleaved(units(act, 0, True), [])
        _interleaved(units(act, 1, True), product(0))
        _interleaved(units(tail, 0, False), product(1))
        _interleaved(units(tail, 1, False), grad_a(0))
        _interleaved(units(back, 0, True), grad_a(1))
        _interleaved(units(back, 1, True), [])

        @pl.when(i == nt - 1)
        def _():
            _reduce_acc(dwf_ref, acc_ref, K_F)
            loss_ref[...] = (0.5 / D_MODEL) * jnp.sum(sums_ref[0], keepdims=True)
            dg3_ref[...] = jnp.sum(sums_ref[1], axis=0, keepdims=True)

    return pl.pallas_call(
        body, name="ffn_out_and_back", grid=(nt,),
        in_specs=[_rows(ts, D_FF), _prev(ts, D_FF), _next(ts, D_FF, s), _rows(ts, D_FF), _rows(ts, D_MODEL),
                  _const((K_F, D_FF)), _const((D_FF, D_MODEL)), _const((D_MODEL, D_FF)), _const((1, D_MODEL)),
                  _rows(ts, D_MODEL)],
        out_specs=[_rows(ts, D_FF), _rows(ts, D_MODEL), _rows(ts, D_MODEL), _acc_out((1, 1)), _acc_out((1, D_MODEL)),
                   _rows(ts, D_FF), _rows(ts, D_FF), _acc_out((K_F, D_FF))],
        out_shape=[_sds((s, D_FF), BF16), _sds((s, D_MODEL), F32), _sds((s, D_MODEL), BF16),
                   _sds((1, 1), F32), _sds((1, D_MODEL), F32),
                   _sds((s, D_FF), BF16), _sds((s, D_FF), BF16), _sds((K_F, D_FF), F32)],
        scratch_shapes=[_ext_scratch(ts, D_FF), pltpu.VMEM((ts, D_FF), BF16), pltpu.VMEM((ts, D_FF), BF16),
                        pltpu.VMEM((ts, D_FF), F32), pltpu.VMEM((K_F, SUBLANES, D_FF), F32),
                        pltpu.VMEM((ts, D_MODEL), F32), pltpu.VMEM((2, SUBLANES, D_MODEL), F32)],
        compiler_params=_params(),
    )(g0, g0, g0, v, x2, wf, w_down, w_down_t, g3, target)


def _bwd_ffn_b(dg, dv, wf, w_gate, w_up, x2, g2, dx3, ts, exchange=None):
    s = x2.shape[0]
    nt = s // ts
    rc = min(ROW_CHUNK, ts)
    n_sub = ts // rc

    def body(dg_ref, dgp_ref, dgn_ref, dv_ref, wf_ref, wg_ref, wu_ref, x2_ref, g2_ref, dx3_ref,
             dg0_ref, dx2_ref, dx2b_ref, dgn2_ref, dge_ref, dg8_ref, a0_ref, a1_ref, p0_ref, p1_ref):
        i = pl.program_id(0)
        vt = jnp.minimum(i, nt - 1)
        live = (i >= 2).astype(F32)

        @pl.when(i == 0)
        def _():
            dg8_ref[...] = jnp.zeros_like(dg8_ref)
            a1_ref[...] = jnp.zeros_like(a1_ref)
            p1_ref[...] = jnp.zeros_like(p1_ref)

        _fill_halo(dge_ref, dgp_ref[...].astype(F32), dgn_ref[...].astype(F32), ts, vt == 0, vt == nt - 1)

        def fill(j, carry):
            r0 = pl.multiple_of(j * rc, rc)
            _put_rows(dge_ref, r0, rc, dg_ref[pl.ds(r0, rc), :].astype(F32))
            return carry

        lax.fori_loop(0, n_sub, fill, 0)

        def stage(a_new, a_old, p_new, p_old):
            def conv_t(r0, l0):
                rows = slice(r0, r0 + rc)
                dg0 = _conv_block(dge_ref, wf_ref, r0, rc, l0, K_F, True).astype(BF16)
                dg0_ref[rows, l0:l0 + LANES] = dg0
                a_new[rows, l0:l0 + LANES] = dg0

            def tail(r0):
                rows = slice(r0, r0 + rc)
                dx, dgrow = _rms_bwd_rows(p_old[rows, :], x2_ref[rows, :], g2_ref[...])
                dx2 = dx3_ref[rows, :] + dx
                dx2_ref[rows, :] = dx2
                dx2b_ref[rows, :] = dx2.astype(BF16)
                dg8_ref[...] = dg8_ref[...] + _fold8(dgrow) * live

            units = []
            for q in range(n_sub):
                units += [(conv_t, (q * rc, l0)) for l0 in range(0, D_FF, LANES)]
                units.append((tail, (q * rc,)))
            _interleaved(units, _matmul_pieces([(a_old, wg_ref), (dv_ref, wu_ref)], p_new, 2))

        @pl.when(i % 2 == 0)
        def _():
            stage(a0_ref, a1_ref, p0_ref, p1_ref)

        @pl.when(i % 2 == 1)
        def _():
            stage(a1_ref, a0_ref, p1_ref, p0_ref)

        @pl.when(i == nt + 1)
        def _():
            dgn2_ref[...] = jnp.sum(dg8_ref[...], axis=0, keepdims=True)

    vtile = lambda i: jnp.minimum(i, nt - 1)
    mtile = lambda i: jnp.clip(i - 1, 0, nt - 1)
    ttile = lambda i: jnp.clip(i - 2, 0, nt - 1)
    return _call(
        body, name="bwd_ffn_b", grid=(nt + 2,),
        in_specs=[_rows_at(ts, D_FF, vtile), _prev_at(ts, D_FF, vtile), _next_at(ts, D_FF, s, vtile),
                  _rows_at(ts, D_FF, mtile), _const((K_F, D_FF)),
                  _const((D_FF, D_MODEL)), _const((D_FF, D_MODEL)), _rows_at(ts, D_MODEL, ttile), _const((1, D_MODEL)),
                  _rows_at(ts, D_MODEL, ttile)],
        out_specs=[_rows_at(ts, D_FF, vtile), _rows_at(ts, D_MODEL, ttile), _rows_at(ts, D_MODEL, ttile),
                   _acc_out((1, D_MODEL))],
        out_shape=[_sds((s, D_FF), BF16), _sds((s, D_MODEL), F32), _sds((s, D_MODEL), BF16), _sds((1, D_MODEL), F32)],
        scratch_shapes=[_ext_scratch(ts, D_FF), pltpu.VMEM((SUBLANES, D_MODEL), F32),
                        pltpu.VMEM((ts, D_FF), BF16), pltpu.VMEM((ts, D_FF), BF16),
                        pltpu.VMEM((ts, D_MODEL), F32), pltpu.VMEM((ts, D_MODEL), F32)],
        args=(dg, dg, dg, dv, wf, w_gate, w_up, x2, g2, dx3), exchange=exchange)


def _bwd_mix_a(dx2b, w_out_t, z, u, wa, lg, lb, ts, exchange=None):
    s = dx2b.shape[0]
    nt = s // ts
    rc = min(ROW_CHUNK, ts)
    n_sub = ts // rc

    def body(dx_ref, wo_ref, z_ref, zp_ref, zn_ref, u_ref, wa_ref, lg_ref, lb_ref,
             dca_ref, du_ref, dab_ref, dwa_ref, dlg_ref, dlb_ref, dbb_ref, pe_ref, dy0_ref, dy1_ref, acc_ref, sacc_ref):
        i = pl.program_id(0)
        t = jnp.maximum(i - 1, 0)

        @pl.when(i == 0)
        def _():
            acc_ref[...] = jnp.zeros_like(acc_ref)
            sacc_ref[...] = jnp.zeros_like(sacc_ref)
            dy1_ref[...] = jnp.zeros_like(dy1_ref)

        pp, _ = _p_u0(zp_ref, slice(None))
        pn, _ = _p_u0(zn_ref, slice(None))
        _fill_halo(pe_ref, pp, pn, ts, t == 0, t == nt - 1)

        def fill(j, carry):
            r0 = pl.multiple_of(j * rc, rc)
            rows = pl.ds(r0, rc)
            _put_rows(pe_ref, r0, rc, z_ref[rows, 2 * D_A:3 * D_A].astype(F32) * z_ref[rows, 0:D_A].astype(F32))
            return carry

        lax.fori_loop(0, n_sub, fill, 0)

        def stage(dy_new, dy_old):
            def piece(m0, n0, width):
                dy_new[m0:m0 + MXU_ROWS, n0:n0 + width] = jnp.dot(
                    dx_ref[m0:m0 + MXU_ROWS, :], wo_ref[:, n0:n0 + width], preferred_element_type=F32).astype(BF16)

            units = []
            for q in range(n_sub):
                units += [(mixer_a, (dy_old, q * rc, l0)) for l0 in range(0, D_A, LANES)]
                units.append((mixer_b, (dy_old, q * rc)))
            _interleaved(units, [(piece, (m0, n0, w)) for n0, w in _col_pieces(D_MODEL) for m0 in range(0, ts, MXU_ROWS)])

        def mixer_a(dy_ref, r0, l0):
            rows = slice(r0, r0 + rc)
            ca = _conv_block(pe_ref, wa_ref, r0, rc, l0, K_A, False)
            a_b = z_ref[rows, D_A + l0:D_A + l0 + LANES].astype(F32)
            dya = dy_ref[rows, l0:l0 + LANES].astype(F32)
            dab_ref[rows, l0:l0 + LANES] = (dya * ca).astype(BF16)
            dca = dya * a_b
            dca_ref[rows, l0:l0 + LANES] = dca
            _conv_wgrad_block(acc_ref, dca, pe_ref, r0, rc, l0, K_A)

        def mixer_b(dy_ref, r0):
            rows = slice(r0, r0 + rc)
            ubs = [u_ref[rows, l0:l0 + LANES] for l0 in range(0, D_B, LANES)]
            mu, rstd = _layernorm_rows(ubs)
            ns, dns = [], []
            m1 = None
            m2 = None
            for q, l0 in enumerate(range(0, D_B, LANES)):
                n = (ubs[q] - mu) * rstd
                lgq = lg_ref[:, l0:l0 + LANES]
                t = n * lgq + lb_ref[:, l0:l0 + LANES]
                sg = _sigmoid(t)
                dt = dy_ref[rows, D_A + l0:D_A + l0 + LANES].astype(F32) * (sg * (1.0 + t * (1.0 - sg)))
                dn = dt * lgq
                ns.append(n)
                dns.append(dn)
                s1 = jnp.sum(dn, axis=-1, keepdims=True)
                s2 = jnp.sum(dn * n, axis=-1, keepdims=True)
                m1 = s1 if m1 is None else m1 + s1
                m2 = s2 if m2 is None else m2 + s2
                sacc_ref[0, :, l0:l0 + LANES] = sacc_ref[0, :, l0:l0 + LANES] + _fold8(dt * n)
                sacc_ref[1, :, l0:l0 + LANES] = sacc_ref[1, :, l0:l0 + LANES] + _fold8(dt)
            m1 = m1 * (1.0 / D_B)
            m2 = m2 * (1.0 / D_B)
            for q, l0 in enumerate(range(0, D_B, LANES)):
                du = rstd * (dns[q] - m1 - ns[q] * m2)
                du_ref[rows, l0:l0 + LANES] = du
                sacc_ref[2, :, l0:l0 + LANES] = sacc_ref[2, :, l0:l0 + LANES] + _fold8(du)

        @pl.when(i % 2 == 0)
        def _():
            stage(dy0_ref, dy1_ref)

        @pl.when(i % 2 == 1)
        def _():
            stage(dy1_ref, dy0_ref)

        @pl.when(i == nt)
        def _():
            _reduce_acc(dwa_ref, acc_ref, K_A)
            dlg_ref[...] = jnp.sum(sacc_ref[0], axis=0, keepdims=True)
            dlb_ref[...] = jnp.sum(sacc_ref[1], axis=0, keepdims=True)
            dbb_ref[...] = jnp.sum(sacc_ref[2], axis=0, keepdims=True)

    cur = lambda i: jnp.minimum(i, nt - 1)
    old = lambda i: jnp.maximum(i - 1, 0)
    return _call(
        body, name="bwd_mix_a", grid=(nt + 1,),
        in_specs=[_rows_at(ts, D_MODEL, cur), _const((D_MODEL, D_MODEL)), _rows_at(ts, D_IN, old), _prev_at(ts, D_IN, old),
                  _next_at(ts, D_IN, s, old), _rows_at(ts, D_B, old), _const((K_A, D_A)), _const((1, D_B)), _const((1, D_B))],
        out_specs=[_rows_at(ts, D_A, old), _rows_at(ts, D_B, old), _rows_at(ts, D_A, old), _acc_out((K_A, D_A)),
                   _acc_out((1, D_B)), _acc_out((1, D_B)), _acc_out((1, D_B))],
        out_shape=[_sds((s, D_A), F32), _sds((s, D_B), F32), _sds((s, D_A), BF16), _sds((K_A, D_A), F32),
                   _sds((1, D_B), F32), _sds((1, D_B), F32), _sds((1, D_B), F32)],
        scratch_shapes=[_ext_scratch(ts, D_A), pltpu.VMEM((ts, D_MODEL), BF16), pltpu.VMEM((ts, D_MODEL), BF16),
                        pltpu.VMEM((K_A, SUBLANES, D_A), F32), pltpu.VMEM((3, SUBLANES, D_B), F32)],
        args=(dx2b, w_out_t, z, z, z, u, wa, lg, lb), exchange=exchange)


def _bwd_mix_b(dca, du, z, dab, wa, wb, w_in, x, g1, dx2, ts, exchange=None):
    s = x.shape[0]
    nt = s // ts
    rc = min(ROW_CHUNK, ts)
    n_sub = ts // rc

    def body(dca_ref, dcap_ref, dcan_ref, du_ref, dup_ref, dun_ref, z_ref, zp_ref, zn_ref, dab_ref,
             wa_ref, wb_ref, wi_ref, x_ref, g1_ref, dx2_ref,
             dz_ref, dx_ref, dg1_ref, dwb_ref, dcae_ref, due_ref, ue_ref, acc_ref, dg8_ref,
             dz0_ref, dz1_ref, dh0_ref, dh1_ref):
        i = pl.program_id(0)

        @pl.when(i == 0)
        def _():
            acc_ref[...] = jnp.zeros_like(acc_ref)
            dg8_ref[...] = jnp.zeros_like(dg8_ref)
            dz1_ref[...] = jnp.zeros_like(dz1_ref)
            dh1_ref[...] = jnp.zeros_like(dh1_ref)

        vt = jnp.minimum(i, nt - 1)
        first = vt == 0
        last = vt == nt - 1
        live = (i < nt).astype(F32)
        _fill_halo(dcae_ref, dcap_ref[...], dcan_ref[...], ts, first, last)
        _fill_halo(due_ref, dup_ref[...], dun_ref[...], ts, first, last)
        _, up = _p_u0(zp_ref, slice(None))
        _, un = _p_u0(zn_ref, slice(None))
        _fill_halo(ue_ref, up, un, ts, first, last)

        def fill(j, carry):
            r0 = pl.multiple_of(j * rc, rc)
            rows = pl.ds(r0, rc)
            _put_rows(dcae_ref, r0, rc, dca_ref[rows, :])
            _put_rows(due_ref, r0, rc, du_ref[rows, :])
            b_v = z_ref[rows, 3 * D_A:3 * D_A + D_B].astype(F32)
            b_g = z_ref[rows, 3 * D_A + D_B:D_IN].astype(F32)
            _put_rows(ue_ref, r0, rc, b_v * _sigmoid(b_g))
            return carry

        lax.fori_loop(0, n_sub, fill, 0)

        def stage(dz_new, dz_old, dh_new, dh_old):
            def put(rows, c0, val):
                dz_ref[rows, c0:c0 + LANES] = val
                dz_new[rows, c0:c0 + LANES] = val

            def mixer_a(r0, l0):
                rows = slice(r0, r0 + rc)
                dp = _conv_block(dcae_ref, wa_ref, r0, rc, l0, K_A, True)
                a_h = z_ref[rows, l0:l0 + LANES].astype(F32)
                a_c = z_ref[rows, 2 * D_A + l0:2 * D_A + l0 + LANES].astype(F32)
                put(rows, l0, (dp * a_c).astype(BF16))
                put(rows, D_A + l0, dab_ref[rows, l0:l0 + LANES])
                put(rows, 2 * D_A + l0, (dp * a_h).astype(BF16))

            def mixer_b(r0, l0):
                rows = slice(r0, r0 + rc)
                du0 = _conv_block(due_ref, wb_ref, r0, rc, l0, K_B, True)
                b_v = z_ref[rows, 3 * D_A + l0:3 * D_A + l0 + LANES].astype(F32)
                b_g = z_ref[rows, 3 * D_A + D_B + l0:3 * D_A + D_B + l0 + LANES].astype(F32)
                sg = _sigmoid(b_g)
                put(rows, 3 * D_A + l0, (du0 * sg).astype(BF16))
                put(rows, 3 * D_A + D_B + l0, (du0 * b_v * (sg * (1.0 - sg))).astype(BF16))
                _conv_wgrad_block(acc_ref, du_ref[rows, l0:l0 + LANES], ue_ref, r0, rc, l0, K_B, live)

            def tail(r0):
                rows = slice(r0, r0 + rc)
                dx, dgrow = _rms_bwd_rows(dh_old[rows, :], x_ref[rows, :], g1_ref[...])
                dx_ref[rows, :] = dx2_ref[rows, :] + dx
                dg8_ref[...] = dg8_ref[...] + _fold8(dgrow)

            units = []
            for q in range(n_sub):
                units += [(mixer_a, (q * rc, l0)) for l0 in range(0, D_A, LANES)]
                units += [(mixer_b, (q * rc, l0)) for l0 in range(0, D_B, LANES)]
                units.append((tail, (q * rc,)))
            _interleaved(units, _matmul_pieces([(dz_old, wi_ref)], dh_new, 2))

        @pl.when(i % 2 == 0)
        def _():
            stage(dz0_ref, dz1_ref, dh0_ref, dh1_ref)

        @pl.when(i % 2 == 1)
        def _():
            stage(dz1_ref, dz0_ref, dh1_ref, dh0_ref)

        @pl.when(i == nt + 1)
        def _():
            _reduce_acc(dwb_ref, acc_ref, K_B)
            dg1_ref[...] = jnp.sum(dg8_ref[...], axis=0, keepdims=True)

    vtile = lambda i: jnp.minimum(i, nt - 1)
    ttile = lambda i: jnp.clip(i - 2, 0, nt - 1)
    return _call(
        body, name="bwd_mix_b", grid=(nt + 2,),
        in_specs=[_rows_at(ts, D_A, vtile), _prev_at(ts, D_A, vtile), _next_at(ts, D_A, s, vtile),
                  _rows_at(ts, D_B, vtile), _prev_at(ts, D_B, vtile), _next_at(ts, D_B, s, vtile),
                  _rows_at(ts, D_IN, vtile), _prev_at(ts, D_IN, vtile), _next_at(ts, D_IN, s, vtile), _rows_at(ts, D_A, vtile),
                  _const((K_A, D_A)), _const((K_B, D_B)), _const((D_IN, D_MODEL)), _rows_at(ts, D_MODEL, ttile),
                  _const((1, D_MODEL)), _rows_at(ts, D_MODEL, ttile)],
        out_specs=[_rows_at(ts, D_IN, vtile), _rows_at(ts, D_MODEL, ttile), _acc_out((1, D_MODEL)), _acc_out((K_B, D_B))],
        out_shape=[_sds((s, D_IN), BF16), _sds((s, D_MODEL), F32), _sds((1, D_MODEL), F32), _sds((K_B, D_B), F32)],
        scratch_shapes=[_ext_scratch(ts, D_A), _ext_scratch(ts, D_B), _ext_scratch(ts, D_B),
                        pltpu.VMEM((K_B, SUBLANES, D_B), F32), pltpu.VMEM((SUBLANES, D_MODEL), F32),
                        pltpu.VMEM((ts, D_IN), BF16), pltpu.VMEM((ts, D_IN), BF16),
                        pltpu.VMEM((ts, D_MODEL), F32), pltpu.VMEM((ts, D_MODEL), F32)],
        args=(dca, dca, dca, du, du, du, z, z, z, dab, wa, wb, w_in, x, g1, dx2), exchange=exchange)


def _matmul_tn(a, b, name, exchange=None):
    s, m = a.shape
    n = b.shape[1]
    tk = min(1024, s)
    nk = s // tk
    tm = 256

    def body(a_ref, b_ref, o_ref, acc_ref):
        k = pl.program_id(0)

        @pl.when(k == 0)
        def _():
            acc_ref[...] = jnp.zeros_like(acc_ref)

        for m0 in range(0, m, tm):
            acc_ref[m0:m0 + tm, :] = acc_ref[m0:m0 + tm, :] + lax.dot_general(
                a_ref[:, m0:m0 + tm], b_ref[...], _TN, preferred_element_type=F32)

        @pl.when(k == nk - 1)
        def _():
            o_ref[...] = acc_ref[...].astype(BF16)

    (out,), got = _call(
        body, name=name, grid=(nk,),
        in_specs=[_rows(tk, m), _rows(tk, n)],
        out_specs=[_acc_out((m, n))],
        out_shape=[_sds((m, n), BF16)],
        scratch_shapes=[pltpu.VMEM((m, n), F32)], args=(a, b), exchange=exchange)
    return out if exchange is None else (out, got)


CHIP_RELS = ((1, 0, 0), (0, 1, 0), (1, 1, 0))
CORE_RELS = ((0, 0, 1),)
ALL_RELS = ((0, 0, 1), (0, 1, 0), (0, 1, 1), (1, 0, 0), (1, 0, 1), (1, 1, 0), (1, 1, 1))


def _chip_slot(dev):
    return 2 * dev[0] + dev[1]


def _dev_slot(dev):
    return 4 * dev[0] + 2 * dev[1] + dev[2]


def _me():
    return (lax.axis_index("x"), lax.axis_index("y"), lax.axis_index("c"))


def _peer(me, rel):
    return tuple((1 - me[a]) if rel[a] else me[a] for a in range(3))


_ANY = pl.BlockSpec(memory_space=pl.ANY)


class _Exchange:
    def __init__(self, inputs, out_shape, scratch, start, finish, forward=None):
        self.inputs, self.out_shape, self.scratch = list(inputs), list(out_shape), list(scratch)
        self.start, self.finish, self.forward = start, finish, forward


def _all_gather(payloads):
    n_p = len(payloads)
    n_k = 1 + 2 * len(CHIP_RELS)

    def copy(srcs, dsts, sems, p, k, block_dev, to, from_src):
        blk = dsts[p].at[_dev_slot(block_dev)]
        return pltpu.make_async_remote_copy(
            src_ref=srcs[p] if from_src else blk, dst_ref=blk,
            send_sem=sems[0].at[n_k * p + k], recv_sem=sems[1].at[n_k * p + k], device_id=to, device_id_type=MESH)

    def own_copy(srcs, dsts, sems, p):
        return pltpu.make_async_copy(srcs[p], dsts[p].at[_dev_slot(_me())], sems[2].at[p])

    def start(srcs, dsts, sems):
        me = _me()
        for p in range(n_p):
            own_copy(srcs, dsts, sems, p).start()
        for j, rel in enumerate(CHIP_RELS):
            for p in range(n_p):
                copy(srcs, dsts, sems, p, 1 + j, me, _peer(me, rel), True).start()
        for p in range(n_p):
            copy(srcs, dsts, sems, p, 0, me, _peer(me, CORE_RELS[0]), True).start()

    def forward(srcs, dsts, sems):
        me = _me()
        sibling = _peer(me, CORE_RELS[0])
        for j, rel in enumerate(CHIP_RELS):
            other = _peer(me, rel)
            for p in range(n_p):
                copy(srcs, dsts, sems, p, 1 + j, other, me, False).wait_recv()
                copy(srcs, dsts, sems, p, 4 + j, other, sibling, False).start()

    def finish(srcs, dsts, sems):
        me = _me()
        sibling = _peer(me, CORE_RELS[0])
        for p in range(n_p):
            copy(srcs, dsts, sems, p, 0, sibling, me, False).wait_recv()
        for j, rel in enumerate(CHIP_RELS):
            for p in range(n_p):
                copy(srcs, dsts, sems, p, 4 + j, _peer(sibling, rel), me, False).wait_recv()
        for p in range(n_p):
            own_copy(srcs, dsts, sems, p).wait()
            copy(srcs, dsts, sems, p, 0, me, sibling, True).wait_send()
            for j, rel in enumerate(CHIP_RELS):
                copy(srcs, dsts, sems, p, 1 + j, me, _peer(me, rel), True).wait_send()
                copy(srcs, dsts, sems, p, 4 + j, _peer(me, rel), sibling, False).wait_send()

    return _Exchange(
        payloads, [_sds((N_DEV,) + p.shape, p.dtype) for p in payloads],
        [pltpu.SemaphoreType.DMA((n_p * n_k,)), pltpu.SemaphoreType.DMA((n_p * n_k,)), pltpu.SemaphoreType.DMA((n_p,))],
        start, finish, forward)


def _gather_direct(payload):
    n_r = len(ALL_RELS)

    def copies(srcs, dsts, sems):
        me = _me()
        mine = dsts[0].at[_dev_slot(me)]
        own = pltpu.make_async_copy(srcs[0], mine, sems[2].at[0])
        remote = [pltpu.make_async_remote_copy(src_ref=srcs[0], dst_ref=mine, send_sem=sems[0].at[k], recv_sem=sems[1].at[k],
                                               device_id=_peer(me, rel), device_id_type=MESH)
                  for k, rel in enumerate(ALL_RELS)]
        return [own] + remote

    def start(srcs, dsts, sems):
        for cp in copies(srcs, dsts, sems):
            cp.start()

    def finish(srcs, dsts, sems):
        for cp in copies(srcs, dsts, sems):
            cp.wait()

    return _Exchange([payload], [_sds((N_DEV,) + payload.shape, payload.dtype)],
                     [pltpu.SemaphoreType.DMA((n_r,)), pltpu.SemaphoreType.DMA((n_r,)), pltpu.SemaphoreType.DMA((1,))],
                     start, finish)


def _scatter_exchange(payloads, rels, src_view, view_shapes):
    n_p = len(payloads)
    n_r = len(rels)

    def copies(srcs, dsts, sems):
        me = _me()
        out = []
        for k, rel in enumerate(rels):
            peer = _peer(me, rel)
            for p in range(n_p):
                out.append(pltpu.make_async_remote_copy(
                    src_ref=src_view(srcs[p], peer), dst_ref=dsts[p].at[k],
                    send_sem=sems[0].at[p * n_r + k], recv_sem=sems[1].at[p * n_r + k],
                    device_id=peer, device_id_type=MESH))
        return out

    def start(srcs, dsts, sems):
        for cp in copies(srcs, dsts, sems):
            cp.start()

    def finish(srcs, dsts, sems):
        for cp in copies(srcs, dsts, sems):
            cp.wait()

    return _Exchange(payloads, [_sds((n_r,) + vs, p.dtype) for vs, p in zip(view_shapes, payloads)],
                     [pltpu.SemaphoreType.DMA((n_p * n_r,)), pltpu.SemaphoreType.DMA((n_p * n_r,))], start, finish)


def _split_refs(refs, sizes):
    out, at = [], 0
    for n in sizes:
        out.append(refs[at:at + n])
        at += n
    return out


def _join(exchanges):
    n_in = [len(e.inputs) for e in exchanges]
    n_out = [len(e.out_shape) for e in exchanges]
    n_sc = [len(e.scratch) for e in exchanges]

    def phase(name):
        def run(ins, outs, scs):
            for e, i, o, s in zip(exchanges, _split_refs(ins, n_in), _split_refs(outs, n_out), _split_refs(scs, n_sc)):
                if getattr(e, name) is not None:
                    getattr(e, name)(i, o, s)
        return run

    return _Exchange([a for e in exchanges for a in e.inputs], [s for e in exchanges for s in e.out_shape],
                     [s for e in exchanges for s in e.scratch], phase("start"), phase("finish"),
                     phase("forward") if any(e.forward is not None for e in exchanges) else None)


def _run_exchanges(name, exchanges):
    n_in = [len(e.inputs) for e in exchanges]
    n_out = [len(e.out_shape) for e in exchanges]
    n_sc = [len(e.scratch) for e in exchanges]

    def body(*refs):
        ins, outs, scs = _split_refs(refs, [sum(n_in), sum(n_out), sum(n_sc)])
        parts = list(zip(exchanges, _split_refs(ins, n_in), _split_refs(outs, n_out), _split_refs(scs, n_sc)))
        for e, i, o, s in parts:
            e.start(i, o, s)
        for e, i, o, s in parts:
            if e.forward is not None:
                e.forward(i, o, s)
        for e, i, o, s in parts:
            e.finish(i, o, s)

    outs = pl.pallas_call(
        body, name=name, in_specs=[_ANY] * sum(n_in), out_specs=[_ANY] * sum(n_out),
        out_shape=[sd for e in exchanges for sd in e.out_shape],
        scratch_shapes=[sc for e in exchanges for sc in e.scratch],
    )(*[a for e in exchanges for a in e.inputs])
    return _split_refs(list(outs), n_out)


def _call(body, *, name, grid, in_specs, out_specs, out_shape, scratch_shapes, args, exchange=None, forward_step=None):
    n_in, n_out, n_sc = len(in_specs), len(out_specs), len(scratch_shapes)
    if exchange is None:
        outs = pl.pallas_call(body, name=name, grid=grid, in_specs=in_specs, out_specs=out_specs, out_shape=out_shape,
                              scratch_shapes=scratch_shapes, compiler_params=_params())(*args)
        return list(outs), []
    e = exchange
    sizes = [n_in, len(e.inputs), n_out, len(e.out_shape), n_sc, len(e.scratch)]
    last = grid[0] - 1

    def wrapped(*refs):
        a, ei, o, eo, sc, es = _split_refs(refs, sizes)
        i = pl.program_id(0)

        @pl.when(i == 0)
        def _():
            e.start(ei, eo, es)

        if e.forward is not None:
            @pl.when(i == forward_step)
            def _():
                e.forward(ei, eo, es)

        body(*a, *o, *sc)

        @pl.when(i == last)
        def _():
            e.finish(ei, eo, es)

    outs = pl.pallas_call(
        wrapped, name=name, grid=grid,
        in_specs=list(in_specs) + [_ANY] * len(e.inputs), out_specs=list(out_specs) + [_ANY] * len(e.out_shape),
        out_shape=list(out_shape) + e.out_shape, scratch_shapes=list(scratch_shapes) + e.scratch,
        compiler_params=_params(),
    )(*args, *e.inputs)
    outs = list(outs)
    return outs[:n_out], outs[n_out:]


def _pair_sum(grads, recvd, my_core, name):
    n_p = len(grads)

    def body(c_ref, *refs):
        del c_ref
        for p in range(n_p):
            refs[2 * n_p + p][...] = (refs[p][...].astype(F32) + refs[n_p + p][...].astype(F32)).astype(BF16)

    def blk(g):
        return (None, None) + g.shape[2:]

    return pl.pallas_call(
        body, name=name,
        grid_spec=pltpu.PrefetchScalarGridSpec(
            num_scalar_prefetch=1, grid=(N_CHIP,),
            in_specs=[pl.BlockSpec(blk(g), lambda j, c: (j, c[0], 0, 0)) for g in grads]
            + [pl.BlockSpec(blk(g), lambda j, c: (0, j, 0, 0)) for g in grads],
            out_specs=[pl.BlockSpec((None,) + g.shape[2:], lambda j, c: (j, 0, 0)) for g in grads]),
        out_shape=[_sds((N_CHIP,) + g.shape[2:], BF16) for g in grads],
        compiler_params=pltpu.CompilerParams(dimension_semantics=("arbitrary",), vmem_limit_bytes=VMEM_LIMIT),
    )(my_core, *grads, *recvd)


def _chip_sum(psums, recvd, my_chip, name):
    n_p = len(psums)

    def body(c_ref, *refs):
        del c_ref
        for p in range(n_p):
            acc = refs[p][...].astype(F32)
            for k in range(len(CHIP_RELS)):
                acc = acc + refs[n_p + p][k].astype(F32)
            refs[2 * n_p + p][...] = acc

    return pl.pallas_call(
        body, name=name,
        grid_spec=pltpu.PrefetchScalarGridSpec(
            num_scalar_prefetch=1, grid=(1,),
            in_specs=[pl.BlockSpec((None,) + g.shape[1:], lambda i, c: (c[0], 0, 0)) for g in psums]
            + [pl.BlockSpec(r.shape, lambda i, c: (0, 0, 0)) for r in recvd],
            out_specs=[pl.BlockSpec(g.shape[1:], lambda i, c: (0, 0)) for g in psums]),
        out_shape=[_sds(g.shape[1:], F32) for g in psums],
        compiler_params=pltpu.CompilerParams(dimension_semantics=("arbitrary",), vmem_limit_bytes=VMEM_LIMIT),
    )(my_chip, *psums, *recvd)


def _sum_devices(parts, shapes):
    plan = _small_rows(shapes)

    def body(p_ref, *refs):
        outs, tot_ref = refs[:-1], refs[-1]
        acc = p_ref[0]
        for j in range(1, N_DEV):
            acc = acc + p_ref[j]
        tot_ref[...] = acc
        for idx, r, c0, width, at in plan:
            outs[idx][r:r + 1, c0:c0 + width] = tot_ref[at:at + 1, 0:width]

    return pl.pallas_call(body, name="small_grad_sum", out_shape=[_sds(s, F32) for s in shapes],
                          scratch_shapes=[pltpu.VMEM(parts.shape[1:], F32)])(parts)


def _transposed(w):
    r, c = w.shape
    tr = MXU_COLS

    def body(w_ref, o_ref):
        o_ref[...] = w_ref[...].T

    return pl.pallas_call(
        body, name="transpose_w_down", grid=(r // tr,),
        in_specs=[pl.BlockSpec((tr, c), lambda i: (i, 0))], out_specs=pl.BlockSpec((c, tr), lambda i: (0, i)),
        out_shape=_sds((c, r), w.dtype), compiler_params=_params())(w)


def _cast_shards(shards):
    def body(*refs):
        for src, dst in zip(refs[:len(shards)], refs[len(shards):]):
            dst[...] = src[...].astype(BF16)

    return pl.pallas_call(body, name="cast_shards", out_shape=[_sds(a.shape, BF16) for a in shards],
                          compiler_params=pltpu.CompilerParams(vmem_limit_bytes=VMEM_LIMIT))(*shards)


def _adamw(ws, gs, ms, vs, name):
    n_t = len(ws)

    def body(*refs):
        w_refs, g_refs, m_refs, v_refs = (refs[j * n_t:(j + 1) * n_t] for j in range(4))
        outs = refs[4 * n_t:]
        for k in range(n_t):
            gg = g_refs[k][...]
            mn = ADAM_B1 * m_refs[k][...] + (1.0 - ADAM_B1) * gg
            vn = ADAM_B2 * v_refs[k][...] + (1.0 - ADAM_B2) * (gg * gg)
            m_hat = mn / (1.0 - ADAM_B1 ** ADAM_STEP)
            v_hat = vn / (1.0 - ADAM_B2 ** ADAM_STEP)
            outs[3 * k][...] = -ADAM_LR * (m_hat / (jnp.sqrt(v_hat) + ADAM_EPS) + ADAM_WD * w_refs[k][...])
            outs[3 * k + 1][...] = mn
            outs[3 * k + 2][...] = vn

    out_shape = [_sds(w.shape, F32) for w in ws for _ in range(3)]
    return pl.pallas_call(body, name=name, out_shape=out_shape,
                          compiler_params=pltpu.CompilerParams(vmem_limit_bytes=VMEM_LIMIT))(*ws, *gs, *ms, *vs)


class _Mesh:
    def __init__(self, shards, my_chip, my_core):
        self.shards, self.my_chip, self.my_core = shards, my_chip.reshape(1), my_core.reshape(1)

    def gather(self, names):
        return _all_gather([self.shards[n] for n in names])

    @staticmethod
    def whole(gathered):
        return gathered.reshape(N_DEV * gathered.shape[1], gathered.shape[2])

    @staticmethod
    def by_device(grads):
        return [g.reshape(N_CHIP, 2, g.shape[0] // N_DEV, g.shape[1]) for g in grads]

    @staticmethod
    def to_sibling(parts):
        return _scatter_exchange(parts, CORE_RELS, lambda ref, peer: ref.at[:, peer[2]],
                                 [(N_CHIP,) + p.shape[2:] for p in parts])

    @staticmethod
    def to_chips(pair):
        return _scatter_exchange(pair, CHIP_RELS, lambda ref, peer: ref.at[_chip_slot(peer)], [p.shape[1:] for p in pair])


def _step(x, target, g1, w_in_t, wa, wb, bb, lg, lb, w_out, g2, w_gate_t, w_up_t, wf, w_down, g3, ts, mesh=None):
    (z, h1), got = _fwd_in(x, g1, w_in_t, ts, exchange=mesh and mesh.gather(["w_out", "w_down"]))
    if mesh:
        w_out, w_down = [mesh.whole(g) for g in got]
    (x2, y, u), got = _fwd_mix(z, x, wa, wb, bb, lg, lb, w_out, ts, exchange=mesh and mesh.gather(["w_gate", "w_up"]))
    if mesh:
        w_gate_t, w_up_t = [mesh.whole(g) for g in got]
    (g0, v, h2, w_down_t), _ = _fwd_ffn_in(x2, g2, w_gate_t, w_up_t, ts, w_down=w_down)
    tk = min(ts, SKEW_TILE)
    a, dx3, dx3b, loss, dg3, dgc, dv, dwf = _ffn_out_and_back(g0, v, x2, wf, w_down, w_down_t, g3, target, tk)
    one = dict(w_down=_matmul_tn(a, dx3b, "wgrad_down"), w_up=_matmul_tn(dv, h2, "wgrad_up"))
    parts1 = mesh and mesh.by_device(list(one.values()))
    (dg0, dx2, dx2b, dg2), got = _bwd_ffn_b(dgc, dv, wf, w_gate_t, w_up_t, x2, g2, dx3, tk,
                                            exchange=mesh and mesh.to_sibling(parts1))
    pair1 = mesh and _pair_sum(parts1, got, mesh.my_core, "rs_pair_sum_1")
    two = dict(w_gate=_matmul_tn(dg0, h2, "wgrad_gate"), w_out=_matmul_tn(y, dx2b, "wgrad_out"))
    parts2 = mesh and mesh.by_device(list(two.values()))
    (dca, du, dab, dwa, dlg, dlb, dbb), got = _bwd_mix_a(
        dx2b, jnp.swapaxes(w_out, 0, 1), z, u, wa, lg, lb, tk,
        exchange=mesh and _join([mesh.to_chips(pair1), mesh.to_sibling(parts2)]))
    (dz, dx, dg1, dwb), _ = _bwd_mix_b(dca, du, z, dab, wa, wb, w_in_t, x, g1, dx2, tk)
    small = dict(norm_mix_g=dg1, conv_a_w=dwa, conv_b_w=dwb, conv_b_b=dbb, ln_b_g=dlg, ln_b_b=dlb,
                 norm_ffn_g=dg2, conv_ffn_w=dwf, norm_final_g=dg3)
    if not mesh:
        return loss, dx, dict(w_in=_matmul_tn(dz, h1, "wgrad_in"), **one, **two), small
    big = dict(zip(one, _chip_sum(pair1, got[:2], mesh.my_chip, "rs_chip_sum_1")))
    pair2 = _pair_sum(parts2, got[2:], mesh.my_core, "rs_pair_sum_2")
    dw_in_t, got = _matmul_tn(dz, h1, "wgrad_in",
                              exchange=_join([mesh.to_chips(pair2), _gather_direct(_pack_small_grads(small, loss))]))
    big.update(zip(two, _chip_sum(pair2, got[:2], mesh.my_chip, "rs_chip_sum_2")))
    every = got[2]
    parts = mesh.by_device([dw_in_t])
    (got,) = _run_exchanges("rs_cores_last", [mesh.to_sibling(parts)])
    pair = _pair_sum(parts, got, mesh.my_core, "rs_pair_sum_last")
    (got,) = _run_exchanges("rs_chips_last", [mesh.to_chips(pair)])
    (big["w_in"],) = _chip_sum(pair, got, mesh.my_chip, "rs_chip_sum_last")
    shapes = [loss.shape if n == "loss" else small[n].shape for n in _SMALL_NAMES]
    return None, dx, big, dict(zip(_SMALL_NAMES, _sum_devices(every, shapes)))


def _pack_small_weights(conv_a_s, conv_b_s, conv_ffn_s):
    def body(a_ref, b_ref, f_ref, out):
        out[...] = jnp.zeros_like(out)
        at = 0
        for src in (a_ref, b_ref, f_ref):
            rows, cols = src.shape
            for r in range(rows):
                out[at:at + 1, 0:cols] = src[r:r + 1, :]
                at += 1

    return pl.pallas_call(body, name="small_weight_pack", out_shape=_sds((SMALL_W_ROWS, SMALL_W_COLS), F32))(
        conv_a_s, conv_b_s, conv_ffn_s)


def _unpack_small_weights(full):
    def take(r0, k, w):
        return jnp.transpose(full[:, r0:r0 + k, 0:w], (1, 0, 2)).reshape(k, N_DEV * w)

    return take(0, K_A, CONV_A_COLS), take(K_A, K_B, CONV_A_COLS), take(K_A + K_B, K_F, W_FF_COLS)


_SMALL_NAMES = ("conv_b_w", "conv_a_w", "conv_ffn_w", "norm_mix_g", "norm_ffn_g", "norm_final_g",
                "conv_b_b", "ln_b_g", "ln_b_b", "loss")
SMALL_G_ROWS = 64


def _small_rows(shapes):
    plan, at = [], 0
    for idx, (rows, cols) in enumerate(shapes):
        for r in range(rows):
            for c0 in range(0, cols, SMALL_G_COLS):
                plan.append((idx, r, c0, min(SMALL_G_COLS, cols - c0), at))
                at += 1
    assert at <= SMALL_G_ROWS
    return plan


def _pack_small_grads(small, loss):
    srcs = [loss if n == "loss" else small[n] for n in _SMALL_NAMES]
    plan = _small_rows([a.shape for a in srcs])

    def body(*refs):
        out = refs[-1]
        out[...] = jnp.zeros_like(out)
        for idx, r, c0, width, at in plan:
            out[at:at + 1, 0:width] = refs[idx][r:r + 1, c0:c0 + width]

    return pl.pallas_call(body, name="small_grad_pack", out_shape=_sds((SMALL_G_ROWS, SMALL_G_COLS), F32))(*srcs)


def kernel(x, norm_mix_g, w_in, conv_a_w, conv_b_w, conv_b_b, ln_b_g, ln_b_b, w_out, norm_ffn_g, w_gate, w_up, conv_ffn_w, w_down, norm_final_g, loss_target, m_norm_mix_g, m_w_in, m_conv_a_w, m_conv_b_w, m_conv_b_b, m_ln_b_g, m_ln_b_b, m_w_out, m_norm_ffn_g, m_w_gate, m_w_up, m_conv_ffn_w, m_w_down, m_norm_final_g, v_norm_mix_g, v_w_in, v_conv_a_w, v_conv_b_w, v_conv_b_b, v_ln_b_g, v_ln_b_b, v_w_out, v_norm_ffn_g, v_w_gate, v_w_up, v_conv_ffn_w, v_w_down, v_norm_final_g):
    ix, iy, ic = lax.axis_index("x"), lax.axis_index("y"), lax.axis_index("c")
    my_chip = (2 * ix + iy).astype(jnp.int32)
    my_core = ic.astype(jnp.int32)
    my_dev = 2 * my_chip + my_core

    weights = dict(norm_mix_g=norm_mix_g, w_in=w_in, conv_a_w=conv_a_w, conv_b_w=conv_b_w, conv_b_b=conv_b_b,
                   ln_b_g=ln_b_g, ln_b_b=ln_b_b, w_out=w_out, norm_ffn_g=norm_ffn_g, w_gate=w_gate, w_up=w_up,
                   conv_ffn_w=conv_ffn_w, w_down=w_down, norm_final_g=norm_final_g)
    m_in = dict(norm_mix_g=m_norm_mix_g, w_in=m_w_in, conv_a_w=m_conv_a_w, conv_b_w=m_conv_b_w, conv_b_b=m_conv_b_b,
                ln_b_g=m_ln_b_g, ln_b_b=m_ln_b_b, w_out=m_w_out, norm_ffn_g=m_norm_ffn_g, w_gate=m_w_gate,
                w_up=m_w_up, conv_ffn_w=m_conv_ffn_w, w_down=m_w_down, norm_final_g=m_norm_final_g)
    v_in = dict(norm_mix_g=v_norm_mix_g, w_in=v_w_in, conv_a_w=v_conv_a_w, conv_b_w=v_conv_b_w, conv_b_b=v_conv_b_b,
                ln_b_g=v_ln_b_g, ln_b_b=v_ln_b_b, w_out=v_w_out, norm_ffn_g=v_norm_ffn_g, w_gate=v_w_gate,
                w_up=v_w_up, conv_ffn_w=v_conv_ffn_w, w_down=v_w_down, norm_final_g=v_norm_final_g)
    order = list(weights)
    big_names = ("w_in", "w_gate", "w_up", "w_out", "w_down")
    transposed = ("w_in", "w_gate", "w_up")

    def shard2d(name, a):
        if name in transposed:
            return jnp.swapaxes(a[0], 0, 1)
        return a.reshape(1, a.shape[0]) if a.ndim == 1 else a.reshape(a.shape[-2:])

    def unshard2d(name, a2, like):
        if name in transposed:
            return jnp.swapaxes(a2, 0, 1)[None]
        return a2.reshape(like.shape)

    mesh = _Mesh(dict(zip(big_names, _cast_shards([shard2d(n, weights[n]) for n in big_names]))), my_chip, my_core)
    gathered, = _run_exchanges("ag_first", [_all_gather(
        [mesh.shards["w_in"], _pack_small_weights(conv_a_w[0], conv_b_w[0], conv_ffn_w[0])])])
    w_in_t = mesh.whole(gathered[0])
    wa_f, wb_f, wf_f = _unpack_small_weights(gathered[1])

    _, dx, gsum, stot = _step(
        x[0], loss_target[0], norm_mix_g, w_in_t, wa_f, wb_f, conv_b_b, ln_b_g, ln_b_b, None, norm_ffn_g,
        None, None, wf_f, None, norm_final_g.reshape(1, D_MODEL), SEQ_TILE, mesh)

    grads2d = dict(
        norm_mix_g=stot["norm_mix_g"],
        conv_a_w=lax.dynamic_slice(stot["conv_a_w"], (0, my_dev * CONV_A_COLS), (K_A, CONV_A_COLS)),
        conv_b_w=lax.dynamic_slice(stot["conv_b_w"], (0, my_dev * CONV_A_COLS), (K_B, CONV_A_COLS)),
        conv_b_b=stot["conv_b_b"], ln_b_g=stot["ln_b_g"], ln_b_b=stot["ln_b_b"],
        norm_ffn_g=stot["norm_ffn_g"],
        conv_ffn_w=lax.dynamic_slice(stot["conv_ffn_w"], (0, my_dev * W_FF_COLS), (K_F, W_FF_COLS)),
        norm_final_g=stot["norm_final_g"],
        **gsum,
    )

    updates = {}
    small_names = [n for n in order if n not in big_names]
    for group, label in [([n], "adamw_" + n) for n in big_names] + [(small_names, "adamw_small")]:
        outs = _adamw([shard2d(n, weights[n]) for n in group], [grads2d[n] for n in group],
                      [shard2d(n, m_in[n]) for n in group], [shard2d(n, v_in[n]) for n in group], label)
        for k, n in enumerate(group):
            updates[n] = outs[3 * k:3 * k + 3]
    g_out = [unshard2d(n, grads2d[n], weights[n]) for n in order]
    d_out, m_out, v_out = [[unshard2d(n, updates[n][j], weights[n]) for n in order] for j in range(3)]

    return (stot["loss"][0, 0], dx[None], *g_out, *d_out, *m_out, *v_out)
```

```python
import jax
import jax.numpy as jnp
from jax import lax
from jax.experimental import pallas as pl
from jax.experimental.pallas import tpu as pltpu

F32 = jnp.float32
BF16 = jnp.bfloat16

D_MODEL = 1024
D_A = 512
D_B = 512
D_IN = 3 * D_A + 2 * D_B
D_FF = 2816
K_A = 3
K_B = 31
K_F = 3
RMS_EPS = 1e-6
LN_EPS = 1e-5

ADAM_LR = 0.001
ADAM_B1 = 0.9
ADAM_B2 = 0.999
ADAM_EPS = 1e-08
ADAM_WD = 0.01
ADAM_STEP = 10

N_DEV = 8
N_CHIP = 4
LANES = 128
SUBLANES = 8
HALO = 16
ROW_CHUNK = 64
SEQ_TILE = 512
SKEW_TILE = 256
VMEM_LIMIT = 56 * 1024 * 1024

MESH = pl.DeviceIdType.MESH

W_FF_COLS = D_FF // N_DEV
CONV_A_COLS = D_A // N_DEV
SMALL_W_ROWS = 40
SMALL_W_COLS = 384
SMALL_G_COLS = 512


def _rows(ts, c):
    return pl.BlockSpec((ts, c), lambda i: (i, 0))


def _const(shape):
    return pl.BlockSpec(shape, lambda i: (0,) * len(shape), pipeline_mode=pl.Buffered(1))


def _acc_out(shape):
    return pl.BlockSpec(shape, lambda i: (0,) * len(shape))


def _rows_at(ts, c, tile):
    return pl.BlockSpec((ts, c), lambda i: (tile(i), 0))


def _prev_at(ts, c, tile):
    return pl.BlockSpec((HALO, c), lambda i: (jnp.maximum(tile(i) * (ts // HALO) - 1, 0), 0))


def _next_at(ts, c, s, tile):
    last = s // HALO - 1
    return pl.BlockSpec((HALO, c), lambda i: (jnp.minimum((tile(i) + 1) * (ts // HALO), last), 0))


def _prev(ts, c):
    return _prev_at(ts, c, lambda i: i)


def _next(ts, c, s):
    return _next_at(ts, c, s, lambda i: i)


MXU_COLS = 256
MXU_ROWS = 256


def _col_pieces(n):
    return [(c0, min(MXU_COLS, n - c0)) for c0 in range(0, n, MXU_COLS)]


def _matmul_pieces(terms, out_ref, k_parts, w_transposed=False):
    m, n = out_ref.shape
    rows = min(MXU_ROWS, m)
    steps = []
    for lhs_ref, w_ref in terms:
        tiles = lhs_ref.shape[1] // MXU_COLS
        cuts = [MXU_COLS * (tiles * j // k_parts) for j in range(k_parts)] + [lhs_ref.shape[1]]
        steps += [(lhs_ref, w_ref, cuts[j], cuts[j + 1]) for j in range(k_parts)]

    def piece(m0, n0, width, step):
        lhs_ref, w_ref, k0, k1 = steps[step]
        if w_transposed:
            part = lax.dot_general(lhs_ref[m0:m0 + rows, k0:k1], w_ref[n0:n0 + width, k0:k1], _NT,
                                   preferred_element_type=F32)
        else:
            part = jnp.dot(lhs_ref[m0:m0 + rows, k0:k1], w_ref[k0:k1, n0:n0 + width], preferred_element_type=F32)
        if step:
            part = part + out_ref[m0:m0 + rows, n0:n0 + width]
        out_ref[m0:m0 + rows, n0:n0 + width] = part

    return [(piece, (m0, n0, w, j)) for j in range(len(steps)) for n0, w in _col_pieces(n) for m0 in range(0, m, rows)]


def _interleaved(vector_units, matmul_pieces):
    n_u, n_p = len(vector_units), len(matmul_pieces)
    done = 0
    for k, (unit, args) in enumerate(vector_units):
        while done < n_p and done * n_u <= k * n_p:
            matmul_pieces[done][0](*matmul_pieces[done][1])
            done += 1
        unit(*args)
    for fn, args in matmul_pieces[done:]:
        fn(*args)


def _params():
    return pltpu.CompilerParams(dimension_semantics=("arbitrary",), vmem_limit_bytes=VMEM_LIMIT)


def _sds(shape, dtype):
    return jax.ShapeDtypeStruct(shape, dtype)


def _sigmoid(v):
    return 0.5 * jnp.tanh(0.5 * v) + 0.5


def _conv_block(ext_ref, w_ref, r0, rc, l0, k_taps, transposed):
    acc = None
    for k in range(k_taps):
        d = (k_taps // 2 - k) if transposed else (k - k_taps // 2)
        term = ext_ref[l0 // LANES, pl.ds(r0 + HALO + d, rc), :] * w_ref[k:k + 1, l0:l0 + LANES]
        acc = term if acc is None else acc + term
    return acc


def _conv_wgrad_block(acc_ref, dout, ext_ref, r0, rc, l0, k_taps, scale=None):
    for k in range(k_taps):
        prod = dout * ext_ref[l0 // LANES, pl.ds(r0 + HALO + k - k_taps // 2, rc), :]
        part = prod.reshape(rc // SUBLANES, SUBLANES, LANES).sum(axis=0)
        if scale is not None:
            part = part * scale
        acc_ref[k, :, l0:l0 + LANES] = acc_ref[k, :, l0:l0 + LANES] + part


def _reduce_acc(out_ref, acc_ref, k_taps):
    for k in range(k_taps):
        out_ref[k:k + 1, :] = jnp.sum(acc_ref[k], axis=0, keepdims=True)


def _fold8(v):
    rc, c = v.shape
    return v.reshape(rc // SUBLANES, SUBLANES, c).sum(axis=0)


def _ext_scratch(ts, c):
    return pltpu.VMEM((c // LANES, ts + 2 * HALO, LANES), F32)


def _put_rows(ext_ref, r0, rc, val):
    for q in range(val.shape[1] // LANES):
        ext_ref[q, pl.ds(r0 + HALO, rc), :] = val[:, q * LANES:(q + 1) * LANES]


def _fill_halo(ext_ref, vals_prev, vals_next, ts, first, last):
    for q in range(vals_prev.shape[1] // LANES):
        cols = slice(q * LANES, (q + 1) * LANES)
        ext_ref[q, 0:HALO, :] = jnp.where(first, 0.0, vals_prev[:, cols])
        ext_ref[q, HALO + ts:HALO + ts + HALO, :] = jnp.where(last, 0.0, vals_next[:, cols])


def _rms_bwd_rows(dh, xf, g):
    r = lax.rsqrt(jnp.mean(xf * xf, axis=-1, keepdims=True) + RMS_EPS)
    xhat = xf * r
    dxh = dh * g
    dx = r * (dxh - xhat * jnp.mean(dxh * xhat, axis=-1, keepdims=True))
    return dx, dh * xhat


_NT = (((1,), (1,)), ((), ()))
_TN = (((0,), (0,)), ((), ()))


def _fwd_in(x, g1, w_in_t, ts, exchange=None):
    s = x.shape[0]

    def body(x_ref, g_ref, w_ref, z_ref, h_ref):
        xf = x_ref[...]
        r = lax.rsqrt(jnp.mean(xf * xf, axis=-1, keepdims=True) + RMS_EPS)
        h = (xf * r * g_ref[...]).astype(BF16)
        h_ref[...] = h
        for n0 in range(0, D_IN, 512):
            z_ref[:, n0:n0 + 512] = lax.dot_general(h, w_ref[n0:n0 + 512, :], _NT,
                                                    preferred_element_type=F32).astype(BF16)

    return _call(
        body, name="fwd_in", grid=(s // ts,),
        in_specs=[_rows(ts, D_MODEL), _const((1, D_MODEL)), _const((D_IN, D_MODEL))],
        out_specs=[_rows(ts, D_IN), _rows(ts, D_MODEL)],
        out_shape=[_sds((s, D_IN), BF16), _sds((s, D_MODEL), BF16)],
        scratch_shapes=[], args=(x, g1, w_in_t), exchange=exchange, forward_step=(s // ts) * 3 // 4)


def _p_u0(z_ref, rows):
    a_h = z_ref[rows, 0:D_A].astype(F32)
    a_c = z_ref[rows, 2 * D_A:3 * D_A].astype(F32)
    b_v = z_ref[rows, 3 * D_A:3 * D_A + D_B].astype(F32)
    b_g = z_ref[rows, 3 * D_A + D_B:D_IN].astype(F32)
    return a_c * a_h, b_v * _sigmoid(b_g)


def _layernorm_rows(u_blocks):
    tot = None
    for ub in u_blocks:
        sm = jnp.sum(ub, axis=-1, keepdims=True)
        tot = sm if tot is None else tot + sm
    mu = tot * (1.0 / D_B)
    var = None
    for ub in u_blocks:
        sq = jnp.sum((ub - mu) * (ub - mu), axis=-1, keepdims=True)
        var = sq if var is None else var + sq
    rstd = lax.rsqrt(var * (1.0 / D_B) + LN_EPS)
    return mu, rstd


def _fwd_mix(z, x, wa, wb, bb, lg, lb, w_out, ts, exchange=None):
    s = x.shape[0]
    nt = s // ts
    rc = min(ROW_CHUNK, ts)

    def body(z_ref, zp_ref, zn_ref, x_ref, wa_ref, wb_ref, bb_ref, lg_ref, lb_ref, wo_ref,
             x2_ref, y_ref, u_ref, pe_ref, ue_ref):
        i = pl.program_id(0)
        pp, up = _p_u0(zp_ref, slice(None))
        pn, un = _p_u0(zn_ref, slice(None))
        _fill_halo(pe_ref, pp, pn, ts, i == 0, i == nt - 1)
        _fill_halo(ue_ref, up, un, ts, i == 0, i == nt - 1)

        def fill(j, carry):
            r0 = pl.multiple_of(j * rc, rc)
            p, u0 = _p_u0(z_ref, pl.ds(r0, rc))
            _put_rows(pe_ref, r0, rc, p)
            _put_rows(ue_ref, r0, rc, u0)
            return carry

        lax.fori_loop(0, ts // rc, fill, 0)

        def mixer_a(r0, l0):
            rows = pl.ds(r0, rc)
            ca = _conv_block(pe_ref, wa_ref, r0, rc, l0, K_A, False)
            a_b = z_ref[rows, D_A + l0:D_A + l0 + LANES].astype(F32)
            y_ref[rows, l0:l0 + LANES] = (a_b * ca).astype(BF16)

        def conv_b(r0, l0):
            u_ref[pl.ds(r0, rc), l0:l0 + LANES] = (_conv_block(ue_ref, wb_ref, r0, rc, l0, K_B, False)
                                                   + bb_ref[:, l0:l0 + LANES])

        def norm_b(r0):
            rows = pl.ds(r0, rc)
            ubs = [u_ref[rows, l0:l0 + LANES] for l0 in range(0, D_B, LANES)]
            mu, rstd = _layernorm_rows(ubs)
            for q, l0 in enumerate(range(0, D_B, LANES)):
                t = (ubs[q] - mu) * rstd * lg_ref[:, l0:l0 + LANES] + lb_ref[:, l0:l0 + LANES]
                y_ref[rows, D_A + l0:D_A + l0 + LANES] = (t * _sigmoid(t)).astype(BF16)

        def chunk_units(r0):
            return ([(mixer_a, (r0, l0)) for l0 in range(0, D_A, LANES)]
                    + [(conv_b, (r0, l0)) for l0 in range(0, D_B, LANES)] + [(norm_b, (r0,))])

        def main(j, carry):
            for fn, args in chunk_units(pl.multiple_of(j * rc, rc)):
                fn(*args)
            return carry

        def project(m0, n0, width):
            rows = slice(m0, m0 + hrows)
            x2_ref[rows, n0:n0 + width] = x_ref[rows, n0:n0 + width] + jnp.dot(
                y_ref[rows, :], wo_ref[:, n0:n0 + width], preferred_element_type=F32)

        hrows = ts // 2
        lax.fori_loop(0, hrows // rc, main, 0, unroll=4)
        _interleaved([u for q in range(hrows // rc) for u in chunk_units(hrows + q * rc)],
                     [(project, (0, n0, w)) for n0, w in _col_pieces(D_MODEL)])
        for n0, w in _col_pieces(D_MODEL):
            project(hrows, n0, w)

    return _call(
        body, name="fwd_mix", grid=(nt,),
        in_specs=[_rows(ts, D_IN), _prev(ts, D_IN), _next(ts, D_IN, s), _rows(ts, D_MODEL),
                  _const((K_A, D_A)), _const((K_B, D_B)), _const((1, D_B)), _const((1, D_B)), _const((1, D_B)),
                  _const((D_MODEL, D_MODEL))],
        out_specs=[_rows(ts, D_MODEL), _rows(ts, D_MODEL), _rows(ts, D_B)],
        out_shape=[_sds((s, D_MODEL), F32), _sds((s, D_MODEL), BF16), _sds((s, D_B), F32)],
        scratch_shapes=[_ext_scratch(ts, D_A), _ext_scratch(ts, D_B)],
        args=(z, z, z, x, wa, wb, bb, lg, lb, w_out), exchange=exchange, forward_step=nt * 5 // 8)


def _fwd_ffn_in(x2, g2, w_gate_t, w_up_t, ts, exchange=None, w_down=None):
    s = x2.shape[0]
    half = D_FF // 2

    def body(x_ref, g_ref, wg_ref, wu_ref, g0_ref, v_ref, h_ref):
        xf = x_ref[...]
        r = lax.rsqrt(jnp.mean(xf * xf, axis=-1, keepdims=True) + RMS_EPS)
        h = (xf * r * g_ref[...]).astype(BF16)
        h_ref[...] = h
        for n0 in range(0, D_FF, half):
            g0_ref[:, n0:n0 + half] = lax.dot_general(h, wg_ref[n0:n0 + half, :], _NT,
                                                      preferred_element_type=F32).astype(BF16)
            v_ref[:, n0:n0 + half] = lax.dot_general(h, wu_ref[n0:n0 + half, :], _NT,
                                                     preferred_element_type=F32).astype(BF16)

    def body_and_transpose(x_ref, g_ref, wg_ref, wu_ref, wd_ref, g0_ref, v_ref, h_ref, wdt_ref):
        @pl.when(pl.program_id(0) < n_blk)
        def _():
            wdt_ref[...] = wd_ref[...].T

        body(x_ref, g_ref, wg_ref, wu_ref, g0_ref, v_ref, h_ref)

    in_specs = [_rows(ts, D_MODEL), _const((1, D_MODEL)), _const((D_FF, D_MODEL)), _const((D_FF, D_MODEL))]
    out_specs = [_rows(ts, D_FF), _rows(ts, D_FF), _rows(ts, D_MODEL)]
    out_shape = [_sds((s, D_FF), BF16), _sds((s, D_FF), BF16), _sds((s, D_MODEL), BF16)]
    n_blk = D_FF // MXU_COLS
    if w_down is None or s // ts < n_blk:
        outs, got = _call(body, name="fwd_ffn_in", grid=(s // ts,), in_specs=in_specs, out_specs=out_specs,
                          out_shape=out_shape, scratch_shapes=[], args=(x2, g2, w_gate_t, w_up_t), exchange=exchange,
                          forward_step=(s // ts) // 2)
        return outs + [None if w_down is None else _transposed(w_down)], got
    blk = lambda i: jnp.minimum(i, n_blk - 1)
    return _call(
        body_and_transpose, name="fwd_ffn_in", grid=(s // ts,),
        in_specs=in_specs + [pl.BlockSpec((MXU_COLS, D_MODEL), lambda i: (blk(i), 0))],
        out_specs=out_specs + [pl.BlockSpec((D_MODEL, MXU_COLS), lambda i: (0, blk(i)))],
        out_shape=out_shape + [_sds((D_MODEL, D_FF), BF16)],
        scratch_shapes=[], args=(x2, g2, w_gate_t, w_up_t, w_down), exchange=exchange, forward_step=(s // ts) // 2)


def _ffn_out_and_back(g0, v, x2, wf, w_down, w_down_t, g3, target, ts):
    s = x2.shape[0]
    nt = s // ts
    rc = min(ROW_CHUNK, ts // 2)

    def body(g0_ref, gp_ref, gn_ref, v_ref, x2_ref, wf_ref, wd_ref, wdt_ref, g3_ref, t_ref,
             a_ref, dx3_ref, dx3b_ref, loss_ref, dg3_ref, dg_ref, dv_ref, dwf_ref,
             ge_ref, silu_ref, dsv_ref, da_ref, acc_ref, p_ref, sums_ref):
        i = pl.program_id(0)

        @pl.when(i == 0)
        def _():
            acc_ref[...] = jnp.zeros_like(acc_ref)
            sums_ref[...] = jnp.zeros_like(sums_ref)

        _fill_halo(ge_ref, gp_ref[...].astype(F32), gn_ref[...].astype(F32), ts, i == 0, i == nt - 1)

        def fill(j, carry):
            r0 = pl.multiple_of(j * rc, rc)
            _put_rows(ge_ref, r0, rc, g0_ref[pl.ds(r0, rc), :].astype(F32))
            return carry

        lax.fori_loop(0, ts // rc, fill, 0)

        def act(r0, l0):
            rows = slice(r0, r0 + rc)
            g = _conv_block(ge_ref, wf_ref, r0, rc, l0, K_F, False)
            vv = v_ref[rows, l0:l0 + LANES].astype(F32)
            sg = _sigmoid(g)
            silu = g * sg
            a_ref[rows, l0:l0 + LANES] = (silu * vv).astype(BF16)
            silu_ref[rows, l0:l0 + LANES] = silu
            dsv_ref[rows, l0:l0 + LANES] = (sg + silu * (1.0 - sg)) * vv

        def tail(r0):
            rows = slice(r0, r0 + rc)
            x3 = x2_ref[rows, :] + p_ref[rows, :]
            r = lax.rsqrt(jnp.mean(x3 * x3, axis=-1, keepdims=True) + RMS_EPS)
            xhat = x3 * r
            diff = xhat * g3_ref[...] - t_ref[rows, :]
            dout = diff * (1.0 / D_MODEL)
            dxh = dout * g3_ref[...]
            dx3 = r * (dxh - xhat * jnp.mean(dxh * xhat, axis=-1, keepdims=True))
            dx3_ref[rows, :] = dx3
            dx3b_ref[rows, :] = dx3.astype(BF16)
            sums_ref[0] = sums_ref[0] + _fold8(diff * diff)
            sums_ref[1] = sums_ref[1] + _fold8(dout * xhat)

        def back(r0, l0):
            rows = slice(r0, r0 + rc)
            da = da_ref[rows, l0:l0 + LANES]
            dv_ref[rows, l0:l0 + LANES] = (da * silu_ref[rows, l0:l0 + LANES]).astype(BF16)
            dgg = da * dsv_ref[rows, l0:l0 + LANES]
            dg_ref[rows, l0:l0 + LANES] = dgg.astype(BF16)
            _conv_wgrad_block(acc_ref, dgg, ge_ref, r0, rc, l0, K_F)

        hrows = ts // 2

        def units(fn, h, per_lane_block):
            starts = [h * hrows + q * rc for q in range(hrows // rc)]
            if per_lane_block:
                return [(fn, (r0, l0)) for r0 in starts for l0 in range(0, D_FF, LANES)]
            return [(fn, (r0,)) for r0 in starts]

        def rows_of(ref, h):
            return ref.at[pl.ds(h * hrows, hrows), :]

        def product(h):
            return _matmul_pieces([(rows_of(a_ref, h), wd_ref)], rows_of(p_ref, h), 2)

        def grad_a(h):
            return _matmul_pieces([(rows_of(dx3b_ref, h), wdt_ref)], rows_of(da_ref, h), 1)

        _interleaved(units(act, 0, True), [])
        _interleaved(units(act, 1, True), product(0))
        _interleaved(units(tail, 0, False), product(1))
        _interleaved(units(tail, 1, False), grad_a(0))
        _interleaved(units(back, 0, True), grad_a(1))
        _interleaved(units(back, 1, True), [])

        @pl.when(i == nt - 1)
        def _():
            _reduce_acc(dwf_ref, acc_ref, K_F)
            loss_ref[...] = (0.5 / D_MODEL) * jnp.sum(sums_ref[0], keepdims=True)
            dg3_ref[...] = jnp.sum(sums_ref[1], axis=0, keepdims=True)

    return pl.pallas_call(
        body, name="ffn_out_and_back", grid=(nt,),
        in_specs=[_rows(ts, D_FF), _prev(ts, D_FF), _next(ts, D_FF, s), _rows(ts, D_FF), _rows(ts, D_MODEL),
                  _const((K_F, D_FF)), _const((D_FF, D_MODEL)), _const((D_MODEL, D_FF)), _const((1, D_MODEL)),
                  _rows(ts, D_MODEL)],
        out_specs=[_rows(ts, D_FF), _rows(ts, D_MODEL), _rows(ts, D_MODEL), _acc_out((1, 1)), _acc_out((1, D_MODEL)),
                   _rows(ts, D_FF), _rows(ts, D_FF), _acc_out((K_F, D_FF))],
        out_shape=[_sds((s, D_FF), BF16), _sds((s, D_MODEL), F32), _sds((s, D_MODEL), BF16),
                   _sds((1, 1), F32), _sds((1, D_MODEL), F32),
                   _sds((s, D_FF), BF16), _sds((s, D_FF), BF16), _sds((K_F, D_FF), F32)],
        scratch_shapes=[_ext_scratch(ts, D_FF), pltpu.VMEM((ts, D_FF), F32), pltpu.VMEM((ts, D_FF), F32),
                        pltpu.VMEM((ts, D_FF), F32), pltpu.VMEM((K_F, SUBLANES, D_FF), F32),
                        pltpu.VMEM((ts, D_MODEL), F32), pltpu.VMEM((2, SUBLANES, D_MODEL), F32)],
        compiler_params=_params(),
    )(g0, g0, g0, v, x2, wf, w_down, w_down_t, g3, target)


def _bwd_ffn_b(dg, dv, wf, w_gate, w_up, x2, g2, dx3, ts, exchange=None):
    s = x2.shape[0]
    nt = s // ts
    rc = min(ROW_CHUNK, ts)
    n_sub = ts // rc

    def body(dg_ref, dgp_ref, dgn_ref, dv_ref, wf_ref, wg_ref, wu_ref, x2_ref, g2_ref, dx3_ref,
             dg0_ref, dx2_ref, dx2b_ref, dgn2_ref, dge_ref, dg8_ref, a0_ref, a1_ref, p0_ref, p1_ref):
        i = pl.program_id(0)
        vt = jnp.minimum(i, nt - 1)
        live = (i >= 2).astype(F32)

        @pl.when(i == 0)
        def _():
            dg8_ref[...] = jnp.zeros_like(dg8_ref)
            a1_ref[...] = jnp.zeros_like(a1_ref)
            p1_ref[...] = jnp.zeros_like(p1_ref)

        _fill_halo(dge_ref, dgp_ref[...].astype(F32), dgn_ref[...].astype(F32), ts, vt == 0, vt == nt - 1)

        def fill(j, carry):
            r0 = pl.multiple_of(j * rc, rc)
            _put_rows(dge_ref, r0, rc, dg_ref[pl.ds(r0, rc), :].astype(F32))
            return carry

        lax.fori_loop(0, n_sub, fill, 0)

        def stage(a_new, a_old, p_new, p_old):
            def conv_t(r0, l0):
                rows = slice(r0, r0 + rc)
                dg0 = _conv_block(dge_ref, wf_ref, r0, rc, l0, K_F, True).astype(BF16)
                dg0_ref[rows, l0:l0 + LANES] = dg0
                a_new[rows, l0:l0 + LANES] = dg0

            def tail(r0):
                rows = slice(r0, r0 + rc)
                dx, dgrow = _rms_bwd_rows(p_old[rows, :], x2_ref[rows, :], g2_ref[...])
                dx2 = dx3_ref[rows, :] + dx
                dx2_ref[rows, :] = dx2
                dx2b_ref[rows, :] = dx2.astype(BF16)
                dg8_ref[...] = dg8_ref[...] + _fold8(dgrow) * live

            units = []
            for q in range(n_sub):
                units += [(conv_t, (q * rc, l0)) for l0 in range(0, D_FF, LANES)]
                units.append((tail, (q * rc,)))
            _interleaved(units, _matmul_pieces([(a_old, wg_ref), (dv_ref, wu_ref)], p_new, 2))

        @pl.when(i % 2 == 0)
        def _():
            stage(a0_ref, a1_ref, p0_ref, p1_ref)

        @pl.when(i % 2 == 1)
        def _():
            stage(a1_ref, a0_ref, p1_ref, p0_ref)

        @pl.when(i == nt + 1)
        def _():
            dgn2_ref[...] = jnp.sum(dg8_ref[...], axis=0, keepdims=True)

    vtile = lambda i: jnp.minimum(i, nt - 1)
    mtile = lambda i: jnp.clip(i - 1, 0, nt - 1)
    ttile = lambda i: jnp.clip(i - 2, 0, nt - 1)
    return _call(
        body, name="bwd_ffn_b", grid=(nt + 2,),
        in_specs=[_rows_at(ts, D_FF, vtile), _prev_at(ts, D_FF, vtile), _next_at(ts, D_FF, s, vtile),
                  _rows_at(ts, D_FF, mtile), _const((K_F, D_FF)),
                  _const((D_FF, D_MODEL)), _const((D_FF, D_MODEL)), _rows_at(ts, D_MODEL, ttile), _const((1, D_MODEL)),
                  _rows_at(ts, D_MODEL, ttile)],
        out_specs=[_rows_at(ts, D_FF, vtile), _rows_at(ts, D_MODEL, ttile), _rows_at(ts, D_MODEL, ttile),
                   _acc_out((1, D_MODEL))],
        out_shape=[_sds((s, D_FF), BF16), _sds((s, D_MODEL), F32), _sds((s, D_MODEL), BF16), _sds((1, D_MODEL), F32)],
        scratch_shapes=[_ext_scratch(ts, D_FF), pltpu.VMEM((SUBLANES, D_MODEL), F32),
                        pltpu.VMEM((ts, D_FF), BF16), pltpu.VMEM((ts, D_FF), BF16),
                        pltpu.VMEM((ts, D_MODEL), F32), pltpu.VMEM((ts, D_MODEL), F32)],
        args=(dg, dg, dg, dv, wf, w_gate, w_up, x2, g2, dx3), exchange=exchange)


def _bwd_mix_a(dx2b, w_out_t, z, u, wa, lg, lb, ts, exchange=None):
    s = dx2b.shape[0]
    nt = s // ts
    rc = min(ROW_CHUNK, ts)
    n_sub = ts // rc

    def body(dx_ref, wo_ref, z_ref, zp_ref, zn_ref, u_ref, wa_ref, lg_ref, lb_ref,
             dca_ref, du_ref, dab_ref, dwa_ref, dlg_ref, dlb_ref, dbb_ref, pe_ref, dy0_ref, dy1_ref, acc_ref, sacc_ref):
        i = pl.program_id(0)
        t = jnp.maximum(i - 1, 0)

        @pl.when(i == 0)
        def _():
            acc_ref[...] = jnp.zeros_like(acc_ref)
            sacc_ref[...] = jnp.zeros_like(sacc_ref)
            dy1_ref[...] = jnp.zeros_like(dy1_ref)

        pp, _ = _p_u0(zp_ref, slice(None))
        pn, _ = _p_u0(zn_ref, slice(None))
        _fill_halo(pe_ref, pp, pn, ts, t == 0, t == nt - 1)

        def fill(j, carry):
            r0 = pl.multiple_of(j * rc, rc)
            rows = pl.ds(r0, rc)
            _put_rows(pe_ref, r0, rc, z_ref[rows, 2 * D_A:3 * D_A].astype(F32) * z_ref[rows, 0:D_A].astype(F32))
            return carry

        lax.fori_loop(0, n_sub, fill, 0)

        def stage(dy_new, dy_old):
            def piece(m0, n0, width):
                dy_new[m0:m0 + MXU_ROWS, n0:n0 + width] = lax.dot_general(
                    dx_ref[m0:m0 + MXU_ROWS, :], wo_ref[n0:n0 + width, :], _NT, preferred_element_type=F32).astype(BF16)

            units = []
            for q in range(n_sub):
                units += [(mixer_a, (dy_old, q * rc, l0)) for l0 in range(0, D_A, LANES)]
                units.append((mixer_b, (dy_old, q * rc)))
            _interleaved(units, [(piece, (m0, n0, w)) for n0, w in _col_pieces(D_MODEL) for m0 in range(0, ts, MXU_ROWS)])

        def mixer_a(dy_ref, r0, l0):
            rows = slice(r0, r0 + rc)
            ca = _conv_block(pe_ref, wa_ref, r0, rc, l0, K_A, False)
            a_b = z_ref[rows, D_A + l0:D_A + l0 + LANES].astype(F32)
            dya = dy_ref[rows, l0:l0 + LANES].astype(F32)
            dab_ref[rows, l0:l0 + LANES] = (dya * ca).astype(BF16)
            dca = dya * a_b
            dca_ref[rows, l0:l0 + LANES] = dca
            _conv_wgrad_block(acc_ref, dca, pe_ref, r0, rc, l0, K_A)

        def mixer_b(dy_ref, r0):
            rows = slice(r0, r0 + rc)
            ubs = [u_ref[rows, l0:l0 + LANES] for l0 in range(0, D_B, LANES)]
            mu, rstd = _layernorm_rows(ubs)
            ns, dns = [], []
            m1 = None
            m2 = None
            for q, l0 in enumerate(range(0, D_B, LANES)):
                n = (ubs[q] - mu) * rstd
                lgq = lg_ref[:, l0:l0 + LANES]
                t = n * lgq + lb_ref[:, l0:l0 + LANES]
                sg = _sigmoid(t)
                dt = dy_ref[rows, D_A + l0:D_A + l0 + LANES].astype(F32) * (sg * (1.0 + t * (1.0 - sg)))
                dn = dt * lgq
                ns.append(n)
                dns.append(dn)
                s1 = jnp.sum(dn, axis=-1, keepdims=True)
                s2 = jnp.sum(dn * n, axis=-1, keepdims=True)
                m1 = s1 if m1 is None else m1 + s1
                m2 = s2 if m2 is None else m2 + s2
                sacc_ref[0, :, l0:l0 + LANES] = sacc_ref[0, :, l0:l0 + LANES] + _fold8(dt * n)
                sacc_ref[1, :, l0:l0 + LANES] = sacc_ref[1, :, l0:l0 + LANES] + _fold8(dt)
            m1 = m1 * (1.0 / D_B)
            m2 = m2 * (1.0 / D_B)
            for q, l0 in enumerate(range(0, D_B, LANES)):
                du = rstd * (dns[q] - m1 - ns[q] * m2)
                du_ref[rows, l0:l0 + LANES] = du
                sacc_ref[2, :, l0:l0 + LANES] = sacc_ref[2, :, l0:l0 + LANES] + _fold8(du)

        @pl.when(i % 2 == 0)
        def _():
            stage(dy0_ref, dy1_ref)

        @pl.when(i % 2 == 1)
        def _():
            stage(dy1_ref, dy0_ref)

        @pl.when(i == nt)
        def _():
            _reduce_acc(dwa_ref, acc_ref, K_A)
            dlg_ref[...] = jnp.sum(sacc_ref[0], axis=0, keepdims=True)
            dlb_ref[...] = jnp.sum(sacc_ref[1], axis=0, keepdims=True)
            dbb_ref[...] = jnp.sum(sacc_ref[2], axis=0, keepdims=True)

    cur = lambda i: jnp.minimum(i, nt - 1)
    old = lambda i: jnp.maximum(i - 1, 0)
    return _call(
        body, name="bwd_mix_a", grid=(nt + 1,),
        in_specs=[_rows_at(ts, D_MODEL, cur), _const((D_MODEL, D_MODEL)), _rows_at(ts, D_IN, old), _prev_at(ts, D_IN, old),
                  _next_at(ts, D_IN, s, old), _rows_at(ts, D_B, old), _const((K_A, D_A)), _const((1, D_B)), _const((1, D_B))],
        out_specs=[_rows_at(ts, D_A, old), _rows_at(ts, D_B, old), _rows_at(ts, D_A, old), _acc_out((K_A, D_A)),
                   _acc_out((1, D_B)), _acc_out((1, D_B)), _acc_out((1, D_B))],
        out_shape=[_sds((s, D_A), F32), _sds((s, D_B), F32), _sds((s, D_A), BF16), _sds((K_A, D_A), F32),
                   _sds((1, D_B), F32), _sds((1, D_B), F32), _sds((1, D_B), F32)],
        scratch_shapes=[_ext_scratch(ts, D_A), pltpu.VMEM((ts, D_MODEL), BF16), pltpu.VMEM((ts, D_MODEL), BF16),
                        pltpu.VMEM((K_A, SUBLANES, D_A), F32), pltpu.VMEM((3, SUBLANES, D_B), F32)],
        args=(dx2b, w_out_t, z, z, z, u, wa, lg, lb), exchange=exchange)


def _bwd_mix_b(dca, du, z, dab, wa, wb, w_in, x, g1, dx2, ts, exchange=None):
    s = x.shape[0]
    nt = s // ts
    rc = min(ROW_CHUNK, ts)
    n_sub = ts // rc

    def body(dca_ref, dcap_ref, dcan_ref, du_ref, dup_ref, dun_ref, z_ref, zp_ref, zn_ref, dab_ref,
             wa_ref, wb_ref, wi_ref, x_ref, g1_ref, dx2_ref,
             dz_ref, dx_ref, dg1_ref, dwb_ref, dcae_ref, due_ref, ue_ref, acc_ref, dg8_ref,
             dz0_ref, dz1_ref, dh0_ref, dh1_ref):
        i = pl.program_id(0)

        @pl.when(i == 0)
        def _():
            acc_ref[...] = jnp.zeros_like(acc_ref)
            dg8_ref[...] = jnp.zeros_like(dg8_ref)
            dz1_ref[...] = jnp.zeros_like(dz1_ref)
            dh1_ref[...] = jnp.zeros_like(dh1_ref)

        vt = jnp.minimum(i, nt - 1)
        first = vt == 0
        last = vt == nt - 1
        live = (i < nt).astype(F32)
        _fill_halo(dcae_ref, dcap_ref[...], dcan_ref[...], ts, first, last)
        _fill_halo(due_ref, dup_ref[...], dun_ref[...], ts, first, last)
        _, up = _p_u0(zp_ref, slice(None))
        _, un = _p_u0(zn_ref, slice(None))
        _fill_halo(ue_ref, up, un, ts, first, last)

        def fill(j, carry):
            r0 = pl.multiple_of(j * rc, rc)
            rows = pl.ds(r0, rc)
            _put_rows(dcae_ref, r0, rc, dca_ref[rows, :])
            _put_rows(due_ref, r0, rc, du_ref[rows, :])
            b_v = z_ref[rows, 3 * D_A:3 * D_A + D_B].astype(F32)
            b_g = z_ref[rows, 3 * D_A + D_B:D_IN].astype(F32)
            _put_rows(ue_ref, r0, rc, b_v * _sigmoid(b_g))
            return carry

        lax.fori_loop(0, n_sub, fill, 0)

        def stage(dz_new, dz_old, dh_new, dh_old):
            def put(rows, c0, val):
                dz_ref[rows, c0:c0 + LANES] = val
                dz_new[rows, c0:c0 + LANES] = val

            def mixer_a(r0, l0):
                rows = slice(r0, r0 + rc)
                dp = _conv_block(dcae_ref, wa_ref, r0, rc, l0, K_A, True)
                a_h = z_ref[rows, l0:l0 + LANES].astype(F32)
                a_c = z_ref[rows, 2 * D_A + l0:2 * D_A + l0 + LANES].astype(F32)
                put(rows, l0, (dp * a_c).astype(BF16))
                put(rows, D_A + l0, dab_ref[rows, l0:l0 + LANES])
                put(rows, 2 * D_A + l0, (dp * a_h).astype(BF16))

            def mixer_b(r0, l0):
                rows = slice(r0, r0 + rc)
                du0 = _conv_block(due_ref, wb_ref, r0, rc, l0, K_B, True)
                b_v = z_ref[rows, 3 * D_A + l0:3 * D_A + l0 + LANES].astype(F32)
                b_g = z_ref[rows, 3 * D_A + D_B + l0:3 * D_A + D_B + l0 + LANES].astype(F32)
                sg = _sigmoid(b_g)
                put(rows, 3 * D_A + l0, (du0 * sg).astype(BF16))
                put(rows, 3 * D_A + D_B + l0, (du0 * b_v * (sg * (1.0 - sg))).astype(BF16))
                _conv_wgrad_block(acc_ref, du_ref[rows, l0:l0 + LANES], ue_ref, r0, rc, l0, K_B, live)

            def tail(r0):
                rows = slice(r0, r0 + rc)
                dx, dgrow = _rms_bwd_rows(dh_old[rows, :], x_ref[rows, :], g1_ref[...])
                dx_ref[rows, :] = dx2_ref[rows, :] + dx
                dg8_ref[...] = dg8_ref[...] + _fold8(dgrow)

            units = []
            for q in range(n_sub):
                units += [(mixer_a, (q * rc, l0)) for l0 in range(0, D_A, LANES)]
                units += [(mixer_b, (q * rc, l0)) for l0 in range(0, D_B, LANES)]
                units.append((tail, (q * rc,)))
            _interleaved(units, _matmul_pieces([(dz_old, wi_ref)], dh_new, 2))

        @pl.when(i % 2 == 0)
        def _():
            stage(dz0_ref, dz1_ref, dh0_ref, dh1_ref)

        @pl.when(i % 2 == 1)
        def _():
            stage(dz1_ref, dz0_ref, dh1_ref, dh0_ref)

        @pl.when(i == nt + 1)
        def _():
            _reduce_acc(dwb_ref, acc_ref, K_B)
            dg1_ref[...] = jnp.sum(dg8_ref[...], axis=0, keepdims=True)

    vtile = lambda i: jnp.minimum(i, nt - 1)
    ttile = lambda i: jnp.clip(i - 2, 0, nt - 1)
    return _call(
        body, name="bwd_mix_b", grid=(nt + 2,),
        in_specs=[_rows_at(ts, D_A, vtile), _prev_at(ts, D_A, vtile), _next_at(ts, D_A, s, vtile),
                  _rows_at(ts, D_B, vtile), _prev_at(ts, D_B, vtile), _next_at(ts, D_B, s, vtile),
                  _rows_at(ts, D_IN, vtile), _prev_at(ts, D_IN, vtile), _next_at(ts, D_IN, s, vtile), _rows_at(ts, D_A, vtile),
                  _const((K_A, D_A)), _const((K_B, D_B)), _const((D_IN, D_MODEL)), _rows_at(ts, D_MODEL, ttile),
                  _const((1, D_MODEL)), _rows_at(ts, D_MODEL, ttile)],
        out_specs=[_rows_at(ts, D_IN, vtile), _rows_at(ts, D_MODEL, ttile), _acc_out((1, D_MODEL)), _acc_out((K_B, D_B))],
        out_shape=[_sds((s, D_IN), BF16), _sds((s, D_MODEL), F32), _sds((1, D_MODEL), F32), _sds((K_B, D_B), F32)],
        scratch_shapes=[_ext_scratch(ts, D_A), _ext_scratch(ts, D_B), _ext_scratch(ts, D_B),
                        pltpu.VMEM((K_B, SUBLANES, D_B), F32), pltpu.VMEM((SUBLANES, D_MODEL), F32),
                        pltpu.VMEM((ts, D_IN), BF16), pltpu.VMEM((ts, D_IN), BF16),
                        pltpu.VMEM((ts, D_MODEL), F32), pltpu.VMEM((ts, D_MODEL), F32)],
        args=(dca, dca, dca, du, du, du, z, z, z, dab, wa, wb, w_in, x, g1, dx2), exchange=exchange)


def _matmul_tn(a, b, name, exchange=None):
    s, m = a.shape
    n = b.shape[1]
    tk = min(1024, s)
    nk = s // tk
    tm = 256

    def body(a_ref, b_ref, o_ref, acc_ref):
        k = pl.program_id(0)

        @pl.when(k == 0)
        def _():
            acc_ref[...] = jnp.zeros_like(acc_ref)

        for m0 in range(0, m, tm):
            acc_ref[m0:m0 + tm, :] = acc_ref[m0:m0 + tm, :] + lax.dot_general(
                a_ref[:, m0:m0 + tm], b_ref[...], _TN, preferred_element_type=F32)

        @pl.when(k == nk - 1)
        def _():
            o_ref[...] = acc_ref[...].astype(BF16)

    (out,), got = _call(
        body, name=name, grid=(nk,),
        in_specs=[_rows(tk, m), _rows(tk, n)],
        out_specs=[_acc_out((m, n))],
        out_shape=[_sds((m, n), BF16)],
        scratch_shapes=[pltpu.VMEM((m, n), F32)], args=(a, b), exchange=exchange)
    return out if exchange is None else (out, got)


CHIP_RELS = ((1, 0, 0), (0, 1, 0), (1, 1, 0))
CORE_RELS = ((0, 0, 1),)
ALL_RELS = ((0, 0, 1), (0, 1, 0), (0, 1, 1), (1, 0, 0), (1, 0, 1), (1, 1, 0), (1, 1, 1))


def _chip_slot(dev):
    return 2 * dev[0] + dev[1]


def _dev_slot(dev):
    return 4 * dev[0] + 2 * dev[1] + dev[2]


def _me():
    return (lax.axis_index("x"), lax.axis_index("y"), lax.axis_index("c"))


def _peer(me, rel):
    return tuple((1 - me[a]) if rel[a] else me[a] for a in range(3))


_ANY = pl.BlockSpec(memory_space=pl.ANY)


class _Exchange:
    def __init__(self, inputs, out_shape, scratch, start, finish, forward=None):
        self.inputs, self.out_shape, self.scratch = list(inputs), list(out_shape), list(scratch)
        self.start, self.finish, self.forward = start, finish, forward


def _all_gather(payloads):
    n_p = len(payloads)
    n_k = 1 + 2 * len(CHIP_RELS)

    def copy(srcs, dsts, sems, p, k, block_dev, to, from_src):
        blk = dsts[p].at[_dev_slot(block_dev)]
        return pltpu.make_async_remote_copy(
            src_ref=srcs[p] if from_src else blk, dst_ref=blk,
            send_sem=sems[0].at[n_k * p + k], recv_sem=sems[1].at[n_k * p + k], device_id=to, device_id_type=MESH)

    def own_copy(srcs, dsts, sems, p):
        return pltpu.make_async_copy(srcs[p], dsts[p].at[_dev_slot(_me())], sems[2].at[p])

    def start(srcs, dsts, sems):
        me = _me()
        for p in range(n_p):
            own_copy(srcs, dsts, sems, p).start()
        for j, rel in enumerate(CHIP_RELS):
            for p in range(n_p):
                copy(srcs, dsts, sems, p, 1 + j, me, _peer(me, rel), True).start()
        for p in range(n_p):
            copy(srcs, dsts, sems, p, 0, me, _peer(me, CORE_RELS[0]), True).start()

    def forward(srcs, dsts, sems):
        me = _me()
        sibling = _peer(me, CORE_RELS[0])
        for j, rel in enumerate(CHIP_RELS):
            other = _peer(me, rel)
            for p in range(n_p):
                copy(srcs, dsts, sems, p, 1 + j, other, me, False).wait_recv()
                copy(srcs, dsts, sems, p, 4 + j, other, sibling, False).start()

    def finish(srcs, dsts, sems):
        me = _me()
        sibling = _peer(me, CORE_RELS[0])
        for p in range(n_p):
            copy(srcs, dsts, sems, p, 0, sibling, me, False).wait_recv()
        for j, rel in enumerate(CHIP_RELS):
            for p in range(n_p):
                copy(srcs, dsts, sems, p, 4 + j, _peer(sibling, rel), me, False).wait_recv()
        for p in range(n_p):
            own_copy(srcs, dsts, sems, p).wait()
            copy(srcs, dsts, sems, p, 0, me, sibling, True).wait_send()
            for j, rel in enumerate(CHIP_RELS):
                copy(srcs, dsts, sems, p, 1 + j, me, _peer(me, rel), True).wait_send()
                copy(srcs, dsts, sems, p, 4 + j, _peer(me, rel), sibling, False).wait_send()

    return _Exchange(
        payloads, [_sds((N_DEV,) + p.shape, p.dtype) for p in payloads],
        [pltpu.SemaphoreType.DMA((n_p * n_k,)), pltpu.SemaphoreType.DMA((n_p * n_k,)), pltpu.SemaphoreType.DMA((n_p,))],
        start, finish, forward)


def _gather_direct(payload):
    n_r = len(ALL_RELS)

    def copies(srcs, dsts, sems):
        me = _me()
        mine = dsts[0].at[_dev_slot(me)]
        own = pltpu.make_async_copy(srcs[0], mine, sems[2].at[0])
        remote = [pltpu.make_async_remote_copy(src_ref=srcs[0], dst_ref=mine, send_sem=sems[0].at[k], recv_sem=sems[1].at[k],
                                               device_id=_peer(me, rel), device_id_type=MESH)
                  for k, rel in enumerate(ALL_RELS)]
        return [own] + remote

    def start(srcs, dsts, sems):
        for cp in copies(srcs, dsts, sems):
            cp.start()

    def finish(srcs, dsts, sems):
        for cp in copies(srcs, dsts, sems):
            cp.wait()

    return _Exchange([payload], [_sds((N_DEV,) + payload.shape, payload.dtype)],
                     [pltpu.SemaphoreType.DMA((n_r,)), pltpu.SemaphoreType.DMA((n_r,)), pltpu.SemaphoreType.DMA((1,))],
                     start, finish)


def _scatter_exchange(payloads, rels, src_view, view_shapes):
    n_p = len(payloads)
    n_r = len(rels)

    def copies(srcs, dsts, sems):
        me = _me()
        out = []
        for k, rel in enumerate(rels):
            peer = _peer(me, rel)
            for p in range(n_p):
                out.append(pltpu.make_async_remote_copy(
                    src_ref=src_view(srcs[p], peer), dst_ref=dsts[p].at[k],
                    send_sem=sems[0].at[p * n_r + k], recv_sem=sems[1].at[p * n_r + k],
                    device_id=peer, device_id_type=MESH))
        return out

    def start(srcs, dsts, sems):
        for cp in copies(srcs, dsts, sems):
            cp.start()

    def finish(srcs, dsts, sems):
        for cp in copies(srcs, dsts, sems):
            cp.wait()

    return _Exchange(payloads, [_sds((n_r,) + vs, p.dtype) for vs, p in zip(view_shapes, payloads)],
                     [pltpu.SemaphoreType.DMA((n_p * n_r,)), pltpu.SemaphoreType.DMA((n_p * n_r,))], start, finish)


def _split_refs(refs, sizes):
    out, at = [], 0
    for n in sizes:
        out.append(refs[at:at + n])
        at += n
    return out


def _join(exchanges):
    n_in = [len(e.inputs) for e in exchanges]
    n_out = [len(e.out_shape) for e in exchanges]
    n_sc = [len(e.scratch) for e in exchanges]

    def phase(name):
        def run(ins, outs, scs):
            for e, i, o, s in zip(exchanges, _split_refs(ins, n_in), _split_refs(outs, n_out), _split_refs(scs, n_sc)):
                if getattr(e, name) is not None:
                    getattr(e, name)(i, o, s)
        return run

    return _Exchange([a for e in exchanges for a in e.inputs], [s for e in exchanges for s in e.out_shape],
                     [s for e in exchanges for s in e.scratch], phase("start"), phase("finish"),
                     phase("forward") if any(e.forward is not None for e in exchanges) else None)


def _run_exchanges(name, exchanges):
    n_in = [len(e.inputs) for e in exchanges]
    n_out = [len(e.out_shape) for e in exchanges]
    n_sc = [len(e.scratch) for e in exchanges]

    def body(*refs):
        ins, outs, scs = _split_refs(refs, [sum(n_in), sum(n_out), sum(n_sc)])
        parts = list(zip(exchanges, _split_refs(ins, n_in), _split_refs(outs, n_out), _split_refs(scs, n_sc)))
        for e, i, o, s in parts:
            e.start(i, o, s)
        for e, i, o, s in parts:
            if e.forward is not None:
                e.forward(i, o, s)
        for e, i, o, s in parts:
            e.finish(i, o, s)

    outs = pl.pallas_call(
        body, name=name, in_specs=[_ANY] * sum(n_in), out_specs=[_ANY] * sum(n_out),
        out_shape=[sd for e in exchanges for sd in e.out_shape],
        scratch_shapes=[sc for e in exchanges for sc in e.scratch],
    )(*[a for e in exchanges for a in e.inputs])
    return _split_refs(list(outs), n_out)


def _call(body, *, name, grid, in_specs, out_specs, out_shape, scratch_shapes, args, exchange=None, forward_step=None):
    n_in, n_out, n_sc = len(in_specs), len(out_specs), len(scratch_shapes)
    if exchange is None:
        outs = pl.pallas_call(body, name=name, grid=grid, in_specs=in_specs, out_specs=out_specs, out_shape=out_shape,
                              scratch_shapes=scratch_shapes, compiler_params=_params())(*args)
        return list(outs), []
    e = exchange
    sizes = [n_in, len(e.inputs), n_out, len(e.out_shape), n_sc, len(e.scratch)]
    last = grid[0] - 1

    def wrapped(*refs):
        a, ei, o, eo, sc, es = _split_refs(refs, sizes)
        i = pl.program_id(0)

        @pl.when(i == 0)
        def _():
            e.start(ei, eo, es)

        if e.forward is not None:
            @pl.when(i == forward_step)
            def _():
                e.forward(ei, eo, es)

        body(*a, *o, *sc)

        @pl.when(i == last)
        def _():
            e.finish(ei, eo, es)

    outs = pl.pallas_call(
        wrapped, name=name, grid=grid,
        in_specs=list(in_specs) + [_ANY] * len(e.inputs), out_specs=list(out_specs) + [_ANY] * len(e.out_shape),
        out_shape=list(out_shape) + e.out_shape, scratch_shapes=list(scratch_shapes) + e.scratch,
        compiler_params=_params(),
    )(*args, *e.inputs)
    outs = list(outs)
    return outs[:n_out], outs[n_out:]


def _pair_sum(grads, recvd, my_core, name):
    n_p = len(grads)

    def body(c_ref, *refs):
        del c_ref
        for p in range(n_p):
            refs[2 * n_p + p][...] = (refs[p][...].astype(F32) + refs[n_p + p][...].astype(F32)).astype(BF16)

    def blk(g):
        return (None, None) + g.shape[2:]

    return pl.pallas_call(
        body, name=name,
        grid_spec=pltpu.PrefetchScalarGridSpec(
            num_scalar_prefetch=1, grid=(N_CHIP,),
            in_specs=[pl.BlockSpec(blk(g), lambda j, c: (j, c[0], 0, 0)) for g in grads]
            + [pl.BlockSpec(blk(g), lambda j, c: (0, j, 0, 0)) for g in grads],
            out_specs=[pl.BlockSpec((None,) + g.shape[2:], lambda j, c: (j, 0, 0)) for g in grads]),
        out_shape=[_sds((N_CHIP,) + g.shape[2:], BF16) for g in grads],
        compiler_params=pltpu.CompilerParams(dimension_semantics=("arbitrary",), vmem_limit_bytes=VMEM_LIMIT),
    )(my_core, *grads, *recvd)


def _chip_sum(psums, recvd, my_chip, name):
    n_p = len(psums)

    def body(c_ref, *refs):
        del c_ref
        for p in range(n_p):
            acc = refs[p][...].astype(F32)
            for k in range(len(CHIP_RELS)):
                acc = acc + refs[n_p + p][k].astype(F32)
            refs[2 * n_p + p][...] = acc

    return pl.pallas_call(
        body, name=name,
        grid_spec=pltpu.PrefetchScalarGridSpec(
            num_scalar_prefetch=1, grid=(1,),
            in_specs=[pl.BlockSpec((None,) + g.shape[1:], lambda i, c: (c[0], 0, 0)) for g in psums]
            + [pl.BlockSpec(r.shape, lambda i, c: (0, 0, 0)) for r in recvd],
            out_specs=[pl.BlockSpec(g.shape[1:], lambda i, c: (0, 0)) for g in psums]),
        out_shape=[_sds(g.shape[1:], F32) for g in psums],
        compiler_params=pltpu.CompilerParams(dimension_semantics=("arbitrary",), vmem_limit_bytes=VMEM_LIMIT),
    )(my_chip, *psums, *recvd)


def _sum_devices(parts, shapes):
    plan = _small_rows(shapes)

    def body(p_ref, *refs):
        outs, tot_ref = refs[:-1], refs[-1]
        acc = p_ref[0]
        for j in range(1, N_DEV):
            acc = acc + p_ref[j]
        tot_ref[...] = acc
        for idx, r, c0, width, at in plan:
            outs[idx][r:r + 1, c0:c0 + width] = tot_ref[at:at + 1, 0:width]

    return pl.pallas_call(body, name="small_grad_sum", out_shape=[_sds(s, F32) for s in shapes],
                          scratch_shapes=[pltpu.VMEM(parts.shape[1:], F32)])(parts)


def _transposed(w):
    r, c = w.shape
    tr = MXU_COLS

    def body(w_ref, o_ref):
        o_ref[...] = w_ref[...].T

    return pl.pallas_call(
        body, name="transpose_w_down", grid=(r // tr,),
        in_specs=[pl.BlockSpec((tr, c), lambda i: (i, 0))], out_specs=pl.BlockSpec((c, tr), lambda i: (0, i)),
        out_shape=_sds((c, r), w.dtype), compiler_params=_params())(w)


def _cast_shards(shards):
    def body(*refs):
        for src, dst in zip(refs[:len(shards)], refs[len(shards):]):
            dst[...] = src[...].astype(BF16)

    return pl.pallas_call(body, name="cast_shards", out_shape=[_sds(a.shape, BF16) for a in shards],
                          compiler_params=pltpu.CompilerParams(vmem_limit_bytes=VMEM_LIMIT))(*shards)


def _adamw(ws, gs, ms, vs, name):
    n_t = len(ws)

    def body(*refs):
        w_refs, g_refs, m_refs, v_refs = (refs[j * n_t:(j + 1) * n_t] for j in range(4))
        outs = refs[4 * n_t:]
        for k in range(n_t):
            gg = g_refs[k][...]
            mn = ADAM_B1 * m_refs[k][...] + (1.0 - ADAM_B1) * gg
            vn = ADAM_B2 * v_refs[k][...] + (1.0 - ADAM_B2) * (gg * gg)
            m_hat = mn / (1.0 - ADAM_B1 ** ADAM_STEP)
            v_hat = vn / (1.0 - ADAM_B2 ** ADAM_STEP)
            outs[3 * k][...] = -ADAM_LR * (m_hat / (jnp.sqrt(v_hat) + ADAM_EPS) + ADAM_WD * w_refs[k][...])
            outs[3 * k + 1][...] = mn
            outs[3 * k + 2][...] = vn

    out_shape = [_sds(w.shape, F32) for w in ws for _ in range(3)]
    return pl.pallas_call(body, name=name, out_shape=out_shape,
                          compiler_params=pltpu.CompilerParams(vmem_limit_bytes=VMEM_LIMIT))(*ws, *gs, *ms, *vs)


class _Mesh:
    def __init__(self, shards, my_chip, my_core):
        self.shards, self.my_chip, self.my_core = shards, my_chip.reshape(1), my_core.reshape(1)

    def gather(self, names):
        return _all_gather([self.shards[n] for n in names])

    @staticmethod
    def whole(gathered):
        return gathered.reshape(N_DEV * gathered.shape[1], gathered.shape[2])

    @staticmethod
    def by_device(grads):
        return [g.reshape(N_CHIP, 2, g.shape[0] // N_DEV, g.shape[1]) for g in grads]

    @staticmethod
    def to_sibling(parts):
        return _scatter_exchange(parts, CORE_RELS, lambda ref, peer: ref.at[:, peer[2]],
                                 [(N_CHIP,) + p.shape[2:] for p in parts])

    @staticmethod
    def to_chips(pair):
        return _scatter_exchange(pair, CHIP_RELS, lambda ref, peer: ref.at[_chip_slot(peer)], [p.shape[1:] for p in pair])


def _step(x, target, g1, w_in_t, wa, wb, bb, lg, lb, w_out, g2, w_gate_t, w_up_t, wf, w_down, g3, ts, mesh=None):
    (z, h1), got = _fwd_in(x, g1, w_in_t, ts, exchange=mesh and mesh.gather(["w_out", "w_down"]))
    if mesh:
        w_out, w_down = [mesh.whole(g) for g in got]
    (x2, y, u), got = _fwd_mix(z, x, wa, wb, bb, lg, lb, w_out, ts, exchange=mesh and mesh.gather(["w_gate", "w_up"]))
    if mesh:
        w_gate_t, w_up_t = [mesh.whole(g) for g in got]
    (g0, v, h2, w_down_t), _ = _fwd_ffn_in(x2, g2, w_gate_t, w_up_t, ts, w_down=w_down)
    tk = min(ts, SKEW_TILE)
    a, dx3, dx3b, loss, dg3, dgc, dv, dwf = _ffn_out_and_back(g0, v, x2, wf, w_down, w_down_t, g3, target, tk)
    one = dict(w_down=_matmul_tn(a, dx3b, "wgrad_down"), w_up=_matmul_tn(dv, h2, "wgrad_up"))
    parts1 = mesh and mesh.by_device(list(one.values()))
    (dg0, dx2, dx2b, dg2), got = _bwd_ffn_b(dgc, dv, wf, w_gate_t, w_up_t, x2, g2, dx3, tk,
                                            exchange=mesh and mesh.to_sibling(parts1))
    pair1 = mesh and _pair_sum(parts1, got, mesh.my_core, "rs_pair_sum_1")
    two = dict(w_gate=_matmul_tn(dg0, h2, "wgrad_gate"), w_out=_matmul_tn(y, dx2b, "wgrad_out"))
    parts2 = mesh and mesh.by_device(list(two.values()))
    (dca, du, dab, dwa, dlg, dlb, dbb), got = _bwd_mix_a(
        dx2b, w_out, z, u, wa, lg, lb, tk,
        exchange=mesh and _join([mesh.to_chips(pair1), mesh.to_sibling(parts2)]))
    (dz, dx, dg1, dwb), _ = _bwd_mix_b(dca, du, z, dab, wa, wb, w_in_t, x, g1, dx2, tk)
    small = dict(norm_mix_g=dg1, conv_a_w=dwa, conv_b_w=dwb, conv_b_b=dbb, ln_b_g=dlg, ln_b_b=dlb,
                 norm_ffn_g=dg2, conv_ffn_w=dwf, norm_final_g=dg3)
    if not mesh:
        return loss, dx, dict(w_in=_matmul_tn(dz, h1, "wgrad_in"), **one, **two), small
    big = dict(zip(one, _chip_sum(pair1, got[:2], mesh.my_chip, "rs_chip_sum_1")))
    pair2 = _pair_sum(parts2, got[2:], mesh.my_core, "rs_pair_sum_2")
    dw_in_t, got = _matmul_tn(dz, h1, "wgrad_in",
                              exchange=_join([mesh.to_chips(pair2), _gather_direct(_pack_small_grads(small, loss))]))
    big.update(zip(two, _chip_sum(pair2, got[:2], mesh.my_chip, "rs_chip_sum_2")))
    every = got[2]
    parts = mesh.by_device([dw_in_t])
    (got,) = _run_exchanges("rs_cores_last", [mesh.to_sibling(parts)])
    pair = _pair_sum(parts, got, mesh.my_core, "rs_pair_sum_last")
    (got,) = _run_exchanges("rs_chips_last", [mesh.to_chips(pair)])
    (big["w_in"],) = _chip_sum(pair, got, mesh.my_chip, "rs_chip_sum_last")
    shapes = [loss.shape if n == "loss" else small[n].shape for n in _SMALL_NAMES]
    return None, dx, big, dict(zip(_SMALL_NAMES, _sum_devices(every, shapes)))


def _pack_small_weights(conv_a_s, conv_b_s, conv_ffn_s):
    def body(a_ref, b_ref, f_ref, out):
        out[...] = jnp.zeros_like(out)
        at = 0
        for src in (a_ref, b_ref, f_ref):
            rows, cols = src.shape
            for r in range(rows):
                out[at:at + 1, 0:cols] = src[r:r + 1, :]
                at += 1

    return pl.pallas_call(body, name="small_weight_pack", out_shape=_sds((SMALL_W_ROWS, SMALL_W_COLS), F32))(
        conv_a_s, conv_b_s, conv_ffn_s)


def _unpack_small_weights(full):
    def take(r0, k, w):
        return jnp.transpose(full[:, r0:r0 + k, 0:w], (1, 0, 2)).reshape(k, N_DEV * w)

    return take(0, K_A, CONV_A_COLS), take(K_A, K_B, CONV_A_COLS), take(K_A + K_B, K_F, W_FF_COLS)


_SMALL_NAMES = ("conv_b_w", "conv_a_w", "conv_ffn_w", "norm_mix_g", "norm_ffn_g", "norm_final_g",
                "conv_b_b", "ln_b_g", "ln_b_b", "loss")
SMALL_G_ROWS = 64


def _small_rows(shapes):
    plan, at = [], 0
    for idx, (rows, cols) in enumerate(shapes):
        for r in range(rows):
            for c0 in range(0, cols, SMALL_G_COLS):
                plan.append((idx, r, c0, min(SMALL_G_COLS, cols - c0), at))
                at += 1
    assert at <= SMALL_G_ROWS
    return plan


def _pack_small_grads(small, loss):
    srcs = [loss if n == "loss" else small[n] for n in _SMALL_NAMES]
    plan = _small_rows([a.shape for a in srcs])

    def body(*refs):
        out = refs[-1]
        out[...] = jnp.zeros_like(out)
        for idx, r, c0, width, at in plan:
            out[at:at + 1, 0:width] = refs[idx][r:r + 1, c0:c0 + width]

    return pl.pallas_call(body, name="small_grad_pack", out_shape=_sds((SMALL_G_ROWS, SMALL_G_COLS), F32))(*srcs)


def kernel(x, norm_mix_g, w_in, conv_a_w, conv_b_w, conv_b_b, ln_b_g, ln_b_b, w_out, norm_ffn_g, w_gate, w_up, conv_ffn_w, w_down, norm_final_g, loss_target, m_norm_mix_g, m_w_in, m_conv_a_w, m_conv_b_w, m_conv_b_b, m_ln_b_g, m_ln_b_b, m_w_out, m_norm_ffn_g, m_w_gate, m_w_up, m_conv_ffn_w, m_w_down, m_norm_final_g, v_norm_mix_g, v_w_in, v_conv_a_w, v_conv_b_w, v_conv_b_b, v_ln_b_g, v_ln_b_b, v_w_out, v_norm_ffn_g, v_w_gate, v_w_up, v_conv_ffn_w, v_w_down, v_norm_final_g):
    ix, iy, ic = lax.axis_index("x"), lax.axis_index("y"), lax.axis_index("c")
    my_chip = (2 * ix + iy).astype(jnp.int32)
    my_core = ic.astype(jnp.int32)
    my_dev = 2 * my_chip + my_core

    weights = dict(norm_mix_g=norm_mix_g, w_in=w_in, conv_a_w=conv_a_w, conv_b_w=conv_b_w, conv_b_b=conv_b_b,
                   ln_b_g=ln_b_g, ln_b_b=ln_b_b, w_out=w_out, norm_ffn_g=norm_ffn_g, w_gate=w_gate, w_up=w_up,
                   conv_ffn_w=conv_ffn_w, w_down=w_down, norm_final_g=norm_final_g)
    m_in = dict(norm_mix_g=m_norm_mix_g, w_in=m_w_in, conv_a_w=m_conv_a_w, conv_b_w=m_conv_b_w, conv_b_b=m_conv_b_b,
                ln_b_g=m_ln_b_g, ln_b_b=m_ln_b_b, w_out=m_w_out, norm_ffn_g=m_norm_ffn_g, w_gate=m_w_gate,
                w_up=m_w_up, conv_ffn_w=m_conv_ffn_w, w_down=m_w_down, norm_final_g=m_norm_final_g)
    v_in = dict(norm_mix_g=v_norm_mix_g, w_in=v_w_in, conv_a_w=v_conv_a_w, conv_b_w=v_conv_b_w, conv_b_b=v_conv_b_b,
                ln_b_g=v_ln_b_g, ln_b_b=v_ln_b_b, w_out=v_w_out, norm_ffn_g=v_norm_ffn_g, w_gate=v_w_gate,
                w_up=v_w_up, conv_ffn_w=v_conv_ffn_w, w_down=v_w_down, norm_final_g=v_norm_final_g)
    order = list(weights)
    big_names = ("w_in", "w_gate", "w_up", "w_out", "w_down")
    transposed = ("w_in", "w_gate", "w_up")

    def shard2d(name, a):
        if name in transposed:
            return jnp.swapaxes(a[0], 0, 1)
        return a.reshape(1, a.shape[0]) if a.ndim == 1 else a.reshape(a.shape[-2:])

    def unshard2d(name, a2, like):
        if name in transposed:
            return jnp.swapaxes(a2, 0, 1)[None]
        return a2.reshape(like.shape)

    mesh = _Mesh(dict(zip(big_names, _cast_shards([shard2d(n, weights[n]) for n in big_names]))), my_chip, my_core)
    gathered, = _run_exchanges("ag_first", [_all_gather(
        [mesh.shards["w_in"], _pack_small_weights(conv_a_w[0], conv_b_w[0], conv_ffn_w[0])])])
    w_in_t = mesh.whole(gathered[0])
    wa_f, wb_f, wf_f = _unpack_small_weights(gathered[1])

    _, dx, gsum, stot = _step(
        x[0], loss_target[0], norm_mix_g, w_in_t, wa_f, wb_f, conv_b_b, ln_b_g, ln_b_b, None, norm_ffn_g,
        None, None, wf_f, None, norm_final_g.reshape(1, D_MODEL), SEQ_TILE, mesh)

    grads2d = dict(
        norm_mix_g=stot["norm_mix_g"],
        conv_a_w=lax.dynamic_slice(stot["conv_a_w"], (0, my_dev * CONV_A_COLS), (K_A, CONV_A_COLS)),
        conv_b_w=lax.dynamic_slice(stot["conv_b_w"], (0, my_dev * CONV_A_COLS), (K_B, CONV_A_COLS)),
        conv_b_b=stot["conv_b_b"], ln_b_g=stot["ln_b_g"], ln_b_b=stot["ln_b_b"],
        norm_ffn_g=stot["norm_ffn_g"],
        conv_ffn_w=lax.dynamic_slice(stot["conv_ffn_w"], (0, my_dev * W_FF_COLS), (K_F, W_FF_COLS)),
        norm_final_g=stot["norm_final_g"],
        **gsum,
    )

    updates = {}
    small_names = [n for n in order if n not in big_names]
    for group, label in [([n], "adamw_" + n) for n in big_names] + [(small_names, "adamw_small")]:
        outs = _adamw([shard2d(n, weights[n]) for n in group], [grads2d[n] for n in group],
                      [shard2d(n, m_in[n]) for n in group], [shard2d(n, v_in[n]) for n in group], label)
        for k, n in enumerate(group):
            updates[n] = outs[3 * k:3 * k + 3]
    g_out = [unshard2d(n, grads2d[n], weights[n]) for n in order]
    d_out, m_out, v_out = [[unshard2d(n, updates[n][j], weights[n]) for n in order] for j in range(3)]

    return (stot["loss"][0, 0], dx[None], *g_out, *d_out, *m_out, *v_out)
```
